```python
import math
import jax, jax.numpy as jnp
from jax import lax
import numpy as np

D_MODEL = 1024
BATCH = 8
SEQ = 8192
DEPTH = 4

HEAD_DIM = 64
MIX_WIDTH = D_MODEL
A_WIDTH = MIX_WIDTH // 4
A_HEADS = A_WIDTH // HEAD_DIM
CHUNK = 128
B_WIDTH = MIX_WIDTH // 2
B_Q_HEADS = B_WIDTH // HEAD_DIM
B_KV_HEADS = 2
B_GROUP = B_Q_HEADS // B_KV_HEADS
WINDOW = 128
ROPE_THETA = 10000.0
C_WIDTH = MIX_WIDTH // 4
C_GROUP = 16
C_GROUPS = C_WIDTH // C_GROUP
C_STATE = 64
DT_MIN = 0.001
DT_MAX = 0.1
IN_A = 2 * A_WIDTH
IN_Q = B_WIDTH
IN_KV = B_KV_HEADS * HEAD_DIM
IN_C = C_WIDTH
IN_COLS = IN_A + IN_Q + 2 * IN_KV + IN_C
D_FF = 4 * D_MODEL
PLE_DIM = 256
EPS = 1e-6

kernel_name = "hybrid_gmlp_swa_s5_trunk"


def rmsnorm(x, g):
    xf = x.astype(jnp.float32)
    y = xf * lax.rsqrt(jnp.mean(xf * xf, axis=-1, keepdims=True) + EPS)
    return (y * g.astype(jnp.float32)).astype(x.dtype)


def layernorm(x, g, b):
    xf = x.astype(jnp.float32)
    mu = jnp.mean(xf, axis=-1, keepdims=True)
    xc = xf - mu
    y = xc * lax.rsqrt(jnp.mean(xc * xc, axis=-1, keepdims=True) + EPS)
    return (y * g.astype(jnp.float32) + b.astype(jnp.float32)).astype(x.dtype)


def rope_tables(positions):
    inv = 1.0 / (ROPE_THETA ** (jnp.arange(0, HEAD_DIM, 2, dtype=jnp.float32) / HEAD_DIM))
    ang = positions.astype(jnp.float32)[..., None] * inv
    return jnp.cos(ang), jnp.sin(ang)


def apply_rope(x, cos, sin):
    xf = x.astype(jnp.float32)
    x1, x2 = jnp.split(xf, 2, axis=-1)
    c = cos[:, :, None, :]
    s = sin[:, :, None, :]
    return jnp.concatenate([x1 * c - x2 * s, x2 * c + x1 * s], axis=-1).astype(x.dtype)


def chunk_gmlp(z, ln_g, ln_b, w_s, b_s):
    bsz, L, _ = z.shape
    z = jax.nn.gelu(z).reshape(bsz, L // CHUNK, CHUNK, A_HEADS, 2 * HEAD_DIM)
    u, v = jnp.split(z, 2, axis=-1)
    v = layernorm(v, ln_g, ln_b)
    causal = jnp.tril(jnp.ones((CHUNK, CHUNK), dtype=bool))
    w = jnp.where(causal, w_s, 0.0).astype(v.dtype)
    sv = jnp.einsum('hts,bnshd->bnthd', w, v) + b_s.T[None, None, :, :, None].astype(v.dtype)
    return (u * sv).reshape(bsz, L, A_WIDTH)


def swa_sink_attention(q, k, v, sinks):
    bsz, L = q.shape[:2]
    nb = L // WINDOW
    qb = q.reshape(bsz, nb, WINDOW, B_KV_HEADS, B_GROUP, HEAD_DIM)

    def band(t):
        t = t.reshape(bsz, nb, WINDOW, B_KV_HEADS, HEAD_DIM)
        prev = jnp.pad(t[:, :-1], ((0, 0), (1, 0), (0, 0), (0, 0), (0, 0)))
        return jnp.concatenate([prev, t], axis=2)

    kb, vb = band(k), band(v)
    s = jnp.einsum('bnqhgd,bnkhd->bnhgqk', qb, kb,
                   preferred_element_type=jnp.float32) * (HEAD_DIM ** -0.5)
    qi = jnp.arange(WINDOW)[:, None] + WINDOW
    kj = jnp.arange(2 * WINDOW)[None, :]
    diff = qi - kj
    band_ok = (diff >= 0) & (diff < WINDOW)
    not_first = jnp.arange(nb)[:, None, None] > 0
    mask = band_ok[None] & (not_first | (kj >= WINDOW)[None])
    s = jnp.where(mask[None, :, None, None], s, -jnp.inf)
    sink = sinks.astype(jnp.float32).reshape(B_KV_HEADS, B_GROUP)[None, None, :, :, None, None]
    m = jnp.maximum(jnp.max(s, axis=-1, keepdims=True), sink)
    pr = jnp.exp(s - m)
    denom = jnp.sum(pr, axis=-1, keepdims=True) + jnp.exp(sink - m)
    o = jnp.einsum('bnhgqk,bnkhd->bnqhgd', (pr / denom).astype(v.dtype), vb)
    return o.reshape(bsz, L, B_WIDTH)


def s5_ssm(u, a_re, a_im, log_dt, b_re, b_im, c_re, c_im, d_skip, glu_w1, glu_w2):
    bsz, L, _ = u.shape
    uf = u.astype(jnp.float32).reshape(bsz, L, C_GROUPS, C_GROUP)
    lam = lax.complex(a_re.astype(jnp.float32), a_im.astype(jnp.float32))
    dt = jnp.exp(log_dt.astype(jnp.float32))[:, None]
    lam_bar = jnp.exp(lam * dt)
    bmat = lax.complex(b_re.astype(jnp.float32), b_im.astype(jnp.float32))
    b_bar = ((lam_bar - 1.0) / lam)[..., None] * bmat
    bu = jnp.einsum('gph,blgh->blgp', b_bar, uf.astype(jnp.complex64))
    a_seq = jnp.broadcast_to(lam_bar, bu.shape)

    def combine(e1, e2):
        a1, x1 = e1
        a2, x2 = e2
        return a1 * a2, a2 * x1 + x2

    _, states = lax.associative_scan(combine, (a_seq, bu), axis=1)
    cmat = lax.complex(c_re.astype(jnp.float32), c_im.astype(jnp.float32))
    y = jnp.real(jnp.einsum('ghp,blgp->blgh', cmat, states)) + d_skip.astype(jnp.float32) * uf
    y = jax.nn.gelu(y.reshape(bsz, L, C_WIDTH)).astype(u.dtype)
    return (y @ glu_w1) * jax.nn.sigmoid(y @ glu_w2)


def hybrid_layer(h, p_i, cos, sin, attn_norm_g, w_in, gmlp_ln_g, gmlp_ln_b, gmlp_ws, gmlp_bs,
                 q_norm_g, k_norm_g, sinks, ssm_a_re, ssm_a_im, ssm_log_dt, ssm_b_re, ssm_b_im,
                 ssm_c_re, ssm_c_im, ssm_d, glu_w1, glu_w2, mix_out_g, w_out,
                 mlp_norm_g, w_ff1, w_ff2, ple_norm_g, w_ple_gate, w_ple_proj):
    bsz, L, _ = h.shape
    xn = rmsnorm(h, attn_norm_g)
    z = xn @ w_in
    za, zq, zk, zv, zc = jnp.split(
        z, [IN_A, IN_A + IN_Q, IN_A + IN_Q + IN_KV, IN_A + IN_Q + 2 * IN_KV], axis=-1)
    ya = chunk_gmlp(za, gmlp_ln_g, gmlp_ln_b, gmlp_ws, gmlp_bs)
    q = zq.reshape(bsz, L, B_Q_HEADS, HEAD_DIM)
    k = zk.reshape(bsz, L, B_KV_HEADS, HEAD_DIM)
    v = zv.reshape(bsz, L, B_KV_HEADS, HEAD_DIM)
    q = apply_rope(rmsnorm(q, q_norm_g), cos, sin)
    k = apply_rope(rmsnorm(k, k_norm_g), cos, sin)
    yb = swa_sink_attention(q, k, v, sinks)
    yc = s5_ssm(zc, ssm_a_re, ssm_a_im, ssm_log_dt, ssm_b_re, ssm_b_im,
                ssm_c_re, ssm_c_im, ssm_d, glu_w1, glu_w2)
    y = jnp.concatenate([
        rmsnorm(ya, mix_out_g[:A_WIDTH]),
        rmsnorm(yb, mix_out_g[A_WIDTH:A_WIDTH + B_WIDTH]),
        rmsnorm(yc, mix_out_g[A_WIDTH + B_WIDTH:]),
    ], axis=-1)
    h = h + y @ w_out
    hn = rmsnorm(h, mlp_norm_g)
    h = h + jnp.square(jax.nn.relu(hn @ w_ff1)) @ w_ff2
    gate = jax.nn.sigmoid(rmsnorm(h, ple_norm_g) @ w_ple_gate)
    return h + gate * (p_i @ w_ple_proj)


def _fwd_setup_inputs(seed: int = 0) -> dict:
    key = jax.random.key(seed)
    ks = iter(jax.random.split(key, 40))
    f32 = jnp.float32

    def nrm(shape, scale):
        return jax.random.normal(next(ks), shape, f32) * scale

    def gain(shape):
        return 1.0 + nrm(shape, 0.02)

    x = nrm((BATCH, SEQ, D_MODEL), 1.0)
    p = nrm((DEPTH, BATCH, SEQ, PLE_DIM), 1.0)
    offsets = jax.random.randint(next(ks), (BATCH, 1), 0, 1024, dtype=jnp.int32)
    positions = offsets + jnp.arange(SEQ, dtype=jnp.int32)[None, :]

    n_idx = jnp.arange(C_STATE, dtype=f32)
    ssm_a_re = -0.5 + nrm((DEPTH, C_GROUPS, C_STATE), 0.01)
    ssm_a_im = math.pi * n_idx[None, None, :] + nrm((DEPTH, C_GROUPS, C_STATE), 0.01)
    ssm_log_dt = jax.random.uniform(next(ks), (DEPTH, C_GROUPS), f32,
                                    math.log(DT_MIN), math.log(DT_MAX))
    b_scale = (2.0 * C_GROUP) ** -0.5
    c_scale = (2.0 * C_STATE) ** -0.5

    return {
        "x": x,
        "p": p,
        "positions": positions,
        "attn_norm_g": gain((DEPTH, D_MODEL)),
        "w_in": nrm((DEPTH, D_MODEL, IN_COLS), D_MODEL ** -0.5),
        "gmlp_ln_g": gain((DEPTH, A_HEADS, HEAD_DIM)),
        "gmlp_ln_b": nrm((DEPTH, A_HEADS, HEAD_DIM), 0.02),
        "gmlp_ws": nrm((DEPTH, A_HEADS, CHUNK, CHUNK), 0.5 * CHUNK ** -0.5),
        "gmlp_bs": gain((DEPTH, A_HEADS, CHUNK)),
        "q_norm_g": gain((DEPTH, HEAD_DIM)),
        "k_norm_g": gain((DEPTH, HEAD_DIM)),
        "sinks": nrm((DEPTH, B_Q_HEADS), 0.5),
        "ssm_a_re": ssm_a_re,
        "ssm_a_im": ssm_a_im,
        "ssm_log_dt": ssm_log_dt,
        "ssm_b_re": nrm((DEPTH, C_GROUPS, C_STATE, C_GROUP), b_scale),
        "ssm_b_im": nrm((DEPTH, C_GROUPS, C_STATE, C_GROUP), b_scale),
        "ssm_c_re": nrm((DEPTH, C_GROUPS, C_GROUP, C_STATE), c_scale),
        "ssm_c_im": nrm((DEPTH, C_GROUPS, C_GROUP, C_STATE), c_scale),
        "ssm_d": nrm((DEPTH, C_GROUPS, C_GROUP), 0.5),
        "glu_w1": nrm((DEPTH, C_WIDTH, C_WIDTH), C_WIDTH ** -0.5),
        "glu_w2": nrm((DEPTH, C_WIDTH, C_WIDTH), C_WIDTH ** -0.5),
        "mix_out_g": gain((DEPTH, MIX_WIDTH)),
        "w_out": nrm((DEPTH, MIX_WIDTH, D_MODEL), MIX_WIDTH ** -0.5),
        "mlp_norm_g": gain((DEPTH, D_MODEL)),
        "w_ff1": nrm((DEPTH, D_MODEL, D_FF), D_MODEL ** -0.5),
        "w_ff2": nrm((DEPTH, D_FF, D_MODEL), D_FF ** -0.5),
        "ple_norm_g": gain((DEPTH, D_MODEL)),
        "w_ple_gate": nrm((DEPTH, D_MODEL, D_MODEL), D_MODEL ** -0.5),
        "w_ple_proj": nrm((DEPTH, PLE_DIM, D_MODEL), 0.5 * PLE_DIM ** -0.5),
    }


def _fwd_reference(x, p, positions, attn_norm_g, w_in, gmlp_ln_g, gmlp_ln_b, gmlp_ws, gmlp_bs,
              q_norm_g, k_norm_g, sinks, ssm_a_re, ssm_a_im, ssm_log_dt, ssm_b_re, ssm_b_im,
              ssm_c_re, ssm_c_im, ssm_d, glu_w1, glu_w2, mix_out_g, w_out,
              mlp_norm_g, w_ff1, w_ff2, ple_norm_g, w_ple_gate, w_ple_proj):
    cos, sin = rope_tables(positions)
    h = x
    for i in range(DEPTH):
        h = hybrid_layer(h, p[i], cos, sin, attn_norm_g[i], w_in[i], gmlp_ln_g[i], gmlp_ln_b[i],
                         gmlp_ws[i], gmlp_bs[i], q_norm_g[i], k_norm_g[i], sinks[i],
                         ssm_a_re[i], ssm_a_im[i], ssm_log_dt[i], ssm_b_re[i], ssm_b_im[i],
                         ssm_c_re[i], ssm_c_im[i], ssm_d[i], glu_w1[i], glu_w2[i],
                         mix_out_g[i], w_out[i], mlp_norm_g[i], w_ff1[i], w_ff2[i],
                         ple_norm_g[i], w_ple_gate[i], w_ple_proj[i])
    return h


import jax as _jax
import jax.numpy as _jnp

TWIN_FORMAT = 'train_step'
FWD_PARAMS = ['x', 'p', 'positions', 'attn_norm_g', 'w_in', 'gmlp_ln_g', 'gmlp_ln_b', 'gmlp_ws', 'gmlp_bs', 'q_norm_g', 'k_norm_g', 'sinks', 'ssm_a_re', 'ssm_a_im', 'ssm_log_dt', 'ssm_b_re', 'ssm_b_im', 'ssm_c_re', 'ssm_c_im', 'ssm_d', 'glu_w1', 'glu_w2', 'mix_out_g', 'w_out', 'mlp_norm_g', 'w_ff1', 'w_ff2', 'ple_norm_g', 'w_ple_gate', 'w_ple_proj']
TWIN_WEIGHTS = ['attn_norm_g', 'w_in', 'gmlp_ln_g', 'gmlp_ln_b', 'gmlp_ws', 'gmlp_bs', 'q_norm_g', 'k_norm_g', 'sinks', 'ssm_a_re', 'ssm_a_im', 'ssm_log_dt', 'ssm_b_re', 'ssm_b_im', 'ssm_c_re', 'ssm_c_im', 'ssm_d', 'glu_w1', 'glu_w2', 'mix_out_g', 'w_out', 'mlp_norm_g', 'w_ff1', 'w_ff2', 'ple_norm_g', 'w_ple_gate', 'w_ple_proj']
TWIN_DIFF_INPUT = 'x'
TWIN_INPUTS = ['x', 'p', 'positions', 'attn_norm_g', 'w_in', 'gmlp_ln_g', 'gmlp_ln_b', 'gmlp_ws', 'gmlp_bs', 'q_norm_g', 'k_norm_g', 'sinks', 'ssm_a_re', 'ssm_a_im', 'ssm_log_dt', 'ssm_b_re', 'ssm_b_im', 'ssm_c_re', 'ssm_c_im', 'ssm_d', 'glu_w1', 'glu_w2', 'mix_out_g', 'w_out', 'mlp_norm_g', 'w_ff1', 'w_ff2', 'ple_norm_g', 'w_ple_gate', 'w_ple_proj', 'loss_target', 'm_attn_norm_g', 'm_w_in', 'm_gmlp_ln_g', 'm_gmlp_ln_b', 'm_gmlp_ws', 'm_gmlp_bs', 'm_q_norm_g', 'm_k_norm_g', 'm_sinks', 'm_ssm_a_re', 'm_ssm_a_im', 'm_ssm_log_dt', 'm_ssm_b_re', 'm_ssm_b_im', 'm_ssm_c_re', 'm_ssm_c_im', 'm_ssm_d', 'm_glu_w1', 'm_glu_w2', 'm_mix_out_g', 'm_w_out', 'm_mlp_norm_g', 'm_w_ff1', 'm_w_ff2', 'm_ple_norm_g', 'm_w_ple_gate', 'm_w_ple_proj', 'v_attn_norm_g', 'v_w_in', 'v_gmlp_ln_g', 'v_gmlp_ln_b', 'v_gmlp_ws', 'v_gmlp_bs', 'v_q_norm_g', 'v_k_norm_g', 'v_sinks', 'v_ssm_a_re', 'v_ssm_a_im', 'v_ssm_log_dt', 'v_ssm_b_re', 'v_ssm_b_im', 'v_ssm_c_re', 'v_ssm_c_im', 'v_ssm_d', 'v_glu_w1', 'v_glu_w2', 'v_mix_out_g', 'v_w_out', 'v_mlp_norm_g', 'v_w_ff1', 'v_w_ff2', 'v_ple_norm_g', 'v_w_ple_gate', 'v_w_ple_proj']
TWIN_OUTPUTS = ['loss', 'grad_x', 'grad_attn_norm_g', 'grad_w_in', 'grad_gmlp_ln_g', 'grad_gmlp_ln_b', 'grad_gmlp_ws', 'grad_gmlp_bs', 'grad_q_norm_g', 'grad_k_norm_g', 'grad_sinks', 'grad_ssm_a_re', 'grad_ssm_a_im', 'grad_ssm_log_dt', 'grad_ssm_b_re', 'grad_ssm_b_im', 'grad_ssm_c_re', 'grad_ssm_c_im', 'grad_ssm_d', 'grad_glu_w1', 'grad_glu_w2', 'grad_mix_out_g', 'grad_w_out', 'grad_mlp_norm_g', 'grad_w_ff1', 'grad_w_ff2', 'grad_ple_norm_g', 'grad_w_ple_gate', 'grad_w_ple_proj', 'delta_attn_norm_g', 'delta_w_in', 'delta_gmlp_ln_g', 'delta_gmlp_ln_b', 'delta_gmlp_ws', 'delta_gmlp_bs', 'delta_q_norm_g', 'delta_k_norm_g', 'delta_sinks', 'delta_ssm_a_re', 'delta_ssm_a_im', 'delta_ssm_log_dt', 'delta_ssm_b_re', 'delta_ssm_b_im', 'delta_ssm_c_re', 'delta_ssm_c_im', 'delta_ssm_d', 'delta_glu_w1', 'delta_glu_w2', 'delta_mix_out_g', 'delta_w_out', 'delta_mlp_norm_g', 'delta_w_ff1', 'delta_w_ff2', 'delta_ple_norm_g', 'delta_w_ple_gate', 'delta_w_ple_proj', 'new_m_attn_norm_g', 'new_m_w_in', 'new_m_gmlp_ln_g', 'new_m_gmlp_ln_b', 'new_m_gmlp_ws', 'new_m_gmlp_bs', 'new_m_q_norm_g', 'new_m_k_norm_g', 'new_m_sinks', 'new_m_ssm_a_re', 'new_m_ssm_a_im', 'new_m_ssm_log_dt', 'new_m_ssm_b_re', 'new_m_ssm_b_im', 'new_m_ssm_c_re', 'new_m_ssm_c_im', 'new_m_ssm_d', 'new_m_glu_w1', 'new_m_glu_w2', 'new_m_mix_out_g', 'new_m_w_out', 'new_m_mlp_norm_g', 'new_m_w_ff1', 'new_m_w_ff2', 'new_m_ple_norm_g', 'new_m_w_ple_gate', 'new_m_w_ple_proj', 'new_v_attn_norm_g', 'new_v_w_in', 'new_v_gmlp_ln_g', 'new_v_gmlp_ln_b', 'new_v_gmlp_ws', 'new_v_gmlp_bs', 'new_v_q_norm_g', 'new_v_k_norm_g', 'new_v_sinks', 'new_v_ssm_a_re', 'new_v_ssm_a_im', 'new_v_ssm_log_dt', 'new_v_ssm_b_re', 'new_v_ssm_b_im', 'new_v_ssm_c_re', 'new_v_ssm_c_im', 'new_v_ssm_d', 'new_v_glu_w1', 'new_v_glu_w2', 'new_v_mix_out_g', 'new_v_w_out', 'new_v_mlp_norm_g', 'new_v_w_ff1', 'new_v_w_ff2', 'new_v_ple_norm_g', 'new_v_w_ple_gate', 'new_v_w_ple_proj']
TWIN_LEAF_KINDS = {'loss': 'loss', 'grad_x': 'grad_x', 'grad_attn_norm_g': 'grad_w', 'grad_w_in': 'grad_w', 'grad_gmlp_ln_g': 'grad_w', 'grad_gmlp_ln_b': 'grad_w', 'grad_gmlp_ws': 'grad_w', 'grad_gmlp_bs': 'grad_w', 'grad_q_norm_g': 'grad_w', 'grad_k_norm_g': 'grad_w', 'grad_sinks': 'grad_w', 'grad_ssm_a_re': 'grad_w', 'grad_ssm_a_im': 'grad_w', 'grad_ssm_log_dt': 'grad_w', 'grad_ssm_b_re': 'grad_w', 'grad_ssm_b_im': 'grad_w', 'grad_ssm_c_re': 'grad_w', 'grad_ssm_c_im': 'grad_w', 'grad_ssm_d': 'grad_w', 'grad_glu_w1': 'grad_w', 'grad_glu_w2': 'grad_w', 'grad_mix_out_g': 'grad_w', 'grad_w_out': 'grad_w', 'grad_mlp_norm_g': 'grad_w', 'grad_w_ff1': 'grad_w', 'grad_w_ff2': 'grad_w', 'grad_ple_norm_g': 'grad_w', 'grad_w_ple_gate': 'grad_w', 'grad_w_ple_proj': 'grad_w', 'delta_attn_norm_g': 'delta_w', 'delta_w_in': 'delta_w', 'delta_gmlp_ln_g': 'delta_w', 'delta_gmlp_ln_b': 'delta_w', 'delta_gmlp_ws': 'delta_w', 'delta_gmlp_bs': 'delta_w', 'delta_q_norm_g': 'delta_w', 'delta_k_norm_g': 'delta_w', 'delta_sinks': 'delta_w', 'delta_ssm_a_re': 'delta_w', 'delta_ssm_a_im': 'delta_w', 'delta_ssm_log_dt': 'delta_w', 'delta_ssm_b_re': 'delta_w', 'delta_ssm_b_im': 'delta_w', 'delta_ssm_c_re': 'delta_w', 'delta_ssm_c_im': 'delta_w', 'delta_ssm_d': 'delta_w', 'delta_glu_w1': 'delta_w', 'delta_glu_w2': 'delta_w', 'delta_mix_out_g': 'delta_w', 'delta_w_out': 'delta_w', 'delta_mlp_norm_g': 'delta_w', 'delta_w_ff1': 'delta_w', 'delta_w_ff2': 'delta_w', 'delta_ple_norm_g': 'delta_w', 'delta_w_ple_gate': 'delta_w', 'delta_w_ple_proj': 'delta_w', 'new_m_attn_norm_g': 'new_m', 'new_m_w_in': 'new_m', 'new_m_gmlp_ln_g': 'new_m', 'new_m_gmlp_ln_b': 'new_m', 'new_m_gmlp_ws': 'new_m', 'new_m_gmlp_bs': 'new_m', 'new_m_q_norm_g': 'new_m', 'new_m_k_norm_g': 'new_m', 'new_m_sinks': 'new_m', 'new_m_ssm_a_re': 'new_m', 'new_m_ssm_a_im': 'new_m', 'new_m_ssm_log_dt': 'new_m', 'new_m_ssm_b_re': 'new_m', 'new_m_ssm_b_im': 'new_m', 'new_m_ssm_c_re': 'new_m', 'new_m_ssm_c_im': 'new_m', 'new_m_ssm_d': 'new_m', 'new_m_glu_w1': 'new_m', 'new_m_glu_w2': 'new_m', 'new_m_mix_out_g': 'new_m', 'new_m_w_out': 'new_m', 'new_m_mlp_norm_g': 'new_m', 'new_m_w_ff1': 'new_m', 'new_m_w_ff2': 'new_m', 'new_m_ple_norm_g': 'new_m', 'new_m_w_ple_gate': 'new_m', 'new_m_w_ple_proj': 'new_m', 'new_v_attn_norm_g': 'new_v', 'new_v_w_in': 'new_v', 'new_v_gmlp_ln_g': 'new_v', 'new_v_gmlp_ln_b': 'new_v', 'new_v_gmlp_ws': 'new_v', 'new_v_gmlp_bs': 'new_v', 'new_v_q_norm_g': 'new_v', 'new_v_k_norm_g': 'new_v', 'new_v_sinks': 'new_v', 'new_v_ssm_a_re': 'new_v', 'new_v_ssm_a_im': 'new_v', 'new_v_ssm_log_dt': 'new_v', 'new_v_ssm_b_re': 'new_v', 'new_v_ssm_b_im': 'new_v', 'new_v_ssm_c_re': 'new_v', 'new_v_ssm_c_im': 'new_v', 'new_v_ssm_d': 'new_v', 'new_v_glu_w1': 'new_v', 'new_v_glu_w2': 'new_v', 'new_v_mix_out_g': 'new_v', 'new_v_w_out': 'new_v', 'new_v_mlp_norm_g': 'new_v', 'new_v_w_ff1': 'new_v', 'new_v_w_ff2': 'new_v', 'new_v_ple_norm_g': 'new_v', 'new_v_w_ple_gate': 'new_v', 'new_v_w_ple_proj': 'new_v'}


def _forward(args):
    return _fwd_reference(*[args[k] for k in FWD_PARAMS])


def _output_shape():
    def fwd():
        inp = _fwd_setup_inputs(0)
        return _fwd_reference(*[inp[k] for k in FWD_PARAMS])
    out = _jax.eval_shape(fwd)
    return out.shape, out.dtype

N_MICROBATCH = 1
ADAM_LR = 0.001
ADAM_B1 = 0.9
ADAM_B2 = 0.999
ADAM_EPS = 1e-08
ADAM_WD = 0.01
ADAM_STEP = 10
PER_EXAMPLE_BATCH_AXIS = {'x': 0, 'p': 1, 'positions': 0, 'loss_target': 0}
SHARED_INPUTS = []
_WEIGHT_DTYPES = {'attn_norm_g': _jnp.float32, 'w_in': _jnp.float32, 'gmlp_ln_g': _jnp.float32, 'gmlp_ln_b': _jnp.float32, 'gmlp_ws': _jnp.float32, 'gmlp_bs': _jnp.float32, 'q_norm_g': _jnp.float32, 'k_norm_g': _jnp.float32, 'sinks': _jnp.float32, 'ssm_a_re': _jnp.float32, 'ssm_a_im': _jnp.float32, 'ssm_log_dt': _jnp.float32, 'ssm_b_re': _jnp.float32, 'ssm_b_im': _jnp.float32, 'ssm_c_re': _jnp.float32, 'ssm_c_im': _jnp.float32, 'ssm_d': _jnp.float32, 'glu_w1': _jnp.float32, 'glu_w2': _jnp.float32, 'mix_out_g': _jnp.float32, 'w_out': _jnp.float32, 'mlp_norm_g': _jnp.float32, 'w_ff1': _jnp.float32, 'w_ff2': _jnp.float32, 'ple_norm_g': _jnp.float32, 'w_ple_gate': _jnp.float32, 'w_ple_proj': _jnp.float32}
MOMENT_SCALE = {'attn_norm_g': 8.190185e+01, 'w_in': 6.783269e+01, 'gmlp_ln_g': 1.154897e+00, 'gmlp_ln_b': 1.782336e+00, 'gmlp_ws': 9.864510e-01, 'gmlp_bs': 2.471554e+00, 'q_norm_g': 5.377151e+00, 'k_norm_g': 6.148571e+00, 'sinks': 3.370741e+00, 'ssm_a_re': 7.157962e+00, 'ssm_a_im': 8.805936e+00, 'ssm_log_dt': 1.404012e+02, 'ssm_b_re': 4.441577e+00, 'ssm_b_im': 3.658686e+00, 'ssm_c_re': 1.163780e+01, 'ssm_c_im': 1.450168e+01, 'ssm_d': 1.861159e+02, 'glu_w1': 8.519541e+01, 'glu_w2': 7.469099e+00, 'mix_out_g': 1.223176e+02, 'w_out': 9.573022e+01, 'mlp_norm_g': 2.044778e+02, 'w_ff1': 3.230491e+01, 'w_ff2': 9.914994e+01, 'ple_norm_g': 7.253847e-01, 'w_ple_gate': 5.644958e-01, 'w_ple_proj': 8.515460e-01}


def _to_microbatches(a, axis):
    t = _jnp.moveaxis(a, axis, 0)
    t = t.reshape((N_MICROBATCH, t.shape[0] // N_MICROBATCH) + t.shape[1:])
    return _jnp.moveaxis(t, 1, axis + 1)


def setup_inputs(seed: int = 0) -> dict:
    inp = _fwd_setup_inputs(seed)
    key = _jax.random.fold_in(_jax.random.key(seed), 7919)
    shape, _ = _output_shape()
    out = dict(inp)
    out["loss_target"] = _jax.random.normal(_jax.random.fold_in(key, 0), shape, _jnp.float32)
    for i, name in enumerate(TWIN_WEIGHTS):
        w = inp[name].astype(_jnp.float32)
        if MOMENT_SCALE is None:
            s = _jnp.sqrt(_jnp.mean(_jnp.square(w)) + 1e-30)
        else:
            s = MOMENT_SCALE[name]
        km, kv = _jax.random.split(_jax.random.fold_in(key, i + 1))
        out[name] = w
        out["m_" + name] = s * _jax.random.normal(km, w.shape, _jnp.float32)
        out["v_" + name] = (s * s) * _jax.random.uniform(kv, w.shape, _jnp.float32, 0.5, 1.5)
    if N_MICROBATCH > 1:
        for name, axis in PER_EXAMPLE_BATCH_AXIS.items():
            out[name] = _to_microbatches(out[name], axis)
    return {'x': out['x'], 'p': out['p'], 'positions': out['positions'], 'attn_norm_g': out['attn_norm_g'], 'w_in': out['w_in'], 'gmlp_ln_g': out['gmlp_ln_g'], 'gmlp_ln_b': out['gmlp_ln_b'], 'gmlp_ws': out['gmlp_ws'], 'gmlp_bs': out['gmlp_bs'], 'q_norm_g': out['q_norm_g'], 'k_norm_g': out['k_norm_g'], 'sinks': out['sinks'], 'ssm_a_re': out['ssm_a_re'], 'ssm_a_im': out['ssm_a_im'], 'ssm_log_dt': out['ssm_log_dt'], 'ssm_b_re': out['ssm_b_re'], 'ssm_b_im': out['ssm_b_im'], 'ssm_c_re': out['ssm_c_re'], 'ssm_c_im': out['ssm_c_im'], 'ssm_d': out['ssm_d'], 'glu_w1': out['glu_w1'], 'glu_w2': out['glu_w2'], 'mix_out_g': out['mix_out_g'], 'w_out': out['w_out'], 'mlp_norm_g': out['mlp_norm_g'], 'w_ff1': out['w_ff1'], 'w_ff2': out['w_ff2'], 'ple_norm_g': out['ple_norm_g'], 'w_ple_gate': out['w_ple_gate'], 'w_ple_proj': out['w_ple_proj'], 'loss_target': out['loss_target'], 'm_attn_norm_g': out['m_attn_norm_g'], 'm_w_in': out['m_w_in'], 'm_gmlp_ln_g': out['m_gmlp_ln_g'], 'm_gmlp_ln_b': out['m_gmlp_ln_b'], 'm_gmlp_ws': out['m_gmlp_ws'], 'm_gmlp_bs': out['m_gmlp_bs'], 'm_q_norm_g': out['m_q_norm_g'], 'm_k_norm_g': out['m_k_norm_g'], 'm_sinks': out['m_sinks'], 'm_ssm_a_re': out['m_ssm_a_re'], 'm_ssm_a_im': out['m_ssm_a_im'], 'm_ssm_log_dt': out['m_ssm_log_dt'], 'm_ssm_b_re': out['m_ssm_b_re'], 'm_ssm_b_im': out['m_ssm_b_im'], 'm_ssm_c_re': out['m_ssm_c_re'], 'm_ssm_c_im': out['m_ssm_c_im'], 'm_ssm_d': out['m_ssm_d'], 'm_glu_w1': out['m_glu_w1'], 'm_glu_w2': out['m_glu_w2'], 'm_mix_out_g': out['m_mix_out_g'], 'm_w_out': out['m_w_out'], 'm_mlp_norm_g': out['m_mlp_norm_g'], 'm_w_ff1': out['m_w_ff1'], 'm_w_ff2': out['m_w_ff2'], 'm_ple_norm_g': out['m_ple_norm_g'], 'm_w_ple_gate': out['m_w_ple_gate'], 'm_w_ple_proj': out['m_w_ple_proj'], 'v_attn_norm_g': out['v_attn_norm_g'], 'v_w_in': out['v_w_in'], 'v_gmlp_ln_g': out['v_gmlp_ln_g'], 'v_gmlp_ln_b': out['v_gmlp_ln_b'], 'v_gmlp_ws': out['v_gmlp_ws'], 'v_gmlp_bs': out['v_gmlp_bs'], 'v_q_norm_g': out['v_q_norm_g'], 'v_k_norm_g': out['v_k_norm_g'], 'v_sinks': out['v_sinks'], 'v_ssm_a_re': out['v_ssm_a_re'], 'v_ssm_a_im': out['v_ssm_a_im'], 'v_ssm_log_dt': out['v_ssm_log_dt'], 'v_ssm_b_re': out['v_ssm_b_re'], 'v_ssm_b_im': out['v_ssm_b_im'], 'v_ssm_c_re': out['v_ssm_c_re'], 'v_ssm_c_im': out['v_ssm_c_im'], 'v_ssm_d': out['v_ssm_d'], 'v_glu_w1': out['v_glu_w1'], 'v_glu_w2': out['v_glu_w2'], 'v_mix_out_g': out['v_mix_out_g'], 'v_w_out': out['v_w_out'], 'v_mlp_norm_g': out['v_mlp_norm_g'], 'v_w_ff1': out['v_w_ff1'], 'v_w_ff2': out['v_w_ff2'], 'v_ple_norm_g': out['v_ple_norm_g'], 'v_w_ple_gate': out['v_w_ple_gate'], 'v_w_ple_proj': out['v_w_ple_proj']}


def _loss(weights, diff, rest, loss_target):
    with _jax.named_scope("forward"):
        args = {**rest, TWIN_DIFF_INPUT: diff, **{k: w.astype(_WEIGHT_DTYPES[k]) for k, w in weights.items()}}
        y = _forward(args)
    with _jax.named_scope("loss_head"):
        err = _jnp.square(y.astype(_jnp.float32) - loss_target)
        return 0.5 * _jnp.sum(_jnp.mean(err, axis=-1)) if err.ndim else 0.5 * err


def _adamw(w, g, m, v):
    m = ADAM_B1 * m + (1.0 - ADAM_B1) * g
    v = ADAM_B2 * v + (1.0 - ADAM_B2) * _jnp.square(g)
    m_hat = m / (1.0 - ADAM_B1 ** ADAM_STEP)
    v_hat = v / (1.0 - ADAM_B2 ** ADAM_STEP)
    delta = -ADAM_LR * (m_hat / (_jnp.sqrt(v_hat) + ADAM_EPS) + ADAM_WD * w)
    return delta, m, v


def reference(x, p, positions, attn_norm_g, w_in, gmlp_ln_g, gmlp_ln_b, gmlp_ws, gmlp_bs, q_norm_g, k_norm_g, sinks, ssm_a_re, ssm_a_im, ssm_log_dt, ssm_b_re, ssm_b_im, ssm_c_re, ssm_c_im, ssm_d, glu_w1, glu_w2, mix_out_g, w_out, mlp_norm_g, w_ff1, w_ff2, ple_norm_g, w_ple_gate, w_ple_proj, loss_target, m_attn_norm_g, m_w_in, m_gmlp_ln_g, m_gmlp_ln_b, m_gmlp_ws, m_gmlp_bs, m_q_norm_g, m_k_norm_g, m_sinks, m_ssm_a_re, m_ssm_a_im, m_ssm_log_dt, m_ssm_b_re, m_ssm_b_im, m_ssm_c_re, m_ssm_c_im, m_ssm_d, m_glu_w1, m_glu_w2, m_mix_out_g, m_w_out, m_mlp_norm_g, m_w_ff1, m_w_ff2, m_ple_norm_g, m_w_ple_gate, m_w_ple_proj, v_attn_norm_g, v_w_in, v_gmlp_ln_g, v_gmlp_ln_b, v_gmlp_ws, v_gmlp_bs, v_q_norm_g, v_k_norm_g, v_sinks, v_ssm_a_re, v_ssm_a_im, v_ssm_log_dt, v_ssm_b_re, v_ssm_b_im, v_ssm_c_re, v_ssm_c_im, v_ssm_d, v_glu_w1, v_glu_w2, v_mix_out_g, v_w_out, v_mlp_norm_g, v_w_ff1, v_w_ff2, v_ple_norm_g, v_w_ple_gate, v_w_ple_proj):
    given = dict(x=x, p=p, positions=positions, attn_norm_g=attn_norm_g, w_in=w_in, gmlp_ln_g=gmlp_ln_g, gmlp_ln_b=gmlp_ln_b, gmlp_ws=gmlp_ws, gmlp_bs=gmlp_bs, q_norm_g=q_norm_g, k_norm_g=k_norm_g, sinks=sinks, ssm_a_re=ssm_a_re, ssm_a_im=ssm_a_im, ssm_log_dt=ssm_log_dt, ssm_b_re=ssm_b_re, ssm_b_im=ssm_b_im, ssm_c_re=ssm_c_re, ssm_c_im=ssm_c_im, ssm_d=ssm_d, glu_w1=glu_w1, glu_w2=glu_w2, mix_out_g=mix_out_g, w_out=w_out, mlp_norm_g=mlp_norm_g, w_ff1=w_ff1, w_ff2=w_ff2, ple_norm_g=ple_norm_g, w_ple_gate=w_ple_gate, w_ple_proj=w_ple_proj, loss_target=loss_target, m_attn_norm_g=m_attn_norm_g, m_w_in=m_w_in, m_gmlp_ln_g=m_gmlp_ln_g, m_gmlp_ln_b=m_gmlp_ln_b, m_gmlp_ws=m_gmlp_ws, m_gmlp_bs=m_gmlp_bs, m_q_norm_g=m_q_norm_g, m_k_norm_g=m_k_norm_g, m_sinks=m_sinks, m_ssm_a_re=m_ssm_a_re, m_ssm_a_im=m_ssm_a_im, m_ssm_log_dt=m_ssm_log_dt, m_ssm_b_re=m_ssm_b_re, m_ssm_b_im=m_ssm_b_im, m_ssm_c_re=m_ssm_c_re, m_ssm_c_im=m_ssm_c_im, m_ssm_d=m_ssm_d, m_glu_w1=m_glu_w1, m_glu_w2=m_glu_w2, m_mix_out_g=m_mix_out_g, m_w_out=m_w_out, m_mlp_norm_g=m_mlp_norm_g, m_w_ff1=m_w_ff1, m_w_ff2=m_w_ff2, m_ple_norm_g=m_ple_norm_g, m_w_ple_gate=m_w_ple_gate, m_w_ple_proj=m_w_ple_proj, v_attn_norm_g=v_attn_norm_g, v_w_in=v_w_in, v_gmlp_ln_g=v_gmlp_ln_g, v_gmlp_ln_b=v_gmlp_ln_b, v_gmlp_ws=v_gmlp_ws, v_gmlp_bs=v_gmlp_bs, v_q_norm_g=v_q_norm_g, v_k_norm_g=v_k_norm_g, v_sinks=v_sinks, v_ssm_a_re=v_ssm_a_re, v_ssm_a_im=v_ssm_a_im, v_ssm_log_dt=v_ssm_log_dt, v_ssm_b_re=v_ssm_b_re, v_ssm_b_im=v_ssm_b_im, v_ssm_c_re=v_ssm_c_re, v_ssm_c_im=v_ssm_c_im, v_ssm_d=v_ssm_d, v_glu_w1=v_glu_w1, v_glu_w2=v_glu_w2, v_mix_out_g=v_mix_out_g, v_w_out=v_w_out, v_mlp_norm_g=v_mlp_norm_g, v_w_ff1=v_w_ff1, v_w_ff2=v_w_ff2, v_ple_norm_g=v_ple_norm_g, v_w_ple_gate=v_w_ple_gate, v_w_ple_proj=v_w_ple_proj)
    weights = {n: given[n] for n in TWIN_WEIGHTS}
    shared = {n: given[n] for n in SHARED_INPUTS}
    per_example = {n: given[n] for n in ['x', 'p', 'positions']}
    grad_fn = _jax.value_and_grad(_loss, argnums=(0, 1))

    def one_microbatch(ex, loss_target):
        ex = dict(ex)
        diff = ex.pop(TWIN_DIFF_INPUT)
        return grad_fn(weights, diff, {**shared, **ex}, loss_target)

    if N_MICROBATCH == 1:
        loss, (grad_w, grad_x) = one_microbatch(per_example, given["loss_target"])
    else:
        def body(carry, xs):
            loss_sum, grad_sum = carry
            l_k, (gw_k, gx_k) = one_microbatch(xs[0], xs[1])
            with _jax.named_scope("update"):
                return (loss_sum + l_k, _jax.tree.map(_jnp.add, grad_sum, gw_k)), gx_k

        init = (_jnp.zeros((), _jnp.float32), _jax.tree.map(_jnp.zeros_like, weights))
        (loss, grad_w), grad_x = _jax.lax.scan(body, init, (per_example, given["loss_target"]))
    with _jax.named_scope("update"):
        delta_w, new_m, new_v = {}, {}, {}
        for n in TWIN_WEIGHTS:
            delta_w[n], new_m[n], new_v[n] = _adamw(weights[n], grad_w[n], given["m_" + n], given["v_" + n])
    return (loss, grad_x, *[grad_w[n] for n in TWIN_WEIGHTS], *[delta_w[n] for n in TWIN_WEIGHTS],
            *[new_m[n] for n in TWIN_WEIGHTS], *[new_v[n] for n in TWIN_WEIGHTS])
```

```python
import functools
import math

import numpy as np
import jax
import jax.numpy as jnp
from jax import lax
from jax.experimental import pallas as pl
from jax.experimental.pallas import tpu as pltpu

F32 = jnp.float32
_MXU = jnp.bfloat16
_ACT = jnp.bfloat16

D_MODEL = 1024
HEAD_DIM = 64
A_HEADS = 4
CHUNK = 128
B_Q_HEADS = 8
B_KV_HEADS = 2
B_GROUP = 4
WINDOW = 128
ROPE_THETA = 10000.0
C_WIDTH = 256
C_GROUP = 16
C_GROUPS = 16
C_STATE = 64
N_STATE = C_GROUPS * C_STATE
IN_A, IN_Q, IN_KV, IN_C = 512, 512, 128, 256
IN_COLS = 1536
D_FF = 4096
PLE_DIM = 256
EPS = 1e-6
NEG = -1e30
ADAM_LR, ADAM_B1, ADAM_B2, ADAM_EPS, ADAM_WD, ADAM_STEP = 0.001, 0.9, 0.999, 1e-08, 0.01, 10

LANES = 128
SUBLANES = 8
VMEM_BYTES = 64 * 2 ** 20
N_CHIPS = 4
MESH = pl.DeviceIdType.MESH


def _vmem_limit(est_bytes):
    return int(min(max(2 * est_bytes + (8 << 20), 32 << 20), VMEM_BYTES - (6 << 20)))


def _cparams(est_bytes, **kw):
    return pltpu.CompilerParams(vmem_limit_bytes=_vmem_limit(est_bytes), **kw)


def _nbytes(shape, dtype):
    return int(np.prod(shape)) * jnp.dtype(dtype).itemsize


def _tile(dim, pref):
    t = min(dim, pref)
    while dim % t:
        t -= LANES
    assert t > 0, (dim, pref)
    return t


def _lane(shape):
    return lax.broadcasted_iota(jnp.int32, shape, len(shape) - 1)


def _row(shape):
    return lax.broadcasted_iota(jnp.int32, shape, len(shape) - 2)


def _gelu(x):
    c = math.sqrt(2.0 / math.pi)
    return 0.5 * x * (1.0 + jnp.tanh(c * (x + 0.044715 * (x * x * x))))


def _gelu_grad(x):
    c = math.sqrt(2.0 / math.pi)
    t = jnp.tanh(c * (x + 0.044715 * (x * x * x)))
    return 0.5 * (1.0 + t) + 0.5 * x * (1.0 - t * t) * (c * (1.0 + 3.0 * 0.044715 * (x * x)))


def _sigmoid(x):
    return 1.0 / (1.0 + jnp.exp(-x))


def _dot(a, b, dims=(((1,), (0,)), ((), ()))):
    return lax.dot_general(a.astype(_MXU), b.astype(_MXU), dims, preferred_element_type=F32)


_NT = (((1,), (1,)), ((), ()))
_TN = (((0,), (0,)), ((), ()))
_NN = (((1,), (0,)), ((), ()))


def _mm(a, b, *, mode, M, N, K, out_dtypes, name, epi=None, extras=(), b_cb=False, o_cb=False,
        a_off=0, b_off=0, tm=512, tn=1024, tk=1024):
    if b_cb or o_cb:
        nc = (b.shape[2] if b_cb else N // 4)
    if mode == "nn":
        if b_cb:
            tn = nc
        tm, tn, tk = _tile(M, tm), _tile(N, tn), _tile(K, tk)
        a_spec = pl.BlockSpec((tm, tk), lambda i, j, k: (i, k + a_off))
        if b_cb:
            b_spec = pl.BlockSpec((None, tk, tn), lambda i, j, k: (j, k, 0))
        else:
            b_spec = pl.BlockSpec((tk, tn), lambda i, j, k: (k, j + b_off))
        dims = _NN
        a_blk, b_blk = (tm, tk), (tk, tn)
    elif mode == "nt":
        if b_cb:
            tk = nc
        tm, tn, tk = _tile(M, tm), _tile(N, tn), _tile(K, tk)
        a_spec = pl.BlockSpec((tm, tk), lambda i, j, k: (i, k + a_off))
        if b_cb:
            b_spec = pl.BlockSpec((None, tn, tk), lambda i, j, k: (k, j, 0))
        else:
            b_spec = pl.BlockSpec((tn, tk), lambda i, j, k: (j, k + b_off))
        dims = _NT
        a_blk, b_blk = (tm, tk), (tn, tk)
    else:
        if o_cb:
            tn = nc
        tm, tn, tk = _tile(M, tm), _tile(N, tn), _tile(K, tk)
        a_spec = pl.BlockSpec((tk, tm), lambda i, j, k: (k, i + a_off))
        b_spec = pl.BlockSpec((tk, tn), lambda i, j, k: (k, j + b_off))
        dims = _TN
        a_blk, b_blk = (tk, tm), (tk, tn)
    gi, gj, gk = M // tm, N // tn, K // tk
    if o_cb:
        o_spec = pl.BlockSpec((None, tm, tn), lambda i, j, k: (j, i, 0))
        o_shape = (gj, M, tn)
    else:
        o_spec = pl.BlockSpec((tm, tn), lambda i, j, k: (i, j))
        o_shape = (M, N)
    e_specs = []
    for e, off in extras:
        if e.shape[0] == 1:
            e_specs.append(pl.BlockSpec((1, tn), lambda i, j, k, off=off: (0, j + off)))
        else:
            e_specs.append(pl.BlockSpec((tm, tn), lambda i, j, k, off=off: (i, j + off)))
    extras = [e for e, _ in extras]
    ne, no = len(extras), len(out_dtypes)

    def body(*refs):
        a_ref, b_ref = refs[0], refs[1]
        e_refs = refs[2:2 + ne]
        o_refs = refs[2 + ne:2 + ne + no]

        def fin(acc):
            vals = epi(acc, *[e[...] for e in e_refs]) if epi is not None else (acc,)
            for o, v in zip(o_refs, vals):
                o[...] = v.astype(o.dtype)

        prod = _dot(a_ref[...], b_ref[...], dims)
        if gk == 1:
            fin(prod)
        else:
            acc_ref = refs[-1]
            k = pl.program_id(2)

            @pl.when(k == 0)
            def _():
                acc_ref[...] = prod

            @pl.when(k > 0)
            def _():
                acc_ref[...] += prod

            @pl.when(k == gk - 1)
            def _():
                fin(acc_ref[...])

    est = (_nbytes(a_blk, a.dtype) + _nbytes(b_blk, b.dtype)
           + sum(_nbytes((tm, tn), d) for d in out_dtypes)
           + sum(_nbytes((tm, tn), e.dtype) for e in extras)) + 2 * _nbytes((tm, tn), F32)
    outs = pl.pallas_call(
        body, name=name, grid=(gi, gj, gk),
        in_specs=[a_spec, b_spec] + e_specs,
        out_specs=[o_spec] * no,
        out_shape=[jax.ShapeDtypeStruct(o_shape, d) for d in out_dtypes],
        scratch_shapes=([pltpu.VMEM((tm, tn), F32)] if gk > 1 else []),
        compiler_params=_cparams(est, dimension_semantics=("parallel", "parallel", "arbitrary")),
    )(a, b, *extras)
    return outs if no > 1 else outs[0]


_TL = 512


def _rms_fwd(h, g, name):
    L, D = h.shape
    tl = _tile(L, _TL)

    def body(h_ref, g_ref, o_ref):
        x = h_ref[...]
        r = lax.rsqrt(jnp.mean(x * x, axis=-1, keepdims=True) + EPS)
        o_ref[...] = ((x * r) * g_ref[...]).astype(o_ref.dtype)

    return pl.pallas_call(
        body, name=name, grid=(L // tl,),
        in_specs=[pl.BlockSpec((tl, D), lambda i: (i, 0)), pl.BlockSpec((1, D), lambda i: (0, 0))],
        out_specs=pl.BlockSpec((tl, D), lambda i: (i, 0)),
        out_shape=jax.ShapeDtypeStruct((L, D), _ACT),
        compiler_params=_cparams(3 * _nbytes((tl, D), F32)),
    )(h, g.reshape(1, D))


def _rms_bwd(dxn, h, g, dres, name):
    L, D = h.shape
    tl = _tile(L, _TL)

    def body(d_ref, h_ref, g_ref, r_ref, o_ref, dg_ref):
        x = h_ref[...]
        r = lax.rsqrt(jnp.mean(x * x, axis=-1, keepdims=True) + EPS)
        xhat = x * r
        d = d_ref[...].astype(F32)
        gy = d * g_ref[...]
        dx = r * (gy - xhat * jnp.mean(gy * xhat, axis=-1, keepdims=True))
        o_ref[...] = r_ref[...] + dx

        @pl.when(pl.program_id(0) == 0)
        def _():
            dg_ref[...] = jnp.zeros_like(dg_ref)

        dg_ref[...] += jnp.sum(d * xhat, axis=0, keepdims=True)

    dh, dg = pl.pallas_call(
        body, name=name, grid=(L // tl,),
        in_specs=[pl.BlockSpec((tl, D), lambda i: (i, 0)), pl.BlockSpec((tl, D), lambda i: (i, 0)),
                  pl.BlockSpec((1, D), lambda i: (0, 0)), pl.BlockSpec((tl, D), lambda i: (i, 0))],
        out_specs=[pl.BlockSpec((tl, D), lambda i: (i, 0)), pl.BlockSpec((1, D), lambda i: (0, 0))],
        out_shape=[jax.ShapeDtypeStruct((L, D), F32), jax.ShapeDtypeStruct((1, D), F32)],
        compiler_params=_cparams(5 * _nbytes((tl, D), F32)),
    )(dxn, h, g.reshape(1, D), dres)
    return dh, dg.reshape(D)


def _rope_tables(positions):
    L = positions.shape[0]
    tl = _tile(L, 1024)
    inv = 1.0 / (ROPE_THETA ** (np.arange(0, HEAD_DIM, 2, dtype=np.float32) / HEAD_DIM))
    inv128 = jnp.asarray(np.tile(inv.astype(np.float32), 4).reshape(1, LANES))

    def body(p_ref, i_ref, c_ref, s_ref):
        ang = p_ref[...].astype(F32) * i_ref[...]
        c_ref[...] = jnp.cos(ang)
        s_ref[...] = jnp.sin(ang)

    return pl.pallas_call(
        body, name="rope_tables", grid=(L // tl,),
        in_specs=[pl.BlockSpec((tl, 1), lambda i: (i, 0)), pl.BlockSpec((1, LANES), lambda i: (0, 0))],
        out_specs=[pl.BlockSpec((tl, LANES), lambda i: (i, 0))] * 2,
        out_shape=[jax.ShapeDtypeStruct((L, LANES), F32)] * 2,
    )(positions.reshape(L, 1), inv128)


_GM_TL = 256


def _gmlp_head(Z, W, bfull, lg, lb, maskv):
    G = _gelu(Z)
    mu = jnp.sum(jnp.where(maskv, G, 0.0), axis=-1, keepdims=True) * (1.0 / HEAD_DIM)
    xc = jnp.where(maskv, G - mu, 0.0)
    var = jnp.sum(xc * xc, axis=-1, keepdims=True) * (1.0 / HEAD_DIM)
    rstd = lax.rsqrt(var + EPS)
    xhat = xc * rstd
    vn = xhat * lg + lb
    sv = _dot(W, vn) + bfull
    return G, xhat, rstd, vn, sv


def _tril(W):
    return jnp.where(_row(W.shape) >= _lane(W.shape), W, 0.0)


def _triu(W):
    return jnp.where(_row(W.shape) <= _lane(W.shape), W, 0.0)


def _gmlp_fwd(z, ws, bfull, lgf, lbf, name):
    L = z.shape[0]
    tl = _tile(L, _GM_TL)
    nch = tl // CHUNK

    def body(z_ref, w_ref, b_ref, lg_ref, lb_ref, o_ref):
        maskv = _lane((CHUNK, LANES)) >= HEAD_DIM
        for c in range(nch):
            rows = slice(c * CHUNK, (c + 1) * CHUNK)
            for hp in range(A_HEADS // 2):
                acc = None
                for hh in range(2):
                    h = 2 * hp + hh
                    Z = z_ref[rows, h * LANES:(h + 1) * LANES]
                    G, _, _, _, sv = _gmlp_head(Z, _tril(w_ref[h]), b_ref[h], lg_ref[h:h + 1, :], lb_ref[h:h + 1, :], maskv)
                    prod = G * pltpu.roll(sv, HEAD_DIM, axis=1)
                    acc = prod if hh == 0 else acc + pltpu.roll(prod, HEAD_DIM, axis=1)
                o_ref[rows, hp * LANES:(hp + 1) * LANES] = acc

    return pl.pallas_call(
        body, name=name, grid=(L // tl,),
        in_specs=[pl.BlockSpec((tl, IN_A), lambda i: (i, 0)),
                  pl.BlockSpec((A_HEADS, CHUNK, CHUNK), lambda i: (0, 0, 0)),
                  pl.BlockSpec((A_HEADS, CHUNK, LANES), lambda i: (0, 0, 0)),
                  pl.BlockSpec((A_HEADS, LANES), lambda i: (0, 0)),
                  pl.BlockSpec((A_HEADS, LANES), lambda i: (0, 0))],
        out_specs=pl.BlockSpec((tl, 2 * LANES), lambda i: (i, 0)),
        out_shape=jax.ShapeDtypeStruct((L, 2 * LANES), F32),
    )(z, ws, bfull, lgf, lbf)


def _gmlp_bwd(z, dya, ws, wsT, bfull, lgf, lbf, name):
    L = z.shape[0]
    tl = _tile(L, _GM_TL)
    nch = tl // CHUNK
    nsteps = L // tl

    def body(z_ref, d_ref, w_ref, wt_ref, b_ref, lg_ref, lb_ref, dz_ref, dw_ref, db_ref, dlg_ref, dlb_ref):
        step = pl.program_id(0)

        @pl.when(step == 0)
        def _():
            dw_ref[...] = jnp.zeros_like(dw_ref)
            db_ref[...] = jnp.zeros_like(db_ref)
            dlg_ref[...] = jnp.zeros_like(dlg_ref)
            dlb_ref[...] = jnp.zeros_like(dlb_ref)

        lane = _lane((CHUNK, LANES))
        maskv = lane >= HEAD_DIM
        for c in range(nch):
            rows = slice(c * CHUNK, (c + 1) * CHUNK)
            for h in range(A_HEADS):
                hp, hh = divmod(h, 2)
                Z = z_ref[rows, h * LANES:(h + 1) * LANES]
                lg = lg_ref[h:h + 1, :]
                G, xhat, rstd, vn, sv = _gmlp_head(Z, _tril(w_ref[h]), b_ref[h], lg, lb_ref[h:h + 1, :], maskv)
                dpair = d_ref[rows, hp * LANES:(hp + 1) * LANES]
                if hh == 1:
                    dpair = pltpu.roll(dpair, HEAD_DIM, axis=1)
                dout = jnp.where(maskv, 0.0, dpair)
                du = dout * pltpu.roll(sv, HEAD_DIM, axis=1)
                dsv = pltpu.roll(dout * G, HEAD_DIM, axis=1)
                dw_ref[h] += _tril(_dot(dsv, vn, _NT))
                db_ref[h] += dsv
                dvn = _dot(_triu(wt_ref[h]), dsv)
                dlg_ref[h] += dvn * xhat
                dlb_ref[h] += dvn
                dxh = dvn * lg
                m1 = jnp.sum(dxh, axis=-1, keepdims=True) * (1.0 / HEAD_DIM)
                m2 = jnp.sum(dxh * xhat, axis=-1, keepdims=True) * (1.0 / HEAD_DIM)
                dv = jnp.where(maskv, rstd * (dxh - m1 - xhat * m2), 0.0)
                dz_ref[rows, h * LANES:(h + 1) * LANES] = ((du + dv) * _gelu_grad(Z)).astype(dz_ref.dtype)

        @pl.when(step == nsteps - 1)
        def _():
            for h in range(A_HEADS):
                db_ref[h] = jnp.broadcast_to(jnp.sum(db_ref[h], axis=1, keepdims=True), (CHUNK, LANES))
                dlg_ref[h] = jnp.broadcast_to(jnp.sum(dlg_ref[h], axis=0, keepdims=True), (CHUNK, LANES))
                dlb_ref[h] = jnp.broadcast_to(jnp.sum(dlb_ref[h], axis=0, keepdims=True), (CHUNK, LANES))

    full3 = pl.BlockSpec((A_HEADS, CHUNK, LANES), lambda i: (0, 0, 0))
    return pl.pallas_call(
        body, name=name, grid=(nsteps,),
        in_specs=[pl.BlockSpec((tl, IN_A), lambda i: (i, 0)),
                  pl.BlockSpec((tl, 2 * LANES), lambda i: (i, 0)),
                  full3, full3, full3,
                  pl.BlockSpec((A_HEADS, LANES), lambda i: (0, 0)),
                  pl.BlockSpec((A_HEADS, LANES), lambda i: (0, 0))],
        out_specs=[pl.BlockSpec((tl, IN_A), lambda i: (i, 0)), full3, full3, full3, full3],
        out_shape=[jax.ShapeDtypeStruct((L, IN_A), _ACT)] + [jax.ShapeDtypeStruct((A_HEADS, CHUNK, LANES), F32)] * 4,
    )(z, dya, ws, wsT, bfull, lgf, lbf)


def _head_rstd(x, lo):
    sq = x * x
    s_lo = jnp.sum(jnp.where(lo, sq, 0.0), axis=-1, keepdims=True)
    s_hi = jnp.sum(jnp.where(lo, 0.0, sq), axis=-1, keepdims=True)
    return jnp.where(lo, lax.rsqrt(s_lo * (1.0 / HEAD_DIM) + EPS), lax.rsqrt(s_hi * (1.0 / HEAD_DIM) + EPS))


def _rot_half(x, first):
    return jnp.where(first, -pltpu.roll(x, LANES - HEAD_DIM // 2, axis=1), pltpu.roll(x, HEAD_DIM // 2, axis=1))


def _qk_prep(z, cos, sin, gq, gk, name):
    L = z.shape[0]
    tl = _tile(L, _TL)
    nq = IN_Q // LANES

    def body(q_ref, k_ref, c_ref, s_ref, gq_ref, gk_ref, qo_ref, ko_ref):
        lane = _lane((tl, LANES))
        lo = lane < HEAD_DIM
        first = (lane % HEAD_DIM) < (HEAD_DIM // 2)
        c, s = c_ref[...], s_ref[...]

        def prep(x, g):
            xn = (x * _head_rstd(x, lo)) * g
            return xn * c + _rot_half(xn, first) * s

        for j in range(nq):
            qo_ref[:, j * LANES:(j + 1) * LANES] = prep(q_ref[:, j * LANES:(j + 1) * LANES], gq_ref[...]).astype(qo_ref.dtype)
        ko_ref[...] = prep(k_ref[...], gk_ref[...]).astype(ko_ref.dtype)

    return pl.pallas_call(
        body, name=name, grid=(L // tl,),
        in_specs=[pl.BlockSpec((tl, IN_Q), lambda i: (i, 1)),
                  pl.BlockSpec((tl, IN_KV), lambda i: (i, 8)),
                  pl.BlockSpec((tl, LANES), lambda i: (i, 0)), pl.BlockSpec((tl, LANES), lambda i: (i, 0)),
                  pl.BlockSpec((1, LANES), lambda i: (0, 0)), pl.BlockSpec((1, LANES), lambda i: (0, 0))],
        out_specs=[pl.BlockSpec((tl, IN_Q), lambda i: (i, 0)), pl.BlockSpec((tl, IN_KV), lambda i: (i, 0))],
        out_shape=[jax.ShapeDtypeStruct((L, IN_Q), _ACT), jax.ShapeDtypeStruct((L, IN_KV), _ACT)],
    )(z, z, cos, sin, gq, gk)


def _qk_prep_bwd(z, dq, dkc, dkp, dvc, dvp, cos, sin, gq, gk, name):
    L = z.shape[0]
    tl = WINDOW
    nb = L // tl
    nq = IN_Q // LANES

    def body(q_ref, k_ref, dq_ref, dkc_ref, dkp_ref, dvc_ref, dvp_ref, c_ref, s_ref, gq_ref, gk_ref,
             dzq_ref, dzk_ref, dzv_ref, dgq_ref, dgk_ref):
        n = pl.program_id(0)

        @pl.when(n == 0)
        def _():
            dgq_ref[...] = jnp.zeros_like(dgq_ref)
            dgk_ref[...] = jnp.zeros_like(dgk_ref)

        lane = _lane((tl, LANES))
        lo = lane < HEAD_DIM
        first = (lane % HEAD_DIM) < (HEAD_DIM // 2)
        c, s = c_ref[...], s_ref[...]
        has_next = jnp.where(n < nb - 1, 1.0, 0.0)

        def bwd(x, g, dy):
            r = _head_rstd(x, lo)
            xhat = x * r
            dxn = dy * c - _rot_half(dy * s, first)
            gy = dxn * g
            t = gy * xhat
            m_lo = jnp.sum(jnp.where(lo, t, 0.0), axis=-1, keepdims=True)
            m_hi = jnp.sum(jnp.where(lo, 0.0, t), axis=-1, keepdims=True)
            m = jnp.where(lo, m_lo, m_hi) * (1.0 / HEAD_DIM)
            dx = r * (gy - xhat * m)
            dg = jnp.sum(dxn * xhat, axis=0, keepdims=True)
            return dx, dg

        dgq = jnp.zeros((1, LANES), F32)
        for j in range(nq):
            sl = slice(j * LANES, (j + 1) * LANES)
            dx, dg = bwd(q_ref[:, sl], gq_ref[...], dq_ref[:, sl].astype(F32))
            dzq_ref[:, sl] = dx.astype(dzq_ref.dtype)
            dgq = dgq + dg
        dgq_ref[...] += dgq + pltpu.roll(dgq, HEAD_DIM, axis=1)
        dk = dkc_ref[...] + has_next * dkp_ref[...]
        dx, dg = bwd(k_ref[...], gk_ref[...], dk)
        dzk_ref[...] = dx.astype(dzk_ref.dtype)
        dgk_ref[...] += dg + pltpu.roll(dg, HEAD_DIM, axis=1)
        dzv_ref[...] = (dvc_ref[...] + has_next * dvp_ref[...]).astype(dzv_ref.dtype)

    nxt = lambda i: (jnp.minimum(i + 1, nb - 1), 0)
    cur = lambda i: (i, 0)
    kv = pl.BlockSpec((tl, IN_KV), cur)
    one = pl.BlockSpec((1, LANES), lambda i: (0, 0))
    return pl.pallas_call(
        body, name=name, grid=(nb,),
        in_specs=[pl.BlockSpec((tl, IN_Q), lambda i: (i, 1)), pl.BlockSpec((tl, IN_KV), lambda i: (i, 8)),
                  pl.BlockSpec((tl, IN_Q), cur), kv, pl.BlockSpec((tl, IN_KV), nxt), kv, pl.BlockSpec((tl, IN_KV), nxt),
                  kv, kv, one, one],
        out_specs=[pl.BlockSpec((tl, IN_Q), cur), kv, kv, one, one],
        out_shape=[jax.ShapeDtypeStruct((L, IN_Q), _ACT), jax.ShapeDtypeStruct((L, IN_KV), _ACT),
                   jax.ShapeDtypeStruct((L, IN_KV), _ACT), jax.ShapeDtypeStruct((1, LANES), F32),
                   jax.ShapeDtypeStruct((1, LANES), F32)],
    )(z, z, dq, dkc, dkp, dvc, dvp, cos, sin, gq, gk)


def _attn_mask(n):
    shp = (B_GROUP * WINDOW, 2 * WINDOW)
    qi = _row(shp) % WINDOW
    kj = _lane(shp)
    off = jnp.where(n > 0, 0, 4 * WINDOW)
    return ((kj >= WINDOW) & (kj - WINDOW <= qi)) | ((kj < WINDOW) & (kj > qi + off))


def _kv_lanes(j):
    lane = _lane((WINDOW, LANES))
    return (lane >= j * HEAD_DIM) & (lane < (j + 1) * HEAD_DIM)


def _stack_heads(ref, j, kvl):
    parts = []
    for g in range(B_GROUP):
        h = j * B_GROUP + g
        slab = ref[:, (h // 2) * LANES:(h // 2 + 1) * LANES].astype(F32)
        if (h % 2) != j:
            slab = pltpu.roll(slab, HEAD_DIM, axis=1)
        parts.append(jnp.where(kvl, slab, 0.0))
    return jnp.concatenate(parts, axis=0)


def _attn_probs(qs, k2, sink_col, mask):
    s = _dot(qs, k2, _NT) * (HEAD_DIM ** -0.5)
    s = jnp.where(mask, s, NEG)
    m = jnp.maximum(jnp.max(s, axis=-1, keepdims=True), sink_col)
    p = jnp.exp(s - m)
    esink = jnp.exp(sink_col - m)
    inv = 1.0 / (jnp.sum(p, axis=-1, keepdims=True) + esink)
    return p * inv, esink * inv


def _sink_col(sink_ref, j):
    return jnp.concatenate([jnp.full((WINDOW, 1), sink_ref[j * B_GROUP + g], F32) for g in range(B_GROUP)], axis=0)


def _attn_fwd(q, k, z, sinks, name):
    L = q.shape[0]
    nb = L // WINDOW
    prev = lambda n: (jnp.maximum(n - 1, 0), 0)
    prev_v = lambda n: (jnp.maximum(n - 1, 0), 9)

    def body(s_ref, q_ref, kp_ref, kc_ref, vp_ref, vc_ref, o_ref):
        n = pl.program_id(0)
        mask = _attn_mask(n)
        k2 = jnp.concatenate([kp_ref[...], kc_ref[...]], axis=0)
        v2 = jnp.concatenate([vp_ref[...], vc_ref[...]], axis=0)
        slabs = [None] * (IN_Q // LANES)
        for j in range(B_KV_HEADS):
            kvl = _kv_lanes(j)
            qs = _stack_heads(q_ref, j, kvl)
            pn, _ = _attn_probs(qs, k2, _sink_col(s_ref, j), mask)
            o = _dot(pn, v2)
            for g in range(B_GROUP):
                h = j * B_GROUP + g
                piece = jnp.where(kvl, o[g * WINDOW:(g + 1) * WINDOW], 0.0)
                if (h % 2) != j:
                    piece = pltpu.roll(piece, HEAD_DIM, axis=1)
                slabs[h // 2] = piece if slabs[h // 2] is None else slabs[h // 2] + piece
        for t, sl in enumerate(slabs):
            o_ref[:, t * LANES:(t + 1) * LANES] = sl

    return pl.pallas_call(
        body, name=name, grid=(nb,),
        in_specs=[pl.BlockSpec(memory_space=pltpu.SMEM),
                  pl.BlockSpec((WINDOW, IN_Q), lambda n: (n, 0)),
                  pl.BlockSpec((WINDOW, IN_KV), prev), pl.BlockSpec((WINDOW, IN_KV), lambda n: (n, 0)),
                  pl.BlockSpec((WINDOW, IN_KV), prev_v), pl.BlockSpec((WINDOW, IN_KV), lambda n: (n, 9))],
        out_specs=pl.BlockSpec((WINDOW, IN_Q), lambda n: (n, 0)),
        out_shape=jax.ShapeDtypeStruct((L, IN_Q), F32),
    )(sinks, q, k, k, z, z)


def _attn_bwd(q, k, z, sinks, dyb, name):
    L = q.shape[0]
    nb = L // WINDOW
    prev = lambda n: (jnp.maximum(n - 1, 0), 0)
    prev_v = lambda n: (jnp.maximum(n - 1, 0), 9)
    cur = lambda n: (n, 0)

    def body(s_ref, q_ref, kp_ref, kc_ref, vp_ref, vc_ref, d_ref, dq_ref, dkc_ref, dkp_ref, dvc_ref, dvp_ref, ds_ref):
        n = pl.program_id(0)

        @pl.when(n == 0)
        def _():
            ds_ref[...] = jnp.zeros_like(ds_ref)

        mask = _attn_mask(n)
        k2 = jnp.concatenate([kp_ref[...], kc_ref[...]], axis=0)
        v2 = jnp.concatenate([vp_ref[...], vc_ref[...]], axis=0)
        slabs = [None] * (IN_Q // LANES)
        dk2 = jnp.zeros((2 * WINDOW, LANES), F32)
        dv2 = jnp.zeros((2 * WINDOW, LANES), F32)
        dsink = jnp.zeros((1, LANES), F32)
        lane1 = _lane((1, LANES))
        for j in range(B_KV_HEADS):
            kvl = _kv_lanes(j)
            qs = _stack_heads(q_ref, j, kvl)
            dos = _stack_heads(d_ref, j, kvl)
            pn, psink = _attn_probs(qs, k2, _sink_col(s_ref, j), mask)
            dp = _dot(dos, v2, _NT)
            dd = jnp.sum(pn * dp, axis=-1, keepdims=True)
            dss = (pn * (dp - dd)) * (HEAD_DIM ** -0.5)
            dqs = _dot(dss, k2)
            dk2 = dk2 + _dot(dss, qs, _TN)
            dv2 = dv2 + _dot(pn, dos, _TN)
            sd = psink * dd
            for g in range(B_GROUP):
                h = j * B_GROUP + g
                piece = jnp.where(kvl, dqs[g * WINDOW:(g + 1) * WINDOW], 0.0)
                if (h % 2) != j:
                    piece = pltpu.roll(piece, HEAD_DIM, axis=1)
                slabs[h // 2] = piece if slabs[h // 2] is None else slabs[h // 2] + piece
                tot = jnp.sum(sd[g * WINDOW:(g + 1) * WINDOW], axis=0, keepdims=True)
                dsink = dsink - jnp.where(lane1 == h, tot, 0.0)
        for t, sl in enumerate(slabs):
            dq_ref[:, t * LANES:(t + 1) * LANES] = sl
        dkp_ref[...] = dk2[:WINDOW]
        dkc_ref[...] = dk2[WINDOW:]
        dvp_ref[...] = dv2[:WINDOW]
        dvc_ref[...] = dv2[WINDOW:]
        ds_ref[0:1, :] += dsink

    kvs = pl.BlockSpec((WINDOW, IN_KV), cur)
    kvo = jax.ShapeDtypeStruct((L, IN_KV), F32)
    return pl.pallas_call(
        body, name=name, grid=(nb,),
        in_specs=[pl.BlockSpec(memory_space=pltpu.SMEM),
                  pl.BlockSpec((WINDOW, IN_Q), cur),
                  pl.BlockSpec((WINDOW, IN_KV), prev), kvs,
                  pl.BlockSpec((WINDOW, IN_KV), prev_v), pl.BlockSpec((WINDOW, IN_KV), lambda n: (n, 9)),
                  pl.BlockSpec((WINDOW, IN_Q), cur)],
        out_specs=[pl.BlockSpec((WINDOW, IN_Q), cur), kvs, kvs, kvs, kvs, pl.BlockSpec((SUBLANES, LANES), lambda n: (0, 0))],
        out_shape=[jax.ShapeDtypeStruct((L, IN_Q), F32), kvo, kvo, kvo, kvo, jax.ShapeDtypeStruct((SUBLANES, LANES), F32)],
    )(sinks, q, k, k, z, z, dyb)


def _ssm_disc(are, aim, ldt, bre, bim):
    dt = jnp.exp(ldt)
    mag = jnp.exp(are * dt)
    lr, li = mag * jnp.cos(aim * dt), mag * jnp.sin(aim * dt)
    den = are * are + aim * aim
    xr, xi = lr - 1.0, li
    cr, ci = (xr * are + xi * aim) / den, (xi * are - xr * aim) / den
    return lr, li, cr * bre - ci * bim, cr * bim + ci * bre


def _ssm_prep(are, aim, ldt, bre, bim):
    shp3, shpb = are.shape, bre.shape

    def body(are_ref, aim_ref, ldt_ref, bre_ref, bim_ref, lr_ref, li_ref, br_ref, bi_ref):
        lr, li, br, bi = _ssm_disc(are_ref[...], aim_ref[...], ldt_ref[...], bre_ref[...], bim_ref[...])
        lr_ref[...] = lr
        li_ref[...] = li
        br_ref[...] = br
        bi_ref[...] = bi

    return pl.pallas_call(
        body, name="ssm_prep",
        out_shape=[jax.ShapeDtypeStruct(shp3, F32)] * 2 + [jax.ShapeDtypeStruct(shpb, F32)] * 2,
    )(are, aim, ldt, bre, bim)


def _ssm_prep_bwd(are, aim, ldt, bre, bim, dlr, dli, dbr, dbi):
    shp3, shpb = are.shape, bre.shape

    def body(are_ref, aim_ref, ldt_ref, bre_ref, bim_ref, dlr_ref, dli_ref, dbr_ref, dbi_ref,
             o_are, o_aim, o_ldt, o_bre, o_bim):
        _, vjp = jax.vjp(_ssm_disc, are_ref[...], aim_ref[...], ldt_ref[...], bre_ref[...], bim_ref[...])
        g = vjp((dlr_ref[...], dli_ref[...], dbr_ref[...], dbi_ref[...]))
        o_are[...] = g[0]
        o_aim[...] = g[1]
        o_ldt[...] = jnp.broadcast_to(jnp.sum(g[2], axis=-1, keepdims=True), shp3)
        o_bre[...] = g[3]
        o_bim[...] = g[4]

    return pl.pallas_call(
        body, name="ssm_prep_bwd",
        out_shape=[jax.ShapeDtypeStruct(shp3, F32)] * 3 + [jax.ShapeDtypeStruct(shpb, F32)] * 2,
    )(are, aim, ldt, bre, bim, dlr, dli, dbr, dbi)


_SCAN_TB = 512
_SCAN_W = 256


def _cmul(ar, ai, br, bi):
    return ar * br - ai * bi, ar * bi + ai * br


def _ssm_scan(x, lam_r, lam_i, name, reverse=False, states=None):
    L = x.shape[0]
    tb = _tile(L, _SCAN_TB)
    nrb = L // tb
    nt = tb // SUBLANES
    W = _SCAN_W
    with_da = states is not None

    def body(*refs):
        if with_da:
            xr_ref, xi_ref, sr_ref, si_ref, ar_ref, ai_ref, o_ref, dar_ref, dai_ref, cr_ref, ci_ref = refs
        else:
            xr_ref, xi_ref, ar_ref, ai_ref, o_ref, cr_ref, ci_ref = refs
        step = pl.program_id(0)

        @pl.when(step == 0)
        def _():
            cr_ref[...] = jnp.zeros_like(cr_ref)
            ci_ref[...] = jnp.zeros_like(ci_ref)
            if with_da:
                dar_ref[...] = jnp.zeros_like(dar_ref)
                dai_ref[...] = jnp.zeros_like(dai_ref)

        row = _row((SUBLANES, W))

        def shift(v, d, fill):
            if reverse:
                return jnp.where(row < SUBLANES - d, pltpu.roll(v, SUBLANES - d, axis=0), fill)
            return jnp.where(row >= d, pltpu.roll(v, d, axis=0), fill)

        edge = 0 if reverse else SUBLANES - 1
        for wb in range(N_STATE // W):
            cols = slice(wb * W, (wb + 1) * W)
            a1r = jnp.broadcast_to(ar_ref[:, cols], (SUBLANES, W))
            a1i = jnp.broadcast_to(ai_ref[:, cols], (SUBLANES, W))
            if reverse:
                a1i = -a1i
            a2r, a2i = _cmul(a1r, a1i, a1r, a1i)
            a4r, a4i = _cmul(a2r, a2i, a2r, a2i)
            pws = ((1, a1r, a1i), (2, a2r, a2i), (4, a4r, a4i))
            pr, pi = a1r, a1i
            for d, _, _ in pws:
                qr, qi = _cmul(pr, pi, shift(pr, d, 1.0), shift(pi, d, 0.0))
                pr, pi = qr, qi

            def tile(i, carry):
                cr, ci, dr, di = carry
                t = (nt - 1 - i) if reverse else i
                r0 = pl.multiple_of(t * SUBLANES, SUBLANES)
                vr = xr_ref[pl.ds(r0, SUBLANES), cols]
                vi = xi_ref[pl.ds(r0, SUBLANES), cols]
                for d, er, ei in pws:
                    tr, ti = _cmul(er, ei, shift(vr, d, 0.0), shift(vi, d, 0.0))
                    vr, vi = vr + tr, vi + ti
                tr, ti = _cmul(pr, pi, cr, ci)
                vr, vi = vr + tr, vi + ti
                o_ref[pl.ds(r0, SUBLANES), cols] = vr
                o_ref[pl.ds(r0, SUBLANES), slice(N_STATE + wb * W, N_STATE + (wb + 1) * W)] = vi
                if with_da:
                    gr = jnp.where(row < SUBLANES - 1, pltpu.roll(vr, SUBLANES - 1, axis=0), cr)
                    gi = jnp.where(row < SUBLANES - 1, pltpu.roll(vi, SUBLANES - 1, axis=0), ci)
                    sr = sr_ref[pl.ds(r0, SUBLANES), cols]
                    si = si_ref[pl.ds(r0, SUBLANES), cols]
                    dr = dr + sr * gr + si * gi
                    di = di + sr * gi - si * gr
                ncr = jnp.broadcast_to(vr[edge:edge + 1, :], (SUBLANES, W))
                nci = jnp.broadcast_to(vi[edge:edge + 1, :], (SUBLANES, W))
                return ncr, nci, dr, di

            zero = jnp.zeros((SUBLANES, W), F32)
            cr, ci, dr, di = lax.fori_loop(0, nt, tile, (cr_ref[:, cols], ci_ref[:, cols], zero, zero), unroll=2)
            cr_ref[:, cols] = cr
            ci_ref[:, cols] = ci
            if with_da:
                dar_ref[:, cols] += dr
                dai_ref[:, cols] += di

        if with_da:
            @pl.when(step == nrb - 1)
            def _():
                dar_ref[...] = jnp.broadcast_to(jnp.sum(dar_ref[...], axis=0, keepdims=True), dar_ref.shape)
                dai_ref[...] = jnp.broadcast_to(jnp.sum(dai_ref[...], axis=0, keepdims=True), dai_ref.shape)

    rb = (lambda i: (nrb - 1 - i, 0)) if reverse else (lambda i: (i, 0))
    rb_im = (lambda i: (nrb - 1 - i, 1)) if reverse else (lambda i: (i, 1))
    blk_r = pl.BlockSpec((tb, N_STATE), rb)
    blk_i = pl.BlockSpec((tb, N_STATE), rb_im)
    one = pl.BlockSpec((1, N_STATE), lambda i: (0, 0))
    acc = pl.BlockSpec((SUBLANES, N_STATE), lambda i: (0, 0))
    ins = [x, x] + ([states, states] if with_da else []) + [lam_r, lam_i]
    in_specs = [blk_r, blk_i] + ([blk_r, blk_i] if with_da else []) + [one, one]
    out_specs = [pl.BlockSpec((tb, 2 * N_STATE), rb)] + ([acc, acc] if with_da else [])
    out_shape = [jax.ShapeDtypeStruct((L, 2 * N_STATE), F32)] + (
        [jax.ShapeDtypeStruct((SUBLANES, N_STATE), F32)] * 2 if with_da else [])
    outs = pl.pallas_call(
        body, name=name, grid=(nrb,), in_specs=in_specs, out_specs=out_specs, out_shape=out_shape,
        scratch_shapes=[pltpu.VMEM((SUBLANES, N_STATE), F32)] * 2,
        compiler_params=_cparams((6 if with_da else 4) * _nbytes((tb, N_STATE), F32),
                                 dimension_semantics=("arbitrary",)),
    )(*ins)
    return outs if with_da else outs[0]


_GROUPS = ((0, 256), (256, 768), (768, 1024))


def _merge_fwd(ya, yb, g12, mixg, name):
    L = ya.shape[0]
    tl = _tile(L, _TL)

    def body(a_ref, b_ref, g_ref, m_ref, o_ref):
        g12v = g_ref[...]
        yc = g12v[:, :C_WIDTH] * _sigmoid(g12v[:, C_WIDTH:])
        for (lo, hi), y in zip(_GROUPS, (a_ref[...], b_ref[...], yc)):
            r = lax.rsqrt(jnp.mean(y * y, axis=-1, keepdims=True) + EPS)
            o_ref[:, lo:hi] = ((y * r) * m_ref[:, lo:hi]).astype(o_ref.dtype)

    row = lambda w: pl.BlockSpec((tl, w), lambda i: (i, 0))
    return pl.pallas_call(
        body, name=name, grid=(L // tl,),
        in_specs=[row(256), row(512), row(512), pl.BlockSpec((1, D_MODEL), lambda i: (0, 0))],
        out_specs=row(D_MODEL), out_shape=jax.ShapeDtypeStruct((L, D_MODEL), _ACT),
    )(ya, yb, g12, mixg.reshape(1, D_MODEL))


def _merge_bwd(dy, ya, yb, g12, mixg, name):
    L = ya.shape[0]
    tl = _tile(L, _TL)

    def body(d_ref, a_ref, b_ref, g_ref, m_ref, da_ref, db_ref, dg_ref, dm_ref):
        @pl.when(pl.program_id(0) == 0)
        def _():
            dm_ref[...] = jnp.zeros_like(dm_ref)

        g12v = g_ref[...]
        g1, sg = g12v[:, :C_WIDTH], _sigmoid(g12v[:, C_WIDTH:])
        yc = g1 * sg
        outs = []
        for (lo, hi), y in zip(_GROUPS, (a_ref[...], b_ref[...], yc)):
            r = lax.rsqrt(jnp.mean(y * y, axis=-1, keepdims=True) + EPS)
            xhat = y * r
            d = d_ref[:, lo:hi]
            gy = d * m_ref[:, lo:hi]
            outs.append(r * (gy - xhat * jnp.mean(gy * xhat, axis=-1, keepdims=True)))
            dm_ref[:, lo:hi] += jnp.sum(d * xhat, axis=0, keepdims=True)
        da_ref[...] = outs[0]
        db_ref[...] = outs[1]
        dyc = outs[2]
        dg_ref[:, :C_WIDTH] = (dyc * sg).astype(dg_ref.dtype)
        dg_ref[:, C_WIDTH:] = (dyc * g1 * sg * (1.0 - sg)).astype(dg_ref.dtype)

    row = lambda w: pl.BlockSpec((tl, w), lambda i: (i, 0))
    one = pl.BlockSpec((1, D_MODEL), lambda i: (0, 0))
    return pl.pallas_call(
        body, name=name, grid=(L // tl,),
        in_specs=[row(D_MODEL), row(256), row(512), row(512), one],
        out_specs=[row(256), row(512), row(512), one],
        out_shape=[jax.ShapeDtypeStruct((L, 256), F32), jax.ShapeDtypeStruct((L, 512), F32),
                   jax.ShapeDtypeStruct((L, 512), _ACT), jax.ShapeDtypeStruct((1, D_MODEL), F32)],
    )(dy, ya, yb, g12, mixg.reshape(1, D_MODEL))


def _ple_bwd_elem(dh, gate, e, name):
    L, D = dh.shape
    tl = _tile(L, _TL)

    def body(d_ref, g_ref, e_ref, p_ref, o_ref):
        d, g = d_ref[...], g_ref[...]
        p_ref[...] = (d * e_ref[...] * g * (1.0 - g)).astype(p_ref.dtype)
        o_ref[...] = (d * g).astype(o_ref.dtype)

    row = pl.BlockSpec((tl, D), lambda i: (i, 0))
    return pl.pallas_call(
        body, name=name, grid=(L // tl,), in_specs=[row] * 3, out_specs=[row] * 2,
        out_shape=[jax.ShapeDtypeStruct((L, D), _ACT)] * 2,
        compiler_params=_cparams(4 * _nbytes((tl, D), F32)),
    )(dh, gate, e)


def _dskip_bwd(dy, z, name):
    L = dy.shape[0]
    tl = _tile(L, _TL)

    def body(d_ref, u_ref, o_ref):
        @pl.when(pl.program_id(0) == 0)
        def _():
            o_ref[...] = jnp.zeros_like(o_ref)

        o_ref[...] += jnp.sum(d_ref[...] * u_ref[...], axis=0, keepdims=True)

    return pl.pallas_call(
        body, name=name, grid=(L // tl,),
        in_specs=[pl.BlockSpec((tl, C_WIDTH), lambda i: (i, 0)), pl.BlockSpec((tl, C_WIDTH), lambda i: (i, 5))],
        out_specs=pl.BlockSpec((1, C_WIDTH), lambda i: (0, 0)),
        out_shape=jax.ShapeDtypeStruct((1, C_WIDTH), F32),
    )(dy, z)


def _loss_fwd_bwd(y, target):
    L, D = y.shape
    tl = _tile(L, _TL)

    def body(y_ref, t_ref, l_ref, d_ref):
        @pl.when(pl.program_id(0) == 0)
        def _():
            l_ref[...] = jnp.zeros_like(l_ref)

        e = y_ref[...] - t_ref[...]
        d_ref[...] = e * (1.0 / D)
        part = jnp.sum(jnp.sum(e * e, axis=-1, keepdims=True), axis=0, keepdims=True)
        l_ref[...] += jnp.broadcast_to(part, l_ref.shape)

    row = pl.BlockSpec((tl, D), lambda i: (i, 0))
    return pl.pallas_call(
        body, name="loss", grid=(L // tl,), in_specs=[row, row],
        out_specs=[pl.BlockSpec((SUBLANES, LANES), lambda i: (0, 0)), row],
        out_shape=[jax.ShapeDtypeStruct((SUBLANES, LANES), F32), jax.ShapeDtypeStruct((L, D), F32)],
    )(y, target)


def _adamw(w, g, m, v, name):
    R, C = w.shape
    tr = R if R <= 512 else _tile_rows(R, 512)

    def body(w_ref, g_ref, m_ref, v_ref, d_ref, nm_ref, nv_ref):
        gv = g_ref[...]
        nm = ADAM_B1 * m_ref[...] + (1.0 - ADAM_B1) * gv
        nv = ADAM_B2 * v_ref[...] + (1.0 - ADAM_B2) * (gv * gv)
        m_hat = nm / (1.0 - ADAM_B1 ** ADAM_STEP)
        v_hat = nv / (1.0 - ADAM_B2 ** ADAM_STEP)
        d_ref[...] = -ADAM_LR * (m_hat / (jnp.sqrt(v_hat) + ADAM_EPS) + ADAM_WD * w_ref[...])
        nm_ref[...] = nm
        nv_ref[...] = nv

    blk = pl.BlockSpec((tr, C), lambda i: (i, 0))
    return pl.pallas_call(
        body, name=name, grid=(R // tr,), in_specs=[blk] * 4, out_specs=[blk] * 3,
        out_shape=[jax.ShapeDtypeStruct((R, C), F32)] * 3,
        compiler_params=_cparams(7 * _nbytes((tr, C), F32)),
    )(w, g, m, v)


def _tile_rows(R, pref):
    t = pref
    while R % t:
        t -= SUBLANES
    assert t > 0
    return t


def _add_n(xs, name):
    R, C = xs[0].shape
    tr = R if R <= 512 else _tile_rows(R, 512)
    n = len(xs)

    def body(*refs):
        acc = refs[0][...].astype(F32)
        for r in refs[1:n]:
            acc = acc + r[...].astype(F32)
        refs[n][...] = acc

    blk = pl.BlockSpec((tr, C), lambda i: (i, 0))
    return pl.pallas_call(
        body, name=name, grid=(R // tr,), in_specs=[blk] * n, out_specs=blk,
        out_shape=jax.ShapeDtypeStruct((R, C), F32),
        compiler_params=_cparams((n + 1) * _nbytes((tr, C), F32)),
    )(*xs)


def _relu2(acc):
    r = jnp.maximum(acc, 0.0)
    return acc, r * r


def _layer_fwd(h, lp, cos, sin):
    L = h.shape[0]
    xn = _rms_fwd(h, lp["attn_norm_g"], "f_norm_attn")
    z = _mm(xn, lp["w_in"], mode="nn", M=L, N=IN_COLS, K=D_MODEL, b_cb=True, out_dtypes=[F32], name="f_w_in")
    ya = _gmlp_fwd(z, lp["ws"], lp["bfull"], lp["lgf"], lp["lbf"], "f_gmlp")
    q, k = _qk_prep(z, cos, sin, lp["gq"], lp["gk"], "f_qk_prep")
    yb = _attn_fwd(q, k, z, lp["sinks"], "f_attn")
    bu = _mm(z, lp["bcat"], mode="nn", M=L, N=2 * N_STATE, K=C_WIDTH, a_off=5, tk=C_WIDTH,
             out_dtypes=[F32], name="f_ssm_in")
    S = _ssm_scan(bu, lp["lam_r"], lp["lam_i"], "f_ssm_scan")
    y, yg = _mm(S, lp["ccat"], mode="nn", M=L, N=C_WIDTH, K=2 * N_STATE, tk=2 * N_STATE,
                extras=[(z, 5), (lp["dskip"], 0)], out_dtypes=[F32, _ACT], name="f_ssm_out",
                epi=lambda acc, u, dsk: (acc + dsk * u, _gelu(acc + dsk * u)))
    g12 = _mm(yg, lp["w12"], mode="nn", M=L, N=2 * C_WIDTH, K=C_WIDTH, out_dtypes=[F32], name="f_glu")
    ycat = _merge_fwd(ya, yb, g12, lp["mix_out_g"], "f_merge")
    h1 = _mm(ycat, lp["w_out"], mode="nn", M=L, N=D_MODEL, K=D_MODEL, extras=[(h, 0)],
             epi=lambda acc, r: (r + acc,), out_dtypes=[F32], name="f_w_out")
    hn = _rms_fwd(h1, lp["mlp_norm_g"], "f_norm_mlp")
    a, r = _mm(hn, lp["w_ff1"], mode="nn", M=L, N=D_FF, K=D_MODEL, b_cb=True, epi=_relu2,
               out_dtypes=[_ACT, _ACT], name="f_ff1")
    h2 = _mm(r, lp["w_ff2"], mode="nn", M=L, N=D_MODEL, K=D_FF, extras=[(h1, 0)],
             epi=lambda acc, r_: (r_ + acc,), out_dtypes=[F32], name="f_ff2")
    hn3 = _rms_fwd(h2, lp["ple_norm_g"], "f_norm_ple")
    e = _mm(lp["p"], lp["w_ple_proj"], mode="nn", M=L, N=D_MODEL, K=PLE_DIM, b_cb=True, tk=PLE_DIM,
            out_dtypes=[F32], name="f_ple_proj")

    def gate_epi(acc, h2_, e_):
        g = _sigmoid(acc)
        return h2_ + g * e_, g

    h3, gate = _mm(hn3, lp["w_ple_gate"], mode="nn", M=L, N=D_MODEL, K=D_MODEL, extras=[(h2, 0), (e, 0)],
                   epi=gate_epi, out_dtypes=[F32, F32], name="f_ple_gate")
    saved = dict(h=h, xn=xn, z=z, ya=ya, q=q, k=k, yb=yb, S=S, y=y, yg=yg, g12=g12, ycat=ycat, h1=h1, hn=hn,
                 a=a, r=r, h2=h2, hn3=hn3, e=e, gate=gate)
    return h3, saved


def _layer_bwd(dh3, lp, sv, cos, sin):
    L = dh3.shape[0]
    z = sv["z"]
    dpre, de = _ple_bwd_elem(dh3, sv["gate"], sv["e"], "b_ple_elem")
    d_gate = _mm(sv["hn3"], dpre, mode="tn", M=D_MODEL, N=D_MODEL, K=L, out_dtypes=[F32], name="b_dw_gate")
    d_proj = _mm(lp["p"], de, mode="tn", M=PLE_DIM, N=D_MODEL, K=L, o_cb=True, tm=PLE_DIM,
                 out_dtypes=[F32], name="b_dw_proj")
    dhn3 = _mm(dpre, lp["w_ple_gate"], mode="nt", M=L, N=D_MODEL, K=D_MODEL, out_dtypes=[F32], name="b_dx_gate")
    dh2, dg_ple = _rms_bwd(dhn3, sv["h2"], lp["ple_norm_g"], dh3, "b_norm_ple")
    da = _mm(dh2, lp["w_ff2"], mode="nt", M=L, N=D_FF, K=D_MODEL, extras=[(sv["a"], 0)],
             epi=lambda acc, a_: (acc * (2.0 * jnp.maximum(a_.astype(F32), 0.0)),), out_dtypes=[_ACT], name="b_dx_ff2")
    d_ff2 = _mm(sv["r"], dh2, mode="tn", M=D_FF, N=D_MODEL, K=L, out_dtypes=[F32], name="b_dw_ff2")
    d_ff1 = _mm(sv["hn"], da, mode="tn", M=D_MODEL, N=D_FF, K=L, o_cb=True, out_dtypes=[F32], name="b_dw_ff1")
    dhn = _mm(da, lp["w_ff1"], mode="nt", M=L, N=D_MODEL, K=D_FF, b_cb=True, out_dtypes=[F32], name="b_dx_ff1")
    dh1, dg_mlp = _rms_bwd(dhn, sv["h1"], lp["mlp_norm_g"], dh2, "b_norm_mlp")
    d_out = _mm(sv["ycat"], dh1, mode="tn", M=D_MODEL, N=D_MODEL, K=L, out_dtypes=[F32], name="b_dw_out")
    dycat = _mm(dh1, lp["w_out"], mode="nt", M=L, N=D_MODEL, K=D_MODEL, out_dtypes=[F32], name="b_dx_out")
    dya, dyb, dg12, dmix = _merge_bwd(dycat, sv["ya"], sv["yb"], sv["g12"], lp["mix_out_g"], "b_merge")
    d_w12 = _mm(sv["yg"], dg12, mode="tn", M=C_WIDTH, N=2 * C_WIDTH, K=L, tm=C_WIDTH, out_dtypes=[F32], name="b_dw_glu")
    dy = _mm(dg12, lp["w12"], mode="nt", M=L, N=C_WIDTH, K=2 * C_WIDTH, tk=2 * C_WIDTH, extras=[(sv["y"], 0)],
             epi=lambda acc, y_: (acc * _gelu_grad(y_),), out_dtypes=[F32], name="b_dx_glu")
    dd = _dskip_bwd(dy, z, "b_dskip")
    dS = _mm(dy, lp["ccat"], mode="nt", M=L, N=2 * N_STATE, K=C_WIDTH, tk=C_WIDTH, out_dtypes=[F32], name="b_dx_ssm_out")
    d_ccat = _mm(sv["S"], dy, mode="tn", M=2 * N_STATE, N=C_WIDTH, K=L, out_dtypes=[F32], name="b_dw_ssm_out")
    G, dar, dai = _ssm_scan(dS, lp["lam_r"], lp["lam_i"], "b_ssm_scan", reverse=True, states=sv["S"])
    d_bcat = _mm(z, G, mode="tn", M=C_WIDTH, N=2 * N_STATE, K=L, a_off=5, tm=C_WIDTH, out_dtypes=[F32], name="b_dw_ssm_in")
    dzc = _mm(G, lp["bcat"], mode="nt", M=L, N=C_WIDTH, K=2 * N_STATE, tk=2 * N_STATE,
              extras=[(dy, 0), (lp["dskip"], 0)], epi=lambda acc, dy_, dsk: (acc + dy_ * dsk,),
              out_dtypes=[_ACT], name="b_dx_ssm_in")
    dq, dkc, dkp, dvc, dvp, dsink = _attn_bwd(sv["q"], sv["k"], z, lp["sinks"], dyb, "b_attn")
    dzq, dzk, dzv, dgq, dgk = _qk_prep_bwd(z, dq, dkc, dkp, dvc, dvp, cos, sin, lp["gq"], lp["gk"], "b_qk_prep")
    dza, dws, dbs, dlg, dlb = _gmlp_bwd(z, dya, lp["ws"], lp["wsT"], lp["bfull"], lp["lgf"], lp["lbf"], "b_gmlp")
    dz = jnp.concatenate([dza, dzq, dzk, dzv, dzc], axis=1)
    d_in = _mm(sv["xn"], dz, mode="tn", M=D_MODEL, N=IN_COLS, K=L, o_cb=True, out_dtypes=[F32], name="b_dw_in")
    dxn = _mm(dz, lp["w_in"], mode="nt", M=L, N=D_MODEL, K=IN_COLS, b_cb=True, out_dtypes=[F32], name="b_dx_in")
    dh, dg_attn = _rms_bwd(dxn, sv["h"], lp["attn_norm_g"], dh1, "b_norm_attn")
    grads = dict(w_in=d_in, w12=d_w12, w_out=d_out, w_ff1=d_ff1, w_ff2=d_ff2, w_ple_gate=d_gate, w_ple_proj=d_proj,
                 attn_norm_g=dg_attn, mlp_norm_g=dg_mlp, ple_norm_g=dg_ple, mix_out_g=dmix.reshape(D_MODEL),
                 dws=dws, dbs=dbs, dlg=dlg, dlb=dlb, dgq=dgq, dgk=dgk, dsink=dsink,
                 dar=dar, dai=dai, d_bcat=d_bcat, d_ccat=d_ccat, dd=dd)
    return dh, grads


SMALL = ("attn_norm_g", "gmlp_ln_g", "gmlp_ln_b", "gmlp_ws", "gmlp_bs", "q_norm_g", "k_norm_g", "sinks",
         "ssm_a_re", "ssm_a_im", "ssm_log_dt", "ssm_b_re", "ssm_b_im", "ssm_c_re", "ssm_c_im", "ssm_d",
         "mix_out_g", "mlp_norm_g", "ple_norm_g")
BIG = ("w_in", "w12", "w_out", "w_ff1", "w_ff2", "w_ple_gate", "w_ple_proj")
COL_SHARDED = ("w_in", "w_ff1", "w_ple_proj")


def _block_diag(t):
    nl, g, a, b = t.shape
    eye = jnp.eye(g, dtype=t.dtype)
    return (t[:, :, :, None, :] * eye[None, :, None, :, None]).reshape(nl, g * a, g * b)


def _diag_blocks(t, a, b):
    nl = t.shape[0]
    t = t.reshape(nl, C_GROUPS, a, C_GROUPS, b)
    idx = jnp.arange(C_GROUPS)
    return jnp.moveaxis(t[:, idx, :, idx, :], 0, 1)


def _local_step(x, p, positions, target, sw, bw):
    nl = sw["attn_norm_g"].shape[0]
    G = nl * C_GROUPS
    zeros = lambda *s: jnp.zeros(s, F32)
    are = sw["ssm_a_re"].reshape(G, 1, C_STATE)
    aim = sw["ssm_a_im"].reshape(G, 1, C_STATE)
    ldt = jnp.broadcast_to(sw["ssm_log_dt"][..., None], (nl, C_GROUPS, C_STATE)).reshape(G, 1, C_STATE)
    bre = jnp.swapaxes(sw["ssm_b_re"], -1, -2).reshape(G, C_GROUP, C_STATE)
    bim = jnp.swapaxes(sw["ssm_b_im"], -1, -2).reshape(G, C_GROUP, C_STATE)
    lr, li, bbr, bbi = _ssm_prep(are, aim, ldt, bre, bim)
    unflat = lambda t: t.reshape(nl, C_GROUPS, C_GROUP, C_STATE)
    lp = dict(
        attn_norm_g=sw["attn_norm_g"], mlp_norm_g=sw["mlp_norm_g"], ple_norm_g=sw["ple_norm_g"],
        mix_out_g=sw["mix_out_g"], sinks=sw["sinks"],
        ws=sw["gmlp_ws"], wsT=jnp.swapaxes(sw["gmlp_ws"], -1, -2),
        bfull=jnp.concatenate([zeros(nl, A_HEADS, CHUNK, HEAD_DIM),
                               jnp.broadcast_to(sw["gmlp_bs"][..., None], (nl, A_HEADS, CHUNK, HEAD_DIM))], axis=-1),
        lgf=jnp.concatenate([zeros(nl, A_HEADS, HEAD_DIM), sw["gmlp_ln_g"]], axis=-1),
        lbf=jnp.concatenate([zeros(nl, A_HEADS, HEAD_DIM), sw["gmlp_ln_b"]], axis=-1),
        gq=jnp.tile(sw["q_norm_g"], (1, 2)).reshape(nl, 1, LANES),
        gk=jnp.tile(sw["k_norm_g"], (1, 2)).reshape(nl, 1, LANES),
        lam_r=lr.reshape(nl, 1, N_STATE), lam_i=li.reshape(nl, 1, N_STATE),
        bcat=jnp.concatenate([_block_diag(unflat(bbr)), _block_diag(unflat(bbi))], axis=-1),
        ccat=jnp.concatenate([_block_diag(jnp.swapaxes(sw["ssm_c_re"], -1, -2)),
                              -_block_diag(jnp.swapaxes(sw["ssm_c_im"], -1, -2))], axis=1),
        dskip=sw["ssm_d"].reshape(nl, 1, C_WIDTH),
        p=p, **bw)
    cos, sin = _rope_tables(positions)

    def fwd_body(h, lpi):
        return _layer_fwd(h, lpi, cos, sin)

    y, saved = lax.scan(fwd_body, x, lp)
    sse, dy = _loss_fwd_bwd(y, target)

    def bwd_body(dh, xs):
        lpi, svi = xs
        return _layer_bwd(dh, lpi, svi, cos, sin)

    grad_x, g = lax.scan(bwd_body, dy, (lp, saved), reverse=True)

    d_bcat = g["d_bcat"]
    dbr = _diag_blocks(d_bcat[:, :, :N_STATE], C_GROUP, C_STATE).reshape(G, C_GROUP, C_STATE)
    dbi = _diag_blocks(d_bcat[:, :, N_STATE:], C_GROUP, C_STATE).reshape(G, C_GROUP, C_STATE)
    dlr = g["dar"][:, 0].reshape(G, 1, C_STATE)
    dli = g["dai"][:, 0].reshape(G, 1, C_STATE)
    g_are, g_aim, g_ldt, g_bre, g_bim = _ssm_prep_bwd(are, aim, ldt, bre, bim, dlr, dli, dbr, dbi)
    d_ccat = g["d_ccat"]
    sg = dict(
        attn_norm_g=g["attn_norm_g"], mlp_norm_g=g["mlp_norm_g"], ple_norm_g=g["ple_norm_g"], mix_out_g=g["mix_out_g"],
        gmlp_ln_g=g["dlg"][:, :, 0, HEAD_DIM:], gmlp_ln_b=g["dlb"][:, :, 0, HEAD_DIM:],
        gmlp_ws=g["dws"], gmlp_bs=g["dbs"][:, :, :, HEAD_DIM],
        q_norm_g=g["dgq"][:, 0, :HEAD_DIM], k_norm_g=g["dgk"][:, 0, :HEAD_DIM],
        sinks=g["dsink"][:, 0, :B_Q_HEADS],
        ssm_a_re=g_are.reshape(nl, C_GROUPS, C_STATE), ssm_a_im=g_aim.reshape(nl, C_GROUPS, C_STATE),
        ssm_log_dt=g_ldt[:, 0, 0].reshape(nl, C_GROUPS),
        ssm_b_re=jnp.swapaxes(g_bre.reshape(nl, C_GROUPS, C_GROUP, C_STATE), -1, -2),
        ssm_b_im=jnp.swapaxes(g_bim.reshape(nl, C_GROUPS, C_GROUP, C_STATE), -1, -2),
        ssm_c_re=jnp.swapaxes(_diag_blocks(d_ccat[:, :N_STATE], C_STATE, C_GROUP), -1, -2),
        ssm_c_im=-jnp.swapaxes(_diag_blocks(d_ccat[:, N_STATE:], C_STATE, C_GROUP), -1, -2),
        ssm_d=g["dd"].reshape(nl, C_GROUPS, C_GROUP),
    )
    bg = {n: g[n] for n in BIG}
    return sse[0, 0], grad_x, sg, bg


_ANY = pl.BlockSpec(memory_space=pl.ANY)
N_LAYERS = 4


def _mesh_pos():
    x, y, c = lax.axis_index("x"), lax.axis_index("y"), lax.axis_index("c")
    chips = [(1 - x, y), (x, 1 - y), (1 - x, 1 - y)]
    return x, y, c, 2 * x + y, chips


def _gather_weights(shards):
    nk = len(shards)

    def body(*refs):
        ins, outs = refs[:nk], refs[nk:2 * nk]
        send_sems, recv_sems, loc_sems = refs[2 * nk:]
        x, y, c, j, chips = _mesh_pos()
        mine, other = pl.ds(2 * c, 2), pl.ds(2 * (1 - c), 2)

        def ici(t, q):
            cx, cy = chips[q]
            return pltpu.make_async_remote_copy(
                src_ref=ins[t].at[mine], dst_ref=outs[t].at[mine, j],
                send_sem=send_sems.at[6 * t + q], recv_sem=recv_sems.at[6 * t + q],
                device_id=(cx, cy, c), device_id_type=MESH)

        def landed(t, q):
            cx, cy = chips[q]
            blk = outs[t].at[mine, 2 * cx + cy]
            return pltpu.make_async_remote_copy(
                src_ref=blk, dst_ref=blk, send_sem=send_sems.at[6 * t + q], recv_sem=recv_sems.at[6 * t + q],
                device_id=(cx, cy, c), device_id_type=MESH)

        def fwd(t, q, rows):
            cx, cy = chips[q]
            blk = outs[t].at[rows, 2 * cx + cy]
            return pltpu.make_async_remote_copy(
                src_ref=blk, dst_ref=blk, send_sem=send_sems.at[6 * t + 3 + q], recv_sem=recv_sems.at[6 * t + 3 + q],
                device_id=(x, y, 1 - c), device_id_type=MESH)

        local = [pltpu.make_async_copy(ins[t], outs[t].at[:, j], loc_sems.at[t]) for t in range(nk)]
        for cp in local:
            cp.start()
        for t in range(nk):
            for q in range(3):
                ici(t, q).start()
        for t in range(nk):
            for q in range(3):
                landed(t, q).wait_recv()
                fwd(t, q, mine).start()
        for t in range(nk):
            for q in range(3):
                fwd(t, q, other).wait_recv()
        for t in range(nk):
            for q in range(3):
                ici(t, q).wait_send()
                fwd(t, q, mine).wait_send()
        for cp in local:
            cp.wait()

    return pl.pallas_call(
        body, name="gather_weights", in_specs=[_ANY] * nk, out_specs=[_ANY] * nk,
        out_shape=[jax.ShapeDtypeStruct((s.shape[0], N_CHIPS) + s.shape[1:], s.dtype) for s in shards],
        scratch_shapes=[pltpu.SemaphoreType.DMA((6 * nk,)), pltpu.SemaphoreType.DMA((6 * nk,)),
                        pltpu.SemaphoreType.DMA((nk,))],
    )(*shards)


def _exchange_sibling_half(gl):
    nk = len(gl)

    def body(*refs):
        ins, outs = refs[:nk], refs[nk:2 * nk]
        send_sems, recv_sems = refs[2 * nk:]
        x, y, c, _, _ = _mesh_pos()
        cps = [pltpu.make_async_remote_copy(
            src_ref=ins[t].at[pl.ds(2 * (1 - c), 2)], dst_ref=outs[t],
            send_sem=send_sems.at[t], recv_sem=recv_sems.at[t],
            device_id=(x, y, 1 - c), device_id_type=MESH) for t in range(nk)]
        for cp in cps:
            cp.start()
        for cp in cps:
            cp.wait()

    return pl.pallas_call(
        body, name="reduce_sibling", in_specs=[_ANY] * nk, out_specs=[_ANY] * nk,
        out_shape=[jax.ShapeDtypeStruct((2,) + g.shape[1:], g.dtype) for g in gl],
        scratch_shapes=[pltpu.SemaphoreType.DMA((nk,)), pltpu.SemaphoreType.DMA((nk,))],
    )(*gl)


def _exchange_chips(ps):
    nk = len(ps)

    def body(*refs):
        ins, outs = refs[:nk], refs[nk:2 * nk]
        send_sems, recv_sems, loc_sems = refs[2 * nk:]
        x, y, c, j, chips = _mesh_pos()

        def send(t, q):
            cx, cy = chips[q]
            return pltpu.make_async_remote_copy(
                src_ref=ins[t].at[:, 2 * cx + cy], dst_ref=outs[t].at[j],
                send_sem=send_sems.at[3 * t + q], recv_sem=recv_sems.at[3 * t + q],
                device_id=(cx, cy, c), device_id_type=MESH)

        def landed(t, q):
            cx, cy = chips[q]
            blk = outs[t].at[2 * cx + cy]
            return pltpu.make_async_remote_copy(
                src_ref=blk, dst_ref=blk, send_sem=send_sems.at[3 * t + q], recv_sem=recv_sems.at[3 * t + q],
                device_id=(cx, cy, c), device_id_type=MESH)

        local = [pltpu.make_async_copy(ins[t].at[:, j], outs[t].at[j], loc_sems.at[t]) for t in range(nk)]
        for cp in local:
            cp.start()
        for t in range(nk):
            for q in range(3):
                send(t, q).start()
        for t in range(nk):
            for q in range(3):
                landed(t, q).wait_recv()
        for t in range(nk):
            for q in range(3):
                send(t, q).wait_send()
        for cp in local:
            cp.wait()

    return pl.pallas_call(
        body, name="reduce_chips", in_specs=[_ANY] * nk, out_specs=[_ANY] * nk,
        out_shape=[jax.ShapeDtypeStruct((N_CHIPS, 2) + p.shape[2:], p.dtype) for p in ps],
        scratch_shapes=[pltpu.SemaphoreType.DMA((3 * nk,)), pltpu.SemaphoreType.DMA((3 * nk,)),
                        pltpu.SemaphoreType.DMA((nk,))],
    )(*ps)


def _share_sibling(qs):
    nk = len(qs)

    def body(*refs):
        ins, outs = refs[:nk], refs[nk:2 * nk]
        send_sems, recv_sems, loc_sems = refs[2 * nk:]
        x, y, c, _, _ = _mesh_pos()
        mine = pl.ds(2 * c, 2)
        local = [pltpu.make_async_copy(ins[t], outs[t].at[mine], loc_sems.at[t]) for t in range(nk)]
        cps = [pltpu.make_async_remote_copy(
            src_ref=ins[t], dst_ref=outs[t].at[mine], send_sem=send_sems.at[t], recv_sem=recv_sems.at[t],
            device_id=(x, y, 1 - c), device_id_type=MESH) for t in range(nk)]
        for cp in local + cps:
            cp.start()
        for cp in cps:
            cp.wait()
        for cp in local:
            cp.wait()

    return pl.pallas_call(
        body, name="share_sibling", in_specs=[_ANY] * nk, out_specs=[_ANY] * nk,
        out_shape=[jax.ShapeDtypeStruct((N_LAYERS,) + q.shape[1:], q.dtype) for q in qs],
        scratch_shapes=[pltpu.SemaphoreType.DMA((nk,)), pltpu.SemaphoreType.DMA((nk,)), pltpu.SemaphoreType.DMA((nk,))],
    )(*qs)


def _add_own_half(gl, r1, c, name):
    _, ns, R, C = gl.shape
    rows = 2 * ns * R
    tr = _tile_rows(rows, 512)
    nblk = rows // tr

    def body(s_ref, a_ref, b_ref, o_ref):
        o_ref[...] = a_ref[...] + b_ref[...]

    out = pl.pallas_call(
        body, name=name,
        grid_spec=pltpu.PrefetchScalarGridSpec(
            num_scalar_prefetch=1, grid=(nblk,),
            in_specs=[pl.BlockSpec((tr, C), lambda i, s: (s[0] * nblk + i, 0)), pl.BlockSpec((tr, C), lambda i, s: (i, 0))],
            out_specs=pl.BlockSpec((tr, C), lambda i, s: (i, 0))),
        out_shape=jax.ShapeDtypeStruct((rows, C), F32),
        compiler_params=_cparams(3 * _nbytes((tr, C), F32)),
    )(jnp.reshape(c, (1,)).astype(jnp.int32), gl.reshape(2 * rows, C), r1.reshape(rows, C))
    return out.reshape(2, ns, R, C)


def _add_chips(r2, name):
    ns, two, R, C = r2.shape
    rows = two * R
    tr = _tile_rows(rows, 512)

    def body(a0, a1, a2, a3, o_ref):
        o_ref[...] = ((a0[...] + a1[...]) + a2[...]) + a3[...]

    out = pl.pallas_call(
        body, name=name, grid=(rows // tr,),
        in_specs=[pl.BlockSpec((None, tr, C), lambda i, s=s: (s, i, 0)) for s in range(N_CHIPS)],
        out_specs=pl.BlockSpec((tr, C), lambda i: (i, 0)),
        out_shape=jax.ShapeDtypeStruct((rows, C), F32),
        compiler_params=_cparams(5 * _nbytes((tr, C), F32)),
    )(*([r2.reshape(ns, rows, C)] * N_CHIPS))
    return out.reshape(two, R, C)


def _allreduce_small(buf):
    Rs = buf.shape[0]

    def body(b_ref, o_ref, t_ref, slots_ref, send_sems, recv_sems):
        x, y, c, j, chips = _mesh_pos()
        sib = pltpu.make_async_remote_copy(
            src_ref=b_ref, dst_ref=t_ref, send_sem=send_sems.at[0], recv_sem=recv_sems.at[0],
            device_id=(x, y, 1 - c), device_id_type=MESH)
        sib.start()
        sib.wait()
        slots_ref[j] = b_ref[...] + t_ref[...]

        def send(q):
            cx, cy = chips[q]
            return pltpu.make_async_remote_copy(
                src_ref=slots_ref.at[j], dst_ref=slots_ref.at[j], send_sem=send_sems.at[1 + q],
                recv_sem=recv_sems.at[1 + q], device_id=(cx, cy, c), device_id_type=MESH)

        def landed(q):
            cx, cy = chips[q]
            blk = slots_ref.at[2 * cx + cy]
            return pltpu.make_async_remote_copy(
                src_ref=blk, dst_ref=blk, send_sem=send_sems.at[1 + q], recv_sem=recv_sems.at[1 + q],
                device_id=(cx, cy, c), device_id_type=MESH)

        for q in range(3):
            send(q).start()
        for q in range(3):
            landed(q).wait_recv()
        for q in range(3):
            send(q).wait_send()
        o_ref[...] = ((slots_ref[0] + slots_ref[1]) + slots_ref[2]) + slots_ref[3]

    vm = pl.BlockSpec(memory_space=pltpu.VMEM)
    return pl.pallas_call(
        body, name="allreduce_small", in_specs=[vm], out_specs=vm,
        out_shape=jax.ShapeDtypeStruct((Rs, LANES), F32),
        scratch_shapes=[pltpu.VMEM((Rs, LANES), F32), pltpu.VMEM((N_CHIPS, Rs, LANES), F32),
                        pltpu.SemaphoreType.DMA((4,)), pltpu.SemaphoreType.DMA((4,))],
        compiler_params=_cparams(4 * _nbytes((Rs, LANES), F32)),
    )(buf)


def _pack(d):
    flat = jnp.concatenate([d[n].reshape(-1) for n in SMALL])
    rows = -(-flat.shape[0] // (SUBLANES * LANES)) * SUBLANES
    return jnp.pad(flat, (0, rows * LANES - flat.shape[0])).reshape(rows, LANES)


def _unpack(buf, like):
    flat = buf.reshape(-1)
    out, off = {}, 0
    for n in SMALL:
        size = int(np.prod(like[n].shape))
        out[n] = flat[off:off + size].reshape(like[n].shape)
        off += size
    return out


ARGS = ("x", "p", "positions", "attn_norm_g", "w_in", "gmlp_ln_g", "gmlp_ln_b", "gmlp_ws", "gmlp_bs", "q_norm_g",
        "k_norm_g", "sinks", "ssm_a_re", "ssm_a_im", "ssm_log_dt", "ssm_b_re", "ssm_b_im", "ssm_c_re", "ssm_c_im",
        "ssm_d", "glu_w1", "glu_w2", "mix_out_g", "w_out", "mlp_norm_g", "w_ff1", "w_ff2", "ple_norm_g", "w_ple_gate",
        "w_ple_proj")
WEIGHTS = ARGS[3:]


def kernel(x, p, positions, attn_norm_g, w_in, gmlp_ln_g, gmlp_ln_b, gmlp_ws, gmlp_bs, q_norm_g, k_norm_g, sinks, ssm_a_re, ssm_a_im, ssm_log_dt, ssm_b_re, ssm_b_im, ssm_c_re, ssm_c_im, ssm_d, glu_w1, glu_w2, mix_out_g, w_out, mlp_norm_g, w_ff1, w_ff2, ple_norm_g, w_ple_gate, w_ple_proj, loss_target, m_attn_norm_g, m_w_in, m_gmlp_ln_g, m_gmlp_ln_b, m_gmlp_ws, m_gmlp_bs, m_q_norm_g, m_k_norm_g, m_sinks, m_ssm_a_re, m_ssm_a_im, m_ssm_log_dt, m_ssm_b_re, m_ssm_b_im, m_ssm_c_re, m_ssm_c_im, m_ssm_d, m_glu_w1, m_glu_w2, m_mix_out_g, m_w_out, m_mlp_norm_g, m_w_ff1, m_w_ff2, m_ple_norm_g, m_w_ple_gate, m_w_ple_proj, v_attn_norm_g, v_w_in, v_gmlp_ln_g, v_gmlp_ln_b, v_gmlp_ws, v_gmlp_bs, v_q_norm_g, v_k_norm_g, v_sinks, v_ssm_a_re, v_ssm_a_im, v_ssm_log_dt, v_ssm_b_re, v_ssm_b_im, v_ssm_c_re, v_ssm_c_im, v_ssm_d, v_glu_w1, v_glu_w2, v_mix_out_g, v_w_out, v_mlp_norm_g, v_w_ff1, v_w_ff2, v_ple_norm_g, v_w_ple_gate, v_w_ple_proj):
    a = dict(locals())
    L = a["x"].shape[1]
    nl = N_LAYERS
    c = lax.axis_index("c")

    shards = dict(w_in=a["w_in"], w12=jnp.concatenate([a["glu_w1"], a["glu_w2"]], axis=-1), w_out=a["w_out"],
                  w_ff1=a["w_ff1"], w_ff2=a["w_ff2"], w_ple_gate=a["w_ple_gate"], w_ple_proj=a["w_ple_proj"])
    gathered = dict(zip(BIG, _gather_weights([shards[n].astype(_MXU) for n in BIG])))
    bw = {n: (g if n in COL_SHARDED else g.reshape(nl, N_CHIPS * g.shape[2], g.shape[3])) for n, g in gathered.items()}

    sw = {n: a[n] for n in SMALL}
    sse, gx, sg, bg = _local_step(a["x"].reshape(L, D_MODEL), a["p"].reshape(nl, L, PLE_DIM),
                                  a["positions"].reshape(L), a["loss_target"].reshape(L, D_MODEL), sw, bw)
    loss = lax.psum(sse * (0.5 / D_MODEL), ("x", "y", "c"))

    gl = [bg[n] if n in COL_SHARDED else bg[n].reshape(nl, N_CHIPS, bg[n].shape[1] // N_CHIPS, bg[n].shape[2])
          for n in BIG]
    r1 = _exchange_sibling_half(gl)
    ps = [_add_own_half(g, r, c, "reduce_add_sibling_" + n) for g, r, n in zip(gl, r1, BIG)]
    r2 = _exchange_chips(ps)
    qs = [_add_chips(r, "reduce_add_chips_" + n) for r, n in zip(r2, BIG)]
    big_grads = dict(zip(BIG, _share_sibling(qs)))
    g12 = big_grads.pop("w12")
    big_grads["glu_w1"], big_grads["glu_w2"] = g12[:, :, :C_WIDTH], g12[:, :, C_WIDTH:]

    small_grads = _unpack(_allreduce_small(_pack(sg)), sw)

    grads, delta, new_m, new_v = {}, {}, {}, {}
    d_s, m_s, v_s = _adamw(_pack(sw), _pack(small_grads), _pack({n: a["m_" + n] for n in SMALL}),
                           _pack({n: a["v_" + n] for n in SMALL}), "adamw_small")
    grads.update(small_grads)
    delta.update(_unpack(d_s, sw))
    new_m.update(_unpack(m_s, sw))
    new_v.update(_unpack(v_s, sw))
    for n, g in big_grads.items():
        shp = a[n].shape
        two_d = lambda t: t.reshape(shp[0] * shp[1], shp[2])
        d, m, v = _adamw(two_d(a[n]), two_d(g), two_d(a["m_" + n]), two_d(a["v_" + n]), "adamw_" + n)
        grads[n], delta[n], new_m[n], new_v[n] = g, d.reshape(shp), m.reshape(shp), v.reshape(shp)

    return (loss, gx.reshape(1, L, D_MODEL), *[grads[n] for n in WEIGHTS], *[delta[n] for n in WEIGHTS],
            *[new_m[n] for n in WEIGHTS], *[new_v[n] for n in WEIGHTS])
```

```python
import functools
import math

import numpy as np
import jax
import jax.numpy as jnp
from jax import lax
from jax.experimental import pallas as pl
from jax.experimental.pallas import tpu as pltpu

F32 = jnp.float32
_MXU = jnp.bfloat16
_ACT = jnp.bfloat16
_WIRE = jnp.bfloat16

D_MODEL = 1024
HEAD_DIM = 64
A_HEADS = 4
CHUNK = 128
B_Q_HEADS = 8
B_KV_HEADS = 2
B_GROUP = 4
WINDOW = 128
ROPE_THETA = 10000.0
C_WIDTH = 256
C_GROUP = 16
C_GROUPS = 16
C_STATE = 64
N_STATE = C_GROUPS * C_STATE
IN_A, IN_Q, IN_KV, IN_C = 512, 512, 128, 256
IN_COLS = 1536
D_FF = 4096
PLE_DIM = 256
EPS = 1e-6
NEG = -1e30
ADAM_LR, ADAM_B1, ADAM_B2, ADAM_EPS, ADAM_WD, ADAM_STEP = 0.001, 0.9, 0.999, 1e-08, 0.01, 10

LANES = 128
SUBLANES = 8
VMEM_BYTES = 64 * 2 ** 20
N_CHIPS = 4
MESH = pl.DeviceIdType.MESH


_MM_VMEM_BUDGET = 44 * 2 ** 20


def _vmem_limit(est_bytes):
    return int(min(max(2 * est_bytes + (8 << 20), 32 << 20), VMEM_BYTES - (6 << 20)))


def _cparams(est_bytes, **kw):
    return pltpu.CompilerParams(vmem_limit_bytes=_vmem_limit(est_bytes), **kw)


def _nbytes(shape, dtype):
    return int(np.prod(shape)) * jnp.dtype(dtype).itemsize


def _tile(dim, pref):
    t = min(dim, pref)
    while dim % t:
        t -= LANES
    assert t > 0, (dim, pref)
    return t


def _lane(shape):
    return lax.broadcasted_iota(jnp.int32, shape, len(shape) - 1)


def _row(shape):
    return lax.broadcasted_iota(jnp.int32, shape, len(shape) - 2)


def _gelu(x):
    c = math.sqrt(2.0 / math.pi)
    return 0.5 * x * (1.0 + jnp.tanh(c * (x + 0.044715 * (x * x * x))))


def _gelu_grad(x):
    c = math.sqrt(2.0 / math.pi)
    t = jnp.tanh(c * (x + 0.044715 * (x * x * x)))
    return 0.5 * (1.0 + t) + 0.5 * x * (1.0 - t * t) * (c * (1.0 + 3.0 * 0.044715 * (x * x)))


def _sigmoid(x):
    return 1.0 / (1.0 + jnp.exp(-x))


def _dot(a, b, dims=(((1,), (0,)), ((), ()))):
    return lax.dot_general(a.astype(_MXU), b.astype(_MXU), dims, preferred_element_type=F32)


_NT = (((1,), (1,)), ((), ()))
_TN = (((0,), (0,)), ((), ()))
_NN = (((1,), (0,)), ((), ()))


def _mm(a, b, *, mode, M, N, K, out_dtypes, name, epi=None, extras=(), b_cb=False, o_cb=False,
        a_off=0, b_off=0, tm=1024, tn=1024, tk=1024, a_lyr=None, b_lyr=None, o_stack=None):
    if isinstance(a, tuple):
        a, a_lyr = a
    if isinstance(b, tuple):
        b, b_lyr = b
    if b_cb or o_cb:
        nc = (b.shape[-1] if b_cb else N // N_CHIPS)
    tn_nom = nc if ((mode == "nn" and b_cb) or (mode == "tn" and o_cb)) else _tile(N, tn)
    tk_nom = nc if (mode == "nt" and b_cb) else _tile(K, tk)
    item = lambda d: jnp.dtype(d).itemsize
    per_row = tk_nom * item(a.dtype) + tn_nom * (sum(item(d) for d in out_dtypes)
                                                   + sum(item(e.dtype) for e, _ in extras if e.shape[0] > 1))
    fixed = tk_nom * tn_nom * item(b.dtype)
    tm = _tile(M, tm)
    while tm > 256 and M % (tm // 2) == 0 and 2 * (tm * per_row + fixed) + 8 * tm * tn_nom > _MM_VMEM_BUDGET:
        tm //= 2

    def spec(block, imap, lyr=None):
        if lyr is None:
            return pl.BlockSpec(block, imap)
        return pl.BlockSpec((None,) + block, lambda i, j, k: (lyr,) + imap(i, j, k))

    if mode == "nn":
        if b_cb:
            tn = nc
        tm, tn, tk = _tile(M, tm), _tile(N, tn), _tile(K, tk)
        a_spec = spec((tm, tk), lambda i, j, k: (i, k + a_off), a_lyr)
        if b_cb:
            b_spec = spec((None, tk, tn), lambda i, j, k: (j, k, 0), b_lyr)
        else:
            b_spec = spec((tk, tn), lambda i, j, k: (k, j + b_off), b_lyr)
        dims = _NN
        a_blk, b_blk = (tm, tk), (tk, tn)
    elif mode == "nt":
        if b_cb:
            tk = nc
        tm, tn, tk = _tile(M, tm), _tile(N, tn), _tile(K, tk)
        a_spec = spec((tm, tk), lambda i, j, k: (i, k + a_off), a_lyr)
        if b_cb:
            b_spec = spec((None, tn, tk), lambda i, j, k: (k, j, 0), b_lyr)
        else:
            b_spec = spec((tn, tk), lambda i, j, k: (j, k + b_off), b_lyr)
        dims = _NT
        a_blk, b_blk = (tm, tk), (tn, tk)
    else:
        if o_cb:
            tn = nc
        tm, tn, tk = _tile(M, tm), _tile(N, tn), _tile(K, tk)
        a_spec = spec((tk, tm), lambda i, j, k: (k, i + a_off), a_lyr)
        b_spec = spec((tk, tn), lambda i, j, k: (k, j + b_off), b_lyr)
        dims = _TN
        a_blk, b_blk = (tk, tm), (tk, tn)
    gi, gj, gk = M // tm, N // tn, K // tk
    o_lyr = None if o_stack is None else o_stack[1]
    if o_cb:
        o_spec = spec((None, tm, tn), lambda i, j, k: (j, i, 0), o_lyr)
        o_shape = (gj, M, tn)
    else:
        o_spec = spec((tm, tn), lambda i, j, k: (i, j), o_lyr)
        o_shape = (M, N)
    e_specs = []
    for e, off in extras:
        if e.shape[0] == 1:
            e_specs.append(pl.BlockSpec((1, tn), lambda i, j, k, off=off: (0, j + off)))
        else:
            e_specs.append(pl.BlockSpec((tm, tn), lambda i, j, k, off=off: (i, j + off)))
    extras = [e for e, _ in extras]
    ne, no = len(extras), len(out_dtypes)
    operands = [a, b, *extras]
    in_specs = [a_spec, b_spec] + e_specs
    out_shape = [jax.ShapeDtypeStruct(o_shape, d) for d in out_dtypes]
    aliases = {}
    if o_stack is not None:
        assert no == 1 and o_stack[0].shape[1:] == o_shape and o_stack[0].dtype == out_dtypes[0]
        operands.append(o_stack[0])
        in_specs.append(pl.BlockSpec(memory_space=pl.ANY))
        out_shape = [jax.ShapeDtypeStruct(o_stack[0].shape, o_stack[0].dtype)]
        aliases = {len(operands) - 1: 0}
    nin = len(operands)

    def body(*refs):
        a_ref, b_ref = refs[0], refs[1]
        e_refs = refs[2:2 + ne]
        o_refs = refs[nin:nin + no]

        def fin(acc):
            vals = epi(acc, *[e[...] for e in e_refs]) if epi is not None else (acc,)
            for o, v in zip(o_refs, vals):
                o[...] = v.astype(o.dtype)

        prod = _dot(a_ref[...], b_ref[...], dims)
        if gk == 1:
            fin(prod)
        else:
            acc_ref = refs[-1]
            k = pl.program_id(2)

            @pl.when(k == 0)
            def _():
                acc_ref[...] = prod

            @pl.when(k > 0)
            def _():
                acc_ref[...] += prod

            @pl.when(k == gk - 1)
            def _():
                fin(acc_ref[...])

    est = (_nbytes(a_blk, a.dtype) + _nbytes(b_blk, b.dtype)
           + sum(_nbytes((tm, tn), d) for d in out_dtypes)
           + sum(_nbytes((tm, tn), e.dtype) for e in extras)) + 2 * _nbytes((tm, tn), F32)
    outs = pl.pallas_call(
        body, name=name, grid=(gi, gj, gk),
        in_specs=in_specs,
        out_specs=[o_spec] * no,
        out_shape=out_shape,
        scratch_shapes=([pltpu.VMEM((tm, tn), F32)] if gk > 1 else []),
        input_output_aliases=aliases,
        compiler_params=_cparams(est, dimension_semantics=("parallel", "parallel", "arbitrary")),
    )(*operands)
    return outs if no > 1 else outs[0]


_TL = 512


def _rms_fwd(h, g, name):
    L, D = h.shape
    tl = _tile(L, _TL)

    def body(h_ref, g_ref, o_ref):
        x = h_ref[...]
        r = lax.rsqrt(jnp.mean(x * x, axis=-1, keepdims=True) + EPS)
        o_ref[...] = ((x * r) * g_ref[...]).astype(o_ref.dtype)

    return pl.pallas_call(
        body, name=name, grid=(L // tl,),
        in_specs=[pl.BlockSpec((tl, D), lambda i: (i, 0)), pl.BlockSpec((1, D), lambda i: (0, 0))],
        out_specs=pl.BlockSpec((tl, D), lambda i: (i, 0)),
        out_shape=jax.ShapeDtypeStruct((L, D), _ACT),
        compiler_params=_cparams(3 * _nbytes((tl, D), F32)),
    )(h, g.reshape(1, D))


def _rms_bwd(dxn, h, g, dres, name):
    L, D = h.shape
    tl = _tile(L, _TL)

    def body(d_ref, h_ref, g_ref, r_ref, o_ref, dg_ref):
        x = h_ref[...]
        r = lax.rsqrt(jnp.mean(x * x, axis=-1, keepdims=True) + EPS)
        xhat = x * r
        d = d_ref[...].astype(F32)
        gy = d * g_ref[...]
        dx = r * (gy - xhat * jnp.mean(gy * xhat, axis=-1, keepdims=True))
        o_ref[...] = r_ref[...] + dx

        @pl.when(pl.program_id(0) == 0)
        def _():
            dg_ref[...] = jnp.zeros_like(dg_ref)

        dg_ref[...] += jnp.sum(d * xhat, axis=0, keepdims=True)

    dh, dg = pl.pallas_call(
        body, name=name, grid=(L // tl,),
        in_specs=[pl.BlockSpec((tl, D), lambda i: (i, 0)), pl.BlockSpec((tl, D), lambda i: (i, 0)),
                  pl.BlockSpec((1, D), lambda i: (0, 0)), pl.BlockSpec((tl, D), lambda i: (i, 0))],
        out_specs=[pl.BlockSpec((tl, D), lambda i: (i, 0)), pl.BlockSpec((1, D), lambda i: (0, 0))],
        out_shape=[jax.ShapeDtypeStruct((L, D), F32), jax.ShapeDtypeStruct((1, D), F32)],
        compiler_params=_cparams(5 * _nbytes((tl, D), F32)),
    )(dxn, h, g.reshape(1, D), dres)
    return dh, dg.reshape(D)


def _rope_tables(positions):
    L = positions.shape[0]
    tl = _tile(L, 1024)
    inv = 1.0 / (ROPE_THETA ** (np.arange(0, HEAD_DIM, 2, dtype=np.float32) / HEAD_DIM))
    inv128 = jnp.asarray(np.tile(inv.astype(np.float32), 4).reshape(1, LANES))

    def body(p_ref, i_ref, c_ref, s_ref):
        ang = p_ref[...].astype(F32) * i_ref[...]
        c_ref[...] = jnp.cos(ang)
        s_ref[...] = jnp.sin(ang)

    return pl.pallas_call(
        body, name="rope_tables", grid=(L // tl,),
        in_specs=[pl.BlockSpec((tl, 1), lambda i: (i, 0)), pl.BlockSpec((1, LANES), lambda i: (0, 0))],
        out_specs=[pl.BlockSpec((tl, LANES), lambda i: (i, 0))] * 2,
        out_shape=[jax.ShapeDtypeStruct((L, LANES), F32)] * 2,
    )(positions.reshape(L, 1), inv128)


_GM_TL = 256


def _gmlp_head(Z, W, bfull, lg, lb, maskv):
    G = _gelu(Z)
    mu = jnp.sum(jnp.where(maskv, G, 0.0), axis=-1, keepdims=True) * (1.0 / HEAD_DIM)
    xc = jnp.where(maskv, G - mu, 0.0)
    var = jnp.sum(xc * xc, axis=-1, keepdims=True) * (1.0 / HEAD_DIM)
    rstd = lax.rsqrt(var + EPS)
    xhat = xc * rstd
    vn = xhat * lg + lb
    sv = _dot(W, vn) + bfull
    return G, xhat, rstd, vn, sv


def _tril(W):
    return jnp.where(_row(W.shape) >= _lane(W.shape), W, 0.0)


def _triu(W):
    return jnp.where(_row(W.shape) <= _lane(W.shape), W, 0.0)


def _gmlp_fwd(z, ws, bfull, lgf, lbf, name):
    L = z.shape[0]
    tl = _tile(L, _GM_TL)
    nch = tl // CHUNK

    def body(z_ref, w_ref, b_ref, lg_ref, lb_ref, o_ref):
        maskv = _lane((CHUNK, LANES)) >= HEAD_DIM
        for c in range(nch):
            rows = slice(c * CHUNK, (c + 1) * CHUNK)
            for hp in range(A_HEADS // 2):
                acc = None
                for hh in range(2):
                    h = 2 * hp + hh
                    Z = z_ref[rows, h * LANES:(h + 1) * LANES]
                    G, _, _, _, sv = _gmlp_head(Z, _tril(w_ref[h]), b_ref[h], lg_ref[h:h + 1, :], lb_ref[h:h + 1, :], maskv)
                    prod = G * pltpu.roll(sv, HEAD_DIM, axis=1)
                    acc = prod if hh == 0 else acc + pltpu.roll(prod, HEAD_DIM, axis=1)
                o_ref[rows, hp * LANES:(hp + 1) * LANES] = acc

    return pl.pallas_call(
        body, name=name, grid=(L // tl,),
        in_specs=[pl.BlockSpec((tl, IN_A), lambda i: (i, 0)),
                  pl.BlockSpec((A_HEADS, CHUNK, CHUNK), lambda i: (0, 0, 0)),
                  pl.BlockSpec((A_HEADS, CHUNK, LANES), lambda i: (0, 0, 0)),
                  pl.BlockSpec((A_HEADS, LANES), lambda i: (0, 0)),
                  pl.BlockSpec((A_HEADS, LANES), lambda i: (0, 0))],
        out_specs=pl.BlockSpec((tl, 2 * LANES), lambda i: (i, 0)),
        out_shape=jax.ShapeDtypeStruct((L, 2 * LANES), F32),
    )(z, ws, bfull, lgf, lbf)


def _gmlp_bwd(z, dya, ws, wsT, bfull, lgf, lbf, name):
    L = z.shape[0]
    tl = _tile(L, _GM_TL)
    nch = tl // CHUNK
    nsteps = L // tl

    def body(z_ref, d_ref, w_ref, wt_ref, b_ref, lg_ref, lb_ref, dz_ref, dw_ref, db_ref, dlg_ref, dlb_ref):
        step = pl.program_id(0)

        @pl.when(step == 0)
        def _():
            dw_ref[...] = jnp.zeros_like(dw_ref)
            db_ref[...] = jnp.zeros_like(db_ref)
            dlg_ref[...] = jnp.zeros_like(dlg_ref)
            dlb_ref[...] = jnp.zeros_like(dlb_ref)

        lane = _lane((CHUNK, LANES))
        maskv = lane >= HEAD_DIM
        for c in range(nch):
            rows = slice(c * CHUNK, (c + 1) * CHUNK)
            for h in range(A_HEADS):
                hp, hh = divmod(h, 2)
                Z = z_ref[rows, h * LANES:(h + 1) * LANES]
                lg = lg_ref[h:h + 1, :]
                G, xhat, rstd, vn, sv = _gmlp_head(Z, _tril(w_ref[h]), b_ref[h], lg, lb_ref[h:h + 1, :], maskv)
                dpair = d_ref[rows, hp * LANES:(hp + 1) * LANES]
                if hh == 1:
                    dpair = pltpu.roll(dpair, HEAD_DIM, axis=1)
                dout = jnp.where(maskv, 0.0, dpair)
                du = dout * pltpu.roll(sv, HEAD_DIM, axis=1)
                dsv = pltpu.roll(dout * G, HEAD_DIM, axis=1)
                dw_ref[h] += _tril(_dot(dsv, vn, _NT))
                db_ref[h] += dsv
                dvn = _dot(_triu(wt_ref[h]), dsv)
                dlg_ref[h] += dvn * xhat
                dlb_ref[h] += dvn
                dxh = dvn * lg
                m1 = jnp.sum(dxh, axis=-1, keepdims=True) * (1.0 / HEAD_DIM)
                m2 = jnp.sum(dxh * xhat, axis=-1, keepdims=True) * (1.0 / HEAD_DIM)
                dv = jnp.where(maskv, rstd * (dxh - m1 - xhat * m2), 0.0)
                dz_ref[rows, h * LANES:(h + 1) * LANES] = ((du + dv) * _gelu_grad(Z)).astype(dz_ref.dtype)

        @pl.when(step == nsteps - 1)
        def _():
            for h in range(A_HEADS):
                db_ref[h] = jnp.broadcast_to(jnp.sum(db_ref[h], axis=1, keepdims=True), (CHUNK, LANES))
                dlg_ref[h] = jnp.broadcast_to(jnp.sum(dlg_ref[h], axis=0, keepdims=True), (CHUNK, LANES))
                dlb_ref[h] = jnp.broadcast_to(jnp.sum(dlb_ref[h], axis=0, keepdims=True), (CHUNK, LANES))

    full3 = pl.BlockSpec((A_HEADS, CHUNK, LANES), lambda i: (0, 0, 0))
    return pl.pallas_call(
        body, name=name, grid=(nsteps,),
        in_specs=[pl.BlockSpec((tl, IN_A), lambda i: (i, 0)),
                  pl.BlockSpec((tl, 2 * LANES), lambda i: (i, 0)),
                  full3, full3, full3,
                  pl.BlockSpec((A_HEADS, LANES), lambda i: (0, 0)),
                  pl.BlockSpec((A_HEADS, LANES), lambda i: (0, 0))],
        out_specs=[pl.BlockSpec((tl, IN_A), lambda i: (i, 0)), full3, full3, full3, full3],
        out_shape=[jax.ShapeDtypeStruct((L, IN_A), _ACT)] + [jax.ShapeDtypeStruct((A_HEADS, CHUNK, LANES), F32)] * 4,
    )(z, dya, ws, wsT, bfull, lgf, lbf)


def _head_rstd(x, lo):
    sq = x * x
    s_lo = jnp.sum(jnp.where(lo, sq, 0.0), axis=-1, keepdims=True)
    s_hi = jnp.sum(jnp.where(lo, 0.0, sq), axis=-1, keepdims=True)
    return jnp.where(lo, lax.rsqrt(s_lo * (1.0 / HEAD_DIM) + EPS), lax.rsqrt(s_hi * (1.0 / HEAD_DIM) + EPS))


def _rot_half(x, first):
    return jnp.where(first, -pltpu.roll(x, LANES - HEAD_DIM // 2, axis=1), pltpu.roll(x, HEAD_DIM // 2, axis=1))


def _qk_prep(z, cos, sin, gq, gk, name):
    L = z.shape[0]
    tl = _tile(L, _TL)
    nq = IN_Q // LANES

    def body(q_ref, k_ref, c_ref, s_ref, gq_ref, gk_ref, qo_ref, ko_ref):
        lane = _lane((tl, LANES))
        lo = lane < HEAD_DIM
        first = (lane % HEAD_DIM) < (HEAD_DIM // 2)
        c, s = c_ref[...], s_ref[...]

        def prep(x, g):
            xn = (x * _head_rstd(x, lo)) * g
            return xn * c + _rot_half(xn, first) * s

        for j in range(nq):
            qo_ref[:, j * LANES:(j + 1) * LANES] = prep(q_ref[:, j * LANES:(j + 1) * LANES], gq_ref[...]).astype(qo_ref.dtype)
        ko_ref[...] = prep(k_ref[...], gk_ref[...]).astype(ko_ref.dtype)

    return pl.pallas_call(
        body, name=name, grid=(L // tl,),
        in_specs=[pl.BlockSpec((tl, IN_Q), lambda i: (i, 1)),
                  pl.BlockSpec((tl, IN_KV), lambda i: (i, 8)),
                  pl.BlockSpec((tl, LANES), lambda i: (i, 0)), pl.BlockSpec((tl, LANES), lambda i: (i, 0)),
                  pl.BlockSpec((1, LANES), lambda i: (0, 0)), pl.BlockSpec((1, LANES), lambda i: (0, 0))],
        out_specs=[pl.BlockSpec((tl, IN_Q), lambda i: (i, 0)), pl.BlockSpec((tl, IN_KV), lambda i: (i, 0))],
        out_shape=[jax.ShapeDtypeStruct((L, IN_Q), _ACT), jax.ShapeDtypeStruct((L, IN_KV), _ACT)],
    )(z, z, cos, sin, gq, gk)


def _qk_prep_bwd(z, dq, dkc, dkp, dvc, dvp, cos, sin, gq, gk, name):
    L = z.shape[0]
    tl = WINDOW
    nb = L // tl
    nq = IN_Q // LANES

    def body(q_ref, k_ref, dq_ref, dkc_ref, dkp_ref, dvc_ref, dvp_ref, c_ref, s_ref, gq_ref, gk_ref,
             dzq_ref, dzk_ref, dzv_ref, dgq_ref, dgk_ref):
        n = pl.program_id(0)

        @pl.when(n == 0)
        def _():
            dgq_ref[...] = jnp.zeros_like(dgq_ref)
            dgk_ref[...] = jnp.zeros_like(dgk_ref)

        lane = _lane((tl, LANES))
        lo = lane < HEAD_DIM
        first = (lane % HEAD_DIM) < (HEAD_DIM // 2)
        c, s = c_ref[...], s_ref[...]
        has_next = jnp.where(n < nb - 1, 1.0, 0.0)

        def bwd(x, g, dy):
            r = _head_rstd(x, lo)
            xhat = x * r
            dxn = dy * c - _rot_half(dy * s, first)
            gy = dxn * g
            t = gy * xhat
            m_lo = jnp.sum(jnp.where(lo, t, 0.0), axis=-1, keepdims=True)
            m_hi = jnp.sum(jnp.where(lo, 0.0, t), axis=-1, keepdims=True)
            m = jnp.where(lo, m_lo, m_hi) * (1.0 / HEAD_DIM)
            dx = r * (gy - xhat * m)
            dg = jnp.sum(dxn * xhat, axis=0, keepdims=True)
            return dx, dg

        dgq = jnp.zeros((1, LANES), F32)
        for j in range(nq):
            sl = slice(j * LANES, (j + 1) * LANES)
            dx, dg = bwd(q_ref[:, sl], gq_ref[...], dq_ref[:, sl].astype(F32))
            dzq_ref[:, sl] = dx.astype(dzq_ref.dtype)
            dgq = dgq + dg
        dgq_ref[...] += dgq + pltpu.roll(dgq, HEAD_DIM, axis=1)
        dk = dkc_ref[...] + has_next * dkp_ref[...]
        dx, dg = bwd(k_ref[...], gk_ref[...], dk)
        dzk_ref[...] = dx.astype(dzk_ref.dtype)
        dgk_ref[...] += dg + pltpu.roll(dg, HEAD_DIM, axis=1)
        dzv_ref[...] = (dvc_ref[...] + has_next * dvp_ref[...]).astype(dzv_ref.dtype)

    nxt = lambda i: (jnp.minimum(i + 1, nb - 1), 0)
    cur = lambda i: (i, 0)
    kv = pl.BlockSpec((tl, IN_KV), cur)
    one = pl.BlockSpec((1, LANES), lambda i: (0, 0))
    return pl.pallas_call(
        body, name=name, grid=(nb,),
        in_specs=[pl.BlockSpec((tl, IN_Q), lambda i: (i, 1)), pl.BlockSpec((tl, IN_KV), lambda i: (i, 8)),
                  pl.BlockSpec((tl, IN_Q), cur), kv, pl.BlockSpec((tl, IN_KV), nxt), kv, pl.BlockSpec((tl, IN_KV), nxt),
                  kv, kv, one, one],
        out_specs=[pl.BlockSpec((tl, IN_Q), cur), kv, kv, one, one],
        out_shape=[jax.ShapeDtypeStruct((L, IN_Q), _ACT), jax.ShapeDtypeStruct((L, IN_KV), _ACT),
                   jax.ShapeDtypeStruct((L, IN_KV), _ACT), jax.ShapeDtypeStruct((1, LANES), F32),
                   jax.ShapeDtypeStruct((1, LANES), F32)],
    )(z, z, dq, dkc, dkp, dvc, dvp, cos, sin, gq, gk)


def _attn_mask(n):
    shp = (B_GROUP * WINDOW, 2 * WINDOW)
    qi = _row(shp) % WINDOW
    kj = _lane(shp)
    off = jnp.where(n > 0, 0, 4 * WINDOW)
    return ((kj >= WINDOW) & (kj - WINDOW <= qi)) | ((kj < WINDOW) & (kj > qi + off))


def _kv_lanes(j):
    lane = _lane((WINDOW, LANES))
    return (lane >= j * HEAD_DIM) & (lane < (j + 1) * HEAD_DIM)


def _stack_heads(ref, j, kvl):
    parts = []
    for g in range(B_GROUP):
        h = j * B_GROUP + g
        slab = ref[:, (h // 2) * LANES:(h // 2 + 1) * LANES].astype(F32)
        if (h % 2) != j:
            slab = pltpu.roll(slab, HEAD_DIM, axis=1)
        parts.append(jnp.where(kvl, slab, 0.0))
    return jnp.concatenate(parts, axis=0)


def _attn_probs(qs, k2, sink_col, mask):
    s = _dot(qs, k2, _NT) * (HEAD_DIM ** -0.5)
    s = jnp.where(mask, s, NEG)
    m = jnp.maximum(jnp.max(s, axis=-1, keepdims=True), sink_col)
    p = jnp.exp(s - m)
    esink = jnp.exp(sink_col - m)
    inv = 1.0 / (jnp.sum(p, axis=-1, keepdims=True) + esink)
    return p * inv, esink * inv


def _sink_col(sink_ref, j):
    return jnp.concatenate([jnp.full((WINDOW, 1), sink_ref[j * B_GROUP + g], F32) for g in range(B_GROUP)], axis=0)


def _attn_fwd(q, k, z, sinks, name):
    L = q.shape[0]
    nb = L // WINDOW
    prev = lambda n: (jnp.maximum(n - 1, 0), 0)
    prev_v = lambda n: (jnp.maximum(n - 1, 0), 9)

    def body(s_ref, q_ref, kp_ref, kc_ref, vp_ref, vc_ref, o_ref):
        n = pl.program_id(0)
        mask = _attn_mask(n)
        k2 = jnp.concatenate([kp_ref[...], kc_ref[...]], axis=0)
        v2 = jnp.concatenate([vp_ref[...], vc_ref[...]], axis=0)
        slabs = [None] * (IN_Q // LANES)
        for j in range(B_KV_HEADS):
            kvl = _kv_lanes(j)
            qs = _stack_heads(q_ref, j, kvl)
            pn, _ = _attn_probs(qs, k2, _sink_col(s_ref, j), mask)
            o = _dot(pn, v2)
            for g in range(B_GROUP):
                h = j * B_GROUP + g
                piece = jnp.where(kvl, o[g * WINDOW:(g + 1) * WINDOW], 0.0)
                if (h % 2) != j:
                    piece = pltpu.roll(piece, HEAD_DIM, axis=1)
                slabs[h // 2] = piece if slabs[h // 2] is None else slabs[h // 2] + piece
        for t, sl in enumerate(slabs):
            o_ref[:, t * LANES:(t + 1) * LANES] = sl

    return pl.pallas_call(
        body, name=name, grid=(nb,),
        in_specs=[pl.BlockSpec(memory_space=pltpu.SMEM),
                  pl.BlockSpec((WINDOW, IN_Q), lambda n: (n, 0)),
                  pl.BlockSpec((WINDOW, IN_KV), prev), pl.BlockSpec((WINDOW, IN_KV), lambda n: (n, 0)),
                  pl.BlockSpec((WINDOW, IN_KV), prev_v), pl.BlockSpec((WINDOW, IN_KV), lambda n: (n, 9))],
        out_specs=pl.BlockSpec((WINDOW, IN_Q), lambda n: (n, 0)),
        out_shape=jax.ShapeDtypeStruct((L, IN_Q), F32),
    )(sinks, q, k, k, z, z)


def _attn_bwd(q, k, z, sinks, dyb, name):
    L = q.shape[0]
    nb = L // WINDOW
    prev = lambda n: (jnp.maximum(n - 1, 0), 0)
    prev_v = lambda n: (jnp.maximum(n - 1, 0), 9)
    cur = lambda n: (n, 0)

    def body(s_ref, q_ref, kp_ref, kc_ref, vp_ref, vc_ref, d_ref, dq_ref, dkc_ref, dkp_ref, dvc_ref, dvp_ref, ds_ref):
        n = pl.program_id(0)

        @pl.when(n == 0)
        def _():
            ds_ref[...] = jnp.zeros_like(ds_ref)

        mask = _attn_mask(n)
        k2 = jnp.concatenate([kp_ref[...], kc_ref[...]], axis=0)
        v2 = jnp.concatenate([vp_ref[...], vc_ref[...]], axis=0)
        slabs = [None] * (IN_Q // LANES)
        dk2 = jnp.zeros((2 * WINDOW, LANES), F32)
        dv2 = jnp.zeros((2 * WINDOW, LANES), F32)
        dsink = jnp.zeros((1, LANES), F32)
        lane1 = _lane((1, LANES))
        for j in range(B_KV_HEADS):
            kvl = _kv_lanes(j)
            qs = _stack_heads(q_ref, j, kvl)
            dos = _stack_heads(d_ref, j, kvl)
            pn, psink = _attn_probs(qs, k2, _sink_col(s_ref, j), mask)
            dp = _dot(dos, v2, _NT)
            dd = jnp.sum(pn * dp, axis=-1, keepdims=True)
            dss = (pn * (dp - dd)) * (HEAD_DIM ** -0.5)
            dqs = _dot(dss, k2)
            dk2 = dk2 + _dot(dss, qs, _TN)
            dv2 = dv2 + _dot(pn, dos, _TN)
            sd = psink * dd
            for g in range(B_GROUP):
                h = j * B_GROUP + g
                piece = jnp.where(kvl, dqs[g * WINDOW:(g + 1) * WINDOW], 0.0)
                if (h % 2) != j:
                    piece = pltpu.roll(piece, HEAD_DIM, axis=1)
                slabs[h // 2] = piece if slabs[h // 2] is None else slabs[h // 2] + piece
                tot = jnp.sum(sd[g * WINDOW:(g + 1) * WINDOW], axis=0, keepdims=True)
                dsink = dsink - jnp.where(lane1 == h, tot, 0.0)
        for t, sl in enumerate(slabs):
            dq_ref[:, t * LANES:(t + 1) * LANES] = sl
        dkp_ref[...] = dk2[:WINDOW]
        dkc_ref[...] = dk2[WINDOW:]
        dvp_ref[...] = dv2[:WINDOW]
        dvc_ref[...] = dv2[WINDOW:]
        ds_ref[0:1, :] += dsink

    kvs = pl.BlockSpec((WINDOW, IN_KV), cur)
    kvo = jax.ShapeDtypeStruct((L, IN_KV), F32)
    return pl.pallas_call(
        body, name=name, grid=(nb,),
        in_specs=[pl.BlockSpec(memory_space=pltpu.SMEM),
                  pl.BlockSpec((WINDOW, IN_Q), cur),
                  pl.BlockSpec((WINDOW, IN_KV), prev), kvs,
                  pl.BlockSpec((WINDOW, IN_KV), prev_v), pl.BlockSpec((WINDOW, IN_KV), lambda n: (n, 9)),
                  pl.BlockSpec((WINDOW, IN_Q), cur)],
        out_specs=[pl.BlockSpec((WINDOW, IN_Q), cur), kvs, kvs, kvs, kvs, pl.BlockSpec((SUBLANES, LANES), lambda n: (0, 0))],
        out_shape=[jax.ShapeDtypeStruct((L, IN_Q), F32), kvo, kvo, kvo, kvo, jax.ShapeDtypeStruct((SUBLANES, LANES), F32)],
    )(sinks, q, k, k, z, z, dyb)


def _ssm_disc(are, aim, ldt, bre, bim):
    dt = jnp.exp(ldt)
    mag = jnp.exp(are * dt)
    lr, li = mag * jnp.cos(aim * dt), mag * jnp.sin(aim * dt)
    den = are * are + aim * aim
    xr, xi = lr - 1.0, li
    cr, ci = (xr * are + xi * aim) / den, (xi * are - xr * aim) / den
    return lr, li, cr * bre - ci * bim, cr * bim + ci * bre


def _ssm_prep(are, aim, ldt, bre, bim):
    shp3, shpb = are.shape, bre.shape

    def body(are_ref, aim_ref, ldt_ref, bre_ref, bim_ref, lr_ref, li_ref, br_ref, bi_ref):
        lr, li, br, bi = _ssm_disc(are_ref[...], aim_ref[...], ldt_ref[...], bre_ref[...], bim_ref[...])
        lr_ref[...] = lr
        li_ref[...] = li
        br_ref[...] = br
        bi_ref[...] = bi

    return pl.pallas_call(
        body, name="ssm_prep",
        out_shape=[jax.ShapeDtypeStruct(shp3, F32)] * 2 + [jax.ShapeDtypeStruct(shpb, F32)] * 2,
    )(are, aim, ldt, bre, bim)


def _ssm_prep_bwd(are, aim, ldt, bre, bim, dlr, dli, dbr, dbi):
    shp3, shpb = are.shape, bre.shape

    def body(are_ref, aim_ref, ldt_ref, bre_ref, bim_ref, dlr_ref, dli_ref, dbr_ref, dbi_ref,
             o_are, o_aim, o_ldt, o_bre, o_bim):
        _, vjp = jax.vjp(_ssm_disc, are_ref[...], aim_ref[...], ldt_ref[...], bre_ref[...], bim_ref[...])
        g = vjp((dlr_ref[...], dli_ref[...], dbr_ref[...], dbi_ref[...]))
        o_are[...] = g[0]
        o_aim[...] = g[1]
        o_ldt[...] = jnp.broadcast_to(jnp.sum(g[2], axis=-1, keepdims=True), shp3)
        o_bre[...] = g[3]
        o_bim[...] = g[4]

    return pl.pallas_call(
        body, name="ssm_prep_bwd",
        out_shape=[jax.ShapeDtypeStruct(shp3, F32)] * 3 + [jax.ShapeDtypeStruct(shpb, F32)] * 2,
    )(are, aim, ldt, bre, bim, dlr, dli, dbr, dbi)


_SCAN_TB = 512
_SCAN_W = 256


def _cmul(ar, ai, br, bi):
    return ar * br - ai * bi, ar * bi + ai * br


def _ssm_scan(x, lam_r, lam_i, name, reverse=False, states=None):
    L = x.shape[0]
    tb = _tile(L, _SCAN_TB)
    nrb = L // tb
    nt = tb // SUBLANES
    W = _SCAN_W
    with_da = states is not None

    def body(*refs):
        if with_da:
            xr_ref, xi_ref, sr_ref, si_ref, ar_ref, ai_ref, o_ref, dar_ref, dai_ref, cr_ref, ci_ref = refs
        else:
            xr_ref, xi_ref, ar_ref, ai_ref, o_ref, cr_ref, ci_ref = refs
        step = pl.program_id(0)

        @pl.when(step == 0)
        def _():
            cr_ref[...] = jnp.zeros_like(cr_ref)
            ci_ref[...] = jnp.zeros_like(ci_ref)
            if with_da:
                dar_ref[...] = jnp.zeros_like(dar_ref)
                dai_ref[...] = jnp.zeros_like(dai_ref)

        row = _row((SUBLANES, W))

        def shift(v, d, fill):
            if reverse:
                return jnp.where(row < SUBLANES - d, pltpu.roll(v, SUBLANES - d, axis=0), fill)
            return jnp.where(row >= d, pltpu.roll(v, d, axis=0), fill)

        edge = 0 if reverse else SUBLANES - 1
        for wb in range(N_STATE // W):
            cols = slice(wb * W, (wb + 1) * W)
            a1r = jnp.broadcast_to(ar_ref[:, cols], (SUBLANES, W))
            a1i = jnp.broadcast_to(ai_ref[:, cols], (SUBLANES, W))
            if reverse:
                a1i = -a1i
            a2r, a2i = _cmul(a1r, a1i, a1r, a1i)
            a4r, a4i = _cmul(a2r, a2i, a2r, a2i)
            pws = ((1, a1r, a1i), (2, a2r, a2i), (4, a4r, a4i))
            pr, pi = a1r, a1i
            for d, _, _ in pws:
                qr, qi = _cmul(pr, pi, shift(pr, d, 1.0), shift(pi, d, 0.0))
                pr, pi = qr, qi

            def tile(i, carry):
                cr, ci, dr, di = carry
                t = (nt - 1 - i) if reverse else i
                r0 = pl.multiple_of(t * SUBLANES, SUBLANES)
                vr = xr_ref[pl.ds(r0, SUBLANES), cols]
                vi = xi_ref[pl.ds(r0, SUBLANES), cols]
                for d, er, ei in pws:
                    tr, ti = _cmul(er, ei, shift(vr, d, 0.0), shift(vi, d, 0.0))
                    vr, vi = vr + tr, vi + ti
                tr, ti = _cmul(pr, pi, cr, ci)
                vr, vi = vr + tr, vi + ti
                o_ref[pl.ds(r0, SUBLANES), cols] = vr
                o_ref[pl.ds(r0, SUBLANES), slice(N_STATE + wb * W, N_STATE + (wb + 1) * W)] = vi
                if with_da:
                    gr = jnp.where(row < SUBLANES - 1, pltpu.roll(vr, SUBLANES - 1, axis=0), cr)
                    gi = jnp.where(row < SUBLANES - 1, pltpu.roll(vi, SUBLANES - 1, axis=0), ci)
                    sr = sr_ref[pl.ds(r0, SUBLANES), cols]
                    si = si_ref[pl.ds(r0, SUBLANES), cols]
                    dr = dr + sr * gr + si * gi
                    di = di + sr * gi - si * gr
                ncr = jnp.broadcast_to(vr[edge:edge + 1, :], (SUBLANES, W))
                nci = jnp.broadcast_to(vi[edge:edge + 1, :], (SUBLANES, W))
                return ncr, nci, dr, di

            zero = jnp.zeros((SUBLANES, W), F32)
            cr, ci, dr, di = lax.fori_loop(0, nt, tile, (cr_ref[:, cols], ci_ref[:, cols], zero, zero), unroll=2)
            cr_ref[:, cols] = cr
            ci_ref[:, cols] = ci
            if with_da:
                dar_ref[:, cols] += dr
                dai_ref[:, cols] += di

        if with_da:
            @pl.when(step == nrb - 1)
            def _():
                dar_ref[...] = jnp.broadcast_to(jnp.sum(dar_ref[...], axis=0, keepdims=True), dar_ref.shape)
                dai_ref[...] = jnp.broadcast_to(jnp.sum(dai_ref[...], axis=0, keepdims=True), dai_ref.shape)

    rb = (lambda i: (nrb - 1 - i, 0)) if reverse else (lambda i: (i, 0))
    rb_im = (lambda i: (nrb - 1 - i, 1)) if reverse else (lambda i: (i, 1))
    blk_r = pl.BlockSpec((tb, N_STATE), rb)
    blk_i = pl.BlockSpec((tb, N_STATE), rb_im)
    one = pl.BlockSpec((1, N_STATE), lambda i: (0, 0))
    acc = pl.BlockSpec((SUBLANES, N_STATE), lambda i: (0, 0))
    ins = [x, x] + ([states, states] if with_da else []) + [lam_r, lam_i]
    in_specs = [blk_r, blk_i] + ([blk_r, blk_i] if with_da else []) + [one, one]
    out_specs = [pl.BlockSpec((tb, 2 * N_STATE), rb)] + ([acc, acc] if with_da else [])
    out_shape = [jax.ShapeDtypeStruct((L, 2 * N_STATE), F32)] + (
        [jax.ShapeDtypeStruct((SUBLANES, N_STATE), F32)] * 2 if with_da else [])
    outs = pl.pallas_call(
        body, name=name, grid=(nrb,), in_specs=in_specs, out_specs=out_specs, out_shape=out_shape,
        scratch_shapes=[pltpu.VMEM((SUBLANES, N_STATE), F32)] * 2,
        compiler_params=_cparams((6 if with_da else 4) * _nbytes((tb, N_STATE), F32),
                                 dimension_semantics=("arbitrary",)),
    )(*ins)
    return outs if with_da else outs[0]


_GROUPS = ((0, 256), (256, 768), (768, 1024))


def _merge_fwd(ya, yb, g12, mixg, name):
    L = ya.shape[0]
    tl = _tile(L, _TL)

    def body(a_ref, b_ref, g_ref, m_ref, o_ref):
        g12v = g_ref[...]
        yc = g12v[:, :C_WIDTH] * _sigmoid(g12v[:, C_WIDTH:])
        for (lo, hi), y in zip(_GROUPS, (a_ref[...], b_ref[...], yc)):
            r = lax.rsqrt(jnp.mean(y * y, axis=-1, keepdims=True) + EPS)
            o_ref[:, lo:hi] = ((y * r) * m_ref[:, lo:hi]).astype(o_ref.dtype)

    row = lambda w: pl.BlockSpec((tl, w), lambda i: (i, 0))
    return pl.pallas_call(
        body, name=name, grid=(L // tl,),
        in_specs=[row(256), row(512), row(512), pl.BlockSpec((1, D_MODEL), lambda i: (0, 0))],
        out_specs=row(D_MODEL), out_shape=jax.ShapeDtypeStruct((L, D_MODEL), _ACT),
    )(ya, yb, g12, mixg.reshape(1, D_MODEL))


def _merge_bwd(dy, ya, yb, g12, mixg, name):
    L = ya.shape[0]
    tl = _tile(L, _TL)

    def body(d_ref, a_ref, b_ref, g_ref, m_ref, da_ref, db_ref, dg_ref, dm_ref):
        @pl.when(pl.program_id(0) == 0)
        def _():
            dm_ref[...] = jnp.zeros_like(dm_ref)

        g12v = g_ref[...]
        g1, sg = g12v[:, :C_WIDTH], _sigmoid(g12v[:, C_WIDTH:])
        yc = g1 * sg
        outs = []
        for (lo, hi), y in zip(_GROUPS, (a_ref[...], b_ref[...], yc)):
            r = lax.rsqrt(jnp.mean(y * y, axis=-1, keepdims=True) + EPS)
            xhat = y * r
            d = d_ref[:, lo:hi]
            gy = d * m_ref[:, lo:hi]
            outs.append(r * (gy - xhat * jnp.mean(gy * xhat, axis=-1, keepdims=True)))
            dm_ref[:, lo:hi] += jnp.sum(d * xhat, axis=0, keepdims=True)
        da_ref[...] = outs[0]
        db_ref[...] = outs[1]
        dyc = outs[2]
        dg_ref[:, :C_WIDTH] = (dyc * sg).astype(dg_ref.dtype)
        dg_ref[:, C_WIDTH:] = (dyc * g1 * sg * (1.0 - sg)).astype(dg_ref.dtype)

    row = lambda w: pl.BlockSpec((tl, w), lambda i: (i, 0))
    one = pl.BlockSpec((1, D_MODEL), lambda i: (0, 0))
    return pl.pallas_call(
        body, name=name, grid=(L // tl,),
        in_specs=[row(D_MODEL), row(256), row(512), row(512), one],
        out_specs=[row(256), row(512), row(512), one],
        out_shape=[jax.ShapeDtypeStruct((L, 256), F32), jax.ShapeDtypeStruct((L, 512), F32),
                   jax.ShapeDtypeStruct((L, 512), _ACT), jax.ShapeDtypeStruct((1, D_MODEL), F32)],
    )(dy, ya, yb, g12, mixg.reshape(1, D_MODEL))


def _ple_bwd_elem(dh, gate, e, name):
    L, D = dh.shape
    tl = _tile(L, _TL)

    def body(d_ref, g_ref, e_ref, p_ref, o_ref):
        d, g = d_ref[...], g_ref[...]
        p_ref[...] = (d * e_ref[...] * g * (1.0 - g)).astype(p_ref.dtype)
        o_ref[...] = (d * g).astype(o_ref.dtype)

    row = pl.BlockSpec((tl, D), lambda i: (i, 0))
    return pl.pallas_call(
        body, name=name, grid=(L // tl,), in_specs=[row] * 3, out_specs=[row] * 2,
        out_shape=[jax.ShapeDtypeStruct((L, D), _ACT)] * 2,
        compiler_params=_cparams(4 * _nbytes((tl, D), F32)),
    )(dh, gate, e)


def _dskip_bwd(dy, z, name):
    L = dy.shape[0]
    tl = _tile(L, _TL)

    def body(d_ref, u_ref, o_ref):
        @pl.when(pl.program_id(0) == 0)
        def _():
            o_ref[...] = jnp.zeros_like(o_ref)

        o_ref[...] += jnp.sum(d_ref[...] * u_ref[...], axis=0, keepdims=True)

    return pl.pallas_call(
        body, name=name, grid=(L // tl,),
        in_specs=[pl.BlockSpec((tl, C_WIDTH), lambda i: (i, 0)), pl.BlockSpec((tl, C_WIDTH), lambda i: (i, 5))],
        out_specs=pl.BlockSpec((1, C_WIDTH), lambda i: (0, 0)),
        out_shape=jax.ShapeDtypeStruct((1, C_WIDTH), F32),
    )(dy, z)


def _loss_fwd_bwd(y, target):
    L, D = y.shape
    tl = _tile(L, _TL)

    def body(y_ref, t_ref, l_ref, d_ref):
        @pl.when(pl.program_id(0) == 0)
        def _():
            l_ref[...] = jnp.zeros_like(l_ref)

        e = y_ref[...] - t_ref[...]
        d_ref[...] = e * (1.0 / D)
        part = jnp.sum(jnp.sum(e * e, axis=-1, keepdims=True), axis=0, keepdims=True)
        l_ref[...] += jnp.broadcast_to(part, l_ref.shape)

    row = pl.BlockSpec((tl, D), lambda i: (i, 0))
    return pl.pallas_call(
        body, name="loss", grid=(L // tl,), in_specs=[row, row],
        out_specs=[pl.BlockSpec((SUBLANES, LANES), lambda i: (0, 0)), row],
        out_shape=[jax.ShapeDtypeStruct((SUBLANES, LANES), F32), jax.ShapeDtypeStruct((L, D), F32)],
    )(y, target)


def _adamw(w, g, m, v, name):
    R, C = w.shape
    tr = R if R <= 512 else _tile_rows(R, 512)

    def body(w_ref, g_ref, m_ref, v_ref, d_ref, nm_ref, nv_ref):
        gv = g_ref[...]
        nm = ADAM_B1 * m_ref[...] + (1.0 - ADAM_B1) * gv
        nv = ADAM_B2 * v_ref[...] + (1.0 - ADAM_B2) * (gv * gv)
        m_hat = nm / (1.0 - ADAM_B1 ** ADAM_STEP)
        v_hat = nv / (1.0 - ADAM_B2 ** ADAM_STEP)
        d_ref[...] = -ADAM_LR * (m_hat / (jnp.sqrt(v_hat) + ADAM_EPS) + ADAM_WD * w_ref[...])
        nm_ref[...] = nm
        nv_ref[...] = nv

    blk = pl.BlockSpec((tr, C), lambda i: (i, 0))
    return pl.pallas_call(
        body, name=name, grid=(R // tr,), in_specs=[blk] * 4, out_specs=[blk] * 3,
        out_shape=[jax.ShapeDtypeStruct((R, C), F32)] * 3,
        compiler_params=_cparams(7 * _nbytes((tr, C), F32)),
    )(w, g, m, v)


def _tile_rows(R, pref):
    t = pref
    while R % t:
        t -= SUBLANES
    assert t > 0
    return t


def _add_n(xs, name):
    R, C = xs[0].shape
    tr = R if R <= 512 else _tile_rows(R, 512)
    n = len(xs)

    def body(*refs):
        acc = refs[0][...].astype(F32)
        for r in refs[1:n]:
            acc = acc + r[...].astype(F32)
        refs[n][...] = acc

    blk = pl.BlockSpec((tr, C), lambda i: (i, 0))
    return pl.pallas_call(
        body, name=name, grid=(R // tr,), in_specs=[blk] * n, out_specs=blk,
        out_shape=jax.ShapeDtypeStruct((R, C), F32),
        compiler_params=_cparams((n + 1) * _nbytes((tr, C), F32)),
    )(*xs)


def _relu2(acc):
    r = jnp.maximum(acc, 0.0)
    return acc, r * r


def _layer_fwd(h, lp, cos, sin):
    L = h.shape[0]
    xn = _rms_fwd(h, lp["attn_norm_g"], "f_norm_attn")
    z = _mm(xn, lp["w_in"], mode="nn", M=L, N=IN_COLS, K=D_MODEL, b_cb=True, out_dtypes=[F32], name="f_w_in")
    ya = _gmlp_fwd(z, lp["ws"], lp["bfull"], lp["lgf"], lp["lbf"], "f_gmlp")
    q, k = _qk_prep(z, cos, sin, lp["gq"], lp["gk"], "f_qk_prep")
    yb = _attn_fwd(q, k, z, lp["sinks"], "f_attn")
    bu = _mm(z, lp["bcat"], mode="nn", M=L, N=2 * N_STATE, K=C_WIDTH, a_off=5, tk=C_WIDTH,
             out_dtypes=[F32], name="f_ssm_in")
    S = _ssm_scan(bu, lp["lam_r"], lp["lam_i"], "f_ssm_scan")
    y, yg = _mm(S, lp["ccat"], mode="nn", M=L, N=C_WIDTH, K=2 * N_STATE, tk=2 * N_STATE,
                extras=[(z, 5), (lp["dskip"], 0)], out_dtypes=[F32, _ACT], name="f_ssm_out",
                epi=lambda acc, u, dsk: (acc + dsk * u, _gelu(acc + dsk * u)))
    g12 = _mm(yg, lp["w12"], mode="nn", M=L, N=2 * C_WIDTH, K=C_WIDTH, out_dtypes=[F32], name="f_glu")
    ycat = _merge_fwd(ya, yb, g12, lp["mix_out_g"], "f_merge")
    h1 = _mm(ycat, lp["w_out"], mode="nn", M=L, N=D_MODEL, K=D_MODEL, extras=[(h, 0)],
             epi=lambda acc, r: (r + acc,), out_dtypes=[F32], name="f_w_out")
    hn = _rms_fwd(h1, lp["mlp_norm_g"], "f_norm_mlp")
    a, r = _mm(hn, lp["w_ff1"], mode="nn", M=L, N=D_FF, K=D_MODEL, b_cb=True, epi=_relu2,
               out_dtypes=[_ACT, _ACT], name="f_ff1")
    h2 = _mm(r, lp["w_ff2"], mode="nn", M=L, N=D_MODEL, K=D_FF, extras=[(h1, 0)],
             epi=lambda acc, r_: (r_ + acc,), out_dtypes=[F32], name="f_ff2")
    hn3 = _rms_fwd(h2, lp["ple_norm_g"], "f_norm_ple")
    e = _mm(lp["p"], lp["w_ple_proj"], mode="nn", M=L, N=D_MODEL, K=PLE_DIM, b_cb=True, tk=PLE_DIM,
            out_dtypes=[F32], name="f_ple_proj")

    def gate_epi(acc, h2_, e_):
        g = _sigmoid(acc)
        return h2_ + g * e_, g

    h3, gate = _mm(hn3, lp["w_ple_gate"], mode="nn", M=L, N=D_MODEL, K=D_MODEL, extras=[(h2, 0), (e, 0)],
                   epi=gate_epi, out_dtypes=[F32, F32], name="f_ple_gate")
    saved = dict(h=h, xn=xn, z=z, ya=ya, q=q, k=k, yb=yb, S=S, y=y, yg=yg, g12=g12, ycat=ycat, h1=h1, hn=hn,
                 a=a, r=r, h2=h2, hn3=hn3, e=e, gate=gate)
    return h3, saved


def _layer_bwd(dh3, lp, sv, cos, sin):
    L = dh3.shape[0]
    z = sv["z"]
    dpre, de = _ple_bwd_elem(dh3, sv["gate"], sv["e"], "b_ple_elem")
    stk = lp["stk"]
    d_gate = _mm(sv["hn3"], dpre, mode="tn", M=D_MODEL, N=D_MODEL, K=L, out_dtypes=[F32], name="b_dw_gate",
                 o_stack=stk["w_ple_gate"])
    d_proj = _mm(lp["p"], de, mode="tn", M=PLE_DIM, N=D_MODEL, K=L, o_cb=True, tm=PLE_DIM,
                 out_dtypes=[F32], name="b_dw_proj", o_stack=stk["w_ple_proj"])
    dhn3 = _mm(dpre, lp["w_ple_gate"], mode="nt", M=L, N=D_MODEL, K=D_MODEL, out_dtypes=[F32], name="b_dx_gate")
    dh2, dg_ple = _rms_bwd(dhn3, sv["h2"], lp["ple_norm_g"], dh3, "b_norm_ple")
    da = _mm(dh2, lp["w_ff2"], mode="nt", M=L, N=D_FF, K=D_MODEL, extras=[(sv["a"], 0)],
             epi=lambda acc, a_: (acc * (2.0 * jnp.maximum(a_.astype(F32), 0.0)),), out_dtypes=[_ACT], name="b_dx_ff2")
    d_ff2 = _mm(sv["r"], dh2, mode="tn", M=D_FF, N=D_MODEL, K=L, out_dtypes=[F32], name="b_dw_ff2",
                o_stack=stk["w_ff2"])
    d_ff1 = _mm(sv["hn"], da, mode="tn", M=D_MODEL, N=D_FF, K=L, o_cb=True, out_dtypes=[F32], name="b_dw_ff1",
                o_stack=stk["w_ff1"])
    dhn = _mm(da, lp["w_ff1"], mode="nt", M=L, N=D_MODEL, K=D_FF, b_cb=True, out_dtypes=[F32], name="b_dx_ff1")
    dh1, dg_mlp = _rms_bwd(dhn, sv["h1"], lp["mlp_norm_g"], dh2, "b_norm_mlp")
    d_out = _mm(sv["ycat"], dh1, mode="tn", M=D_MODEL, N=D_MODEL, K=L, out_dtypes=[F32], name="b_dw_out",
                o_stack=stk["w_out"])
    dycat = _mm(dh1, lp["w_out"], mode="nt", M=L, N=D_MODEL, K=D_MODEL, out_dtypes=[F32], name="b_dx_out")
    dya, dyb, dg12, dmix = _merge_bwd(dycat, sv["ya"], sv["yb"], sv["g12"], lp["mix_out_g"], "b_merge")
    d_w12 = _mm(sv["yg"], dg12, mode="tn", M=C_WIDTH, N=2 * C_WIDTH, K=L, tm=C_WIDTH, out_dtypes=[F32], name="b_dw_glu",
                o_stack=stk["w12"])
    dy = _mm(dg12, lp["w12"], mode="nt", M=L, N=C_WIDTH, K=2 * C_WIDTH, tk=2 * C_WIDTH, extras=[(sv["y"], 0)],
             epi=lambda acc, y_: (acc * _gelu_grad(y_),), out_dtypes=[F32], name="b_dx_glu")
    dd = _dskip_bwd(dy, z, "b_dskip")
    dS = _mm(dy, lp["ccat"], mode="nt", M=L, N=2 * N_STATE, K=C_WIDTH, tk=C_WIDTH, out_dtypes=[F32], name="b_dx_ssm_out")
    d_ccat = _mm(sv["S"], dy, mode="tn", M=2 * N_STATE, N=C_WIDTH, K=L, out_dtypes=[F32], name="b_dw_ssm_out")
    G, dar, dai = _ssm_scan(dS, lp["lam_r"], lp["lam_i"], "b_ssm_scan", reverse=True, states=sv["S"])
    d_bcat = _mm(z, G, mode="tn", M=C_WIDTH, N=2 * N_STATE, K=L, a_off=5, tm=C_WIDTH, out_dtypes=[F32], name="b_dw_ssm_in")
    dzc = _mm(G, lp["bcat"], mode="nt", M=L, N=C_WIDTH, K=2 * N_STATE, tk=2 * N_STATE,
              extras=[(dy, 0), (lp["dskip"], 0)], epi=lambda acc, dy_, dsk: (acc + dy_ * dsk,),
              out_dtypes=[_ACT], name="b_dx_ssm_in")
    dq, dkc, dkp, dvc, dvp, dsink = _attn_bwd(sv["q"], sv["k"], z, lp["sinks"], dyb, "b_attn")
    dzq, dzk, dzv, dgq, dgk = _qk_prep_bwd(z, dq, dkc, dkp, dvc, dvp, cos, sin, lp["gq"], lp["gk"], "b_qk_prep")
    dza, dws, dbs, dlg, dlb = _gmlp_bwd(z, dya, lp["ws"], lp["wsT"], lp["bfull"], lp["lgf"], lp["lbf"], "b_gmlp")
    dz = jnp.concatenate([dza, dzq, dzk, dzv, dzc], axis=1)
    d_in = _mm(sv["xn"], dz, mode="tn", M=D_MODEL, N=IN_COLS, K=L, o_cb=True, out_dtypes=[F32], name="b_dw_in",
               o_stack=stk["w_in"])
    dxn = _mm(dz, lp["w_in"], mode="nt", M=L, N=D_MODEL, K=IN_COLS, b_cb=True, out_dtypes=[F32], name="b_dx_in")
    dh, dg_attn = _rms_bwd(dxn, sv["h"], lp["attn_norm_g"], dh1, "b_norm_attn")
    grads = dict(w_in=d_in, w12=d_w12, w_out=d_out, w_ff1=d_ff1, w_ff2=d_ff2, w_ple_gate=d_gate, w_ple_proj=d_proj,
                 attn_norm_g=dg_attn, mlp_norm_g=dg_mlp, ple_norm_g=dg_ple, mix_out_g=dmix.reshape(D_MODEL),
                 dws=dws, dbs=dbs, dlg=dlg, dlb=dlb, dgq=dgq, dgk=dgk, dsink=dsink,
                 dar=dar, dai=dai, d_bcat=d_bcat, d_ccat=d_ccat, dd=dd)
    return dh, grads


SMALL = ("attn_norm_g", "gmlp_ln_g", "gmlp_ln_b", "gmlp_ws", "gmlp_bs", "q_norm_g", "k_norm_g", "sinks",
         "ssm_a_re", "ssm_a_im", "ssm_log_dt", "ssm_b_re", "ssm_b_im", "ssm_c_re", "ssm_c_im", "ssm_d",
         "mix_out_g", "mlp_norm_g", "ple_norm_g")
BIG = ("w_in", "w12", "w_out", "w_ff1", "w_ff2", "w_ple_gate", "w_ple_proj")
COL_SHARDED = ("w_in", "w_ff1", "w_ple_proj")


def _block_diag(t):
    nl, g, a, b = t.shape
    eye = jnp.eye(g, dtype=t.dtype)
    return (t[:, :, :, None, :] * eye[None, :, None, :, None]).reshape(nl, g * a, g * b)


def _diag_blocks(t, a, b):
    nl = t.shape[0]
    t = t.reshape(nl, C_GROUPS, a, C_GROUPS, b)
    idx = jnp.arange(C_GROUPS)
    return jnp.moveaxis(t[:, idx, :, idx, :], 0, 1)


def _local_step(x, p, positions, target, sw, bw):
    nl = sw["attn_norm_g"].shape[0]
    G = nl * C_GROUPS
    zeros = lambda *s: jnp.zeros(s, F32)
    are = sw["ssm_a_re"].reshape(G, 1, C_STATE)
    aim = sw["ssm_a_im"].reshape(G, 1, C_STATE)
    ldt = jnp.broadcast_to(sw["ssm_log_dt"][..., None], (nl, C_GROUPS, C_STATE)).reshape(G, 1, C_STATE)
    bre = jnp.swapaxes(sw["ssm_b_re"], -1, -2).reshape(G, C_GROUP, C_STATE)
    bim = jnp.swapaxes(sw["ssm_b_im"], -1, -2).reshape(G, C_GROUP, C_STATE)
    lr, li, bbr, bbi = _ssm_prep(are, aim, ldt, bre, bim)
    unflat = lambda t: t.reshape(nl, C_GROUPS, C_GROUP, C_STATE)
    lp = dict(
        attn_norm_g=sw["attn_norm_g"], mlp_norm_g=sw["mlp_norm_g"], ple_norm_g=sw["ple_norm_g"],
        mix_out_g=sw["mix_out_g"], sinks=sw["sinks"],
        ws=sw["gmlp_ws"], wsT=jnp.swapaxes(sw["gmlp_ws"], -1, -2),
        bfull=jnp.concatenate([zeros(nl, A_HEADS, CHUNK, HEAD_DIM),
                               jnp.broadcast_to(sw["gmlp_bs"][..., None], (nl, A_HEADS, CHUNK, HEAD_DIM))], axis=-1),
        lgf=jnp.concatenate([zeros(nl, A_HEADS, HEAD_DIM), sw["gmlp_ln_g"]], axis=-1),
        lbf=jnp.concatenate([zeros(nl, A_HEADS, HEAD_DIM), sw["gmlp_ln_b"]], axis=-1),
        gq=jnp.tile(sw["q_norm_g"], (1, 2)).reshape(nl, 1, LANES),
        gk=jnp.tile(sw["k_norm_g"], (1, 2)).reshape(nl, 1, LANES),
        lam_r=lr.reshape(nl, 1, N_STATE), lam_i=li.reshape(nl, 1, N_STATE),
        bcat=jnp.concatenate([_block_diag(unflat(bbr)), _block_diag(unflat(bbi))], axis=-1),
        ccat=jnp.concatenate([_block_diag(jnp.swapaxes(sw["ssm_c_re"], -1, -2)),
                              -_block_diag(jnp.swapaxes(sw["ssm_c_im"], -1, -2))], axis=1),
        dskip=sw["ssm_d"].reshape(nl, 1, C_WIDTH))
    cos, sin = _rope_tables(positions)

    def layer_params(l, stk=None):
        lpi = {n: v[l] for n, v in lp.items()}
        lpi.update({n: (w, l) for n, w in bw.items()})
        lpi["p"] = (p, l)
        if stk is not None:
            lpi["stk"] = {n: (s, l) for n, s in stk.items()}
        return lpi

    h, saved = x, []
    for l in range(nl):
        h, sv = _layer_fwd(h, layer_params(l), cos, sin)
        saved.append(sv)
    sse, dh = _loss_fwd_bwd(h, target)

    stk = {n: lax.empty(w.shape, F32) for n, w in bw.items()}
    per_layer = [None] * nl
    for l in reversed(range(nl)):
        dh, gl = _layer_bwd(dh, layer_params(l, stk), saved[l], cos, sin)
        stk = {n: gl.pop(n) for n in BIG}
        per_layer[l] = gl
    grad_x = dh
    g = {n: jnp.stack([per_layer[l][n] for l in range(nl)]) for n in per_layer[0]}
    g.update(stk)

    d_bcat = g["d_bcat"]
    dbr = _diag_blocks(d_bcat[:, :, :N_STATE], C_GROUP, C_STATE).reshape(G, C_GROUP, C_STATE)
    dbi = _diag_blocks(d_bcat[:, :, N_STATE:], C_GROUP, C_STATE).reshape(G, C_GROUP, C_STATE)
    dlr = g["dar"][:, 0].reshape(G, 1, C_STATE)
    dli = g["dai"][:, 0].reshape(G, 1, C_STATE)
    g_are, g_aim, g_ldt, g_bre, g_bim = _ssm_prep_bwd(are, aim, ldt, bre, bim, dlr, dli, dbr, dbi)
    d_ccat = g["d_ccat"]
    sg = dict(
        attn_norm_g=g["attn_norm_g"], mlp_norm_g=g["mlp_norm_g"], ple_norm_g=g["ple_norm_g"], mix_out_g=g["mix_out_g"],
        gmlp_ln_g=g["dlg"][:, :, 0, HEAD_DIM:], gmlp_ln_b=g["dlb"][:, :, 0, HEAD_DIM:],
        gmlp_ws=g["dws"], gmlp_bs=g["dbs"][:, :, :, HEAD_DIM],
        q_norm_g=g["dgq"][:, 0, :HEAD_DIM], k_norm_g=g["dgk"][:, 0, :HEAD_DIM],
        sinks=g["dsink"][:, 0, :B_Q_HEADS],
        ssm_a_re=g_are.reshape(nl, C_GROUPS, C_STATE), ssm_a_im=g_aim.reshape(nl, C_GROUPS, C_STATE),
        ssm_log_dt=g_ldt[:, 0, 0].reshape(nl, C_GROUPS),
        ssm_b_re=jnp.swapaxes(g_bre.reshape(nl, C_GROUPS, C_GROUP, C_STATE), -1, -2),
        ssm_b_im=jnp.swapaxes(g_bim.reshape(nl, C_GROUPS, C_GROUP, C_STATE), -1, -2),
        ssm_c_re=jnp.swapaxes(_diag_blocks(d_ccat[:, :N_STATE], C_STATE, C_GROUP), -1, -2),
        ssm_c_im=-jnp.swapaxes(_diag_blocks(d_ccat[:, N_STATE:], C_STATE, C_GROUP), -1, -2),
        ssm_d=g["dd"].reshape(nl, C_GROUPS, C_GROUP),
    )
    bg = {n: g[n] for n in BIG}
    return sse[0, 0], grad_x, sg, bg


_ANY = pl.BlockSpec(memory_space=pl.ANY)
N_LAYERS = 4


def _mesh_pos():
    x, y, c = lax.axis_index("x"), lax.axis_index("y"), lax.axis_index("c")
    chips = [(1 - x, y), (x, 1 - y), (1 - x, 1 - y)]
    return x, y, c, 2 * x + y, chips


def _cast_into_slot(ws, j, name):
    nl, R, _ = ws[0].shape
    widths = [w.shape[2] for w in ws]
    C = sum(widths)
    tr = R if R <= 512 else _tile_rows(R, 512)
    nw = len(ws)

    def body(s_ref, *refs):
        o_ref = refs[nw]
        off = 0
        for r, wd in zip(refs[:nw], widths):
            o_ref[:, off:off + wd] = r[...].astype(o_ref.dtype)
            off += wd

    return pl.pallas_call(
        body, name=name,
        grid_spec=pltpu.PrefetchScalarGridSpec(
            num_scalar_prefetch=1, grid=(nl, R // tr),
            in_specs=[pl.BlockSpec((None, tr, wd), lambda l, i, s: (l, i, 0)) for wd in widths],
            out_specs=pl.BlockSpec((None, None, tr, C), lambda l, i, s: (l, s[0], i, 0))),
        out_shape=jax.ShapeDtypeStruct((nl, N_CHIPS, R, C), _MXU),
    )(jnp.reshape(j, (1,)).astype(jnp.int32), *ws)


def _gather_weights(bufs):
    nk = len(bufs)

    def body(*refs):
        ins, outs = refs[:nk], refs[nk:2 * nk]
        send_sems, recv_sems = refs[2 * nk:]
        x, y, c, j, chips = _mesh_pos()
        mine, other = pl.ds(2 * c, 2), pl.ds(2 * (1 - c), 2)

        def ici(t, q):
            cx, cy = chips[q]
            return pltpu.make_async_remote_copy(
                src_ref=ins[t].at[mine, j], dst_ref=outs[t].at[mine, j],
                send_sem=send_sems.at[6 * t + q], recv_sem=recv_sems.at[6 * t + q],
                device_id=(cx, cy, c), device_id_type=MESH)

        def landed(t, q):
            cx, cy = chips[q]
            blk = outs[t].at[mine, 2 * cx + cy]
            return pltpu.make_async_remote_copy(
                src_ref=blk, dst_ref=blk, send_sem=send_sems.at[6 * t + q], recv_sem=recv_sems.at[6 * t + q],
                device_id=(cx, cy, c), device_id_type=MESH)

        def fwd(t, q, rows):
            cx, cy = chips[q]
            blk = outs[t].at[rows, 2 * cx + cy]
            return pltpu.make_async_remote_copy(
                src_ref=blk, dst_ref=blk, send_sem=send_sems.at[6 * t + 3 + q], recv_sem=recv_sems.at[6 * t + 3 + q],
                device_id=(x, y, 1 - c), device_id_type=MESH)

        for t in range(nk):
            for q in range(3):
                ici(t, q).start()
        for t in range(nk):
            for q in range(3):
                landed(t, q).wait_recv()
                fwd(t, q, mine).start()
        for t in range(nk):
            for q in range(3):
                fwd(t, q, other).wait_recv()
        for t in range(nk):
            for q in range(3):
                ici(t, q).wait_send()
                fwd(t, q, mine).wait_send()

    return pl.pallas_call(
        body, name="gather_weights", in_specs=[_ANY] * nk, out_specs=[_ANY] * nk,
        out_shape=[jax.ShapeDtypeStruct(b.shape, b.dtype) for b in bufs],
        input_output_aliases={t: t for t in range(nk)},
        scratch_shapes=[pltpu.SemaphoreType.DMA((6 * nk,)), pltpu.SemaphoreType.DMA((6 * nk,))],
    )(*bufs)


def _exchange_sibling_half(gl):
    nk = len(gl)

    def body(*refs):
        ins, outs = refs[:nk], refs[nk:2 * nk]
        send_sems, recv_sems = refs[2 * nk:]
        x, y, c, _, _ = _mesh_pos()
        cps = [pltpu.make_async_remote_copy(
            src_ref=ins[t].at[pl.ds(2 * (1 - c), 2)], dst_ref=outs[t],
            send_sem=send_sems.at[t], recv_sem=recv_sems.at[t],
            device_id=(x, y, 1 - c), device_id_type=MESH) for t in range(nk)]
        for cp in cps:
            cp.start()
        for cp in cps:
            cp.wait()

    return pl.pallas_call(
        body, name="reduce_sibling", in_specs=[_ANY] * nk, out_specs=[_ANY] * nk,
        out_shape=[jax.ShapeDtypeStruct((2,) + g.shape[1:], g.dtype) for g in gl],
        scratch_shapes=[pltpu.SemaphoreType.DMA((nk,)), pltpu.SemaphoreType.DMA((nk,))],
    )(*gl)


def _exchange_chips(ps):
    nk = len(ps)

    def body(*refs):
        ins, outs = refs[:nk], refs[nk:2 * nk]
        send_sems, recv_sems = refs[2 * nk:]
        x, y, c, j, chips = _mesh_pos()

        def send(t, q):
            cx, cy = chips[q]
            return pltpu.make_async_remote_copy(
                src_ref=ins[t].at[:, 2 * cx + cy], dst_ref=outs[t].at[j],
                send_sem=send_sems.at[3 * t + q], recv_sem=recv_sems.at[3 * t + q],
                device_id=(cx, cy, c), device_id_type=MESH)

        def landed(t, q):
            cx, cy = chips[q]
            blk = outs[t].at[2 * cx + cy]
            return pltpu.make_async_remote_copy(
                src_ref=blk, dst_ref=blk, send_sem=send_sems.at[3 * t + q], recv_sem=recv_sems.at[3 * t + q],
                device_id=(cx, cy, c), device_id_type=MESH)

        for t in range(nk):
            for q in range(3):
                send(t, q).start()
        for t in range(nk):
            for q in range(3):
                landed(t, q).wait_recv()
        for t in range(nk):
            for q in range(3):
                send(t, q).wait_send()

    return pl.pallas_call(
        body, name="reduce_chips", in_specs=[_ANY] * nk, out_specs=[_ANY] * nk,
        out_shape=[jax.ShapeDtypeStruct((N_CHIPS, 2) + p.shape[2:], p.dtype) for p in ps],
        scratch_shapes=[pltpu.SemaphoreType.DMA((3 * nk,)), pltpu.SemaphoreType.DMA((3 * nk,))],
    )(*ps)


def _share_sibling(fs):
    nk = len(fs)

    def body(*refs):
        ins, outs = refs[:nk], refs[nk:2 * nk]
        send_sems, recv_sems = refs[2 * nk:]
        x, y, c, _, _ = _mesh_pos()
        mine = pl.ds(2 * c, 2)
        cps = [pltpu.make_async_remote_copy(
            src_ref=ins[t].at[mine], dst_ref=outs[t].at[mine], send_sem=send_sems.at[t], recv_sem=recv_sems.at[t],
            device_id=(x, y, 1 - c), device_id_type=MESH) for t in range(nk)]
        for cp in cps:
            cp.start()
        for cp in cps:
            cp.wait_send()
        for t in range(nk):
            blk = outs[t].at[pl.ds(2 * (1 - c), 2)]
            pltpu.make_async_remote_copy(
                src_ref=blk, dst_ref=blk, send_sem=send_sems.at[t], recv_sem=recv_sems.at[t],
                device_id=(x, y, 1 - c), device_id_type=MESH).wait_recv()

    return pl.pallas_call(
        body, name="share_sibling", in_specs=[_ANY] * nk, out_specs=[_ANY] * nk,
        out_shape=[jax.ShapeDtypeStruct(f.shape, f.dtype) for f in fs],
        input_output_aliases={t: t for t in range(nk)},
        scratch_shapes=[pltpu.SemaphoreType.DMA((nk,)), pltpu.SemaphoreType.DMA((nk,))],
    )(*fs)


def _add_own_half(gl, r1, c, name):
    _, ns, R, C = gl.shape
    rows = 2 * ns * R
    tr = _tile_rows(rows, 512)
    nblk = rows // tr

    def body(s_ref, a_ref, b_ref, o_ref):
        o_ref[...] = (a_ref[...] + b_ref[...]).astype(o_ref.dtype)

    out = pl.pallas_call(
        body, name=name,
        grid_spec=pltpu.PrefetchScalarGridSpec(
            num_scalar_prefetch=1, grid=(nblk,),
            in_specs=[pl.BlockSpec((tr, C), lambda i, s: (s[0] * nblk + i, 0)), pl.BlockSpec((tr, C), lambda i, s: (i, 0))],
            out_specs=pl.BlockSpec((tr, C), lambda i, s: (i, 0))),
        out_shape=jax.ShapeDtypeStruct((rows, C), _WIRE),
        compiler_params=_cparams(3 * _nbytes((tr, C), F32)),
    )(jnp.reshape(c, (1,)).astype(jnp.int32), gl.reshape(2 * rows, C), r1.reshape(rows, C))
    return out.reshape(2, ns, R, C)


def _add_chips(p, r2, j, c, name):
    _, ns, R, C = p.shape
    tr = R if R <= 512 else _tile_rows(R, 512)

    def body(s_ref, own, a1, a2, a3, o_ref):
        f = lambda r: r[...].astype(F32)
        o_ref[...] = ((f(own) + f(a1)) + f(a2)) + f(a3)

    blk = (None, None, tr, C)
    return pl.pallas_call(
        body, name=name,
        grid_spec=pltpu.PrefetchScalarGridSpec(
            num_scalar_prefetch=1, grid=(2, R // tr),
            in_specs=[pl.BlockSpec(blk, lambda h, i, s: (h, s[0], i, 0))]
            + [pl.BlockSpec(blk, lambda h, i, s, k=k: ((s[0] + k) % N_CHIPS, h, i, 0)) for k in (1, 2, 3)],
            out_specs=pl.BlockSpec((None, tr, C), lambda h, i, s: (2 * s[1] + h, i, 0))),
        out_shape=jax.ShapeDtypeStruct((N_LAYERS, R, C), F32),
        compiler_params=_cparams(6 * _nbytes((tr, C), F32)),
    )(jnp.stack([j, c]).astype(jnp.int32), p, r2, r2, r2)


def _allreduce_small(buf):
    Rs = buf.shape[0]

    def body(b_ref, o_ref, t_ref, slots_ref, send_sems, recv_sems):
        x, y, c, j, chips = _mesh_pos()
        sib = pltpu.make_async_remote_copy(
            src_ref=b_ref, dst_ref=t_ref, send_sem=send_sems.at[0], recv_sem=recv_sems.at[0],
            device_id=(x, y, 1 - c), device_id_type=MESH)
        sib.start()
        sib.wait()
        slots_ref[j] = b_ref[...] + t_ref[...]

        def send(q):
            cx, cy = chips[q]
            return pltpu.make_async_remote_copy(
                src_ref=slots_ref.at[j], dst_ref=slots_ref.at[j], send_sem=send_sems.at[1 + q],
                recv_sem=recv_sems.at[1 + q], device_id=(cx, cy, c), device_id_type=MESH)

        def landed(q):
            cx, cy = chips[q]
            blk = slots_ref.at[2 * cx + cy]
            return pltpu.make_async_remote_copy(
                src_ref=blk, dst_ref=blk, send_sem=send_sems.at[1 + q], recv_sem=recv_sems.at[1 + q],
                device_id=(cx, cy, c), device_id_type=MESH)

        for q in range(3):
            send(q).start()
        for q in range(3):
            landed(q).wait_recv()
        for q in range(3):
            send(q).wait_send()
        o_ref[...] = ((slots_ref[0] + slots_ref[1]) + slots_ref[2]) + slots_ref[3]

    vm = pl.BlockSpec(memory_space=pltpu.VMEM)
    return pl.pallas_call(
        body, name="allreduce_small", in_specs=[vm], out_specs=vm,
        out_shape=jax.ShapeDtypeStruct((Rs, LANES), F32),
        scratch_shapes=[pltpu.VMEM((Rs, LANES), F32), pltpu.VMEM((N_CHIPS, Rs, LANES), F32),
                        pltpu.SemaphoreType.DMA((4,)), pltpu.SemaphoreType.DMA((4,))],
        compiler_params=_cparams(4 * _nbytes((Rs, LANES), F32)),
    )(buf)


def _pack(d):
    flat = jnp.concatenate([d[n].reshape(-1) for n in SMALL])
    rows = -(-flat.shape[0] // (SUBLANES * LANES)) * SUBLANES
    return jnp.pad(flat, (0, rows * LANES - flat.shape[0])).reshape(rows, LANES)


def _unpack(buf, like):
    flat = buf.reshape(-1)
    out, off = {}, 0
    for n in SMALL:
        size = int(np.prod(like[n].shape))
        out[n] = flat[off:off + size].reshape(like[n].shape)
        off += size
    return out


ARGS = ("x", "p", "positions", "attn_norm_g", "w_in", "gmlp_ln_g", "gmlp_ln_b", "gmlp_ws", "gmlp_bs", "q_norm_g",
        "k_norm_g", "sinks", "ssm_a_re", "ssm_a_im", "ssm_log_dt", "ssm_b_re", "ssm_b_im", "ssm_c_re", "ssm_c_im",
        "ssm_d", "glu_w1", "glu_w2", "mix_out_g", "w_out", "mlp_norm_g", "w_ff1", "w_ff2", "ple_norm_g", "w_ple_gate",
        "w_ple_proj")
WEIGHTS = ARGS[3:]


def kernel(x, p, positions, attn_norm_g, w_in, gmlp_ln_g, gmlp_ln_b, gmlp_ws, gmlp_bs, q_norm_g, k_norm_g, sinks, ssm_a_re, ssm_a_im, ssm_log_dt, ssm_b_re, ssm_b_im, ssm_c_re, ssm_c_im, ssm_d, glu_w1, glu_w2, mix_out_g, w_out, mlp_norm_g, w_ff1, w_ff2, ple_norm_g, w_ple_gate, w_ple_proj, loss_target, m_attn_norm_g, m_w_in, m_gmlp_ln_g, m_gmlp_ln_b, m_gmlp_ws, m_gmlp_bs, m_q_norm_g, m_k_norm_g, m_sinks, m_ssm_a_re, m_ssm_a_im, m_ssm_log_dt, m_ssm_b_re, m_ssm_b_im, m_ssm_c_re, m_ssm_c_im, m_ssm_d, m_glu_w1, m_glu_w2, m_mix_out_g, m_w_out, m_mlp_norm_g, m_w_ff1, m_w_ff2, m_ple_norm_g, m_w_ple_gate, m_w_ple_proj, v_attn_norm_g, v_w_in, v_gmlp_ln_g, v_gmlp_ln_b, v_gmlp_ws, v_gmlp_bs, v_q_norm_g, v_k_norm_g, v_sinks, v_ssm_a_re, v_ssm_a_im, v_ssm_log_dt, v_ssm_b_re, v_ssm_b_im, v_ssm_c_re, v_ssm_c_im, v_ssm_d, v_glu_w1, v_glu_w2, v_mix_out_g, v_w_out, v_mlp_norm_g, v_w_ff1, v_w_ff2, v_ple_norm_g, v_w_ple_gate, v_w_ple_proj):
    a = dict(locals())
    L = a["x"].shape[1]
    nl = N_LAYERS
    c = lax.axis_index("c")
    j = 2 * lax.axis_index("x") + lax.axis_index("y")

    shards = dict(w_in=[a["w_in"]], w12=[a["glu_w1"], a["glu_w2"]], w_out=[a["w_out"]], w_ff1=[a["w_ff1"]],
                  w_ff2=[a["w_ff2"]], w_ple_gate=[a["w_ple_gate"]], w_ple_proj=[a["w_ple_proj"]])
    gathered = dict(zip(BIG, _gather_weights([_cast_into_slot(shards[n], j, "cast_" + n) for n in BIG])))
    bw = {n: (g if n in COL_SHARDED else g.reshape(nl, N_CHIPS * g.shape[2], g.shape[3])) for n, g in gathered.items()}

    sw = {n: a[n] for n in SMALL}
    sse, gx, sg, bg = _local_step(a["x"].reshape(L, D_MODEL), a["p"].reshape(nl, L, PLE_DIM),
                                  a["positions"].reshape(L), a["loss_target"].reshape(L, D_MODEL), sw, bw)
    loss = lax.psum(sse * (0.5 / D_MODEL), ("x", "y", "c"))

    gl = [bg[n] if n in COL_SHARDED else bg[n].reshape(nl, N_CHIPS, bg[n].shape[1] // N_CHIPS, bg[n].shape[2])
          for n in BIG]
    r1 = _exchange_sibling_half(gl)
    ps = [_add_own_half(g, r, c, "reduce_add_sibling_" + n) for g, r, n in zip(gl, r1, BIG)]
    r2 = _exchange_chips(ps)
    fs = [_add_chips(p_, r, j, c, "reduce_add_chips_" + n) for p_, r, n in zip(ps, r2, BIG)]
    big_grads = dict(zip(BIG, _share_sibling(fs)))
    g12 = big_grads.pop("w12")
    big_grads["glu_w1"], big_grads["glu_w2"] = g12[:, :, :C_WIDTH], g12[:, :, C_WIDTH:]

    small_grads = _unpack(_allreduce_small(_pack(sg)), sw)

    grads, delta, new_m, new_v = {}, {}, {}, {}
    d_s, m_s, v_s = _adamw(_pack(sw), _pack(small_grads), _pack({n: a["m_" + n] for n in SMALL}),
                           _pack({n: a["v_" + n] for n in SMALL}), "adamw_small")
    grads.update(small_grads)
    delta.update(_unpack(d_s, sw))
    new_m.update(_unpack(m_s, sw))
    new_v.update(_unpack(v_s, sw))
    for n, g in big_grads.items():
        shp = a[n].shape
        two_d = lambda t: t.reshape(shp[0] * shp[1], shp[2])
        d, m, v = _adamw(two_d(a[n]), two_d(g), two_d(a["m_" + n]), two_d(a["v_" + n]), "adamw_" + n)
        grads[n], delta[n], new_m[n], new_v[n] = g, d.reshape(shp), m.reshape(shp), v.reshape(shp)

    return (loss, gx.reshape(1, L, D_MODEL), *[grads[n] for n in WEIGHTS], *[delta[n] for n in WEIGHTS],
            *[new_m[n] for n in WEIGHTS], *[new_v[n] for n in WEIGHTS])
```

```python
import functools
import math

import numpy as np
import jax
import jax.numpy as jnp
from jax import lax
from jax.experimental import pallas as pl
from jax.experimental.pallas import tpu as pltpu

F32 = jnp.float32
_MXU = jnp.bfloat16
_ACT = jnp.bfloat16
_WIRE = jnp.bfloat16

D_MODEL = 1024
HEAD_DIM = 64
A_HEADS = 4
CHUNK = 128
B_Q_HEADS = 8
B_KV_HEADS = 2
B_GROUP = 4
WINDOW = 128
ROPE_THETA = 10000.0
C_WIDTH = 256
C_GROUP = 16
C_GROUPS = 16
C_STATE = 64
N_STATE = C_GROUPS * C_STATE
IN_A, IN_Q, IN_KV, IN_C = 512, 512, 128, 256
IN_COLS = 1536
D_FF = 4096
PLE_DIM = 256
EPS = 1e-6
NEG = -1e30
ADAM_LR, ADAM_B1, ADAM_B2, ADAM_EPS, ADAM_WD, ADAM_STEP = 0.001, 0.9, 0.999, 1e-08, 0.01, 10

LANES = 128
SUBLANES = 8
VMEM_BYTES = 64 * 2 ** 20
N_CHIPS = 4
MESH = pl.DeviceIdType.MESH


_MM_VMEM_BUDGET = 44 * 2 ** 20


def _vmem_limit(est_bytes):
    return int(min(max(2 * est_bytes + (8 << 20), 32 << 20), VMEM_BYTES - (6 << 20)))


def _cparams(est_bytes, **kw):
    return pltpu.CompilerParams(vmem_limit_bytes=_vmem_limit(est_bytes), **kw)


def _nbytes(shape, dtype):
    return int(np.prod(shape)) * jnp.dtype(dtype).itemsize


def _tile(dim, pref):
    t = min(dim, pref)
    while dim % t:
        t -= LANES
    assert t > 0, (dim, pref)
    return t


def _lane(shape):
    return lax.broadcasted_iota(jnp.int32, shape, len(shape) - 1)


def _row(shape):
    return lax.broadcasted_iota(jnp.int32, shape, len(shape) - 2)


def _gelu(x):
    c = math.sqrt(2.0 / math.pi)
    return 0.5 * x * (1.0 + jnp.tanh(c * (x + 0.044715 * (x * x * x))))


def _gelu_grad(x):
    c = math.sqrt(2.0 / math.pi)
    t = jnp.tanh(c * (x + 0.044715 * (x * x * x)))
    return 0.5 * (1.0 + t) + 0.5 * x * (1.0 - t * t) * (c * (1.0 + 3.0 * 0.044715 * (x * x)))


def _sigmoid(x):
    return 1.0 / (1.0 + jnp.exp(-x))


def _dot(a, b, dims=(((1,), (0,)), ((), ()))):
    return lax.dot_general(a.astype(_MXU), b.astype(_MXU), dims, preferred_element_type=F32)


_NT = (((1,), (1,)), ((), ()))
_TN = (((0,), (0,)), ((), ()))
_NN = (((1,), (0,)), ((), ()))


def _mm(a, b, *, mode, M, N, K, out_dtypes, name, epi=None, extras=(), b_cb=False, o_cb=False,
        a_off=0, b_off=0, tm=1024, tn=1024, tk=1024, a_lyr=None, b_lyr=None, o_stack=None, n_acc=0):
    if isinstance(a, tuple):
        a, a_lyr = a
    if isinstance(b, tuple):
        b, b_lyr = b
    if b_cb or o_cb:
        nc = (b.shape[-1] if b_cb else N // N_CHIPS)
    tn_nom = nc if ((mode == "nn" and b_cb) or (mode == "tn" and o_cb)) else _tile(N, tn)
    tk_nom = nc if (mode == "nt" and b_cb) else _tile(K, tk)
    item = lambda d: jnp.dtype(d).itemsize
    per_row = tk_nom * item(a.dtype) + tn_nom * (sum(item(d) for d in out_dtypes)
                                                   + sum(item(e.dtype) for e, _ in extras if e.shape[0] > 1))
    fixed = tk_nom * tn_nom * item(b.dtype)
    tm = _tile(M, tm)
    while tm > 256 and M % (tm // 2) == 0 and 2 * (tm * per_row + fixed) + 8 * tm * tn_nom > _MM_VMEM_BUDGET:
        tm //= 2

    def spec(block, imap, lyr=None):
        if lyr is None:
            return pl.BlockSpec(block, imap)
        return pl.BlockSpec((None,) + block, lambda i, j, k: (lyr,) + imap(i, j, k))

    if mode == "nn":
        if b_cb:
            tn = nc
        tm, tn, tk = _tile(M, tm), _tile(N, tn), _tile(K, tk)
        a_spec = spec((tm, tk), lambda i, j, k: (i, k + a_off), a_lyr)
        if b_cb:
            b_spec = spec((None, tk, tn), lambda i, j, k: (j, k, 0), b_lyr)
        else:
            b_spec = spec((tk, tn), lambda i, j, k: (k, j + b_off), b_lyr)
        dims = _NN
        a_blk, b_blk = (tm, tk), (tk, tn)
    elif mode == "nt":
        if b_cb:
            tk = nc
        tm, tn, tk = _tile(M, tm), _tile(N, tn), _tile(K, tk)
        a_spec = spec((tm, tk), lambda i, j, k: (i, k + a_off), a_lyr)
        if b_cb:
            b_spec = spec((None, tn, tk), lambda i, j, k: (k, j, 0), b_lyr)
        else:
            b_spec = spec((tn, tk), lambda i, j, k: (j, k + b_off), b_lyr)
        dims = _NT
        a_blk, b_blk = (tm, tk), (tn, tk)
    else:
        if o_cb:
            tn = nc
        tm, tn, tk = _tile(M, tm), _tile(N, tn), _tile(K, tk)
        a_spec = spec((tk, tm), lambda i, j, k: (k, i + a_off), a_lyr)
        b_spec = spec((tk, tn), lambda i, j, k: (k, j + b_off), b_lyr)
        dims = _TN
        a_blk, b_blk = (tk, tm), (tk, tn)
    gi, gj, gk = M // tm, N // tn, K // tk
    o_lyr = None if o_stack is None else o_stack[1]
    if o_cb:
        o_spec = spec((None, tm, tn), lambda i, j, k: (j, i, 0), o_lyr)
        o_shape = (gj, M, tn)
    else:
        o_spec = spec((tm, tn), lambda i, j, k: (i, j), o_lyr)
        o_shape = (M, N)
    e_specs = []
    for e, off in extras:
        if e.shape[0] == 1:
            e_specs.append(pl.BlockSpec((1, tn), lambda i, j, k, off=off: (0, j + off)))
        else:
            e_specs.append(pl.BlockSpec((tm, tn), lambda i, j, k, off=off: (i, j + off)))
    extras = [e for e, _ in extras]
    ne, no = len(extras), len(out_dtypes)
    operands = [a, b, *extras]
    in_specs = [a_spec, b_spec] + e_specs
    out_shape = [jax.ShapeDtypeStruct(o_shape, d) for d in out_dtypes]
    aliases = {}
    if o_stack is not None:
        assert no == 1 and o_stack[0].shape[1:] == o_shape and o_stack[0].dtype == out_dtypes[0]
        operands.append(o_stack[0])
        in_specs.append(pl.BlockSpec(memory_space=pl.ANY))
        out_shape = [jax.ShapeDtypeStruct(o_stack[0].shape, o_stack[0].dtype)]
        aliases = {len(operands) - 1: 0}
    nin = len(operands)
    out_specs = [o_spec] * no
    if n_acc:
        assert gj == 1 and o_stack is None
        out_specs[no - n_acc:] = [pl.BlockSpec((1, tn), lambda i, j, k: (0, 0))] * n_acc
        out_shape[no - n_acc:] = [jax.ShapeDtypeStruct((1, N), d) for d in out_dtypes[no - n_acc:]]

    def body(*refs):
        a_ref, b_ref = refs[0], refs[1]
        e_refs = refs[2:2 + ne]
        o_refs = refs[nin:nin + no]
        first_rows = pl.program_id(0) == 0

        def fin(acc):
            vals = epi(acc, *[e[...] for e in e_refs]) if epi is not None else (acc,)
            for t, (o, v) in enumerate(zip(o_refs, vals)):
                if t < no - n_acc:
                    o[...] = v.astype(o.dtype)
                else:
                    @pl.when(first_rows)
                    def _():
                        o[...] = jnp.zeros_like(o)

                    o[...] += v.astype(o.dtype)

        prod = _dot(a_ref[...], b_ref[...], dims)
        if gk == 1:
            fin(prod)
        else:
            acc_ref = refs[-1]
            k = pl.program_id(2)

            @pl.when(k == 0)
            def _():
                acc_ref[...] = prod

            @pl.when(k > 0)
            def _():
                acc_ref[...] += prod

            @pl.when(k == gk - 1)
            def _():
                fin(acc_ref[...])

    est = (_nbytes(a_blk, a.dtype) + _nbytes(b_blk, b.dtype)
           + sum(_nbytes((tm, tn), d) for d in out_dtypes)
           + sum(_nbytes((tm, tn), e.dtype) for e in extras)) + 2 * _nbytes((tm, tn), F32)
    outs = pl.pallas_call(
        body, name=name, grid=(gi, gj, gk),
        in_specs=in_specs,
        out_specs=out_specs,
        out_shape=out_shape,
        scratch_shapes=([pltpu.VMEM((tm, tn), F32)] if gk > 1 else []),
        input_output_aliases=aliases,
        compiler_params=_cparams(est, dimension_semantics=(("arbitrary" if n_acc else "parallel"), "parallel", "arbitrary")),
    )(*operands)
    return outs if no > 1 else outs[0]


_TL = 512


def _rms_fwd(h, g, name):
    L, D = h.shape
    tl = _tile(L, _TL)

    def body(h_ref, g_ref, o_ref):
        x = h_ref[...]
        r = lax.rsqrt(jnp.mean(x * x, axis=-1, keepdims=True) + EPS)
        o_ref[...] = ((x * r) * g_ref[...]).astype(o_ref.dtype)

    return pl.pallas_call(
        body, name=name, grid=(L // tl,),
        in_specs=[pl.BlockSpec((tl, D), lambda i: (i, 0)), pl.BlockSpec((1, D), lambda i: (0, 0))],
        out_specs=pl.BlockSpec((tl, D), lambda i: (i, 0)),
        out_shape=jax.ShapeDtypeStruct((L, D), _ACT),
        compiler_params=_cparams(3 * _nbytes((tl, D), F32)),
    )(h, g.reshape(1, D))


def _rms_bwd(dxn, h, g, dres, name):
    L, D = h.shape
    tl = _tile(L, _TL)

    def body(d_ref, h_ref, g_ref, r_ref, o_ref, dg_ref):
        x = h_ref[...]
        r = lax.rsqrt(jnp.mean(x * x, axis=-1, keepdims=True) + EPS)
        xhat = x * r
        d = d_ref[...].astype(F32)
        gy = d * g_ref[...]
        dx = r * (gy - xhat * jnp.mean(gy * xhat, axis=-1, keepdims=True))
        o_ref[...] = r_ref[...] + dx

        @pl.when(pl.program_id(0) == 0)
        def _():
            dg_ref[...] = jnp.zeros_like(dg_ref)

        dg_ref[...] += jnp.sum(d * xhat, axis=0, keepdims=True)

    dh, dg = pl.pallas_call(
        body, name=name, grid=(L // tl,),
        in_specs=[pl.BlockSpec((tl, D), lambda i: (i, 0)), pl.BlockSpec((tl, D), lambda i: (i, 0)),
                  pl.BlockSpec((1, D), lambda i: (0, 0)), pl.BlockSpec((tl, D), lambda i: (i, 0))],
        out_specs=[pl.BlockSpec((tl, D), lambda i: (i, 0)), pl.BlockSpec((1, D), lambda i: (0, 0))],
        out_shape=[jax.ShapeDtypeStruct((L, D), F32), jax.ShapeDtypeStruct((1, D), F32)],
        compiler_params=_cparams(5 * _nbytes((tl, D), F32)),
    )(dxn, h, g.reshape(1, D), dres)
    return dh, dg.reshape(D)


def _rope_tables(positions):
    L = positions.shape[0]
    tl = _tile(L, 1024)
    inv = 1.0 / (ROPE_THETA ** (np.arange(0, HEAD_DIM, 2, dtype=np.float32) / HEAD_DIM))
    inv128 = jnp.asarray(np.tile(inv.astype(np.float32), 4).reshape(1, LANES))

    def body(p_ref, i_ref, c_ref, s_ref):
        ang = p_ref[...].astype(F32) * i_ref[...]
        c_ref[...] = jnp.cos(ang)
        s_ref[...] = jnp.sin(ang)

    return pl.pallas_call(
        body, name="rope_tables", grid=(L // tl,),
        in_specs=[pl.BlockSpec((tl, 1), lambda i: (i, 0)), pl.BlockSpec((1, LANES), lambda i: (0, 0))],
        out_specs=[pl.BlockSpec((tl, LANES), lambda i: (i, 0))] * 2,
        out_shape=[jax.ShapeDtypeStruct((L, LANES), F32)] * 2,
    )(positions.reshape(L, 1), inv128)


_GM_TL = 256


def _gmlp_head(Z, W, bfull, lg, lb, maskv):
    G = _gelu(Z)
    mu = jnp.sum(jnp.where(maskv, G, 0.0), axis=-1, keepdims=True) * (1.0 / HEAD_DIM)
    xc = jnp.where(maskv, G - mu, 0.0)
    var = jnp.sum(xc * xc, axis=-1, keepdims=True) * (1.0 / HEAD_DIM)
    rstd = lax.rsqrt(var + EPS)
    xhat = xc * rstd
    vn = xhat * lg + lb
    sv = _dot(W, vn) + bfull
    return G, xhat, rstd, vn, sv


def _tril(W):
    return jnp.where(_row(W.shape) >= _lane(W.shape), W, 0.0)


def _triu(W):
    return jnp.where(_row(W.shape) <= _lane(W.shape), W, 0.0)


def _gmlp_fwd(z, ws, bfull, lgf, lbf, name):
    L = z.shape[0]
    tl = _tile(L, _GM_TL)
    nch = tl // CHUNK

    def body(z_ref, w_ref, b_ref, lg_ref, lb_ref, o_ref):
        maskv = _lane((CHUNK, LANES)) >= HEAD_DIM
        for c in range(nch):
            rows = slice(c * CHUNK, (c + 1) * CHUNK)
            for hp in range(A_HEADS // 2):
                acc = None
                for hh in range(2):
                    h = 2 * hp + hh
                    Z = z_ref[rows, h * LANES:(h + 1) * LANES]
                    G, _, _, _, sv = _gmlp_head(Z, _tril(w_ref[h]), b_ref[h], lg_ref[h:h + 1, :], lb_ref[h:h + 1, :], maskv)
                    prod = G * pltpu.roll(sv, HEAD_DIM, axis=1)
                    acc = prod if hh == 0 else acc + pltpu.roll(prod, HEAD_DIM, axis=1)
                o_ref[rows, hp * LANES:(hp + 1) * LANES] = acc

    return pl.pallas_call(
        body, name=name, grid=(L // tl,),
        in_specs=[pl.BlockSpec((tl, IN_A), lambda i: (i, 0)),
                  pl.BlockSpec((A_HEADS, CHUNK, CHUNK), lambda i: (0, 0, 0)),
                  pl.BlockSpec((A_HEADS, CHUNK, LANES), lambda i: (0, 0, 0)),
                  pl.BlockSpec((A_HEADS, LANES), lambda i: (0, 0)),
                  pl.BlockSpec((A_HEADS, LANES), lambda i: (0, 0))],
        out_specs=pl.BlockSpec((tl, 2 * LANES), lambda i: (i, 0)),
        out_shape=jax.ShapeDtypeStruct((L, 2 * LANES), F32),
    )(z, ws, bfull, lgf, lbf)


def _gmlp_bwd(z, dya, ws, wsT, bfull, lgf, lbf, name):
    L = z.shape[0]
    tl = _tile(L, _GM_TL)
    nch = tl // CHUNK
    nsteps = L // tl

    def body(z_ref, d_ref, w_ref, wt_ref, b_ref, lg_ref, lb_ref, dz_ref, dw_ref, db_ref, dlg_ref, dlb_ref):
        step = pl.program_id(0)

        @pl.when(step == 0)
        def _():
            dw_ref[...] = jnp.zeros_like(dw_ref)
            db_ref[...] = jnp.zeros_like(db_ref)
            dlg_ref[...] = jnp.zeros_like(dlg_ref)
            dlb_ref[...] = jnp.zeros_like(dlb_ref)

        lane = _lane((CHUNK, LANES))
        maskv = lane >= HEAD_DIM
        for c in range(nch):
            rows = slice(c * CHUNK, (c + 1) * CHUNK)
            for h in range(A_HEADS):
                hp, hh = divmod(h, 2)
                Z = z_ref[rows, h * LANES:(h + 1) * LANES]
                lg = lg_ref[h:h + 1, :]
                G, xhat, rstd, vn, sv = _gmlp_head(Z, _tril(w_ref[h]), b_ref[h], lg, lb_ref[h:h + 1, :], maskv)
                dpair = d_ref[rows, hp * LANES:(hp + 1) * LANES]
                if hh == 1:
                    dpair = pltpu.roll(dpair, HEAD_DIM, axis=1)
                dout = jnp.where(maskv, 0.0, dpair)
                du = dout * pltpu.roll(sv, HEAD_DIM, axis=1)
                dsv = pltpu.roll(dout * G, HEAD_DIM, axis=1)
                dw_ref[h] += _tril(_dot(dsv, vn, _NT))
                db_ref[h] += dsv
                dvn = _dot(_triu(wt_ref[h]), dsv)
                dlg_ref[h] += dvn * xhat
                dlb_ref[h] += dvn
                dxh = dvn * lg
                m1 = jnp.sum(dxh, axis=-1, keepdims=True) * (1.0 / HEAD_DIM)
                m2 = jnp.sum(dxh * xhat, axis=-1, keepdims=True) * (1.0 / HEAD_DIM)
                dv = jnp.where(maskv, rstd * (dxh - m1 - xhat * m2), 0.0)
                dz_ref[rows, h * LANES:(h + 1) * LANES] = ((du + dv) * _gelu_grad(Z)).astype(dz_ref.dtype)

        @pl.when(step == nsteps - 1)
        def _():
            for h in range(A_HEADS):
                db_ref[h] = jnp.broadcast_to(jnp.sum(db_ref[h], axis=1, keepdims=True), (CHUNK, LANES))
                dlg_ref[h] = jnp.broadcast_to(jnp.sum(dlg_ref[h], axis=0, keepdims=True), (CHUNK, LANES))
                dlb_ref[h] = jnp.broadcast_to(jnp.sum(dlb_ref[h], axis=0, keepdims=True), (CHUNK, LANES))

    full3 = pl.BlockSpec((A_HEADS, CHUNK, LANES), lambda i: (0, 0, 0))
    return pl.pallas_call(
        body, name=name, grid=(nsteps,),
        in_specs=[pl.BlockSpec((tl, IN_A), lambda i: (i, 0)),
                  pl.BlockSpec((tl, 2 * LANES), lambda i: (i, 0)),
                  full3, full3, full3,
                  pl.BlockSpec((A_HEADS, LANES), lambda i: (0, 0)),
                  pl.BlockSpec((A_HEADS, LANES), lambda i: (0, 0))],
        out_specs=[pl.BlockSpec((tl, IN_A), lambda i: (i, 0)), full3, full3, full3, full3],
        out_shape=[jax.ShapeDtypeStruct((L, IN_A), _ACT)] + [jax.ShapeDtypeStruct((A_HEADS, CHUNK, LANES), F32)] * 4,
    )(z, dya, ws, wsT, bfull, lgf, lbf)


def _head_rstd(x, lo):
    sq = x * x
    s_lo = jnp.sum(jnp.where(lo, sq, 0.0), axis=-1, keepdims=True)
    s_hi = jnp.sum(jnp.where(lo, 0.0, sq), axis=-1, keepdims=True)
    return jnp.where(lo, lax.rsqrt(s_lo * (1.0 / HEAD_DIM) + EPS), lax.rsqrt(s_hi * (1.0 / HEAD_DIM) + EPS))


def _rot_half(x, first):
    return jnp.where(first, -pltpu.roll(x, LANES - HEAD_DIM // 2, axis=1), pltpu.roll(x, HEAD_DIM // 2, axis=1))


def _qk_prep(z, cos, sin, gq, gk, name):
    L = z.shape[0]
    tl = _tile(L, _TL)
    nq = IN_Q // LANES

    def body(q_ref, k_ref, c_ref, s_ref, gq_ref, gk_ref, qo_ref, ko_ref):
        lane = _lane((tl, LANES))
        lo = lane < HEAD_DIM
        first = (lane % HEAD_DIM) < (HEAD_DIM // 2)
        c, s = c_ref[...], s_ref[...]

        def prep(x, g):
            xn = (x * _head_rstd(x, lo)) * g
            return xn * c + _rot_half(xn, first) * s

        for j in range(nq):
            qo_ref[:, j * LANES:(j + 1) * LANES] = prep(q_ref[:, j * LANES:(j + 1) * LANES], gq_ref[...]).astype(qo_ref.dtype)
        ko_ref[...] = prep(k_ref[...], gk_ref[...]).astype(ko_ref.dtype)

    return pl.pallas_call(
        body, name=name, grid=(L // tl,),
        in_specs=[pl.BlockSpec((tl, IN_Q), lambda i: (i, 1)),
                  pl.BlockSpec((tl, IN_KV), lambda i: (i, 8)),
                  pl.BlockSpec((tl, LANES), lambda i: (i, 0)), pl.BlockSpec((tl, LANES), lambda i: (i, 0)),
                  pl.BlockSpec((1, LANES), lambda i: (0, 0)), pl.BlockSpec((1, LANES), lambda i: (0, 0))],
        out_specs=[pl.BlockSpec((tl, IN_Q), lambda i: (i, 0)), pl.BlockSpec((tl, IN_KV), lambda i: (i, 0))],
        out_shape=[jax.ShapeDtypeStruct((L, IN_Q), _ACT), jax.ShapeDtypeStruct((L, IN_KV), _ACT)],
    )(z, z, cos, sin, gq, gk)


def _qk_prep_bwd(z, dq, dkc, dkp, dvc, dvp, cos, sin, gq, gk, name):
    L = z.shape[0]
    tl = _ATT_QB * WINDOW
    nb = L // tl
    nq = IN_Q // LANES

    def body(q_ref, k_ref, dq_ref, dkc_ref, dkp_ref, dvc_ref, dvp_ref, c_ref, s_ref, gq_ref, gk_ref,
             dzq_ref, dzk_ref, dzv_ref, dgq_ref, dgk_ref):
        n = pl.program_id(0)

        @pl.when(n == 0)
        def _():
            dgq_ref[...] = jnp.zeros_like(dgq_ref)
            dgk_ref[...] = jnp.zeros_like(dgk_ref)

        lane = _lane((tl, LANES))
        lo = lane < HEAD_DIM
        first = (lane % HEAD_DIM) < (HEAD_DIM // 2)
        c, s = c_ref[...], s_ref[...]
        has_next = jnp.where(n < nb - 1, 1.0, 0.0)

        def bwd(x, g, dy):
            r = _head_rstd(x, lo)
            xhat = x * r
            dxn = dy * c - _rot_half(dy * s, first)
            gy = dxn * g
            t = gy * xhat
            m_lo = jnp.sum(jnp.where(lo, t, 0.0), axis=-1, keepdims=True)
            m_hi = jnp.sum(jnp.where(lo, 0.0, t), axis=-1, keepdims=True)
            m = jnp.where(lo, m_lo, m_hi) * (1.0 / HEAD_DIM)
            dx = r * (gy - xhat * m)
            dg = jnp.sum(dxn * xhat, axis=0, keepdims=True)
            return dx, dg

        dgq = jnp.zeros((1, LANES), F32)
        for j in range(nq):
            sl = slice(j * LANES, (j + 1) * LANES)
            dx, dg = bwd(q_ref[:, sl], gq_ref[...], dq_ref[:, sl].astype(F32))
            dzq_ref[:, sl] = dx.astype(dzq_ref.dtype)
            dgq = dgq + dg
        dgq_ref[...] += dgq + pltpu.roll(dgq, HEAD_DIM, axis=1)
        def with_next(cur_ref, nxt_ref):
            head = jnp.zeros((tl - WINDOW, IN_KV), F32)
            return cur_ref[...] + jnp.concatenate([head, has_next * nxt_ref[...]], axis=0)

        dx, dg = bwd(k_ref[...], gk_ref[...], with_next(dkc_ref, dkp_ref))
        dzk_ref[...] = dx.astype(dzk_ref.dtype)
        dgk_ref[...] += dg + pltpu.roll(dg, HEAD_DIM, axis=1)
        dzv_ref[...] = with_next(dvc_ref, dvp_ref).astype(dzv_ref.dtype)

    nxt = lambda i: (jnp.minimum(i + 1, nb - 1), 0)
    cur = lambda i: (i, 0)
    kv = pl.BlockSpec((tl, IN_KV), cur)
    kvn = pl.BlockSpec((WINDOW, IN_KV), nxt)
    one = pl.BlockSpec((1, LANES), lambda i: (0, 0))
    return pl.pallas_call(
        body, name=name, grid=(nb,),
        in_specs=[pl.BlockSpec((tl, IN_Q), lambda i: (i, 1)), pl.BlockSpec((tl, IN_KV), lambda i: (i, 8)),
                  pl.BlockSpec((tl, IN_Q), cur), kv, kvn, kv, kvn,
                  kv, kv, one, one],
        out_specs=[pl.BlockSpec((tl, IN_Q), cur), kv, kv, one, one],
        out_shape=[jax.ShapeDtypeStruct((L, IN_Q), _ACT), jax.ShapeDtypeStruct((L, IN_KV), _ACT),
                   jax.ShapeDtypeStruct((L, IN_KV), _ACT), jax.ShapeDtypeStruct((1, LANES), F32),
                   jax.ShapeDtypeStruct((1, LANES), F32)],
    )(z, z, dq, dkc, dkp, dvc, dvp, cos, sin, gq, gk)


def _attn_mask(n):
    shp = (2 * WINDOW, B_GROUP * WINDOW)
    qi = _lane(shp) % WINDOW
    kj = _row(shp)
    off = 0 if n is None else jnp.where(n > 0, 0, 4 * WINDOW)
    return ((kj >= WINDOW) & (kj - WINDOW <= qi)) | ((kj < WINDOW) & (kj > qi + off))


def _kv_lanes(j):
    lane = _lane((WINDOW, LANES))
    return (lane >= j * HEAD_DIM) & (lane < (j + 1) * HEAD_DIM)


_ATT_QB = 4


def _stack_heads(ref, rows, j, kvl):
    parts = []
    for g in range(B_GROUP):
        h = j * B_GROUP + g
        slab = ref[rows, (h // 2) * LANES:(h // 2 + 1) * LANES].astype(F32)
        if (h % 2) != j:
            slab = pltpu.roll(slab, HEAD_DIM, axis=1)
        parts.append(jnp.where(kvl, slab, 0.0))
    return jnp.concatenate(parts, axis=0)


def _attn_probs(qs, k2, sink_row, mask):
    s = _dot(k2, qs, _NT) * (HEAD_DIM ** -0.5)
    s = jnp.where(mask, s, NEG)
    m = jnp.maximum(jnp.max(s, axis=0, keepdims=True), sink_row)
    p = jnp.exp(s - m)
    esink = jnp.exp(sink_row - m)
    inv = 1.0 / (jnp.sum(p, axis=0, keepdims=True) + esink)
    return p * inv, esink * inv


def _sink_row(sink_ref, j):
    lane = _lane((1, B_GROUP * WINDOW))
    row = jnp.full((1, B_GROUP * WINDOW), sink_ref[j * B_GROUP], F32)
    for g in range(1, B_GROUP):
        row = jnp.where(lane >= g * WINDOW, sink_ref[j * B_GROUP + g], row)
    return row


def _attn_fwd(q, k, z, sinks, name):
    L = q.shape[0]
    QB = _ATT_QB
    tq = QB * WINDOW
    prev = lambda n: (jnp.maximum(QB * n - 1, 0), 0)
    prev_v = lambda n: (jnp.maximum(QB * n - 1, 0), 9)

    def body(s_ref, q_ref, kp_ref, kc_ref, vp_ref, vc_ref, o_ref):
        n = pl.program_id(0)
        k3 = jnp.concatenate([kp_ref[...], kc_ref[...]], axis=0)
        v3 = jnp.concatenate([vp_ref[...], vc_ref[...]], axis=0)
        for b in range(QB):
            rows = slice(b * WINDOW, (b + 1) * WINDOW)
            mask = _attn_mask(n if b == 0 else None)
            k2 = k3[b * WINDOW:(b + 2) * WINDOW]
            v2 = v3[b * WINDOW:(b + 2) * WINDOW]
            slabs = [None] * (IN_Q // LANES)
            for j in range(B_KV_HEADS):
                kvl = _kv_lanes(j)
                qs = _stack_heads(q_ref, rows, j, kvl)
                pn, _ = _attn_probs(qs, k2, _sink_row(s_ref, j), mask)
                o = _dot(pn, v2, _TN)
                for g in range(B_GROUP):
                    h = j * B_GROUP + g
                    piece = jnp.where(kvl, o[g * WINDOW:(g + 1) * WINDOW], 0.0)
                    if (h % 2) != j:
                        piece = pltpu.roll(piece, HEAD_DIM, axis=1)
                    slabs[h // 2] = piece if slabs[h // 2] is None else slabs[h // 2] + piece
            for t, sl in enumerate(slabs):
                o_ref[rows, t * LANES:(t + 1) * LANES] = sl

    return pl.pallas_call(
        body, name=name, grid=(L // tq,),
        in_specs=[pl.BlockSpec(memory_space=pltpu.SMEM),
                  pl.BlockSpec((tq, IN_Q), lambda n: (n, 0)),
                  pl.BlockSpec((WINDOW, IN_KV), prev), pl.BlockSpec((tq, IN_KV), lambda n: (n, 0)),
                  pl.BlockSpec((WINDOW, IN_KV), prev_v), pl.BlockSpec((tq, IN_KV), lambda n: (n, 9))],
        out_specs=pl.BlockSpec((tq, IN_Q), lambda n: (n, 0)),
        out_shape=jax.ShapeDtypeStruct((L, IN_Q), F32),
    )(sinks, q, k, k, z, z)


def _attn_bwd(q, k, z, sinks, dyb, name):
    L = q.shape[0]
    QB = _ATT_QB
    tq = QB * WINDOW
    nsteps = L // tq
    prev = lambda n: (jnp.maximum(QB * n - 1, 0), 0)
    prev_v = lambda n: (jnp.maximum(QB * n - 1, 0), 9)
    cur = lambda n: (n, 0)

    def body(s_ref, q_ref, kp_ref, kc_ref, vp_ref, vc_ref, d_ref, dq_ref, dkc_ref, dkp_ref, dvc_ref, dvp_ref, ds_ref):
        n = pl.program_id(0)

        @pl.when(n == 0)
        def _():
            ds_ref[...] = jnp.zeros_like(ds_ref)

        k3 = jnp.concatenate([kp_ref[...], kc_ref[...]], axis=0)
        v3 = jnp.concatenate([vp_ref[...], vc_ref[...]], axis=0)
        dkb = [None] * (QB + 1)
        dvb = [None] * (QB + 1)
        dsink = jnp.zeros((1, LANES), F32)
        lane1 = _lane((1, LANES))
        add = lambda acc, v: v if acc is None else acc + v
        for b in range(QB):
            rows = slice(b * WINDOW, (b + 1) * WINDOW)
            mask = _attn_mask(n if b == 0 else None)
            k2 = k3[b * WINDOW:(b + 2) * WINDOW]
            v2 = v3[b * WINDOW:(b + 2) * WINDOW]
            slabs = [None] * (IN_Q // LANES)
            for j in range(B_KV_HEADS):
                kvl = _kv_lanes(j)
                qs = _stack_heads(q_ref, rows, j, kvl)
                dos = _stack_heads(d_ref, rows, j, kvl)
                pn, psink = _attn_probs(qs, k2, _sink_row(s_ref, j), mask)
                dp = _dot(v2, dos, _NT)
                dd = jnp.sum(pn * dp, axis=0, keepdims=True)
                dss = (pn * (dp - dd)) * (HEAD_DIM ** -0.5)
                dqs = _dot(dss, k2, _TN)
                dk2 = _dot(dss, qs)
                dv2 = _dot(pn, dos)
                dkb[b], dkb[b + 1] = add(dkb[b], dk2[:WINDOW]), add(dkb[b + 1], dk2[WINDOW:])
                dvb[b], dvb[b + 1] = add(dvb[b], dv2[:WINDOW]), add(dvb[b + 1], dv2[WINDOW:])
                sd = psink * dd
                for g in range(B_GROUP):
                    h = j * B_GROUP + g
                    piece = jnp.where(kvl, dqs[g * WINDOW:(g + 1) * WINDOW], 0.0)
                    if (h % 2) != j:
                        piece = pltpu.roll(piece, HEAD_DIM, axis=1)
                    slabs[h // 2] = piece if slabs[h // 2] is None else slabs[h // 2] + piece
                    tot = jnp.sum(sd[:, g * WINDOW:(g + 1) * WINDOW], axis=1, keepdims=True)
                    dsink = dsink - jnp.where(lane1 == h, tot, 0.0)
            for t, sl in enumerate(slabs):
                dq_ref[rows, t * LANES:(t + 1) * LANES] = sl
        dkp_ref[...] = dkb[0]
        dvp_ref[...] = dvb[0]
        for b in range(QB):
            dkc_ref[b * WINDOW:(b + 1) * WINDOW, :] = dkb[b + 1]
            dvc_ref[b * WINDOW:(b + 1) * WINDOW, :] = dvb[b + 1]
        ds_ref[0:1, :] += dsink

    kvs = pl.BlockSpec((tq, IN_KV), cur)
    kvp = pl.BlockSpec((WINDOW, IN_KV), cur)
    kvo = jax.ShapeDtypeStruct((L, IN_KV), F32)
    kvpo = jax.ShapeDtypeStruct((nsteps * WINDOW, IN_KV), F32)
    return pl.pallas_call(
        body, name=name, grid=(nsteps,),
        in_specs=[pl.BlockSpec(memory_space=pltpu.SMEM),
                  pl.BlockSpec((tq, IN_Q), cur),
                  pl.BlockSpec((WINDOW, IN_KV), prev), kvs,
                  pl.BlockSpec((WINDOW, IN_KV), prev_v), pl.BlockSpec((tq, IN_KV), lambda n: (n, 9)),
                  pl.BlockSpec((tq, IN_Q), cur)],
        out_specs=[pl.BlockSpec((tq, IN_Q), cur), kvs, kvp, kvs, kvp, pl.BlockSpec((SUBLANES, LANES), lambda n: (0, 0))],
        out_shape=[jax.ShapeDtypeStruct((L, IN_Q), F32), kvo, kvpo, kvo, kvpo, jax.ShapeDtypeStruct((SUBLANES, LANES), F32)],
    )(sinks, q, k, k, z, z, dyb)


def _ssm_disc(are, aim, ldt, bre, bim):
    dt = jnp.exp(ldt)
    mag = jnp.exp(are * dt)
    lr, li = mag * jnp.cos(aim * dt), mag * jnp.sin(aim * dt)
    den = are * are + aim * aim
    xr, xi = lr - 1.0, li
    cr, ci = (xr * are + xi * aim) / den, (xi * are - xr * aim) / den
    return lr, li, cr * bre - ci * bim, cr * bim + ci * bre


def _ssm_prep(are, aim, ldt, bre, bim):
    shp3, shpb = are.shape, bre.shape

    def body(are_ref, aim_ref, ldt_ref, bre_ref, bim_ref, lr_ref, li_ref, br_ref, bi_ref):
        lr, li, br, bi = _ssm_disc(are_ref[...], aim_ref[...], ldt_ref[...], bre_ref[...], bim_ref[...])
        lr_ref[...] = lr
        li_ref[...] = li
        br_ref[...] = br
        bi_ref[...] = bi

    return pl.pallas_call(
        body, name="ssm_prep",
        out_shape=[jax.ShapeDtypeStruct(shp3, F32)] * 2 + [jax.ShapeDtypeStruct(shpb, F32)] * 2,
    )(are, aim, ldt, bre, bim)


def _ssm_prep_bwd(are, aim, ldt, bre, bim, dlr, dli, dbr, dbi):
    shp3, shpb = are.shape, bre.shape

    def body(are_ref, aim_ref, ldt_ref, bre_ref, bim_ref, dlr_ref, dli_ref, dbr_ref, dbi_ref,
             o_are, o_aim, o_ldt, o_bre, o_bim):
        _, vjp = jax.vjp(_ssm_disc, are_ref[...], aim_ref[...], ldt_ref[...], bre_ref[...], bim_ref[...])
        g = vjp((dlr_ref[...], dli_ref[...], dbr_ref[...], dbi_ref[...]))
        o_are[...] = g[0]
        o_aim[...] = g[1]
        o_ldt[...] = jnp.broadcast_to(jnp.sum(g[2], axis=-1, keepdims=True), shp3)
        o_bre[...] = g[3]
        o_bim[...] = g[4]

    return pl.pallas_call(
        body, name="ssm_prep_bwd",
        out_shape=[jax.ShapeDtypeStruct(shp3, F32)] * 3 + [jax.ShapeDtypeStruct(shpb, F32)] * 2,
    )(are, aim, ldt, bre, bim, dlr, dli, dbr, dbi)


_SCAN_TB = 512
_SCAN_W = 256


def _cmul(ar, ai, br, bi):
    return ar * br - ai * bi, ar * bi + ai * br


def _ssm_scan(x, lam_r, lam_i, name, reverse=False, states=None):
    L = x.shape[0]
    tb = _tile(L, _SCAN_TB)
    nrb = L // tb
    nt = tb // SUBLANES
    W = _SCAN_W
    with_da = states is not None

    def body(*refs):
        if with_da:
            xr_ref, xi_ref, sr_ref, si_ref, ar_ref, ai_ref, o_ref, dar_ref, dai_ref, cr_ref, ci_ref = refs
        else:
            xr_ref, xi_ref, ar_ref, ai_ref, o_ref, cr_ref, ci_ref = refs
        step = pl.program_id(0)

        @pl.when(step == 0)
        def _():
            cr_ref[...] = jnp.zeros_like(cr_ref)
            ci_ref[...] = jnp.zeros_like(ci_ref)
            if with_da:
                dar_ref[...] = jnp.zeros_like(dar_ref)
                dai_ref[...] = jnp.zeros_like(dai_ref)

        row = _row((SUBLANES, W))

        def shift(v, d, fill):
            if reverse:
                return jnp.where(row < SUBLANES - d, pltpu.roll(v, SUBLANES - d, axis=0), fill)
            return jnp.where(row >= d, pltpu.roll(v, d, axis=0), fill)

        edge = 0 if reverse else SUBLANES - 1
        for wb in range(N_STATE // W):
            cols = slice(wb * W, (wb + 1) * W)
            a1r = jnp.broadcast_to(ar_ref[:, cols], (SUBLANES, W))
            a1i = jnp.broadcast_to(ai_ref[:, cols], (SUBLANES, W))
            if reverse:
                a1i = -a1i
            a2r, a2i = _cmul(a1r, a1i, a1r, a1i)
            a4r, a4i = _cmul(a2r, a2i, a2r, a2i)
            pws = ((1, a1r, a1i), (2, a2r, a2i), (4, a4r, a4i))
            pr, pi = a1r, a1i
            for d, _, _ in pws:
                qr, qi = _cmul(pr, pi, shift(pr, d, 1.0), shift(pi, d, 0.0))
                pr, pi = qr, qi
            mws = []
            for d, er, ei in pws:
                ok = (row < SUBLANES - d) if reverse else (row >= d)
                mws.append(((SUBLANES - d) if reverse else d, jnp.where(ok, er, 0.0), jnp.where(ok, ei, 0.0)))

            def tile(i, carry):
                cr, ci, dr, di = carry
                t = (nt - 1 - i) if reverse else i
                r0 = pl.multiple_of(t * SUBLANES, SUBLANES)
                vr = xr_ref[pl.ds(r0, SUBLANES), cols]
                vi = xi_ref[pl.ds(r0, SUBLANES), cols]
                for sh, er, ei in mws:
                    tr, ti = _cmul(er, ei, pltpu.roll(vr, sh, axis=0), pltpu.roll(vi, sh, axis=0))
                    vr, vi = vr + tr, vi + ti
                tr, ti = _cmul(pr, pi, cr, ci)
                vr, vi = vr + tr, vi + ti
                o_ref[pl.ds(r0, SUBLANES), cols] = vr
                o_ref[pl.ds(r0, SUBLANES), slice(N_STATE + wb * W, N_STATE + (wb + 1) * W)] = vi
                if with_da:
                    gr = jnp.where(row < SUBLANES - 1, pltpu.roll(vr, SUBLANES - 1, axis=0), cr)
                    gi = jnp.where(row < SUBLANES - 1, pltpu.roll(vi, SUBLANES - 1, axis=0), ci)
                    sr = sr_ref[pl.ds(r0, SUBLANES), cols]
                    si = si_ref[pl.ds(r0, SUBLANES), cols]
                    dr = dr + sr * gr + si * gi
                    di = di + sr * gi - si * gr
                ncr = jnp.broadcast_to(vr[edge:edge + 1, :], (SUBLANES, W))
                nci = jnp.broadcast_to(vi[edge:edge + 1, :], (SUBLANES, W))
                return ncr, nci, dr, di

            zero = jnp.zeros((SUBLANES, W), F32)
            cr, ci, dr, di = lax.fori_loop(0, nt, tile, (cr_ref[:, cols], ci_ref[:, cols], zero, zero), unroll=2)
            cr_ref[:, cols] = cr
            ci_ref[:, cols] = ci
            if with_da:
                dar_ref[:, cols] += dr
                dai_ref[:, cols] += di

        if with_da:
            @pl.when(step == nrb - 1)
            def _():
                dar_ref[...] = jnp.broadcast_to(jnp.sum(dar_ref[...], axis=0, keepdims=True), dar_ref.shape)
                dai_ref[...] = jnp.broadcast_to(jnp.sum(dai_ref[...], axis=0, keepdims=True), dai_ref.shape)

    rb = (lambda i: (nrb - 1 - i, 0)) if reverse else (lambda i: (i, 0))
    rb_im = (lambda i: (nrb - 1 - i, 1)) if reverse else (lambda i: (i, 1))
    blk_r = pl.BlockSpec((tb, N_STATE), rb)
    blk_i = pl.BlockSpec((tb, N_STATE), rb_im)
    one = pl.BlockSpec((1, N_STATE), lambda i: (0, 0))
    acc = pl.BlockSpec((SUBLANES, N_STATE), lambda i: (0, 0))
    ins = [x, x] + ([states, states] if with_da else []) + [lam_r, lam_i]
    in_specs = [blk_r, blk_i] + ([blk_r, blk_i] if with_da else []) + [one, one]
    out_specs = [pl.BlockSpec((tb, 2 * N_STATE), rb)] + ([acc, acc] if with_da else [])
    out_shape = [jax.ShapeDtypeStruct((L, 2 * N_STATE), F32)] + (
        [jax.ShapeDtypeStruct((SUBLANES, N_STATE), F32)] * 2 if with_da else [])
    outs = pl.pallas_call(
        body, name=name, grid=(nrb,), in_specs=in_specs, out_specs=out_specs, out_shape=out_shape,
        scratch_shapes=[pltpu.VMEM((SUBLANES, N_STATE), F32)] * 2,
        compiler_params=_cparams((6 if with_da else 4) * _nbytes((tb, N_STATE), F32),
                                 dimension_semantics=("arbitrary",)),
    )(*ins)
    return outs if with_da else outs[0]


_GROUPS = ((0, 256), (256, 768), (768, 1024))


def _merge_fwd(ya, yb, g12, mixg, name):
    L = ya.shape[0]
    tl = _tile(L, _TL)

    def body(a_ref, b_ref, g_ref, m_ref, o_ref):
        g12v = g_ref[...]
        yc = g12v[:, :C_WIDTH] * _sigmoid(g12v[:, C_WIDTH:])
        for (lo, hi), y in zip(_GROUPS, (a_ref[...], b_ref[...], yc)):
            r = lax.rsqrt(jnp.mean(y * y, axis=-1, keepdims=True) + EPS)
            o_ref[:, lo:hi] = ((y * r) * m_ref[:, lo:hi]).astype(o_ref.dtype)

    row = lambda w: pl.BlockSpec((tl, w), lambda i: (i, 0))
    return pl.pallas_call(
        body, name=name, grid=(L // tl,),
        in_specs=[row(256), row(512), row(512), pl.BlockSpec((1, D_MODEL), lambda i: (0, 0))],
        out_specs=row(D_MODEL), out_shape=jax.ShapeDtypeStruct((L, D_MODEL), _ACT),
    )(ya, yb, g12, mixg.reshape(1, D_MODEL))


def _merge_bwd(dy, ya, yb, g12, mixg, name):
    L = ya.shape[0]
    tl = _tile(L, _TL)

    def body(d_ref, a_ref, b_ref, g_ref, m_ref, da_ref, db_ref, dg_ref, dm_ref):
        @pl.when(pl.program_id(0) == 0)
        def _():
            dm_ref[...] = jnp.zeros_like(dm_ref)

        g12v = g_ref[...]
        g1, sg = g12v[:, :C_WIDTH], _sigmoid(g12v[:, C_WIDTH:])
        yc = g1 * sg
        outs = []
        for (lo, hi), y in zip(_GROUPS, (a_ref[...], b_ref[...], yc)):
            r = lax.rsqrt(jnp.mean(y * y, axis=-1, keepdims=True) + EPS)
            xhat = y * r
            d = d_ref[:, lo:hi]
            gy = d * m_ref[:, lo:hi]
            outs.append(r * (gy - xhat * jnp.mean(gy * xhat, axis=-1, keepdims=True)))
            dm_ref[:, lo:hi] += jnp.sum(d * xhat, axis=0, keepdims=True)
        da_ref[...] = outs[0]
        db_ref[...] = outs[1]
        dyc = outs[2]
        dg_ref[:, :C_WIDTH] = (dyc * sg).astype(dg_ref.dtype)
        dg_ref[:, C_WIDTH:] = (dyc * g1 * sg * (1.0 - sg)).astype(dg_ref.dtype)

    row = lambda w: pl.BlockSpec((tl, w), lambda i: (i, 0))
    one = pl.BlockSpec((1, D_MODEL), lambda i: (0, 0))
    return pl.pallas_call(
        body, name=name, grid=(L // tl,),
        in_specs=[row(D_MODEL), row(256), row(512), row(512), one],
        out_specs=[row(256), row(512), row(512), one],
        out_shape=[jax.ShapeDtypeStruct((L, 256), F32), jax.ShapeDtypeStruct((L, 512), F32),
                   jax.ShapeDtypeStruct((L, 512), _ACT), jax.ShapeDtypeStruct((1, D_MODEL), F32)],
    )(dy, ya, yb, g12, mixg.reshape(1, D_MODEL))


def _ple_bwd_elem(dh, gate, e, name):
    L, D = dh.shape
    tl = _tile(L, _TL)

    def body(d_ref, g_ref, e_ref, p_ref, o_ref):
        d, g = d_ref[...], g_ref[...]
        p_ref[...] = (d * e_ref[...] * g * (1.0 - g)).astype(p_ref.dtype)
        o_ref[...] = (d * g).astype(o_ref.dtype)

    row = pl.BlockSpec((tl, D), lambda i: (i, 0))
    return pl.pallas_call(
        body, name=name, grid=(L // tl,), in_specs=[row] * 3, out_specs=[row] * 2,
        out_shape=[jax.ShapeDtypeStruct((L, D), _ACT)] * 2,
        compiler_params=_cparams(4 * _nbytes((tl, D), F32)),
    )(dh, gate, e)


def _dskip_bwd(dy, z, name):
    L = dy.shape[0]
    tl = _tile(L, _TL)

    def body(d_ref, u_ref, o_ref):
        @pl.when(pl.program_id(0) == 0)
        def _():
            o_ref[...] = jnp.zeros_like(o_ref)

        o_ref[...] += jnp.sum(d_ref[...] * u_ref[...], axis=0, keepdims=True)

    return pl.pallas_call(
        body, name=name, grid=(L // tl,),
        in_specs=[pl.BlockSpec((tl, C_WIDTH), lambda i: (i, 0)), pl.BlockSpec((tl, C_WIDTH), lambda i: (i, 5))],
        out_specs=pl.BlockSpec((1, C_WIDTH), lambda i: (0, 0)),
        out_shape=jax.ShapeDtypeStruct((1, C_WIDTH), F32),
    )(dy, z)


def _loss_fwd_bwd(y, target):
    L, D = y.shape
    tl = _tile(L, _TL)

    def body(y_ref, t_ref, l_ref, d_ref):
        @pl.when(pl.program_id(0) == 0)
        def _():
            l_ref[...] = jnp.zeros_like(l_ref)

        e = y_ref[...] - t_ref[...]
        d_ref[...] = e * (1.0 / D)
        part = jnp.sum(jnp.sum(e * e, axis=-1, keepdims=True), axis=0, keepdims=True)
        l_ref[...] += jnp.broadcast_to(part, l_ref.shape)

    row = pl.BlockSpec((tl, D), lambda i: (i, 0))
    return pl.pallas_call(
        body, name="loss", grid=(L // tl,), in_specs=[row, row],
        out_specs=[pl.BlockSpec((SUBLANES, LANES), lambda i: (0, 0)), row],
        out_shape=[jax.ShapeDtypeStruct((SUBLANES, LANES), F32), jax.ShapeDtypeStruct((L, D), F32)],
    )(y, target)


def _adamw(w, g, m, v, name):
    R, C = w.shape
    tr = R if R <= 512 else _tile_rows(R, 512)

    def body(w_ref, g_ref, m_ref, v_ref, d_ref, nm_ref, nv_ref):
        gv = g_ref[...]
        nm = ADAM_B1 * m_ref[...] + (1.0 - ADAM_B1) * gv
        nv = ADAM_B2 * v_ref[...] + (1.0 - ADAM_B2) * (gv * gv)
        m_hat = nm / (1.0 - ADAM_B1 ** ADAM_STEP)
        v_hat = nv / (1.0 - ADAM_B2 ** ADAM_STEP)
        d_ref[...] = -ADAM_LR * (m_hat / (jnp.sqrt(v_hat) + ADAM_EPS) + ADAM_WD * w_ref[...])
        nm_ref[...] = nm
        nv_ref[...] = nv

    blk = pl.BlockSpec((tr, C), lambda i: (i, 0))
    return pl.pallas_call(
        body, name=name, grid=(R // tr,), in_specs=[blk] * 4, out_specs=[blk] * 3,
        out_shape=[jax.ShapeDtypeStruct((R, C), F32)] * 3,
        compiler_params=_cparams(7 * _nbytes((tr, C), F32)),
    )(w, g, m, v)


def _tile_rows(R, pref):
    t = pref
    while R % t:
        t -= SUBLANES
    assert t > 0
    return t


def _add_n(xs, name):
    R, C = xs[0].shape
    tr = R if R <= 512 else _tile_rows(R, 512)
    n = len(xs)

    def body(*refs):
        acc = refs[0][...].astype(F32)
        for r in refs[1:n]:
            acc = acc + r[...].astype(F32)
        refs[n][...] = acc

    blk = pl.BlockSpec((tr, C), lambda i: (i, 0))
    return pl.pallas_call(
        body, name=name, grid=(R // tr,), in_specs=[blk] * n, out_specs=blk,
        out_shape=jax.ShapeDtypeStruct((R, C), F32),
        compiler_params=_cparams((n + 1) * _nbytes((tr, C), F32)),
    )(*xs)


def _relu2(acc):
    r = jnp.maximum(acc, 0.0)
    return (r * r,)


def _rms_rows(x, g):
    return (x * lax.rsqrt(jnp.mean(x * x, axis=-1, keepdims=True) + EPS)) * g


def _resid_norm_epi(acc, res, g):
    h = res + acc
    return h, _rms_rows(h, g)


def _rms_bwd_epi(acc, h, dres, g):
    r = lax.rsqrt(jnp.mean(h * h, axis=-1, keepdims=True) + EPS)
    xhat = h * r
    gy = acc * g
    dx = r * (gy - xhat * jnp.mean(gy * xhat, axis=-1, keepdims=True))
    return dres + dx, jnp.sum(acc * xhat, axis=0, keepdims=True)


def _layer_fwd(h, xn, lp, cos, sin, g_next):
    L = h.shape[0]
    row = lambda n: lp[n].reshape(1, D_MODEL)
    z = _mm(xn, lp["w_in"], mode="nn", M=L, N=IN_COLS, K=D_MODEL, b_cb=True, out_dtypes=[F32], name="f_w_in")
    ya = _gmlp_fwd(z, lp["ws"], lp["bfull"], lp["lgf"], lp["lbf"], "f_gmlp")
    q, k = _qk_prep(z, cos, sin, lp["gq"], lp["gk"], "f_qk_prep")
    yb = _attn_fwd(q, k, z, lp["sinks"], "f_attn")
    bu = _mm(z, lp["bcat"], mode="nn", M=L, N=2 * N_STATE, K=C_WIDTH, a_off=5, tk=C_WIDTH,
             out_dtypes=[F32], name="f_ssm_in")
    S = _ssm_scan(bu, lp["lam_r"], lp["lam_i"], "f_ssm_scan")
    y, yg = _mm(S, lp["ccat"], mode="nn", M=L, N=C_WIDTH, K=2 * N_STATE, tk=2 * N_STATE,
                extras=[(z, 5), (lp["dskip"], 0)], out_dtypes=[F32, _ACT], name="f_ssm_out",
                epi=lambda acc, u, dsk: (acc + dsk * u, _gelu(acc + dsk * u)))
    g12 = _mm(yg, lp["w12"], mode="nn", M=L, N=2 * C_WIDTH, K=C_WIDTH, out_dtypes=[F32], name="f_glu")
    ycat = _merge_fwd(ya, yb, g12, lp["mix_out_g"], "f_merge")
    h1, hn = _mm(ycat, lp["w_out"], mode="nn", M=L, N=D_MODEL, K=D_MODEL, extras=[(h, 0), (row("mlp_norm_g"), 0)],
                 epi=_resid_norm_epi, out_dtypes=[F32, _ACT], name="f_w_out")
    r = _mm(hn, lp["w_ff1"], mode="nn", M=L, N=D_FF, K=D_MODEL, b_cb=True, epi=_relu2,
            out_dtypes=[_ACT], name="f_ff1")
    h2, hn3 = _mm(r, lp["w_ff2"], mode="nn", M=L, N=D_MODEL, K=D_FF, extras=[(h1, 0), (row("ple_norm_g"), 0)],
                  epi=_resid_norm_epi, out_dtypes=[F32, _ACT], name="f_ff2")
    e = _mm(lp["p"], lp["w_ple_proj"], mode="nn", M=L, N=D_MODEL, K=PLE_DIM, b_cb=True, tk=PLE_DIM,
            out_dtypes=[F32], name="f_ple_proj")

    def gate_epi(acc, h2_, e_, *g):
        gate_ = _sigmoid(acc)
        h3_ = h2_ + gate_ * e_
        return (h3_, gate_) + ((_rms_rows(h3_, g[0]),) if g else ())

    outs = _mm(hn3, lp["w_ple_gate"], mode="nn", M=L, N=D_MODEL, K=D_MODEL,
               extras=[(h2, 0), (e, 0)] + ([(g_next.reshape(1, D_MODEL), 0)] if g_next is not None else []),
               epi=gate_epi, out_dtypes=[F32, F32] + ([_ACT] if g_next is not None else []), name="f_ple_gate")
    h3, gate = outs[0], outs[1]
    xn_next = outs[2] if g_next is not None else None
    saved = dict(h=h, xn=xn, z=z, ya=ya, q=q, k=k, yb=yb, S=S, y=y, yg=yg, g12=g12, ycat=ycat, h1=h1, hn=hn,
                 r=r, h2=h2, hn3=hn3, e=e, gate=gate)
    return h3, xn_next, saved


def _layer_bwd(dh3, lp, sv, cos, sin):
    L = dh3.shape[0]
    z = sv["z"]
    dpre, de = _ple_bwd_elem(dh3, sv["gate"], sv["e"], "b_ple_elem")
    stk = lp["stk"]
    d_gate = _mm(sv["hn3"], dpre, mode="tn", M=D_MODEL, N=D_MODEL, K=L, out_dtypes=[F32], name="b_dw_gate",
                 o_stack=stk["w_ple_gate"])
    d_proj = _mm(lp["p"], de, mode="tn", M=PLE_DIM, N=D_MODEL, K=L, o_cb=True, tm=PLE_DIM,
                 out_dtypes=[F32], name="b_dw_proj", o_stack=stk["w_ple_proj"])
    row = lambda n: lp[n].reshape(1, D_MODEL)
    dh2, dg_ple = _mm(dpre, lp["w_ple_gate"], mode="nt", M=L, N=D_MODEL, K=D_MODEL,
                      extras=[(sv["h2"], 0), (dh3, 0), (row("ple_norm_g"), 0)], epi=_rms_bwd_epi,
                      out_dtypes=[F32, F32], n_acc=1, name="b_dx_gate")
    da = _mm(dh2, lp["w_ff2"], mode="nt", M=L, N=D_FF, K=D_MODEL, extras=[(sv["r"], 0)],
             epi=lambda acc, r_: (acc * (2.0 * jnp.sqrt(r_.astype(F32))),), out_dtypes=[_ACT], name="b_dx_ff2")
    d_ff2 = _mm(sv["r"], dh2, mode="tn", M=D_FF, N=D_MODEL, K=L, out_dtypes=[F32], name="b_dw_ff2",
                o_stack=stk["w_ff2"])
    d_ff1 = _mm(sv["hn"], da, mode="tn", M=D_MODEL, N=D_FF, K=L, o_cb=True, out_dtypes=[F32], name="b_dw_ff1",
                o_stack=stk["w_ff1"])
    dh1, dg_mlp = _mm(da, lp["w_ff1"], mode="nt", M=L, N=D_MODEL, K=D_FF, b_cb=True,
                      extras=[(sv["h1"], 0), (dh2, 0), (row("mlp_norm_g"), 0)], epi=_rms_bwd_epi,
                      out_dtypes=[F32, F32], n_acc=1, name="b_dx_ff1")
    d_out = _mm(sv["ycat"], dh1, mode="tn", M=D_MODEL, N=D_MODEL, K=L, out_dtypes=[F32], name="b_dw_out",
                o_stack=stk["w_out"])
    dycat = _mm(dh1, lp["w_out"], mode="nt", M=L, N=D_MODEL, K=D_MODEL, out_dtypes=[F32], name="b_dx_out")
    dya, dyb, dg12, dmix = _merge_bwd(dycat, sv["ya"], sv["yb"], sv["g12"], lp["mix_out_g"], "b_merge")
    d_w12 = _mm(sv["yg"], dg12, mode="tn", M=C_WIDTH, N=2 * C_WIDTH, K=L, tm=C_WIDTH, out_dtypes=[F32], name="b_dw_glu",
                o_stack=stk["w12"])
    dy = _mm(dg12, lp["w12"], mode="nt", M=L, N=C_WIDTH, K=2 * C_WIDTH, tk=2 * C_WIDTH, extras=[(sv["y"], 0)],
             epi=lambda acc, y_: (acc * _gelu_grad(y_),), out_dtypes=[F32], name="b_dx_glu")
    dd = _dskip_bwd(dy, z, "b_dskip")
    dS = _mm(dy, lp["ccat"], mode="nt", M=L, N=2 * N_STATE, K=C_WIDTH, tk=C_WIDTH, out_dtypes=[F32], name="b_dx_ssm_out")
    d_ccat = _mm(sv["S"], dy, mode="tn", M=2 * N_STATE, N=C_WIDTH, K=L, out_dtypes=[F32], name="b_dw_ssm_out")
    G, dar, dai = _ssm_scan(dS, lp["lam_r"], lp["lam_i"], "b_ssm_scan", reverse=True, states=sv["S"])
    d_bcat = _mm(z, G, mode="tn", M=C_WIDTH, N=2 * N_STATE, K=L, a_off=5, tm=C_WIDTH, out_dtypes=[F32], name="b_dw_ssm_in")
    dzc = _mm(G, lp["bcat"], mode="nt", M=L, N=C_WIDTH, K=2 * N_STATE, tk=2 * N_STATE,
              extras=[(dy, 0), (lp["dskip"], 0)], epi=lambda acc, dy_, dsk: (acc + dy_ * dsk,),
              out_dtypes=[_ACT], name="b_dx_ssm_in")
    dq, dkc, dkp, dvc, dvp, dsink = _attn_bwd(sv["q"], sv["k"], z, lp["sinks"], dyb, "b_attn")
    dzq, dzk, dzv, dgq, dgk = _qk_prep_bwd(z, dq, dkc, dkp, dvc, dvp, cos, sin, lp["gq"], lp["gk"], "b_qk_prep")
    dza, dws, dbs, dlg, dlb = _gmlp_bwd(z, dya, lp["ws"], lp["wsT"], lp["bfull"], lp["lgf"], lp["lbf"], "b_gmlp")
    dz = jnp.concatenate([dza, dzq, dzk, dzv, dzc], axis=1)
    d_in = _mm(sv["xn"], dz, mode="tn", M=D_MODEL, N=IN_COLS, K=L, o_cb=True, out_dtypes=[F32], name="b_dw_in",
               o_stack=stk["w_in"])
    dh, dg_attn = _mm(dz, lp["w_in"], mode="nt", M=L, N=D_MODEL, K=IN_COLS, b_cb=True,
                      extras=[(sv["h"], 0), (dh1, 0), (row("attn_norm_g"), 0)], epi=_rms_bwd_epi,
                      out_dtypes=[F32, F32], n_acc=1, name="b_dx_in")
    grads = dict(w_in=d_in, w12=d_w12, w_out=d_out, w_ff1=d_ff1, w_ff2=d_ff2, w_ple_gate=d_gate, w_ple_proj=d_proj,
                 attn_norm_g=dg_attn.reshape(D_MODEL), mlp_norm_g=dg_mlp.reshape(D_MODEL),
                 ple_norm_g=dg_ple.reshape(D_MODEL), mix_out_g=dmix.reshape(D_MODEL),
                 dws=dws, dbs=dbs, dlg=dlg, dlb=dlb, dgq=dgq, dgk=dgk, dsink=dsink,
                 dar=dar, dai=dai, d_bcat=d_bcat, d_ccat=d_ccat, dd=dd)
    return dh, grads


SMALL = ("attn_norm_g", "gmlp_ln_g", "gmlp_ln_b", "gmlp_ws", "gmlp_bs", "q_norm_g", "k_norm_g", "sinks",
         "ssm_a_re", "ssm_a_im", "ssm_log_dt", "ssm_b_re", "ssm_b_im", "ssm_c_re", "ssm_c_im", "ssm_d",
         "mix_out_g", "mlp_norm_g", "ple_norm_g")
BIG = ("w_in", "w12", "w_out", "w_ff1", "w_ff2", "w_ple_gate", "w_ple_proj")
COL_SHARDED = ("w_in", "w_ff1", "w_ple_proj")


def _block_diag(t):
    nl, g, a, b = t.shape
    eye = jnp.eye(g, dtype=t.dtype)
    return (t[:, :, :, None, :] * eye[None, :, None, :, None]).reshape(nl, g * a, g * b)


def _diag_blocks(t, a, b):
    nl = t.shape[0]
    t = t.reshape(nl, C_GROUPS, a, C_GROUPS, b)
    idx = jnp.arange(C_GROUPS)
    return jnp.moveaxis(t[:, idx, :, idx, :], 0, 1)


def _local_step(x, p, positions, target, sw, bw):
    nl = sw["attn_norm_g"].shape[0]
    G = nl * C_GROUPS
    zeros = lambda *s: jnp.zeros(s, F32)
    are = sw["ssm_a_re"].reshape(G, 1, C_STATE)
    aim = sw["ssm_a_im"].reshape(G, 1, C_STATE)
    ldt = jnp.broadcast_to(sw["ssm_log_dt"][..., None], (nl, C_GROUPS, C_STATE)).reshape(G, 1, C_STATE)
    bre = jnp.swapaxes(sw["ssm_b_re"], -1, -2).reshape(G, C_GROUP, C_STATE)
    bim = jnp.swapaxes(sw["ssm_b_im"], -1, -2).reshape(G, C_GROUP, C_STATE)
    lr, li, bbr, bbi = _ssm_prep(are, aim, ldt, bre, bim)
    unflat = lambda t: t.reshape(nl, C_GROUPS, C_GROUP, C_STATE)
    lp = dict(
        attn_norm_g=sw["attn_norm_g"], mlp_norm_g=sw["mlp_norm_g"], ple_norm_g=sw["ple_norm_g"],
        mix_out_g=sw["mix_out_g"], sinks=sw["sinks"],
        ws=sw["gmlp_ws"], wsT=jnp.swapaxes(sw["gmlp_ws"], -1, -2),
        bfull=jnp.concatenate([zeros(nl, A_HEADS, CHUNK, HEAD_DIM),
                               jnp.broadcast_to(sw["gmlp_bs"][..., None], (nl, A_HEADS, CHUNK, HEAD_DIM))], axis=-1),
        lgf=jnp.concatenate([zeros(nl, A_HEADS, HEAD_DIM), sw["gmlp_ln_g"]], axis=-1),
        lbf=jnp.concatenate([zeros(nl, A_HEADS, HEAD_DIM), sw["gmlp_ln_b"]], axis=-1),
        gq=jnp.tile(sw["q_norm_g"], (1, 2)).reshape(nl, 1, LANES),
        gk=jnp.tile(sw["k_norm_g"], (1, 2)).reshape(nl, 1, LANES),
        lam_r=lr.reshape(nl, 1, N_STATE), lam_i=li.reshape(nl, 1, N_STATE),
        bcat=jnp.concatenate([_block_diag(unflat(bbr)), _block_diag(unflat(bbi))], axis=-1),
        ccat=jnp.concatenate([_block_diag(jnp.swapaxes(sw["ssm_c_re"], -1, -2)),
                              -_block_diag(jnp.swapaxes(sw["ssm_c_im"], -1, -2))], axis=1),
        dskip=sw["ssm_d"].reshape(nl, 1, C_WIDTH))
    cos, sin = _rope_tables(positions)

    def layer_params(l, stk=None):
        lpi = {n: v[l] for n, v in lp.items()}
        lpi.update({n: (w, l) for n, w in bw.items()})
        lpi["p"] = (p, l)
        if stk is not None:
            lpi["stk"] = {n: (s, l) for n, s in stk.items()}
        return lpi

    h, saved = x, []
    xn = _rms_fwd(x, sw["attn_norm_g"][0], "f_norm_attn")
    for l in range(nl):
        g_next = sw["attn_norm_g"][l + 1] if l + 1 < nl else None
        h, xn, sv = _layer_fwd(h, xn, layer_params(l), cos, sin, g_next)
        saved.append(sv)
    sse, dh = _loss_fwd_bwd(h, target)

    stk = {n: lax.empty(w.shape, F32) for n, w in bw.items()}
    per_layer = [None] * nl
    for l in reversed(range(nl)):
        dh, gl = _layer_bwd(dh, layer_params(l, stk), saved[l], cos, sin)
        stk = {n: gl.pop(n) for n in BIG}
        per_layer[l] = gl
    grad_x = dh
    g = {n: jnp.stack([per_layer[l][n] for l in range(nl)]) for n in per_layer[0]}
    g.update(stk)

    d_bcat = g["d_bcat"]
    dbr = _diag_blocks(d_bcat[:, :, :N_STATE], C_GROUP, C_STATE).reshape(G, C_GROUP, C_STATE)
    dbi = _diag_blocks(d_bcat[:, :, N_STATE:], C_GROUP, C_STATE).reshape(G, C_GROUP, C_STATE)
    dlr = g["dar"][:, 0].reshape(G, 1, C_STATE)
    dli = g["dai"][:, 0].reshape(G, 1, C_STATE)
    g_are, g_aim, g_ldt, g_bre, g_bim = _ssm_prep_bwd(are, aim, ldt, bre, bim, dlr, dli, dbr, dbi)
    d_ccat = g["d_ccat"]
    sg = dict(
        attn_norm_g=g["attn_norm_g"], mlp_norm_g=g["mlp_norm_g"], ple_norm_g=g["ple_norm_g"], mix_out_g=g["mix_out_g"],
        gmlp_ln_g=g["dlg"][:, :, 0, HEAD_DIM:], gmlp_ln_b=g["dlb"][:, :, 0, HEAD_DIM:],
        gmlp_ws=g["dws"], gmlp_bs=g["dbs"][:, :, :, HEAD_DIM],
        q_norm_g=g["dgq"][:, 0, :HEAD_DIM], k_norm_g=g["dgk"][:, 0, :HEAD_DIM],
        sinks=g["dsink"][:, 0, :B_Q_HEADS],
        ssm_a_re=g_are.reshape(nl, C_GROUPS, C_STATE), ssm_a_im=g_aim.reshape(nl, C_GROUPS, C_STATE),
        ssm_log_dt=g_ldt[:, 0, 0].reshape(nl, C_GROUPS),
        ssm_b_re=jnp.swapaxes(g_bre.reshape(nl, C_GROUPS, C_GROUP, C_STATE), -1, -2),
        ssm_b_im=jnp.swapaxes(g_bim.reshape(nl, C_GROUPS, C_GROUP, C_STATE), -1, -2),
        ssm_c_re=jnp.swapaxes(_diag_blocks(d_ccat[:, :N_STATE], C_STATE, C_GROUP), -1, -2),
        ssm_c_im=-jnp.swapaxes(_diag_blocks(d_ccat[:, N_STATE:], C_STATE, C_GROUP), -1, -2),
        ssm_d=g["dd"].reshape(nl, C_GROUPS, C_GROUP),
    )
    bg = {n: g[n] for n in BIG}
    return sse[0, 0], grad_x, sg, bg


_ANY = pl.BlockSpec(memory_space=pl.ANY)
N_LAYERS = 4


def _mesh_pos():
    x, y, c = lax.axis_index("x"), lax.axis_index("y"), lax.axis_index("c")
    chips = [(1 - x, y), (x, 1 - y), (1 - x, 1 - y)]
    return x, y, c, 2 * x + y, chips


def _cast_into_slot(ws, j, name):
    nl, R, _ = ws[0].shape
    widths = [w.shape[2] for w in ws]
    C = sum(widths)
    tr = R if R <= 512 else _tile_rows(R, 512)
    nw = len(ws)

    def body(s_ref, *refs):
        o_ref = refs[nw]
        off = 0
        for r, wd in zip(refs[:nw], widths):
            o_ref[:, off:off + wd] = r[...].astype(o_ref.dtype)
            off += wd

    return pl.pallas_call(
        body, name=name,
        grid_spec=pltpu.PrefetchScalarGridSpec(
            num_scalar_prefetch=1, grid=(nl, R // tr),
            in_specs=[pl.BlockSpec((None, tr, wd), lambda l, i, s: (l, i, 0)) for wd in widths],
            out_specs=pl.BlockSpec((None, None, tr, C), lambda l, i, s: (l, s[0], i, 0))),
        out_shape=jax.ShapeDtypeStruct((nl, N_CHIPS, R, C), _MXU),
    )(jnp.reshape(j, (1,)).astype(jnp.int32), *ws)


def _gather_weights(bufs):
    nk = len(bufs)

    def body(*refs):
        ins, outs = refs[:nk], refs[nk:2 * nk]
        send_sems, recv_sems = refs[2 * nk:]
        x, y, c, j, chips = _mesh_pos()
        mine, other = pl.ds(2 * c, 2), pl.ds(2 * (1 - c), 2)

        def ici(t, q):
            cx, cy = chips[q]
            return pltpu.make_async_remote_copy(
                src_ref=ins[t].at[mine, j], dst_ref=outs[t].at[mine, j],
                send_sem=send_sems.at[6 * t + q], recv_sem=recv_sems.at[6 * t + q],
                device_id=(cx, cy, c), device_id_type=MESH)

        def landed(t, q):
            cx, cy = chips[q]
            blk = outs[t].at[mine, 2 * cx + cy]
            return pltpu.make_async_remote_copy(
                src_ref=blk, dst_ref=blk, send_sem=send_sems.at[6 * t + q], recv_sem=recv_sems.at[6 * t + q],
                device_id=(cx, cy, c), device_id_type=MESH)

        def fwd(t, q, rows):
            cx, cy = chips[q]
            blk = outs[t].at[rows, 2 * cx + cy]
            return pltpu.make_async_remote_copy(
                src_ref=blk, dst_ref=blk, send_sem=send_sems.at[6 * t + 3 + q], recv_sem=recv_sems.at[6 * t + 3 + q],
                device_id=(x, y, 1 - c), device_id_type=MESH)

        for t in range(nk):
            for q in range(3):
                ici(t, q).start()
        for t in range(nk):
            for q in range(3):
                landed(t, q).wait_recv()
                fwd(t, q, mine).start()
        for t in range(nk):
            for q in range(3):
                fwd(t, q, other).wait_recv()
        for t in range(nk):
            for q in range(3):
                ici(t, q).wait_send()
                fwd(t, q, mine).wait_send()

    return pl.pallas_call(
        body, name="gather_weights", in_specs=[_ANY] * nk, out_specs=[_ANY] * nk,
        out_shape=[jax.ShapeDtypeStruct(b.shape, b.dtype) for b in bufs],
        input_output_aliases={t: t for t in range(nk)},
        scratch_shapes=[pltpu.SemaphoreType.DMA((6 * nk,)), pltpu.SemaphoreType.DMA((6 * nk,))],
    )(*bufs)


def _exchange_sibling_half(gl):
    nk = len(gl)

    def body(*refs):
        ins, outs = refs[:nk], refs[nk:2 * nk]
        send_sems, recv_sems = refs[2 * nk:]
        x, y, c, _, _ = _mesh_pos()
        cps = [pltpu.make_async_remote_copy(
            src_ref=ins[t].at[pl.ds(2 * (1 - c), 2)], dst_ref=outs[t],
            send_sem=send_sems.at[t], recv_sem=recv_sems.at[t],
            device_id=(x, y, 1 - c), device_id_type=MESH) for t in range(nk)]
        for cp in cps:
            cp.start()
        for cp in cps:
            cp.wait()

    return pl.pallas_call(
        body, name="reduce_sibling", in_specs=[_ANY] * nk, out_specs=[_ANY] * nk,
        out_shape=[jax.ShapeDtypeStruct((2,) + g.shape[1:], g.dtype) for g in gl],
        scratch_shapes=[pltpu.SemaphoreType.DMA((nk,)), pltpu.SemaphoreType.DMA((nk,))],
    )(*gl)


def _exchange_chips(ps):
    nk = len(ps)

    def body(*refs):
        ins, outs = refs[:nk], refs[nk:2 * nk]
        send_sems, recv_sems = refs[2 * nk:]
        x, y, c, j, chips = _mesh_pos()

        def send(t, q):
            cx, cy = chips[q]
            return pltpu.make_async_remote_copy(
                src_ref=ins[t].at[:, 2 * cx + cy], dst_ref=outs[t].at[j],
                send_sem=send_sems.at[3 * t + q], recv_sem=recv_sems.at[3 * t + q],
                device_id=(cx, cy, c), device_id_type=MESH)

        def landed(t, q):
            cx, cy = chips[q]
            blk = outs[t].at[2 * cx + cy]
            return pltpu.make_async_remote_copy(
                src_ref=blk, dst_ref=blk, send_sem=send_sems.at[3 * t + q], recv_sem=recv_sems.at[3 * t + q],
                device_id=(cx, cy, c), device_id_type=MESH)

        for t in range(nk):
            for q in range(3):
                send(t, q).start()
        for t in range(nk):
            for q in range(3):
                landed(t, q).wait_recv()
        for t in range(nk):
            for q in range(3):
                send(t, q).wait_send()

    return pl.pallas_call(
        body, name="reduce_chips", in_specs=[_ANY] * nk, out_specs=[_ANY] * nk,
        out_shape=[jax.ShapeDtypeStruct((N_CHIPS, 2) + p.shape[2:], p.dtype) for p in ps],
        scratch_shapes=[pltpu.SemaphoreType.DMA((3 * nk,)), pltpu.SemaphoreType.DMA((3 * nk,))],
    )(*ps)


def _share_sibling(fs):
    nk = len(fs)

    def body(*refs):
        ins, outs = refs[:nk], refs[nk:2 * nk]
        send_sems, recv_sems = refs[2 * nk:]
        x, y, c, _, _ = _mesh_pos()
        mine = pl.ds(2 * c, 2)
        cps = [pltpu.make_async_remote_copy(
            src_ref=ins[t].at[mine], dst_ref=outs[t].at[mine], send_sem=send_sems.at[t], recv_sem=recv_sems.at[t],
            device_id=(x, y, 1 - c), device_id_type=MESH) for t in range(nk)]
        for cp in cps:
            cp.start()
        for cp in cps:
            cp.wait_send()
        for t in range(nk):
            blk = outs[t].at[pl.ds(2 * (1 - c), 2)]
            pltpu.make_async_remote_copy(
                src_ref=blk, dst_ref=blk, send_sem=send_sems.at[t], recv_sem=recv_sems.at[t],
                device_id=(x, y, 1 - c), device_id_type=MESH).wait_recv()

    return pl.pallas_call(
        body, name="share_sibling", in_specs=[_ANY] * nk, out_specs=[_ANY] * nk,
        out_shape=[jax.ShapeDtypeStruct(f.shape, f.dtype) for f in fs],
        input_output_aliases={t: t for t in range(nk)},
        scratch_shapes=[pltpu.SemaphoreType.DMA((nk,)), pltpu.SemaphoreType.DMA((nk,))],
    )(*fs)


def _add_own_half(gl, r1, c, name):
    _, ns, R, C = gl.shape
    rows = 2 * ns * R
    tr = _tile_rows(rows, 512)
    nblk = rows // tr

    def body(s_ref, a_ref, b_ref, o_ref):
        o_ref[...] = (a_ref[...] + b_ref[...]).astype(o_ref.dtype)

    out = pl.pallas_call(
        body, name=name,
        grid_spec=pltpu.PrefetchScalarGridSpec(
            num_scalar_prefetch=1, grid=(nblk,),
            in_specs=[pl.BlockSpec((tr, C), lambda i, s: (s[0] * nblk + i, 0)), pl.BlockSpec((tr, C), lambda i, s: (i, 0))],
            out_specs=pl.BlockSpec((tr, C), lambda i, s: (i, 0))),
        out_shape=jax.ShapeDtypeStruct((rows, C), _WIRE),
        compiler_params=_cparams(3 * _nbytes((tr, C), F32)),
    )(jnp.reshape(c, (1,)).astype(jnp.int32), gl.reshape(2 * rows, C), r1.reshape(rows, C))
    return out.reshape(2, ns, R, C)


def _add_chips(p, r2, j, c, name):
    _, ns, R, C = p.shape
    tr = R if R <= 512 else _tile_rows(R, 512)

    def body(s_ref, own, a1, a2, a3, o_ref):
        f = lambda r: r[...].astype(F32)
        o_ref[...] = ((f(own) + f(a1)) + f(a2)) + f(a3)

    blk = (None, None, tr, C)
    return pl.pallas_call(
        body, name=name,
        grid_spec=pltpu.PrefetchScalarGridSpec(
            num_scalar_prefetch=1, grid=(2, R // tr),
            in_specs=[pl.BlockSpec(blk, lambda h, i, s: (h, s[0], i, 0))]
            + [pl.BlockSpec(blk, lambda h, i, s, k=k: ((s[0] + k) % N_CHIPS, h, i, 0)) for k in (1, 2, 3)],
            out_specs=pl.BlockSpec((None, tr, C), lambda h, i, s: (2 * s[1] + h, i, 0))),
        out_shape=jax.ShapeDtypeStruct((N_LAYERS, R, C), F32),
        compiler_params=_cparams(6 * _nbytes((tr, C), F32)),
    )(jnp.stack([j, c]).astype(jnp.int32), p, r2, r2, r2)


def _allreduce_small(buf):
    Rs = buf.shape[0]

    def body(b_ref, o_ref, t_ref, slots_ref, send_sems, recv_sems):
        x, y, c, j, chips = _mesh_pos()
        sib = pltpu.make_async_remote_copy(
            src_ref=b_ref, dst_ref=t_ref, send_sem=send_sems.at[0], recv_sem=recv_sems.at[0],
            device_id=(x, y, 1 - c), device_id_type=MESH)
        sib.start()
        sib.wait()
        slots_ref[j] = b_ref[...] + t_ref[...]

        def send(q):
            cx, cy = chips[q]
            return pltpu.make_async_remote_copy(
                src_ref=slots_ref.at[j], dst_ref=slots_ref.at[j], send_sem=send_sems.at[1 + q],
                recv_sem=recv_sems.at[1 + q], device_id=(cx, cy, c), device_id_type=MESH)

        def landed(q):
            cx, cy = chips[q]
            blk = slots_ref.at[2 * cx + cy]
            return pltpu.make_async_remote_copy(
                src_ref=blk, dst_ref=blk, send_sem=send_sems.at[1 + q], recv_sem=recv_sems.at[1 + q],
                device_id=(cx, cy, c), device_id_type=MESH)

        for q in range(3):
            send(q).start()
        for q in range(3):
            landed(q).wait_recv()
        for q in range(3):
            send(q).wait_send()
        o_ref[...] = ((slots_ref[0] + slots_ref[1]) + slots_ref[2]) + slots_ref[3]

    vm = pl.BlockSpec(memory_space=pltpu.VMEM)
    return pl.pallas_call(
        body, name="allreduce_small", in_specs=[vm], out_specs=vm,
        out_shape=jax.ShapeDtypeStruct((Rs, LANES), F32),
        scratch_shapes=[pltpu.VMEM((Rs, LANES), F32), pltpu.VMEM((N_CHIPS, Rs, LANES), F32),
                        pltpu.SemaphoreType.DMA((4,)), pltpu.SemaphoreType.DMA((4,))],
        compiler_params=_cparams(4 * _nbytes((Rs, LANES), F32)),
    )(buf)


def _pack(d):
    flat = jnp.concatenate([d[n].reshape(-1) for n in SMALL])
    rows = -(-flat.shape[0] // (SUBLANES * LANES)) * SUBLANES
    return jnp.pad(flat, (0, rows * LANES - flat.shape[0])).reshape(rows, LANES)


def _unpack(buf, like):
    flat = buf.reshape(-1)
    out, off = {}, 0
    for n in SMALL:
        size = int(np.prod(like[n].shape))
        out[n] = flat[off:off + size].reshape(like[n].shape)
        off += size
    return out


ARGS = ("x", "p", "positions", "attn_norm_g", "w_in", "gmlp_ln_g", "gmlp_ln_b", "gmlp_ws", "gmlp_bs", "q_norm_g",
        "k_norm_g", "sinks", "ssm_a_re", "ssm_a_im", "ssm_log_dt", "ssm_b_re", "ssm_b_im", "ssm_c_re", "ssm_c_im",
        "ssm_d", "glu_w1", "glu_w2", "mix_out_g", "w_out", "mlp_norm_g", "w_ff1", "w_ff2", "ple_norm_g", "w_ple_gate",
        "w_ple_proj")
WEIGHTS = ARGS[3:]


def kernel(x, p, positions, attn_norm_g, w_in, gmlp_ln_g, gmlp_ln_b, gmlp_ws, gmlp_bs, q_norm_g, k_norm_g, sinks, ssm_a_re, ssm_a_im, ssm_log_dt, ssm_b_re, ssm_b_im, ssm_c_re, ssm_c_im, ssm_d, glu_w1, glu_w2, mix_out_g, w_out, mlp_norm_g, w_ff1, w_ff2, ple_norm_g, w_ple_gate, w_ple_proj, loss_target, m_attn_norm_g, m_w_in, m_gmlp_ln_g, m_gmlp_ln_b, m_gmlp_ws, m_gmlp_bs, m_q_norm_g, m_k_norm_g, m_sinks, m_ssm_a_re, m_ssm_a_im, m_ssm_log_dt, m_ssm_b_re, m_ssm_b_im, m_ssm_c_re, m_ssm_c_im, m_ssm_d, m_glu_w1, m_glu_w2, m_mix_out_g, m_w_out, m_mlp_norm_g, m_w_ff1, m_w_ff2, m_ple_norm_g, m_w_ple_gate, m_w_ple_proj, v_attn_norm_g, v_w_in, v_gmlp_ln_g, v_gmlp_ln_b, v_gmlp_ws, v_gmlp_bs, v_q_norm_g, v_k_norm_g, v_sinks, v_ssm_a_re, v_ssm_a_im, v_ssm_log_dt, v_ssm_b_re, v_ssm_b_im, v_ssm_c_re, v_ssm_c_im, v_ssm_d, v_glu_w1, v_glu_w2, v_mix_out_g, v_w_out, v_mlp_norm_g, v_w_ff1, v_w_ff2, v_ple_norm_g, v_w_ple_gate, v_w_ple_proj):
    a = dict(locals())
    L = a["x"].shape[1]
    nl = N_LAYERS
    c = lax.axis_index("c")
    j = 2 * lax.axis_index("x") + lax.axis_index("y")

    shards = dict(w_in=[a["w_in"]], w12=[a["glu_w1"], a["glu_w2"]], w_out=[a["w_out"]], w_ff1=[a["w_ff1"]],
                  w_ff2=[a["w_ff2"]], w_ple_gate=[a["w_ple_gate"]], w_ple_proj=[a["w_ple_proj"]])
    gathered = dict(zip(BIG, _gather_weights([_cast_into_slot(shards[n], j, "cast_" + n) for n in BIG])))
    bw = {n: (g if n in COL_SHARDED else g.reshape(nl, N_CHIPS * g.shape[2], g.shape[3])) for n, g in gathered.items()}

    sw = {n: a[n] for n in SMALL}
    sse, gx, sg, bg = _local_step(a["x"].reshape(L, D_MODEL), a["p"].reshape(nl, L, PLE_DIM),
                                  a["positions"].reshape(L), a["loss_target"].reshape(L, D_MODEL), sw, bw)
    loss = lax.psum(sse * (0.5 / D_MODEL), ("x", "y", "c"))

    gl = [bg[n] if n in COL_SHARDED else bg[n].reshape(nl, N_CHIPS, bg[n].shape[1] // N_CHIPS, bg[n].shape[2])
          for n in BIG]
    r1 = _exchange_sibling_half(gl)
    ps = [_add_own_half(g, r, c, "reduce_add_sibling_" + n) for g, r, n in zip(gl, r1, BIG)]
    r2 = _exchange_chips(ps)
    fs = [_add_chips(p_, r, j, c, "reduce_add_chips_" + n) for p_, r, n in zip(ps, r2, BIG)]
    big_grads = dict(zip(BIG, _share_sibling(fs)))
    g12 = big_grads.pop("w12")
    big_grads["glu_w1"], big_grads["glu_w2"] = g12[:, :, :C_WIDTH], g12[:, :, C_WIDTH:]

    small_grads = _unpack(_allreduce_small(_pack(sg)), sw)

    grads, delta, new_m, new_v = {}, {}, {}, {}
    d_s, m_s, v_s = _adamw(_pack(sw), _pack(small_grads), _pack({n: a["m_" + n] for n in SMALL}),
                           _pack({n: a["v_" + n] for n in SMALL}), "adamw_small")
    grads.update(small_grads)
    delta.update(_unpack(d_s, sw))
    new_m.update(_unpack(m_s, sw))
    new_v.update(_unpack(v_s, sw))
    for n, g in big_grads.items():
        shp = a[n].shape
        two_d = lambda t: t.reshape(shp[0] * shp[1], shp[2])
        d, m, v = _adamw(two_d(a[n]), two_d(g), two_d(a["m_" + n]), two_d(a["v_" + n]), "adamw_" + n)
        grads[n], delta[n], new_m[n], new_v[n] = g, d.reshape(shp), m.reshape(shp), v.reshape(shp)

    return (loss, gx.reshape(1, L, D_MODEL), *[grads[n] for n in WEIGHTS], *[delta[n] for n in WEIGHTS],
            *[new_m[n] for n in WEIGHTS], *[new_v[n] for n in WEIGHTS])
```

```python
import functools
import math

import numpy as np
import jax
import jax.numpy as jnp
from jax import lax
from jax.experimental import pallas as pl
from jax.experimental.pallas import tpu as pltpu

F32 = jnp.float32
_MXU = jnp.bfloat16
_ACT = jnp.bfloat16
_WIRE = jnp.bfloat16

D_MODEL = 1024
HEAD_DIM = 64
A_HEADS = 4
CHUNK = 128
B_Q_HEADS = 8
B_KV_HEADS = 2
B_GROUP = 4
WINDOW = 128
ROPE_THETA = 10000.0
C_WIDTH = 256
C_GROUP = 16
C_GROUPS = 16
C_STATE = 64
N_STATE = C_GROUPS * C_STATE
IN_A, IN_Q, IN_KV, IN_C = 512, 512, 128, 256
IN_COLS = 1536
D_FF = 4096
PLE_DIM = 256
EPS = 1e-6
NEG = -1e30
ADAM_LR, ADAM_B1, ADAM_B2, ADAM_EPS, ADAM_WD, ADAM_STEP = 0.001, 0.9, 0.999, 1e-08, 0.01, 10

LANES = 128
SUBLANES = 8
VMEM_BYTES = 64 * 2 ** 20
N_CHIPS = 4
MESH = pl.DeviceIdType.MESH


_MM_VMEM_BUDGET = 44 * 2 ** 20


def _vmem_limit(est_bytes):
    return int(min(max(2 * est_bytes + (8 << 20), 32 << 20), VMEM_BYTES - (6 << 20)))


def _cparams(est_bytes, **kw):
    return pltpu.CompilerParams(vmem_limit_bytes=_vmem_limit(est_bytes), **kw)


def _sds(shape, dtype):
    return pltpu.HBM(tuple(shape), dtype)


def _nbytes(shape, dtype):
    return int(np.prod(shape)) * jnp.dtype(dtype).itemsize


def _tile(dim, pref):
    t = min(dim, pref)
    while dim % t:
        t -= LANES
    assert t > 0, (dim, pref)
    return t


def _lane(shape):
    return lax.broadcasted_iota(jnp.int32, shape, len(shape) - 1)


def _row(shape):
    return lax.broadcasted_iota(jnp.int32, shape, len(shape) - 2)


def _gelu(x):
    c = math.sqrt(2.0 / math.pi)
    return 0.5 * x * (1.0 + jnp.tanh(c * (x + 0.044715 * (x * x * x))))


def _gelu_grad(x):
    c = math.sqrt(2.0 / math.pi)
    t = jnp.tanh(c * (x + 0.044715 * (x * x * x)))
    return 0.5 * (1.0 + t) + 0.5 * x * (1.0 - t * t) * (c * (1.0 + 3.0 * 0.044715 * (x * x)))


def _sigmoid(x):
    return 1.0 / (1.0 + jnp.exp(-x))


def _dot(a, b, dims=(((1,), (0,)), ((), ()))):
    return lax.dot_general(a.astype(_MXU), b.astype(_MXU), dims, preferred_element_type=F32)


_NT = (((1,), (1,)), ((), ()))
_TN = (((0,), (0,)), ((), ()))
_NN = (((1,), (0,)), ((), ()))


def _mm(a, b, *, mode, M, N, K, out_dtypes, name, epi=None, extras=(), b_cb=False, o_cb=False,
        a_off=0, b_off=0, tm=1024, tn=1024, tk=1024, a_lyr=None, b_lyr=None, o_stack=None, n_acc=0):
    if isinstance(a, tuple):
        a, a_lyr = a
    if isinstance(b, tuple):
        b, b_lyr = b
    if b_cb or o_cb:
        nc = (b.shape[-1] if b_cb else N // N_CHIPS)
    tn_nom = nc if ((mode == "nn" and b_cb) or (mode == "tn" and o_cb)) else _tile(N, tn)
    tk_nom = nc if (mode == "nt" and b_cb) else _tile(K, tk)
    item = lambda d: jnp.dtype(d).itemsize
    per_row = tk_nom * item(a.dtype) + tn_nom * (sum(item(d) for d in out_dtypes)
                                                   + sum(item(e.dtype) for e, _ in extras if e.shape[0] > 1))
    fixed = tk_nom * tn_nom * item(b.dtype)
    tm = _tile(M, tm)
    while tm > 256 and M % (tm // 2) == 0 and 2 * (tm * per_row + fixed) + 8 * tm * tn_nom > _MM_VMEM_BUDGET:
        tm //= 2

    def spec(block, imap, lyr=None):
        if lyr is None:
            return pl.BlockSpec(block, imap)
        return pl.BlockSpec((None,) + block, lambda i, j, k: (lyr,) + imap(i, j, k))

    if mode == "nn":
        if b_cb:
            tn = nc
        tm, tn, tk = _tile(M, tm), _tile(N, tn), _tile(K, tk)
        a_spec = spec((tm, tk), lambda i, j, k: (i, k + a_off), a_lyr)
        if b_cb:
            b_spec = spec((None, tk, tn), lambda i, j, k: (j, k, 0), b_lyr)
        else:
            b_spec = spec((tk, tn), lambda i, j, k: (k, j + b_off), b_lyr)
        dims = _NN
        a_blk, b_blk = (tm, tk), (tk, tn)
    elif mode == "nt":
        if b_cb:
            tk = nc
        tm, tn, tk = _tile(M, tm), _tile(N, tn), _tile(K, tk)
        a_spec = spec((tm, tk), lambda i, j, k: (i, k + a_off), a_lyr)
        if b_cb:
            b_spec = spec((None, tn, tk), lambda i, j, k: (k, j, 0), b_lyr)
        else:
            b_spec = spec((tn, tk), lambda i, j, k: (j, k + b_off), b_lyr)
        dims = _NT
        a_blk, b_blk = (tm, tk), (tn, tk)
    else:
        if o_cb:
            tn = nc
        tm, tn, tk = _tile(M, tm), _tile(N, tn), _tile(K, tk)
        a_spec = spec((tk, tm), lambda i, j, k: (k, i + a_off), a_lyr)
        b_spec = spec((tk, tn), lambda i, j, k: (k, j + b_off), b_lyr)
        dims = _TN
        a_blk, b_blk = (tk, tm), (tk, tn)
    gi, gj, gk = M // tm, N // tn, K // tk
    o_lyr = None if o_stack is None else o_stack[1]
    if o_cb:
        o_spec = spec((None, tm, tn), lambda i, j, k: (j, i, 0), o_lyr)
        o_shape = (gj, M, tn)
    else:
        o_spec = spec((tm, tn), lambda i, j, k: (i, j), o_lyr)
        o_shape = (M, N)
    e_specs = []
    for e, off in extras:
        if e.shape[0] == 1:
            e_specs.append(pl.BlockSpec((1, tn), lambda i, j, k, off=off: (0, j + off)))
        else:
            e_specs.append(pl.BlockSpec((tm, tn), lambda i, j, k, off=off: (i, j + off)))
    extras = [e for e, _ in extras]
    ne, no = len(extras), len(out_dtypes)
    operands = [a, b, *extras]
    in_specs = [a_spec, b_spec] + e_specs
    out_shape = [_sds(o_shape, d) for d in out_dtypes]
    aliases = {}
    if o_stack is not None:
        assert no == 1 and o_stack[0].shape[1:] == o_shape and o_stack[0].dtype == out_dtypes[0]
        operands.append(o_stack[0])
        in_specs.append(pl.BlockSpec(memory_space=pl.ANY))
        out_shape = [_sds(o_stack[0].shape, o_stack[0].dtype)]
        aliases = {len(operands) - 1: 0}
    nin = len(operands)
    out_specs = [o_spec] * no
    if n_acc:
        assert gj == 1 and o_stack is None
        out_specs[no - n_acc:] = [pl.BlockSpec((1, tn), lambda i, j, k: (0, 0))] * n_acc
        out_shape[no - n_acc:] = [_sds((1, N), d) for d in out_dtypes[no - n_acc:]]

    def body(*refs):
        a_ref, b_ref = refs[0], refs[1]
        e_refs = refs[2:2 + ne]
        o_refs = refs[nin:nin + no]
        first_rows = pl.program_id(0) == 0

        def fin(acc):
            vals = epi(acc, *[e[...] for e in e_refs]) if epi is not None else (acc,)
            for t, (o, v) in enumerate(zip(o_refs, vals)):
                if t < no - n_acc:
                    o[...] = v.astype(o.dtype)
                else:
                    @pl.when(first_rows)
                    def _():
                        o[...] = jnp.zeros_like(o)

                    o[...] += v.astype(o.dtype)

        prod = _dot(a_ref[...], b_ref[...], dims)
        if gk == 1:
            fin(prod)
        else:
            acc_ref = refs[-1]
            k = pl.program_id(2)

            @pl.when(k == 0)
            def _():
                acc_ref[...] = prod

            @pl.when(k > 0)
            def _():
                acc_ref[...] += prod

            @pl.when(k == gk - 1)
            def _():
                fin(acc_ref[...])

    est = (_nbytes(a_blk, a.dtype) + _nbytes(b_blk, b.dtype)
           + sum(_nbytes((tm, tn), d) for d in out_dtypes)
           + sum(_nbytes((tm, tn), e.dtype) for e in extras)) + 2 * _nbytes((tm, tn), F32)
    outs = pl.pallas_call(
        body, name=name, grid=(gi, gj, gk),
        in_specs=in_specs,
        out_specs=out_specs,
        out_shape=out_shape,
        scratch_shapes=([pltpu.VMEM((tm, tn), F32)] if gk > 1 else []),
        input_output_aliases=aliases,
        compiler_params=_cparams(est, dimension_semantics=(("arbitrary" if n_acc else "parallel"), "parallel", "arbitrary")),
    )(*operands)
    return outs if no > 1 else outs[0]


_TL = 512


def _rms_fwd(h, g, name):
    L, D = h.shape
    tl = _tile(L, _TL)

    def body(h_ref, g_ref, o_ref):
        x = h_ref[...]
        r = lax.rsqrt(jnp.mean(x * x, axis=-1, keepdims=True) + EPS)
        o_ref[...] = ((x * r) * g_ref[...]).astype(o_ref.dtype)

    return pl.pallas_call(
        body, name=name, grid=(L // tl,),
        in_specs=[pl.BlockSpec((tl, D), lambda i: (i, 0)), pl.BlockSpec((1, D), lambda i: (0, 0))],
        out_specs=pl.BlockSpec((tl, D), lambda i: (i, 0)),
        out_shape=_sds((L, D), _ACT),
        compiler_params=_cparams(3 * _nbytes((tl, D), F32)),
    )(h, g.reshape(1, D))


def _rms_bwd(dxn, h, g, dres, name):
    L, D = h.shape
    tl = _tile(L, _TL)

    def body(d_ref, h_ref, g_ref, r_ref, o_ref, dg_ref):
        x = h_ref[...]
        r = lax.rsqrt(jnp.mean(x * x, axis=-1, keepdims=True) + EPS)
        xhat = x * r
        d = d_ref[...].astype(F32)
        gy = d * g_ref[...]
        dx = r * (gy - xhat * jnp.mean(gy * xhat, axis=-1, keepdims=True))
        o_ref[...] = r_ref[...] + dx

        @pl.when(pl.program_id(0) == 0)
        def _():
            dg_ref[...] = jnp.zeros_like(dg_ref)

        dg_ref[...] += jnp.sum(d * xhat, axis=0, keepdims=True)

    dh, dg = pl.pallas_call(
        body, name=name, grid=(L // tl,),
        in_specs=[pl.BlockSpec((tl, D), lambda i: (i, 0)), pl.BlockSpec((tl, D), lambda i: (i, 0)),
                  pl.BlockSpec((1, D), lambda i: (0, 0)), pl.BlockSpec((tl, D), lambda i: (i, 0))],
        out_specs=[pl.BlockSpec((tl, D), lambda i: (i, 0)), pl.BlockSpec((1, D), lambda i: (0, 0))],
        out_shape=[_sds((L, D), F32), _sds((1, D), F32)],
        compiler_params=_cparams(5 * _nbytes((tl, D), F32)),
    )(dxn, h, g.reshape(1, D), dres)
    return dh, dg.reshape(D)


def _rope_tables(positions):
    L = positions.shape[0]
    tl = _tile(L, 1024)
    inv = 1.0 / (ROPE_THETA ** (np.arange(0, HEAD_DIM, 2, dtype=np.float32) / HEAD_DIM))
    inv128 = jnp.asarray(np.tile(inv.astype(np.float32), 4).reshape(1, LANES))

    def body(p_ref, i_ref, c_ref, s_ref):
        ang = p_ref[...].astype(F32) * i_ref[...]
        c_ref[...] = jnp.cos(ang)
        s_ref[...] = jnp.sin(ang)

    return pl.pallas_call(
        body, name="rope_tables", grid=(L // tl,),
        in_specs=[pl.BlockSpec((tl, 1), lambda i: (i, 0)), pl.BlockSpec((1, LANES), lambda i: (0, 0))],
        out_specs=[pl.BlockSpec((tl, LANES), lambda i: (i, 0))] * 2,
        out_shape=[_sds((L, LANES), F32)] * 2,
    )(positions.reshape(L, 1), inv128)


_GM_TL = 256


def _gmlp_head(Z, W, bfull, lg, lb, maskv):
    G = _gelu(Z)
    mu = jnp.sum(jnp.where(maskv, G, 0.0), axis=-1, keepdims=True) * (1.0 / HEAD_DIM)
    xc = jnp.where(maskv, G - mu, 0.0)
    var = jnp.sum(xc * xc, axis=-1, keepdims=True) * (1.0 / HEAD_DIM)
    rstd = lax.rsqrt(var + EPS)
    xhat = xc * rstd
    vn = xhat * lg + lb
    sv = _dot(W, vn) + bfull
    return G, xhat, rstd, vn, sv


def _tril(W):
    return jnp.where(_row(W.shape) >= _lane(W.shape), W, 0.0)


def _triu(W):
    return jnp.where(_row(W.shape) <= _lane(W.shape), W, 0.0)


def _gmlp_fwd(z, ws, bfull, lgf, lbf, name):
    L = z.shape[0]
    tl = _tile(L, _GM_TL)
    nch = tl // CHUNK

    def body(z_ref, w_ref, b_ref, lg_ref, lb_ref, o_ref):
        maskv = _lane((CHUNK, LANES)) >= HEAD_DIM
        for c in range(nch):
            rows = slice(c * CHUNK, (c + 1) * CHUNK)
            for hp in range(A_HEADS // 2):
                acc = None
                for hh in range(2):
                    h = 2 * hp + hh
                    Z = z_ref[rows, h * LANES:(h + 1) * LANES]
                    G, _, _, _, sv = _gmlp_head(Z, _tril(w_ref[h]), b_ref[h], lg_ref[h:h + 1, :], lb_ref[h:h + 1, :], maskv)
                    prod = G * pltpu.roll(sv, HEAD_DIM, axis=1)
                    acc = prod if hh == 0 else acc + pltpu.roll(prod, HEAD_DIM, axis=1)
                o_ref[rows, hp * LANES:(hp + 1) * LANES] = acc

    return pl.pallas_call(
        body, name=name, grid=(L // tl,),
        in_specs=[pl.BlockSpec((tl, IN_A), lambda i: (i, 0)),
                  pl.BlockSpec((A_HEADS, CHUNK, CHUNK), lambda i: (0, 0, 0)),
                  pl.BlockSpec((A_HEADS, CHUNK, LANES), lambda i: (0, 0, 0)),
                  pl.BlockSpec((A_HEADS, LANES), lambda i: (0, 0)),
                  pl.BlockSpec((A_HEADS, LANES), lambda i: (0, 0))],
        out_specs=pl.BlockSpec((tl, 2 * LANES), lambda i: (i, 0)),
        out_shape=_sds((L, 2 * LANES), F32),
    )(z, ws, bfull, lgf, lbf)


def _gmlp_bwd(z, dya, ws, wsT, bfull, lgf, lbf, name):
    L = z.shape[0]
    tl = _tile(L, _GM_TL)
    nch = tl // CHUNK
    nsteps = L // tl

    def body(z_ref, d_ref, w_ref, wt_ref, b_ref, lg_ref, lb_ref, dz_ref, dw_ref, db_ref, dlg_ref, dlb_ref):
        step = pl.program_id(0)

        @pl.when(step == 0)
        def _():
            dw_ref[...] = jnp.zeros_like(dw_ref)
            db_ref[...] = jnp.zeros_like(db_ref)
            dlg_ref[...] = jnp.zeros_like(dlg_ref)
            dlb_ref[...] = jnp.zeros_like(dlb_ref)

        lane = _lane((CHUNK, LANES))
        maskv = lane >= HEAD_DIM
        for c in range(nch):
            rows = slice(c * CHUNK, (c + 1) * CHUNK)
            for h in range(A_HEADS):
                hp, hh = divmod(h, 2)
                Z = z_ref[rows, h * LANES:(h + 1) * LANES]
                lg = lg_ref[h:h + 1, :]
                G, xhat, rstd, vn, sv = _gmlp_head(Z, _tril(w_ref[h]), b_ref[h], lg, lb_ref[h:h + 1, :], maskv)
                dpair = d_ref[rows, hp * LANES:(hp + 1) * LANES]
                if hh == 1:
                    dpair = pltpu.roll(dpair, HEAD_DIM, axis=1)
                dout = jnp.where(maskv, 0.0, dpair)
                du = dout * pltpu.roll(sv, HEAD_DIM, axis=1)
                dsv = pltpu.roll(dout * G, HEAD_DIM, axis=1)
                dw_ref[h] += _tril(_dot(dsv, vn, _NT))
                db_ref[h] += dsv
                dvn = _dot(_triu(wt_ref[h]), dsv)
                dlg_ref[h] += dvn * xhat
                dlb_ref[h] += dvn
                dxh = dvn * lg
                m1 = jnp.sum(dxh, axis=-1, keepdims=True) * (1.0 / HEAD_DIM)
                m2 = jnp.sum(dxh * xhat, axis=-1, keepdims=True) * (1.0 / HEAD_DIM)
                dv = jnp.where(maskv, rstd * (dxh - m1 - xhat * m2), 0.0)
                dz_ref[rows, h * LANES:(h + 1) * LANES] = ((du + dv) * _gelu_grad(Z)).astype(dz_ref.dtype)

        @pl.when(step == nsteps - 1)
        def _():
            for h in range(A_HEADS):
                db_ref[h] = jnp.broadcast_to(jnp.sum(db_ref[h], axis=1, keepdims=True), (CHUNK, LANES))
                dlg_ref[h] = jnp.broadcast_to(jnp.sum(dlg_ref[h], axis=0, keepdims=True), (CHUNK, LANES))
                dlb_ref[h] = jnp.broadcast_to(jnp.sum(dlb_ref[h], axis=0, keepdims=True), (CHUNK, LANES))

    full3 = pl.BlockSpec((A_HEADS, CHUNK, LANES), lambda i: (0, 0, 0))
    return pl.pallas_call(
        body, name=name, grid=(nsteps,),
        in_specs=[pl.BlockSpec((tl, IN_A), lambda i: (i, 0)),
                  pl.BlockSpec((tl, 2 * LANES), lambda i: (i, 0)),
                  full3, full3, full3,
                  pl.BlockSpec((A_HEADS, LANES), lambda i: (0, 0)),
                  pl.BlockSpec((A_HEADS, LANES), lambda i: (0, 0))],
        out_specs=[pl.BlockSpec((tl, IN_A), lambda i: (i, 0)), full3, full3, full3, full3],
        out_shape=[_sds((L, IN_A), _ACT)] + [_sds((A_HEADS, CHUNK, LANES), F32)] * 4,
    )(z, dya, ws, wsT, bfull, lgf, lbf)


def _head_rstd(x, lo):
    sq = x * x
    s_lo = jnp.sum(jnp.where(lo, sq, 0.0), axis=-1, keepdims=True)
    s_hi = jnp.sum(jnp.where(lo, 0.0, sq), axis=-1, keepdims=True)
    return jnp.where(lo, lax.rsqrt(s_lo * (1.0 / HEAD_DIM) + EPS), lax.rsqrt(s_hi * (1.0 / HEAD_DIM) + EPS))


def _rot_half(x, first):
    return jnp.where(first, -pltpu.roll(x, LANES - HEAD_DIM // 2, axis=1), pltpu.roll(x, HEAD_DIM // 2, axis=1))


def _qk_prep(z, cos, sin, gq, gk, name):
    L = z.shape[0]
    tl = _tile(L, _TL)
    nq = IN_Q // LANES

    def body(q_ref, k_ref, c_ref, s_ref, gq_ref, gk_ref, qo_ref, ko_ref):
        lane = _lane((tl, LANES))
        lo = lane < HEAD_DIM
        first = (lane % HEAD_DIM) < (HEAD_DIM // 2)
        c, s = c_ref[...], s_ref[...]

        def prep(x, g):
            xn = (x * _head_rstd(x, lo)) * g
            return xn * c + _rot_half(xn, first) * s

        for j in range(nq):
            qo_ref[:, j * LANES:(j + 1) * LANES] = prep(q_ref[:, j * LANES:(j + 1) * LANES], gq_ref[...]).astype(qo_ref.dtype)
        ko_ref[...] = prep(k_ref[...], gk_ref[...]).astype(ko_ref.dtype)

    return pl.pallas_call(
        body, name=name, grid=(L // tl,),
        in_specs=[pl.BlockSpec((tl, IN_Q), lambda i: (i, 1)),
                  pl.BlockSpec((tl, IN_KV), lambda i: (i, 8)),
                  pl.BlockSpec((tl, LANES), lambda i: (i, 0)), pl.BlockSpec((tl, LANES), lambda i: (i, 0)),
                  pl.BlockSpec((1, LANES), lambda i: (0, 0)), pl.BlockSpec((1, LANES), lambda i: (0, 0))],
        out_specs=[pl.BlockSpec((tl, IN_Q), lambda i: (i, 0)), pl.BlockSpec((tl, IN_KV), lambda i: (i, 0))],
        out_shape=[_sds((L, IN_Q), _ACT), _sds((L, IN_KV), _ACT)],
    )(z, z, cos, sin, gq, gk)


def _qk_prep_bwd(z, dq, dkc, dkp, dvc, dvp, cos, sin, gq, gk, name):
    L = z.shape[0]
    tl = _ATT_QB * WINDOW
    nb = L // tl
    nq = IN_Q // LANES

    def body(q_ref, k_ref, dq_ref, dkc_ref, dkp_ref, dvc_ref, dvp_ref, c_ref, s_ref, gq_ref, gk_ref,
             dzq_ref, dzk_ref, dzv_ref, dgq_ref, dgk_ref):
        n = pl.program_id(0)

        @pl.when(n == 0)
        def _():
            dgq_ref[...] = jnp.zeros_like(dgq_ref)
            dgk_ref[...] = jnp.zeros_like(dgk_ref)

        lane = _lane((tl, LANES))
        lo = lane < HEAD_DIM
        first = (lane % HEAD_DIM) < (HEAD_DIM // 2)
        c, s = c_ref[...], s_ref[...]
        has_next = jnp.where(n < nb - 1, 1.0, 0.0)

        def bwd(x, g, dy):
            r = _head_rstd(x, lo)
            xhat = x * r
            dxn = dy * c - _rot_half(dy * s, first)
            gy = dxn * g
            t = gy * xhat
            m_lo = jnp.sum(jnp.where(lo, t, 0.0), axis=-1, keepdims=True)
            m_hi = jnp.sum(jnp.where(lo, 0.0, t), axis=-1, keepdims=True)
            m = jnp.where(lo, m_lo, m_hi) * (1.0 / HEAD_DIM)
            dx = r * (gy - xhat * m)
            dg = jnp.sum(dxn * xhat, axis=0, keepdims=True)
            return dx, dg

        dgq = jnp.zeros((1, LANES), F32)
        for j in range(nq):
            sl = slice(j * LANES, (j + 1) * LANES)
            dx, dg = bwd(q_ref[:, sl], gq_ref[...], dq_ref[:, sl].astype(F32))
            dzq_ref[:, sl] = dx.astype(dzq_ref.dtype)
            dgq = dgq + dg
        dgq_ref[...] += dgq + pltpu.roll(dgq, HEAD_DIM, axis=1)
        def with_next(cur_ref, nxt_ref):
            head = jnp.zeros((tl - WINDOW, IN_KV), F32)
            return cur_ref[...] + jnp.concatenate([head, has_next * nxt_ref[...]], axis=0)

        dx, dg = bwd(k_ref[...], gk_ref[...], with_next(dkc_ref, dkp_ref))
        dzk_ref[...] = dx.astype(dzk_ref.dtype)
        dgk_ref[...] += dg + pltpu.roll(dg, HEAD_DIM, axis=1)
        dzv_ref[...] = with_next(dvc_ref, dvp_ref).astype(dzv_ref.dtype)

    nxt = lambda i: (jnp.minimum(i + 1, nb - 1), 0)
    cur = lambda i: (i, 0)
    kv = pl.BlockSpec((tl, IN_KV), cur)
    kvn = pl.BlockSpec((WINDOW, IN_KV), nxt)
    one = pl.BlockSpec((1, LANES), lambda i: (0, 0))
    return pl.pallas_call(
        body, name=name, grid=(nb,),
        in_specs=[pl.BlockSpec((tl, IN_Q), lambda i: (i, 1)), pl.BlockSpec((tl, IN_KV), lambda i: (i, 8)),
                  pl.BlockSpec((tl, IN_Q), cur), kv, kvn, kv, kvn,
                  kv, kv, one, one],
        out_specs=[pl.BlockSpec((tl, IN_Q), cur), kv, kv, one, one],
        out_shape=[_sds((L, IN_Q), _ACT), _sds((L, IN_KV), _ACT),
                   _sds((L, IN_KV), _ACT), _sds((1, LANES), F32),
                   _sds((1, LANES), F32)],
    )(z, z, dq, dkc, dkp, dvc, dvp, cos, sin, gq, gk)


def _attn_mask(n):
    shp = (2 * WINDOW, B_GROUP * WINDOW)
    qi = _lane(shp) % WINDOW
    kj = _row(shp)
    off = 0 if n is None else jnp.where(n > 0, 0, 4 * WINDOW)
    return ((kj >= WINDOW) & (kj - WINDOW <= qi)) | ((kj < WINDOW) & (kj > qi + off))


def _kv_lanes(j):
    lane = _lane((WINDOW, LANES))
    return (lane >= j * HEAD_DIM) & (lane < (j + 1) * HEAD_DIM)


_ATT_QB = 4


def _stack_heads(ref, rows, j, kvl):
    parts = []
    for g in range(B_GROUP):
        h = j * B_GROUP + g
        slab = ref[rows, (h // 2) * LANES:(h // 2 + 1) * LANES].astype(F32)
        if (h % 2) != j:
            slab = pltpu.roll(slab, HEAD_DIM, axis=1)
        parts.append(jnp.where(kvl, slab, 0.0))
    return jnp.concatenate(parts, axis=0)


def _attn_probs(qs, k2, sink_row, mask):
    s = _dot(k2, qs, _NT) * (HEAD_DIM ** -0.5)
    s = jnp.where(mask, s, NEG)
    m = jnp.maximum(jnp.max(s, axis=0, keepdims=True), sink_row)
    p = jnp.exp(s - m)
    esink = jnp.exp(sink_row - m)
    inv = 1.0 / (jnp.sum(p, axis=0, keepdims=True) + esink)
    return p * inv, esink * inv


def _sink_row(sink_ref, j):
    lane = _lane((1, B_GROUP * WINDOW))
    row = jnp.full((1, B_GROUP * WINDOW), sink_ref[j * B_GROUP], F32)
    for g in range(1, B_GROUP):
        row = jnp.where(lane >= g * WINDOW, sink_ref[j * B_GROUP + g], row)
    return row


def _attn_fwd(q, k, z, sinks, name):
    L = q.shape[0]
    QB = _ATT_QB
    tq = QB * WINDOW
    prev = lambda n: (jnp.maximum(QB * n - 1, 0), 0)
    prev_v = lambda n: (jnp.maximum(QB * n - 1, 0), 9)

    def body(s_ref, q_ref, kp_ref, kc_ref, vp_ref, vc_ref, o_ref):
        n = pl.program_id(0)
        k3 = jnp.concatenate([kp_ref[...], kc_ref[...]], axis=0)
        v3 = jnp.concatenate([vp_ref[...], vc_ref[...]], axis=0)
        for b in range(QB):
            rows = slice(b * WINDOW, (b + 1) * WINDOW)
            mask = _attn_mask(n if b == 0 else None)
            k2 = k3[b * WINDOW:(b + 2) * WINDOW]
            v2 = v3[b * WINDOW:(b + 2) * WINDOW]
            slabs = [None] * (IN_Q // LANES)
            for j in range(B_KV_HEADS):
                kvl = _kv_lanes(j)
                qs = _stack_heads(q_ref, rows, j, kvl)
                pn, _ = _attn_probs(qs, k2, _sink_row(s_ref, j), mask)
                o = _dot(pn, v2, _TN)
                for g in range(B_GROUP):
                    h = j * B_GROUP + g
                    piece = jnp.where(kvl, o[g * WINDOW:(g + 1) * WINDOW], 0.0)
                    if (h % 2) != j:
                        piece = pltpu.roll(piece, HEAD_DIM, axis=1)
                    slabs[h // 2] = piece if slabs[h // 2] is None else slabs[h // 2] + piece
            for t, sl in enumerate(slabs):
                o_ref[rows, t * LANES:(t + 1) * LANES] = sl

    return pl.pallas_call(
        body, name=name, grid=(L // tq,),
        in_specs=[pl.BlockSpec(memory_space=pltpu.SMEM),
                  pl.BlockSpec((tq, IN_Q), lambda n: (n, 0)),
                  pl.BlockSpec((WINDOW, IN_KV), prev), pl.BlockSpec((tq, IN_KV), lambda n: (n, 0)),
                  pl.BlockSpec((WINDOW, IN_KV), prev_v), pl.BlockSpec((tq, IN_KV), lambda n: (n, 9))],
        out_specs=pl.BlockSpec((tq, IN_Q), lambda n: (n, 0)),
        out_shape=_sds((L, IN_Q), F32),
    )(sinks, q, k, k, z, z)


def _attn_bwd(q, k, z, sinks, dyb, name):
    L = q.shape[0]
    QB = _ATT_QB
    tq = QB * WINDOW
    nsteps = L // tq
    prev = lambda n: (jnp.maximum(QB * n - 1, 0), 0)
    prev_v = lambda n: (jnp.maximum(QB * n - 1, 0), 9)
    cur = lambda n: (n, 0)

    def body(s_ref, q_ref, kp_ref, kc_ref, vp_ref, vc_ref, d_ref, dq_ref, dkc_ref, dkp_ref, dvc_ref, dvp_ref, ds_ref):
        n = pl.program_id(0)

        @pl.when(n == 0)
        def _():
            ds_ref[...] = jnp.zeros_like(ds_ref)

        k3 = jnp.concatenate([kp_ref[...], kc_ref[...]], axis=0)
        v3 = jnp.concatenate([vp_ref[...], vc_ref[...]], axis=0)
        dkb = [None] * (QB + 1)
        dvb = [None] * (QB + 1)
        dsink = jnp.zeros((1, LANES), F32)
        lane1 = _lane((1, LANES))
        add = lambda acc, v: v if acc is None else acc + v
        for b in range(QB):
            rows = slice(b * WINDOW, (b + 1) * WINDOW)
            mask = _attn_mask(n if b == 0 else None)
            k2 = k3[b * WINDOW:(b + 2) * WINDOW]
            v2 = v3[b * WINDOW:(b + 2) * WINDOW]
            slabs = [None] * (IN_Q // LANES)
            for j in range(B_KV_HEADS):
                kvl = _kv_lanes(j)
                qs = _stack_heads(q_ref, rows, j, kvl)
                dos = _stack_heads(d_ref, rows, j, kvl)
                pn, psink = _attn_probs(qs, k2, _sink_row(s_ref, j), mask)
                dp = _dot(v2, dos, _NT)
                dd = jnp.sum(pn * dp, axis=0, keepdims=True)
                dss = (pn * (dp - dd)) * (HEAD_DIM ** -0.5)
                dqs = _dot(dss, k2, _TN)
                dk2 = _dot(dss, qs)
                dv2 = _dot(pn, dos)
                dkb[b], dkb[b + 1] = add(dkb[b], dk2[:WINDOW]), add(dkb[b + 1], dk2[WINDOW:])
                dvb[b], dvb[b + 1] = add(dvb[b], dv2[:WINDOW]), add(dvb[b + 1], dv2[WINDOW:])
                sd = psink * dd
                for g in range(B_GROUP):
                    h = j * B_GROUP + g
                    piece = jnp.where(kvl, dqs[g * WINDOW:(g + 1) * WINDOW], 0.0)
                    if (h % 2) != j:
                        piece = pltpu.roll(piece, HEAD_DIM, axis=1)
                    slabs[h // 2] = piece if slabs[h // 2] is None else slabs[h // 2] + piece
                    tot = jnp.sum(sd[:, g * WINDOW:(g + 1) * WINDOW], axis=1, keepdims=True)
                    dsink = dsink - jnp.where(lane1 == h, tot, 0.0)
            for t, sl in enumerate(slabs):
                dq_ref[rows, t * LANES:(t + 1) * LANES] = sl
        dkp_ref[...] = dkb[0]
        dvp_ref[...] = dvb[0]
        for b in range(QB):
            dkc_ref[b * WINDOW:(b + 1) * WINDOW, :] = dkb[b + 1]
            dvc_ref[b * WINDOW:(b + 1) * WINDOW, :] = dvb[b + 1]
        ds_ref[0:1, :] += dsink

    kvs = pl.BlockSpec((tq, IN_KV), cur)
    kvp = pl.BlockSpec((WINDOW, IN_KV), cur)
    kvo = _sds((L, IN_KV), F32)
    kvpo = _sds((nsteps * WINDOW, IN_KV), F32)
    return pl.pallas_call(
        body, name=name, grid=(nsteps,),
        in_specs=[pl.BlockSpec(memory_space=pltpu.SMEM),
                  pl.BlockSpec((tq, IN_Q), cur),
                  pl.BlockSpec((WINDOW, IN_KV), prev), kvs,
                  pl.BlockSpec((WINDOW, IN_KV), prev_v), pl.BlockSpec((tq, IN_KV), lambda n: (n, 9)),
                  pl.BlockSpec((tq, IN_Q), cur)],
        out_specs=[pl.BlockSpec((tq, IN_Q), cur), kvs, kvp, kvs, kvp, pl.BlockSpec((SUBLANES, LANES), lambda n: (0, 0))],
        out_shape=[_sds((L, IN_Q), F32), kvo, kvpo, kvo, kvpo, _sds((SUBLANES, LANES), F32)],
    )(sinks, q, k, k, z, z, dyb)


def _ssm_disc(are, aim, ldt, bre, bim):
    dt = jnp.exp(ldt)
    mag = jnp.exp(are * dt)
    lr, li = mag * jnp.cos(aim * dt), mag * jnp.sin(aim * dt)
    den = are * are + aim * aim
    xr, xi = lr - 1.0, li
    cr, ci = (xr * are + xi * aim) / den, (xi * are - xr * aim) / den
    return lr, li, cr * bre - ci * bim, cr * bim + ci * bre


def _ssm_prep(are, aim, ldt, bre, bim):
    shp3, shpb = are.shape, bre.shape

    def body(are_ref, aim_ref, ldt_ref, bre_ref, bim_ref, lr_ref, li_ref, br_ref, bi_ref):
        lr, li, br, bi = _ssm_disc(are_ref[...], aim_ref[...], ldt_ref[...], bre_ref[...], bim_ref[...])
        lr_ref[...] = lr
        li_ref[...] = li
        br_ref[...] = br
        bi_ref[...] = bi

    return pl.pallas_call(
        body, name="ssm_prep",
        out_shape=[_sds(shp3, F32)] * 2 + [_sds(shpb, F32)] * 2,
    )(are, aim, ldt, bre, bim)


def _ssm_prep_bwd(are, aim, ldt, bre, bim, dlr, dli, dbr, dbi):
    shp3, shpb = are.shape, bre.shape

    def body(are_ref, aim_ref, ldt_ref, bre_ref, bim_ref, dlr_ref, dli_ref, dbr_ref, dbi_ref,
             o_are, o_aim, o_ldt, o_bre, o_bim):
        _, vjp = jax.vjp(_ssm_disc, are_ref[...], aim_ref[...], ldt_ref[...], bre_ref[...], bim_ref[...])
        g = vjp((dlr_ref[...], dli_ref[...], dbr_ref[...], dbi_ref[...]))
        o_are[...] = g[0]
        o_aim[...] = g[1]
        o_ldt[...] = jnp.broadcast_to(jnp.sum(g[2], axis=-1, keepdims=True), shp3)
        o_bre[...] = g[3]
        o_bim[...] = g[4]

    return pl.pallas_call(
        body, name="ssm_prep_bwd",
        out_shape=[_sds(shp3, F32)] * 3 + [_sds(shpb, F32)] * 2,
    )(are, aim, ldt, bre, bim, dlr, dli, dbr, dbi)


_SCAN_TB = 512
_SCAN_W = 256


def _cmul(ar, ai, br, bi):
    return ar * br - ai * bi, ar * bi + ai * br


def _ssm_scan(x, lam_r, lam_i, name, reverse=False, states=None):
    L = x.shape[0]
    tb = _tile(L, _SCAN_TB)
    nrb = L // tb
    nt = tb // SUBLANES
    W = _SCAN_W
    with_da = states is not None

    def body(*refs):
        if with_da:
            xr_ref, xi_ref, sr_ref, si_ref, ar_ref, ai_ref, o_ref, dar_ref, dai_ref, cr_ref, ci_ref = refs
        else:
            xr_ref, xi_ref, ar_ref, ai_ref, o_ref, cr_ref, ci_ref = refs
        step = pl.program_id(0)

        @pl.when(step == 0)
        def _():
            cr_ref[...] = jnp.zeros_like(cr_ref)
            ci_ref[...] = jnp.zeros_like(ci_ref)
            if with_da:
                dar_ref[...] = jnp.zeros_like(dar_ref)
                dai_ref[...] = jnp.zeros_like(dai_ref)

        row = _row((SUBLANES, W))

        def shift(v, d, fill):
            if reverse:
                return jnp.where(row < SUBLANES - d, pltpu.roll(v, SUBLANES - d, axis=0), fill)
            return jnp.where(row >= d, pltpu.roll(v, d, axis=0), fill)

        edge = 0 if reverse else SUBLANES - 1
        for wb in range(N_STATE // W):
            cols = slice(wb * W, (wb + 1) * W)
            a1r = jnp.broadcast_to(ar_ref[:, cols], (SUBLANES, W))
            a1i = jnp.broadcast_to(ai_ref[:, cols], (SUBLANES, W))
            if reverse:
                a1i = -a1i
            a2r, a2i = _cmul(a1r, a1i, a1r, a1i)
            a4r, a4i = _cmul(a2r, a2i, a2r, a2i)
            pws = ((1, a1r, a1i), (2, a2r, a2i), (4, a4r, a4i))
            pr, pi = a1r, a1i
            for d, _, _ in pws:
                qr, qi = _cmul(pr, pi, shift(pr, d, 1.0), shift(pi, d, 0.0))
                pr, pi = qr, qi
            mws = []
            for d, er, ei in pws:
                ok = (row < SUBLANES - d) if reverse else (row >= d)
                mws.append(((SUBLANES - d) if reverse else d, jnp.where(ok, er, 0.0), jnp.where(ok, ei, 0.0)))

            def tile(i, carry):
                cr, ci, dr, di = carry
                t = (nt - 1 - i) if reverse else i
                r0 = pl.multiple_of(t * SUBLANES, SUBLANES)
                vr = xr_ref[pl.ds(r0, SUBLANES), cols]
                vi = xi_ref[pl.ds(r0, SUBLANES), cols]
                for sh, er, ei in mws:
                    tr, ti = _cmul(er, ei, pltpu.roll(vr, sh, axis=0), pltpu.roll(vi, sh, axis=0))
                    vr, vi = vr + tr, vi + ti
                tr, ti = _cmul(pr, pi, cr, ci)
                vr, vi = vr + tr, vi + ti
                o_ref[pl.ds(r0, SUBLANES), cols] = vr
                o_ref[pl.ds(r0, SUBLANES), slice(N_STATE + wb * W, N_STATE + (wb + 1) * W)] = vi
                if with_da:
                    gr = jnp.where(row < SUBLANES - 1, pltpu.roll(vr, SUBLANES - 1, axis=0), cr)
                    gi = jnp.where(row < SUBLANES - 1, pltpu.roll(vi, SUBLANES - 1, axis=0), ci)
                    sr = sr_ref[pl.ds(r0, SUBLANES), cols]
                    si = si_ref[pl.ds(r0, SUBLANES), cols]
                    dr = dr + sr * gr + si * gi
                    di = di + sr * gi - si * gr
                ncr = jnp.broadcast_to(vr[edge:edge + 1, :], (SUBLANES, W))
                nci = jnp.broadcast_to(vi[edge:edge + 1, :], (SUBLANES, W))
                return ncr, nci, dr, di

            zero = jnp.zeros((SUBLANES, W), F32)
            cr, ci, dr, di = lax.fori_loop(0, nt, tile, (cr_ref[:, cols], ci_ref[:, cols], zero, zero), unroll=2)
            cr_ref[:, cols] = cr
            ci_ref[:, cols] = ci
            if with_da:
                dar_ref[:, cols] += dr
                dai_ref[:, cols] += di

        if with_da:
            @pl.when(step == nrb - 1)
            def _():
                dar_ref[...] = jnp.broadcast_to(jnp.sum(dar_ref[...], axis=0, keepdims=True), dar_ref.shape)
                dai_ref[...] = jnp.broadcast_to(jnp.sum(dai_ref[...], axis=0, keepdims=True), dai_ref.shape)

    rb = (lambda i: (nrb - 1 - i, 0)) if reverse else (lambda i: (i, 0))
    rb_im = (lambda i: (nrb - 1 - i, 1)) if reverse else (lambda i: (i, 1))
    blk_r = pl.BlockSpec((tb, N_STATE), rb)
    blk_i = pl.BlockSpec((tb, N_STATE), rb_im)
    one = pl.BlockSpec((1, N_STATE), lambda i: (0, 0))
    acc = pl.BlockSpec((SUBLANES, N_STATE), lambda i: (0, 0))
    ins = [x, x] + ([states, states] if with_da else []) + [lam_r, lam_i]
    in_specs = [blk_r, blk_i] + ([blk_r, blk_i] if with_da else []) + [one, one]
    out_specs = [pl.BlockSpec((tb, 2 * N_STATE), rb)] + ([acc, acc] if with_da else [])
    out_shape = [_sds((L, 2 * N_STATE), F32)] + (
        [_sds((SUBLANES, N_STATE), F32)] * 2 if with_da else [])
    outs = pl.pallas_call(
        body, name=name, grid=(nrb,), in_specs=in_specs, out_specs=out_specs, out_shape=out_shape,
        scratch_shapes=[pltpu.VMEM((SUBLANES, N_STATE), F32)] * 2,
        compiler_params=_cparams((6 if with_da else 4) * _nbytes((tb, N_STATE), F32),
                                 dimension_semantics=("arbitrary",)),
    )(*ins)
    return outs if with_da else outs[0]


_GROUPS = ((0, 256), (256, 768), (768, 1024))


def _merge_fwd(ya, yb, g12, mixg, name):
    L = ya.shape[0]
    tl = _tile(L, _TL)

    def body(a_ref, b_ref, g_ref, m_ref, o_ref):
        g12v = g_ref[...]
        yc = g12v[:, :C_WIDTH] * _sigmoid(g12v[:, C_WIDTH:])
        for (lo, hi), y in zip(_GROUPS, (a_ref[...], b_ref[...], yc)):
            r = lax.rsqrt(jnp.mean(y * y, axis=-1, keepdims=True) + EPS)
            o_ref[:, lo:hi] = ((y * r) * m_ref[:, lo:hi]).astype(o_ref.dtype)

    row = lambda w: pl.BlockSpec((tl, w), lambda i: (i, 0))
    return pl.pallas_call(
        body, name=name, grid=(L // tl,),
        in_specs=[row(256), row(512), row(512), pl.BlockSpec((1, D_MODEL), lambda i: (0, 0))],
        out_specs=row(D_MODEL), out_shape=_sds((L, D_MODEL), _ACT),
    )(ya, yb, g12, mixg.reshape(1, D_MODEL))


def _merge_bwd(dy, ya, yb, g12, mixg, name):
    L = ya.shape[0]
    tl = _tile(L, _TL)

    def body(d_ref, a_ref, b_ref, g_ref, m_ref, da_ref, db_ref, dg_ref, dm_ref):
        @pl.when(pl.program_id(0) == 0)
        def _():
            dm_ref[...] = jnp.zeros_like(dm_ref)

        g12v = g_ref[...]
        g1, sg = g12v[:, :C_WIDTH], _sigmoid(g12v[:, C_WIDTH:])
        yc = g1 * sg
        outs = []
        for (lo, hi), y in zip(_GROUPS, (a_ref[...], b_ref[...], yc)):
            r = lax.rsqrt(jnp.mean(y * y, axis=-1, keepdims=True) + EPS)
            xhat = y * r
            d = d_ref[:, lo:hi]
            gy = d * m_ref[:, lo:hi]
            outs.append(r * (gy - xhat * jnp.mean(gy * xhat, axis=-1, keepdims=True)))
            dm_ref[:, lo:hi] += jnp.sum(d * xhat, axis=0, keepdims=True)
        da_ref[...] = outs[0]
        db_ref[...] = outs[1]
        dyc = outs[2]
        dg_ref[:, :C_WIDTH] = (dyc * sg).astype(dg_ref.dtype)
        dg_ref[:, C_WIDTH:] = (dyc * g1 * sg * (1.0 - sg)).astype(dg_ref.dtype)

    row = lambda w: pl.BlockSpec((tl, w), lambda i: (i, 0))
    one = pl.BlockSpec((1, D_MODEL), lambda i: (0, 0))
    return pl.pallas_call(
        body, name=name, grid=(L // tl,),
        in_specs=[row(D_MODEL), row(256), row(512), row(512), one],
        out_specs=[row(256), row(512), row(512), one],
        out_shape=[_sds((L, 256), F32), _sds((L, 512), F32),
                   _sds((L, 512), _ACT), _sds((1, D_MODEL), F32)],
    )(dy, ya, yb, g12, mixg.reshape(1, D_MODEL))


def _ple_bwd_elem(dh, gate, e, name):
    L, D = dh.shape
    tl = _tile(L, _TL)

    def body(d_ref, g_ref, e_ref, p_ref, o_ref):
        d, g = d_ref[...], g_ref[...]
        p_ref[...] = (d * e_ref[...] * g * (1.0 - g)).astype(p_ref.dtype)
        o_ref[...] = (d * g).astype(o_ref.dtype)

    row = pl.BlockSpec((tl, D), lambda i: (i, 0))
    return pl.pallas_call(
        body, name=name, grid=(L // tl,), in_specs=[row] * 3, out_specs=[row] * 2,
        out_shape=[_sds((L, D), _ACT)] * 2,
        compiler_params=_cparams(4 * _nbytes((tl, D), F32)),
    )(dh, gate, e)


def _dskip_bwd(dy, z, name):
    L = dy.shape[0]
    tl = _tile(L, _TL)

    def body(d_ref, u_ref, o_ref):
        @pl.when(pl.program_id(0) == 0)
        def _():
            o_ref[...] = jnp.zeros_like(o_ref)

        o_ref[...] += jnp.sum(d_ref[...] * u_ref[...], axis=0, keepdims=True)

    return pl.pallas_call(
        body, name=name, grid=(L // tl,),
        in_specs=[pl.BlockSpec((tl, C_WIDTH), lambda i: (i, 0)), pl.BlockSpec((tl, C_WIDTH), lambda i: (i, 5))],
        out_specs=pl.BlockSpec((1, C_WIDTH), lambda i: (0, 0)),
        out_shape=_sds((1, C_WIDTH), F32),
    )(dy, z)


def _loss_fwd_bwd(y, target):
    L, D = y.shape
    tl = _tile(L, _TL)

    def body(y_ref, t_ref, l_ref, d_ref):
        @pl.when(pl.program_id(0) == 0)
        def _():
            l_ref[...] = jnp.zeros_like(l_ref)

        e = y_ref[...] - t_ref[...]
        d_ref[...] = e * (1.0 / D)
        part = jnp.sum(jnp.sum(e * e, axis=-1, keepdims=True), axis=0, keepdims=True)
        l_ref[...] += jnp.broadcast_to(part, l_ref.shape)

    row = pl.BlockSpec((tl, D), lambda i: (i, 0))
    return pl.pallas_call(
        body, name="loss", grid=(L // tl,), in_specs=[row, row],
        out_specs=[pl.BlockSpec((SUBLANES, LANES), lambda i: (0, 0)), row],
        out_shape=[_sds((SUBLANES, LANES), F32), _sds((L, D), F32)],
    )(y, target)


def _adamw(w, g, m, v, name):
    R, C = w.shape
    tr = R if R <= 512 else _tile_rows(R, 512)

    def body(w_ref, g_ref, m_ref, v_ref, d_ref, nm_ref, nv_ref):
        gv = g_ref[...]
        nm = ADAM_B1 * m_ref[...] + (1.0 - ADAM_B1) * gv
        nv = ADAM_B2 * v_ref[...] + (1.0 - ADAM_B2) * (gv * gv)
        m_hat = nm / (1.0 - ADAM_B1 ** ADAM_STEP)
        v_hat = nv / (1.0 - ADAM_B2 ** ADAM_STEP)
        d_ref[...] = -ADAM_LR * (m_hat / (jnp.sqrt(v_hat) + ADAM_EPS) + ADAM_WD * w_ref[...])
        nm_ref[...] = nm
        nv_ref[...] = nv

    blk = pl.BlockSpec((tr, C), lambda i: (i, 0))
    return pl.pallas_call(
        body, name=name, grid=(R // tr,), in_specs=[blk] * 4, out_specs=[blk] * 3,
        out_shape=[_sds((R, C), F32)] * 3,
        compiler_params=_cparams(7 * _nbytes((tr, C), F32)),
    )(w, g, m, v)


def _tile_rows(R, pref):
    t = pref
    while R % t:
        t -= SUBLANES
    assert t > 0
    return t


def _add_n(xs, name):
    R, C = xs[0].shape
    tr = R if R <= 512 else _tile_rows(R, 512)
    n = len(xs)

    def body(*refs):
        acc = refs[0][...].astype(F32)
        for r in refs[1:n]:
            acc = acc + r[...].astype(F32)
        refs[n][...] = acc

    blk = pl.BlockSpec((tr, C), lambda i: (i, 0))
    return pl.pallas_call(
        body, name=name, grid=(R // tr,), in_specs=[blk] * n, out_specs=blk,
        out_shape=_sds((R, C), F32),
        compiler_params=_cparams((n + 1) * _nbytes((tr, C), F32)),
    )(*xs)


def _relu2(acc):
    r = jnp.maximum(acc, 0.0)
    return (r * r,)


def _rms_rows(x, g):
    return (x * lax.rsqrt(jnp.mean(x * x, axis=-1, keepdims=True) + EPS)) * g


def _resid_norm_epi(acc, res, g):
    h = res + acc
    return h, _rms_rows(h, g)


def _rms_bwd_epi(acc, h, dres, g):
    r = lax.rsqrt(jnp.mean(h * h, axis=-1, keepdims=True) + EPS)
    xhat = h * r
    gy = acc * g
    dx = r * (gy - xhat * jnp.mean(gy * xhat, axis=-1, keepdims=True))
    return dres + dx, jnp.sum(acc * xhat, axis=0, keepdims=True)


def _layer_fwd(h, xn, lp, cos, sin, g_next):
    L = h.shape[0]
    row = lambda n: lp[n].reshape(1, D_MODEL)
    z = _mm(xn, lp["w_in"], mode="nn", M=L, N=IN_COLS, K=D_MODEL, b_cb=True, out_dtypes=[F32], name="f_w_in")
    ya = _gmlp_fwd(z, lp["ws"], lp["bfull"], lp["lgf"], lp["lbf"], "f_gmlp")
    q, k = _qk_prep(z, cos, sin, lp["gq"], lp["gk"], "f_qk_prep")
    yb = _attn_fwd(q, k, z, lp["sinks"], "f_attn")
    bu = _mm(z, lp["bcat"], mode="nn", M=L, N=2 * N_STATE, K=C_WIDTH, a_off=5, tk=C_WIDTH,
             out_dtypes=[F32], name="f_ssm_in")
    S = _ssm_scan(bu, lp["lam_r"], lp["lam_i"], "f_ssm_scan")
    y, yg = _mm(S, lp["ccat"], mode="nn", M=L, N=C_WIDTH, K=2 * N_STATE, tk=2 * N_STATE,
                extras=[(z, 5), (lp["dskip"], 0)], out_dtypes=[F32, _ACT], name="f_ssm_out",
                epi=lambda acc, u, dsk: (acc + dsk * u, _gelu(acc + dsk * u)))
    g12 = _mm(yg, lp["w12"], mode="nn", M=L, N=2 * C_WIDTH, K=C_WIDTH, out_dtypes=[F32], name="f_glu")
    ycat = _merge_fwd(ya, yb, g12, lp["mix_out_g"], "f_merge")
    h1, hn = _mm(ycat, lp["w_out"], mode="nn", M=L, N=D_MODEL, K=D_MODEL, extras=[(h, 0), (row("mlp_norm_g"), 0)],
                 epi=_resid_norm_epi, out_dtypes=[F32, _ACT], name="f_w_out")
    r = _mm(hn, lp["w_ff1"], mode="nn", M=L, N=D_FF, K=D_MODEL, b_cb=True, epi=_relu2,
            out_dtypes=[_ACT], name="f_ff1")
    h2, hn3 = _mm(r, lp["w_ff2"], mode="nn", M=L, N=D_MODEL, K=D_FF, extras=[(h1, 0), (row("ple_norm_g"), 0)],
                  epi=_resid_norm_epi, out_dtypes=[F32, _ACT], name="f_ff2")
    e = _mm(lp["p"], lp["w_ple_proj"], mode="nn", M=L, N=D_MODEL, K=PLE_DIM, b_cb=True, tk=PLE_DIM,
            out_dtypes=[F32], name="f_ple_proj")

    def gate_epi(acc, h2_, e_, *g):
        gate_ = _sigmoid(acc)
        h3_ = h2_ + gate_ * e_
        return (h3_, gate_) + ((_rms_rows(h3_, g[0]),) if g else ())

    outs = _mm(hn3, lp["w_ple_gate"], mode="nn", M=L, N=D_MODEL, K=D_MODEL,
               extras=[(h2, 0), (e, 0)] + ([(g_next.reshape(1, D_MODEL), 0)] if g_next is not None else []),
               epi=gate_epi, out_dtypes=[F32, F32] + ([_ACT] if g_next is not None else []), name="f_ple_gate")
    h3, gate = outs[0], outs[1]
    xn_next = outs[2] if g_next is not None else None
    saved = dict(h=h, xn=xn, z=z, ya=ya, q=q, k=k, yb=yb, S=S, y=y, yg=yg, g12=g12, ycat=ycat, h1=h1, hn=hn,
                 r=r, h2=h2, hn3=hn3, e=e, gate=gate)
    return h3, xn_next, saved


def _layer_bwd(dh3, lp, sv, cos, sin):
    L = dh3.shape[0]
    z = sv["z"]
    dpre, de = _ple_bwd_elem(dh3, sv["gate"], sv["e"], "b_ple_elem")
    stk = lp["stk"]
    d_gate = _mm(sv["hn3"], dpre, mode="tn", M=D_MODEL, N=D_MODEL, K=L, out_dtypes=[F32], name="b_dw_gate",
                 o_stack=stk["w_ple_gate"])
    d_proj = _mm(lp["p"], de, mode="tn", M=PLE_DIM, N=D_MODEL, K=L, o_cb=True, tm=PLE_DIM,
                 out_dtypes=[F32], name="b_dw_proj", o_stack=stk["w_ple_proj"])
    row = lambda n: lp[n].reshape(1, D_MODEL)
    dh2, dg_ple = _mm(dpre, lp["w_ple_gate"], mode="nt", M=L, N=D_MODEL, K=D_MODEL,
                      extras=[(sv["h2"], 0), (dh3, 0), (row("ple_norm_g"), 0)], epi=_rms_bwd_epi,
                      out_dtypes=[F32, F32], n_acc=1, name="b_dx_gate")
    da = _mm(dh2, lp["w_ff2"], mode="nt", M=L, N=D_FF, K=D_MODEL, extras=[(sv["r"], 0)],
             epi=lambda acc, r_: (acc * (2.0 * jnp.sqrt(r_.astype(F32))),), out_dtypes=[_ACT], name="b_dx_ff2")
    d_ff2 = _mm(sv["r"], dh2, mode="tn", M=D_FF, N=D_MODEL, K=L, out_dtypes=[F32], name="b_dw_ff2",
                o_stack=stk["w_ff2"])
    d_ff1 = _mm(sv["hn"], da, mode="tn", M=D_MODEL, N=D_FF, K=L, o_cb=True, out_dtypes=[F32], name="b_dw_ff1",
                o_stack=stk["w_ff1"])
    dh1, dg_mlp = _mm(da, lp["w_ff1"], mode="nt", M=L, N=D_MODEL, K=D_FF, b_cb=True,
                      extras=[(sv["h1"], 0), (dh2, 0), (row("mlp_norm_g"), 0)], epi=_rms_bwd_epi,
                      out_dtypes=[F32, F32], n_acc=1, name="b_dx_ff1")
    d_out = _mm(sv["ycat"], dh1, mode="tn", M=D_MODEL, N=D_MODEL, K=L, out_dtypes=[F32], name="b_dw_out",
                o_stack=stk["w_out"])
    dycat = _mm(dh1, lp["w_out"], mode="nt", M=L, N=D_MODEL, K=D_MODEL, out_dtypes=[F32], name="b_dx_out")
    dya, dyb, dg12, dmix = _merge_bwd(dycat, sv["ya"], sv["yb"], sv["g12"], lp["mix_out_g"], "b_merge")
    d_w12 = _mm(sv["yg"], dg12, mode="tn", M=C_WIDTH, N=2 * C_WIDTH, K=L, tm=C_WIDTH, out_dtypes=[F32], name="b_dw_glu",
                o_stack=stk["w12"])
    dy = _mm(dg12, lp["w12"], mode="nt", M=L, N=C_WIDTH, K=2 * C_WIDTH, tk=2 * C_WIDTH, extras=[(sv["y"], 0)],
             epi=lambda acc, y_: (acc * _gelu_grad(y_),), out_dtypes=[F32], name="b_dx_glu")
    dd = _dskip_bwd(dy, z, "b_dskip")
    dS = _mm(dy, lp["ccat"], mode="nt", M=L, N=2 * N_STATE, K=C_WIDTH, tk=C_WIDTH, out_dtypes=[F32], name="b_dx_ssm_out")
    d_ccat = _mm(sv["S"], dy, mode="tn", M=2 * N_STATE, N=C_WIDTH, K=L, out_dtypes=[F32], name="b_dw_ssm_out")
    G, dar, dai = _ssm_scan(dS, lp["lam_r"], lp["lam_i"], "b_ssm_scan", reverse=True, states=sv["S"])
    d_bcat = _mm(z, G, mode="tn", M=C_WIDTH, N=2 * N_STATE, K=L, a_off=5, tm=C_WIDTH, out_dtypes=[F32], name="b_dw_ssm_in")
    dzc = _mm(G, lp["bcat"], mode="nt", M=L, N=C_WIDTH, K=2 * N_STATE, tk=2 * N_STATE,
              extras=[(dy, 0), (lp["dskip"], 0)], epi=lambda acc, dy_, dsk: (acc + dy_ * dsk,),
              out_dtypes=[_ACT], name="b_dx_ssm_in")
    dq, dkc, dkp, dvc, dvp, dsink = _attn_bwd(sv["q"], sv["k"], z, lp["sinks"], dyb, "b_attn")
    dzq, dzk, dzv, dgq, dgk = _qk_prep_bwd(z, dq, dkc, dkp, dvc, dvp, cos, sin, lp["gq"], lp["gk"], "b_qk_prep")
    dza, dws, dbs, dlg, dlb = _gmlp_bwd(z, dya, lp["ws"], lp["wsT"], lp["bfull"], lp["lgf"], lp["lbf"], "b_gmlp")
    dz = jnp.concatenate([dza, dzq, dzk, dzv, dzc], axis=1)
    d_in = _mm(sv["xn"], dz, mode="tn", M=D_MODEL, N=IN_COLS, K=L, o_cb=True, out_dtypes=[F32], name="b_dw_in",
               o_stack=stk["w_in"])
    dh, dg_attn = _mm(dz, lp["w_in"], mode="nt", M=L, N=D_MODEL, K=IN_COLS, b_cb=True,
                      extras=[(sv["h"], 0), (dh1, 0), (row("attn_norm_g"), 0)], epi=_rms_bwd_epi,
                      out_dtypes=[F32, F32], n_acc=1, name="b_dx_in")
    grads = dict(w_in=d_in, w12=d_w12, w_out=d_out, w_ff1=d_ff1, w_ff2=d_ff2, w_ple_gate=d_gate, w_ple_proj=d_proj,
                 attn_norm_g=dg_attn.reshape(D_MODEL), mlp_norm_g=dg_mlp.reshape(D_MODEL),
                 ple_norm_g=dg_ple.reshape(D_MODEL), mix_out_g=dmix.reshape(D_MODEL),
                 dws=dws, dbs=dbs, dlg=dlg, dlb=dlb, dgq=dgq, dgk=dgk, dsink=dsink,
                 dar=dar, dai=dai, d_bcat=d_bcat, d_ccat=d_ccat, dd=dd)
    return dh, grads


SMALL = ("attn_norm_g", "gmlp_ln_g", "gmlp_ln_b", "gmlp_ws", "gmlp_bs", "q_norm_g", "k_norm_g", "sinks",
         "ssm_a_re", "ssm_a_im", "ssm_log_dt", "ssm_b_re", "ssm_b_im", "ssm_c_re", "ssm_c_im", "ssm_d",
         "mix_out_g", "mlp_norm_g", "ple_norm_g")
BIG = ("w_in", "w12", "w_out", "w_ff1", "w_ff2", "w_ple_gate", "w_ple_proj")
COL_SHARDED = ("w_in", "w_ff1", "w_ple_proj")


def _block_diag(t):
    nl, g, a, b = t.shape
    eye = jnp.eye(g, dtype=t.dtype)
    return (t[:, :, :, None, :] * eye[None, :, None, :, None]).reshape(nl, g * a, g * b)


def _diag_blocks(t, a, b):
    nl = t.shape[0]
    t = t.reshape(nl, C_GROUPS, a, C_GROUPS, b)
    idx = jnp.arange(C_GROUPS)
    return jnp.moveaxis(t[:, idx, :, idx, :], 0, 1)


def _local_step(x, p, positions, target, sw, bw):
    nl = sw["attn_norm_g"].shape[0]
    G = nl * C_GROUPS
    zeros = lambda *s: jnp.zeros(s, F32)
    are = sw["ssm_a_re"].reshape(G, 1, C_STATE)
    aim = sw["ssm_a_im"].reshape(G, 1, C_STATE)
    ldt = jnp.broadcast_to(sw["ssm_log_dt"][..., None], (nl, C_GROUPS, C_STATE)).reshape(G, 1, C_STATE)
    bre = jnp.swapaxes(sw["ssm_b_re"], -1, -2).reshape(G, C_GROUP, C_STATE)
    bim = jnp.swapaxes(sw["ssm_b_im"], -1, -2).reshape(G, C_GROUP, C_STATE)
    lr, li, bbr, bbi = _ssm_prep(are, aim, ldt, bre, bim)
    unflat = lambda t: t.reshape(nl, C_GROUPS, C_GROUP, C_STATE)
    lp = dict(
        attn_norm_g=sw["attn_norm_g"], mlp_norm_g=sw["mlp_norm_g"], ple_norm_g=sw["ple_norm_g"],
        mix_out_g=sw["mix_out_g"], sinks=sw["sinks"],
        ws=sw["gmlp_ws"], wsT=jnp.swapaxes(sw["gmlp_ws"], -1, -2),
        bfull=jnp.concatenate([zeros(nl, A_HEADS, CHUNK, HEAD_DIM),
                               jnp.broadcast_to(sw["gmlp_bs"][..., None], (nl, A_HEADS, CHUNK, HEAD_DIM))], axis=-1),
        lgf=jnp.concatenate([zeros(nl, A_HEADS, HEAD_DIM), sw["gmlp_ln_g"]], axis=-1),
        lbf=jnp.concatenate([zeros(nl, A_HEADS, HEAD_DIM), sw["gmlp_ln_b"]], axis=-1),
        gq=jnp.tile(sw["q_norm_g"], (1, 2)).reshape(nl, 1, LANES),
        gk=jnp.tile(sw["k_norm_g"], (1, 2)).reshape(nl, 1, LANES),
        lam_r=lr.reshape(nl, 1, N_STATE), lam_i=li.reshape(nl, 1, N_STATE),
        bcat=jnp.concatenate([_block_diag(unflat(bbr)), _block_diag(unflat(bbi))], axis=-1),
        ccat=jnp.concatenate([_block_diag(jnp.swapaxes(sw["ssm_c_re"], -1, -2)),
                              -_block_diag(jnp.swapaxes(sw["ssm_c_im"], -1, -2))], axis=1),
        dskip=sw["ssm_d"].reshape(nl, 1, C_WIDTH))
    cos, sin = _rope_tables(positions)

    def layer_params(l, stk=None):
        lpi = {n: v[l] for n, v in lp.items()}
        lpi.update({n: (w, l) for n, w in bw.items()})
        lpi["p"] = (p, l)
        if stk is not None:
            lpi["stk"] = {n: (s, l) for n, s in stk.items()}
        return lpi

    h, saved = x, []
    xn = _rms_fwd(x, sw["attn_norm_g"][0], "f_norm_attn")
    for l in range(nl):
        g_next = sw["attn_norm_g"][l + 1] if l + 1 < nl else None
        h, xn, sv = _layer_fwd(h, xn, layer_params(l), cos, sin, g_next)
        saved.append(sv)
    sse, dh = _loss_fwd_bwd(h, target)

    stk = {n: lax.empty(w.shape, F32) for n, w in bw.items()}
    per_layer = [None] * nl
    for l in reversed(range(nl)):
        dh, gl = _layer_bwd(dh, layer_params(l, stk), saved[l], cos, sin)
        stk = {n: gl.pop(n) for n in BIG}
        per_layer[l] = gl
    grad_x = dh
    g = {n: jnp.stack([per_layer[l][n] for l in range(nl)]) for n in per_layer[0]}
    g.update(stk)

    d_bcat = g["d_bcat"]
    dbr = _diag_blocks(d_bcat[:, :, :N_STATE], C_GROUP, C_STATE).reshape(G, C_GROUP, C_STATE)
    dbi = _diag_blocks(d_bcat[:, :, N_STATE:], C_GROUP, C_STATE).reshape(G, C_GROUP, C_STATE)
    dlr = g["dar"][:, 0].reshape(G, 1, C_STATE)
    dli = g["dai"][:, 0].reshape(G, 1, C_STATE)
    g_are, g_aim, g_ldt, g_bre, g_bim = _ssm_prep_bwd(are, aim, ldt, bre, bim, dlr, dli, dbr, dbi)
    d_ccat = g["d_ccat"]
    sg = dict(
        attn_norm_g=g["attn_norm_g"], mlp_norm_g=g["mlp_norm_g"], ple_norm_g=g["ple_norm_g"], mix_out_g=g["mix_out_g"],
        gmlp_ln_g=g["dlg"][:, :, 0, HEAD_DIM:], gmlp_ln_b=g["dlb"][:, :, 0, HEAD_DIM:],
        gmlp_ws=g["dws"], gmlp_bs=g["dbs"][:, :, :, HEAD_DIM],
        q_norm_g=g["dgq"][:, 0, :HEAD_DIM], k_norm_g=g["dgk"][:, 0, :HEAD_DIM],
        sinks=g["dsink"][:, 0, :B_Q_HEADS],
        ssm_a_re=g_are.reshape(nl, C_GROUPS, C_STATE), ssm_a_im=g_aim.reshape(nl, C_GROUPS, C_STATE),
        ssm_log_dt=g_ldt[:, 0, 0].reshape(nl, C_GROUPS),
        ssm_b_re=jnp.swapaxes(g_bre.reshape(nl, C_GROUPS, C_GROUP, C_STATE), -1, -2),
        ssm_b_im=jnp.swapaxes(g_bim.reshape(nl, C_GROUPS, C_GROUP, C_STATE), -1, -2),
        ssm_c_re=jnp.swapaxes(_diag_blocks(d_ccat[:, :N_STATE], C_STATE, C_GROUP), -1, -2),
        ssm_c_im=-jnp.swapaxes(_diag_blocks(d_ccat[:, N_STATE:], C_STATE, C_GROUP), -1, -2),
        ssm_d=g["dd"].reshape(nl, C_GROUPS, C_GROUP),
    )
    bg = {n: g[n] for n in BIG}
    return sse[0, 0], grad_x, sg, bg


_ANY = pl.BlockSpec(memory_space=pl.ANY)
N_LAYERS = 4


def _mesh_pos():
    x, y, c = lax.axis_index("x"), lax.axis_index("y"), lax.axis_index("c")
    chips = [(1 - x, y), (x, 1 - y), (1 - x, 1 - y)]
    return x, y, c, 2 * x + y, chips


def _cast_into_slot(ws, j, name):
    nl, R, _ = ws[0].shape
    widths = [w.shape[2] for w in ws]
    C = sum(widths)
    tr = R if R <= 512 else _tile_rows(R, 512)
    nw = len(ws)

    def body(s_ref, *refs):
        o_ref = refs[nw]
        off = 0
        for r, wd in zip(refs[:nw], widths):
            o_ref[:, off:off + wd] = r[...].astype(o_ref.dtype)
            off += wd

    return pl.pallas_call(
        body, name=name,
        grid_spec=pltpu.PrefetchScalarGridSpec(
            num_scalar_prefetch=1, grid=(nl, R // tr),
            in_specs=[pl.BlockSpec((None, tr, wd), lambda l, i, s: (l, i, 0)) for wd in widths],
            out_specs=pl.BlockSpec((None, None, tr, C), lambda l, i, s: (l, s[0], i, 0))),
        out_shape=_sds((nl, N_CHIPS, R, C), _MXU),
    )(jnp.reshape(j, (1,)).astype(jnp.int32), *ws)


def _gather_weights(bufs):
    nk = len(bufs)

    def body(*refs):
        ins, outs = refs[:nk], refs[nk:2 * nk]
        send_sems, recv_sems = refs[2 * nk:]
        x, y, c, j, chips = _mesh_pos()
        mine, other = pl.ds(2 * c, 2), pl.ds(2 * (1 - c), 2)

        def ici(t, q):
            cx, cy = chips[q]
            return pltpu.make_async_remote_copy(
                src_ref=ins[t].at[mine, j], dst_ref=outs[t].at[mine, j],
                send_sem=send_sems.at[6 * t + q], recv_sem=recv_sems.at[6 * t + q],
                device_id=(cx, cy, c), device_id_type=MESH)

        def landed(t, q):
            cx, cy = chips[q]
            blk = outs[t].at[mine, 2 * cx + cy]
            return pltpu.make_async_remote_copy(
                src_ref=blk, dst_ref=blk, send_sem=send_sems.at[6 * t + q], recv_sem=recv_sems.at[6 * t + q],
                device_id=(cx, cy, c), device_id_type=MESH)

        def fwd(t, q, rows):
            cx, cy = chips[q]
            blk = outs[t].at[rows, 2 * cx + cy]
            return pltpu.make_async_remote_copy(
                src_ref=blk, dst_ref=blk, send_sem=send_sems.at[6 * t + 3 + q], recv_sem=recv_sems.at[6 * t + 3 + q],
                device_id=(x, y, 1 - c), device_id_type=MESH)

        for t in range(nk):
            for q in range(3):
                ici(t, q).start()
        for t in range(nk):
            for q in range(3):
                landed(t, q).wait_recv()
                fwd(t, q, mine).start()
        for t in range(nk):
            for q in range(3):
                fwd(t, q, other).wait_recv()
        for t in range(nk):
            for q in range(3):
                ici(t, q).wait_send()
                fwd(t, q, mine).wait_send()

    return pl.pallas_call(
        body, name="gather_weights", in_specs=[_ANY] * nk, out_specs=[_ANY] * nk,
        out_shape=[_sds(b.shape, b.dtype) for b in bufs],
        input_output_aliases={t: t for t in range(nk)},
        scratch_shapes=[pltpu.SemaphoreType.DMA((6 * nk,)), pltpu.SemaphoreType.DMA((6 * nk,))],
    )(*bufs)


def _exchange_sibling_half(gl):
    nk = len(gl)

    def body(*refs):
        ins, outs = refs[:nk], refs[nk:2 * nk]
        send_sems, recv_sems = refs[2 * nk:]
        x, y, c, _, _ = _mesh_pos()
        cps = [pltpu.make_async_remote_copy(
            src_ref=ins[t].at[pl.ds(2 * (1 - c), 2)], dst_ref=outs[t],
            send_sem=send_sems.at[t], recv_sem=recv_sems.at[t],
            device_id=(x, y, 1 - c), device_id_type=MESH) for t in range(nk)]
        for cp in cps:
            cp.start()
        for cp in cps:
            cp.wait()

    return pl.pallas_call(
        body, name="reduce_sibling", in_specs=[_ANY] * nk, out_specs=[_ANY] * nk,
        out_shape=[_sds((2,) + g.shape[1:], g.dtype) for g in gl],
        scratch_shapes=[pltpu.SemaphoreType.DMA((nk,)), pltpu.SemaphoreType.DMA((nk,))],
    )(*gl)


def _exchange_chips(ps):
    nk = len(ps)

    def body(*refs):
        ins, outs = refs[:nk], refs[nk:2 * nk]
        send_sems, recv_sems = refs[2 * nk:]
        x, y, c, j, chips = _mesh_pos()

        def send(t, q):
            cx, cy = chips[q]
            return pltpu.make_async_remote_copy(
                src_ref=ins[t].at[:, 2 * cx + cy], dst_ref=outs[t].at[j],
                send_sem=send_sems.at[3 * t + q], recv_sem=recv_sems.at[3 * t + q],
                device_id=(cx, cy, c), device_id_type=MESH)

        def landed(t, q):
            cx, cy = chips[q]
            blk = outs[t].at[2 * cx + cy]
            return pltpu.make_async_remote_copy(
                src_ref=blk, dst_ref=blk, send_sem=send_sems.at[3 * t + q], recv_sem=recv_sems.at[3 * t + q],
                device_id=(cx, cy, c), device_id_type=MESH)

        for t in range(nk):
            for q in range(3):
                send(t, q).start()
        for t in range(nk):
            for q in range(3):
                landed(t, q).wait_recv()
        for t in range(nk):
            for q in range(3):
                send(t, q).wait_send()

    return pl.pallas_call(
        body, name="reduce_chips", in_specs=[_ANY] * nk, out_specs=[_ANY] * nk,
        out_shape=[_sds((N_CHIPS, 2) + p.shape[2:], p.dtype) for p in ps],
        scratch_shapes=[pltpu.SemaphoreType.DMA((3 * nk,)), pltpu.SemaphoreType.DMA((3 * nk,))],
    )(*ps)


def _share_sibling(fs):
    nk = len(fs)

    def body(*refs):
        ins, outs = refs[:nk], refs[nk:2 * nk]
        send_sems, recv_sems = refs[2 * nk:]
        x, y, c, _, _ = _mesh_pos()
        mine = pl.ds(2 * c, 2)
        cps = [pltpu.make_async_remote_copy(
            src_ref=ins[t].at[mine], dst_ref=outs[t].at[mine], send_sem=send_sems.at[t], recv_sem=recv_sems.at[t],
            device_id=(x, y, 1 - c), device_id_type=MESH) for t in range(nk)]
        for cp in cps:
            cp.start()
        for cp in cps:
            cp.wait_send()
        for t in range(nk):
            blk = outs[t].at[pl.ds(2 * (1 - c), 2)]
            pltpu.make_async_remote_copy(
                src_ref=blk, dst_ref=blk, send_sem=send_sems.at[t], recv_sem=recv_sems.at[t],
                device_id=(x, y, 1 - c), device_id_type=MESH).wait_recv()

    return pl.pallas_call(
        body, name="share_sibling", in_specs=[_ANY] * nk, out_specs=[_ANY] * nk,
        out_shape=[_sds(f.shape, f.dtype) for f in fs],
        input_output_aliases={t: t for t in range(nk)},
        scratch_shapes=[pltpu.SemaphoreType.DMA((nk,)), pltpu.SemaphoreType.DMA((nk,))],
    )(*fs)


def _add_own_half(gl, r1, c, name):
    _, ns, R, C = gl.shape
    rows = 2 * ns * R
    tr = _tile_rows(rows, 512)
    nblk = rows // tr

    def body(s_ref, a_ref, b_ref, o_ref):
        o_ref[...] = (a_ref[...] + b_ref[...]).astype(o_ref.dtype)

    out = pl.pallas_call(
        body, name=name,
        grid_spec=pltpu.PrefetchScalarGridSpec(
            num_scalar_prefetch=1, grid=(nblk,),
            in_specs=[pl.BlockSpec((tr, C), lambda i, s: (s[0] * nblk + i, 0)), pl.BlockSpec((tr, C), lambda i, s: (i, 0))],
            out_specs=pl.BlockSpec((tr, C), lambda i, s: (i, 0))),
        out_shape=_sds((rows, C), _WIRE),
        compiler_params=_cparams(3 * _nbytes((tr, C), F32)),
    )(jnp.reshape(c, (1,)).astype(jnp.int32), gl.reshape(2 * rows, C), r1.reshape(rows, C))
    return out.reshape(2, ns, R, C)


def _add_chips(p, r2, j, c, name):
    _, ns, R, C = p.shape
    tr = R if R <= 512 else _tile_rows(R, 512)

    def body(s_ref, own, a1, a2, a3, o_ref):
        f = lambda r: r[...].astype(F32)
        o_ref[...] = ((f(own) + f(a1)) + f(a2)) + f(a3)

    blk = (None, None, tr, C)
    return pl.pallas_call(
        body, name=name,
        grid_spec=pltpu.PrefetchScalarGridSpec(
            num_scalar_prefetch=1, grid=(2, R // tr),
            in_specs=[pl.BlockSpec(blk, lambda h, i, s: (h, s[0], i, 0))]
            + [pl.BlockSpec(blk, lambda h, i, s, k=k: ((s[0] + k) % N_CHIPS, h, i, 0)) for k in (1, 2, 3)],
            out_specs=pl.BlockSpec((None, tr, C), lambda h, i, s: (2 * s[1] + h, i, 0))),
        out_shape=_sds((N_LAYERS, R, C), F32),
        compiler_params=_cparams(6 * _nbytes((tr, C), F32)),
    )(jnp.stack([j, c]).astype(jnp.int32), p, r2, r2, r2)


def _allreduce_small(buf):
    Rs = buf.shape[0]

    def body(b_ref, o_ref, t_ref, slots_ref, send_sems, recv_sems):
        x, y, c, j, chips = _mesh_pos()
        sib = pltpu.make_async_remote_copy(
            src_ref=b_ref, dst_ref=t_ref, send_sem=send_sems.at[0], recv_sem=recv_sems.at[0],
            device_id=(x, y, 1 - c), device_id_type=MESH)
        sib.start()
        sib.wait()
        slots_ref[j] = b_ref[...] + t_ref[...]

        def send(q):
            cx, cy = chips[q]
            return pltpu.make_async_remote_copy(
                src_ref=slots_ref.at[j], dst_ref=slots_ref.at[j], send_sem=send_sems.at[1 + q],
                recv_sem=recv_sems.at[1 + q], device_id=(cx, cy, c), device_id_type=MESH)

        def landed(q):
            cx, cy = chips[q]
            blk = slots_ref.at[2 * cx + cy]
            return pltpu.make_async_remote_copy(
                src_ref=blk, dst_ref=blk, send_sem=send_sems.at[1 + q], recv_sem=recv_sems.at[1 + q],
                device_id=(cx, cy, c), device_id_type=MESH)

        for q in range(3):
            send(q).start()
        for q in range(3):
            landed(q).wait_recv()
        for q in range(3):
            send(q).wait_send()
        o_ref[...] = ((slots_ref[0] + slots_ref[1]) + slots_ref[2]) + slots_ref[3]

    vm = pl.BlockSpec(memory_space=pltpu.VMEM)
    return pl.pallas_call(
        body, name="allreduce_small", in_specs=[vm], out_specs=vm,
        out_shape=_sds((Rs, LANES), F32),
        scratch_shapes=[pltpu.VMEM((Rs, LANES), F32), pltpu.VMEM((N_CHIPS, Rs, LANES), F32),
                        pltpu.SemaphoreType.DMA((4,)), pltpu.SemaphoreType.DMA((4,))],
        compiler_params=_cparams(4 * _nbytes((Rs, LANES), F32)),
    )(buf)


def _rows_of(shape):
    return -(-int(np.prod(shape)) // (SUBLANES * LANES)) * SUBLANES


def _pack(d):
    parts = []
    for n in SMALL:
        flat = d[n].reshape(-1)
        parts.append(jnp.pad(flat, (0, _rows_of(flat.shape) * LANES - flat.shape[0])).reshape(-1, LANES))
    return jnp.concatenate(parts, axis=0)


def _unpack(buf, like):
    out, r0 = {}, 0
    for n in SMALL:
        shape = like[n].shape
        size, nr = int(np.prod(shape)), _rows_of(shape)
        piece = lax.optimization_barrier(buf[r0:r0 + nr])
        out[n] = piece.reshape(-1)[:size].reshape(shape)
        r0 += nr
    return out


ARGS = ("x", "p", "positions", "attn_norm_g", "w_in", "gmlp_ln_g", "gmlp_ln_b", "gmlp_ws", "gmlp_bs", "q_norm_g",
        "k_norm_g", "sinks", "ssm_a_re", "ssm_a_im", "ssm_log_dt", "ssm_b_re", "ssm_b_im", "ssm_c_re", "ssm_c_im",
        "ssm_d", "glu_w1", "glu_w2", "mix_out_g", "w_out", "mlp_norm_g", "w_ff1", "w_ff2", "ple_norm_g", "w_ple_gate",
        "w_ple_proj")
WEIGHTS = ARGS[3:]


def kernel(x, p, positions, attn_norm_g, w_in, gmlp_ln_g, gmlp_ln_b, gmlp_ws, gmlp_bs, q_norm_g, k_norm_g, sinks, ssm_a_re, ssm_a_im, ssm_log_dt, ssm_b_re, ssm_b_im, ssm_c_re, ssm_c_im, ssm_d, glu_w1, glu_w2, mix_out_g, w_out, mlp_norm_g, w_ff1, w_ff2, ple_norm_g, w_ple_gate, w_ple_proj, loss_target, m_attn_norm_g, m_w_in, m_gmlp_ln_g, m_gmlp_ln_b, m_gmlp_ws, m_gmlp_bs, m_q_norm_g, m_k_norm_g, m_sinks, m_ssm_a_re, m_ssm_a_im, m_ssm_log_dt, m_ssm_b_re, m_ssm_b_im, m_ssm_c_re, m_ssm_c_im, m_ssm_d, m_glu_w1, m_glu_w2, m_mix_out_g, m_w_out, m_mlp_norm_g, m_w_ff1, m_w_ff2, m_ple_norm_g, m_w_ple_gate, m_w_ple_proj, v_attn_norm_g, v_w_in, v_gmlp_ln_g, v_gmlp_ln_b, v_gmlp_ws, v_gmlp_bs, v_q_norm_g, v_k_norm_g, v_sinks, v_ssm_a_re, v_ssm_a_im, v_ssm_log_dt, v_ssm_b_re, v_ssm_b_im, v_ssm_c_re, v_ssm_c_im, v_ssm_d, v_glu_w1, v_glu_w2, v_mix_out_g, v_w_out, v_mlp_norm_g, v_w_ff1, v_w_ff2, v_ple_norm_g, v_w_ple_gate, v_w_ple_proj):
    a = dict(locals())
    L = a["x"].shape[1]
    nl = N_LAYERS
    c = lax.axis_index("c")
    j = 2 * lax.axis_index("x") + lax.axis_index("y")

    shards = dict(w_in=[a["w_in"]], w12=[a["glu_w1"], a["glu_w2"]], w_out=[a["w_out"]], w_ff1=[a["w_ff1"]],
                  w_ff2=[a["w_ff2"]], w_ple_gate=[a["w_ple_gate"]], w_ple_proj=[a["w_ple_proj"]])
    gathered = dict(zip(BIG, _gather_weights([_cast_into_slot(shards[n], j, "cast_" + n) for n in BIG])))
    bw = {n: (g if n in COL_SHARDED else g.reshape(nl, N_CHIPS * g.shape[2], g.shape[3])) for n, g in gathered.items()}

    sw = {n: a[n] for n in SMALL}
    sse, gx, sg, bg = _local_step(a["x"].reshape(L, D_MODEL), a["p"].reshape(nl, L, PLE_DIM),
                                  a["positions"].reshape(L), a["loss_target"].reshape(L, D_MODEL), sw, bw)
    loss = lax.psum(sse * (0.5 / D_MODEL), ("x", "y", "c"))

    gl = [bg[n] if n in COL_SHARDED else bg[n].reshape(nl, N_CHIPS, bg[n].shape[1] // N_CHIPS, bg[n].shape[2])
          for n in BIG]
    r1 = _exchange_sibling_half(gl)
    ps = [_add_own_half(g, r, c, "reduce_add_sibling_" + n) for g, r, n in zip(gl, r1, BIG)]
    r2 = _exchange_chips(ps)
    fs = [_add_chips(p_, r, j, c, "reduce_add_chips_" + n) for p_, r, n in zip(ps, r2, BIG)]
    big_grads = dict(zip(BIG, _share_sibling(fs)))
    g12 = big_grads.pop("w12")
    big_grads["glu_w1"], big_grads["glu_w2"] = g12[:, :, :C_WIDTH], g12[:, :, C_WIDTH:]

    small_grads = _unpack(_allreduce_small(_pack(sg)), sw)

    grads, delta, new_m, new_v = {}, {}, {}, {}
    d_s, m_s, v_s = _adamw(_pack(sw), _pack(small_grads), _pack({n: a["m_" + n] for n in SMALL}),
                           _pack({n: a["v_" + n] for n in SMALL}), "adamw_small")
    grads.update(small_grads)
    delta.update(_unpack(d_s, sw))
    new_m.update(_unpack(m_s, sw))
    new_v.update(_unpack(v_s, sw))
    for n, g in big_grads.items():
        shp = a[n].shape
        two_d = lambda t: t.reshape(shp[0] * shp[1], shp[2])
        d, m, v = _adamw(two_d(a[n]), two_d(g), two_d(a["m_" + n]), two_d(a["v_" + n]), "adamw_" + n)
        grads[n], delta[n], new_m[n], new_v[n] = g, d.reshape(shp), m.reshape(shp), v.reshape(shp)

    return (loss, gx.reshape(1, L, D_MODEL), *[grads[n] for n in WEIGHTS], *[delta[n] for n in WEIGHTS],
            *[new_m[n] for n in WEIGHTS], *[new_v[n] for n in WEIGHTS])
```

```python
import functools
import math

import numpy as np
import jax
import jax.numpy as jnp
from jax import lax
from jax.experimental import pallas as pl
from jax.experimental.pallas import tpu as pltpu

F32 = jnp.float32
_MXU = jnp.bfloat16
_ACT = jnp.bfloat16
_WIRE = jnp.bfloat16

D_MODEL = 1024
HEAD_DIM = 64
A_HEADS = 4
CHUNK = 128
B_Q_HEADS = 8
B_KV_HEADS = 2
B_GROUP = 4
WINDOW = 128
ROPE_THETA = 10000.0
C_WIDTH = 256
C_GROUP = 16
C_GROUPS = 16
C_STATE = 64
N_STATE = C_GROUPS * C_STATE
IN_A, IN_Q, IN_KV, IN_C = 512, 512, 128, 256
IN_COLS = 1536
D_FF = 4096
PLE_DIM = 256
EPS = 1e-6
NEG = -1e30
ADAM_LR, ADAM_B1, ADAM_B2, ADAM_EPS, ADAM_WD, ADAM_STEP = 0.001, 0.9, 0.999, 1e-08, 0.01, 10

LANES = 128
SUBLANES = 8
VMEM_BYTES = 64 * 2 ** 20
N_CHIPS = 4
MESH = pl.DeviceIdType.MESH


_MM_VMEM_BUDGET = 44 * 2 ** 20


def _vmem_limit(est_bytes):
    return int(min(max(2 * est_bytes + (8 << 20), 32 << 20), VMEM_BYTES - (6 << 20)))


def _cparams(est_bytes, **kw):
    return pltpu.CompilerParams(vmem_limit_bytes=_vmem_limit(est_bytes), **kw)


def _sds(shape, dtype):
    return pltpu.HBM(tuple(shape), dtype)


def _hbm(x):
    return pltpu.with_memory_space_constraint(x, pltpu.HBM) if x.size >= (1 << 20) else x


def _nbytes(shape, dtype):
    return int(np.prod(shape)) * jnp.dtype(dtype).itemsize


def _tile(dim, pref):
    t = min(dim, pref)
    while dim % t:
        t -= LANES
    assert t > 0, (dim, pref)
    return t


def _lane(shape):
    return lax.broadcasted_iota(jnp.int32, shape, len(shape) - 1)


def _row(shape):
    return lax.broadcasted_iota(jnp.int32, shape, len(shape) - 2)


def _gelu(x):
    c = math.sqrt(2.0 / math.pi)
    return 0.5 * x * (1.0 + jnp.tanh(c * (x + 0.044715 * (x * x * x))))


def _gelu_grad(x):
    c = math.sqrt(2.0 / math.pi)
    t = jnp.tanh(c * (x + 0.044715 * (x * x * x)))
    return 0.5 * (1.0 + t) + 0.5 * x * (1.0 - t * t) * (c * (1.0 + 3.0 * 0.044715 * (x * x)))


def _sigmoid(x):
    return 1.0 / (1.0 + jnp.exp(-x))


def _dot(a, b, dims=(((1,), (0,)), ((), ()))):
    return lax.dot_general(a.astype(_MXU), b.astype(_MXU), dims, preferred_element_type=F32)


_NT = (((1,), (1,)), ((), ()))
_TN = (((0,), (0,)), ((), ()))
_NN = (((1,), (0,)), ((), ()))


def _mm(a, b, *, mode, M, N, K, out_dtypes, name, epi=None, extras=(), b_cb=False, o_cb=False,
        a_off=0, b_off=0, tm=1024, tn=1024, tk=1024, a_lyr=None, b_lyr=None, o_stack=None, n_acc=0, comm=None):
    if isinstance(a, tuple):
        a, a_lyr = a
    if isinstance(b, tuple):
        b, b_lyr = b
    if b_cb or o_cb:
        nc = (b.shape[-1] if b_cb else N // N_CHIPS)
    tn_nom = nc if ((mode == "nn" and b_cb) or (mode == "tn" and o_cb)) else _tile(N, tn)
    tk_nom = nc if (mode == "nt" and b_cb) else _tile(K, tk)
    item = lambda d: jnp.dtype(d).itemsize
    per_row = tk_nom * item(a.dtype) + tn_nom * (sum(item(d) for d in out_dtypes)
                                                   + sum(item(e.dtype) for e, _ in extras if e.shape[0] > 1))
    fixed = tk_nom * tn_nom * item(b.dtype)
    tm = _tile(M, tm)
    while tm > 256 and M % (tm // 2) == 0 and 2 * (tm * per_row + fixed) + 8 * tm * tn_nom > _MM_VMEM_BUDGET:
        tm //= 2

    def spec(block, imap, lyr=None):
        if lyr is None:
            return pl.BlockSpec(block, imap)
        return pl.BlockSpec((None,) + block, lambda i, j, k: (lyr,) + imap(i, j, k))

    if mode == "nn":
        if b_cb:
            tn = nc
        tm, tn, tk = _tile(M, tm), _tile(N, tn), _tile(K, tk)
        a_spec = spec((tm, tk), lambda i, j, k: (i, k + a_off), a_lyr)
        if b_cb:
            b_spec = spec((None, tk, tn), lambda i, j, k: (j, k, 0), b_lyr)
        else:
            b_spec = spec((tk, tn), lambda i, j, k: (k, j + b_off), b_lyr)
        dims = _NN
        a_blk, b_blk = (tm, tk), (tk, tn)
    elif mode == "nt":
        if b_cb:
            tk = nc
        tm, tn, tk = _tile(M, tm), _tile(N, tn), _tile(K, tk)
        a_spec = spec((tm, tk), lambda i, j, k: (i, k + a_off), a_lyr)
        if b_cb:
            b_spec = spec((None, tn, tk), lambda i, j, k: (k, j, 0), b_lyr)
        else:
            b_spec = spec((tn, tk), lambda i, j, k: (j, k + b_off), b_lyr)
        dims = _NT
        a_blk, b_blk = (tm, tk), (tn, tk)
    else:
        if o_cb:
            tn = nc
        tm, tn, tk = _tile(M, tm), _tile(N, tn), _tile(K, tk)
        a_spec = spec((tk, tm), lambda i, j, k: (k, i + a_off), a_lyr)
        b_spec = spec((tk, tn), lambda i, j, k: (k, j + b_off), b_lyr)
        dims = _TN
        a_blk, b_blk = (tk, tm), (tk, tn)
    gi, gj, gk = M // tm, N // tn, K // tk
    o_lyr = None if o_stack is None else o_stack[1]
    if o_cb:
        o_spec = spec((None, tm, tn), lambda i, j, k: (j, i, 0), o_lyr)
        o_shape = (gj, M, tn)
    else:
        o_spec = spec((tm, tn), lambda i, j, k: (i, j), o_lyr)
        o_shape = (M, N)
    e_specs = []
    for e, off in extras:
        if e.shape[0] == 1:
            e_specs.append(pl.BlockSpec((1, tn), lambda i, j, k, off=off: (0, j + off)))
        else:
            e_specs.append(pl.BlockSpec((tm, tn), lambda i, j, k, off=off: (i, j + off)))
    extras = [e for e, _ in extras]
    ne, no = len(extras), len(out_dtypes)
    operands = [_hbm(t) for t in (a, b, *extras)]
    in_specs = [a_spec, b_spec] + e_specs
    out_shape = [_sds(o_shape, d) for d in out_dtypes]
    aliases = {}
    if o_stack is not None:
        assert no == 1 and o_stack[0].shape[1:] == o_shape and o_stack[0].dtype == out_dtypes[0]
        operands.append(_hbm(o_stack[0]))
        in_specs.append(pl.BlockSpec(memory_space=pl.ANY))
        out_shape = [_sds(o_stack[0].shape, o_stack[0].dtype)]
        aliases = {len(operands) - 1: 0}
    out_specs = [o_spec] * no
    if n_acc:
        assert gj == 1 and o_stack is None
        out_specs[no - n_acc:] = [pl.BlockSpec((1, tn), lambda i, j, k: (0, 0))] * n_acc
        out_shape[no - n_acc:] = [_sds((1, N), d) for d in out_dtypes[no - n_acc:]]
    nx_in = nx_out = 0
    if comm is not None:
        nx_in, ncin0 = len(comm.ins), len(operands)
        operands += list(comm.ins)
        in_specs += [pl.BlockSpec(memory_space=pl.ANY)] * nx_in
        for t, x_ in enumerate(comm.ins):
            if comm.aliased[t]:
                aliases[ncin0 + t] = len(out_shape)
                out_shape.append(_sds(x_.shape, x_.dtype))
        out_shape += [_sds(sh, dt) for sh, dt in comm.fresh]
        nx_out = len(out_shape) - no
        out_specs += [pl.BlockSpec(memory_space=pl.ANY)] * nx_out
    nin = len(operands)

    def body(*refs):
        a_ref, b_ref = refs[0], refs[1]
        e_refs = refs[2:2 + ne]
        o_refs = refs[nin:nin + no]
        first_rows = pl.program_id(0) == 0
        if comm is not None:
            x_ins = refs[nin - nx_in:nin]
            x_outs = refs[nin + no:nin + no + nx_out]
            sems = refs[nin + no + nx_out:nin + no + nx_out + 2]
            pid = [pl.program_id(d) for d in range(3)]

            @pl.when((pid[0] == 0) & (pid[1] == 0) & (pid[2] == 0))
            def _():
                comm.start(x_ins, x_outs, *sems)

        def fin(acc):
            vals = epi(acc, *[e[...] for e in e_refs]) if epi is not None else (acc,)
            for t, (o, v) in enumerate(zip(o_refs, vals)):
                if t < no - n_acc:
                    o[...] = v.astype(o.dtype)
                else:
                    @pl.when(first_rows)
                    def _():
                        o[...] = jnp.zeros_like(o)

                    o[...] += v.astype(o.dtype)

        prod = _dot(a_ref[...], b_ref[...], dims)
        if gk == 1:
            fin(prod)
        else:
            acc_ref = refs[-1]
            k = pl.program_id(2)

            @pl.when(k == 0)
            def _():
                acc_ref[...] = prod

            @pl.when(k > 0)
            def _():
                acc_ref[...] += prod

            @pl.when(k == gk - 1)
            def _():
                fin(acc_ref[...])

        if comm is not None:
            @pl.when((pid[0] == gi - 1) & (pid[1] == gj - 1) & (pid[2] == gk - 1))
            def _():
                comm.wait(x_ins, x_outs, *sems)

    est = (_nbytes(a_blk, a.dtype) + _nbytes(b_blk, b.dtype)
           + sum(_nbytes((tm, tn), d) for d in out_dtypes)
           + sum(_nbytes((tm, tn), e.dtype) for e in extras)) + 2 * _nbytes((tm, tn), F32)
    sem_scratch = [pltpu.SemaphoreType.DMA((comm.n_sems,))] * 2 if comm is not None else []
    row_sem = "arbitrary" if (n_acc or comm is not None) else "parallel"
    outs = pl.pallas_call(
        body, name=name, grid=(gi, gj, gk),
        in_specs=in_specs,
        out_specs=out_specs,
        out_shape=out_shape,
        scratch_shapes=sem_scratch + ([pltpu.VMEM((tm, tn), F32)] if gk > 1 else []),
        input_output_aliases=aliases,
        compiler_params=_cparams(est, dimension_semantics=(row_sem, "arbitrary" if comm is not None else "parallel",
                                                           "arbitrary")),
    )(*operands)
    if comm is not None:
        main = outs[:no]
        return (main if no > 1 else main[0]), list(outs[no:])
    return outs if no > 1 else outs[0]


_TL = 512


def _rms_fwd(h, g, name):
    L, D = h.shape
    tl = _tile(L, _TL)

    def body(h_ref, g_ref, o_ref):
        x = h_ref[...]
        r = lax.rsqrt(jnp.mean(x * x, axis=-1, keepdims=True) + EPS)
        o_ref[...] = ((x * r) * g_ref[...]).astype(o_ref.dtype)

    return pl.pallas_call(
        body, name=name, grid=(L // tl,),
        in_specs=[pl.BlockSpec((tl, D), lambda i: (i, 0)), pl.BlockSpec((1, D), lambda i: (0, 0))],
        out_specs=pl.BlockSpec((tl, D), lambda i: (i, 0)),
        out_shape=_sds((L, D), _ACT),
        compiler_params=_cparams(3 * _nbytes((tl, D), F32)),
    )(h, g.reshape(1, D))


def _rms_bwd(dxn, h, g, dres, name):
    L, D = h.shape
    tl = _tile(L, _TL)

    def body(d_ref, h_ref, g_ref, r_ref, o_ref, dg_ref):
        x = h_ref[...]
        r = lax.rsqrt(jnp.mean(x * x, axis=-1, keepdims=True) + EPS)
        xhat = x * r
        d = d_ref[...].astype(F32)
        gy = d * g_ref[...]
        dx = r * (gy - xhat * jnp.mean(gy * xhat, axis=-1, keepdims=True))
        o_ref[...] = r_ref[...] + dx

        @pl.when(pl.program_id(0) == 0)
        def _():
            dg_ref[...] = jnp.zeros_like(dg_ref)

        dg_ref[...] += jnp.sum(d * xhat, axis=0, keepdims=True)

    dh, dg = pl.pallas_call(
        body, name=name, grid=(L // tl,),
        in_specs=[pl.BlockSpec((tl, D), lambda i: (i, 0)), pl.BlockSpec((tl, D), lambda i: (i, 0)),
                  pl.BlockSpec((1, D), lambda i: (0, 0)), pl.BlockSpec((tl, D), lambda i: (i, 0))],
        out_specs=[pl.BlockSpec((tl, D), lambda i: (i, 0)), pl.BlockSpec((1, D), lambda i: (0, 0))],
        out_shape=[_sds((L, D), F32), _sds((1, D), F32)],
        compiler_params=_cparams(5 * _nbytes((tl, D), F32)),
    )(dxn, h, g.reshape(1, D), dres)
    return dh, dg.reshape(D)


def _rope_tables(positions):
    L = positions.shape[0]
    tl = _tile(L, 1024)
    inv = 1.0 / (ROPE_THETA ** (np.arange(0, HEAD_DIM, 2, dtype=np.float32) / HEAD_DIM))
    inv128 = jnp.asarray(np.tile(inv.astype(np.float32), 4).reshape(1, LANES))

    def body(p_ref, i_ref, c_ref, s_ref):
        ang = p_ref[...].astype(F32) * i_ref[...]
        c_ref[...] = jnp.cos(ang)
        s_ref[...] = jnp.sin(ang)

    return pl.pallas_call(
        body, name="rope_tables", grid=(L // tl,),
        in_specs=[pl.BlockSpec((tl, 1), lambda i: (i, 0)), pl.BlockSpec((1, LANES), lambda i: (0, 0))],
        out_specs=[pl.BlockSpec((tl, LANES), lambda i: (i, 0))] * 2,
        out_shape=[_sds((L, LANES), F32)] * 2,
    )(positions.reshape(L, 1), inv128)


_GM_TL = 256


def _gmlp_head(Z, W, bfull, lg, lb, maskv):
    G = _gelu(Z)
    mu = jnp.sum(jnp.where(maskv, G, 0.0), axis=-1, keepdims=True) * (1.0 / HEAD_DIM)
    xc = jnp.where(maskv, G - mu, 0.0)
    var = jnp.sum(xc * xc, axis=-1, keepdims=True) * (1.0 / HEAD_DIM)
    rstd = lax.rsqrt(var + EPS)
    xhat = xc * rstd
    vn = xhat * lg + lb
    sv = _dot(W, vn) + bfull
    return G, xhat, rstd, vn, sv


def _tril(W):
    return jnp.where(_row(W.shape) >= _lane(W.shape), W, 0.0)


def _triu(W):
    return jnp.where(_row(W.shape) <= _lane(W.shape), W, 0.0)


def _gmlp_fwd(z, ws, bfull, lgf, lbf, name):
    L = z.shape[0]
    tl = _tile(L, _GM_TL)
    nch = tl // CHUNK

    def body(z_ref, w_ref, b_ref, lg_ref, lb_ref, o_ref):
        maskv = _lane((CHUNK, LANES)) >= HEAD_DIM
        for c in range(nch):
            rows = slice(c * CHUNK, (c + 1) * CHUNK)
            for hp in range(A_HEADS // 2):
                acc = None
                for hh in range(2):
                    h = 2 * hp + hh
                    Z = z_ref[rows, h * LANES:(h + 1) * LANES]
                    G, _, _, _, sv = _gmlp_head(Z, _tril(w_ref[h]), b_ref[h], lg_ref[h:h + 1, :], lb_ref[h:h + 1, :], maskv)
                    prod = G * pltpu.roll(sv, HEAD_DIM, axis=1)
                    acc = prod if hh == 0 else acc + pltpu.roll(prod, HEAD_DIM, axis=1)
                o_ref[rows, hp * LANES:(hp + 1) * LANES] = acc

    return pl.pallas_call(
        body, name=name, grid=(L // tl,),
        in_specs=[pl.BlockSpec((tl, IN_A), lambda i: (i, 0)),
                  pl.BlockSpec((A_HEADS, CHUNK, CHUNK), lambda i: (0, 0, 0)),
                  pl.BlockSpec((A_HEADS, CHUNK, LANES), lambda i: (0, 0, 0)),
                  pl.BlockSpec((A_HEADS, LANES), lambda i: (0, 0)),
                  pl.BlockSpec((A_HEADS, LANES), lambda i: (0, 0))],
        out_specs=pl.BlockSpec((tl, 2 * LANES), lambda i: (i, 0)),
        out_shape=_sds((L, 2 * LANES), F32),
    )(z, ws, bfull, lgf, lbf)


def _gmlp_bwd(z, dya, ws, wsT, bfull, lgf, lbf, name):
    L = z.shape[0]
    tl = _tile(L, _GM_TL)
    nch = tl // CHUNK
    nsteps = L // tl

    def body(z_ref, d_ref, w_ref, wt_ref, b_ref, lg_ref, lb_ref, dz_ref, dw_ref, db_ref, dlg_ref, dlb_ref):
        step = pl.program_id(0)

        @pl.when(step == 0)
        def _():
            dw_ref[...] = jnp.zeros_like(dw_ref)
            db_ref[...] = jnp.zeros_like(db_ref)
            dlg_ref[...] = jnp.zeros_like(dlg_ref)
            dlb_ref[...] = jnp.zeros_like(dlb_ref)

        lane = _lane((CHUNK, LANES))
        maskv = lane >= HEAD_DIM
        for c in range(nch):
            rows = slice(c * CHUNK, (c + 1) * CHUNK)
            for h in range(A_HEADS):
                hp, hh = divmod(h, 2)
                Z = z_ref[rows, h * LANES:(h + 1) * LANES]
                lg = lg_ref[h:h + 1, :]
                G, xhat, rstd, vn, sv = _gmlp_head(Z, _tril(w_ref[h]), b_ref[h], lg, lb_ref[h:h + 1, :], maskv)
                dpair = d_ref[rows, hp * LANES:(hp + 1) * LANES]
                if hh == 1:
                    dpair = pltpu.roll(dpair, HEAD_DIM, axis=1)
                dout = jnp.where(maskv, 0.0, dpair)
                du = dout * pltpu.roll(sv, HEAD_DIM, axis=1)
                dsv = pltpu.roll(dout * G, HEAD_DIM, axis=1)
                dw_ref[h] += _tril(_dot(dsv, vn, _NT))
                db_ref[h] += dsv
                dvn = _dot(_triu(wt_ref[h]), dsv)
                dlg_ref[h] += dvn * xhat
                dlb_ref[h] += dvn
                dxh = dvn * lg
                m1 = jnp.sum(dxh, axis=-1, keepdims=True) * (1.0 / HEAD_DIM)
                m2 = jnp.sum(dxh * xhat, axis=-1, keepdims=True) * (1.0 / HEAD_DIM)
                dv = jnp.where(maskv, rstd * (dxh - m1 - xhat * m2), 0.0)
                dz_ref[rows, h * LANES:(h + 1) * LANES] = ((du + dv) * _gelu_grad(Z)).astype(dz_ref.dtype)

        @pl.when(step == nsteps - 1)
        def _():
            for h in range(A_HEADS):
                db_ref[h] = jnp.broadcast_to(jnp.sum(db_ref[h], axis=1, keepdims=True), (CHUNK, LANES))
                dlg_ref[h] = jnp.broadcast_to(jnp.sum(dlg_ref[h], axis=0, keepdims=True), (CHUNK, LANES))
                dlb_ref[h] = jnp.broadcast_to(jnp.sum(dlb_ref[h], axis=0, keepdims=True), (CHUNK, LANES))

    full3 = pl.BlockSpec((A_HEADS, CHUNK, LANES), lambda i: (0, 0, 0))
    return pl.pallas_call(
        body, name=name, grid=(nsteps,),
        in_specs=[pl.BlockSpec((tl, IN_A), lambda i: (i, 0)),
                  pl.BlockSpec((tl, 2 * LANES), lambda i: (i, 0)),
                  full3, full3, full3,
                  pl.BlockSpec((A_HEADS, LANES), lambda i: (0, 0)),
                  pl.BlockSpec((A_HEADS, LANES), lambda i: (0, 0))],
        out_specs=[pl.BlockSpec((tl, IN_A), lambda i: (i, 0)), full3, full3, full3, full3],
        out_shape=[_sds((L, IN_A), _ACT)] + [_sds((A_HEADS, CHUNK, LANES), F32)] * 4,
    )(z, dya, ws, wsT, bfull, lgf, lbf)


def _head_rstd(x, lo):
    sq = x * x
    s_lo = jnp.sum(jnp.where(lo, sq, 0.0), axis=-1, keepdims=True)
    s_hi = jnp.sum(jnp.where(lo, 0.0, sq), axis=-1, keepdims=True)
    return jnp.where(lo, lax.rsqrt(s_lo * (1.0 / HEAD_DIM) + EPS), lax.rsqrt(s_hi * (1.0 / HEAD_DIM) + EPS))


def _rot_half(x, first):
    return jnp.where(first, -pltpu.roll(x, LANES - HEAD_DIM // 2, axis=1), pltpu.roll(x, HEAD_DIM // 2, axis=1))


def _qk_prep(z, cos, sin, gq, gk, name):
    L = z.shape[0]
    tl = _tile(L, _TL)
    nq = IN_Q // LANES

    def body(q_ref, k_ref, c_ref, s_ref, gq_ref, gk_ref, qo_ref, ko_ref):
        lane = _lane((tl, LANES))
        lo = lane < HEAD_DIM
        first = (lane % HEAD_DIM) < (HEAD_DIM // 2)
        c, s = c_ref[...], s_ref[...]

        def prep(x, g):
            xn = (x * _head_rstd(x, lo)) * g
            return xn * c + _rot_half(xn, first) * s

        for j in range(nq):
            qo_ref[:, j * LANES:(j + 1) * LANES] = prep(q_ref[:, j * LANES:(j + 1) * LANES], gq_ref[...]).astype(qo_ref.dtype)
        ko_ref[...] = prep(k_ref[...], gk_ref[...]).astype(ko_ref.dtype)

    return pl.pallas_call(
        body, name=name, grid=(L // tl,),
        in_specs=[pl.BlockSpec((tl, IN_Q), lambda i: (i, 1)),
                  pl.BlockSpec((tl, IN_KV), lambda i: (i, 8)),
                  pl.BlockSpec((tl, LANES), lambda i: (i, 0)), pl.BlockSpec((tl, LANES), lambda i: (i, 0)),
                  pl.BlockSpec((1, LANES), lambda i: (0, 0)), pl.BlockSpec((1, LANES), lambda i: (0, 0))],
        out_specs=[pl.BlockSpec((tl, IN_Q), lambda i: (i, 0)), pl.BlockSpec((tl, IN_KV), lambda i: (i, 0))],
        out_shape=[_sds((L, IN_Q), _ACT), _sds((L, IN_KV), _ACT)],
    )(z, z, cos, sin, gq, gk)


def _qk_prep_bwd(z, dq, dkc, dkp, dvc, dvp, cos, sin, gq, gk, name):
    L = z.shape[0]
    tl = _ATT_QB * WINDOW
    nb = L // tl
    nq = IN_Q // LANES

    def body(q_ref, k_ref, dq_ref, dkc_ref, dkp_ref, dvc_ref, dvp_ref, c_ref, s_ref, gq_ref, gk_ref,
             dzq_ref, dzk_ref, dzv_ref, dgq_ref, dgk_ref):
        n = pl.program_id(0)

        @pl.when(n == 0)
        def _():
            dgq_ref[...] = jnp.zeros_like(dgq_ref)
            dgk_ref[...] = jnp.zeros_like(dgk_ref)

        lane = _lane((tl, LANES))
        lo = lane < HEAD_DIM
        first = (lane % HEAD_DIM) < (HEAD_DIM // 2)
        c, s = c_ref[...], s_ref[...]
        has_next = jnp.where(n < nb - 1, 1.0, 0.0)

        def bwd(x, g, dy):
            r = _head_rstd(x, lo)
            xhat = x * r
            dxn = dy * c - _rot_half(dy * s, first)
            gy = dxn * g
            t = gy * xhat
            m_lo = jnp.sum(jnp.where(lo, t, 0.0), axis=-1, keepdims=True)
            m_hi = jnp.sum(jnp.where(lo, 0.0, t), axis=-1, keepdims=True)
            m = jnp.where(lo, m_lo, m_hi) * (1.0 / HEAD_DIM)
            dx = r * (gy - xhat * m)
            dg = jnp.sum(dxn * xhat, axis=0, keepdims=True)
            return dx, dg

        dgq = jnp.zeros((1, LANES), F32)
        for j in range(nq):
            sl = slice(j * LANES, (j + 1) * LANES)
            dx, dg = bwd(q_ref[:, sl], gq_ref[...], dq_ref[:, sl].astype(F32))
            dzq_ref[:, sl] = dx.astype(dzq_ref.dtype)
            dgq = dgq + dg
        dgq_ref[...] += dgq + pltpu.roll(dgq, HEAD_DIM, axis=1)
        def with_next(cur_ref, nxt_ref):
            head = jnp.zeros((tl - WINDOW, IN_KV), F32)
            return cur_ref[...] + jnp.concatenate([head, has_next * nxt_ref[...]], axis=0)

        dx, dg = bwd(k_ref[...], gk_ref[...], with_next(dkc_ref, dkp_ref))
        dzk_ref[...] = dx.astype(dzk_ref.dtype)
        dgk_ref[...] += dg + pltpu.roll(dg, HEAD_DIM, axis=1)
        dzv_ref[...] = with_next(dvc_ref, dvp_ref).astype(dzv_ref.dtype)

    nxt = lambda i: (jnp.minimum(i + 1, nb - 1), 0)
    cur = lambda i: (i, 0)
    kv = pl.BlockSpec((tl, IN_KV), cur)
    kvn = pl.BlockSpec((WINDOW, IN_KV), nxt)
    one = pl.BlockSpec((1, LANES), lambda i: (0, 0))
    return pl.pallas_call(
        body, name=name, grid=(nb,),
        in_specs=[pl.BlockSpec((tl, IN_Q), lambda i: (i, 1)), pl.BlockSpec((tl, IN_KV), lambda i: (i, 8)),
                  pl.BlockSpec((tl, IN_Q), cur), kv, kvn, kv, kvn,
                  kv, kv, one, one],
        out_specs=[pl.BlockSpec((tl, IN_Q), cur), kv, kv, one, one],
        out_shape=[_sds((L, IN_Q), _ACT), _sds((L, IN_KV), _ACT),
                   _sds((L, IN_KV), _ACT), _sds((1, LANES), F32),
                   _sds((1, LANES), F32)],
    )(z, z, dq, dkc, dkp, dvc, dvp, cos, sin, gq, gk)


def _attn_mask(n):
    shp = (2 * WINDOW, B_GROUP * WINDOW)
    qi = _lane(shp) % WINDOW
    kj = _row(shp)
    off = 0 if n is None else jnp.where(n > 0, 0, 4 * WINDOW)
    return ((kj >= WINDOW) & (kj - WINDOW <= qi)) | ((kj < WINDOW) & (kj > qi + off))


def _kv_lanes(j):
    lane = _lane((WINDOW, LANES))
    return (lane >= j * HEAD_DIM) & (lane < (j + 1) * HEAD_DIM)


_ATT_QB = 4


def _stack_heads(ref, rows, j, kvl):
    parts = []
    for g in range(B_GROUP):
        h = j * B_GROUP + g
        slab = ref[rows, (h // 2) * LANES:(h // 2 + 1) * LANES].astype(F32)
        if (h % 2) != j:
            slab = pltpu.roll(slab, HEAD_DIM, axis=1)
        parts.append(jnp.where(kvl, slab, 0.0))
    return jnp.concatenate(parts, axis=0)


def _attn_probs(qs, k2, sink_row, mask):
    s = _dot(k2, qs, _NT) * (HEAD_DIM ** -0.5)
    s = jnp.where(mask, s, NEG)
    m = jnp.maximum(jnp.max(s, axis=0, keepdims=True), sink_row)
    p = jnp.exp(s - m)
    esink = jnp.exp(sink_row - m)
    inv = 1.0 / (jnp.sum(p, axis=0, keepdims=True) + esink)
    return p * inv, esink * inv


def _sink_row(sink_ref, j):
    lane = _lane((1, B_GROUP * WINDOW))
    row = jnp.full((1, B_GROUP * WINDOW), sink_ref[j * B_GROUP], F32)
    for g in range(1, B_GROUP):
        row = jnp.where(lane >= g * WINDOW, sink_ref[j * B_GROUP + g], row)
    return row


def _attn_fwd(q, k, z, sinks, name):
    L = q.shape[0]
    QB = _ATT_QB
    tq = QB * WINDOW
    prev = lambda n: (jnp.maximum(QB * n - 1, 0), 0)
    prev_v = lambda n: (jnp.maximum(QB * n - 1, 0), 9)

    def body(s_ref, q_ref, kp_ref, kc_ref, vp_ref, vc_ref, o_ref):
        n = pl.program_id(0)
        k3 = jnp.concatenate([kp_ref[...], kc_ref[...]], axis=0)
        v3 = jnp.concatenate([vp_ref[...], vc_ref[...]], axis=0)
        for b in range(QB):
            rows = slice(b * WINDOW, (b + 1) * WINDOW)
            mask = _attn_mask(n if b == 0 else None)
            k2 = k3[b * WINDOW:(b + 2) * WINDOW]
            v2 = v3[b * WINDOW:(b + 2) * WINDOW]
            slabs = [None] * (IN_Q // LANES)
            for j in range(B_KV_HEADS):
                kvl = _kv_lanes(j)
                qs = _stack_heads(q_ref, rows, j, kvl)
                pn, _ = _attn_probs(qs, k2, _sink_row(s_ref, j), mask)
                o = _dot(pn, v2, _TN)
                for g in range(B_GROUP):
                    h = j * B_GROUP + g
                    piece = jnp.where(kvl, o[g * WINDOW:(g + 1) * WINDOW], 0.0)
                    if (h % 2) != j:
                        piece = pltpu.roll(piece, HEAD_DIM, axis=1)
                    slabs[h // 2] = piece if slabs[h // 2] is None else slabs[h // 2] + piece
            for t, sl in enumerate(slabs):
                o_ref[rows, t * LANES:(t + 1) * LANES] = sl

    return pl.pallas_call(
        body, name=name, grid=(L // tq,),
        in_specs=[pl.BlockSpec(memory_space=pltpu.SMEM),
                  pl.BlockSpec((tq, IN_Q), lambda n: (n, 0)),
                  pl.BlockSpec((WINDOW, IN_KV), prev), pl.BlockSpec((tq, IN_KV), lambda n: (n, 0)),
                  pl.BlockSpec((WINDOW, IN_KV), prev_v), pl.BlockSpec((tq, IN_KV), lambda n: (n, 9))],
        out_specs=pl.BlockSpec((tq, IN_Q), lambda n: (n, 0)),
        out_shape=_sds((L, IN_Q), F32),
    )(sinks, q, k, k, z, z)


def _attn_bwd(q, k, z, sinks, dyb, name):
    L = q.shape[0]
    QB = _ATT_QB
    tq = QB * WINDOW
    nsteps = L // tq
    prev = lambda n: (jnp.maximum(QB * n - 1, 0), 0)
    prev_v = lambda n: (jnp.maximum(QB * n - 1, 0), 9)
    cur = lambda n: (n, 0)

    def body(s_ref, q_ref, kp_ref, kc_ref, vp_ref, vc_ref, d_ref, dq_ref, dkc_ref, dkp_ref, dvc_ref, dvp_ref, ds_ref):
        n = pl.program_id(0)

        @pl.when(n == 0)
        def _():
            ds_ref[...] = jnp.zeros_like(ds_ref)

        k3 = jnp.concatenate([kp_ref[...], kc_ref[...]], axis=0)
        v3 = jnp.concatenate([vp_ref[...], vc_ref[...]], axis=0)
        dkb = [None] * (QB + 1)
        dvb = [None] * (QB + 1)
        dsink = jnp.zeros((1, LANES), F32)
        lane1 = _lane((1, LANES))
        add = lambda acc, v: v if acc is None else acc + v
        for b in range(QB):
            rows = slice(b * WINDOW, (b + 1) * WINDOW)
            mask = _attn_mask(n if b == 0 else None)
            k2 = k3[b * WINDOW:(b + 2) * WINDOW]
            v2 = v3[b * WINDOW:(b + 2) * WINDOW]
            slabs = [None] * (IN_Q // LANES)
            for j in range(B_KV_HEADS):
                kvl = _kv_lanes(j)
                qs = _stack_heads(q_ref, rows, j, kvl)
                dos = _stack_heads(d_ref, rows, j, kvl)
                pn, psink = _attn_probs(qs, k2, _sink_row(s_ref, j), mask)
                dp = _dot(v2, dos, _NT)
                dd = jnp.sum(pn * dp, axis=0, keepdims=True)
                dss = (pn * (dp - dd)) * (HEAD_DIM ** -0.5)
                dqs = _dot(dss, k2, _TN)
                dk2 = _dot(dss, qs)
                dv2 = _dot(pn, dos)
                dkb[b], dkb[b + 1] = add(dkb[b], dk2[:WINDOW]), add(dkb[b + 1], dk2[WINDOW:])
                dvb[b], dvb[b + 1] = add(dvb[b], dv2[:WINDOW]), add(dvb[b + 1], dv2[WINDOW:])
                sd = psink * dd
                for g in range(B_GROUP):
                    h = j * B_GROUP + g
                    piece = jnp.where(kvl, dqs[g * WINDOW:(g + 1) * WINDOW], 0.0)
                    if (h % 2) != j:
                        piece = pltpu.roll(piece, HEAD_DIM, axis=1)
                    slabs[h // 2] = piece if slabs[h // 2] is None else slabs[h // 2] + piece
                    tot = jnp.sum(sd[:, g * WINDOW:(g + 1) * WINDOW], axis=1, keepdims=True)
                    dsink = dsink - jnp.where(lane1 == h, tot, 0.0)
            for t, sl in enumerate(slabs):
                dq_ref[rows, t * LANES:(t + 1) * LANES] = sl
        dkp_ref[...] = dkb[0]
        dvp_ref[...] = dvb[0]
        for b in range(QB):
            dkc_ref[b * WINDOW:(b + 1) * WINDOW, :] = dkb[b + 1]
            dvc_ref[b * WINDOW:(b + 1) * WINDOW, :] = dvb[b + 1]
        ds_ref[0:1, :] += dsink

    kvs = pl.BlockSpec((tq, IN_KV), cur)
    kvp = pl.BlockSpec((WINDOW, IN_KV), cur)
    kvo = _sds((L, IN_KV), F32)
    kvpo = _sds((nsteps * WINDOW, IN_KV), F32)
    return pl.pallas_call(
        body, name=name, grid=(nsteps,),
        in_specs=[pl.BlockSpec(memory_space=pltpu.SMEM),
                  pl.BlockSpec((tq, IN_Q), cur),
                  pl.BlockSpec((WINDOW, IN_KV), prev), kvs,
                  pl.BlockSpec((WINDOW, IN_KV), prev_v), pl.BlockSpec((tq, IN_KV), lambda n: (n, 9)),
                  pl.BlockSpec((tq, IN_Q), cur)],
        out_specs=[pl.BlockSpec((tq, IN_Q), cur), kvs, kvp, kvs, kvp, pl.BlockSpec((SUBLANES, LANES), lambda n: (0, 0))],
        out_shape=[_sds((L, IN_Q), F32), kvo, kvpo, kvo, kvpo, _sds((SUBLANES, LANES), F32)],
    )(sinks, q, k, k, z, z, dyb)


def _ssm_disc(are, aim, ldt, bre, bim):
    dt = jnp.exp(ldt)
    mag = jnp.exp(are * dt)
    lr, li = mag * jnp.cos(aim * dt), mag * jnp.sin(aim * dt)
    den = are * are + aim * aim
    xr, xi = lr - 1.0, li
    cr, ci = (xr * are + xi * aim) / den, (xi * are - xr * aim) / den
    return lr, li, cr * bre - ci * bim, cr * bim + ci * bre


def _ssm_prep(are, aim, ldt, bre, bim):
    shp3, shpb = are.shape, bre.shape

    def body(are_ref, aim_ref, ldt_ref, bre_ref, bim_ref, lr_ref, li_ref, br_ref, bi_ref):
        lr, li, br, bi = _ssm_disc(are_ref[...], aim_ref[...], ldt_ref[...], bre_ref[...], bim_ref[...])
        lr_ref[...] = lr
        li_ref[...] = li
        br_ref[...] = br
        bi_ref[...] = bi

    return pl.pallas_call(
        body, name="ssm_prep",
        out_shape=[_sds(shp3, F32)] * 2 + [_sds(shpb, F32)] * 2,
    )(are, aim, ldt, bre, bim)


def _ssm_prep_bwd(are, aim, ldt, bre, bim, dlr, dli, dbr, dbi):
    shp3, shpb = are.shape, bre.shape

    def body(are_ref, aim_ref, ldt_ref, bre_ref, bim_ref, dlr_ref, dli_ref, dbr_ref, dbi_ref,
             o_are, o_aim, o_ldt, o_bre, o_bim):
        _, vjp = jax.vjp(_ssm_disc, are_ref[...], aim_ref[...], ldt_ref[...], bre_ref[...], bim_ref[...])
        g = vjp((dlr_ref[...], dli_ref[...], dbr_ref[...], dbi_ref[...]))
        o_are[...] = g[0]
        o_aim[...] = g[1]
        o_ldt[...] = jnp.broadcast_to(jnp.sum(g[2], axis=-1, keepdims=True), shp3)
        o_bre[...] = g[3]
        o_bim[...] = g[4]

    return pl.pallas_call(
        body, name="ssm_prep_bwd",
        out_shape=[_sds(shp3, F32)] * 3 + [_sds(shpb, F32)] * 2,
    )(are, aim, ldt, bre, bim, dlr, dli, dbr, dbi)


_SCAN_TB = 512
_SCAN_W = 256


def _cmul(ar, ai, br, bi):
    return ar * br - ai * bi, ar * bi + ai * br


def _ssm_scan(x, lam_r, lam_i, name, reverse=False, states=None):
    L = x.shape[0]
    tb = _tile(L, _SCAN_TB)
    nrb = L // tb
    nt = tb // SUBLANES
    W = _SCAN_W
    with_da = states is not None

    def body(*refs):
        if with_da:
            xr_ref, xi_ref, sr_ref, si_ref, ar_ref, ai_ref, o_ref, dar_ref, dai_ref, cr_ref, ci_ref = refs
        else:
            xr_ref, xi_ref, ar_ref, ai_ref, o_ref, cr_ref, ci_ref = refs
        step = pl.program_id(0)

        @pl.when(step == 0)
        def _():
            cr_ref[...] = jnp.zeros_like(cr_ref)
            ci_ref[...] = jnp.zeros_like(ci_ref)
            if with_da:
                dar_ref[...] = jnp.zeros_like(dar_ref)
                dai_ref[...] = jnp.zeros_like(dai_ref)

        row = _row((SUBLANES, W))

        def shift(v, d, fill):
            if reverse:
                return jnp.where(row < SUBLANES - d, pltpu.roll(v, SUBLANES - d, axis=0), fill)
            return jnp.where(row >= d, pltpu.roll(v, d, axis=0), fill)

        edge = 0 if reverse else SUBLANES - 1
        for wb in range(N_STATE // W):
            cols = slice(wb * W, (wb + 1) * W)
            a1r = jnp.broadcast_to(ar_ref[:, cols], (SUBLANES, W))
            a1i = jnp.broadcast_to(ai_ref[:, cols], (SUBLANES, W))
            if reverse:
                a1i = -a1i
            a2r, a2i = _cmul(a1r, a1i, a1r, a1i)
            a4r, a4i = _cmul(a2r, a2i, a2r, a2i)
            pws = ((1, a1r, a1i), (2, a2r, a2i), (4, a4r, a4i))
            pr, pi = a1r, a1i
            for d, _, _ in pws:
                qr, qi = _cmul(pr, pi, shift(pr, d, 1.0), shift(pi, d, 0.0))
                pr, pi = qr, qi
            mws = []
            for d, er, ei in pws:
                ok = (row < SUBLANES - d) if reverse else (row >= d)
                mws.append(((SUBLANES - d) if reverse else d, jnp.where(ok, er, 0.0), jnp.where(ok, ei, 0.0)))

            def tile(i, carry):
                cr, ci, dr, di = carry
                t = (nt - 1 - i) if reverse else i
                r0 = pl.multiple_of(t * SUBLANES, SUBLANES)
                vr = xr_ref[pl.ds(r0, SUBLANES), cols]
                vi = xi_ref[pl.ds(r0, SUBLANES), cols]
                for sh, er, ei in mws:
                    tr, ti = _cmul(er, ei, pltpu.roll(vr, sh, axis=0), pltpu.roll(vi, sh, axis=0))
                    vr, vi = vr + tr, vi + ti
                tr, ti = _cmul(pr, pi, cr, ci)
                vr, vi = vr + tr, vi + ti
                o_ref[pl.ds(r0, SUBLANES), cols] = vr
                o_ref[pl.ds(r0, SUBLANES), slice(N_STATE + wb * W, N_STATE + (wb + 1) * W)] = vi
                if with_da:
                    gr = jnp.where(row < SUBLANES - 1, pltpu.roll(vr, SUBLANES - 1, axis=0), cr)
                    gi = jnp.where(row < SUBLANES - 1, pltpu.roll(vi, SUBLANES - 1, axis=0), ci)
                    sr = sr_ref[pl.ds(r0, SUBLANES), cols]
                    si = si_ref[pl.ds(r0, SUBLANES), cols]
                    dr = dr + sr * gr + si * gi
                    di = di + sr * gi - si * gr
                ncr = jnp.broadcast_to(vr[edge:edge + 1, :], (SUBLANES, W))
                nci = jnp.broadcast_to(vi[edge:edge + 1, :], (SUBLANES, W))
                return ncr, nci, dr, di

            zero = jnp.zeros((SUBLANES, W), F32)
            cr, ci, dr, di = lax.fori_loop(0, nt, tile, (cr_ref[:, cols], ci_ref[:, cols], zero, zero), unroll=2)
            cr_ref[:, cols] = cr
            ci_ref[:, cols] = ci
            if with_da:
                dar_ref[:, cols] += dr
                dai_ref[:, cols] += di

        if with_da:
            @pl.when(step == nrb - 1)
            def _():
                dar_ref[...] = jnp.broadcast_to(jnp.sum(dar_ref[...], axis=0, keepdims=True), dar_ref.shape)
                dai_ref[...] = jnp.broadcast_to(jnp.sum(dai_ref[...], axis=0, keepdims=True), dai_ref.shape)

    rb = (lambda i: (nrb - 1 - i, 0)) if reverse else (lambda i: (i, 0))
    rb_im = (lambda i: (nrb - 1 - i, 1)) if reverse else (lambda i: (i, 1))
    blk_r = pl.BlockSpec((tb, N_STATE), rb)
    blk_i = pl.BlockSpec((tb, N_STATE), rb_im)
    one = pl.BlockSpec((1, N_STATE), lambda i: (0, 0))
    acc = pl.BlockSpec((SUBLANES, N_STATE), lambda i: (0, 0))
    ins = [x, x] + ([states, states] if with_da else []) + [lam_r, lam_i]
    in_specs = [blk_r, blk_i] + ([blk_r, blk_i] if with_da else []) + [one, one]
    out_specs = [pl.BlockSpec((tb, 2 * N_STATE), rb)] + ([acc, acc] if with_da else [])
    out_shape = [_sds((L, 2 * N_STATE), F32)] + (
        [_sds((SUBLANES, N_STATE), F32)] * 2 if with_da else [])
    outs = pl.pallas_call(
        body, name=name, grid=(nrb,), in_specs=in_specs, out_specs=out_specs, out_shape=out_shape,
        scratch_shapes=[pltpu.VMEM((SUBLANES, N_STATE), F32)] * 2,
        compiler_params=_cparams((6 if with_da else 4) * _nbytes((tb, N_STATE), F32),
                                 dimension_semantics=("arbitrary",)),
    )(*ins)
    return outs if with_da else outs[0]


_GROUPS = ((0, 256), (256, 768), (768, 1024))


def _merge_fwd(ya, yb, g12, mixg, name):
    L = ya.shape[0]
    tl = _tile(L, _TL)

    def body(a_ref, b_ref, g_ref, m_ref, o_ref):
        g12v = g_ref[...]
        yc = g12v[:, :C_WIDTH] * _sigmoid(g12v[:, C_WIDTH:])
        for (lo, hi), y in zip(_GROUPS, (a_ref[...], b_ref[...], yc)):
            r = lax.rsqrt(jnp.mean(y * y, axis=-1, keepdims=True) + EPS)
            o_ref[:, lo:hi] = ((y * r) * m_ref[:, lo:hi]).astype(o_ref.dtype)

    row = lambda w: pl.BlockSpec((tl, w), lambda i: (i, 0))
    return pl.pallas_call(
        body, name=name, grid=(L // tl,),
        in_specs=[row(256), row(512), row(512), pl.BlockSpec((1, D_MODEL), lambda i: (0, 0))],
        out_specs=row(D_MODEL), out_shape=_sds((L, D_MODEL), _ACT),
    )(ya, yb, g12, mixg.reshape(1, D_MODEL))


def _merge_bwd(dy, ya, yb, g12, mixg, name):
    L = ya.shape[0]
    tl = _tile(L, _TL)

    def body(d_ref, a_ref, b_ref, g_ref, m_ref, da_ref, db_ref, dg_ref, dm_ref):
        @pl.when(pl.program_id(0) == 0)
        def _():
            dm_ref[...] = jnp.zeros_like(dm_ref)

        g12v = g_ref[...]
        g1, sg = g12v[:, :C_WIDTH], _sigmoid(g12v[:, C_WIDTH:])
        yc = g1 * sg
        outs = []
        for (lo, hi), y in zip(_GROUPS, (a_ref[...], b_ref[...], yc)):
            r = lax.rsqrt(jnp.mean(y * y, axis=-1, keepdims=True) + EPS)
            xhat = y * r
            d = d_ref[:, lo:hi]
            gy = d * m_ref[:, lo:hi]
            outs.append(r * (gy - xhat * jnp.mean(gy * xhat, axis=-1, keepdims=True)))
            dm_ref[:, lo:hi] += jnp.sum(d * xhat, axis=0, keepdims=True)
        da_ref[...] = outs[0]
        db_ref[...] = outs[1]
        dyc = outs[2]
        dg_ref[:, :C_WIDTH] = (dyc * sg).astype(dg_ref.dtype)
        dg_ref[:, C_WIDTH:] = (dyc * g1 * sg * (1.0 - sg)).astype(dg_ref.dtype)

    row = lambda w: pl.BlockSpec((tl, w), lambda i: (i, 0))
    one = pl.BlockSpec((1, D_MODEL), lambda i: (0, 0))
    return pl.pallas_call(
        body, name=name, grid=(L // tl,),
        in_specs=[row(D_MODEL), row(256), row(512), row(512), one],
        out_specs=[row(256), row(512), row(512), one],
        out_shape=[_sds((L, 256), F32), _sds((L, 512), F32),
                   _sds((L, 512), _ACT), _sds((1, D_MODEL), F32)],
    )(dy, ya, yb, g12, mixg.reshape(1, D_MODEL))


def _ple_bwd_elem(dh, gate, e, name):
    L, D = dh.shape
    tl = _tile(L, _TL)

    def body(d_ref, g_ref, e_ref, p_ref, o_ref):
        d, g = d_ref[...], g_ref[...]
        p_ref[...] = (d * e_ref[...] * g * (1.0 - g)).astype(p_ref.dtype)
        o_ref[...] = (d * g).astype(o_ref.dtype)

    row = pl.BlockSpec((tl, D), lambda i: (i, 0))
    return pl.pallas_call(
        body, name=name, grid=(L // tl,), in_specs=[row] * 3, out_specs=[row] * 2,
        out_shape=[_sds((L, D), _ACT)] * 2,
        compiler_params=_cparams(4 * _nbytes((tl, D), F32)),
    )(dh, gate, e)


def _dskip_bwd(dy, z, name):
    L = dy.shape[0]
    tl = _tile(L, _TL)

    def body(d_ref, u_ref, o_ref):
        @pl.when(pl.program_id(0) == 0)
        def _():
            o_ref[...] = jnp.zeros_like(o_ref)

        o_ref[...] += jnp.sum(d_ref[...] * u_ref[...], axis=0, keepdims=True)

    return pl.pallas_call(
        body, name=name, grid=(L // tl,),
        in_specs=[pl.BlockSpec((tl, C_WIDTH), lambda i: (i, 0)), pl.BlockSpec((tl, C_WIDTH), lambda i: (i, 5))],
        out_specs=pl.BlockSpec((1, C_WIDTH), lambda i: (0, 0)),
        out_shape=_sds((1, C_WIDTH), F32),
    )(dy, z)


def _loss_fwd_bwd(y, target):
    L, D = y.shape
    tl = _tile(L, _TL)

    def body(y_ref, t_ref, l_ref, d_ref):
        @pl.when(pl.program_id(0) == 0)
        def _():
            l_ref[...] = jnp.zeros_like(l_ref)

        e = y_ref[...] - t_ref[...]
        d_ref[...] = e * (1.0 / D)
        part = jnp.sum(jnp.sum(e * e, axis=-1, keepdims=True), axis=0, keepdims=True)
        l_ref[...] += jnp.broadcast_to(part, l_ref.shape)

    row = pl.BlockSpec((tl, D), lambda i: (i, 0))
    return pl.pallas_call(
        body, name="loss", grid=(L // tl,), in_specs=[row, row],
        out_specs=[pl.BlockSpec((SUBLANES, LANES), lambda i: (0, 0)), row],
        out_shape=[_sds((SUBLANES, LANES), F32), _sds((L, D), F32)],
    )(y, target)


def _adamw(w, g, m, v, name):
    R, C = w.shape
    tr = R if R <= 512 else _tile_rows(R, 512)

    def body(w_ref, g_ref, m_ref, v_ref, d_ref, nm_ref, nv_ref):
        gv = g_ref[...]
        nm = ADAM_B1 * m_ref[...] + (1.0 - ADAM_B1) * gv
        nv = ADAM_B2 * v_ref[...] + (1.0 - ADAM_B2) * (gv * gv)
        m_hat = nm / (1.0 - ADAM_B1 ** ADAM_STEP)
        v_hat = nv / (1.0 - ADAM_B2 ** ADAM_STEP)
        d_ref[...] = -ADAM_LR * (m_hat / (jnp.sqrt(v_hat) + ADAM_EPS) + ADAM_WD * w_ref[...])
        nm_ref[...] = nm
        nv_ref[...] = nv

    blk = pl.BlockSpec((tr, C), lambda i: (i, 0))
    return pl.pallas_call(
        body, name=name, grid=(R // tr,), in_specs=[blk] * 4, out_specs=[blk] * 3,
        out_shape=[_sds((R, C), F32)] * 3,
        compiler_params=_cparams(7 * _nbytes((tr, C), F32)),
    )(w, g, m, v)


def _tile_rows(R, pref):
    t = pref
    while R % t:
        t -= SUBLANES
    assert t > 0
    return t


def _add_n(xs, name):
    R, C = xs[0].shape
    tr = R if R <= 512 else _tile_rows(R, 512)
    n = len(xs)

    def body(*refs):
        acc = refs[0][...].astype(F32)
        for r in refs[1:n]:
            acc = acc + r[...].astype(F32)
        refs[n][...] = acc

    blk = pl.BlockSpec((tr, C), lambda i: (i, 0))
    return pl.pallas_call(
        body, name=name, grid=(R // tr,), in_specs=[blk] * n, out_specs=blk,
        out_shape=_sds((R, C), F32),
        compiler_params=_cparams((n + 1) * _nbytes((tr, C), F32)),
    )(*xs)


class _Exchange:
    def __init__(self, ins, aliased, fresh, n_sems, start, wait, done):
        self.ins, self.aliased, self.fresh, self.n_sems = ins, aliased, fresh, n_sems
        self.start, self.wait, self.done = start, wait, done


def _mm_host(lp, key, *args, **kw):
    plan = lp.get(key)
    if plan is None:
        return _mm(*args, **kw)
    if not isinstance(plan, _Exchange):
        plan = plan()
    res, outs = _mm(*args, comm=plan, **kw)
    plan.done(outs)
    return res


def _relu2(acc):
    r = jnp.maximum(acc, 0.0)
    return (r * r,)


def _rms_rows(x, g):
    return (x * lax.rsqrt(jnp.mean(x * x, axis=-1, keepdims=True) + EPS)) * g


def _resid_norm_epi(acc, res, g):
    h = res + acc
    return h, _rms_rows(h, g)


def _rms_bwd_epi(acc, h, dres, g):
    r = lax.rsqrt(jnp.mean(h * h, axis=-1, keepdims=True) + EPS)
    xhat = h * r
    gy = acc * g
    dx = r * (gy - xhat * jnp.mean(gy * xhat, axis=-1, keepdims=True))
    return dres + dx, jnp.sum(acc * xhat, axis=0, keepdims=True)


def _layer_fwd(h, xn, lp, cos, sin, g_next):
    L = h.shape[0]
    row = lambda n: lp[n].reshape(1, D_MODEL)
    z = _mm(xn, lp["w_in"], mode="nn", M=L, N=IN_COLS, K=D_MODEL, b_cb=True, out_dtypes=[F32], name="f_w_in")
    ya = _gmlp_fwd(z, lp["ws"], lp["bfull"], lp["lgf"], lp["lbf"], "f_gmlp")
    q, k = _qk_prep(z, cos, sin, lp["gq"], lp["gk"], "f_qk_prep")
    yb = _attn_fwd(q, k, z, lp["sinks"], "f_attn")
    bu = _mm(z, lp["bcat"], mode="nn", M=L, N=2 * N_STATE, K=C_WIDTH, a_off=5, tk=C_WIDTH,
             out_dtypes=[F32], name="f_ssm_in")
    S = _ssm_scan(bu, lp["lam_r"], lp["lam_i"], "f_ssm_scan")
    y, yg = _mm(S, lp["ccat"], mode="nn", M=L, N=C_WIDTH, K=2 * N_STATE, tk=2 * N_STATE,
                extras=[(z, 5), (lp["dskip"], 0)], out_dtypes=[F32, _ACT], name="f_ssm_out",
                epi=lambda acc, u, dsk: (acc + dsk * u, _gelu(acc + dsk * u)))
    g12 = _mm(yg, lp["w12"], mode="nn", M=L, N=2 * C_WIDTH, K=C_WIDTH, out_dtypes=[F32], name="f_glu")
    ycat = _merge_fwd(ya, yb, g12, lp["mix_out_g"], "f_merge")
    h1, hn = _mm(ycat, lp["w_out"], mode="nn", M=L, N=D_MODEL, K=D_MODEL, extras=[(h, 0), (row("mlp_norm_g"), 0)],
                 epi=_resid_norm_epi, out_dtypes=[F32, _ACT], name="f_w_out")
    r = _mm(hn, lp["w_ff1"], mode="nn", M=L, N=D_FF, K=D_MODEL, b_cb=True, epi=_relu2,
            out_dtypes=[_ACT], name="f_ff1")
    h2, hn3 = _mm_host(lp, "x_ff2", r, lp["w_ff2"], mode="nn", M=L, N=D_MODEL, K=D_FF,
                       extras=[(h1, 0), (row("ple_norm_g"), 0)],
                       epi=_resid_norm_epi, out_dtypes=[F32, _ACT], name="f_ff2")
    e = _mm(lp["p"], lp["w_ple_proj"], mode="nn", M=L, N=D_MODEL, K=PLE_DIM, b_cb=True, tk=PLE_DIM,
            out_dtypes=[F32], name="f_ple_proj")

    def gate_epi(acc, h2_, e_, *g):
        gate_ = _sigmoid(acc)
        h3_ = h2_ + gate_ * e_
        return (h3_, gate_) + ((_rms_rows(h3_, g[0]),) if g else ())

    outs = _mm_host(lp, "x_gate", hn3, lp["w_ple_gate"], mode="nn", M=L, N=D_MODEL, K=D_MODEL,
                    extras=[(h2, 0), (e, 0)] + ([(g_next.reshape(1, D_MODEL), 0)] if g_next is not None else []),
                    epi=gate_epi, out_dtypes=[F32, F32] + ([_ACT] if g_next is not None else []), name="f_ple_gate")
    h3, gate = outs[0], outs[1]
    xn_next = outs[2] if g_next is not None else None
    saved = dict(h=h, xn=xn, z=z, ya=ya, q=q, k=k, yb=yb, S=S, y=y, yg=yg, g12=g12, ycat=ycat, h1=h1, hn=hn,
                 r=r, h2=h2, hn3=hn3, e=e, gate=gate)
    return h3, xn_next, saved


def _layer_bwd(dh3, lp, sv, cos, sin):
    L = dh3.shape[0]
    z = sv["z"]
    dpre, de = _ple_bwd_elem(dh3, sv["gate"], sv["e"], "b_ple_elem")
    stk = {n: None for n in BIG}
    d_gate = _mm(sv["hn3"], dpre, mode="tn", M=D_MODEL, N=D_MODEL, K=L, out_dtypes=[F32], name="b_dw_gate",
                 o_stack=stk["w_ple_gate"])
    d_proj = _mm(lp["p"], de, mode="tn", M=PLE_DIM, N=D_MODEL, K=L, o_cb=True, tm=PLE_DIM,
                 out_dtypes=[F32], name="b_dw_proj", o_stack=stk["w_ple_proj"])
    row = lambda n: lp[n].reshape(1, D_MODEL)
    dh2, dg_ple = _mm(dpre, lp["w_ple_gate"], mode="nt", M=L, N=D_MODEL, K=D_MODEL,
                      extras=[(sv["h2"], 0), (dh3, 0), (row("ple_norm_g"), 0)], epi=_rms_bwd_epi,
                      out_dtypes=[F32, F32], n_acc=1, name="b_dx_gate")
    da = _mm_host(lp, "x_bwd", dh2, lp["w_ff2"], mode="nt", M=L, N=D_FF, K=D_MODEL, extras=[(sv["r"], 0)],
                  epi=lambda acc, r_: (acc * (2.0 * jnp.sqrt(r_.astype(F32))),), out_dtypes=[_ACT], name="b_dx_ff2")
    d_ff2 = _mm(sv["r"], dh2, mode="tn", M=D_FF, N=D_MODEL, K=L, out_dtypes=[F32], name="b_dw_ff2",
                o_stack=stk["w_ff2"])
    d_ff1 = _mm(sv["hn"], da, mode="tn", M=D_MODEL, N=D_FF, K=L, o_cb=True, out_dtypes=[F32], name="b_dw_ff1",
                o_stack=stk["w_ff1"])
    dh1, dg_mlp = _mm(da, lp["w_ff1"], mode="nt", M=L, N=D_MODEL, K=D_FF, b_cb=True,
                      extras=[(sv["h1"], 0), (dh2, 0), (row("mlp_norm_g"), 0)], epi=_rms_bwd_epi,
                      out_dtypes=[F32, F32], n_acc=1, name="b_dx_ff1")
    d_out = _mm(sv["ycat"], dh1, mode="tn", M=D_MODEL, N=D_MODEL, K=L, out_dtypes=[F32], name="b_dw_out",
                o_stack=stk["w_out"])
    dycat = _mm(dh1, lp["w_out"], mode="nt", M=L, N=D_MODEL, K=D_MODEL, out_dtypes=[F32], name="b_dx_out")
    dya, dyb, dg12, dmix = _merge_bwd(dycat, sv["ya"], sv["yb"], sv["g12"], lp["mix_out_g"], "b_merge")
    d_w12 = _mm(sv["yg"], dg12, mode="tn", M=C_WIDTH, N=2 * C_WIDTH, K=L, tm=C_WIDTH, out_dtypes=[F32], name="b_dw_glu",
                o_stack=stk["w12"])
    dy = _mm(dg12, lp["w12"], mode="nt", M=L, N=C_WIDTH, K=2 * C_WIDTH, tk=2 * C_WIDTH, extras=[(sv["y"], 0)],
             epi=lambda acc, y_: (acc * _gelu_grad(y_),), out_dtypes=[F32], name="b_dx_glu")
    dd = _dskip_bwd(dy, z, "b_dskip")
    dS = _mm(dy, lp["ccat"], mode="nt", M=L, N=2 * N_STATE, K=C_WIDTH, tk=C_WIDTH, out_dtypes=[F32], name="b_dx_ssm_out")
    d_ccat = _mm(sv["S"], dy, mode="tn", M=2 * N_STATE, N=C_WIDTH, K=L, out_dtypes=[F32], name="b_dw_ssm_out")
    G, dar, dai = _ssm_scan(dS, lp["lam_r"], lp["lam_i"], "b_ssm_scan", reverse=True, states=sv["S"])
    d_bcat = _mm(z, G, mode="tn", M=C_WIDTH, N=2 * N_STATE, K=L, a_off=5, tm=C_WIDTH, out_dtypes=[F32], name="b_dw_ssm_in")
    dzc = _mm(G, lp["bcat"], mode="nt", M=L, N=C_WIDTH, K=2 * N_STATE, tk=2 * N_STATE,
              extras=[(dy, 0), (lp["dskip"], 0)], epi=lambda acc, dy_, dsk: (acc + dy_ * dsk,),
              out_dtypes=[_ACT], name="b_dx_ssm_in")
    dq, dkc, dkp, dvc, dvp, dsink = _attn_bwd(sv["q"], sv["k"], z, lp["sinks"], dyb, "b_attn")
    dzq, dzk, dzv, dgq, dgk = _qk_prep_bwd(z, dq, dkc, dkp, dvc, dvp, cos, sin, lp["gq"], lp["gk"], "b_qk_prep")
    dza, dws, dbs, dlg, dlb = _gmlp_bwd(z, dya, lp["ws"], lp["wsT"], lp["bfull"], lp["lgf"], lp["lbf"], "b_gmlp")
    dz = jnp.concatenate([dza, dzq, dzk, dzv, dzc], axis=1)
    d_in = _mm(sv["xn"], dz, mode="tn", M=D_MODEL, N=IN_COLS, K=L, o_cb=True, out_dtypes=[F32], name="b_dw_in",
               o_stack=stk["w_in"])
    dh, dg_attn = _mm(dz, lp["w_in"], mode="nt", M=L, N=D_MODEL, K=IN_COLS, b_cb=True,
                      extras=[(sv["h"], 0), (dh1, 0), (row("attn_norm_g"), 0)], epi=_rms_bwd_epi,
                      out_dtypes=[F32, F32], n_acc=1, name="b_dx_in")
    grads = dict(w_in=d_in, w12=d_w12, w_out=d_out, w_ff1=d_ff1, w_ff2=d_ff2, w_ple_gate=d_gate, w_ple_proj=d_proj,
                 attn_norm_g=dg_attn.reshape(D_MODEL), mlp_norm_g=dg_mlp.reshape(D_MODEL),
                 ple_norm_g=dg_ple.reshape(D_MODEL), mix_out_g=dmix.reshape(D_MODEL),
                 dws=dws, dbs=dbs, dlg=dlg, dlb=dlb, dgq=dgq, dgk=dgk, dsink=dsink,
                 dar=dar, dai=dai, d_bcat=d_bcat, d_ccat=d_ccat, dd=dd)
    return dh, grads


SMALL = ("attn_norm_g", "gmlp_ln_g", "gmlp_ln_b", "gmlp_ws", "gmlp_bs", "q_norm_g", "k_norm_g", "sinks",
         "ssm_a_re", "ssm_a_im", "ssm_log_dt", "ssm_b_re", "ssm_b_im", "ssm_c_re", "ssm_c_im", "ssm_d",
         "mix_out_g", "mlp_norm_g", "ple_norm_g")
BIG = ("w_in", "w12", "w_out", "w_ff1", "w_ff2", "w_ple_gate", "w_ple_proj")
COL_SHARDED = ("w_in", "w_ff1", "w_ple_proj")


def _block_diag(t):
    nl, g, a, b = t.shape
    eye = jnp.eye(g, dtype=t.dtype)
    return (t[:, :, :, None, :] * eye[None, :, None, :, None]).reshape(nl, g * a, g * b)


def _diag_blocks(t, a, b):
    nl = t.shape[0]
    t = t.reshape(nl, C_GROUPS, a, C_GROUPS, b)
    idx = jnp.arange(C_GROUPS)
    return jnp.moveaxis(t[:, idx, :, idx, :], 0, 1)


def _local_step(x, p, positions, target, sw, bw):
    nl = sw["attn_norm_g"].shape[0]
    G = nl * C_GROUPS
    zeros = lambda *s: jnp.zeros(s, F32)
    are = sw["ssm_a_re"].reshape(G, 1, C_STATE)
    aim = sw["ssm_a_im"].reshape(G, 1, C_STATE)
    ldt = jnp.broadcast_to(sw["ssm_log_dt"][..., None], (nl, C_GROUPS, C_STATE)).reshape(G, 1, C_STATE)
    bre = jnp.swapaxes(sw["ssm_b_re"], -1, -2).reshape(G, C_GROUP, C_STATE)
    bim = jnp.swapaxes(sw["ssm_b_im"], -1, -2).reshape(G, C_GROUP, C_STATE)
    lr, li, bbr, bbi = _ssm_prep(are, aim, ldt, bre, bim)
    unflat = lambda t: t.reshape(nl, C_GROUPS, C_GROUP, C_STATE)
    lp = dict(
        attn_norm_g=sw["attn_norm_g"], mlp_norm_g=sw["mlp_norm_g"], ple_norm_g=sw["ple_norm_g"],
        mix_out_g=sw["mix_out_g"], sinks=sw["sinks"],
        ws=sw["gmlp_ws"], wsT=jnp.swapaxes(sw["gmlp_ws"], -1, -2),
        bfull=jnp.concatenate([zeros(nl, A_HEADS, CHUNK, HEAD_DIM),
                               jnp.broadcast_to(sw["gmlp_bs"][..., None], (nl, A_HEADS, CHUNK, HEAD_DIM))], axis=-1),
        lgf=jnp.concatenate([zeros(nl, A_HEADS, HEAD_DIM), sw["gmlp_ln_g"]], axis=-1),
        lbf=jnp.concatenate([zeros(nl, A_HEADS, HEAD_DIM), sw["gmlp_ln_b"]], axis=-1),
        gq=jnp.tile(sw["q_norm_g"], (1, 2)).reshape(nl, 1, LANES),
        gk=jnp.tile(sw["k_norm_g"], (1, 2)).reshape(nl, 1, LANES),
        lam_r=lr.reshape(nl, 1, N_STATE), lam_i=li.reshape(nl, 1, N_STATE),
        bcat=jnp.concatenate([_block_diag(unflat(bbr)), _block_diag(unflat(bbi))], axis=-1),
        ccat=jnp.concatenate([_block_diag(jnp.swapaxes(sw["ssm_c_re"], -1, -2)),
                              -_block_diag(jnp.swapaxes(sw["ssm_c_im"], -1, -2))], axis=1),
        dskip=sw["ssm_d"].reshape(nl, 1, C_WIDTH))
    cos, sin = _rope_tables(positions)

    def layer_params(l, hooks):
        lpi = {n: v[l] for n, v in lp.items()}
        lpi.update(bw.layer(l))
        lpi["p"] = (p, l)
        lpi.update(hooks)
        return lpi

    h, saved = x, []
    xn = _rms_fwd(x, sw["attn_norm_g"][0], "f_norm_attn")
    for l in range(nl):
        g_next = sw["attn_norm_g"][l + 1] if l + 1 < nl else None
        h, xn, sv = _layer_fwd(h, xn, layer_params(l, bw.fwd_hooks(l)), cos, sin, g_next)
        saved.append(sv)
    sse, dh = _loss_fwd_bwd(h, target)

    per_layer = [None] * nl
    for l in reversed(range(nl)):
        dh, gl = _layer_bwd(dh, layer_params(l, bw.bwd_hooks(l)), saved[l], cos, sin)
        bw.grads(l, {n: gl.pop(n) for n in BIG})
        per_layer[l] = gl
    grad_x = dh
    g = {n: jnp.stack([per_layer[l][n] for l in range(nl)]) for n in per_layer[0]}

    d_bcat = g["d_bcat"]
    dbr = _diag_blocks(d_bcat[:, :, :N_STATE], C_GROUP, C_STATE).reshape(G, C_GROUP, C_STATE)
    dbi = _diag_blocks(d_bcat[:, :, N_STATE:], C_GROUP, C_STATE).reshape(G, C_GROUP, C_STATE)
    dlr = g["dar"][:, 0].reshape(G, 1, C_STATE)
    dli = g["dai"][:, 0].reshape(G, 1, C_STATE)
    g_are, g_aim, g_ldt, g_bre, g_bim = _ssm_prep_bwd(are, aim, ldt, bre, bim, dlr, dli, dbr, dbi)
    d_ccat = g["d_ccat"]
    sg = dict(
        attn_norm_g=g["attn_norm_g"], mlp_norm_g=g["mlp_norm_g"], ple_norm_g=g["ple_norm_g"], mix_out_g=g["mix_out_g"],
        gmlp_ln_g=g["dlg"][:, :, 0, HEAD_DIM:], gmlp_ln_b=g["dlb"][:, :, 0, HEAD_DIM:],
        gmlp_ws=g["dws"], gmlp_bs=g["dbs"][:, :, :, HEAD_DIM],
        q_norm_g=g["dgq"][:, 0, :HEAD_DIM], k_norm_g=g["dgk"][:, 0, :HEAD_DIM],
        sinks=g["dsink"][:, 0, :B_Q_HEADS],
        ssm_a_re=g_are.reshape(nl, C_GROUPS, C_STATE), ssm_a_im=g_aim.reshape(nl, C_GROUPS, C_STATE),
        ssm_log_dt=g_ldt[:, 0, 0].reshape(nl, C_GROUPS),
        ssm_b_re=jnp.swapaxes(g_bre.reshape(nl, C_GROUPS, C_GROUP, C_STATE), -1, -2),
        ssm_b_im=jnp.swapaxes(g_bim.reshape(nl, C_GROUPS, C_GROUP, C_STATE), -1, -2),
        ssm_c_re=jnp.swapaxes(_diag_blocks(d_ccat[:, :N_STATE], C_STATE, C_GROUP), -1, -2),
        ssm_c_im=-jnp.swapaxes(_diag_blocks(d_ccat[:, N_STATE:], C_STATE, C_GROUP), -1, -2),
        ssm_d=g["dd"].reshape(nl, C_GROUPS, C_GROUP),
    )
    return sse[0, 0], grad_x, sg, bw.finish()


_ANY = pl.BlockSpec(memory_space=pl.ANY)
N_LAYERS = 4


def _mesh_pos():
    x, y, c = lax.axis_index("x"), lax.axis_index("y"), lax.axis_index("c")
    chips = [(1 - x, y), (x, 1 - y), (1 - x, 1 - y)]
    return x, y, c, 2 * x + y, chips


def _cast_into_slot(ws, j, name):
    nl, R, _ = ws[0].shape
    widths = [w.shape[2] for w in ws]
    C = sum(widths)
    tr = R if R <= 512 else _tile_rows(R, 512)
    nw = len(ws)

    def body(s_ref, *refs):
        o_ref = refs[nw]
        off = 0
        for r, wd in zip(refs[:nw], widths):
            o_ref[:, off:off + wd] = r[...].astype(o_ref.dtype)
            off += wd

    return pl.pallas_call(
        body, name=name,
        grid_spec=pltpu.PrefetchScalarGridSpec(
            num_scalar_prefetch=1, grid=(nl, R // tr),
            in_specs=[pl.BlockSpec((None, tr, wd), lambda l, i, s: (l, i, 0)) for wd in widths],
            out_specs=pl.BlockSpec((None, None, tr, C), lambda l, i, s: (l, s[0], i, 0))),
        out_shape=_sds((nl, N_CHIPS, R, C), _MXU),
    )(jnp.reshape(j, (1,)).astype(jnp.int32), *ws)


def _gather_weights(bufs):
    nk = len(bufs)

    def body(*refs):
        ins, outs = refs[:nk], refs[nk:2 * nk]
        send_sems, recv_sems = refs[2 * nk:]
        x, y, c, j, chips = _mesh_pos()
        mine, other = pl.ds(2 * c, 2), pl.ds(2 * (1 - c), 2)

        def ici(t, q):
            cx, cy = chips[q]
            return pltpu.make_async_remote_copy(
                src_ref=ins[t].at[mine, j], dst_ref=outs[t].at[mine, j],
                send_sem=send_sems.at[6 * t + q], recv_sem=recv_sems.at[6 * t + q],
                device_id=(cx, cy, c), device_id_type=MESH)

        def landed(t, q):
            cx, cy = chips[q]
            blk = outs[t].at[mine, 2 * cx + cy]
            return pltpu.make_async_remote_copy(
                src_ref=blk, dst_ref=blk, send_sem=send_sems.at[6 * t + q], recv_sem=recv_sems.at[6 * t + q],
                device_id=(cx, cy, c), device_id_type=MESH)

        def fwd(t, q, rows):
            cx, cy = chips[q]
            blk = outs[t].at[rows, 2 * cx + cy]
            return pltpu.make_async_remote_copy(
                src_ref=blk, dst_ref=blk, send_sem=send_sems.at[6 * t + 3 + q], recv_sem=recv_sems.at[6 * t + 3 + q],
                device_id=(x, y, 1 - c), device_id_type=MESH)

        for t in range(nk):
            for q in range(3):
                ici(t, q).start()
        for t in range(nk):
            for q in range(3):
                landed(t, q).wait_recv()
                fwd(t, q, mine).start()
        for t in range(nk):
            for q in range(3):
                fwd(t, q, other).wait_recv()
        for t in range(nk):
            for q in range(3):
                ici(t, q).wait_send()
                fwd(t, q, mine).wait_send()

    return pl.pallas_call(
        body, name="gather_weights", in_specs=[_ANY] * nk, out_specs=[_ANY] * nk,
        out_shape=[_sds(b.shape, b.dtype) for b in bufs],
        input_output_aliases={t: t for t in range(nk)},
        scratch_shapes=[pltpu.SemaphoreType.DMA((6 * nk,)), pltpu.SemaphoreType.DMA((6 * nk,))],
    )(*bufs)


def _exchange_sibling_half(gl):
    nk = len(gl)

    def body(*refs):
        ins, outs = refs[:nk], refs[nk:2 * nk]
        send_sems, recv_sems = refs[2 * nk:]
        x, y, c, _, _ = _mesh_pos()
        cps = [pltpu.make_async_remote_copy(
            src_ref=ins[t].at[pl.ds(2 * (1 - c), 2)], dst_ref=outs[t],
            send_sem=send_sems.at[t], recv_sem=recv_sems.at[t],
            device_id=(x, y, 1 - c), device_id_type=MESH) for t in range(nk)]
        for cp in cps:
            cp.start()
        for cp in cps:
            cp.wait()

    return pl.pallas_call(
        body, name="reduce_sibling", in_specs=[_ANY] * nk, out_specs=[_ANY] * nk,
        out_shape=[_sds((2,) + g.shape[1:], g.dtype) for g in gl],
        scratch_shapes=[pltpu.SemaphoreType.DMA((nk,)), pltpu.SemaphoreType.DMA((nk,))],
    )(*gl)


def _exchange_chips(ps):
    nk = len(ps)

    def body(*refs):
        ins, outs = refs[:nk], refs[nk:2 * nk]
        send_sems, recv_sems = refs[2 * nk:]
        x, y, c, j, chips = _mesh_pos()

        def send(t, q):
            cx, cy = chips[q]
            return pltpu.make_async_remote_copy(
                src_ref=ins[t].at[:, 2 * cx + cy], dst_ref=outs[t].at[j],
                send_sem=send_sems.at[3 * t + q], recv_sem=recv_sems.at[3 * t + q],
                device_id=(cx, cy, c), device_id_type=MESH)

        def landed(t, q):
            cx, cy = chips[q]
            blk = outs[t].at[2 * cx + cy]
            return pltpu.make_async_remote_copy(
                src_ref=blk, dst_ref=blk, send_sem=send_sems.at[3 * t + q], recv_sem=recv_sems.at[3 * t + q],
                device_id=(cx, cy, c), device_id_type=MESH)

        for t in range(nk):
            for q in range(3):
                send(t, q).start()
        for t in range(nk):
            for q in range(3):
                landed(t, q).wait_recv()
        for t in range(nk):
            for q in range(3):
                send(t, q).wait_send()

    return pl.pallas_call(
        body, name="reduce_chips", in_specs=[_ANY] * nk, out_specs=[_ANY] * nk,
        out_shape=[_sds((N_CHIPS, 2) + p.shape[2:], p.dtype) for p in ps],
        scratch_shapes=[pltpu.SemaphoreType.DMA((3 * nk,)), pltpu.SemaphoreType.DMA((3 * nk,))],
    )(*ps)


def _share_sibling(fs):
    nk = len(fs)

    def body(*refs):
        ins, outs = refs[:nk], refs[nk:2 * nk]
        send_sems, recv_sems = refs[2 * nk:]
        x, y, c, _, _ = _mesh_pos()
        mine = pl.ds(2 * c, 2)
        cps = [pltpu.make_async_remote_copy(
            src_ref=ins[t].at[mine], dst_ref=outs[t].at[mine], send_sem=send_sems.at[t], recv_sem=recv_sems.at[t],
            device_id=(x, y, 1 - c), device_id_type=MESH) for t in range(nk)]
        for cp in cps:
            cp.start()
        for cp in cps:
            cp.wait_send()
        for t in range(nk):
            blk = outs[t].at[pl.ds(2 * (1 - c), 2)]
            pltpu.make_async_remote_copy(
                src_ref=blk, dst_ref=blk, send_sem=send_sems.at[t], recv_sem=recv_sems.at[t],
                device_id=(x, y, 1 - c), device_id_type=MESH).wait_recv()

    return pl.pallas_call(
        body, name="share_sibling", in_specs=[_ANY] * nk, out_specs=[_ANY] * nk,
        out_shape=[_sds(f.shape, f.dtype) for f in fs],
        input_output_aliases={t: t for t in range(nk)},
        scratch_shapes=[pltpu.SemaphoreType.DMA((nk,)), pltpu.SemaphoreType.DMA((nk,))],
    )(*fs)


def _add_own_half(gl, r1, c, name):
    _, ns, R, C = gl.shape
    rows = 2 * ns * R
    tr = _tile_rows(rows, 512)
    nblk = rows // tr

    def body(s_ref, a_ref, b_ref, o_ref):
        o_ref[...] = (a_ref[...] + b_ref[...]).astype(o_ref.dtype)

    out = pl.pallas_call(
        body, name=name,
        grid_spec=pltpu.PrefetchScalarGridSpec(
            num_scalar_prefetch=1, grid=(nblk,),
            in_specs=[pl.BlockSpec((tr, C), lambda i, s: (s[0] * nblk + i, 0)), pl.BlockSpec((tr, C), lambda i, s: (i, 0))],
            out_specs=pl.BlockSpec((tr, C), lambda i, s: (i, 0))),
        out_shape=_sds((rows, C), _WIRE),
        compiler_params=_cparams(3 * _nbytes((tr, C), F32)),
    )(jnp.reshape(c, (1,)).astype(jnp.int32), gl.reshape(2 * rows, C), r1.reshape(rows, C))
    return out.reshape(2, ns, R, C)


def _add_chips(p, r2, j, c, name):
    _, ns, R, C = p.shape
    tr = R if R <= 512 else _tile_rows(R, 512)

    def body(s_ref, own, a1, a2, a3, o_ref):
        f = lambda r: r[...].astype(F32)
        o_ref[...] = ((f(own) + f(a1)) + f(a2)) + f(a3)

    blk = (None, None, tr, C)
    return pl.pallas_call(
        body, name=name,
        grid_spec=pltpu.PrefetchScalarGridSpec(
            num_scalar_prefetch=1, grid=(2, R // tr),
            in_specs=[pl.BlockSpec(blk, lambda h, i, s: (h, s[0], i, 0))]
            + [pl.BlockSpec(blk, lambda h, i, s, k=k: ((s[0] + k) % N_CHIPS, h, i, 0)) for k in (1, 2, 3)],
            out_specs=pl.BlockSpec((None, tr, C), lambda h, i, s: (2 * s[1] + h, i, 0))),
        out_shape=_sds((N_LAYERS, R, C), F32),
        compiler_params=_cparams(6 * _nbytes((tr, C), F32)),
    )(jnp.stack([j, c]).astype(jnp.int32), p, r2, r2, r2)


def _allreduce_small(buf):
    Rs = buf.shape[0]

    def body(b_ref, o_ref, t_ref, slots_ref, send_sems, recv_sems):
        x, y, c, j, chips = _mesh_pos()
        sib = pltpu.make_async_remote_copy(
            src_ref=b_ref, dst_ref=t_ref, send_sem=send_sems.at[0], recv_sem=recv_sems.at[0],
            device_id=(x, y, 1 - c), device_id_type=MESH)
        sib.start()
        sib.wait()
        slots_ref[j] = b_ref[...] + t_ref[...]

        def send(q):
            cx, cy = chips[q]
            return pltpu.make_async_remote_copy(
                src_ref=slots_ref.at[j], dst_ref=slots_ref.at[j], send_sem=send_sems.at[1 + q],
                recv_sem=recv_sems.at[1 + q], device_id=(cx, cy, c), device_id_type=MESH)

        def landed(q):
            cx, cy = chips[q]
            blk = slots_ref.at[2 * cx + cy]
            return pltpu.make_async_remote_copy(
                src_ref=blk, dst_ref=blk, send_sem=send_sems.at[1 + q], recv_sem=recv_sems.at[1 + q],
                device_id=(cx, cy, c), device_id_type=MESH)

        for q in range(3):
            send(q).start()
        for q in range(3):
            landed(q).wait_recv()
        for q in range(3):
            send(q).wait_send()
        o_ref[...] = ((slots_ref[0] + slots_ref[1]) + slots_ref[2]) + slots_ref[3]

    vm = pl.BlockSpec(memory_space=pltpu.VMEM)
    return pl.pallas_call(
        body, name="allreduce_small", in_specs=[vm], out_specs=vm,
        out_shape=_sds((Rs, LANES), F32),
        scratch_shapes=[pltpu.VMEM((Rs, LANES), F32), pltpu.VMEM((N_CHIPS, Rs, LANES), F32),
                        pltpu.SemaphoreType.DMA((4,)), pltpu.SemaphoreType.DMA((4,))],
        compiler_params=_cparams(4 * _nbytes((Rs, LANES), F32)),
    )(buf)


def _own_rows(c, R):
    return pl.ds(c * (R // 2), R // 2)


def _cast_layer_slot(ws, l, j, name):
    _, R, _ = ws[0].shape
    widths = [w.shape[2] for w in ws]
    C = sum(widths)
    tr = R if R <= 512 else _tile_rows(R, 512)
    nw = len(ws)

    def body(s_ref, *refs):
        o_ref = refs[nw]
        off = 0
        for r, wd in zip(refs[:nw], widths):
            o_ref[:, off:off + wd] = r[...].astype(o_ref.dtype)
            off += wd

    return pl.pallas_call(
        body, name=name,
        grid_spec=pltpu.PrefetchScalarGridSpec(
            num_scalar_prefetch=1, grid=(R // tr,),
            in_specs=[pl.BlockSpec((None, tr, wd), lambda i, s: (l, i, 0)) for wd in widths],
            out_specs=pl.BlockSpec((None, tr, C), lambda i, s: (s[0], i, 0))),
        out_shape=_sds((N_CHIPS, R, C), _MXU),
    )(jnp.reshape(j, (1,)).astype(jnp.int32), *ws)


def _gather_ici(bufs, done):
    nk = len(bufs)

    def copy(ins, outs, ss, rs, t, q, landed):
        x, y, c, j, chips = _mesh_pos()
        cx, cy = chips[q]
        rows = _own_rows(c, ins[t].shape[1])
        src = outs[t].at[2 * cx + cy, rows] if landed else ins[t].at[j, rows]
        dst = outs[t].at[2 * cx + cy, rows] if landed else outs[t].at[j, rows]
        return pltpu.make_async_remote_copy(src_ref=src, dst_ref=dst, send_sem=ss.at[3 * t + q], recv_sem=rs.at[3 * t + q],
                                            device_id=(cx, cy, c), device_id_type=MESH)

    def start(ins, outs, ss, rs):
        for t in range(nk):
            for q in range(3):
                copy(ins, outs, ss, rs, t, q, False).start()

    def wait(ins, outs, ss, rs):
        for t in range(nk):
            for q in range(3):
                copy(ins, outs, ss, rs, t, q, True).wait_recv()
                copy(ins, outs, ss, rs, t, q, False).wait_send()

    return _Exchange(bufs, [True] * nk, [], 3 * nk, start, wait, done)


def _gather_d2d(bufs, done):
    nk = len(bufs)

    def copy(ins, outs, ss, rs, t, q, mine):
        x, y, c, j, chips = _mesh_pos()
        cx, cy = chips[q]
        rows = _own_rows(c if mine else 1 - c, ins[t].shape[1])
        src = (ins if mine else outs)[t].at[2 * cx + cy, rows]
        return pltpu.make_async_remote_copy(src_ref=src, dst_ref=outs[t].at[2 * cx + cy, rows],
                                            send_sem=ss.at[3 * t + q], recv_sem=rs.at[3 * t + q],
                                            device_id=(x, y, 1 - c), device_id_type=MESH)

    def start(ins, outs, ss, rs):
        for t in range(nk):
            for q in range(3):
                copy(ins, outs, ss, rs, t, q, True).start()

    def wait(ins, outs, ss, rs):
        for t in range(nk):
            for q in range(3):
                copy(ins, outs, ss, rs, t, q, False).wait_recv()
                copy(ins, outs, ss, rs, t, q, True).wait_send()

    return _Exchange(bufs, [True] * nk, [], 3 * nk, start, wait, done)


def _reduce_d2d(gl, done):
    nk = len(gl)

    def copy(ins, outs, ss, rs, t):
        x, y, c, _, _ = _mesh_pos()
        return pltpu.make_async_remote_copy(
            src_ref=ins[t].at[:, _own_rows(1 - c, ins[t].shape[1])], dst_ref=outs[t],
            send_sem=ss.at[t], recv_sem=rs.at[t], device_id=(x, y, 1 - c), device_id_type=MESH)

    def start(ins, outs, ss, rs):
        for t in range(nk):
            copy(ins, outs, ss, rs, t).start()

    def wait(ins, outs, ss, rs):
        for t in range(nk):
            copy(ins, outs, ss, rs, t).wait()

    fresh = [((N_CHIPS, g.shape[1] // 2, g.shape[2]), g.dtype) for g in gl]
    return _Exchange(gl, [False] * nk, fresh, nk, start, wait, done)


def _reduce_ici(ps, done):
    nk = len(ps)

    def copy(ins, outs, ss, rs, t, q, landed):
        x, y, c, j, chips = _mesh_pos()
        cx, cy = chips[q]
        src = outs[t].at[2 * cx + cy] if landed else ins[t].at[2 * cx + cy]
        dst = outs[t].at[2 * cx + cy] if landed else outs[t].at[j]
        return pltpu.make_async_remote_copy(src_ref=src, dst_ref=dst, send_sem=ss.at[3 * t + q], recv_sem=rs.at[3 * t + q],
                                            device_id=(cx, cy, c), device_id_type=MESH)

    def start(ins, outs, ss, rs):
        for t in range(nk):
            for q in range(3):
                copy(ins, outs, ss, rs, t, q, False).start()

    def wait(ins, outs, ss, rs):
        for t in range(nk):
            for q in range(3):
                copy(ins, outs, ss, rs, t, q, True).wait_recv()
                copy(ins, outs, ss, rs, t, q, False).wait_send()

    return _Exchange(ps, [False] * nk, [(p_.shape, p_.dtype) for p_ in ps], 3 * nk, start, wait, done)


def _share_d2d(fs, done):
    nk = len(fs)

    def copy(ins, outs, ss, rs, t, mine):
        x, y, c, _, _ = _mesh_pos()
        rows = _own_rows(c if mine else 1 - c, ins[t].shape[1])
        src = (ins if mine else outs)[t].at[:, rows]
        return pltpu.make_async_remote_copy(src_ref=src, dst_ref=outs[t].at[:, rows], send_sem=ss.at[t], recv_sem=rs.at[t],
                                            device_id=(x, y, 1 - c), device_id_type=MESH)

    def start(ins, outs, ss, rs):
        for t in range(nk):
            copy(ins, outs, ss, rs, t, True).start()

    def wait(ins, outs, ss, rs):
        for t in range(nk):
            copy(ins, outs, ss, rs, t, False).wait_recv()
            copy(ins, outs, ss, rs, t, True).wait_send()

    return _Exchange(fs, [True] * nk, [], nk, start, wait, done)


def _run_exchange(plan, name):
    nin = len(plan.ins)
    out_shape = [_sds(x_.shape, x_.dtype) for x_, al in zip(plan.ins, plan.aliased) if al]
    aliases, k = {}, 0
    for t, al in enumerate(plan.aliased):
        if al:
            aliases[t] = k
            k += 1
    out_shape += [_sds(sh, dt) for sh, dt in plan.fresh]
    nout = len(out_shape)

    def body(*refs):
        ins, outs, sems = refs[:nin], refs[nin:nin + nout], refs[nin + nout:]
        plan.start(ins, outs, *sems)
        plan.wait(ins, outs, *sems)

    outs = pl.pallas_call(
        body, name=name, in_specs=[_ANY] * nin, out_specs=[_ANY] * nout, out_shape=out_shape,
        input_output_aliases=aliases,
        scratch_shapes=[pltpu.SemaphoreType.DMA((plan.n_sems,))] * 2,
    )(*plan.ins)
    plan.done(list(outs))


def _add_sibling_rows(g, r1, c, name):
    ns, R, C = g.shape
    hr = R // 2
    tr = hr if hr <= 512 else _tile_rows(hr, 512)
    nblk = hr // tr

    def body(s_ref, a_ref, b_ref, o_ref):
        o_ref[...] = (a_ref[...] + b_ref[...]).astype(o_ref.dtype)

    blk = (None, tr, C)
    return pl.pallas_call(
        body, name=name,
        grid_spec=pltpu.PrefetchScalarGridSpec(
            num_scalar_prefetch=1, grid=(ns, nblk),
            in_specs=[pl.BlockSpec(blk, lambda s_, i, s: (s_, s[0] * nblk + i, 0)), pl.BlockSpec(blk, lambda s_, i, s: (s_, i, 0))],
            out_specs=pl.BlockSpec(blk, lambda s_, i, s: (s_, i, 0))),
        out_shape=_sds((ns, hr, C), _WIRE),
        compiler_params=_cparams(3 * _nbytes((tr, C), F32)),
    )(jnp.reshape(c, (1,)).astype(jnp.int32), g, r1)


def _add_chip_rows(p_, r2, f, l, j, c, name):
    _, hr, C = p_.shape
    tr = hr if hr <= 512 else _tile_rows(hr, 512)
    nblk = hr // tr

    def body(s_ref, own, a1, a2, a3, f_ref, o_ref):
        v = lambda r: r[...].astype(F32)
        o_ref[...] = ((v(own) + v(a1)) + v(a2)) + v(a3)

    blk = (None, tr, C)
    return pl.pallas_call(
        body, name=name,
        grid_spec=pltpu.PrefetchScalarGridSpec(
            num_scalar_prefetch=1, grid=(nblk,),
            in_specs=[pl.BlockSpec(blk, lambda i, s: (s[0], i, 0))]
            + [pl.BlockSpec(blk, lambda i, s, k=k: ((s[0] + k) % N_CHIPS, i, 0)) for k in (1, 2, 3)]
            + [pl.BlockSpec(memory_space=pl.ANY)],
            out_specs=pl.BlockSpec(blk, lambda i, s: (l, s[1] * nblk + i, 0))),
        out_shape=_sds(f.shape, F32),
        input_output_aliases={5: 0},
        compiler_params=_cparams(6 * _nbytes((tr, C), F32)),
    )(jnp.stack([j, c]).astype(jnp.int32), p_, r2, r2, r2, f)


class _ShardedWeights:
    def __init__(self, a, j, c):
        self.j, self.c = j, c
        shards = dict(w_in=[a["w_in"]], w12=[a["glu_w1"], a["glu_w2"]], w_out=[a["w_out"]], w_ff1=[a["w_ff1"]],
                      w_ff2=[a["w_ff2"]], w_ple_gate=[a["w_ple_gate"]], w_ple_proj=[a["w_ple_proj"]])
        self.bufs = [[_cast_layer_slot(shards[n], l, j, "cast_%s_%d" % (n, l)) for n in BIG] for l in range(N_LAYERS)]
        _run_exchange(_gather_ici(self.bufs[0], lambda o: self._set_bufs(0, o)), "gather_ici_0")
        _run_exchange(_gather_d2d(self.bufs[0], lambda o: self._set_bufs(0, o)), "gather_d2d_0")
        self.pending = None
        self.final = [lax.empty((N_LAYERS,) + b.shape[1:], F32) for b in self.bufs[0]]

    def _set_bufs(self, l, outs):
        self.bufs[l] = outs

    def layer(self, l):
        return {n: (b if n in COL_SHARDED else b.reshape(N_CHIPS * b.shape[1], b.shape[2]))
                for n, b in zip(BIG, self.bufs[l])}

    def fwd_hooks(self, l):
        if l + 1 == N_LAYERS:
            return {}
        nxt = l + 1
        return dict(x_ff2=lambda: _gather_ici(self.bufs[nxt], lambda o: self._set_bufs(nxt, o)),
                    x_gate=lambda: _gather_d2d(self.bufs[nxt], lambda o: self._set_bufs(nxt, o)))

    def bwd_hooks(self, l):
        if self.pending is None:
            return {}
        lyr, ps = self.pending
        return dict(x_bwd=_reduce_ici(ps, lambda o: self._landed(lyr, ps, o)))

    def _landed(self, lyr, ps, r2):
        self.pending = None
        self.final = [_add_chip_rows(p_, r, f, lyr, self.j, self.c, "reduce_add_chips_%s_%d" % (n, lyr))
                      for p_, r, f, n in zip(ps, r2, self.final, BIG)]

    def grads(self, l, g):
        gl = [g[n] if n in COL_SHARDED else g[n].reshape(N_CHIPS, g[n].shape[0] // N_CHIPS, g[n].shape[1]) for n in BIG]
        got = []
        _run_exchange(_reduce_d2d(gl, got.extend), "reduce_d2d_%d" % l)
        ps = [_add_sibling_rows(g_, r1, self.c, "reduce_add_sibling_%s_%d" % (n, l)) for g_, r1, n in zip(gl, got, BIG)]
        self.pending = (l, ps)

    def finish(self):
        lyr, ps = self.pending
        _run_exchange(_reduce_ici(ps, lambda o: self._landed(lyr, ps, o)), "reduce_ici_%d" % lyr)
        out = []
        _run_exchange(_share_d2d(self.final, out.extend), "share_d2d")
        return dict(zip(BIG, out))


def _rows_of(shape):
    return -(-int(np.prod(shape)) // (SUBLANES * LANES)) * SUBLANES


def _pack(d):
    parts = []
    for n in SMALL:
        flat = d[n].reshape(-1)
        parts.append(jnp.pad(flat, (0, _rows_of(flat.shape) * LANES - flat.shape[0])).reshape(-1, LANES))
    return jnp.concatenate(parts, axis=0)


def _unpack(buf, like):
    out, r0 = {}, 0
    for n in SMALL:
        shape = like[n].shape
        size, nr = int(np.prod(shape)), _rows_of(shape)
        piece = lax.optimization_barrier(buf[r0:r0 + nr])
        out[n] = piece.reshape(-1)[:size].reshape(shape)
        r0 += nr
    return out


ARGS = ("x", "p", "positions", "attn_norm_g", "w_in", "gmlp_ln_g", "gmlp_ln_b", "gmlp_ws", "gmlp_bs", "q_norm_g",
        "k_norm_g", "sinks", "ssm_a_re", "ssm_a_im", "ssm_log_dt", "ssm_b_re", "ssm_b_im", "ssm_c_re", "ssm_c_im",
        "ssm_d", "glu_w1", "glu_w2", "mix_out_g", "w_out", "mlp_norm_g", "w_ff1", "w_ff2", "ple_norm_g", "w_ple_gate",
        "w_ple_proj")
WEIGHTS = ARGS[3:]


def kernel(x, p, positions, attn_norm_g, w_in, gmlp_ln_g, gmlp_ln_b, gmlp_ws, gmlp_bs, q_norm_g, k_norm_g, sinks, ssm_a_re, ssm_a_im, ssm_log_dt, ssm_b_re, ssm_b_im, ssm_c_re, ssm_c_im, ssm_d, glu_w1, glu_w2, mix_out_g, w_out, mlp_norm_g, w_ff1, w_ff2, ple_norm_g, w_ple_gate, w_ple_proj, loss_target, m_attn_norm_g, m_w_in, m_gmlp_ln_g, m_gmlp_ln_b, m_gmlp_ws, m_gmlp_bs, m_q_norm_g, m_k_norm_g, m_sinks, m_ssm_a_re, m_ssm_a_im, m_ssm_log_dt, m_ssm_b_re, m_ssm_b_im, m_ssm_c_re, m_ssm_c_im, m_ssm_d, m_glu_w1, m_glu_w2, m_mix_out_g, m_w_out, m_mlp_norm_g, m_w_ff1, m_w_ff2, m_ple_norm_g, m_w_ple_gate, m_w_ple_proj, v_attn_norm_g, v_w_in, v_gmlp_ln_g, v_gmlp_ln_b, v_gmlp_ws, v_gmlp_bs, v_q_norm_g, v_k_norm_g, v_sinks, v_ssm_a_re, v_ssm_a_im, v_ssm_log_dt, v_ssm_b_re, v_ssm_b_im, v_ssm_c_re, v_ssm_c_im, v_ssm_d, v_glu_w1, v_glu_w2, v_mix_out_g, v_w_out, v_mlp_norm_g, v_w_ff1, v_w_ff2, v_ple_norm_g, v_w_ple_gate, v_w_ple_proj):
    a = dict(locals())
    L = a["x"].shape[1]
    nl = N_LAYERS
    c = lax.axis_index("c")
    j = 2 * lax.axis_index("x") + lax.axis_index("y")

    sw = {n: a[n] for n in SMALL}
    sse, gx, sg, big_grads = _local_step(a["x"].reshape(L, D_MODEL), a["p"].reshape(nl, L, PLE_DIM),
                                         a["positions"].reshape(L), a["loss_target"].reshape(L, D_MODEL), sw,
                                         _ShardedWeights(a, j, c))
    loss = lax.psum(sse * (0.5 / D_MODEL), ("x", "y", "c"))
    g12 = big_grads.pop("w12")
    big_grads["glu_w1"], big_grads["glu_w2"] = g12[:, :, :C_WIDTH], g12[:, :, C_WIDTH:]

    small_grads = _unpack(_allreduce_small(_pack(sg)), sw)

    grads, delta, new_m, new_v = {}, {}, {}, {}
    d_s, m_s, v_s = _adamw(_pack(sw), _pack(small_grads), _pack({n: a["m_" + n] for n in SMALL}),
                           _pack({n: a["v_" + n] for n in SMALL}), "adamw_small")
    grads.update(small_grads)
    delta.update(_unpack(d_s, sw))
    new_m.update(_unpack(m_s, sw))
    new_v.update(_unpack(v_s, sw))
    for n, g in big_grads.items():
        shp = a[n].shape
        two_d = lambda t: t.reshape(shp[0] * shp[1], shp[2])
        d, m, v = _adamw(two_d(a[n]), two_d(g), two_d(a["m_" + n]), two_d(a["v_" + n]), "adamw_" + n)
        grads[n], delta[n], new_m[n], new_v[n] = g, d.reshape(shp), m.reshape(shp), v.reshape(shp)

    return (loss, gx.reshape(1, L, D_MODEL), *[grads[n] for n in WEIGHTS], *[delta[n] for n in WEIGHTS],
            *[new_m[n] for n in WEIGHTS], *[new_v[n] for n in WEIGHTS])
```

```python
import functools
import math

import numpy as np
import jax
import jax.numpy as jnp
from jax import lax
from jax.experimental import pallas as pl
from jax.experimental.pallas import tpu as pltpu

F32 = jnp.float32
_MXU = jnp.bfloat16
_ACT = jnp.bfloat16
_WIRE = jnp.bfloat16

D_MODEL = 1024
HEAD_DIM = 64
A_HEADS = 4
CHUNK = 128
B_Q_HEADS = 8
B_KV_HEADS = 2
B_GROUP = 4
WINDOW = 128
ROPE_THETA = 10000.0
C_WIDTH = 256
C_GROUP = 16
C_GROUPS = 16
C_STATE = 64
N_STATE = C_GROUPS * C_STATE
IN_A, IN_Q, IN_KV, IN_C = 512, 512, 128, 256
IN_COLS = 1536
D_FF = 4096
PLE_DIM = 256
EPS = 1e-6
NEG = -1e30
ADAM_LR, ADAM_B1, ADAM_B2, ADAM_EPS, ADAM_WD, ADAM_STEP = 0.001, 0.9, 0.999, 1e-08, 0.01, 10

LANES = 128
SUBLANES = 8
VMEM_BYTES = 64 * 2 ** 20
N_CHIPS = 4
MESH = pl.DeviceIdType.MESH


_MM_VMEM_BUDGET = 44 * 2 ** 20


def _vmem_limit(est_bytes):
    return int(min(max(2 * est_bytes + (8 << 20), 32 << 20), VMEM_BYTES - (6 << 20)))


def _cparams(est_bytes, **kw):
    return pltpu.CompilerParams(vmem_limit_bytes=_vmem_limit(est_bytes), **kw)


def _sds(shape, dtype):
    return pltpu.HBM(tuple(shape), dtype)


def _hbm(x):
    return pltpu.with_memory_space_constraint(x, pltpu.HBM) if x.size >= (1 << 20) else x


def _nbytes(shape, dtype):
    return int(np.prod(shape)) * jnp.dtype(dtype).itemsize


def _tile(dim, pref):
    t = min(dim, pref)
    while dim % t:
        t -= LANES
    assert t > 0, (dim, pref)
    return t


def _lane(shape):
    return lax.broadcasted_iota(jnp.int32, shape, len(shape) - 1)


def _row(shape):
    return lax.broadcasted_iota(jnp.int32, shape, len(shape) - 2)


def _gelu(x):
    c = math.sqrt(2.0 / math.pi)
    return 0.5 * x * (1.0 + jnp.tanh(c * (x + 0.044715 * (x * x * x))))


def _gelu_grad(x):
    c = math.sqrt(2.0 / math.pi)
    t = jnp.tanh(c * (x + 0.044715 * (x * x * x)))
    return 0.5 * (1.0 + t) + 0.5 * x * (1.0 - t * t) * (c * (1.0 + 3.0 * 0.044715 * (x * x)))


def _sigmoid(x):
    return 1.0 / (1.0 + jnp.exp(-x))


def _dot(a, b, dims=(((1,), (0,)), ((), ()))):
    return lax.dot_general(a.astype(_MXU), b.astype(_MXU), dims, preferred_element_type=F32)


_NT = (((1,), (1,)), ((), ()))
_TN = (((0,), (0,)), ((), ()))
_NN = (((1,), (0,)), ((), ()))


def _mm(a, b, *, mode, M, N, K, out_dtypes, name, epi=None, extras=(), b_cb=False, o_cb=False,
        a_off=0, b_off=0, tm=1024, tn=1024, tk=1024, a_lyr=None, b_lyr=None, o_stack=None, n_acc=0, comm=None):
    if isinstance(a, tuple):
        a, a_lyr = a
    if isinstance(b, tuple):
        b, b_lyr = b
    if b_cb or o_cb:
        nc = (b.shape[-1] if b_cb else N // N_CHIPS)
    tn_nom = nc if ((mode == "nn" and b_cb) or (mode == "tn" and o_cb)) else _tile(N, tn)
    tk_nom = nc if (mode == "nt" and b_cb) else _tile(K, tk)
    item = lambda d: jnp.dtype(d).itemsize
    per_row = tk_nom * item(a.dtype) + tn_nom * (sum(item(d) for d in out_dtypes)
                                                   + sum(item(e.dtype) for e, _ in extras if e.shape[0] > 1))
    fixed = tk_nom * tn_nom * item(b.dtype)
    tm = _tile(M, tm)
    while tm > 256 and M % (tm // 2) == 0 and 2 * (tm * per_row + fixed) + 8 * tm * tn_nom > _MM_VMEM_BUDGET:
        tm //= 2

    def spec(block, imap, lyr=None):
        if lyr is None:
            return pl.BlockSpec(block, imap)
        return pl.BlockSpec((None,) + block, lambda i, j, k: (lyr,) + imap(i, j, k))

    if mode == "nn":
        if b_cb:
            tn = nc
        tm, tn, tk = _tile(M, tm), _tile(N, tn), _tile(K, tk)
        a_spec = spec((tm, tk), lambda i, j, k: (i, k + a_off), a_lyr)
        if b_cb:
            b_spec = spec((None, tk, tn), lambda i, j, k: (j, k, 0), b_lyr)
        else:
            b_spec = spec((tk, tn), lambda i, j, k: (k, j + b_off), b_lyr)
        dims = _NN
        a_blk, b_blk = (tm, tk), (tk, tn)
    elif mode == "nt":
        if b_cb:
            tk = nc
        tm, tn, tk = _tile(M, tm), _tile(N, tn), _tile(K, tk)
        a_spec = spec((tm, tk), lambda i, j, k: (i, k + a_off), a_lyr)
        if b_cb:
            b_spec = spec((None, tn, tk), lambda i, j, k: (k, j, 0), b_lyr)
        else:
            b_spec = spec((tn, tk), lambda i, j, k: (j, k + b_off), b_lyr)
        dims = _NT
        a_blk, b_blk = (tm, tk), (tn, tk)
    else:
        if o_cb:
            tn = nc
        tm, tn, tk = _tile(M, tm), _tile(N, tn), _tile(K, tk)
        a_spec = spec((tk, tm), lambda i, j, k: (k, i + a_off), a_lyr)
        b_spec = spec((tk, tn), lambda i, j, k: (k, j + b_off), b_lyr)
        dims = _TN
        a_blk, b_blk = (tk, tm), (tk, tn)
    gi, gj, gk = M // tm, N // tn, K // tk
    o_lyr = None if o_stack is None else o_stack[1]
    if o_cb:
        o_spec = spec((None, tm, tn), lambda i, j, k: (j, i, 0), o_lyr)
        o_shape = (gj, M, tn)
    else:
        o_spec = spec((tm, tn), lambda i, j, k: (i, j), o_lyr)
        o_shape = (M, N)
    e_specs = []
    for e, off in extras:
        if e.shape[0] == 1:
            e_specs.append(pl.BlockSpec((1, tn), lambda i, j, k, off=off: (0, j + off)))
        else:
            e_specs.append(pl.BlockSpec((tm, tn), lambda i, j, k, off=off: (i, j + off)))
    extras = [e for e, _ in extras]
    ne, no = len(extras), len(out_dtypes)
    operands = [_hbm(t) for t in (a, b, *extras)]
    in_specs = [a_spec, b_spec] + e_specs
    out_shape = [_sds(o_shape, d) for d in out_dtypes]
    aliases = {}
    if o_stack is not None:
        assert no == 1 and o_stack[0].shape[1:] == o_shape and o_stack[0].dtype == out_dtypes[0]
        operands.append(_hbm(o_stack[0]))
        in_specs.append(pl.BlockSpec(memory_space=pl.ANY))
        out_shape = [_sds(o_stack[0].shape, o_stack[0].dtype)]
        aliases = {len(operands) - 1: 0}
    out_specs = [o_spec] * no
    if n_acc:
        assert gj == 1 and o_stack is None
        out_specs[no - n_acc:] = [pl.BlockSpec((1, tn), lambda i, j, k: (0, 0))] * n_acc
        out_shape[no - n_acc:] = [_sds((1, N), d) for d in out_dtypes[no - n_acc:]]
    nx_in = nx_out = 0
    if comm is not None:
        nx_in, ncin0 = len(comm.ins), len(operands)
        operands += list(comm.ins)
        in_specs += [pl.BlockSpec(memory_space=pl.ANY)] * nx_in
        for t, x_ in enumerate(comm.ins):
            if comm.aliased[t]:
                aliases[ncin0 + t] = len(out_shape)
                out_shape.append(_sds(x_.shape, x_.dtype))
        out_shape += [_sds(sh, dt) for sh, dt in comm.fresh]
        nx_out = len(out_shape) - no
        out_specs += [pl.BlockSpec(memory_space=pl.ANY)] * nx_out
    nin = len(operands)

    def body(*refs):
        a_ref, b_ref = refs[0], refs[1]
        e_refs = refs[2:2 + ne]
        o_refs = refs[nin:nin + no]
        first_rows = pl.program_id(0) == 0
        if comm is not None:
            x_ins = refs[nin - nx_in:nin]
            x_outs = refs[nin + no:nin + no + nx_out]
            sems = refs[nin + no + nx_out:nin + no + nx_out + 2]
            pid = [pl.program_id(d) for d in range(3)]

            @pl.when((pid[0] == 0) & (pid[1] == 0) & (pid[2] == 0))
            def _():
                comm.start(x_ins, x_outs, *sems)

        def fin(acc):
            vals = epi(acc, *[e[...] for e in e_refs]) if epi is not None else (acc,)
            for t, (o, v) in enumerate(zip(o_refs, vals)):
                if t < no - n_acc:
                    o[...] = v.astype(o.dtype)
                else:
                    @pl.when(first_rows)
                    def _():
                        o[...] = jnp.zeros_like(o)

                    o[...] += v.astype(o.dtype)

        prod = _dot(a_ref[...], b_ref[...], dims)
        if gk == 1:
            fin(prod)
        else:
            acc_ref = refs[-1]
            k = pl.program_id(2)

            @pl.when(k == 0)
            def _():
                acc_ref[...] = prod

            @pl.when(k > 0)
            def _():
                acc_ref[...] += prod

            @pl.when(k == gk - 1)
            def _():
                fin(acc_ref[...])

        if comm is not None:
            @pl.when((pid[0] == gi - 1) & (pid[1] == gj - 1) & (pid[2] == gk - 1))
            def _():
                comm.wait(x_ins, x_outs, *sems)

    est = (_nbytes(a_blk, a.dtype) + _nbytes(b_blk, b.dtype)
           + sum(_nbytes((tm, tn), d) for d in out_dtypes)
           + sum(_nbytes((tm, tn), e.dtype) for e in extras)) + 2 * _nbytes((tm, tn), F32)
    sem_scratch = [pltpu.SemaphoreType.DMA((comm.n_sems,))] * 2 if comm is not None else []
    row_sem = "arbitrary" if (n_acc or comm is not None) else "parallel"
    outs = pl.pallas_call(
        body, name=name, grid=(gi, gj, gk),
        in_specs=in_specs,
        out_specs=out_specs,
        out_shape=out_shape,
        scratch_shapes=sem_scratch + ([pltpu.VMEM((tm, tn), F32)] if gk > 1 else []),
        input_output_aliases=aliases,
        compiler_params=_cparams(est, dimension_semantics=(row_sem, "arbitrary" if comm is not None else "parallel",
                                                           "arbitrary")),
    )(*operands)
    if comm is not None:
        main = outs[:no]
        return (main if no > 1 else main[0]), list(outs[no:])
    return outs if no > 1 else outs[0]


_TL = 512


def _rms_fwd(h, g, name):
    L, D = h.shape
    tl = _tile(L, _TL)

    def body(h_ref, g_ref, o_ref):
        x = h_ref[...]
        r = lax.rsqrt(jnp.mean(x * x, axis=-1, keepdims=True) + EPS)
        o_ref[...] = ((x * r) * g_ref[...]).astype(o_ref.dtype)

    return pl.pallas_call(
        body, name=name, grid=(L // tl,),
        in_specs=[pl.BlockSpec((tl, D), lambda i: (i, 0)), pl.BlockSpec((1, D), lambda i: (0, 0))],
        out_specs=pl.BlockSpec((tl, D), lambda i: (i, 0)),
        out_shape=_sds((L, D), _ACT),
        compiler_params=_cparams(3 * _nbytes((tl, D), F32)),
    )(h, g.reshape(1, D))


def _rms_bwd(dxn, h, g, dres, name):
    L, D = h.shape
    tl = _tile(L, _TL)

    def body(d_ref, h_ref, g_ref, r_ref, o_ref, dg_ref):
        x = h_ref[...]
        r = lax.rsqrt(jnp.mean(x * x, axis=-1, keepdims=True) + EPS)
        xhat = x * r
        d = d_ref[...].astype(F32)
        gy = d * g_ref[...]
        dx = r * (gy - xhat * jnp.mean(gy * xhat, axis=-1, keepdims=True))
        o_ref[...] = r_ref[...] + dx

        @pl.when(pl.program_id(0) == 0)
        def _():
            dg_ref[...] = jnp.zeros_like(dg_ref)

        dg_ref[...] += jnp.sum(d * xhat, axis=0, keepdims=True)

    dh, dg = pl.pallas_call(
        body, name=name, grid=(L // tl,),
        in_specs=[pl.BlockSpec((tl, D), lambda i: (i, 0)), pl.BlockSpec((tl, D), lambda i: (i, 0)),
                  pl.BlockSpec((1, D), lambda i: (0, 0)), pl.BlockSpec((tl, D), lambda i: (i, 0))],
        out_specs=[pl.BlockSpec((tl, D), lambda i: (i, 0)), pl.BlockSpec((1, D), lambda i: (0, 0))],
        out_shape=[_sds((L, D), F32), _sds((1, D), F32)],
        compiler_params=_cparams(5 * _nbytes((tl, D), F32)),
    )(dxn, h, g.reshape(1, D), dres)
    return dh, dg.reshape(D)


def _rope_tables(positions):
    L = positions.shape[0]
    tl = _tile(L, 1024)
    inv = 1.0 / (ROPE_THETA ** (np.arange(0, HEAD_DIM, 2, dtype=np.float32) / HEAD_DIM))
    inv128 = jnp.asarray(np.tile(inv.astype(np.float32), 4).reshape(1, LANES))

    def body(p_ref, i_ref, c_ref, s_ref):
        ang = p_ref[...].astype(F32) * i_ref[...]
        c_ref[...] = jnp.cos(ang)
        s_ref[...] = jnp.sin(ang)

    return pl.pallas_call(
        body, name="rope_tables", grid=(L // tl,),
        in_specs=[pl.BlockSpec((tl, 1), lambda i: (i, 0)), pl.BlockSpec((1, LANES), lambda i: (0, 0))],
        out_specs=[pl.BlockSpec((tl, LANES), lambda i: (i, 0))] * 2,
        out_shape=[_sds((L, LANES), F32)] * 2,
    )(positions.reshape(L, 1), inv128)


_GM_TL = 256


def _gmlp_head(Z, W, bfull, lg, lb, maskv):
    G = _gelu(Z)
    mu = jnp.sum(jnp.where(maskv, G, 0.0), axis=-1, keepdims=True) * (1.0 / HEAD_DIM)
    xc = jnp.where(maskv, G - mu, 0.0)
    var = jnp.sum(xc * xc, axis=-1, keepdims=True) * (1.0 / HEAD_DIM)
    rstd = lax.rsqrt(var + EPS)
    xhat = xc * rstd
    vn = xhat * lg + lb
    sv = _dot(W, vn) + bfull
    return G, xhat, rstd, vn, sv


def _tril(W):
    return jnp.where(_row(W.shape) >= _lane(W.shape), W, 0.0)


def _triu(W):
    return jnp.where(_row(W.shape) <= _lane(W.shape), W, 0.0)


def _gmlp_fwd(z, ws, bfull, lgf, lbf, name):
    L = z.shape[0]
    tl = _tile(L, _GM_TL)
    nch = tl // CHUNK

    def body(z_ref, w_ref, b_ref, lg_ref, lb_ref, o_ref):
        maskv = _lane((CHUNK, LANES)) >= HEAD_DIM
        for c in range(nch):
            rows = slice(c * CHUNK, (c + 1) * CHUNK)
            for hp in range(A_HEADS // 2):
                acc = None
                for hh in range(2):
                    h = 2 * hp + hh
                    Z = z_ref[rows, h * LANES:(h + 1) * LANES]
                    G, _, _, _, sv = _gmlp_head(Z, _tril(w_ref[h]), b_ref[h], lg_ref[h:h + 1, :], lb_ref[h:h + 1, :], maskv)
                    prod = G * pltpu.roll(sv, HEAD_DIM, axis=1)
                    acc = prod if hh == 0 else acc + pltpu.roll(prod, HEAD_DIM, axis=1)
                o_ref[rows, hp * LANES:(hp + 1) * LANES] = acc

    return pl.pallas_call(
        body, name=name, grid=(L // tl,),
        in_specs=[pl.BlockSpec((tl, IN_A), lambda i: (i, 0)),
                  pl.BlockSpec((A_HEADS, CHUNK, CHUNK), lambda i: (0, 0, 0)),
                  pl.BlockSpec((A_HEADS, CHUNK, LANES), lambda i: (0, 0, 0)),
                  pl.BlockSpec((A_HEADS, LANES), lambda i: (0, 0)),
                  pl.BlockSpec((A_HEADS, LANES), lambda i: (0, 0))],
        out_specs=pl.BlockSpec((tl, 2 * LANES), lambda i: (i, 0)),
        out_shape=_sds((L, 2 * LANES), F32),
    )(z, ws, bfull, lgf, lbf)


def _gmlp_bwd(z, dya, ws, wsT, bfull, lgf, lbf, name):
    L = z.shape[0]
    tl = _tile(L, _GM_TL)
    nch = tl // CHUNK
    nsteps = L // tl

    def body(z_ref, d_ref, w_ref, wt_ref, b_ref, lg_ref, lb_ref, dz_ref, dw_ref, db_ref, dlg_ref, dlb_ref):
        step = pl.program_id(0)

        @pl.when(step == 0)
        def _():
            dw_ref[...] = jnp.zeros_like(dw_ref)
            db_ref[...] = jnp.zeros_like(db_ref)
            dlg_ref[...] = jnp.zeros_like(dlg_ref)
            dlb_ref[...] = jnp.zeros_like(dlb_ref)

        lane = _lane((CHUNK, LANES))
        maskv = lane >= HEAD_DIM
        for c in range(nch):
            rows = slice(c * CHUNK, (c + 1) * CHUNK)
            for h in range(A_HEADS):
                hp, hh = divmod(h, 2)
                Z = z_ref[rows, h * LANES:(h + 1) * LANES]
                lg = lg_ref[h:h + 1, :]
                G, xhat, rstd, vn, sv = _gmlp_head(Z, _tril(w_ref[h]), b_ref[h], lg, lb_ref[h:h + 1, :], maskv)
                dpair = d_ref[rows, hp * LANES:(hp + 1) * LANES]
                if hh == 1:
                    dpair = pltpu.roll(dpair, HEAD_DIM, axis=1)
                dout = jnp.where(maskv, 0.0, dpair)
                du = dout * pltpu.roll(sv, HEAD_DIM, axis=1)
                dsv = pltpu.roll(dout * G, HEAD_DIM, axis=1)
                dw_ref[h] += _tril(_dot(dsv, vn, _NT))
                db_ref[h] += dsv
                dvn = _dot(_triu(wt_ref[h]), dsv)
                dlg_ref[h] += dvn * xhat
                dlb_ref[h] += dvn
                dxh = dvn * lg
                m1 = jnp.sum(dxh, axis=-1, keepdims=True) * (1.0 / HEAD_DIM)
                m2 = jnp.sum(dxh * xhat, axis=-1, keepdims=True) * (1.0 / HEAD_DIM)
                dv = jnp.where(maskv, rstd * (dxh - m1 - xhat * m2), 0.0)
                dz_ref[rows, h * LANES:(h + 1) * LANES] = ((du + dv) * _gelu_grad(Z)).astype(dz_ref.dtype)

        @pl.when(step == nsteps - 1)
        def _():
            for h in range(A_HEADS):
                db_ref[h] = jnp.broadcast_to(jnp.sum(db_ref[h], axis=1, keepdims=True), (CHUNK, LANES))
                dlg_ref[h] = jnp.broadcast_to(jnp.sum(dlg_ref[h], axis=0, keepdims=True), (CHUNK, LANES))
                dlb_ref[h] = jnp.broadcast_to(jnp.sum(dlb_ref[h], axis=0, keepdims=True), (CHUNK, LANES))

    full3 = pl.BlockSpec((A_HEADS, CHUNK, LANES), lambda i: (0, 0, 0))
    return pl.pallas_call(
        body, name=name, grid=(nsteps,),
        in_specs=[pl.BlockSpec((tl, IN_A), lambda i: (i, 0)),
                  pl.BlockSpec((tl, 2 * LANES), lambda i: (i, 0)),
                  full3, full3, full3,
                  pl.BlockSpec((A_HEADS, LANES), lambda i: (0, 0)),
                  pl.BlockSpec((A_HEADS, LANES), lambda i: (0, 0))],
        out_specs=[pl.BlockSpec((tl, IN_A), lambda i: (i, 0)), full3, full3, full3, full3],
        out_shape=[_sds((L, IN_A), _ACT)] + [_sds((A_HEADS, CHUNK, LANES), F32)] * 4,
    )(z, dya, ws, wsT, bfull, lgf, lbf)


def _head_rstd(x, lo):
    sq = x * x
    s_lo = jnp.sum(jnp.where(lo, sq, 0.0), axis=-1, keepdims=True)
    s_hi = jnp.sum(jnp.where(lo, 0.0, sq), axis=-1, keepdims=True)
    return jnp.where(lo, lax.rsqrt(s_lo * (1.0 / HEAD_DIM) + EPS), lax.rsqrt(s_hi * (1.0 / HEAD_DIM) + EPS))


def _rot_half(x, first):
    return jnp.where(first, -pltpu.roll(x, LANES - HEAD_DIM // 2, axis=1), pltpu.roll(x, HEAD_DIM // 2, axis=1))


def _qk_prep(z, cos, sin, gq, gk, name):
    L = z.shape[0]
    tl = _tile(L, _TL)
    nq = IN_Q // LANES

    def body(q_ref, k_ref, c_ref, s_ref, gq_ref, gk_ref, qo_ref, ko_ref):
        lane = _lane((tl, LANES))
        lo = lane < HEAD_DIM
        first = (lane % HEAD_DIM) < (HEAD_DIM // 2)
        c, s = c_ref[...], s_ref[...]

        def prep(x, g):
            xn = (x * _head_rstd(x, lo)) * g
            return xn * c + _rot_half(xn, first) * s

        for j in range(nq):
            qo_ref[:, j * LANES:(j + 1) * LANES] = prep(q_ref[:, j * LANES:(j + 1) * LANES], gq_ref[...]).astype(qo_ref.dtype)
        ko_ref[...] = prep(k_ref[...], gk_ref[...]).astype(ko_ref.dtype)

    return pl.pallas_call(
        body, name=name, grid=(L // tl,),
        in_specs=[pl.BlockSpec((tl, IN_Q), lambda i: (i, 1)),
                  pl.BlockSpec((tl, IN_KV), lambda i: (i, 8)),
                  pl.BlockSpec((tl, LANES), lambda i: (i, 0)), pl.BlockSpec((tl, LANES), lambda i: (i, 0)),
                  pl.BlockSpec((1, LANES), lambda i: (0, 0)), pl.BlockSpec((1, LANES), lambda i: (0, 0))],
        out_specs=[pl.BlockSpec((tl, IN_Q), lambda i: (i, 0)), pl.BlockSpec((tl, IN_KV), lambda i: (i, 0))],
        out_shape=[_sds((L, IN_Q), _ACT), _sds((L, IN_KV), _ACT)],
    )(z, z, cos, sin, gq, gk)


def _qk_prep_bwd(z, dq, dkc, dkp, dvc, dvp, cos, sin, gq, gk, name):
    L = z.shape[0]
    tl = _ATT_QB * WINDOW
    nb = L // tl
    nq = IN_Q // LANES

    def body(q_ref, k_ref, dq_ref, dkc_ref, dkp_ref, dvc_ref, dvp_ref, c_ref, s_ref, gq_ref, gk_ref,
             dzq_ref, dzk_ref, dzv_ref, dgq_ref, dgk_ref):
        n = pl.program_id(0)

        @pl.when(n == 0)
        def _():
            dgq_ref[...] = jnp.zeros_like(dgq_ref)
            dgk_ref[...] = jnp.zeros_like(dgk_ref)

        lane = _lane((tl, LANES))
        lo = lane < HEAD_DIM
        first = (lane % HEAD_DIM) < (HEAD_DIM // 2)
        c, s = c_ref[...], s_ref[...]
        has_next = jnp.where(n < nb - 1, 1.0, 0.0)

        def bwd(x, g, dy):
            r = _head_rstd(x, lo)
            xhat = x * r
            dxn = dy * c - _rot_half(dy * s, first)
            gy = dxn * g
            t = gy * xhat
            m_lo = jnp.sum(jnp.where(lo, t, 0.0), axis=-1, keepdims=True)
            m_hi = jnp.sum(jnp.where(lo, 0.0, t), axis=-1, keepdims=True)
            m = jnp.where(lo, m_lo, m_hi) * (1.0 / HEAD_DIM)
            dx = r * (gy - xhat * m)
            dg = jnp.sum(dxn * xhat, axis=0, keepdims=True)
            return dx, dg

        dgq = jnp.zeros((1, LANES), F32)
        for j in range(nq):
            sl = slice(j * LANES, (j + 1) * LANES)
            dx, dg = bwd(q_ref[:, sl], gq_ref[...], dq_ref[:, sl].astype(F32))
            dzq_ref[:, sl] = dx.astype(dzq_ref.dtype)
            dgq = dgq + dg
        dgq_ref[...] += dgq + pltpu.roll(dgq, HEAD_DIM, axis=1)
        def with_next(cur_ref, nxt_ref):
            head = jnp.zeros((tl - WINDOW, IN_KV), F32)
            return cur_ref[...] + jnp.concatenate([head, has_next * nxt_ref[...]], axis=0)

        dx, dg = bwd(k_ref[...], gk_ref[...], with_next(dkc_ref, dkp_ref))
        dzk_ref[...] = dx.astype(dzk_ref.dtype)
        dgk_ref[...] += dg + pltpu.roll(dg, HEAD_DIM, axis=1)
        dzv_ref[...] = with_next(dvc_ref, dvp_ref).astype(dzv_ref.dtype)

    nxt = lambda i: (jnp.minimum(i + 1, nb - 1), 0)
    cur = lambda i: (i, 0)
    kv = pl.BlockSpec((tl, IN_KV), cur)
    kvn = pl.BlockSpec((WINDOW, IN_KV), nxt)
    one = pl.BlockSpec((1, LANES), lambda i: (0, 0))
    return pl.pallas_call(
        body, name=name, grid=(nb,),
        in_specs=[pl.BlockSpec((tl, IN_Q), lambda i: (i, 1)), pl.BlockSpec((tl, IN_KV), lambda i: (i, 8)),
                  pl.BlockSpec((tl, IN_Q), cur), kv, kvn, kv, kvn,
                  kv, kv, one, one],
        out_specs=[pl.BlockSpec((tl, IN_Q), cur), kv, kv, one, one],
        out_shape=[_sds((L, IN_Q), _ACT), _sds((L, IN_KV), _ACT),
                   _sds((L, IN_KV), _ACT), _sds((1, LANES), F32),
                   _sds((1, LANES), F32)],
    )(z, z, dq, dkc, dkp, dvc, dvp, cos, sin, gq, gk)


def _attn_mask(n):
    shp = (2 * WINDOW, B_GROUP * WINDOW)
    qi = _lane(shp) % WINDOW
    kj = _row(shp)
    off = 0 if n is None else jnp.where(n > 0, 0, 4 * WINDOW)
    return ((kj >= WINDOW) & (kj - WINDOW <= qi)) | ((kj < WINDOW) & (kj > qi + off))


def _kv_lanes(j):
    lane = _lane((WINDOW, LANES))
    return (lane >= j * HEAD_DIM) & (lane < (j + 1) * HEAD_DIM)


_ATT_QB = 4


def _stack_heads(ref, rows, j, kvl):
    parts = []
    for g in range(B_GROUP):
        h = j * B_GROUP + g
        slab = ref[rows, (h // 2) * LANES:(h // 2 + 1) * LANES].astype(F32)
        if (h % 2) != j:
            slab = pltpu.roll(slab, HEAD_DIM, axis=1)
        parts.append(jnp.where(kvl, slab, 0.0))
    return jnp.concatenate(parts, axis=0)


def _attn_probs(qs, k2, sink_row, mask):
    s = _dot(k2, qs, _NT) * (HEAD_DIM ** -0.5)
    s = jnp.where(mask, s, NEG)
    m = jnp.maximum(jnp.max(s, axis=0, keepdims=True), sink_row)
    p = jnp.exp(s - m)
    esink = jnp.exp(sink_row - m)
    inv = 1.0 / (jnp.sum(p, axis=0, keepdims=True) + esink)
    return p * inv, esink * inv


def _sink_row(sink_ref, j):
    lane = _lane((1, B_GROUP * WINDOW))
    row = jnp.full((1, B_GROUP * WINDOW), sink_ref[j * B_GROUP], F32)
    for g in range(1, B_GROUP):
        row = jnp.where(lane >= g * WINDOW, sink_ref[j * B_GROUP + g], row)
    return row


def _attn_fwd(q, k, z, sinks, name):
    L = q.shape[0]
    QB = _ATT_QB
    tq = QB * WINDOW
    prev = lambda n: (jnp.maximum(QB * n - 1, 0), 0)
    prev_v = lambda n: (jnp.maximum(QB * n - 1, 0), 9)

    def body(s_ref, q_ref, kp_ref, kc_ref, vp_ref, vc_ref, o_ref):
        n = pl.program_id(0)
        k3 = jnp.concatenate([kp_ref[...], kc_ref[...]], axis=0)
        v3 = jnp.concatenate([vp_ref[...], vc_ref[...]], axis=0)
        for b in range(QB):
            rows = slice(b * WINDOW, (b + 1) * WINDOW)
            mask = _attn_mask(n if b == 0 else None)
            k2 = k3[b * WINDOW:(b + 2) * WINDOW]
            v2 = v3[b * WINDOW:(b + 2) * WINDOW]
            slabs = [None] * (IN_Q // LANES)
            for j in range(B_KV_HEADS):
                kvl = _kv_lanes(j)
                qs = _stack_heads(q_ref, rows, j, kvl)
                pn, _ = _attn_probs(qs, k2, _sink_row(s_ref, j), mask)
                o = _dot(pn, v2, _TN)
                for g in range(B_GROUP):
                    h = j * B_GROUP + g
                    piece = jnp.where(kvl, o[g * WINDOW:(g + 1) * WINDOW], 0.0)
                    if (h % 2) != j:
                        piece = pltpu.roll(piece, HEAD_DIM, axis=1)
                    slabs[h // 2] = piece if slabs[h // 2] is None else slabs[h // 2] + piece
            for t, sl in enumerate(slabs):
                o_ref[rows, t * LANES:(t + 1) * LANES] = sl

    return pl.pallas_call(
        body, name=name, grid=(L // tq,),
        in_specs=[pl.BlockSpec(memory_space=pltpu.SMEM),
                  pl.BlockSpec((tq, IN_Q), lambda n: (n, 0)),
                  pl.BlockSpec((WINDOW, IN_KV), prev), pl.BlockSpec((tq, IN_KV), lambda n: (n, 0)),
                  pl.BlockSpec((WINDOW, IN_KV), prev_v), pl.BlockSpec((tq, IN_KV), lambda n: (n, 9))],
        out_specs=pl.BlockSpec((tq, IN_Q), lambda n: (n, 0)),
        out_shape=_sds((L, IN_Q), F32),
    )(sinks, q, k, k, z, z)


def _attn_bwd(q, k, z, sinks, dyb, name):
    L = q.shape[0]
    QB = _ATT_QB
    tq = QB * WINDOW
    nsteps = L // tq
    prev = lambda n: (jnp.maximum(QB * n - 1, 0), 0)
    prev_v = lambda n: (jnp.maximum(QB * n - 1, 0), 9)
    cur = lambda n: (n, 0)

    def body(s_ref, q_ref, kp_ref, kc_ref, vp_ref, vc_ref, d_ref, dq_ref, dkc_ref, dkp_ref, dvc_ref, dvp_ref, ds_ref):
        n = pl.program_id(0)

        @pl.when(n == 0)
        def _():
            ds_ref[...] = jnp.zeros_like(ds_ref)

        k3 = jnp.concatenate([kp_ref[...], kc_ref[...]], axis=0)
        v3 = jnp.concatenate([vp_ref[...], vc_ref[...]], axis=0)
        dkb = [None] * (QB + 1)
        dvb = [None] * (QB + 1)
        dsink = jnp.zeros((1, LANES), F32)
        lane1 = _lane((1, LANES))
        add = lambda acc, v: v if acc is None else acc + v
        for b in range(QB):
            rows = slice(b * WINDOW, (b + 1) * WINDOW)
            mask = _attn_mask(n if b == 0 else None)
            k2 = k3[b * WINDOW:(b + 2) * WINDOW]
            v2 = v3[b * WINDOW:(b + 2) * WINDOW]
            slabs = [None] * (IN_Q // LANES)
            for j in range(B_KV_HEADS):
                kvl = _kv_lanes(j)
                qs = _stack_heads(q_ref, rows, j, kvl)
                dos = _stack_heads(d_ref, rows, j, kvl)
                pn, psink = _attn_probs(qs, k2, _sink_row(s_ref, j), mask)
                dp = _dot(v2, dos, _NT)
                dd = jnp.sum(pn * dp, axis=0, keepdims=True)
                dss = (pn * (dp - dd)) * (HEAD_DIM ** -0.5)
                dqs = _dot(dss, k2, _TN)
                dk2 = _dot(dss, qs)
                dv2 = _dot(pn, dos)
                dkb[b], dkb[b + 1] = add(dkb[b], dk2[:WINDOW]), add(dkb[b + 1], dk2[WINDOW:])
                dvb[b], dvb[b + 1] = add(dvb[b], dv2[:WINDOW]), add(dvb[b + 1], dv2[WINDOW:])
                sd = psink * dd
                for g in range(B_GROUP):
                    h = j * B_GROUP + g
                    piece = jnp.where(kvl, dqs[g * WINDOW:(g + 1) * WINDOW], 0.0)
                    if (h % 2) != j:
                        piece = pltpu.roll(piece, HEAD_DIM, axis=1)
                    slabs[h // 2] = piece if slabs[h // 2] is None else slabs[h // 2] + piece
                    tot = jnp.sum(sd[:, g * WINDOW:(g + 1) * WINDOW], axis=1, keepdims=True)
                    dsink = dsink - jnp.where(lane1 == h, tot, 0.0)
            for t, sl in enumerate(slabs):
                dq_ref[rows, t * LANES:(t + 1) * LANES] = sl
        dkp_ref[...] = dkb[0]
        dvp_ref[...] = dvb[0]
        for b in range(QB):
            dkc_ref[b * WINDOW:(b + 1) * WINDOW, :] = dkb[b + 1]
            dvc_ref[b * WINDOW:(b + 1) * WINDOW, :] = dvb[b + 1]
        ds_ref[0:1, :] += dsink

    kvs = pl.BlockSpec((tq, IN_KV), cur)
    kvp = pl.BlockSpec((WINDOW, IN_KV), cur)
    kvo = _sds((L, IN_KV), F32)
    kvpo = _sds((nsteps * WINDOW, IN_KV), F32)
    return pl.pallas_call(
        body, name=name, grid=(nsteps,),
        in_specs=[pl.BlockSpec(memory_space=pltpu.SMEM),
                  pl.BlockSpec((tq, IN_Q), cur),
                  pl.BlockSpec((WINDOW, IN_KV), prev), kvs,
                  pl.BlockSpec((WINDOW, IN_KV), prev_v), pl.BlockSpec((tq, IN_KV), lambda n: (n, 9)),
                  pl.BlockSpec((tq, IN_Q), cur)],
        out_specs=[pl.BlockSpec((tq, IN_Q), cur), kvs, kvp, kvs, kvp, pl.BlockSpec((SUBLANES, LANES), lambda n: (0, 0))],
        out_shape=[_sds((L, IN_Q), F32), kvo, kvpo, kvo, kvpo, _sds((SUBLANES, LANES), F32)],
    )(sinks, q, k, k, z, z, dyb)


def _ssm_disc(are, aim, ldt, bre, bim):
    dt = jnp.exp(ldt)
    mag = jnp.exp(are * dt)
    lr, li = mag * jnp.cos(aim * dt), mag * jnp.sin(aim * dt)
    den = are * are + aim * aim
    xr, xi = lr - 1.0, li
    cr, ci = (xr * are + xi * aim) / den, (xi * are - xr * aim) / den
    return lr, li, cr * bre - ci * bim, cr * bim + ci * bre


def _ssm_prep(are, aim, ldt, bre, bim):
    shp3, shpb = are.shape, bre.shape

    def body(are_ref, aim_ref, ldt_ref, bre_ref, bim_ref, lr_ref, li_ref, br_ref, bi_ref):
        lr, li, br, bi = _ssm_disc(are_ref[...], aim_ref[...], ldt_ref[...], bre_ref[...], bim_ref[...])
        lr_ref[...] = lr
        li_ref[...] = li
        br_ref[...] = br
        bi_ref[...] = bi

    return pl.pallas_call(
        body, name="ssm_prep",
        out_shape=[_sds(shp3, F32)] * 2 + [_sds(shpb, F32)] * 2,
    )(are, aim, ldt, bre, bim)


def _ssm_prep_bwd(are, aim, ldt, bre, bim, dlr, dli, dbr, dbi):
    shp3, shpb = are.shape, bre.shape

    def body(are_ref, aim_ref, ldt_ref, bre_ref, bim_ref, dlr_ref, dli_ref, dbr_ref, dbi_ref,
             o_are, o_aim, o_ldt, o_bre, o_bim):
        _, vjp = jax.vjp(_ssm_disc, are_ref[...], aim_ref[...], ldt_ref[...], bre_ref[...], bim_ref[...])
        g = vjp((dlr_ref[...], dli_ref[...], dbr_ref[...], dbi_ref[...]))
        o_are[...] = g[0]
        o_aim[...] = g[1]
        o_ldt[...] = jnp.broadcast_to(jnp.sum(g[2], axis=-1, keepdims=True), shp3)
        o_bre[...] = g[3]
        o_bim[...] = g[4]

    return pl.pallas_call(
        body, name="ssm_prep_bwd",
        out_shape=[_sds(shp3, F32)] * 3 + [_sds(shpb, F32)] * 2,
    )(are, aim, ldt, bre, bim, dlr, dli, dbr, dbi)


_SCAN_TB = 512
_SCAN_W = 256


def _cmul(ar, ai, br, bi):
    return ar * br - ai * bi, ar * bi + ai * br


def _ssm_scan(x, lam_r, lam_i, name, reverse=False, states=None):
    L = x.shape[0]
    tb = _tile(L, _SCAN_TB)
    nrb = L // tb
    nt = tb // SUBLANES
    W = _SCAN_W
    with_da = states is not None

    def body(*refs):
        if with_da:
            xr_ref, xi_ref, sr_ref, si_ref, ar_ref, ai_ref, o_ref, dar_ref, dai_ref, cr_ref, ci_ref = refs
        else:
            xr_ref, xi_ref, ar_ref, ai_ref, o_ref, cr_ref, ci_ref = refs
        step = pl.program_id(0)

        @pl.when(step == 0)
        def _():
            cr_ref[...] = jnp.zeros_like(cr_ref)
            ci_ref[...] = jnp.zeros_like(ci_ref)
            if with_da:
                dar_ref[...] = jnp.zeros_like(dar_ref)
                dai_ref[...] = jnp.zeros_like(dai_ref)

        row = _row((SUBLANES, W))

        def shift(v, d, fill):
            if reverse:
                return jnp.where(row < SUBLANES - d, pltpu.roll(v, SUBLANES - d, axis=0), fill)
            return jnp.where(row >= d, pltpu.roll(v, d, axis=0), fill)

        edge = 0 if reverse else SUBLANES - 1
        for wb in range(N_STATE // W):
            cols = slice(wb * W, (wb + 1) * W)
            a1r = jnp.broadcast_to(ar_ref[:, cols], (SUBLANES, W))
            a1i = jnp.broadcast_to(ai_ref[:, cols], (SUBLANES, W))
            if reverse:
                a1i = -a1i
            a2r, a2i = _cmul(a1r, a1i, a1r, a1i)
            a4r, a4i = _cmul(a2r, a2i, a2r, a2i)
            pws = ((1, a1r, a1i), (2, a2r, a2i), (4, a4r, a4i))
            pr, pi = a1r, a1i
            for d, _, _ in pws:
                qr, qi = _cmul(pr, pi, shift(pr, d, 1.0), shift(pi, d, 0.0))
                pr, pi = qr, qi
            mws = []
            for d, er, ei in pws:
                ok = (row < SUBLANES - d) if reverse else (row >= d)
                mws.append(((SUBLANES - d) if reverse else d, jnp.where(ok, er, 0.0), jnp.where(ok, ei, 0.0)))

            def tile(i, carry):
                cr, ci, dr, di = carry
                t = (nt - 1 - i) if reverse else i
                r0 = pl.multiple_of(t * SUBLANES, SUBLANES)
                vr = xr_ref[pl.ds(r0, SUBLANES), cols]
                vi = xi_ref[pl.ds(r0, SUBLANES), cols]
                for sh, er, ei in mws:
                    tr, ti = _cmul(er, ei, pltpu.roll(vr, sh, axis=0), pltpu.roll(vi, sh, axis=0))
                    vr, vi = vr + tr, vi + ti
                tr, ti = _cmul(pr, pi, cr, ci)
                vr, vi = vr + tr, vi + ti
                o_ref[pl.ds(r0, SUBLANES), cols] = vr
                o_ref[pl.ds(r0, SUBLANES), slice(N_STATE + wb * W, N_STATE + (wb + 1) * W)] = vi
                if with_da:
                    gr = jnp.where(row < SUBLANES - 1, pltpu.roll(vr, SUBLANES - 1, axis=0), cr)
                    gi = jnp.where(row < SUBLANES - 1, pltpu.roll(vi, SUBLANES - 1, axis=0), ci)
                    sr = sr_ref[pl.ds(r0, SUBLANES), cols]
                    si = si_ref[pl.ds(r0, SUBLANES), cols]
                    dr = dr + sr * gr + si * gi
                    di = di + sr * gi - si * gr
                ncr = jnp.broadcast_to(vr[edge:edge + 1, :], (SUBLANES, W))
                nci = jnp.broadcast_to(vi[edge:edge + 1, :], (SUBLANES, W))
                return ncr, nci, dr, di

            zero = jnp.zeros((SUBLANES, W), F32)
            cr, ci, dr, di = lax.fori_loop(0, nt, tile, (cr_ref[:, cols], ci_ref[:, cols], zero, zero), unroll=2)
            cr_ref[:, cols] = cr
            ci_ref[:, cols] = ci
            if with_da:
                dar_ref[:, cols] += dr
                dai_ref[:, cols] += di

        if with_da:
            @pl.when(step == nrb - 1)
            def _():
                dar_ref[...] = jnp.broadcast_to(jnp.sum(dar_ref[...], axis=0, keepdims=True), dar_ref.shape)
                dai_ref[...] = jnp.broadcast_to(jnp.sum(dai_ref[...], axis=0, keepdims=True), dai_ref.shape)

    rb = (lambda i: (nrb - 1 - i, 0)) if reverse else (lambda i: (i, 0))
    rb_im = (lambda i: (nrb - 1 - i, 1)) if reverse else (lambda i: (i, 1))
    blk_r = pl.BlockSpec((tb, N_STATE), rb)
    blk_i = pl.BlockSpec((tb, N_STATE), rb_im)
    one = pl.BlockSpec((1, N_STATE), lambda i: (0, 0))
    acc = pl.BlockSpec((SUBLANES, N_STATE), lambda i: (0, 0))
    ins = [x, x] + ([states, states] if with_da else []) + [lam_r, lam_i]
    in_specs = [blk_r, blk_i] + ([blk_r, blk_i] if with_da else []) + [one, one]
    out_specs = [pl.BlockSpec((tb, 2 * N_STATE), rb)] + ([acc, acc] if with_da else [])
    out_shape = [_sds((L, 2 * N_STATE), F32)] + (
        [_sds((SUBLANES, N_STATE), F32)] * 2 if with_da else [])
    outs = pl.pallas_call(
        body, name=name, grid=(nrb,), in_specs=in_specs, out_specs=out_specs, out_shape=out_shape,
        scratch_shapes=[pltpu.VMEM((SUBLANES, N_STATE), F32)] * 2,
        compiler_params=_cparams((6 if with_da else 4) * _nbytes((tb, N_STATE), F32),
                                 dimension_semantics=("arbitrary",)),
    )(*ins)
    return outs if with_da else outs[0]


_GROUPS = ((0, 256), (256, 768), (768, 1024))


def _merge_fwd(ya, yb, g12, mixg, name):
    L = ya.shape[0]
    tl = _tile(L, _TL)

    def body(a_ref, b_ref, g_ref, m_ref, o_ref):
        g12v = g_ref[...]
        yc = g12v[:, :C_WIDTH] * _sigmoid(g12v[:, C_WIDTH:])
        for (lo, hi), y in zip(_GROUPS, (a_ref[...], b_ref[...], yc)):
            r = lax.rsqrt(jnp.mean(y * y, axis=-1, keepdims=True) + EPS)
            o_ref[:, lo:hi] = ((y * r) * m_ref[:, lo:hi]).astype(o_ref.dtype)

    row = lambda w: pl.BlockSpec((tl, w), lambda i: (i, 0))
    return pl.pallas_call(
        body, name=name, grid=(L // tl,),
        in_specs=[row(256), row(512), row(512), pl.BlockSpec((1, D_MODEL), lambda i: (0, 0))],
        out_specs=row(D_MODEL), out_shape=_sds((L, D_MODEL), _ACT),
    )(ya, yb, g12, mixg.reshape(1, D_MODEL))


def _merge_bwd(dy, ya, yb, g12, mixg, name):
    L = ya.shape[0]
    tl = _tile(L, _TL)

    def body(d_ref, a_ref, b_ref, g_ref, m_ref, da_ref, db_ref, dg_ref, dm_ref):
        @pl.when(pl.program_id(0) == 0)
        def _():
            dm_ref[...] = jnp.zeros_like(dm_ref)

        g12v = g_ref[...]
        g1, sg = g12v[:, :C_WIDTH], _sigmoid(g12v[:, C_WIDTH:])
        yc = g1 * sg
        outs = []
        for (lo, hi), y in zip(_GROUPS, (a_ref[...], b_ref[...], yc)):
            r = lax.rsqrt(jnp.mean(y * y, axis=-1, keepdims=True) + EPS)
            xhat = y * r
            d = d_ref[:, lo:hi]
            gy = d * m_ref[:, lo:hi]
            outs.append(r * (gy - xhat * jnp.mean(gy * xhat, axis=-1, keepdims=True)))
            dm_ref[:, lo:hi] += jnp.sum(d * xhat, axis=0, keepdims=True)
        da_ref[...] = outs[0]
        db_ref[...] = outs[1]
        dyc = outs[2]
        dg_ref[:, :C_WIDTH] = (dyc * sg).astype(dg_ref.dtype)
        dg_ref[:, C_WIDTH:] = (dyc * g1 * sg * (1.0 - sg)).astype(dg_ref.dtype)

    row = lambda w: pl.BlockSpec((tl, w), lambda i: (i, 0))
    one = pl.BlockSpec((1, D_MODEL), lambda i: (0, 0))
    return pl.pallas_call(
        body, name=name, grid=(L // tl,),
        in_specs=[row(D_MODEL), row(256), row(512), row(512), one],
        out_specs=[row(256), row(512), row(512), one],
        out_shape=[_sds((L, 256), F32), _sds((L, 512), F32),
                   _sds((L, 512), _ACT), _sds((1, D_MODEL), F32)],
    )(dy, ya, yb, g12, mixg.reshape(1, D_MODEL))


def _ple_bwd_elem(dh, gate, e, name):
    L, D = dh.shape
    tl = _tile(L, _TL)

    def body(d_ref, g_ref, e_ref, p_ref, o_ref):
        d, g = d_ref[...], g_ref[...]
        p_ref[...] = (d * e_ref[...] * g * (1.0 - g)).astype(p_ref.dtype)
        o_ref[...] = (d * g).astype(o_ref.dtype)

    row = pl.BlockSpec((tl, D), lambda i: (i, 0))
    return pl.pallas_call(
        body, name=name, grid=(L // tl,), in_specs=[row] * 3, out_specs=[row] * 2,
        out_shape=[_sds((L, D), _ACT)] * 2,
        compiler_params=_cparams(4 * _nbytes((tl, D), F32)),
    )(dh, gate, e)


def _dskip_bwd(dy, z, name):
    L = dy.shape[0]
    tl = _tile(L, _TL)

    def body(d_ref, u_ref, o_ref):
        @pl.when(pl.program_id(0) == 0)
        def _():
            o_ref[...] = jnp.zeros_like(o_ref)

        o_ref[...] += jnp.sum(d_ref[...] * u_ref[...], axis=0, keepdims=True)

    return pl.pallas_call(
        body, name=name, grid=(L // tl,),
        in_specs=[pl.BlockSpec((tl, C_WIDTH), lambda i: (i, 0)), pl.BlockSpec((tl, C_WIDTH), lambda i: (i, 5))],
        out_specs=pl.BlockSpec((1, C_WIDTH), lambda i: (0, 0)),
        out_shape=_sds((1, C_WIDTH), F32),
    )(dy, z)


def _loss_fwd_bwd(y, target):
    L, D = y.shape
    tl = _tile(L, _TL)

    def body(y_ref, t_ref, l_ref, d_ref):
        @pl.when(pl.program_id(0) == 0)
        def _():
            l_ref[...] = jnp.zeros_like(l_ref)

        e = y_ref[...] - t_ref[...]
        d_ref[...] = e * (1.0 / D)
        part = jnp.sum(jnp.sum(e * e, axis=-1, keepdims=True), axis=0, keepdims=True)
        l_ref[...] += jnp.broadcast_to(part, l_ref.shape)

    row = pl.BlockSpec((tl, D), lambda i: (i, 0))
    return pl.pallas_call(
        body, name="loss", grid=(L // tl,), in_specs=[row, row],
        out_specs=[pl.BlockSpec((SUBLANES, LANES), lambda i: (0, 0)), row],
        out_shape=[_sds((SUBLANES, LANES), F32), _sds((L, D), F32)],
    )(y, target)


def _adamw(w, g, m, v, name):
    R, C = w.shape
    tr = R if R <= 512 else _tile_rows(R, 512)

    def body(w_ref, g_ref, m_ref, v_ref, d_ref, nm_ref, nv_ref):
        gv = g_ref[...]
        nm = ADAM_B1 * m_ref[...] + (1.0 - ADAM_B1) * gv
        nv = ADAM_B2 * v_ref[...] + (1.0 - ADAM_B2) * (gv * gv)
        m_hat = nm / (1.0 - ADAM_B1 ** ADAM_STEP)
        v_hat = nv / (1.0 - ADAM_B2 ** ADAM_STEP)
        d_ref[...] = -ADAM_LR * (m_hat / (jnp.sqrt(v_hat) + ADAM_EPS) + ADAM_WD * w_ref[...])
        nm_ref[...] = nm
        nv_ref[...] = nv

    blk = pl.BlockSpec((tr, C), lambda i: (i, 0))
    return pl.pallas_call(
        body, name=name, grid=(R // tr,), in_specs=[blk] * 4, out_specs=[blk] * 3,
        out_shape=[_sds((R, C), F32)] * 3,
        compiler_params=_cparams(7 * _nbytes((tr, C), F32)),
    )(w, g, m, v)


def _tile_rows(R, pref):
    t = pref
    while R % t:
        t -= SUBLANES
    assert t > 0
    return t


def _add_n(xs, name):
    R, C = xs[0].shape
    tr = R if R <= 512 else _tile_rows(R, 512)
    n = len(xs)

    def body(*refs):
        acc = refs[0][...].astype(F32)
        for r in refs[1:n]:
            acc = acc + r[...].astype(F32)
        refs[n][...] = acc

    blk = pl.BlockSpec((tr, C), lambda i: (i, 0))
    return pl.pallas_call(
        body, name=name, grid=(R // tr,), in_specs=[blk] * n, out_specs=blk,
        out_shape=_sds((R, C), F32),
        compiler_params=_cparams((n + 1) * _nbytes((tr, C), F32)),
    )(*xs)


class _Exchange:
    def __init__(self, ins, aliased, fresh, n_sems, start, wait, done):
        self.ins, self.aliased, self.fresh, self.n_sems = ins, aliased, fresh, n_sems
        self.start, self.wait, self.done = start, wait, done


def _mm_host(lp, key, *args, **kw):
    plan = lp.get(key)
    if plan is None:
        return _mm(*args, **kw)
    if not isinstance(plan, _Exchange):
        plan = plan()
    res, outs = _mm(*args, comm=plan, **kw)
    plan.done(outs)
    return res


def _relu2(acc):
    r = jnp.maximum(acc, 0.0)
    return (r * r,)


def _rms_rows(x, g):
    return (x * lax.rsqrt(jnp.mean(x * x, axis=-1, keepdims=True) + EPS)) * g


def _resid_norm_epi(acc, res, g):
    h = res + acc
    return h, _rms_rows(h, g)


def _rms_bwd_epi(acc, h, dres, g):
    r = lax.rsqrt(jnp.mean(h * h, axis=-1, keepdims=True) + EPS)
    xhat = h * r
    gy = acc * g
    dx = r * (gy - xhat * jnp.mean(gy * xhat, axis=-1, keepdims=True))
    return dres + dx, jnp.sum(acc * xhat, axis=0, keepdims=True)


def _layer_fwd(h, xn, lp, cos, sin, g_next):
    L = h.shape[0]
    row = lambda n: lp[n].reshape(1, D_MODEL)
    z = _mm(xn, lp["w_in"], mode="nn", M=L, N=IN_COLS, K=D_MODEL, b_cb=True, out_dtypes=[F32], name="f_w_in")
    ya = _gmlp_fwd(z, lp["ws"], lp["bfull"], lp["lgf"], lp["lbf"], "f_gmlp")
    q, k = _qk_prep(z, cos, sin, lp["gq"], lp["gk"], "f_qk_prep")
    yb = _attn_fwd(q, k, z, lp["sinks"], "f_attn")
    bu = _mm(z, lp["bcat"], mode="nn", M=L, N=2 * N_STATE, K=C_WIDTH, a_off=5, tk=C_WIDTH,
             out_dtypes=[F32], name="f_ssm_in")
    S = _ssm_scan(bu, lp["lam_r"], lp["lam_i"], "f_ssm_scan")
    y, yg = _mm(S, lp["ccat"], mode="nn", M=L, N=C_WIDTH, K=2 * N_STATE, tk=2 * N_STATE,
                extras=[(z, 5), (lp["dskip"], 0)], out_dtypes=[F32, _ACT], name="f_ssm_out",
                epi=lambda acc, u, dsk: (acc + dsk * u, _gelu(acc + dsk * u)))
    g12 = _mm(yg, lp["w12"], mode="nn", M=L, N=2 * C_WIDTH, K=C_WIDTH, out_dtypes=[F32], name="f_glu")
    ycat = _merge_fwd(ya, yb, g12, lp["mix_out_g"], "f_merge")
    h1, hn = _mm(ycat, lp["w_out"], mode="nn", M=L, N=D_MODEL, K=D_MODEL, extras=[(h, 0), (row("mlp_norm_g"), 0)],
                 epi=_resid_norm_epi, out_dtypes=[F32, _ACT], name="f_w_out")
    r = _mm_host(lp, "x_ff1", hn, lp["w_ff1"], mode="nn", M=L, N=D_FF, K=D_MODEL, b_cb=True, epi=_relu2,
                 out_dtypes=[_ACT], name="f_ff1")
    h2, hn3 = _mm_host(lp, "x_ff2", r, lp["w_ff2"], mode="nn", M=L, N=D_MODEL, K=D_FF,
                       extras=[(h1, 0), (row("ple_norm_g"), 0)],
                       epi=_resid_norm_epi, out_dtypes=[F32, _ACT], name="f_ff2")
    e = _mm(lp["p"], lp["w_ple_proj"], mode="nn", M=L, N=D_MODEL, K=PLE_DIM, b_cb=True, tk=PLE_DIM,
            out_dtypes=[F32], name="f_ple_proj")

    def gate_epi(acc, h2_, e_, *g):
        gate_ = _sigmoid(acc)
        h3_ = h2_ + gate_ * e_
        return (h3_, gate_) + ((_rms_rows(h3_, g[0]),) if g else ())

    outs = _mm_host(lp, "x_gate", hn3, lp["w_ple_gate"], mode="nn", M=L, N=D_MODEL, K=D_MODEL,
                    extras=[(h2, 0), (e, 0)] + ([(g_next.reshape(1, D_MODEL), 0)] if g_next is not None else []),
                    epi=gate_epi, out_dtypes=[F32, F32] + ([_ACT] if g_next is not None else []), name="f_ple_gate")
    h3, gate = outs[0], outs[1]
    xn_next = outs[2] if g_next is not None else None
    saved = dict(h=h, xn=xn, z=z, ya=ya, q=q, k=k, yb=yb, S=S, y=y, yg=yg, g12=g12, ycat=ycat, h1=h1, hn=hn,
                 r=r, h2=h2, hn3=hn3, e=e, gate=gate)
    return h3, xn_next, saved


def _layer_bwd(dh3, lp, sv, cos, sin):
    L = dh3.shape[0]
    z = sv["z"]
    dpre, de = _ple_bwd_elem(dh3, sv["gate"], sv["e"], "b_ple_elem")
    stk = {n: None for n in BIG}
    d_gate = _mm(sv["hn3"], dpre, mode="tn", M=D_MODEL, N=D_MODEL, K=L, out_dtypes=[F32], name="b_dw_gate",
                 o_stack=stk["w_ple_gate"])
    d_proj = _mm(lp["p"], de, mode="tn", M=PLE_DIM, N=D_MODEL, K=L, o_cb=True, tm=PLE_DIM,
                 out_dtypes=[F32], name="b_dw_proj", o_stack=stk["w_ple_proj"])
    row = lambda n: lp[n].reshape(1, D_MODEL)
    dh2, dg_ple = _mm_host(lp, "x_bwd0", dpre, lp["w_ple_gate"], mode="nt", M=L, N=D_MODEL, K=D_MODEL,
                           extras=[(sv["h2"], 0), (dh3, 0), (row("ple_norm_g"), 0)], epi=_rms_bwd_epi,
                           out_dtypes=[F32, F32], n_acc=1, name="b_dx_gate")
    da = _mm_host(lp, "x_bwd", dh2, lp["w_ff2"], mode="nt", M=L, N=D_FF, K=D_MODEL, extras=[(sv["r"], 0)],
                  epi=lambda acc, r_: (acc * (2.0 * jnp.sqrt(r_.astype(F32))),), out_dtypes=[_ACT], name="b_dx_ff2")
    d_ff2 = _mm_host(lp, "x_bwd2", sv["r"], dh2, mode="tn", M=D_FF, N=D_MODEL, K=L, out_dtypes=[F32], name="b_dw_ff2")
    d_ff1 = _mm(sv["hn"], da, mode="tn", M=D_MODEL, N=D_FF, K=L, o_cb=True, out_dtypes=[F32], name="b_dw_ff1",
                o_stack=stk["w_ff1"])
    dh1, dg_mlp = _mm(da, lp["w_ff1"], mode="nt", M=L, N=D_MODEL, K=D_FF, b_cb=True,
                      extras=[(sv["h1"], 0), (dh2, 0), (row("mlp_norm_g"), 0)], epi=_rms_bwd_epi,
                      out_dtypes=[F32, F32], n_acc=1, name="b_dx_ff1")
    d_out = _mm(sv["ycat"], dh1, mode="tn", M=D_MODEL, N=D_MODEL, K=L, out_dtypes=[F32], name="b_dw_out",
                o_stack=stk["w_out"])
    dycat = _mm(dh1, lp["w_out"], mode="nt", M=L, N=D_MODEL, K=D_MODEL, out_dtypes=[F32], name="b_dx_out")
    dya, dyb, dg12, dmix = _merge_bwd(dycat, sv["ya"], sv["yb"], sv["g12"], lp["mix_out_g"], "b_merge")
    d_w12 = _mm(sv["yg"], dg12, mode="tn", M=C_WIDTH, N=2 * C_WIDTH, K=L, tm=C_WIDTH, out_dtypes=[F32], name="b_dw_glu",
                o_stack=stk["w12"])
    dy = _mm(dg12, lp["w12"], mode="nt", M=L, N=C_WIDTH, K=2 * C_WIDTH, tk=2 * C_WIDTH, extras=[(sv["y"], 0)],
             epi=lambda acc, y_: (acc * _gelu_grad(y_),), out_dtypes=[F32], name="b_dx_glu")
    dd = _dskip_bwd(dy, z, "b_dskip")
    dS = _mm(dy, lp["ccat"], mode="nt", M=L, N=2 * N_STATE, K=C_WIDTH, tk=C_WIDTH, out_dtypes=[F32], name="b_dx_ssm_out")
    d_ccat = _mm(sv["S"], dy, mode="tn", M=2 * N_STATE, N=C_WIDTH, K=L, out_dtypes=[F32], name="b_dw_ssm_out")
    G, dar, dai = _ssm_scan(dS, lp["lam_r"], lp["lam_i"], "b_ssm_scan", reverse=True, states=sv["S"])
    d_bcat = _mm(z, G, mode="tn", M=C_WIDTH, N=2 * N_STATE, K=L, a_off=5, tm=C_WIDTH, out_dtypes=[F32], name="b_dw_ssm_in")
    dzc = _mm(G, lp["bcat"], mode="nt", M=L, N=C_WIDTH, K=2 * N_STATE, tk=2 * N_STATE,
              extras=[(dy, 0), (lp["dskip"], 0)], epi=lambda acc, dy_, dsk: (acc + dy_ * dsk,),
              out_dtypes=[_ACT], name="b_dx_ssm_in")
    dq, dkc, dkp, dvc, dvp, dsink = _attn_bwd(sv["q"], sv["k"], z, lp["sinks"], dyb, "b_attn")
    dzq, dzk, dzv, dgq, dgk = _qk_prep_bwd(z, dq, dkc, dkp, dvc, dvp, cos, sin, lp["gq"], lp["gk"], "b_qk_prep")
    dza, dws, dbs, dlg, dlb = _gmlp_bwd(z, dya, lp["ws"], lp["wsT"], lp["bfull"], lp["lgf"], lp["lbf"], "b_gmlp")
    dz = jnp.concatenate([dza, dzq, dzk, dzv, dzc], axis=1)
    d_in = _mm(sv["xn"], dz, mode="tn", M=D_MODEL, N=IN_COLS, K=L, o_cb=True, out_dtypes=[F32], name="b_dw_in",
               o_stack=stk["w_in"])
    dh, dg_attn = _mm(dz, lp["w_in"], mode="nt", M=L, N=D_MODEL, K=IN_COLS, b_cb=True,
                      extras=[(sv["h"], 0), (dh1, 0), (row("attn_norm_g"), 0)], epi=_rms_bwd_epi,
                      out_dtypes=[F32, F32], n_acc=1, name="b_dx_in")
    grads = dict(w_in=d_in, w12=d_w12, w_out=d_out, w_ff1=d_ff1, w_ff2=d_ff2, w_ple_gate=d_gate, w_ple_proj=d_proj,
                 attn_norm_g=dg_attn.reshape(D_MODEL), mlp_norm_g=dg_mlp.reshape(D_MODEL),
                 ple_norm_g=dg_ple.reshape(D_MODEL), mix_out_g=dmix.reshape(D_MODEL),
                 dws=dws, dbs=dbs, dlg=dlg, dlb=dlb, dgq=dgq, dgk=dgk, dsink=dsink,
                 dar=dar, dai=dai, d_bcat=d_bcat, d_ccat=d_ccat, dd=dd)
    return dh, grads


SMALL = ("attn_norm_g", "gmlp_ln_g", "gmlp_ln_b", "gmlp_ws", "gmlp_bs", "q_norm_g", "k_norm_g", "sinks",
         "ssm_a_re", "ssm_a_im", "ssm_log_dt", "ssm_b_re", "ssm_b_im", "ssm_c_re", "ssm_c_im", "ssm_d",
         "mix_out_g", "mlp_norm_g", "ple_norm_g")
BIG = ("w_in", "w12", "w_out", "w_ff1", "w_ff2", "w_ple_gate", "w_ple_proj")
COL_SHARDED = ("w_in", "w_ff1", "w_ple_proj")


def _block_diag(t):
    nl, g, a, b = t.shape
    eye = jnp.eye(g, dtype=t.dtype)
    return (t[:, :, :, None, :] * eye[None, :, None, :, None]).reshape(nl, g * a, g * b)


def _diag_blocks(t, a, b):
    nl = t.shape[0]
    t = t.reshape(nl, C_GROUPS, a, C_GROUPS, b)
    idx = jnp.arange(C_GROUPS)
    return jnp.moveaxis(t[:, idx, :, idx, :], 0, 1)


def _local_step(x, p, positions, target, sw, bw):
    nl = sw["attn_norm_g"].shape[0]
    G = nl * C_GROUPS
    zeros = lambda *s: jnp.zeros(s, F32)
    are = sw["ssm_a_re"].reshape(G, 1, C_STATE)
    aim = sw["ssm_a_im"].reshape(G, 1, C_STATE)
    ldt = jnp.broadcast_to(sw["ssm_log_dt"][..., None], (nl, C_GROUPS, C_STATE)).reshape(G, 1, C_STATE)
    bre = jnp.swapaxes(sw["ssm_b_re"], -1, -2).reshape(G, C_GROUP, C_STATE)
    bim = jnp.swapaxes(sw["ssm_b_im"], -1, -2).reshape(G, C_GROUP, C_STATE)
    lr, li, bbr, bbi = _ssm_prep(are, aim, ldt, bre, bim)
    unflat = lambda t: t.reshape(nl, C_GROUPS, C_GROUP, C_STATE)
    lp = dict(
        attn_norm_g=sw["attn_norm_g"], mlp_norm_g=sw["mlp_norm_g"], ple_norm_g=sw["ple_norm_g"],
        mix_out_g=sw["mix_out_g"], sinks=sw["sinks"],
        ws=sw["gmlp_ws"], wsT=jnp.swapaxes(sw["gmlp_ws"], -1, -2),
        bfull=jnp.concatenate([zeros(nl, A_HEADS, CHUNK, HEAD_DIM),
                               jnp.broadcast_to(sw["gmlp_bs"][..., None], (nl, A_HEADS, CHUNK, HEAD_DIM))], axis=-1),
        lgf=jnp.concatenate([zeros(nl, A_HEADS, HEAD_DIM), sw["gmlp_ln_g"]], axis=-1),
        lbf=jnp.concatenate([zeros(nl, A_HEADS, HEAD_DIM), sw["gmlp_ln_b"]], axis=-1),
        gq=jnp.tile(sw["q_norm_g"], (1, 2)).reshape(nl, 1, LANES),
        gk=jnp.tile(sw["k_norm_g"], (1, 2)).reshape(nl, 1, LANES),
        lam_r=lr.reshape(nl, 1, N_STATE), lam_i=li.reshape(nl, 1, N_STATE),
        bcat=jnp.concatenate([_block_diag(unflat(bbr)), _block_diag(unflat(bbi))], axis=-1),
        ccat=jnp.concatenate([_block_diag(jnp.swapaxes(sw["ssm_c_re"], -1, -2)),
                              -_block_diag(jnp.swapaxes(sw["ssm_c_im"], -1, -2))], axis=1),
        dskip=sw["ssm_d"].reshape(nl, 1, C_WIDTH))
    cos, sin = _rope_tables(positions)

    def layer_params(l, hooks):
        lpi = {n: v[l] for n, v in lp.items()}
        lpi.update(bw.layer(l))
        lpi["p"] = (p, l)
        lpi.update(hooks)
        return lpi

    h, saved = x, []
    xn = _rms_fwd(x, sw["attn_norm_g"][0], "f_norm_attn")
    for l in range(nl):
        g_next = sw["attn_norm_g"][l + 1] if l + 1 < nl else None
        h, xn, sv = _layer_fwd(h, xn, layer_params(l, bw.fwd_hooks(l)), cos, sin, g_next)
        saved.append(sv)
    sse, dh = _loss_fwd_bwd(h, target)

    per_layer = [None] * nl
    for l in reversed(range(nl)):
        dh, gl = _layer_bwd(dh, layer_params(l, bw.bwd_hooks(l)), saved[l], cos, sin)
        bw.grads(l, {n: gl.pop(n) for n in BIG})
        per_layer[l] = gl
    grad_x = dh
    g = {n: jnp.stack([per_layer[l][n] for l in range(nl)]) for n in per_layer[0]}

    d_bcat = g["d_bcat"]
    dbr = _diag_blocks(d_bcat[:, :, :N_STATE], C_GROUP, C_STATE).reshape(G, C_GROUP, C_STATE)
    dbi = _diag_blocks(d_bcat[:, :, N_STATE:], C_GROUP, C_STATE).reshape(G, C_GROUP, C_STATE)
    dlr = g["dar"][:, 0].reshape(G, 1, C_STATE)
    dli = g["dai"][:, 0].reshape(G, 1, C_STATE)
    g_are, g_aim, g_ldt, g_bre, g_bim = _ssm_prep_bwd(are, aim, ldt, bre, bim, dlr, dli, dbr, dbi)
    d_ccat = g["d_ccat"]
    sg = dict(
        attn_norm_g=g["attn_norm_g"], mlp_norm_g=g["mlp_norm_g"], ple_norm_g=g["ple_norm_g"], mix_out_g=g["mix_out_g"],
        gmlp_ln_g=g["dlg"][:, :, 0, HEAD_DIM:], gmlp_ln_b=g["dlb"][:, :, 0, HEAD_DIM:],
        gmlp_ws=g["dws"], gmlp_bs=g["dbs"][:, :, :, HEAD_DIM],
        q_norm_g=g["dgq"][:, 0, :HEAD_DIM], k_norm_g=g["dgk"][:, 0, :HEAD_DIM],
        sinks=g["dsink"][:, 0, :B_Q_HEADS],
        ssm_a_re=g_are.reshape(nl, C_GROUPS, C_STATE), ssm_a_im=g_aim.reshape(nl, C_GROUPS, C_STATE),
        ssm_log_dt=g_ldt[:, 0, 0].reshape(nl, C_GROUPS),
        ssm_b_re=jnp.swapaxes(g_bre.reshape(nl, C_GROUPS, C_GROUP, C_STATE), -1, -2),
        ssm_b_im=jnp.swapaxes(g_bim.reshape(nl, C_GROUPS, C_GROUP, C_STATE), -1, -2),
        ssm_c_re=jnp.swapaxes(_diag_blocks(d_ccat[:, :N_STATE], C_STATE, C_GROUP), -1, -2),
        ssm_c_im=-jnp.swapaxes(_diag_blocks(d_ccat[:, N_STATE:], C_STATE, C_GROUP), -1, -2),
        ssm_d=g["dd"].reshape(nl, C_GROUPS, C_GROUP),
    )
    return sse[0, 0], grad_x, sg, bw.finish()


_ANY = pl.BlockSpec(memory_space=pl.ANY)
N_LAYERS = 4


def _mesh_pos():
    x, y, c = lax.axis_index("x"), lax.axis_index("y"), lax.axis_index("c")
    chips = [(1 - x, y), (x, 1 - y), (1 - x, 1 - y)]
    return x, y, c, 2 * x + y, chips


def _cast_into_slot(ws, j, name):
    nl, R, _ = ws[0].shape
    widths = [w.shape[2] for w in ws]
    C = sum(widths)
    tr = R if R <= 512 else _tile_rows(R, 512)
    nw = len(ws)

    def body(s_ref, *refs):
        o_ref = refs[nw]
        off = 0
        for r, wd in zip(refs[:nw], widths):
            o_ref[:, off:off + wd] = r[...].astype(o_ref.dtype)
            off += wd

    return pl.pallas_call(
        body, name=name,
        grid_spec=pltpu.PrefetchScalarGridSpec(
            num_scalar_prefetch=1, grid=(nl, R // tr),
            in_specs=[pl.BlockSpec((None, tr, wd), lambda l, i, s: (l, i, 0)) for wd in widths],
            out_specs=pl.BlockSpec((None, None, tr, C), lambda l, i, s: (l, s[0], i, 0))),
        out_shape=_sds((nl, N_CHIPS, R, C), _MXU),
    )(jnp.reshape(j, (1,)).astype(jnp.int32), *ws)


def _gather_weights(bufs):
    nk = len(bufs)

    def body(*refs):
        ins, outs = refs[:nk], refs[nk:2 * nk]
        send_sems, recv_sems = refs[2 * nk:]
        x, y, c, j, chips = _mesh_pos()
        mine, other = pl.ds(2 * c, 2), pl.ds(2 * (1 - c), 2)

        def ici(t, q):
            cx, cy = chips[q]
            return pltpu.make_async_remote_copy(
                src_ref=ins[t].at[mine, j], dst_ref=outs[t].at[mine, j],
                send_sem=send_sems.at[6 * t + q], recv_sem=recv_sems.at[6 * t + q],
                device_id=(cx, cy, c), device_id_type=MESH)

        def landed(t, q):
            cx, cy = chips[q]
            blk = outs[t].at[mine, 2 * cx + cy]
            return pltpu.make_async_remote_copy(
                src_ref=blk, dst_ref=blk, send_sem=send_sems.at[6 * t + q], recv_sem=recv_sems.at[6 * t + q],
                device_id=(cx, cy, c), device_id_type=MESH)

        def fwd(t, q, rows):
            cx, cy = chips[q]
            blk = outs[t].at[rows, 2 * cx + cy]
            return pltpu.make_async_remote_copy(
                src_ref=blk, dst_ref=blk, send_sem=send_sems.at[6 * t + 3 + q], recv_sem=recv_sems.at[6 * t + 3 + q],
                device_id=(x, y, 1 - c), device_id_type=MESH)

        for t in range(nk):
            for q in range(3):
                ici(t, q).start()
        for t in range(nk):
            for q in range(3):
                landed(t, q).wait_recv()
                fwd(t, q, mine).start()
        for t in range(nk):
            for q in range(3):
                fwd(t, q, other).wait_recv()
        for t in range(nk):
            for q in range(3):
                ici(t, q).wait_send()
                fwd(t, q, mine).wait_send()

    return pl.pallas_call(
        body, name="gather_weights", in_specs=[_ANY] * nk, out_specs=[_ANY] * nk,
        out_shape=[_sds(b.shape, b.dtype) for b in bufs],
        input_output_aliases={t: t for t in range(nk)},
        scratch_shapes=[pltpu.SemaphoreType.DMA((6 * nk,)), pltpu.SemaphoreType.DMA((6 * nk,))],
    )(*bufs)


def _exchange_sibling_half(gl):
    nk = len(gl)

    def body(*refs):
        ins, outs = refs[:nk], refs[nk:2 * nk]
        send_sems, recv_sems = refs[2 * nk:]
        x, y, c, _, _ = _mesh_pos()
        cps = [pltpu.make_async_remote_copy(
            src_ref=ins[t].at[pl.ds(2 * (1 - c), 2)], dst_ref=outs[t],
            send_sem=send_sems.at[t], recv_sem=recv_sems.at[t],
            device_id=(x, y, 1 - c), device_id_type=MESH) for t in range(nk)]
        for cp in cps:
            cp.start()
        for cp in cps:
            cp.wait()

    return pl.pallas_call(
        body, name="reduce_sibling", in_specs=[_ANY] * nk, out_specs=[_ANY] * nk,
        out_shape=[_sds((2,) + g.shape[1:], g.dtype) for g in gl],
        scratch_shapes=[pltpu.SemaphoreType.DMA((nk,)), pltpu.SemaphoreType.DMA((nk,))],
    )(*gl)


def _exchange_chips(ps):
    nk = len(ps)

    def body(*refs):
        ins, outs = refs[:nk], refs[nk:2 * nk]
        send_sems, recv_sems = refs[2 * nk:]
        x, y, c, j, chips = _mesh_pos()

        def send(t, q):
            cx, cy = chips[q]
            return pltpu.make_async_remote_copy(
                src_ref=ins[t].at[:, 2 * cx + cy], dst_ref=outs[t].at[j],
                send_sem=send_sems.at[3 * t + q], recv_sem=recv_sems.at[3 * t + q],
                device_id=(cx, cy, c), device_id_type=MESH)

        def landed(t, q):
            cx, cy = chips[q]
            blk = outs[t].at[2 * cx + cy]
            return pltpu.make_async_remote_copy(
                src_ref=blk, dst_ref=blk, send_sem=send_sems.at[3 * t + q], recv_sem=recv_sems.at[3 * t + q],
                device_id=(cx, cy, c), device_id_type=MESH)

        for t in range(nk):
            for q in range(3):
                send(t, q).start()
        for t in range(nk):
            for q in range(3):
                landed(t, q).wait_recv()
        for t in range(nk):
            for q in range(3):
                send(t, q).wait_send()

    return pl.pallas_call(
        body, name="reduce_chips", in_specs=[_ANY] * nk, out_specs=[_ANY] * nk,
        out_shape=[_sds((N_CHIPS, 2) + p.shape[2:], p.dtype) for p in ps],
        scratch_shapes=[pltpu.SemaphoreType.DMA((3 * nk,)), pltpu.SemaphoreType.DMA((3 * nk,))],
    )(*ps)


def _share_sibling(fs):
    nk = len(fs)

    def body(*refs):
        ins, outs = refs[:nk], refs[nk:2 * nk]
        send_sems, recv_sems = refs[2 * nk:]
        x, y, c, _, _ = _mesh_pos()
        mine = pl.ds(2 * c, 2)
        cps = [pltpu.make_async_remote_copy(
            src_ref=ins[t].at[mine], dst_ref=outs[t].at[mine], send_sem=send_sems.at[t], recv_sem=recv_sems.at[t],
            device_id=(x, y, 1 - c), device_id_type=MESH) for t in range(nk)]
        for cp in cps:
            cp.start()
        for cp in cps:
            cp.wait_send()
        for t in range(nk):
            blk = outs[t].at[pl.ds(2 * (1 - c), 2)]
            pltpu.make_async_remote_copy(
                src_ref=blk, dst_ref=blk, send_sem=send_sems.at[t], recv_sem=recv_sems.at[t],
                device_id=(x, y, 1 - c), device_id_type=MESH).wait_recv()

    return pl.pallas_call(
        body, name="share_sibling", in_specs=[_ANY] * nk, out_specs=[_ANY] * nk,
        out_shape=[_sds(f.shape, f.dtype) for f in fs],
        input_output_aliases={t: t for t in range(nk)},
        scratch_shapes=[pltpu.SemaphoreType.DMA((nk,)), pltpu.SemaphoreType.DMA((nk,))],
    )(*fs)


def _add_own_half(gl, r1, c, name):
    _, ns, R, C = gl.shape
    rows = 2 * ns * R
    tr = _tile_rows(rows, 512)
    nblk = rows // tr

    def body(s_ref, a_ref, b_ref, o_ref):
        o_ref[...] = (a_ref[...] + b_ref[...]).astype(o_ref.dtype)

    out = pl.pallas_call(
        body, name=name,
        grid_spec=pltpu.PrefetchScalarGridSpec(
            num_scalar_prefetch=1, grid=(nblk,),
            in_specs=[pl.BlockSpec((tr, C), lambda i, s: (s[0] * nblk + i, 0)), pl.BlockSpec((tr, C), lambda i, s: (i, 0))],
            out_specs=pl.BlockSpec((tr, C), lambda i, s: (i, 0))),
        out_shape=_sds((rows, C), _WIRE),
        compiler_params=_cparams(3 * _nbytes((tr, C), F32)),
    )(jnp.reshape(c, (1,)).astype(jnp.int32), gl.reshape(2 * rows, C), r1.reshape(rows, C))
    return out.reshape(2, ns, R, C)


def _add_chips(p, r2, j, c, name):
    _, ns, R, C = p.shape
    tr = R if R <= 512 else _tile_rows(R, 512)

    def body(s_ref, own, a1, a2, a3, o_ref):
        f = lambda r: r[...].astype(F32)
        o_ref[...] = ((f(own) + f(a1)) + f(a2)) + f(a3)

    blk = (None, None, tr, C)
    return pl.pallas_call(
        body, name=name,
        grid_spec=pltpu.PrefetchScalarGridSpec(
            num_scalar_prefetch=1, grid=(2, R // tr),
            in_specs=[pl.BlockSpec(blk, lambda h, i, s: (h, s[0], i, 0))]
            + [pl.BlockSpec(blk, lambda h, i, s, k=k: ((s[0] + k) % N_CHIPS, h, i, 0)) for k in (1, 2, 3)],
            out_specs=pl.BlockSpec((None, tr, C), lambda h, i, s: (2 * s[1] + h, i, 0))),
        out_shape=_sds((N_LAYERS, R, C), F32),
        compiler_params=_cparams(6 * _nbytes((tr, C), F32)),
    )(jnp.stack([j, c]).astype(jnp.int32), p, r2, r2, r2)


def _allreduce_small(buf):
    Rs = buf.shape[0]

    def body(b_ref, o_ref, t_ref, slots_ref, send_sems, recv_sems):
        x, y, c, j, chips = _mesh_pos()
        sib = pltpu.make_async_remote_copy(
            src_ref=b_ref, dst_ref=t_ref, send_sem=send_sems.at[0], recv_sem=recv_sems.at[0],
            device_id=(x, y, 1 - c), device_id_type=MESH)
        sib.start()
        sib.wait()
        slots_ref[j] = b_ref[...] + t_ref[...]

        def send(q):
            cx, cy = chips[q]
            return pltpu.make_async_remote_copy(
                src_ref=slots_ref.at[j], dst_ref=slots_ref.at[j], send_sem=send_sems.at[1 + q],
                recv_sem=recv_sems.at[1 + q], device_id=(cx, cy, c), device_id_type=MESH)

        def landed(q):
            cx, cy = chips[q]
            blk = slots_ref.at[2 * cx + cy]
            return pltpu.make_async_remote_copy(
                src_ref=blk, dst_ref=blk, send_sem=send_sems.at[1 + q], recv_sem=recv_sems.at[1 + q],
                device_id=(cx, cy, c), device_id_type=MESH)

        for q in range(3):
            send(q).start()
        for q in range(3):
            landed(q).wait_recv()
        for q in range(3):
            send(q).wait_send()
        o_ref[...] = ((slots_ref[0] + slots_ref[1]) + slots_ref[2]) + slots_ref[3]

    vm = pl.BlockSpec(memory_space=pltpu.VMEM)
    return pl.pallas_call(
        body, name="allreduce_small", in_specs=[vm], out_specs=vm,
        out_shape=_sds((Rs, LANES), F32),
        scratch_shapes=[pltpu.VMEM((Rs, LANES), F32), pltpu.VMEM((N_CHIPS, Rs, LANES), F32),
                        pltpu.SemaphoreType.DMA((4,)), pltpu.SemaphoreType.DMA((4,))],
        compiler_params=_cparams(4 * _nbytes((Rs, LANES), F32)),
    )(buf)


def _own_rows(c, R):
    return pl.ds(c * (R // 2), R // 2)


def _cast_layer_slot(ws, l, j, name):
    _, R, _ = ws[0].shape
    widths = [w.shape[2] for w in ws]
    C = sum(widths)
    tr = R if R <= 512 else _tile_rows(R, 512)
    nw = len(ws)

    def body(s_ref, *refs):
        o_ref = refs[nw]
        off = 0
        for r, wd in zip(refs[:nw], widths):
            o_ref[:, off:off + wd] = r[...].astype(o_ref.dtype)
            off += wd

    return pl.pallas_call(
        body, name=name,
        grid_spec=pltpu.PrefetchScalarGridSpec(
            num_scalar_prefetch=1, grid=(R // tr,),
            in_specs=[pl.BlockSpec((None, tr, wd), lambda i, s: (l, i, 0)) for wd in widths],
            out_specs=pl.BlockSpec((None, tr, C), lambda i, s: (s[0], i, 0))),
        out_shape=_sds((N_CHIPS, R, C), _MXU),
    )(jnp.reshape(j, (1,)).astype(jnp.int32), *ws)


def _gather_ici(bufs, done):
    nk = len(bufs)

    def copy(ins, outs, ss, rs, t, q, landed):
        x, y, c, j, chips = _mesh_pos()
        cx, cy = chips[q]
        rows = _own_rows(c, ins[t].shape[1])
        src = outs[t].at[2 * cx + cy, rows] if landed else ins[t].at[j, rows]
        dst = outs[t].at[2 * cx + cy, rows] if landed else outs[t].at[j, rows]
        return pltpu.make_async_remote_copy(src_ref=src, dst_ref=dst, send_sem=ss.at[3 * t + q], recv_sem=rs.at[3 * t + q],
                                            device_id=(cx, cy, c), device_id_type=MESH)

    def start(ins, outs, ss, rs):
        for t in range(nk):
            for q in range(3):
                copy(ins, outs, ss, rs, t, q, False).start()

    def wait(ins, outs, ss, rs):
        for t in range(nk):
            for q in range(3):
                copy(ins, outs, ss, rs, t, q, True).wait_recv()
                copy(ins, outs, ss, rs, t, q, False).wait_send()

    return _Exchange(bufs, [True] * nk, [], 3 * nk, start, wait, done)


def _gather_d2d(bufs, done):
    nk = len(bufs)

    def copy(ins, outs, ss, rs, t, q, mine):
        x, y, c, j, chips = _mesh_pos()
        cx, cy = chips[q]
        rows = _own_rows(c if mine else 1 - c, ins[t].shape[1])
        src = (ins if mine else outs)[t].at[2 * cx + cy, rows]
        return pltpu.make_async_remote_copy(src_ref=src, dst_ref=outs[t].at[2 * cx + cy, rows],
                                            send_sem=ss.at[3 * t + q], recv_sem=rs.at[3 * t + q],
                                            device_id=(x, y, 1 - c), device_id_type=MESH)

    def start(ins, outs, ss, rs):
        for t in range(nk):
            for q in range(3):
                copy(ins, outs, ss, rs, t, q, True).start()

    def wait(ins, outs, ss, rs):
        for t in range(nk):
            for q in range(3):
                copy(ins, outs, ss, rs, t, q, False).wait_recv()
                copy(ins, outs, ss, rs, t, q, True).wait_send()

    return _Exchange(bufs, [True] * nk, [], 3 * nk, start, wait, done)


def _reduce_d2d(gl, done):
    nk = len(gl)

    def copy(ins, outs, ss, rs, t):
        x, y, c, _, _ = _mesh_pos()
        return pltpu.make_async_remote_copy(
            src_ref=ins[t].at[:, _own_rows(1 - c, ins[t].shape[1])], dst_ref=outs[t],
            send_sem=ss.at[t], recv_sem=rs.at[t], device_id=(x, y, 1 - c), device_id_type=MESH)

    def start(ins, outs, ss, rs):
        for t in range(nk):
            copy(ins, outs, ss, rs, t).start()

    def wait(ins, outs, ss, rs):
        for t in range(nk):
            copy(ins, outs, ss, rs, t).wait()

    fresh = [((N_CHIPS, g.shape[1] // 2, g.shape[2]), g.dtype) for g in gl]
    return _Exchange(gl, [False] * nk, fresh, nk, start, wait, done)


def _reduce_ici(ps, done):
    nk = len(ps)

    def copy(ins, outs, ss, rs, t, q, landed):
        x, y, c, j, chips = _mesh_pos()
        cx, cy = chips[q]
        src = outs[t].at[2 * cx + cy] if landed else ins[t].at[2 * cx + cy]
        dst = outs[t].at[2 * cx + cy] if landed else outs[t].at[j]
        return pltpu.make_async_remote_copy(src_ref=src, dst_ref=dst, send_sem=ss.at[3 * t + q], recv_sem=rs.at[3 * t + q],
                                            device_id=(cx, cy, c), device_id_type=MESH)

    def start(ins, outs, ss, rs):
        for t in range(nk):
            for q in range(3):
                copy(ins, outs, ss, rs, t, q, False).start()

    def wait(ins, outs, ss, rs):
        for t in range(nk):
            for q in range(3):
                copy(ins, outs, ss, rs, t, q, True).wait_recv()
                copy(ins, outs, ss, rs, t, q, False).wait_send()

    return _Exchange(ps, [False] * nk, [(p_.shape, p_.dtype) for p_ in ps], 3 * nk, start, wait, done)


def _share_d2d(fs, done):
    nk = len(fs)

    def copy(ins, outs, ss, rs, t, mine):
        x, y, c, _, _ = _mesh_pos()
        rows = _own_rows(c if mine else 1 - c, ins[t].shape[1])
        src = (ins if mine else outs)[t].at[:, rows]
        return pltpu.make_async_remote_copy(src_ref=src, dst_ref=outs[t].at[:, rows], send_sem=ss.at[t], recv_sem=rs.at[t],
                                            device_id=(x, y, 1 - c), device_id_type=MESH)

    def start(ins, outs, ss, rs):
        for t in range(nk):
            copy(ins, outs, ss, rs, t, True).start()

    def wait(ins, outs, ss, rs):
        for t in range(nk):
            copy(ins, outs, ss, rs, t, False).wait_recv()
            copy(ins, outs, ss, rs, t, True).wait_send()

    return _Exchange(fs, [True] * nk, [], nk, start, wait, done)


def _run_exchange(plan, name):
    nin = len(plan.ins)
    out_shape = [_sds(x_.shape, x_.dtype) for x_, al in zip(plan.ins, plan.aliased) if al]
    aliases, k = {}, 0
    for t, al in enumerate(plan.aliased):
        if al:
            aliases[t] = k
            k += 1
    out_shape += [_sds(sh, dt) for sh, dt in plan.fresh]
    nout = len(out_shape)

    def body(*refs):
        ins, outs, sems = refs[:nin], refs[nin:nin + nout], refs[nin + nout:]
        plan.start(ins, outs, *sems)
        plan.wait(ins, outs, *sems)

    outs = pl.pallas_call(
        body, name=name, in_specs=[_ANY] * nin, out_specs=[_ANY] * nout, out_shape=out_shape,
        input_output_aliases=aliases,
        scratch_shapes=[pltpu.SemaphoreType.DMA((plan.n_sems,))] * 2,
    )(*plan.ins)
    plan.done(list(outs))


def _add_sibling_rows(g, r1, c, name):
    ns, R, C = g.shape
    hr = R // 2
    tr = hr if hr <= 512 else _tile_rows(hr, 512)
    nblk = hr // tr

    def body(s_ref, a_ref, b_ref, o_ref):
        o_ref[...] = (a_ref[...] + b_ref[...]).astype(o_ref.dtype)

    blk = (None, tr, C)
    return pl.pallas_call(
        body, name=name,
        grid_spec=pltpu.PrefetchScalarGridSpec(
            num_scalar_prefetch=1, grid=(ns, nblk),
            in_specs=[pl.BlockSpec(blk, lambda s_, i, s: (s_, s[0] * nblk + i, 0)), pl.BlockSpec(blk, lambda s_, i, s: (s_, i, 0))],
            out_specs=pl.BlockSpec(blk, lambda s_, i, s: (s_, i, 0))),
        out_shape=_sds((ns, hr, C), _WIRE),
        compiler_params=_cparams(3 * _nbytes((tr, C), F32)),
    )(jnp.reshape(c, (1,)).astype(jnp.int32), g, r1)


def _add_chip_rows(p_, r2, f, l, j, c, name):
    _, hr, C = p_.shape
    tr = hr if hr <= 512 else _tile_rows(hr, 512)
    nblk = hr // tr

    def body(s_ref, own, a1, a2, a3, f_ref, o_ref):
        v = lambda r: r[...].astype(F32)
        o_ref[...] = ((v(own) + v(a1)) + v(a2)) + v(a3)

    blk = (None, tr, C)
    return pl.pallas_call(
        body, name=name,
        grid_spec=pltpu.PrefetchScalarGridSpec(
            num_scalar_prefetch=1, grid=(nblk,),
            in_specs=[pl.BlockSpec(blk, lambda i, s: (s[0], i, 0))]
            + [pl.BlockSpec(blk, lambda i, s, k=k: ((s[0] + k) % N_CHIPS, i, 0)) for k in (1, 2, 3)]
            + [pl.BlockSpec(memory_space=pl.ANY)],
            out_specs=pl.BlockSpec(blk, lambda i, s: (l, s[1] * nblk + i, 0))),
        out_shape=_sds(f.shape, F32),
        input_output_aliases={5: 0},
        compiler_params=_cparams(6 * _nbytes((tr, C), F32)),
    )(jnp.stack([j, c]).astype(jnp.int32), p_, r2, r2, r2, f)


class _ShardedWeights:
    def __init__(self, a, j, c):
        self.j, self.c = j, c
        shards = dict(w_in=[a["w_in"]], w12=[a["glu_w1"], a["glu_w2"]], w_out=[a["w_out"]], w_ff1=[a["w_ff1"]],
                      w_ff2=[a["w_ff2"]], w_ple_gate=[a["w_ple_gate"]], w_ple_proj=[a["w_ple_proj"]])
        self.bufs = [[_cast_layer_slot(shards[n], l, j, "cast_%s_%d" % (n, l)) for n in BIG] for l in range(N_LAYERS)]
        _run_exchange(_gather_ici(self.bufs[0], lambda o: self._set_bufs(0, o)), "gather_ici_0")
        _run_exchange(_gather_d2d(self.bufs[0], lambda o: self._set_bufs(0, o)), "gather_d2d_0")
        self.raw = None
        self.pending = None
        self.final = [lax.empty((N_LAYERS,) + b.shape[1:], F32) for b in self.bufs[0]]

    def _set_bufs(self, l, outs):
        self.bufs[l] = outs

    def layer(self, l):
        return {n: (b if n in COL_SHARDED else b.reshape(N_CHIPS * b.shape[1], b.shape[2]))
                for n, b in zip(BIG, self.bufs[l])}

    _FIRST, _SECOND = (0, 3), (1, 2, 4, 5, 6)

    def _gather_part(self, nxt, idx):
        def done(outs):
            for i, o in zip(idx, outs):
                self.bufs[nxt][i] = o
        return _gather_ici([self.bufs[nxt][i] for i in idx], done)

    def fwd_hooks(self, l):
        if l + 1 == N_LAYERS:
            return {}
        nxt = l + 1
        return dict(x_ff1=lambda: self._gather_part(nxt, self._FIRST),
                    x_ff2=lambda: self._gather_part(nxt, self._SECOND),
                    x_gate=lambda: _gather_d2d(self.bufs[nxt], lambda o: self._set_bufs(nxt, o)))

    def _sibling_done(self, lyr, gl, got):
        ps = [_add_sibling_rows(g_, r1, self.c, "reduce_add_sibling_%s_%d" % (n, lyr)) for g_, r1, n in zip(gl, got, BIG)]
        self.pending = (lyr, ps)

    def _reduce_part(self, idx):
        lyr, ps = self.pending

        def done(r2):
            for i, r in zip(idx, r2):
                self.final[i] = _add_chip_rows(ps[i], r, self.final[i], lyr, self.j, self.c,
                                               "reduce_add_chips_%s_%d" % (BIG[i], lyr))
        return _reduce_ici([ps[i] for i in idx], done)

    def bwd_hooks(self, l):
        if self.raw is None:
            return {}
        lyr, gl = self.raw
        self.raw = None
        return dict(x_bwd0=lambda: _reduce_d2d(gl, lambda got: self._sibling_done(lyr, gl, got)),
                    x_bwd=lambda: self._reduce_part(self._FIRST),
                    x_bwd2=lambda: self._reduce_part(self._SECOND))

    def grads(self, l, g):
        gl = [g[n] if n in COL_SHARDED else g[n].reshape(N_CHIPS, g[n].shape[0] // N_CHIPS, g[n].shape[1]) for n in BIG]
        self.raw = (l, gl)

    def finish(self):
        lyr, gl = self.raw
        _run_exchange(_reduce_d2d(gl, lambda got: self._sibling_done(lyr, gl, got)), "reduce_d2d_%d" % lyr)
        _run_exchange(self._reduce_part(self._FIRST + self._SECOND), "reduce_ici_%d" % lyr)
        out = []
        _run_exchange(_share_d2d(self.final, out.extend), "share_d2d")
        return dict(zip(BIG, out))


def _rows_of(shape):
    return -(-int(np.prod(shape)) // (SUBLANES * LANES)) * SUBLANES


def _pack(d):
    parts = []
    for n in SMALL:
        flat = d[n].reshape(-1)
        parts.append(jnp.pad(flat, (0, _rows_of(flat.shape) * LANES - flat.shape[0])).reshape(-1, LANES))
    return jnp.concatenate(parts, axis=0)


def _unpack(buf, like):
    out, r0 = {}, 0
    for n in SMALL:
        shape = like[n].shape
        size, nr = int(np.prod(shape)), _rows_of(shape)
        piece = lax.optimization_barrier(buf[r0:r0 + nr])
        out[n] = piece.reshape(-1)[:size].reshape(shape)
        r0 += nr
    return out


ARGS = ("x", "p", "positions", "attn_norm_g", "w_in", "gmlp_ln_g", "gmlp_ln_b", "gmlp_ws", "gmlp_bs", "q_norm_g",
        "k_norm_g", "sinks", "ssm_a_re", "ssm_a_im", "ssm_log_dt", "ssm_b_re", "ssm_b_im", "ssm_c_re", "ssm_c_im",
        "ssm_d", "glu_w1", "glu_w2", "mix_out_g", "w_out", "mlp_norm_g", "w_ff1", "w_ff2", "ple_norm_g", "w_ple_gate",
        "w_ple_proj")
WEIGHTS = ARGS[3:]


def kernel(x, p, positions, attn_norm_g, w_in, gmlp_ln_g, gmlp_ln_b, gmlp_ws, gmlp_bs, q_norm_g, k_norm_g, sinks, ssm_a_re, ssm_a_im, ssm_log_dt, ssm_b_re, ssm_b_im, ssm_c_re, ssm_c_im, ssm_d, glu_w1, glu_w2, mix_out_g, w_out, mlp_norm_g, w_ff1, w_ff2, ple_norm_g, w_ple_gate, w_ple_proj, loss_target, m_attn_norm_g, m_w_in, m_gmlp_ln_g, m_gmlp_ln_b, m_gmlp_ws, m_gmlp_bs, m_q_norm_g, m_k_norm_g, m_sinks, m_ssm_a_re, m_ssm_a_im, m_ssm_log_dt, m_ssm_b_re, m_ssm_b_im, m_ssm_c_re, m_ssm_c_im, m_ssm_d, m_glu_w1, m_glu_w2, m_mix_out_g, m_w_out, m_mlp_norm_g, m_w_ff1, m_w_ff2, m_ple_norm_g, m_w_ple_gate, m_w_ple_proj, v_attn_norm_g, v_w_in, v_gmlp_ln_g, v_gmlp_ln_b, v_gmlp_ws, v_gmlp_bs, v_q_norm_g, v_k_norm_g, v_sinks, v_ssm_a_re, v_ssm_a_im, v_ssm_log_dt, v_ssm_b_re, v_ssm_b_im, v_ssm_c_re, v_ssm_c_im, v_ssm_d, v_glu_w1, v_glu_w2, v_mix_out_g, v_w_out, v_mlp_norm_g, v_w_ff1, v_w_ff2, v_ple_norm_g, v_w_ple_gate, v_w_ple_proj):
    a = dict(locals())
    L = a["x"].shape[1]
    nl = N_LAYERS
    c = lax.axis_index("c")
    j = 2 * lax.axis_index("x") + lax.axis_index("y")

    sw = {n: a[n] for n in SMALL}
    sse, gx, sg, big_grads = _local_step(a["x"].reshape(L, D_MODEL), a["p"].reshape(nl, L, PLE_DIM),
                                         a["positions"].reshape(L), a["loss_target"].reshape(L, D_MODEL), sw,
                                         _ShardedWeights(a, j, c))
    loss = lax.psum(sse * (0.5 / D_MODEL), ("x", "y", "c"))
    g12 = big_grads.pop("w12")
    big_grads["glu_w1"], big_grads["glu_w2"] = g12[:, :, :C_WIDTH], g12[:, :, C_WIDTH:]

    small_grads = _unpack(_allreduce_small(_pack(sg)), sw)

    grads, delta, new_m, new_v = {}, {}, {}, {}
    d_s, m_s, v_s = _adamw(_pack(sw), _pack(small_grads), _pack({n: a["m_" + n] for n in SMALL}),
                           _pack({n: a["v_" + n] for n in SMALL}), "adamw_small")
    grads.update(small_grads)
    delta.update(_unpack(d_s, sw))
    new_m.update(_unpack(m_s, sw))
    new_v.update(_unpack(v_s, sw))
    for n, g in big_grads.items():
        shp = a[n].shape
        two_d = lambda t: t.reshape(shp[0] * shp[1], shp[2])
        d, m, v = _adamw(two_d(a[n]), two_d(g), two_d(a["m_" + n]), two_d(a["v_" + n]), "adamw_" + n)
        grads[n], delta[n], new_m[n], new_v[n] = g, d.reshape(shp), m.reshape(shp), v.reshape(shp)

    return (loss, gx.reshape(1, L, D_MODEL), *[grads[n] for n in WEIGHTS], *[delta[n] for n in WEIGHTS],
            *[new_m[n] for n in WEIGHTS], *[new_v[n] for n in WEIGHTS])
```

```python
import functools
import math

import numpy as np
import jax
import jax.numpy as jnp
from jax import lax
from jax.experimental import pallas as pl
from jax.experimental.pallas import tpu as pltpu

F32 = jnp.float32
_MXU = jnp.bfloat16
_ACT = jnp.bfloat16
_WIRE = jnp.bfloat16

D_MODEL = 1024
HEAD_DIM = 64
A_HEADS = 4
CHUNK = 128
B_Q_HEADS = 8
B_KV_HEADS = 2
B_GROUP = 4
WINDOW = 128
ROPE_THETA = 10000.0
C_WIDTH = 256
C_GROUP = 16
C_GROUPS = 16
C_STATE = 64
N_STATE = C_GROUPS * C_STATE
IN_A, IN_Q, IN_KV, IN_C = 512, 512, 128, 256
IN_COLS = 1536
D_FF = 4096
PLE_DIM = 256
EPS = 1e-6
NEG = -1e30
ADAM_LR, ADAM_B1, ADAM_B2, ADAM_EPS, ADAM_WD, ADAM_STEP = 0.001, 0.9, 0.999, 1e-08, 0.01, 10

LANES = 128
SUBLANES = 8
VMEM_BYTES = 64 * 2 ** 20
N_CHIPS = 4
MESH = pl.DeviceIdType.MESH


_MM_VMEM_BUDGET = 50 * 2 ** 20
_EPI_ROWS = 256


def _vmem_limit(est_bytes):
    return int(min(max(2 * est_bytes + (8 << 20), 32 << 20), VMEM_BYTES - (6 << 20)))


def _cparams(est_bytes, **kw):
    return pltpu.CompilerParams(vmem_limit_bytes=_vmem_limit(est_bytes), **kw)


def _sds(shape, dtype):
    return pltpu.HBM(tuple(shape), dtype)


def _hbm(x):
    return pltpu.with_memory_space_constraint(x, pltpu.HBM) if x.size >= (1 << 20) else x


def _nbytes(shape, dtype):
    return int(np.prod(shape)) * jnp.dtype(dtype).itemsize


def _tile(dim, pref):
    t = min(dim, pref)
    while dim % t:
        t -= LANES
    assert t > 0, (dim, pref)
    return t


def _lane(shape):
    return lax.broadcasted_iota(jnp.int32, shape, len(shape) - 1)


def _row(shape):
    return lax.broadcasted_iota(jnp.int32, shape, len(shape) - 2)


def _gelu(x):
    c = math.sqrt(2.0 / math.pi)
    return 0.5 * x * (1.0 + jnp.tanh(c * (x + 0.044715 * (x * x * x))))


def _gelu_grad(x):
    c = math.sqrt(2.0 / math.pi)
    t = jnp.tanh(c * (x + 0.044715 * (x * x * x)))
    return 0.5 * (1.0 + t) + 0.5 * x * (1.0 - t * t) * (c * (1.0 + 3.0 * 0.044715 * (x * x)))


def _sigmoid(x):
    return 1.0 / (1.0 + jnp.exp(-x))


def _dot(a, b, dims=(((1,), (0,)), ((), ()))):
    return lax.dot_general(a.astype(_MXU), b.astype(_MXU), dims, preferred_element_type=F32)


_NT = (((1,), (1,)), ((), ()))
_TN = (((0,), (0,)), ((), ()))
_NN = (((1,), (0,)), ((), ()))


def _mm(a, b, *, mode, M, N, K, out_dtypes, name, epi=None, extras=(), b_cb=False, o_cb=False,
        a_off=0, b_off=0, tm=1024, tn=1024, tk=1024, a_lyr=None, b_lyr=None, o_stack=None, n_acc=0, comm=None):
    if isinstance(a, tuple):
        a, a_lyr = a
    if isinstance(b, tuple):
        b, b_lyr = b
    if b_cb or o_cb:
        nc = (b.shape[-1] if b_cb else N // N_CHIPS)
    tn_nom = nc if ((mode == "nn" and b_cb) or (mode == "tn" and o_cb)) else _tile(N, tn)
    tk_nom = nc if (mode == "nt" and b_cb) else _tile(K, tk)
    item = lambda d: jnp.dtype(d).itemsize
    per_row = tk_nom * item(a.dtype) + tn_nom * (sum(item(d) for d in out_dtypes)
                                                   + sum(item(e.dtype) for e, _ in extras if e.shape[0] > 1))
    fixed = tk_nom * tn_nom * item(b.dtype)
    tm = _tile(M, tm)
    while tm > 256 and M % (tm // 2) == 0 and 2 * (tm * per_row + fixed) + 8 * tm * tn_nom > _MM_VMEM_BUDGET:
        tm //= 2

    def spec(block, imap, lyr=None):
        if lyr is None:
            return pl.BlockSpec(block, imap)
        return pl.BlockSpec((None,) + block, lambda i, j, k: (lyr,) + imap(i, j, k))

    if mode == "nn":
        if b_cb:
            tn = nc
        tm, tn, tk = _tile(M, tm), _tile(N, tn), _tile(K, tk)
        a_spec = spec((tm, tk), lambda i, j, k: (i, k + a_off), a_lyr)
        if b_cb:
            b_spec = spec((None, tk, tn), lambda i, j, k: (j, k, 0), b_lyr)
        else:
            b_spec = spec((tk, tn), lambda i, j, k: (k, j + b_off), b_lyr)
        dims = _NN
        a_blk, b_blk = (tm, tk), (tk, tn)
    elif mode == "nt":
        if b_cb:
            tk = nc
        tm, tn, tk = _tile(M, tm), _tile(N, tn), _tile(K, tk)
        a_spec = spec((tm, tk), lambda i, j, k: (i, k + a_off), a_lyr)
        if b_cb:
            b_spec = spec((None, tn, tk), lambda i, j, k: (k, j, 0), b_lyr)
        else:
            b_spec = spec((tn, tk), lambda i, j, k: (j, k + b_off), b_lyr)
        dims = _NT
        a_blk, b_blk = (tm, tk), (tn, tk)
    else:
        if o_cb:
            tn = nc
        tm, tn, tk = _tile(M, tm), _tile(N, tn), _tile(K, tk)
        a_spec = spec((tk, tm), lambda i, j, k: (k, i + a_off), a_lyr)
        b_spec = spec((tk, tn), lambda i, j, k: (k, j + b_off), b_lyr)
        dims = _TN
        a_blk, b_blk = (tk, tm), (tk, tn)
    gi, gj, gk = M // tm, N // tn, K // tk
    o_lyr = None if o_stack is None else o_stack[1]
    if o_cb:
        o_spec = spec((None, tm, tn), lambda i, j, k: (j, i, 0), o_lyr)
        o_shape = (gj, M, tn)
    else:
        o_spec = spec((tm, tn), lambda i, j, k: (i, j), o_lyr)
        o_shape = (M, N)
    e_specs = []
    for e, off in extras:
        if e.shape[0] == 1:
            e_specs.append(pl.BlockSpec((1, tn), lambda i, j, k, off=off: (0, j + off)))
        else:
            e_specs.append(pl.BlockSpec((tm, tn), lambda i, j, k, off=off: (i, j + off)))
    extras = [e for e, _ in extras]
    ne, no = len(extras), len(out_dtypes)
    operands = [_hbm(t) for t in (a, b, *extras)]
    in_specs = [a_spec, b_spec] + e_specs
    out_shape = [_sds(o_shape, d) for d in out_dtypes]
    aliases = {}
    if o_stack is not None:
        assert no == 1 and o_stack[0].shape[1:] == o_shape and o_stack[0].dtype == out_dtypes[0]
        operands.append(_hbm(o_stack[0]))
        in_specs.append(pl.BlockSpec(memory_space=pl.ANY))
        out_shape = [_sds(o_stack[0].shape, o_stack[0].dtype)]
        aliases = {len(operands) - 1: 0}
    out_specs = [o_spec] * no
    if n_acc:
        assert gj == 1 and o_stack is None
        out_specs[no - n_acc:] = [pl.BlockSpec((1, tn), lambda i, j, k: (0, 0))] * n_acc
        out_shape[no - n_acc:] = [_sds((1, N), d) for d in out_dtypes[no - n_acc:]]
    nx_in = nx_out = 0
    if comm is not None:
        nx_in, ncin0 = len(comm.ins), len(operands)
        operands += list(comm.ins)
        in_specs += [pl.BlockSpec(memory_space=pl.ANY)] * nx_in
        for t, x_ in enumerate(comm.ins):
            if comm.aliased[t]:
                aliases[ncin0 + t] = len(out_shape)
                out_shape.append(_sds(x_.shape, x_.dtype))
        out_shape += [_sds(sh, dt) for sh, dt in comm.fresh]
        nx_out = len(out_shape) - no
        out_specs += [pl.BlockSpec(memory_space=pl.ANY)] * nx_out
    nin = len(operands)

    def body(*refs):
        a_ref, b_ref = refs[0], refs[1]
        e_refs = refs[2:2 + ne]
        o_refs = refs[nin:nin + no]
        first_rows = pl.program_id(0) == 0
        if comm is not None:
            x_ins = refs[nin - nx_in:nin]
            x_outs = refs[nin + no:nin + no + nx_out]
            sems = refs[nin + no + nx_out:nin + no + nx_out + 2]
            pid = [pl.program_id(d) for d in range(3)]

            @pl.when((pid[0] == 0) & (pid[1] == 0) & (pid[2] == 0))
            def _():
                comm.start(x_ins, x_outs, *sems)

        def fin(acc):
            for t in range(no - n_acc, no):
                @pl.when(first_rows)
                def _():
                    o_refs[t][...] = jnp.zeros_like(o_refs[t])

            rc = _EPI_ROWS if (epi is not None and tm % _EPI_ROWS == 0) else tm
            for c0 in range(0, tm, rc):
                rows = slice(c0, c0 + rc)
                ex = [e[...] if e.shape[0] == 1 else e[rows, :] for e in e_refs]
                vals = epi(acc[rows, :], *ex) if epi is not None else (acc[rows, :],)
                for t, (o, v) in enumerate(zip(o_refs, vals)):
                    if t < no - n_acc:
                        o[rows, :] = v.astype(o.dtype)
                    else:
                        o[...] += v.astype(o.dtype)

        prod = _dot(a_ref[...], b_ref[...], dims)
        if gk == 1:
            fin(prod)
        else:
            acc_ref = refs[-1]
            k = pl.program_id(2)

            @pl.when(k == 0)
            def _():
                acc_ref[...] = prod

            @pl.when(k > 0)
            def _():
                acc_ref[...] += prod

            @pl.when(k == gk - 1)
            def _():
                fin(acc_ref)

        if comm is not None:
            @pl.when((pid[0] == gi - 1) & (pid[1] == gj - 1) & (pid[2] == gk - 1))
            def _():
                comm.wait(x_ins, x_outs, *sems)

    est = (_nbytes(a_blk, a.dtype) + _nbytes(b_blk, b.dtype)
           + sum(_nbytes((tm, tn), d) for d in out_dtypes)
           + sum(_nbytes((tm, tn), e.dtype) for e in extras)) + 2 * _nbytes((tm, tn), F32)
    sem_scratch = [pltpu.SemaphoreType.DMA((comm.n_sems,))] * 2 if comm is not None else []
    row_sem = "arbitrary" if (n_acc or comm is not None) else "parallel"
    outs = pl.pallas_call(
        body, name=name, grid=(gi, gj, gk),
        in_specs=in_specs,
        out_specs=out_specs,
        out_shape=out_shape,
        scratch_shapes=sem_scratch + ([pltpu.VMEM((tm, tn), F32)] if gk > 1 else []),
        input_output_aliases=aliases,
        compiler_params=_cparams(est, dimension_semantics=(row_sem, "arbitrary" if comm is not None else "parallel",
                                                           "arbitrary")),
    )(*operands)
    if comm is not None:
        main = outs[:no]
        return (main if no > 1 else main[0]), list(outs[no:])
    return outs if no > 1 else outs[0]


_TL = 512


def _rms_fwd(h, g, name):
    L, D = h.shape
    tl = _tile(L, _TL)

    def body(h_ref, g_ref, o_ref):
        x = h_ref[...]
        r = lax.rsqrt(jnp.mean(x * x, axis=-1, keepdims=True) + EPS)
        o_ref[...] = ((x * r) * g_ref[...]).astype(o_ref.dtype)

    return pl.pallas_call(
        body, name=name, grid=(L // tl,),
        in_specs=[pl.BlockSpec((tl, D), lambda i: (i, 0)), pl.BlockSpec((1, D), lambda i: (0, 0))],
        out_specs=pl.BlockSpec((tl, D), lambda i: (i, 0)),
        out_shape=_sds((L, D), _ACT),
        compiler_params=_cparams(3 * _nbytes((tl, D), F32)),
    )(h, g.reshape(1, D))


def _rms_bwd(dxn, h, g, dres, name):
    L, D = h.shape
    tl = _tile(L, _TL)

    def body(d_ref, h_ref, g_ref, r_ref, o_ref, dg_ref):
        x = h_ref[...]
        r = lax.rsqrt(jnp.mean(x * x, axis=-1, keepdims=True) + EPS)
        xhat = x * r
        d = d_ref[...].astype(F32)
        gy = d * g_ref[...]
        dx = r * (gy - xhat * jnp.mean(gy * xhat, axis=-1, keepdims=True))
        o_ref[...] = r_ref[...] + dx

        @pl.when(pl.program_id(0) == 0)
        def _():
            dg_ref[...] = jnp.zeros_like(dg_ref)

        dg_ref[...] += jnp.sum(d * xhat, axis=0, keepdims=True)

    dh, dg = pl.pallas_call(
        body, name=name, grid=(L // tl,),
        in_specs=[pl.BlockSpec((tl, D), lambda i: (i, 0)), pl.BlockSpec((tl, D), lambda i: (i, 0)),
                  pl.BlockSpec((1, D), lambda i: (0, 0)), pl.BlockSpec((tl, D), lambda i: (i, 0))],
        out_specs=[pl.BlockSpec((tl, D), lambda i: (i, 0)), pl.BlockSpec((1, D), lambda i: (0, 0))],
        out_shape=[_sds((L, D), F32), _sds((1, D), F32)],
        compiler_params=_cparams(5 * _nbytes((tl, D), F32)),
    )(dxn, h, g.reshape(1, D), dres)
    return dh, dg.reshape(D)


def _rope_tables(positions):
    L = positions.shape[0]
    tl = _tile(L, 1024)
    inv = 1.0 / (ROPE_THETA ** (np.arange(0, HEAD_DIM, 2, dtype=np.float32) / HEAD_DIM))
    inv128 = jnp.asarray(np.tile(inv.astype(np.float32), 4).reshape(1, LANES))

    def body(p_ref, i_ref, c_ref, s_ref):
        ang = p_ref[...].astype(F32) * i_ref[...]
        c_ref[...] = jnp.cos(ang)
        s_ref[...] = jnp.sin(ang)

    return pl.pallas_call(
        body, name="rope_tables", grid=(L // tl,),
        in_specs=[pl.BlockSpec((tl, 1), lambda i: (i, 0)), pl.BlockSpec((1, LANES), lambda i: (0, 0))],
        out_specs=[pl.BlockSpec((tl, LANES), lambda i: (i, 0))] * 2,
        out_shape=[_sds((L, LANES), F32)] * 2,
    )(positions.reshape(L, 1), inv128)


_GM_TL = 256


def _gmlp_head(Z, W, bfull, lg, lb, maskv):
    G = _gelu(Z)
    mu = jnp.sum(jnp.where(maskv, G, 0.0), axis=-1, keepdims=True) * (1.0 / HEAD_DIM)
    xc = jnp.where(maskv, G - mu, 0.0)
    var = jnp.sum(xc * xc, axis=-1, keepdims=True) * (1.0 / HEAD_DIM)
    rstd = lax.rsqrt(var + EPS)
    xhat = xc * rstd
    vn = xhat * lg + lb
    sv = _dot(W, vn) + bfull
    return G, xhat, rstd, vn, sv


def _tril(W):
    return jnp.where(_row(W.shape) >= _lane(W.shape), W, 0.0)


def _triu(W):
    return jnp.where(_row(W.shape) <= _lane(W.shape), W, 0.0)


def _gmlp_fwd(z, ws, bfull, lgf, lbf, name):
    L = z.shape[0]
    tl = _tile(L, _GM_TL)
    nch = tl // CHUNK

    def body(z_ref, w_ref, b_ref, lg_ref, lb_ref, o_ref):
        maskv = _lane((CHUNK, LANES)) >= HEAD_DIM
        for c in range(nch):
            rows = slice(c * CHUNK, (c + 1) * CHUNK)
            for hp in range(A_HEADS // 2):
                acc = None
                for hh in range(2):
                    h = 2 * hp + hh
                    Z = z_ref[rows, h * LANES:(h + 1) * LANES]
                    G, _, _, _, sv = _gmlp_head(Z, _tril(w_ref[h]), b_ref[h], lg_ref[h:h + 1, :], lb_ref[h:h + 1, :], maskv)
                    prod = G * pltpu.roll(sv, HEAD_DIM, axis=1)
                    acc = prod if hh == 0 else acc + pltpu.roll(prod, HEAD_DIM, axis=1)
                o_ref[rows, hp * LANES:(hp + 1) * LANES] = acc

    return pl.pallas_call(
        body, name=name, grid=(L // tl,),
        in_specs=[pl.BlockSpec((tl, IN_A), lambda i: (i, 0)),
                  pl.BlockSpec((A_HEADS, CHUNK, CHUNK), lambda i: (0, 0, 0)),
                  pl.BlockSpec((A_HEADS, CHUNK, LANES), lambda i: (0, 0, 0)),
                  pl.BlockSpec((A_HEADS, LANES), lambda i: (0, 0)),
                  pl.BlockSpec((A_HEADS, LANES), lambda i: (0, 0))],
        out_specs=pl.BlockSpec((tl, 2 * LANES), lambda i: (i, 0)),
        out_shape=_sds((L, 2 * LANES), F32),
    )(z, ws, bfull, lgf, lbf)


def _gmlp_bwd(z, dya, ws, wsT, bfull, lgf, lbf, name):
    L = z.shape[0]
    tl = _tile(L, _GM_TL)
    nch = tl // CHUNK
    nsteps = L // tl

    def body(z_ref, d_ref, w_ref, wt_ref, b_ref, lg_ref, lb_ref, dz_ref, dw_ref, db_ref, dlg_ref, dlb_ref):
        step = pl.program_id(0)

        @pl.when(step == 0)
        def _():
            dw_ref[...] = jnp.zeros_like(dw_ref)
            db_ref[...] = jnp.zeros_like(db_ref)
            dlg_ref[...] = jnp.zeros_like(dlg_ref)
            dlb_ref[...] = jnp.zeros_like(dlb_ref)

        lane = _lane((CHUNK, LANES))
        maskv = lane >= HEAD_DIM
        for c in range(nch):
            rows = slice(c * CHUNK, (c + 1) * CHUNK)
            for h in range(A_HEADS):
                hp, hh = divmod(h, 2)
                Z = z_ref[rows, h * LANES:(h + 1) * LANES]
                lg = lg_ref[h:h + 1, :]
                G, xhat, rstd, vn, sv = _gmlp_head(Z, _tril(w_ref[h]), b_ref[h], lg, lb_ref[h:h + 1, :], maskv)
                dpair = d_ref[rows, hp * LANES:(hp + 1) * LANES]
                if hh == 1:
                    dpair = pltpu.roll(dpair, HEAD_DIM, axis=1)
                dout = jnp.where(maskv, 0.0, dpair)
                du = dout * pltpu.roll(sv, HEAD_DIM, axis=1)
                dsv = pltpu.roll(dout * G, HEAD_DIM, axis=1)
                dw_ref[h] += _tril(_dot(dsv, vn, _NT))
                db_ref[h] += dsv
                dvn = _dot(_triu(wt_ref[h]), dsv)
                dlg_ref[h] += dvn * xhat
                dlb_ref[h] += dvn
                dxh = dvn * lg
                m1 = jnp.sum(dxh, axis=-1, keepdims=True) * (1.0 / HEAD_DIM)
                m2 = jnp.sum(dxh * xhat, axis=-1, keepdims=True) * (1.0 / HEAD_DIM)
                dv = jnp.where(maskv, rstd * (dxh - m1 - xhat * m2), 0.0)
                dz_ref[rows, h * LANES:(h + 1) * LANES] = ((du + dv) * _gelu_grad(Z)).astype(dz_ref.dtype)

        @pl.when(step == nsteps - 1)
        def _():
            for h in range(A_HEADS):
                db_ref[h] = jnp.broadcast_to(jnp.sum(db_ref[h], axis=1, keepdims=True), (CHUNK, LANES))
                dlg_ref[h] = jnp.broadcast_to(jnp.sum(dlg_ref[h], axis=0, keepdims=True), (CHUNK, LANES))
                dlb_ref[h] = jnp.broadcast_to(jnp.sum(dlb_ref[h], axis=0, keepdims=True), (CHUNK, LANES))

    full3 = pl.BlockSpec((A_HEADS, CHUNK, LANES), lambda i: (0, 0, 0))
    return pl.pallas_call(
        body, name=name, grid=(nsteps,),
        in_specs=[pl.BlockSpec((tl, IN_A), lambda i: (i, 0)),
                  pl.BlockSpec((tl, 2 * LANES), lambda i: (i, 0)),
                  full3, full3, full3,
                  pl.BlockSpec((A_HEADS, LANES), lambda i: (0, 0)),
                  pl.BlockSpec((A_HEADS, LANES), lambda i: (0, 0))],
        out_specs=[pl.BlockSpec((tl, IN_A), lambda i: (i, 0)), full3, full3, full3, full3],
        out_shape=[_sds((L, IN_A), _ACT)] + [_sds((A_HEADS, CHUNK, LANES), F32)] * 4,
    )(z, dya, ws, wsT, bfull, lgf, lbf)


def _head_rstd(x, lo):
    sq = x * x
    s_lo = jnp.sum(jnp.where(lo, sq, 0.0), axis=-1, keepdims=True)
    s_hi = jnp.sum(jnp.where(lo, 0.0, sq), axis=-1, keepdims=True)
    return jnp.where(lo, lax.rsqrt(s_lo * (1.0 / HEAD_DIM) + EPS), lax.rsqrt(s_hi * (1.0 / HEAD_DIM) + EPS))


def _rot_half(x, first):
    return jnp.where(first, -pltpu.roll(x, LANES - HEAD_DIM // 2, axis=1), pltpu.roll(x, HEAD_DIM // 2, axis=1))


def _qk_prep(z, cos, sin, gq, gk, name):
    L = z.shape[0]
    tl = _tile(L, _TL)
    nq = IN_Q // LANES

    def body(q_ref, k_ref, c_ref, s_ref, gq_ref, gk_ref, qo_ref, ko_ref):
        lane = _lane((tl, LANES))
        lo = lane < HEAD_DIM
        first = (lane % HEAD_DIM) < (HEAD_DIM // 2)
        c, s = c_ref[...], s_ref[...]

        def prep(x, g):
            xn = (x * _head_rstd(x, lo)) * g
            return xn * c + _rot_half(xn, first) * s

        for j in range(nq):
            qo_ref[:, j * LANES:(j + 1) * LANES] = prep(q_ref[:, j * LANES:(j + 1) * LANES], gq_ref[...]).astype(qo_ref.dtype)
        ko_ref[...] = prep(k_ref[...], gk_ref[...]).astype(ko_ref.dtype)

    return pl.pallas_call(
        body, name=name, grid=(L // tl,),
        in_specs=[pl.BlockSpec((tl, IN_Q), lambda i: (i, 1)),
                  pl.BlockSpec((tl, IN_KV), lambda i: (i, 8)),
                  pl.BlockSpec((tl, LANES), lambda i: (i, 0)), pl.BlockSpec((tl, LANES), lambda i: (i, 0)),
                  pl.BlockSpec((1, LANES), lambda i: (0, 0)), pl.BlockSpec((1, LANES), lambda i: (0, 0))],
        out_specs=[pl.BlockSpec((tl, IN_Q), lambda i: (i, 0)), pl.BlockSpec((tl, IN_KV), lambda i: (i, 0))],
        out_shape=[_sds((L, IN_Q), _ACT), _sds((L, IN_KV), _ACT)],
    )(z, z, cos, sin, gq, gk)


def _qk_prep_bwd(z, dq, dkc, dkp, dvc, dvp, cos, sin, gq, gk, name):
    L = z.shape[0]
    tl = _ATT_QB * WINDOW
    nb = L // tl
    nq = IN_Q // LANES

    def body(q_ref, k_ref, dq_ref, dkc_ref, dkp_ref, dvc_ref, dvp_ref, c_ref, s_ref, gq_ref, gk_ref,
             dzq_ref, dzk_ref, dzv_ref, dgq_ref, dgk_ref):
        n = pl.program_id(0)

        @pl.when(n == 0)
        def _():
            dgq_ref[...] = jnp.zeros_like(dgq_ref)
            dgk_ref[...] = jnp.zeros_like(dgk_ref)

        lane = _lane((tl, LANES))
        lo = lane < HEAD_DIM
        first = (lane % HEAD_DIM) < (HEAD_DIM // 2)
        c, s = c_ref[...], s_ref[...]
        has_next = jnp.where(n < nb - 1, 1.0, 0.0)

        def bwd(x, g, dy):
            r = _head_rstd(x, lo)
            xhat = x * r
            dxn = dy * c - _rot_half(dy * s, first)
            gy = dxn * g
            t = gy * xhat
            m_lo = jnp.sum(jnp.where(lo, t, 0.0), axis=-1, keepdims=True)
            m_hi = jnp.sum(jnp.where(lo, 0.0, t), axis=-1, keepdims=True)
            m = jnp.where(lo, m_lo, m_hi) * (1.0 / HEAD_DIM)
            dx = r * (gy - xhat * m)
            dg = jnp.sum(dxn * xhat, axis=0, keepdims=True)
            return dx, dg

        dgq = jnp.zeros((1, LANES), F32)
        for j in range(nq):
            sl = slice(j * LANES, (j + 1) * LANES)
            dx, dg = bwd(q_ref[:, sl], gq_ref[...], dq_ref[:, sl].astype(F32))
            dzq_ref[:, sl] = dx.astype(dzq_ref.dtype)
            dgq = dgq + dg
        dgq_ref[...] += dgq + pltpu.roll(dgq, HEAD_DIM, axis=1)
        def with_next(cur_ref, nxt_ref):
            head = jnp.zeros((tl - WINDOW, IN_KV), F32)
            return cur_ref[...] + jnp.concatenate([head, has_next * nxt_ref[...]], axis=0)

        dx, dg = bwd(k_ref[...], gk_ref[...], with_next(dkc_ref, dkp_ref))
        dzk_ref[...] = dx.astype(dzk_ref.dtype)
        dgk_ref[...] += dg + pltpu.roll(dg, HEAD_DIM, axis=1)
        dzv_ref[...] = with_next(dvc_ref, dvp_ref).astype(dzv_ref.dtype)

    nxt = lambda i: (jnp.minimum(i + 1, nb - 1), 0)
    cur = lambda i: (i, 0)
    kv = pl.BlockSpec((tl, IN_KV), cur)
    kvn = pl.BlockSpec((WINDOW, IN_KV), nxt)
    one = pl.BlockSpec((1, LANES), lambda i: (0, 0))
    return pl.pallas_call(
        body, name=name, grid=(nb,),
        in_specs=[pl.BlockSpec((tl, IN_Q), lambda i: (i, 1)), pl.BlockSpec((tl, IN_KV), lambda i: (i, 8)),
                  pl.BlockSpec((tl, IN_Q), cur), kv, kvn, kv, kvn,
                  kv, kv, one, one],
        out_specs=[pl.BlockSpec((tl, IN_Q), cur), kv, kv, one, one],
        out_shape=[_sds((L, IN_Q), _ACT), _sds((L, IN_KV), _ACT),
                   _sds((L, IN_KV), _ACT), _sds((1, LANES), F32),
                   _sds((1, LANES), F32)],
    )(z, z, dq, dkc, dkp, dvc, dvp, cos, sin, gq, gk)


def _attn_mask(n):
    shp = (2 * WINDOW, B_GROUP * WINDOW)
    qi = _lane(shp) % WINDOW
    kj = _row(shp)
    off = 0 if n is None else jnp.where(n > 0, 0, 4 * WINDOW)
    return ((kj >= WINDOW) & (kj - WINDOW <= qi)) | ((kj < WINDOW) & (kj > qi + off))


def _kv_lanes(j):
    lane = _lane((WINDOW, LANES))
    return (lane >= j * HEAD_DIM) & (lane < (j + 1) * HEAD_DIM)


_ATT_QB = 4


def _stack_heads(ref, rows, j, kvl):
    parts = []
    for g in range(B_GROUP):
        h = j * B_GROUP + g
        slab = ref[rows, (h // 2) * LANES:(h // 2 + 1) * LANES].astype(F32)
        if (h % 2) != j:
            slab = pltpu.roll(slab, HEAD_DIM, axis=1)
        parts.append(jnp.where(kvl, slab, 0.0))
    return jnp.concatenate(parts, axis=0)


def _attn_probs(qs, k2, sink_row, mask):
    s = _dot(k2, qs, _NT) * (HEAD_DIM ** -0.5)
    s = jnp.where(mask, s, NEG)
    m = jnp.maximum(jnp.max(s, axis=0, keepdims=True), sink_row)
    p = jnp.exp(s - m)
    esink = jnp.exp(sink_row - m)
    inv = 1.0 / (jnp.sum(p, axis=0, keepdims=True) + esink)
    return p * inv, esink * inv


def _sink_row(sink_ref, j):
    lane = _lane((1, B_GROUP * WINDOW))
    row = jnp.full((1, B_GROUP * WINDOW), sink_ref[j * B_GROUP], F32)
    for g in range(1, B_GROUP):
        row = jnp.where(lane >= g * WINDOW, sink_ref[j * B_GROUP + g], row)
    return row


def _attn_fwd(q, k, z, sinks, name):
    L = q.shape[0]
    QB = _ATT_QB
    tq = QB * WINDOW
    prev = lambda n: (jnp.maximum(QB * n - 1, 0), 0)
    prev_v = lambda n: (jnp.maximum(QB * n - 1, 0), 9)

    def body(s_ref, q_ref, kp_ref, kc_ref, vp_ref, vc_ref, o_ref):
        n = pl.program_id(0)
        k3 = jnp.concatenate([kp_ref[...], kc_ref[...]], axis=0)
        v3 = jnp.concatenate([vp_ref[...], vc_ref[...]], axis=0)
        for b in range(QB):
            rows = slice(b * WINDOW, (b + 1) * WINDOW)
            mask = _attn_mask(n if b == 0 else None)
            k2 = k3[b * WINDOW:(b + 2) * WINDOW]
            v2 = v3[b * WINDOW:(b + 2) * WINDOW]
            slabs = [None] * (IN_Q // LANES)
            for j in range(B_KV_HEADS):
                kvl = _kv_lanes(j)
                qs = _stack_heads(q_ref, rows, j, kvl)
                pn, _ = _attn_probs(qs, k2, _sink_row(s_ref, j), mask)
                o = _dot(pn, v2, _TN)
                for g in range(B_GROUP):
                    h = j * B_GROUP + g
                    piece = jnp.where(kvl, o[g * WINDOW:(g + 1) * WINDOW], 0.0)
                    if (h % 2) != j:
                        piece = pltpu.roll(piece, HEAD_DIM, axis=1)
                    slabs[h // 2] = piece if slabs[h // 2] is None else slabs[h // 2] + piece
            for t, sl in enumerate(slabs):
                o_ref[rows, t * LANES:(t + 1) * LANES] = sl

    return pl.pallas_call(
        body, name=name, grid=(L // tq,),
        in_specs=[pl.BlockSpec(memory_space=pltpu.SMEM),
                  pl.BlockSpec((tq, IN_Q), lambda n: (n, 0)),
                  pl.BlockSpec((WINDOW, IN_KV), prev), pl.BlockSpec((tq, IN_KV), lambda n: (n, 0)),
                  pl.BlockSpec((WINDOW, IN_KV), prev_v), pl.BlockSpec((tq, IN_KV), lambda n: (n, 9))],
        out_specs=pl.BlockSpec((tq, IN_Q), lambda n: (n, 0)),
        out_shape=_sds((L, IN_Q), F32),
    )(sinks, q, k, k, z, z)


def _attn_bwd(q, k, z, sinks, dyb, name):
    L = q.shape[0]
    QB = _ATT_QB
    tq = QB * WINDOW
    nsteps = L // tq
    prev = lambda n: (jnp.maximum(QB * n - 1, 0), 0)
    prev_v = lambda n: (jnp.maximum(QB * n - 1, 0), 9)
    cur = lambda n: (n, 0)

    def body(s_ref, q_ref, kp_ref, kc_ref, vp_ref, vc_ref, d_ref, dq_ref, dkc_ref, dkp_ref, dvc_ref, dvp_ref, ds_ref):
        n = pl.program_id(0)

        @pl.when(n == 0)
        def _():
            ds_ref[...] = jnp.zeros_like(ds_ref)

        k3 = jnp.concatenate([kp_ref[...], kc_ref[...]], axis=0)
        v3 = jnp.concatenate([vp_ref[...], vc_ref[...]], axis=0)
        dkb = [None] * (QB + 1)
        dvb = [None] * (QB + 1)
        dsink = jnp.zeros((1, LANES), F32)
        lane1 = _lane((1, LANES))
        add = lambda acc, v: v if acc is None else acc + v
        for b in range(QB):
            rows = slice(b * WINDOW, (b + 1) * WINDOW)
            mask = _attn_mask(n if b == 0 else None)
            k2 = k3[b * WINDOW:(b + 2) * WINDOW]
            v2 = v3[b * WINDOW:(b + 2) * WINDOW]
            slabs = [None] * (IN_Q // LANES)
            for j in range(B_KV_HEADS):
                kvl = _kv_lanes(j)
                qs = _stack_heads(q_ref, rows, j, kvl)
                dos = _stack_heads(d_ref, rows, j, kvl)
                pn, psink = _attn_probs(qs, k2, _sink_row(s_ref, j), mask)
                dp = _dot(v2, dos, _NT)
                dd = jnp.sum(pn * dp, axis=0, keepdims=True)
                dss = (pn * (dp - dd)) * (HEAD_DIM ** -0.5)
                dqs = _dot(dss, k2, _TN)
                dk2 = _dot(dss, qs)
                dv2 = _dot(pn, dos)
                dkb[b], dkb[b + 1] = add(dkb[b], dk2[:WINDOW]), add(dkb[b + 1], dk2[WINDOW:])
                dvb[b], dvb[b + 1] = add(dvb[b], dv2[:WINDOW]), add(dvb[b + 1], dv2[WINDOW:])
                sd = psink * dd
                for g in range(B_GROUP):
                    h = j * B_GROUP + g
                    piece = jnp.where(kvl, dqs[g * WINDOW:(g + 1) * WINDOW], 0.0)
                    if (h % 2) != j:
                        piece = pltpu.roll(piece, HEAD_DIM, axis=1)
                    slabs[h // 2] = piece if slabs[h // 2] is None else slabs[h // 2] + piece
                    tot = jnp.sum(sd[:, g * WINDOW:(g + 1) * WINDOW], axis=1, keepdims=True)
                    dsink = dsink - jnp.where(lane1 == h, tot, 0.0)
            for t, sl in enumerate(slabs):
                dq_ref[rows, t * LANES:(t + 1) * LANES] = sl
        dkp_ref[...] = dkb[0]
        dvp_ref[...] = dvb[0]
        for b in range(QB):
            dkc_ref[b * WINDOW:(b + 1) * WINDOW, :] = dkb[b + 1]
            dvc_ref[b * WINDOW:(b + 1) * WINDOW, :] = dvb[b + 1]
        ds_ref[0:1, :] += dsink

    kvs = pl.BlockSpec((tq, IN_KV), cur)
    kvp = pl.BlockSpec((WINDOW, IN_KV), cur)
    kvo = _sds((L, IN_KV), F32)
    kvpo = _sds((nsteps * WINDOW, IN_KV), F32)
    return pl.pallas_call(
        body, name=name, grid=(nsteps,),
        in_specs=[pl.BlockSpec(memory_space=pltpu.SMEM),
                  pl.BlockSpec((tq, IN_Q), cur),
                  pl.BlockSpec((WINDOW, IN_KV), prev), kvs,
                  pl.BlockSpec((WINDOW, IN_KV), prev_v), pl.BlockSpec((tq, IN_KV), lambda n: (n, 9)),
                  pl.BlockSpec((tq, IN_Q), cur)],
        out_specs=[pl.BlockSpec((tq, IN_Q), cur), kvs, kvp, kvs, kvp, pl.BlockSpec((SUBLANES, LANES), lambda n: (0, 0))],
        out_shape=[_sds((L, IN_Q), F32), kvo, kvpo, kvo, kvpo, _sds((SUBLANES, LANES), F32)],
    )(sinks, q, k, k, z, z, dyb)


def _ssm_disc(are, aim, ldt, bre, bim):
    dt = jnp.exp(ldt)
    mag = jnp.exp(are * dt)
    lr, li = mag * jnp.cos(aim * dt), mag * jnp.sin(aim * dt)
    den = are * are + aim * aim
    xr, xi = lr - 1.0, li
    cr, ci = (xr * are + xi * aim) / den, (xi * are - xr * aim) / den
    return lr, li, cr * bre - ci * bim, cr * bim + ci * bre


def _ssm_prep(are, aim, ldt, bre, bim):
    shp3, shpb = are.shape, bre.shape

    def body(are_ref, aim_ref, ldt_ref, bre_ref, bim_ref, lr_ref, li_ref, br_ref, bi_ref):
        lr, li, br, bi = _ssm_disc(are_ref[...], aim_ref[...], ldt_ref[...], bre_ref[...], bim_ref[...])
        lr_ref[...] = lr
        li_ref[...] = li
        br_ref[...] = br
        bi_ref[...] = bi

    return pl.pallas_call(
        body, name="ssm_prep",
        out_shape=[_sds(shp3, F32)] * 2 + [_sds(shpb, F32)] * 2,
    )(are, aim, ldt, bre, bim)


def _ssm_prep_bwd(are, aim, ldt, bre, bim, dlr, dli, dbr, dbi):
    shp3, shpb = are.shape, bre.shape

    def body(are_ref, aim_ref, ldt_ref, bre_ref, bim_ref, dlr_ref, dli_ref, dbr_ref, dbi_ref,
             o_are, o_aim, o_ldt, o_bre, o_bim):
        _, vjp = jax.vjp(_ssm_disc, are_ref[...], aim_ref[...], ldt_ref[...], bre_ref[...], bim_ref[...])
        g = vjp((dlr_ref[...], dli_ref[...], dbr_ref[...], dbi_ref[...]))
        o_are[...] = g[0]
        o_aim[...] = g[1]
        o_ldt[...] = jnp.broadcast_to(jnp.sum(g[2], axis=-1, keepdims=True), shp3)
        o_bre[...] = g[3]
        o_bim[...] = g[4]

    return pl.pallas_call(
        body, name="ssm_prep_bwd",
        out_shape=[_sds(shp3, F32)] * 3 + [_sds(shpb, F32)] * 2,
    )(are, aim, ldt, bre, bim, dlr, dli, dbr, dbi)


_SCAN_TB = 512
_SCAN_W = 256


def _cmul(ar, ai, br, bi):
    return ar * br - ai * bi, ar * bi + ai * br


def _ssm_scan(x, lam_r, lam_i, name, reverse=False, states=None):
    L = x.shape[0]
    tb = _tile(L, _SCAN_TB)
    nrb = L // tb
    nt = tb // SUBLANES
    W = _SCAN_W
    with_da = states is not None

    def body(*refs):
        if with_da:
            xr_ref, xi_ref, sr_ref, si_ref, ar_ref, ai_ref, o_ref, dar_ref, dai_ref, cr_ref, ci_ref = refs
        else:
            xr_ref, xi_ref, ar_ref, ai_ref, o_ref, cr_ref, ci_ref = refs
        step = pl.program_id(0)

        @pl.when(step == 0)
        def _():
            cr_ref[...] = jnp.zeros_like(cr_ref)
            ci_ref[...] = jnp.zeros_like(ci_ref)
            if with_da:
                dar_ref[...] = jnp.zeros_like(dar_ref)
                dai_ref[...] = jnp.zeros_like(dai_ref)

        row = _row((SUBLANES, W))

        def shift(v, d, fill):
            if reverse:
                return jnp.where(row < SUBLANES - d, pltpu.roll(v, SUBLANES - d, axis=0), fill)
            return jnp.where(row >= d, pltpu.roll(v, d, axis=0), fill)

        edge = 0 if reverse else SUBLANES - 1
        for wb in range(N_STATE // W):
            cols = slice(wb * W, (wb + 1) * W)
            a1r = jnp.broadcast_to(ar_ref[:, cols], (SUBLANES, W))
            a1i = jnp.broadcast_to(ai_ref[:, cols], (SUBLANES, W))
            if reverse:
                a1i = -a1i
            a2r, a2i = _cmul(a1r, a1i, a1r, a1i)
            a4r, a4i = _cmul(a2r, a2i, a2r, a2i)
            pws = ((1, a1r, a1i), (2, a2r, a2i), (4, a4r, a4i))
            pr, pi = a1r, a1i
            for d, _, _ in pws:
                qr, qi = _cmul(pr, pi, shift(pr, d, 1.0), shift(pi, d, 0.0))
                pr, pi = qr, qi
            mws = []
            for d, er, ei in pws:
                ok = (row < SUBLANES - d) if reverse else (row >= d)
                mws.append(((SUBLANES - d) if reverse else d, jnp.where(ok, er, 0.0), jnp.where(ok, ei, 0.0)))

            def tile(i, carry):
                cr, ci, dr, di = carry
                t = (nt - 1 - i) if reverse else i
                r0 = pl.multiple_of(t * SUBLANES, SUBLANES)
                vr = xr_ref[pl.ds(r0, SUBLANES), cols]
                vi = xi_ref[pl.ds(r0, SUBLANES), cols]
                for sh, er, ei in mws:
                    tr, ti = _cmul(er, ei, pltpu.roll(vr, sh, axis=0), pltpu.roll(vi, sh, axis=0))
                    vr, vi = vr + tr, vi + ti
                tr, ti = _cmul(pr, pi, cr, ci)
                vr, vi = vr + tr, vi + ti
                o_ref[pl.ds(r0, SUBLANES), cols] = vr
                o_ref[pl.ds(r0, SUBLANES), slice(N_STATE + wb * W, N_STATE + (wb + 1) * W)] = vi
                if with_da:
                    gr = jnp.where(row < SUBLANES - 1, pltpu.roll(vr, SUBLANES - 1, axis=0), cr)
                    gi = jnp.where(row < SUBLANES - 1, pltpu.roll(vi, SUBLANES - 1, axis=0), ci)
                    sr = sr_ref[pl.ds(r0, SUBLANES), cols]
                    si = si_ref[pl.ds(r0, SUBLANES), cols]
                    dr = dr + sr * gr + si * gi
                    di = di + sr * gi - si * gr
                ncr = jnp.broadcast_to(vr[edge:edge + 1, :], (SUBLANES, W))
                nci = jnp.broadcast_to(vi[edge:edge + 1, :], (SUBLANES, W))
                return ncr, nci, dr, di

            zero = jnp.zeros((SUBLANES, W), F32)
            cr, ci, dr, di = lax.fori_loop(0, nt, tile, (cr_ref[:, cols], ci_ref[:, cols], zero, zero), unroll=2)
            cr_ref[:, cols] = cr
            ci_ref[:, cols] = ci
            if with_da:
                dar_ref[:, cols] += dr
                dai_ref[:, cols] += di

        if with_da:
            @pl.when(step == nrb - 1)
            def _():
                dar_ref[...] = jnp.broadcast_to(jnp.sum(dar_ref[...], axis=0, keepdims=True), dar_ref.shape)
                dai_ref[...] = jnp.broadcast_to(jnp.sum(dai_ref[...], axis=0, keepdims=True), dai_ref.shape)

    rb = (lambda i: (nrb - 1 - i, 0)) if reverse else (lambda i: (i, 0))
    rb_im = (lambda i: (nrb - 1 - i, 1)) if reverse else (lambda i: (i, 1))
    blk_r = pl.BlockSpec((tb, N_STATE), rb)
    blk_i = pl.BlockSpec((tb, N_STATE), rb_im)
    one = pl.BlockSpec((1, N_STATE), lambda i: (0, 0))
    acc = pl.BlockSpec((SUBLANES, N_STATE), lambda i: (0, 0))
    ins = [x, x] + ([states, states] if with_da else []) + [lam_r, lam_i]
    in_specs = [blk_r, blk_i] + ([blk_r, blk_i] if with_da else []) + [one, one]
    out_specs = [pl.BlockSpec((tb, 2 * N_STATE), rb)] + ([acc, acc] if with_da else [])
    out_shape = [_sds((L, 2 * N_STATE), F32)] + (
        [_sds((SUBLANES, N_STATE), F32)] * 2 if with_da else [])
    outs = pl.pallas_call(
        body, name=name, grid=(nrb,), in_specs=in_specs, out_specs=out_specs, out_shape=out_shape,
        scratch_shapes=[pltpu.VMEM((SUBLANES, N_STATE), F32)] * 2,
        compiler_params=_cparams((6 if with_da else 4) * _nbytes((tb, N_STATE), F32),
                                 dimension_semantics=("arbitrary",)),
    )(*ins)
    return outs if with_da else outs[0]


def _scan_block(x_ref, o_ref, s_ref, ar_ref, ai_ref, cr_ref, ci_ref, dar_ref, dai_ref, nt, reverse):
    W = _SCAN_W
    with_da = s_ref is not None
    row = _row((SUBLANES, W))

    def shift(v, d, fill):
        if reverse:
            return jnp.where(row < SUBLANES - d, pltpu.roll(v, SUBLANES - d, axis=0), fill)
        return jnp.where(row >= d, pltpu.roll(v, d, axis=0), fill)

    edge = 0 if reverse else SUBLANES - 1
    for wb in range(N_STATE // W):
        cols = slice(wb * W, (wb + 1) * W)
        icols = slice(N_STATE + wb * W, N_STATE + (wb + 1) * W)
        a1r = jnp.broadcast_to(ar_ref[:, cols], (SUBLANES, W))
        a1i = jnp.broadcast_to(ai_ref[:, cols], (SUBLANES, W))
        if reverse:
            a1i = -a1i
        a2r, a2i = _cmul(a1r, a1i, a1r, a1i)
        a4r, a4i = _cmul(a2r, a2i, a2r, a2i)
        pws = ((1, a1r, a1i), (2, a2r, a2i), (4, a4r, a4i))
        pr, pi = a1r, a1i
        for d, _, _ in pws:
            qr, qi = _cmul(pr, pi, shift(pr, d, 1.0), shift(pi, d, 0.0))
            pr, pi = qr, qi
        mws = []
        for d, er, ei in pws:
            ok = (row < SUBLANES - d) if reverse else (row >= d)
            mws.append(((SUBLANES - d) if reverse else d, jnp.where(ok, er, 0.0), jnp.where(ok, ei, 0.0)))

        def tile(i, carry):
            cr, ci, dr, di = carry
            t = (nt - 1 - i) if reverse else i
            r0 = pl.multiple_of(t * SUBLANES, SUBLANES)
            vr = x_ref[pl.ds(r0, SUBLANES), cols]
            vi = x_ref[pl.ds(r0, SUBLANES), icols]
            for sh, er, ei in mws:
                tr, ti = _cmul(er, ei, pltpu.roll(vr, sh, axis=0), pltpu.roll(vi, sh, axis=0))
                vr, vi = vr + tr, vi + ti
            tr, ti = _cmul(pr, pi, cr, ci)
            vr, vi = vr + tr, vi + ti
            o_ref[pl.ds(r0, SUBLANES), cols] = vr
            o_ref[pl.ds(r0, SUBLANES), icols] = vi
            if with_da:
                gr = jnp.where(row < SUBLANES - 1, pltpu.roll(vr, SUBLANES - 1, axis=0), cr)
                gi = jnp.where(row < SUBLANES - 1, pltpu.roll(vi, SUBLANES - 1, axis=0), ci)
                sr = s_ref[pl.ds(r0, SUBLANES), cols]
                si = s_ref[pl.ds(r0, SUBLANES), icols]
                dr = dr + sr * gr + si * gi
                di = di + sr * gi - si * gr
            ncr = jnp.broadcast_to(vr[edge:edge + 1, :], (SUBLANES, W))
            nci = jnp.broadcast_to(vi[edge:edge + 1, :], (SUBLANES, W))
            return ncr, nci, dr, di

        zero = jnp.zeros((SUBLANES, W), F32)
        cr, ci, dr, di = lax.fori_loop(0, nt, tile, (cr_ref[:, cols], ci_ref[:, cols], zero, zero), unroll=2)
        cr_ref[:, cols] = cr
        ci_ref[:, cols] = ci
        if with_da:
            dar_ref[:, cols] += dr
            dai_ref[:, cols] += di


def _ssm_fwd(z, bcat, ccat, dskip, lam_r, lam_i, name):
    L = z.shape[0]
    tb = _tile(L, _SCAN_TB)
    nrb = L // tb
    nt = tb // SUBLANES

    def body(u_ref, b_ref, c_ref, d_ref, ar_ref, ai_ref, s_ref, y_ref, yg_ref, xs_ref, cr_ref, ci_ref):
        @pl.when(pl.program_id(0) == 0)
        def _():
            cr_ref[...] = jnp.zeros_like(cr_ref)
            ci_ref[...] = jnp.zeros_like(ci_ref)

        u = u_ref[...]
        xs_ref[...] = _dot(u, b_ref[...])
        _scan_block(xs_ref, s_ref, None, ar_ref, ai_ref, cr_ref, ci_ref, None, None, nt, False)
        y = _dot(s_ref[...], c_ref[...]) + d_ref[...] * u
        y_ref[...] = y
        yg_ref[...] = _gelu(y).astype(yg_ref.dtype)

    full = lambda shp: pl.BlockSpec(shp, lambda i: (0, 0))
    rows = lambda w: pl.BlockSpec((tb, w), lambda i: (i, 0))
    return pl.pallas_call(
        body, name=name, grid=(nrb,),
        in_specs=[pl.BlockSpec((tb, C_WIDTH), lambda i: (i, 5)), full((C_WIDTH, 2 * N_STATE)), full((2 * N_STATE, C_WIDTH)),
                  full((1, C_WIDTH)), full((1, N_STATE)), full((1, N_STATE))],
        out_specs=[rows(2 * N_STATE), rows(C_WIDTH), rows(C_WIDTH)],
        out_shape=[_sds((L, 2 * N_STATE), F32), _sds((L, C_WIDTH), F32), _sds((L, C_WIDTH), _ACT)],
        scratch_shapes=[pltpu.VMEM((tb, 2 * N_STATE), F32)] + [pltpu.VMEM((SUBLANES, N_STATE), F32)] * 2,
        compiler_params=_cparams(5 * _nbytes((tb, 2 * N_STATE), F32), dimension_semantics=("arbitrary",)),
    )(_hbm(z), bcat, ccat, dskip, lam_r, lam_i)


def _ssm_bwd(dy, z, S, bcat, ccat, dskip, lam_r, lam_i, name):
    L = z.shape[0]
    tb = _tile(L, _SCAN_TB)
    nrb = L // tb
    nt = tb // SUBLANES

    def body(dy_ref, u_ref, s_ref, b_ref, c_ref, d_ref, ar_ref, ai_ref,
             du_ref, db_ref, dc_ref, dd_ref, dar_ref, dai_ref, xs_ref, gs_ref, cr_ref, ci_ref):
        step = pl.program_id(0)

        @pl.when(step == 0)
        def _():
            for r in (cr_ref, ci_ref, db_ref, dc_ref, dd_ref, dar_ref, dai_ref):
                r[...] = jnp.zeros_like(r)

        dyv, u = dy_ref[...], u_ref[...]
        xs_ref[...] = _dot(dyv, c_ref[...], _NT)
        _scan_block(xs_ref, gs_ref, s_ref, ar_ref, ai_ref, cr_ref, ci_ref, dar_ref, dai_ref, nt, True)
        g = gs_ref[...]
        du_ref[...] = (_dot(g, b_ref[...], _NT) + dyv * d_ref[...]).astype(du_ref.dtype)
        db_ref[...] += _dot(u, g, _TN)
        dc_ref[...] += _dot(s_ref[...], dyv, _TN)
        dd_ref[...] += jnp.sum(dyv * u, axis=0, keepdims=True)

        @pl.when(step == nrb - 1)
        def _():
            dar_ref[...] = jnp.broadcast_to(jnp.sum(dar_ref[...], axis=0, keepdims=True), dar_ref.shape)
            dai_ref[...] = jnp.broadcast_to(jnp.sum(dai_ref[...], axis=0, keepdims=True), dai_ref.shape)

    full = lambda shp: pl.BlockSpec(shp, lambda i: (0, 0))
    rows = lambda w, col=0: pl.BlockSpec((tb, w), lambda i: (nrb - 1 - i, col))
    acc = full((SUBLANES, N_STATE))
    return pl.pallas_call(
        body, name=name, grid=(nrb,),
        in_specs=[rows(C_WIDTH), rows(C_WIDTH, 5), rows(2 * N_STATE), full((C_WIDTH, 2 * N_STATE)),
                  full((2 * N_STATE, C_WIDTH)), full((1, C_WIDTH)), full((1, N_STATE)), full((1, N_STATE))],
        out_specs=[rows(C_WIDTH), full((C_WIDTH, 2 * N_STATE)), full((2 * N_STATE, C_WIDTH)), full((1, C_WIDTH)), acc, acc],
        out_shape=[_sds((L, C_WIDTH), _ACT), _sds((C_WIDTH, 2 * N_STATE), F32), _sds((2 * N_STATE, C_WIDTH), F32),
                   _sds((1, C_WIDTH), F32), _sds((SUBLANES, N_STATE), F32), _sds((SUBLANES, N_STATE), F32)],
        scratch_shapes=[pltpu.VMEM((tb, 2 * N_STATE), F32)] * 2 + [pltpu.VMEM((SUBLANES, N_STATE), F32)] * 2,
        compiler_params=_cparams(7 * _nbytes((tb, 2 * N_STATE), F32), dimension_semantics=("arbitrary",)),
    )(dy, _hbm(z), _hbm(S), bcat, ccat, dskip, lam_r, lam_i)


_GROUPS = ((0, 256), (256, 768), (768, 1024))


def _merge_fwd(ya, yb, g12, mixg, name):
    L = ya.shape[0]
    tl = _tile(L, _TL)

    def body(a_ref, b_ref, g_ref, m_ref, o_ref):
        g12v = g_ref[...]
        yc = g12v[:, :C_WIDTH] * _sigmoid(g12v[:, C_WIDTH:])
        for (lo, hi), y in zip(_GROUPS, (a_ref[...], b_ref[...], yc)):
            r = lax.rsqrt(jnp.mean(y * y, axis=-1, keepdims=True) + EPS)
            o_ref[:, lo:hi] = ((y * r) * m_ref[:, lo:hi]).astype(o_ref.dtype)

    row = lambda w: pl.BlockSpec((tl, w), lambda i: (i, 0))
    return pl.pallas_call(
        body, name=name, grid=(L // tl,),
        in_specs=[row(256), row(512), row(512), pl.BlockSpec((1, D_MODEL), lambda i: (0, 0))],
        out_specs=row(D_MODEL), out_shape=_sds((L, D_MODEL), _ACT),
    )(ya, yb, g12, mixg.reshape(1, D_MODEL))


def _merge_bwd(dy, ya, yb, g12, mixg, name):
    L = ya.shape[0]
    tl = _tile(L, _TL)

    def body(d_ref, a_ref, b_ref, g_ref, m_ref, da_ref, db_ref, dg_ref, dm_ref):
        @pl.when(pl.program_id(0) == 0)
        def _():
            dm_ref[...] = jnp.zeros_like(dm_ref)

        g12v = g_ref[...]
        g1, sg = g12v[:, :C_WIDTH], _sigmoid(g12v[:, C_WIDTH:])
        yc = g1 * sg
        outs = []
        for (lo, hi), y in zip(_GROUPS, (a_ref[...], b_ref[...], yc)):
            r = lax.rsqrt(jnp.mean(y * y, axis=-1, keepdims=True) + EPS)
            xhat = y * r
            d = d_ref[:, lo:hi]
            gy = d * m_ref[:, lo:hi]
            outs.append(r * (gy - xhat * jnp.mean(gy * xhat, axis=-1, keepdims=True)))
            dm_ref[:, lo:hi] += jnp.sum(d * xhat, axis=0, keepdims=True)
        da_ref[...] = outs[0]
        db_ref[...] = outs[1]
        dyc = outs[2]
        dg_ref[:, :C_WIDTH] = (dyc * sg).astype(dg_ref.dtype)
        dg_ref[:, C_WIDTH:] = (dyc * g1 * sg * (1.0 - sg)).astype(dg_ref.dtype)

    row = lambda w: pl.BlockSpec((tl, w), lambda i: (i, 0))
    one = pl.BlockSpec((1, D_MODEL), lambda i: (0, 0))
    return pl.pallas_call(
        body, name=name, grid=(L // tl,),
        in_specs=[row(D_MODEL), row(256), row(512), row(512), one],
        out_specs=[row(256), row(512), row(512), one],
        out_shape=[_sds((L, 256), F32), _sds((L, 512), F32),
                   _sds((L, 512), _ACT), _sds((1, D_MODEL), F32)],
    )(dy, ya, yb, g12, mixg.reshape(1, D_MODEL))


def _ple_bwd_elem(dh, gate, e, name):
    L, D = dh.shape
    tl = _tile(L, _TL)

    def body(d_ref, g_ref, e_ref, p_ref, o_ref):
        d, g = d_ref[...], g_ref[...]
        p_ref[...] = (d * e_ref[...] * g * (1.0 - g)).astype(p_ref.dtype)
        o_ref[...] = (d * g).astype(o_ref.dtype)

    row = pl.BlockSpec((tl, D), lambda i: (i, 0))
    return pl.pallas_call(
        body, name=name, grid=(L // tl,), in_specs=[row] * 3, out_specs=[row] * 2,
        out_shape=[_sds((L, D), _ACT)] * 2,
        compiler_params=_cparams(4 * _nbytes((tl, D), F32)),
    )(dh, gate, e)


def _dskip_bwd(dy, z, name):
    L = dy.shape[0]
    tl = _tile(L, _TL)

    def body(d_ref, u_ref, o_ref):
        @pl.when(pl.program_id(0) == 0)
        def _():
            o_ref[...] = jnp.zeros_like(o_ref)

        o_ref[...] += jnp.sum(d_ref[...] * u_ref[...], axis=0, keepdims=True)

    return pl.pallas_call(
        body, name=name, grid=(L // tl,),
        in_specs=[pl.BlockSpec((tl, C_WIDTH), lambda i: (i, 0)), pl.BlockSpec((tl, C_WIDTH), lambda i: (i, 5))],
        out_specs=pl.BlockSpec((1, C_WIDTH), lambda i: (0, 0)),
        out_shape=_sds((1, C_WIDTH), F32),
    )(dy, z)


def _loss_fwd_bwd(y, target):
    L, D = y.shape
    tl = _tile(L, _TL)

    def body(y_ref, t_ref, l_ref, d_ref):
        @pl.when(pl.program_id(0) == 0)
        def _():
            l_ref[...] = jnp.zeros_like(l_ref)

        e = y_ref[...] - t_ref[...]
        d_ref[...] = e * (1.0 / D)
        part = jnp.sum(jnp.sum(e * e, axis=-1, keepdims=True), axis=0, keepdims=True)
        l_ref[...] += jnp.broadcast_to(part, l_ref.shape)

    row = pl.BlockSpec((tl, D), lambda i: (i, 0))
    return pl.pallas_call(
        body, name="loss", grid=(L // tl,), in_specs=[row, row],
        out_specs=[pl.BlockSpec((SUBLANES, LANES), lambda i: (0, 0)), row],
        out_shape=[_sds((SUBLANES, LANES), F32), _sds((L, D), F32)],
    )(y, target)


def _adamw(w, g, m, v, name):
    R, C = w.shape
    tr = R if R <= 512 else _tile_rows(R, 512)

    def body(w_ref, g_ref, m_ref, v_ref, d_ref, nm_ref, nv_ref):
        gv = g_ref[...]
        nm = ADAM_B1 * m_ref[...] + (1.0 - ADAM_B1) * gv
        nv = ADAM_B2 * v_ref[...] + (1.0 - ADAM_B2) * (gv * gv)
        m_hat = nm / (1.0 - ADAM_B1 ** ADAM_STEP)
        v_hat = nv / (1.0 - ADAM_B2 ** ADAM_STEP)
        d_ref[...] = -ADAM_LR * (m_hat / (jnp.sqrt(v_hat) + ADAM_EPS) + ADAM_WD * w_ref[...])
        nm_ref[...] = nm
        nv_ref[...] = nv

    blk = pl.BlockSpec((tr, C), lambda i: (i, 0))
    return pl.pallas_call(
        body, name=name, grid=(R // tr,), in_specs=[blk] * 4, out_specs=[blk] * 3,
        out_shape=[_sds((R, C), F32)] * 3,
        compiler_params=_cparams(7 * _nbytes((tr, C), F32)),
    )(w, g, m, v)


def _tile_rows(R, pref):
    t = pref
    while R % t:
        t -= SUBLANES
    assert t > 0
    return t


def _add_n(xs, name):
    R, C = xs[0].shape
    tr = R if R <= 512 else _tile_rows(R, 512)
    n = len(xs)

    def body(*refs):
        acc = refs[0][...].astype(F32)
        for r in refs[1:n]:
            acc = acc + r[...].astype(F32)
        refs[n][...] = acc

    blk = pl.BlockSpec((tr, C), lambda i: (i, 0))
    return pl.pallas_call(
        body, name=name, grid=(R // tr,), in_specs=[blk] * n, out_specs=blk,
        out_shape=_sds((R, C), F32),
        compiler_params=_cparams((n + 1) * _nbytes((tr, C), F32)),
    )(*xs)


class _Exchange:
    def __init__(self, ins, aliased, fresh, n_sems, start, wait, done):
        self.ins, self.aliased, self.fresh, self.n_sems = ins, aliased, fresh, n_sems
        self.start, self.wait, self.done = start, wait, done


def _mm_host(lp, key, *args, **kw):
    plan = lp.get(key)
    if plan is None:
        return _mm(*args, **kw)
    if not isinstance(plan, _Exchange):
        plan = plan()
    res, outs = _mm(*args, comm=plan, **kw)
    plan.done(outs)
    return res


def _relu2(acc):
    r = jnp.maximum(acc, 0.0)
    return (r * r,)


def _rms_rows(x, g):
    return (x * lax.rsqrt(jnp.mean(x * x, axis=-1, keepdims=True) + EPS)) * g


def _resid_norm_epi(acc, res, g):
    h = res + acc
    return h, _rms_rows(h, g)


def _rms_bwd_epi(acc, h, dres, g):
    r = lax.rsqrt(jnp.mean(h * h, axis=-1, keepdims=True) + EPS)
    xhat = h * r
    gy = acc * g
    dx = r * (gy - xhat * jnp.mean(gy * xhat, axis=-1, keepdims=True))
    return dres + dx, jnp.sum(acc * xhat, axis=0, keepdims=True)


def _layer_fwd(h, xn, lp, cos, sin, g_next):
    L = h.shape[0]
    row = lambda n: lp[n].reshape(1, D_MODEL)
    z = _mm(xn, lp["w_in"], mode="nn", M=L, N=IN_COLS, K=D_MODEL, b_cb=True, out_dtypes=[F32], name="f_w_in")
    ya = _gmlp_fwd(z, lp["ws"], lp["bfull"], lp["lgf"], lp["lbf"], "f_gmlp")
    q, k = _qk_prep(z, cos, sin, lp["gq"], lp["gk"], "f_qk_prep")
    yb = _attn_fwd(q, k, z, lp["sinks"], "f_attn")
    S, y, yg = _ssm_fwd(z, lp["bcat"], lp["ccat"], lp["dskip"], lp["lam_r"], lp["lam_i"], "f_ssm")
    g12 = _mm(yg, lp["w12"], mode="nn", M=L, N=2 * C_WIDTH, K=C_WIDTH, out_dtypes=[F32], name="f_glu")
    ycat = _merge_fwd(ya, yb, g12, lp["mix_out_g"], "f_merge")
    h1, hn = _mm(ycat, lp["w_out"], mode="nn", M=L, N=D_MODEL, K=D_MODEL, extras=[(h, 0), (row("mlp_norm_g"), 0)],
                 epi=_resid_norm_epi, out_dtypes=[F32, _ACT], name="f_w_out")
    r = _mm_host(lp, "x_ff1", hn, lp["w_ff1"], mode="nn", M=L, N=D_FF, K=D_MODEL, b_cb=True, epi=_relu2,
                 out_dtypes=[_ACT], name="f_ff1")
    h2, hn3 = _mm_host(lp, "x_ff2", r, lp["w_ff2"], mode="nn", M=L, N=D_MODEL, K=D_FF,
                       extras=[(h1, 0), (row("ple_norm_g"), 0)],
                       epi=_resid_norm_epi, out_dtypes=[F32, _ACT], name="f_ff2")
    e = _mm(lp["p"], lp["w_ple_proj"], mode="nn", M=L, N=D_MODEL, K=PLE_DIM, b_cb=True, tk=PLE_DIM,
            out_dtypes=[F32], name="f_ple_proj")

    def gate_epi(acc, h2_, e_, *g):
        gate_ = _sigmoid(acc)
        h3_ = h2_ + gate_ * e_
        return (h3_, gate_) + ((_rms_rows(h3_, g[0]),) if g else ())

    outs = _mm_host(lp, "x_gate", hn3, lp["w_ple_gate"], mode="nn", M=L, N=D_MODEL, K=D_MODEL,
                    extras=[(h2, 0), (e, 0)] + ([(g_next.reshape(1, D_MODEL), 0)] if g_next is not None else []),
                    epi=gate_epi, out_dtypes=[F32, F32] + ([_ACT] if g_next is not None else []), name="f_ple_gate")
    h3, gate = outs[0], outs[1]
    xn_next = outs[2] if g_next is not None else None
    saved = dict(h=h, xn=xn, z=z, ya=ya, q=q, k=k, yb=yb, S=S, y=y, yg=yg, g12=g12, ycat=ycat, h1=h1, hn=hn,
                 r=r, h2=h2, hn3=hn3, e=e, gate=gate)
    return h3, xn_next, saved


def _layer_bwd(dh3, lp, sv, cos, sin):
    L = dh3.shape[0]
    z = sv["z"]
    dpre, de = _ple_bwd_elem(dh3, sv["gate"], sv["e"], "b_ple_elem")
    stk = {n: None for n in BIG}
    d_gate = _mm(sv["hn3"], dpre, mode="tn", M=D_MODEL, N=D_MODEL, K=L, out_dtypes=[F32], name="b_dw_gate",
                 o_stack=stk["w_ple_gate"])
    d_proj = _mm(lp["p"], de, mode="tn", M=PLE_DIM, N=D_MODEL, K=L, o_cb=True, tm=PLE_DIM,
                 out_dtypes=[F32], name="b_dw_proj", o_stack=stk["w_ple_proj"])
    row = lambda n: lp[n].reshape(1, D_MODEL)
    dh2, dg_ple = _mm_host(lp, "x_bwd0", dpre, lp["w_ple_gate"], mode="nt", M=L, N=D_MODEL, K=D_MODEL,
                           extras=[(sv["h2"], 0), (dh3, 0), (row("ple_norm_g"), 0)], epi=_rms_bwd_epi,
                           out_dtypes=[F32, F32], n_acc=1, name="b_dx_gate")
    da = _mm_host(lp, "x_bwd", dh2, lp["w_ff2"], mode="nt", M=L, N=D_FF, K=D_MODEL, extras=[(sv["r"], 0)],
                  epi=lambda acc, r_: (acc * (2.0 * jnp.sqrt(r_.astype(F32))),), out_dtypes=[_ACT], name="b_dx_ff2")
    d_ff2 = _mm_host(lp, "x_bwd2", sv["r"], dh2, mode="tn", M=D_FF, N=D_MODEL, K=L, out_dtypes=[F32], name="b_dw_ff2")
    d_ff1 = _mm(sv["hn"], da, mode="tn", M=D_MODEL, N=D_FF, K=L, o_cb=True, out_dtypes=[F32], name="b_dw_ff1",
                o_stack=stk["w_ff1"])
    dh1, dg_mlp = _mm(da, lp["w_ff1"], mode="nt", M=L, N=D_MODEL, K=D_FF, b_cb=True,
                      extras=[(sv["h1"], 0), (dh2, 0), (row("mlp_norm_g"), 0)], epi=_rms_bwd_epi,
                      out_dtypes=[F32, F32], n_acc=1, name="b_dx_ff1")
    d_out = _mm(sv["ycat"], dh1, mode="tn", M=D_MODEL, N=D_MODEL, K=L, out_dtypes=[F32], name="b_dw_out",
                o_stack=stk["w_out"])
    dycat = _mm(dh1, lp["w_out"], mode="nt", M=L, N=D_MODEL, K=D_MODEL, out_dtypes=[F32], name="b_dx_out")
    dya, dyb, dg12, dmix = _merge_bwd(dycat, sv["ya"], sv["yb"], sv["g12"], lp["mix_out_g"], "b_merge")
    d_w12 = _mm(sv["yg"], dg12, mode="tn", M=C_WIDTH, N=2 * C_WIDTH, K=L, tm=C_WIDTH, out_dtypes=[F32], name="b_dw_glu",
                o_stack=stk["w12"])
    dy = _mm(dg12, lp["w12"], mode="nt", M=L, N=C_WIDTH, K=2 * C_WIDTH, tk=2 * C_WIDTH, extras=[(sv["y"], 0)],
             epi=lambda acc, y_: (acc * _gelu_grad(y_),), out_dtypes=[F32], name="b_dx_glu")
    dzc, d_bcat, d_ccat, dd, dar, dai = _ssm_bwd(dy, z, sv["S"], lp["bcat"], lp["ccat"], lp["dskip"],
                                                 lp["lam_r"], lp["lam_i"], "b_ssm")
    dq, dkc, dkp, dvc, dvp, dsink = _attn_bwd(sv["q"], sv["k"], z, lp["sinks"], dyb, "b_attn")
    dzq, dzk, dzv, dgq, dgk = _qk_prep_bwd(z, dq, dkc, dkp, dvc, dvp, cos, sin, lp["gq"], lp["gk"], "b_qk_prep")
    dza, dws, dbs, dlg, dlb = _gmlp_bwd(z, dya, lp["ws"], lp["wsT"], lp["bfull"], lp["lgf"], lp["lbf"], "b_gmlp")
    dz = jnp.concatenate([dza, dzq, dzk, dzv, dzc], axis=1)
    d_in = _mm(sv["xn"], dz, mode="tn", M=D_MODEL, N=IN_COLS, K=L, o_cb=True, out_dtypes=[F32], name="b_dw_in",
               o_stack=stk["w_in"])
    dh, dg_attn = _mm(dz, lp["w_in"], mode="nt", M=L, N=D_MODEL, K=IN_COLS, b_cb=True,
                      extras=[(sv["h"], 0), (dh1, 0), (row("attn_norm_g"), 0)], epi=_rms_bwd_epi,
                      out_dtypes=[F32, F32], n_acc=1, name="b_dx_in")
    grads = dict(w_in=d_in, w12=d_w12, w_out=d_out, w_ff1=d_ff1, w_ff2=d_ff2, w_ple_gate=d_gate, w_ple_proj=d_proj,
                 attn_norm_g=dg_attn.reshape(D_MODEL), mlp_norm_g=dg_mlp.reshape(D_MODEL),
                 ple_norm_g=dg_ple.reshape(D_MODEL), mix_out_g=dmix.reshape(D_MODEL),
                 dws=dws, dbs=dbs, dlg=dlg, dlb=dlb, dgq=dgq, dgk=dgk, dsink=dsink,
                 dar=dar, dai=dai, d_bcat=d_bcat, d_ccat=d_ccat, dd=dd)
    return dh, grads


SMALL = ("attn_norm_g", "gmlp_ln_g", "gmlp_ln_b", "gmlp_ws", "gmlp_bs", "q_norm_g", "k_norm_g", "sinks",
         "ssm_a_re", "ssm_a_im", "ssm_log_dt", "ssm_b_re", "ssm_b_im", "ssm_c_re", "ssm_c_im", "ssm_d",
         "mix_out_g", "mlp_norm_g", "ple_norm_g")
BIG = ("w_in", "w12", "w_out", "w_ff1", "w_ff2", "w_ple_gate", "w_ple_proj")
COL_SHARDED = ("w_in", "w_ff1", "w_ple_proj")


def _block_diag(t):
    nl, g, a, b = t.shape
    eye = jnp.eye(g, dtype=t.dtype)
    return (t[:, :, :, None, :] * eye[None, :, None, :, None]).reshape(nl, g * a, g * b)


def _diag_blocks(t, a, b):
    nl = t.shape[0]
    t = t.reshape(nl, C_GROUPS, a, C_GROUPS, b)
    idx = jnp.arange(C_GROUPS)
    return jnp.moveaxis(t[:, idx, :, idx, :], 0, 1)


def _local_step(x, p, positions, target, sw, bw):
    nl = sw["attn_norm_g"].shape[0]
    G = nl * C_GROUPS
    zeros = lambda *s: jnp.zeros(s, F32)
    are = sw["ssm_a_re"].reshape(G, 1, C_STATE)
    aim = sw["ssm_a_im"].reshape(G, 1, C_STATE)
    ldt = jnp.broadcast_to(sw["ssm_log_dt"][..., None], (nl, C_GROUPS, C_STATE)).reshape(G, 1, C_STATE)
    bre = jnp.swapaxes(sw["ssm_b_re"], -1, -2).reshape(G, C_GROUP, C_STATE)
    bim = jnp.swapaxes(sw["ssm_b_im"], -1, -2).reshape(G, C_GROUP, C_STATE)
    lr, li, bbr, bbi = _ssm_prep(are, aim, ldt, bre, bim)
    unflat = lambda t: t.reshape(nl, C_GROUPS, C_GROUP, C_STATE)
    lp = dict(
        attn_norm_g=sw["attn_norm_g"], mlp_norm_g=sw["mlp_norm_g"], ple_norm_g=sw["ple_norm_g"],
        mix_out_g=sw["mix_out_g"], sinks=sw["sinks"],
        ws=sw["gmlp_ws"], wsT=jnp.swapaxes(sw["gmlp_ws"], -1, -2),
        bfull=jnp.concatenate([zeros(nl, A_HEADS, CHUNK, HEAD_DIM),
                               jnp.broadcast_to(sw["gmlp_bs"][..., None], (nl, A_HEADS, CHUNK, HEAD_DIM))], axis=-1),
        lgf=jnp.concatenate([zeros(nl, A_HEADS, HEAD_DIM), sw["gmlp_ln_g"]], axis=-1),
        lbf=jnp.concatenate([zeros(nl, A_HEADS, HEAD_DIM), sw["gmlp_ln_b"]], axis=-1),
        gq=jnp.tile(sw["q_norm_g"], (1, 2)).reshape(nl, 1, LANES),
        gk=jnp.tile(sw["k_norm_g"], (1, 2)).reshape(nl, 1, LANES),
        lam_r=lr.reshape(nl, 1, N_STATE), lam_i=li.reshape(nl, 1, N_STATE),
        bcat=jnp.concatenate([_block_diag(unflat(bbr)), _block_diag(unflat(bbi))], axis=-1),
        ccat=jnp.concatenate([_block_diag(jnp.swapaxes(sw["ssm_c_re"], -1, -2)),
                              -_block_diag(jnp.swapaxes(sw["ssm_c_im"], -1, -2))], axis=1),
        dskip=sw["ssm_d"].reshape(nl, 1, C_WIDTH))
    cos, sin = _rope_tables(positions)

    def layer_params(l, hooks):
        lpi = {n: v[l] for n, v in lp.items()}
        lpi.update(bw.layer(l))
        lpi["p"] = (p, l)
        lpi.update(hooks)
        return lpi

    h, saved = x, []
    xn = _rms_fwd(x, sw["attn_norm_g"][0], "f_norm_attn")
    for l in range(nl):
        g_next = sw["attn_norm_g"][l + 1] if l + 1 < nl else None
        h, xn, sv = _layer_fwd(h, xn, layer_params(l, bw.fwd_hooks(l)), cos, sin, g_next)
        saved.append(sv)
    sse, dh = _loss_fwd_bwd(h, target)

    per_layer = [None] * nl
    for l in reversed(range(nl)):
        dh, gl = _layer_bwd(dh, layer_params(l, bw.bwd_hooks(l)), saved[l], cos, sin)
        bw.grads(l, {n: gl.pop(n) for n in BIG})
        per_layer[l] = gl
    grad_x = dh
    g = {n: jnp.stack([per_layer[l][n] for l in range(nl)]) for n in per_layer[0]}

    d_bcat = g["d_bcat"]
    dbr = _diag_blocks(d_bcat[:, :, :N_STATE], C_GROUP, C_STATE).reshape(G, C_GROUP, C_STATE)
    dbi = _diag_blocks(d_bcat[:, :, N_STATE:], C_GROUP, C_STATE).reshape(G, C_GROUP, C_STATE)
    dlr = g["dar"][:, 0].reshape(G, 1, C_STATE)
    dli = g["dai"][:, 0].reshape(G, 1, C_STATE)
    g_are, g_aim, g_ldt, g_bre, g_bim = _ssm_prep_bwd(are, aim, ldt, bre, bim, dlr, dli, dbr, dbi)
    d_ccat = g["d_ccat"]
    sg = dict(
        attn_norm_g=g["attn_norm_g"], mlp_norm_g=g["mlp_norm_g"], ple_norm_g=g["ple_norm_g"], mix_out_g=g["mix_out_g"],
        gmlp_ln_g=g["dlg"][:, :, 0, HEAD_DIM:], gmlp_ln_b=g["dlb"][:, :, 0, HEAD_DIM:],
        gmlp_ws=g["dws"], gmlp_bs=g["dbs"][:, :, :, HEAD_DIM],
        q_norm_g=g["dgq"][:, 0, :HEAD_DIM], k_norm_g=g["dgk"][:, 0, :HEAD_DIM],
        sinks=g["dsink"][:, 0, :B_Q_HEADS],
        ssm_a_re=g_are.reshape(nl, C_GROUPS, C_STATE), ssm_a_im=g_aim.reshape(nl, C_GROUPS, C_STATE),
        ssm_log_dt=g_ldt[:, 0, 0].reshape(nl, C_GROUPS),
        ssm_b_re=jnp.swapaxes(g_bre.reshape(nl, C_GROUPS, C_GROUP, C_STATE), -1, -2),
        ssm_b_im=jnp.swapaxes(g_bim.reshape(nl, C_GROUPS, C_GROUP, C_STATE), -1, -2),
        ssm_c_re=jnp.swapaxes(_diag_blocks(d_ccat[:, :N_STATE], C_STATE, C_GROUP), -1, -2),
        ssm_c_im=-jnp.swapaxes(_diag_blocks(d_ccat[:, N_STATE:], C_STATE, C_GROUP), -1, -2),
        ssm_d=g["dd"].reshape(nl, C_GROUPS, C_GROUP),
    )
    return (sse[0, 0], grad_x, sg) + tuple(bw.finish(_pack(sg)))


_ANY = pl.BlockSpec(memory_space=pl.ANY)
N_LAYERS = 4


def _mesh_pos():
    x, y, c = lax.axis_index("x"), lax.axis_index("y"), lax.axis_index("c")
    chips = [(1 - x, y), (x, 1 - y), (1 - x, 1 - y)]
    return x, y, c, 2 * x + y, chips


def _cast_into_slot(ws, j, name):
    nl, R, _ = ws[0].shape
    widths = [w.shape[2] for w in ws]
    C = sum(widths)
    tr = R if R <= 512 else _tile_rows(R, 512)
    nw = len(ws)

    def body(s_ref, *refs):
        o_ref = refs[nw]
        off = 0
        for r, wd in zip(refs[:nw], widths):
            o_ref[:, off:off + wd] = r[...].astype(o_ref.dtype)
            off += wd

    return pl.pallas_call(
        body, name=name,
        grid_spec=pltpu.PrefetchScalarGridSpec(
            num_scalar_prefetch=1, grid=(nl, R // tr),
            in_specs=[pl.BlockSpec((None, tr, wd), lambda l, i, s: (l, i, 0)) for wd in widths],
            out_specs=pl.BlockSpec((None, None, tr, C), lambda l, i, s: (l, s[0], i, 0))),
        out_shape=_sds((nl, N_CHIPS, R, C), _MXU),
    )(jnp.reshape(j, (1,)).astype(jnp.int32), *ws)


def _gather_weights(bufs):
    nk = len(bufs)

    def body(*refs):
        ins, outs = refs[:nk], refs[nk:2 * nk]
        send_sems, recv_sems = refs[2 * nk:]
        x, y, c, j, chips = _mesh_pos()
        mine, other = pl.ds(2 * c, 2), pl.ds(2 * (1 - c), 2)

        def ici(t, q):
            cx, cy = chips[q]
            return pltpu.make_async_remote_copy(
                src_ref=ins[t].at[mine, j], dst_ref=outs[t].at[mine, j],
                send_sem=send_sems.at[6 * t + q], recv_sem=recv_sems.at[6 * t + q],
                device_id=(cx, cy, c), device_id_type=MESH)

        def landed(t, q):
            cx, cy = chips[q]
            blk = outs[t].at[mine, 2 * cx + cy]
            return pltpu.make_async_remote_copy(
                src_ref=blk, dst_ref=blk, send_sem=send_sems.at[6 * t + q], recv_sem=recv_sems.at[6 * t + q],
                device_id=(cx, cy, c), device_id_type=MESH)

        def fwd(t, q, rows):
            cx, cy = chips[q]
            blk = outs[t].at[rows, 2 * cx + cy]
            return pltpu.make_async_remote_copy(
                src_ref=blk, dst_ref=blk, send_sem=send_sems.at[6 * t + 3 + q], recv_sem=recv_sems.at[6 * t + 3 + q],
                device_id=(x, y, 1 - c), device_id_type=MESH)

        for t in range(nk):
            for q in range(3):
                ici(t, q).start()
        for t in range(nk):
            for q in range(3):
                landed(t, q).wait_recv()
                fwd(t, q, mine).start()
        for t in range(nk):
            for q in range(3):
                fwd(t, q, other).wait_recv()
        for t in range(nk):
            for q in range(3):
                ici(t, q).wait_send()
                fwd(t, q, mine).wait_send()

    return pl.pallas_call(
        body, name="gather_weights", in_specs=[_ANY] * nk, out_specs=[_ANY] * nk,
        out_shape=[_sds(b.shape, b.dtype) for b in bufs],
        input_output_aliases={t: t for t in range(nk)},
        scratch_shapes=[pltpu.SemaphoreType.DMA((6 * nk,)), pltpu.SemaphoreType.DMA((6 * nk,))],
    )(*bufs)


def _exchange_sibling_half(gl):
    nk = len(gl)

    def body(*refs):
        ins, outs = refs[:nk], refs[nk:2 * nk]
        send_sems, recv_sems = refs[2 * nk:]
        x, y, c, _, _ = _mesh_pos()
        cps = [pltpu.make_async_remote_copy(
            src_ref=ins[t].at[pl.ds(2 * (1 - c), 2)], dst_ref=outs[t],
            send_sem=send_sems.at[t], recv_sem=recv_sems.at[t],
            device_id=(x, y, 1 - c), device_id_type=MESH) for t in range(nk)]
        for cp in cps:
            cp.start()
        for cp in cps:
            cp.wait()

    return pl.pallas_call(
        body, name="reduce_sibling", in_specs=[_ANY] * nk, out_specs=[_ANY] * nk,
        out_shape=[_sds((2,) + g.shape[1:], g.dtype) for g in gl],
        scratch_shapes=[pltpu.SemaphoreType.DMA((nk,)), pltpu.SemaphoreType.DMA((nk,))],
    )(*gl)


def _exchange_chips(ps):
    nk = len(ps)

    def body(*refs):
        ins, outs = refs[:nk], refs[nk:2 * nk]
        send_sems, recv_sems = refs[2 * nk:]
        x, y, c, j, chips = _mesh_pos()

        def send(t, q):
            cx, cy = chips[q]
            return pltpu.make_async_remote_copy(
                src_ref=ins[t].at[:, 2 * cx + cy], dst_ref=outs[t].at[j],
                send_sem=send_sems.at[3 * t + q], recv_sem=recv_sems.at[3 * t + q],
                device_id=(cx, cy, c), device_id_type=MESH)

        def landed(t, q):
            cx, cy = chips[q]
            blk = outs[t].at[2 * cx + cy]
            return pltpu.make_async_remote_copy(
                src_ref=blk, dst_ref=blk, send_sem=send_sems.at[3 * t + q], recv_sem=recv_sems.at[3 * t + q],
                device_id=(cx, cy, c), device_id_type=MESH)

        for t in range(nk):
            for q in range(3):
                send(t, q).start()
        for t in range(nk):
            for q in range(3):
                landed(t, q).wait_recv()
        for t in range(nk):
            for q in range(3):
                send(t, q).wait_send()

    return pl.pallas_call(
        body, name="reduce_chips", in_specs=[_ANY] * nk, out_specs=[_ANY] * nk,
        out_shape=[_sds((N_CHIPS, 2) + p.shape[2:], p.dtype) for p in ps],
        scratch_shapes=[pltpu.SemaphoreType.DMA((3 * nk,)), pltpu.SemaphoreType.DMA((3 * nk,))],
    )(*ps)


def _share_sibling(fs):
    nk = len(fs)

    def body(*refs):
        ins, outs = refs[:nk], refs[nk:2 * nk]
        send_sems, recv_sems = refs[2 * nk:]
        x, y, c, _, _ = _mesh_pos()
        mine = pl.ds(2 * c, 2)
        cps = [pltpu.make_async_remote_copy(
            src_ref=ins[t].at[mine], dst_ref=outs[t].at[mine], send_sem=send_sems.at[t], recv_sem=recv_sems.at[t],
            device_id=(x, y, 1 - c), device_id_type=MESH) for t in range(nk)]
        for cp in cps:
            cp.start()
        for cp in cps:
            cp.wait_send()
        for t in range(nk):
            blk = outs[t].at[pl.ds(2 * (1 - c), 2)]
            pltpu.make_async_remote_copy(
                src_ref=blk, dst_ref=blk, send_sem=send_sems.at[t], recv_sem=recv_sems.at[t],
                device_id=(x, y, 1 - c), device_id_type=MESH).wait_recv()

    return pl.pallas_call(
        body, name="share_sibling", in_specs=[_ANY] * nk, out_specs=[_ANY] * nk,
        out_shape=[_sds(f.shape, f.dtype) for f in fs],
        input_output_aliases={t: t for t in range(nk)},
        scratch_shapes=[pltpu.SemaphoreType.DMA((nk,)), pltpu.SemaphoreType.DMA((nk,))],
    )(*fs)


def _add_own_half(gl, r1, c, name):
    _, ns, R, C = gl.shape
    rows = 2 * ns * R
    tr = _tile_rows(rows, 512)
    nblk = rows // tr

    def body(s_ref, a_ref, b_ref, o_ref):
        o_ref[...] = (a_ref[...] + b_ref[...]).astype(o_ref.dtype)

    out = pl.pallas_call(
        body, name=name,
        grid_spec=pltpu.PrefetchScalarGridSpec(
            num_scalar_prefetch=1, grid=(nblk,),
            in_specs=[pl.BlockSpec((tr, C), lambda i, s: (s[0] * nblk + i, 0)), pl.BlockSpec((tr, C), lambda i, s: (i, 0))],
            out_specs=pl.BlockSpec((tr, C), lambda i, s: (i, 0))),
        out_shape=_sds((rows, C), _WIRE),
        compiler_params=_cparams(3 * _nbytes((tr, C), F32)),
    )(jnp.reshape(c, (1,)).astype(jnp.int32), gl.reshape(2 * rows, C), r1.reshape(rows, C))
    return out.reshape(2, ns, R, C)


def _add_chips(p, r2, j, c, name):
    _, ns, R, C = p.shape
    tr = R if R <= 512 else _tile_rows(R, 512)

    def body(s_ref, own, a1, a2, a3, o_ref):
        f = lambda r: r[...].astype(F32)
        o_ref[...] = ((f(own) + f(a1)) + f(a2)) + f(a3)

    blk = (None, None, tr, C)
    return pl.pallas_call(
        body, name=name,
        grid_spec=pltpu.PrefetchScalarGridSpec(
            num_scalar_prefetch=1, grid=(2, R // tr),
            in_specs=[pl.BlockSpec(blk, lambda h, i, s: (h, s[0], i, 0))]
            + [pl.BlockSpec(blk, lambda h, i, s, k=k: ((s[0] + k) % N_CHIPS, h, i, 0)) for k in (1, 2, 3)],
            out_specs=pl.BlockSpec((None, tr, C), lambda h, i, s: (2 * s[1] + h, i, 0))),
        out_shape=_sds((N_LAYERS, R, C), F32),
        compiler_params=_cparams(6 * _nbytes((tr, C), F32)),
    )(jnp.stack([j, c]).astype(jnp.int32), p, r2, r2, r2)


def _allreduce_small(buf, plan=None):
    Rs = buf.shape[0]
    nx = 0 if plan is None else len(plan.ins)
    x_out_shape = [] if plan is None else [_sds(sh, dt) for sh, dt in plan.fresh]
    assert plan is None or not any(plan.aliased)
    nxo = len(x_out_shape)

    def body(*refs):
        b_ref, x_ins = refs[0], refs[1:1 + nx]
        o_ref, x_outs = refs[1 + nx], refs[2 + nx:2 + nx + nxo]
        t_ref, slots_ref, send_sems, recv_sems = refs[2 + nx + nxo:6 + nx + nxo]
        x_sems = refs[6 + nx + nxo:]
        if plan is not None:
            plan.start(x_ins, x_outs, *x_sems)
        x, y, c, j, chips = _mesh_pos()
        sib = pltpu.make_async_remote_copy(
            src_ref=b_ref, dst_ref=t_ref, send_sem=send_sems.at[0], recv_sem=recv_sems.at[0],
            device_id=(x, y, 1 - c), device_id_type=MESH)
        sib.start()
        sib.wait()
        slots_ref[j] = b_ref[...] + t_ref[...]

        def send(q):
            cx, cy = chips[q]
            return pltpu.make_async_remote_copy(
                src_ref=slots_ref.at[j], dst_ref=slots_ref.at[j], send_sem=send_sems.at[1 + q],
                recv_sem=recv_sems.at[1 + q], device_id=(cx, cy, c), device_id_type=MESH)

        def landed(q):
            cx, cy = chips[q]
            blk = slots_ref.at[2 * cx + cy]
            return pltpu.make_async_remote_copy(
                src_ref=blk, dst_ref=blk, send_sem=send_sems.at[1 + q], recv_sem=recv_sems.at[1 + q],
                device_id=(cx, cy, c), device_id_type=MESH)

        for q in range(3):
            send(q).start()
        for q in range(3):
            landed(q).wait_recv()
        for q in range(3):
            send(q).wait_send()
        o_ref[...] = ((slots_ref[0] + slots_ref[1]) + slots_ref[2]) + slots_ref[3]
        if plan is not None:
            plan.wait(x_ins, x_outs, *x_sems)

    vm = pl.BlockSpec(memory_space=pltpu.VMEM)
    outs = pl.pallas_call(
        body, name="allreduce_small", in_specs=[vm] + [_ANY] * nx, out_specs=[vm] + [_ANY] * nxo,
        out_shape=[_sds((Rs, LANES), F32)] + x_out_shape,
        scratch_shapes=[pltpu.VMEM((Rs, LANES), F32), pltpu.VMEM((N_CHIPS, Rs, LANES), F32),
                        pltpu.SemaphoreType.DMA((4,)), pltpu.SemaphoreType.DMA((4,))]
        + ([pltpu.SemaphoreType.DMA((plan.n_sems,))] * 2 if plan is not None else []),
        compiler_params=_cparams(4 * _nbytes((Rs, LANES), F32)),
    )(buf, *([] if plan is None else plan.ins))
    if plan is not None:
        plan.done(list(outs[1:]))
    return outs[0]


def _own_rows(c, R):
    return pl.ds(c * (R // 2), R // 2)


def _cast_layer_slot(ws, l, j, name):
    _, R, _ = ws[0].shape
    widths = [w.shape[2] for w in ws]
    C = sum(widths)
    tr = R if R <= 512 else _tile_rows(R, 512)
    nw = len(ws)

    def body(s_ref, *refs):
        o_ref = refs[nw]
        off = 0
        for r, wd in zip(refs[:nw], widths):
            o_ref[:, off:off + wd] = r[...].astype(o_ref.dtype)
            off += wd

    return pl.pallas_call(
        body, name=name,
        grid_spec=pltpu.PrefetchScalarGridSpec(
            num_scalar_prefetch=1, grid=(R // tr,),
            in_specs=[pl.BlockSpec((None, tr, wd), lambda i, s: (l, i, 0)) for wd in widths],
            out_specs=pl.BlockSpec((None, tr, C), lambda i, s: (s[0], i, 0))),
        out_shape=_sds((N_CHIPS, R, C), _MXU),
    )(jnp.reshape(j, (1,)).astype(jnp.int32), *ws)


def _gather_ici(bufs, done):
    nk = len(bufs)

    def copy(ins, outs, ss, rs, t, q, landed):
        x, y, c, j, chips = _mesh_pos()
        cx, cy = chips[q]
        rows = _own_rows(c, ins[t].shape[1])
        src = outs[t].at[2 * cx + cy, rows] if landed else ins[t].at[j, rows]
        dst = outs[t].at[2 * cx + cy, rows] if landed else outs[t].at[j, rows]
        return pltpu.make_async_remote_copy(src_ref=src, dst_ref=dst, send_sem=ss.at[3 * t + q], recv_sem=rs.at[3 * t + q],
                                            device_id=(cx, cy, c), device_id_type=MESH)

    def start(ins, outs, ss, rs):
        for t in range(nk):
            for q in range(3):
                copy(ins, outs, ss, rs, t, q, False).start()

    def wait(ins, outs, ss, rs):
        for t in range(nk):
            for q in range(3):
                copy(ins, outs, ss, rs, t, q, True).wait_recv()
                copy(ins, outs, ss, rs, t, q, False).wait_send()

    return _Exchange(bufs, [True] * nk, [], 3 * nk, start, wait, done)


def _gather_d2d(bufs, done):
    nk = len(bufs)

    def copy(ins, outs, ss, rs, t, q, mine):
        x, y, c, j, chips = _mesh_pos()
        cx, cy = chips[q]
        rows = _own_rows(c if mine else 1 - c, ins[t].shape[1])
        src = (ins if mine else outs)[t].at[2 * cx + cy, rows]
        return pltpu.make_async_remote_copy(src_ref=src, dst_ref=outs[t].at[2 * cx + cy, rows],
                                            send_sem=ss.at[3 * t + q], recv_sem=rs.at[3 * t + q],
                                            device_id=(x, y, 1 - c), device_id_type=MESH)

    def start(ins, outs, ss, rs):
        for t in range(nk):
            for q in range(3):
                copy(ins, outs, ss, rs, t, q, True).start()

    def wait(ins, outs, ss, rs):
        for t in range(nk):
            for q in range(3):
                copy(ins, outs, ss, rs, t, q, False).wait_recv()
                copy(ins, outs, ss, rs, t, q, True).wait_send()

    return _Exchange(bufs, [True] * nk, [], 3 * nk, start, wait, done)


def _reduce_d2d(gl, done):
    nk = len(gl)

    def copy(ins, outs, ss, rs, t):
        x, y, c, _, _ = _mesh_pos()
        return pltpu.make_async_remote_copy(
            src_ref=ins[t].at[:, _own_rows(1 - c, ins[t].shape[1])], dst_ref=outs[t],
            send_sem=ss.at[t], recv_sem=rs.at[t], device_id=(x, y, 1 - c), device_id_type=MESH)

    def start(ins, outs, ss, rs):
        for t in range(nk):
            copy(ins, outs, ss, rs, t).start()

    def wait(ins, outs, ss, rs):
        for t in range(nk):
            copy(ins, outs, ss, rs, t).wait()

    fresh = [((N_CHIPS, g.shape[1] // 2, g.shape[2]), g.dtype) for g in gl]
    return _Exchange(gl, [False] * nk, fresh, nk, start, wait, done)


def _reduce_ici(ps, done):
    nk = len(ps)

    def copy(ins, outs, ss, rs, t, q, landed):
        x, y, c, j, chips = _mesh_pos()
        cx, cy = chips[q]
        src = outs[t].at[2 * cx + cy] if landed else ins[t].at[2 * cx + cy]
        dst = outs[t].at[2 * cx + cy] if landed else outs[t].at[j]
        return pltpu.make_async_remote_copy(src_ref=src, dst_ref=dst, send_sem=ss.at[3 * t + q], recv_sem=rs.at[3 * t + q],
                                            device_id=(cx, cy, c), device_id_type=MESH)

    def start(ins, outs, ss, rs):
        for t in range(nk):
            for q in range(3):
                copy(ins, outs, ss, rs, t, q, False).start()

    def wait(ins, outs, ss, rs):
        for t in range(nk):
            for q in range(3):
                copy(ins, outs, ss, rs, t, q, True).wait_recv()
                copy(ins, outs, ss, rs, t, q, False).wait_send()

    return _Exchange(ps, [False] * nk, [(p_.shape, p_.dtype) for p_ in ps], 3 * nk, start, wait, done)


def _share_d2d(fs, done):
    nk = len(fs)

    def copy(ins, outs, ss, rs, t, mine):
        x, y, c, _, _ = _mesh_pos()
        rows = _own_rows(c if mine else 1 - c, ins[t].shape[1])
        src = (ins if mine else outs)[t].at[:, rows]
        return pltpu.make_async_remote_copy(src_ref=src, dst_ref=outs[t].at[:, rows], send_sem=ss.at[t], recv_sem=rs.at[t],
                                            device_id=(x, y, 1 - c), device_id_type=MESH)

    def start(ins, outs, ss, rs):
        for t in range(nk):
            copy(ins, outs, ss, rs, t, True).start()

    def wait(ins, outs, ss, rs):
        for t in range(nk):
            copy(ins, outs, ss, rs, t, False).wait_recv()
            copy(ins, outs, ss, rs, t, True).wait_send()

    return _Exchange(fs, [True] * nk, [], nk, start, wait, done)


def _run_exchange(plan, name):
    nin = len(plan.ins)
    out_shape = [_sds(x_.shape, x_.dtype) for x_, al in zip(plan.ins, plan.aliased) if al]
    aliases, k = {}, 0
    for t, al in enumerate(plan.aliased):
        if al:
            aliases[t] = k
            k += 1
    out_shape += [_sds(sh, dt) for sh, dt in plan.fresh]
    nout = len(out_shape)

    def body(*refs):
        ins, outs, sems = refs[:nin], refs[nin:nin + nout], refs[nin + nout:]
        plan.start(ins, outs, *sems)
        plan.wait(ins, outs, *sems)

    outs = pl.pallas_call(
        body, name=name, in_specs=[_ANY] * nin, out_specs=[_ANY] * nout, out_shape=out_shape,
        input_output_aliases=aliases,
        scratch_shapes=[pltpu.SemaphoreType.DMA((plan.n_sems,))] * 2,
    )(*plan.ins)
    plan.done(list(outs))


def _add_sibling_rows(g, r1, c, name):
    ns, R, C = g.shape
    hr = R // 2
    tr = hr if hr <= 512 else _tile_rows(hr, 512)
    nblk = hr // tr

    def body(s_ref, a_ref, b_ref, o_ref):
        o_ref[...] = (a_ref[...] + b_ref[...]).astype(o_ref.dtype)

    blk = (None, tr, C)
    return pl.pallas_call(
        body, name=name,
        grid_spec=pltpu.PrefetchScalarGridSpec(
            num_scalar_prefetch=1, grid=(ns, nblk),
            in_specs=[pl.BlockSpec(blk, lambda s_, i, s: (s_, s[0] * nblk + i, 0)), pl.BlockSpec(blk, lambda s_, i, s: (s_, i, 0))],
            out_specs=pl.BlockSpec(blk, lambda s_, i, s: (s_, i, 0))),
        out_shape=_sds((ns, hr, C), _WIRE),
        compiler_params=_cparams(3 * _nbytes((tr, C), F32)),
    )(jnp.reshape(c, (1,)).astype(jnp.int32), g, r1)


def _add_chip_rows(p_, r2, f, l, j, c, name):
    _, hr, C = p_.shape
    tr = hr if hr <= 512 else _tile_rows(hr, 512)
    nblk = hr // tr

    def body(s_ref, own, a1, a2, a3, f_ref, o_ref):
        v = lambda r: r[...].astype(F32)
        o_ref[...] = ((v(own) + v(a1)) + v(a2)) + v(a3)

    blk = (None, tr, C)
    return pl.pallas_call(
        body, name=name,
        grid_spec=pltpu.PrefetchScalarGridSpec(
            num_scalar_prefetch=1, grid=(nblk,),
            in_specs=[pl.BlockSpec(blk, lambda i, s: (s[0], i, 0))]
            + [pl.BlockSpec(blk, lambda i, s, k=k: ((s[0] + k) % N_CHIPS, i, 0)) for k in (1, 2, 3)]
            + [pl.BlockSpec(memory_space=pl.ANY)],
            out_specs=pl.BlockSpec(blk, lambda i, s: (l, s[1] * nblk + i, 0))),
        out_shape=_sds(f.shape, F32),
        input_output_aliases={5: 0},
        compiler_params=_cparams(6 * _nbytes((tr, C), F32)),
    )(jnp.stack([j, c]).astype(jnp.int32), p_, r2, r2, r2, f)


class _ShardedWeights:
    def __init__(self, a, j, c):
        self.j, self.c = j, c
        shards = dict(w_in=[a["w_in"]], w12=[a["glu_w1"], a["glu_w2"]], w_out=[a["w_out"]], w_ff1=[a["w_ff1"]],
                      w_ff2=[a["w_ff2"]], w_ple_gate=[a["w_ple_gate"]], w_ple_proj=[a["w_ple_proj"]])
        self.bufs = [[_cast_layer_slot(shards[n], l, j, "cast_%s_%d" % (n, l)) for n in BIG] for l in range(N_LAYERS)]
        _run_exchange(_gather_ici(self.bufs[0], lambda o: self._set_bufs(0, o)), "gather_ici_0")
        _run_exchange(_gather_d2d(self.bufs[0], lambda o: self._set_bufs(0, o)), "gather_d2d_0")
        self.raw = None
        self.pending = None
        self.final = [lax.empty((N_LAYERS,) + b.shape[1:], F32) for b in self.bufs[0]]

    def _set_bufs(self, l, outs):
        self.bufs[l] = outs

    def layer(self, l):
        return {n: (b if n in COL_SHARDED else b.reshape(N_CHIPS * b.shape[1], b.shape[2]))
                for n, b in zip(BIG, self.bufs[l])}

    _FIRST, _SECOND = (0, 3), (1, 2, 4, 5, 6)

    def _gather_part(self, nxt, idx):
        def done(outs):
            for i, o in zip(idx, outs):
                self.bufs[nxt][i] = o
        return _gather_ici([self.bufs[nxt][i] for i in idx], done)

    def fwd_hooks(self, l):
        if l + 1 == N_LAYERS:
            return {}
        nxt = l + 1
        return dict(x_ff1=lambda: self._gather_part(nxt, self._FIRST),
                    x_ff2=lambda: self._gather_part(nxt, self._SECOND),
                    x_gate=lambda: _gather_d2d(self.bufs[nxt], lambda o: self._set_bufs(nxt, o)))

    def _sibling_done(self, lyr, gl, got):
        ps = [_add_sibling_rows(g_, r1, self.c, "reduce_add_sibling_%s_%d" % (n, lyr)) for g_, r1, n in zip(gl, got, BIG)]
        self.pending = (lyr, ps)

    def _reduce_part(self, idx):
        lyr, ps = self.pending

        def done(r2):
            for i, r in zip(idx, r2):
                self.final[i] = _add_chip_rows(ps[i], r, self.final[i], lyr, self.j, self.c,
                                               "reduce_add_chips_%s_%d" % (BIG[i], lyr))
        return _reduce_ici([ps[i] for i in idx], done)

    def bwd_hooks(self, l):
        if self.raw is None:
            return {}
        lyr, gl = self.raw
        self.raw = None
        return dict(x_bwd0=lambda: _reduce_d2d(gl, lambda got: self._sibling_done(lyr, gl, got)),
                    x_bwd=lambda: self._reduce_part(self._FIRST),
                    x_bwd2=lambda: self._reduce_part(self._SECOND))

    def grads(self, l, g):
        gl = [g[n] if n in COL_SHARDED else g[n].reshape(N_CHIPS, g[n].shape[0] // N_CHIPS, g[n].shape[1]) for n in BIG]
        self.raw = (l, gl)

    def finish(self, small):
        lyr, gl = self.raw
        _run_exchange(_reduce_d2d(gl, lambda got: self._sibling_done(lyr, gl, got)), "reduce_d2d_%d" % lyr)
        small = _allreduce_small(small, self._reduce_part(self._FIRST + self._SECOND))
        out = []
        _run_exchange(_share_d2d(self.final, out.extend), "share_d2d")
        return dict(zip(BIG, out)), small


def _rows_of(shape):
    return -(-int(np.prod(shape)) // (SUBLANES * LANES)) * SUBLANES


def _pack(d):
    parts = []
    for n in SMALL:
        flat = d[n].reshape(-1)
        parts.append(jnp.pad(flat, (0, _rows_of(flat.shape) * LANES - flat.shape[0])).reshape(-1, LANES))
    return jnp.concatenate(parts, axis=0)


def _unpack(buf, like):
    out, r0 = {}, 0
    for n in SMALL:
        shape = like[n].shape
        size, nr = int(np.prod(shape)), _rows_of(shape)
        piece = lax.optimization_barrier(buf[r0:r0 + nr])
        out[n] = piece.reshape(-1)[:size].reshape(shape)
        r0 += nr
    return out


ARGS = ("x", "p", "positions", "attn_norm_g", "w_in", "gmlp_ln_g", "gmlp_ln_b", "gmlp_ws", "gmlp_bs", "q_norm_g",
        "k_norm_g", "sinks", "ssm_a_re", "ssm_a_im", "ssm_log_dt", "ssm_b_re", "ssm_b_im", "ssm_c_re", "ssm_c_im",
        "ssm_d", "glu_w1", "glu_w2", "mix_out_g", "w_out", "mlp_norm_g", "w_ff1", "w_ff2", "ple_norm_g", "w_ple_gate",
        "w_ple_proj")
WEIGHTS = ARGS[3:]


def kernel(x, p, positions, attn_norm_g, w_in, gmlp_ln_g, gmlp_ln_b, gmlp_ws, gmlp_bs, q_norm_g, k_norm_g, sinks, ssm_a_re, ssm_a_im, ssm_log_dt, ssm_b_re, ssm_b_im, ssm_c_re, ssm_c_im, ssm_d, glu_w1, glu_w2, mix_out_g, w_out, mlp_norm_g, w_ff1, w_ff2, ple_norm_g, w_ple_gate, w_ple_proj, loss_target, m_attn_norm_g, m_w_in, m_gmlp_ln_g, m_gmlp_ln_b, m_gmlp_ws, m_gmlp_bs, m_q_norm_g, m_k_norm_g, m_sinks, m_ssm_a_re, m_ssm_a_im, m_ssm_log_dt, m_ssm_b_re, m_ssm_b_im, m_ssm_c_re, m_ssm_c_im, m_ssm_d, m_glu_w1, m_glu_w2, m_mix_out_g, m_w_out, m_mlp_norm_g, m_w_ff1, m_w_ff2, m_ple_norm_g, m_w_ple_gate, m_w_ple_proj, v_attn_norm_g, v_w_in, v_gmlp_ln_g, v_gmlp_ln_b, v_gmlp_ws, v_gmlp_bs, v_q_norm_g, v_k_norm_g, v_sinks, v_ssm_a_re, v_ssm_a_im, v_ssm_log_dt, v_ssm_b_re, v_ssm_b_im, v_ssm_c_re, v_ssm_c_im, v_ssm_d, v_glu_w1, v_glu_w2, v_mix_out_g, v_w_out, v_mlp_norm_g, v_w_ff1, v_w_ff2, v_ple_norm_g, v_w_ple_gate, v_w_ple_proj):
    a = dict(locals())
    L = a["x"].shape[1]
    nl = N_LAYERS
    c = lax.axis_index("c")
    j = 2 * lax.axis_index("x") + lax.axis_index("y")

    sw = {n: a[n] for n in SMALL}
    sse, gx, _, big_grads, small_sum = _local_step(
        a["x"].reshape(L, D_MODEL), a["p"].reshape(nl, L, PLE_DIM), a["positions"].reshape(L),
        a["loss_target"].reshape(L, D_MODEL), sw, _ShardedWeights(a, j, c))
    loss = lax.psum(sse * (0.5 / D_MODEL), ("x", "y", "c"))
    g12 = big_grads.pop("w12")
    big_grads["glu_w1"], big_grads["glu_w2"] = g12[:, :, :C_WIDTH], g12[:, :, C_WIDTH:]

    small_grads = _unpack(small_sum, sw)

    grads, delta, new_m, new_v = {}, {}, {}, {}
    d_s, m_s, v_s = _adamw(_pack(sw), _pack(small_grads), _pack({n: a["m_" + n] for n in SMALL}),
                           _pack({n: a["v_" + n] for n in SMALL}), "adamw_small")
    grads.update(small_grads)
    delta.update(_unpack(d_s, sw))
    new_m.update(_unpack(m_s, sw))
    new_v.update(_unpack(v_s, sw))
    for n, g in big_grads.items():
        shp = a[n].shape
        two_d = lambda t: t.reshape(shp[0] * shp[1], shp[2])
        d, m, v = _adamw(two_d(a[n]), two_d(g), two_d(a["m_" + n]), two_d(a["v_" + n]), "adamw_" + n)
        grads[n], delta[n], new_m[n], new_v[n] = g, d.reshape(shp), m.reshape(shp), v.reshape(shp)

    return (loss, gx.reshape(1, L, D_MODEL), *[grads[n] for n in WEIGHTS], *[delta[n] for n in WEIGHTS],
            *[new_m[n] for n in WEIGHTS], *[new_v[n] for n in WEIGHTS])
```

```python
import functools
import math

import numpy as np
import jax
import jax.numpy as jnp
from jax import lax
from jax.experimental import pallas as pl
from jax.experimental.pallas import tpu as pltpu

F32 = jnp.float32
_MXU = jnp.bfloat16
_ACT = jnp.bfloat16
_WIRE = jnp.bfloat16

D_MODEL = 1024
HEAD_DIM = 64
A_HEADS = 4
CHUNK = 128
B_Q_HEADS = 8
B_KV_HEADS = 2
B_GROUP = 4
WINDOW = 128
ROPE_THETA = 10000.0
C_WIDTH = 256
C_GROUP = 16
C_GROUPS = 16
C_STATE = 64
N_STATE = C_GROUPS * C_STATE
IN_A, IN_Q, IN_KV, IN_C = 512, 512, 128, 256
IN_COLS = 1536
D_FF = 4096
PLE_DIM = 256
EPS = 1e-6
NEG = -1e30
ADAM_LR, ADAM_B1, ADAM_B2, ADAM_EPS, ADAM_WD, ADAM_STEP = 0.001, 0.9, 0.999, 1e-08, 0.01, 10

LANES = 128
SUBLANES = 8
VMEM_BYTES = 64 * 2 ** 20
N_CHIPS = 4
MESH = pl.DeviceIdType.MESH


_MM_VMEM_BUDGET = 50 * 2 ** 20
_EPI_ROWS = 256


def _vmem_limit(est_bytes):
    return int(min(max(2 * est_bytes + (8 << 20), 32 << 20), VMEM_BYTES - (6 << 20)))


def _cparams(est_bytes, **kw):
    return pltpu.CompilerParams(vmem_limit_bytes=_vmem_limit(est_bytes), **kw)


def _sds(shape, dtype):
    return pltpu.HBM(tuple(shape), dtype)


def _hbm(x):
    return pltpu.with_memory_space_constraint(x, pltpu.HBM) if x.size >= (1 << 20) else x


def _nbytes(shape, dtype):
    return int(np.prod(shape)) * jnp.dtype(dtype).itemsize


def _tile(dim, pref):
    t = min(dim, pref)
    while dim % t:
        t -= LANES
    assert t > 0, (dim, pref)
    return t


def _lane(shape):
    return lax.broadcasted_iota(jnp.int32, shape, len(shape) - 1)


def _row(shape):
    return lax.broadcasted_iota(jnp.int32, shape, len(shape) - 2)


def _gelu(x):
    c = math.sqrt(2.0 / math.pi)
    return 0.5 * x * (1.0 + jnp.tanh(c * (x + 0.044715 * (x * x * x))))


def _gelu_grad(x):
    c = math.sqrt(2.0 / math.pi)
    t = jnp.tanh(c * (x + 0.044715 * (x * x * x)))
    return 0.5 * (1.0 + t) + 0.5 * x * (1.0 - t * t) * (c * (1.0 + 3.0 * 0.044715 * (x * x)))


def _sigmoid(x):
    return 1.0 / (1.0 + jnp.exp(-x))


def _dot(a, b, dims=(((1,), (0,)), ((), ()))):
    return lax.dot_general(a.astype(_MXU), b.astype(_MXU), dims, preferred_element_type=F32)


_NT = (((1,), (1,)), ((), ()))
_TN = (((0,), (0,)), ((), ()))
_NN = (((1,), (0,)), ((), ()))


def _mm(a, b, *, mode, M, N, K, out_dtypes, name, epi=None, extras=(), b_cb=False, o_cb=False,
        a_off=0, b_off=0, tm=1024, tn=1024, tk=2048, a_lyr=None, b_lyr=None, o_stack=None, n_acc=0, comm=None):
    if isinstance(a, tuple):
        a, a_lyr = a
    if isinstance(b, tuple):
        b, b_lyr = b
    if b_cb or o_cb:
        nc = (b.shape[-1] if b_cb else N // N_CHIPS)
    tn_nom = nc if ((mode == "nn" and b_cb) or (mode == "tn" and o_cb)) else _tile(N, tn)
    tk_nom = nc if (mode == "nt" and b_cb) else _tile(K, tk)
    item = lambda d: jnp.dtype(d).itemsize
    per_row = tk_nom * item(a.dtype) + tn_nom * (sum(item(d) for d in out_dtypes)
                                                   + sum(item(e.dtype) for e, _ in extras if e.shape[0] > 1))
    fixed = tk_nom * tn_nom * item(b.dtype)
    tm = _tile(M, tm)
    while tm > 256 and M % (tm // 2) == 0 and 2 * (tm * per_row + fixed) + 8 * tm * tn_nom > _MM_VMEM_BUDGET:
        tm //= 2

    def spec(block, imap, lyr=None):
        if lyr is None:
            return pl.BlockSpec(block, imap)
        return pl.BlockSpec((None,) + block, lambda i, j, k: (lyr,) + imap(i, j, k))

    if mode == "nn":
        if b_cb:
            tn = nc
        tm, tn, tk = _tile(M, tm), _tile(N, tn), _tile(K, tk)
        a_spec = spec((tm, tk), lambda i, j, k: (i, k + a_off), a_lyr)
        if b_cb:
            b_spec = spec((None, tk, tn), lambda i, j, k: (j, k, 0), b_lyr)
        else:
            b_spec = spec((tk, tn), lambda i, j, k: (k, j + b_off), b_lyr)
        dims = _NN
        a_blk, b_blk = (tm, tk), (tk, tn)
    elif mode == "nt":
        if b_cb:
            tk = nc
        tm, tn, tk = _tile(M, tm), _tile(N, tn), _tile(K, tk)
        a_spec = spec((tm, tk), lambda i, j, k: (i, k + a_off), a_lyr)
        if b_cb:
            b_spec = spec((None, tn, tk), lambda i, j, k: (k, j, 0), b_lyr)
        else:
            b_spec = spec((tn, tk), lambda i, j, k: (j, k + b_off), b_lyr)
        dims = _NT
        a_blk, b_blk = (tm, tk), (tn, tk)
    else:
        if o_cb:
            tn = nc
        tm, tn, tk = _tile(M, tm), _tile(N, tn), _tile(K, tk)
        a_spec = spec((tk, tm), lambda i, j, k: (k, i + a_off), a_lyr)
        b_spec = spec((tk, tn), lambda i, j, k: (k, j + b_off), b_lyr)
        dims = _TN
        a_blk, b_blk = (tk, tm), (tk, tn)
    gi, gj, gk = M // tm, N // tn, K // tk
    o_lyr = None if o_stack is None else o_stack[1]
    if o_cb:
        o_spec = spec((None, tm, tn), lambda i, j, k: (j, i, 0), o_lyr)
        o_shape = (gj, M, tn)
    else:
        o_spec = spec((tm, tn), lambda i, j, k: (i, j), o_lyr)
        o_shape = (M, N)
    e_specs = []
    for e, off in extras:
        if e.shape[0] == 1:
            e_specs.append(pl.BlockSpec((1, tn), lambda i, j, k, off=off: (0, j + off)))
        else:
            e_specs.append(pl.BlockSpec((tm, tn), lambda i, j, k, off=off: (i, j + off)))
    extras = [e for e, _ in extras]
    ne, no = len(extras), len(out_dtypes)
    operands = [_hbm(t) for t in (a, b, *extras)]
    in_specs = [a_spec, b_spec] + e_specs
    out_shape = [_sds(o_shape, d) for d in out_dtypes]
    aliases = {}
    if o_stack is not None:
        assert no == 1 and o_stack[0].shape[1:] == o_shape and o_stack[0].dtype == out_dtypes[0]
        operands.append(_hbm(o_stack[0]))
        in_specs.append(pl.BlockSpec(memory_space=pl.ANY))
        out_shape = [_sds(o_stack[0].shape, o_stack[0].dtype)]
        aliases = {len(operands) - 1: 0}
    out_specs = [o_spec] * no
    if n_acc:
        assert gj == 1 and o_stack is None
        out_specs[no - n_acc:] = [pl.BlockSpec((1, tn), lambda i, j, k: (0, 0))] * n_acc
        out_shape[no - n_acc:] = [_sds((1, N), d) for d in out_dtypes[no - n_acc:]]
    nx_in = nx_out = 0
    if comm is not None:
        nx_in, ncin0 = len(comm.ins), len(operands)
        operands += list(comm.ins)
        in_specs += [pl.BlockSpec(memory_space=pl.ANY)] * nx_in
        for t, x_ in enumerate(comm.ins):
            if comm.aliased[t]:
                aliases[ncin0 + t] = len(out_shape)
                out_shape.append(_sds(x_.shape, x_.dtype))
        out_shape += [_sds(sh, dt) for sh, dt in comm.fresh]
        nx_out = len(out_shape) - no
        out_specs += [pl.BlockSpec(memory_space=pl.ANY)] * nx_out
    nin = len(operands)

    def body(*refs):
        a_ref, b_ref = refs[0], refs[1]
        e_refs = refs[2:2 + ne]
        o_refs = refs[nin:nin + no]
        first_rows = pl.program_id(0) == 0
        if comm is not None:
            x_ins = refs[nin - nx_in:nin]
            x_outs = refs[nin + no:nin + no + nx_out]
            sems = refs[nin + no + nx_out:nin + no + nx_out + 2]
            pid = [pl.program_id(d) for d in range(3)]

            @pl.when((pid[0] == 0) & (pid[1] == 0) & (pid[2] == 0))
            def _():
                comm.start(x_ins, x_outs, *sems)

        def fin(acc):
            for t in range(no - n_acc, no):
                @pl.when(first_rows)
                def _():
                    o_refs[t][...] = jnp.zeros_like(o_refs[t])

            rc = _EPI_ROWS if (epi is not None and tm % _EPI_ROWS == 0) else tm
            for c0 in range(0, tm, rc):
                rows = slice(c0, c0 + rc)
                ex = [e[...] if e.shape[0] == 1 else e[rows, :] for e in e_refs]
                vals = epi(acc[rows, :], *ex) if epi is not None else (acc[rows, :],)
                for t, (o, v) in enumerate(zip(o_refs, vals)):
                    if t < no - n_acc:
                        o[rows, :] = v.astype(o.dtype)
                    else:
                        o[...] += v.astype(o.dtype)

        prod = _dot(a_ref[...], b_ref[...], dims)
        if gk == 1:
            fin(prod)
        else:
            acc_ref = refs[-1]
            k = pl.program_id(2)

            @pl.when(k == 0)
            def _():
                acc_ref[...] = prod

            @pl.when(k > 0)
            def _():
                acc_ref[...] += prod

            @pl.when(k == gk - 1)
            def _():
                fin(acc_ref)

        if comm is not None:
            @pl.when((pid[0] == gi - 1) & (pid[1] == gj - 1) & (pid[2] == gk - 1))
            def _():
                comm.wait(x_ins, x_outs, *sems)

    est = (_nbytes(a_blk, a.dtype) + _nbytes(b_blk, b.dtype)
           + sum(_nbytes((tm, tn), d) for d in out_dtypes)
           + sum(_nbytes((tm, tn), e.dtype) for e in extras)) + 2 * _nbytes((tm, tn), F32)
    sem_scratch = [pltpu.SemaphoreType.DMA((comm.n_sems,))] * 2 if comm is not None else []
    row_sem = "arbitrary" if (n_acc or comm is not None) else "parallel"
    outs = pl.pallas_call(
        body, name=name, grid=(gi, gj, gk),
        in_specs=in_specs,
        out_specs=out_specs,
        out_shape=out_shape,
        scratch_shapes=sem_scratch + ([pltpu.VMEM((tm, tn), F32)] if gk > 1 else []),
        input_output_aliases=aliases,
        compiler_params=_cparams(est, dimension_semantics=(row_sem, "arbitrary" if comm is not None else "parallel",
                                                           "arbitrary")),
    )(*operands)
    if comm is not None:
        main = outs[:no]
        return (main if no > 1 else main[0]), list(outs[no:])
    return outs if no > 1 else outs[0]


_TL = 512


def _rms_fwd(h, g, name):
    L, D = h.shape
    tl = _tile(L, _TL)

    def body(h_ref, g_ref, o_ref):
        x = h_ref[...]
        r = lax.rsqrt(jnp.mean(x * x, axis=-1, keepdims=True) + EPS)
        o_ref[...] = ((x * r) * g_ref[...]).astype(o_ref.dtype)

    return pl.pallas_call(
        body, name=name, grid=(L // tl,),
        in_specs=[pl.BlockSpec((tl, D), lambda i: (i, 0)), pl.BlockSpec((1, D), lambda i: (0, 0))],
        out_specs=pl.BlockSpec((tl, D), lambda i: (i, 0)),
        out_shape=_sds((L, D), _ACT),
        compiler_params=_cparams(3 * _nbytes((tl, D), F32)),
    )(h, g.reshape(1, D))


def _rms_bwd(dxn, h, g, dres, name):
    L, D = h.shape
    tl = _tile(L, _TL)

    def body(d_ref, h_ref, g_ref, r_ref, o_ref, dg_ref):
        x = h_ref[...]
        r = lax.rsqrt(jnp.mean(x * x, axis=-1, keepdims=True) + EPS)
        xhat = x * r
        d = d_ref[...].astype(F32)
        gy = d * g_ref[...]
        dx = r * (gy - xhat * jnp.mean(gy * xhat, axis=-1, keepdims=True))
        o_ref[...] = r_ref[...] + dx

        @pl.when(pl.program_id(0) == 0)
        def _():
            dg_ref[...] = jnp.zeros_like(dg_ref)

        dg_ref[...] += jnp.sum(d * xhat, axis=0, keepdims=True)

    dh, dg = pl.pallas_call(
        body, name=name, grid=(L // tl,),
        in_specs=[pl.BlockSpec((tl, D), lambda i: (i, 0)), pl.BlockSpec((tl, D), lambda i: (i, 0)),
                  pl.BlockSpec((1, D), lambda i: (0, 0)), pl.BlockSpec((tl, D), lambda i: (i, 0))],
        out_specs=[pl.BlockSpec((tl, D), lambda i: (i, 0)), pl.BlockSpec((1, D), lambda i: (0, 0))],
        out_shape=[_sds((L, D), F32), _sds((1, D), F32)],
        compiler_params=_cparams(5 * _nbytes((tl, D), F32)),
    )(dxn, h, g.reshape(1, D), dres)
    return dh, dg.reshape(D)


def _rope_tables(positions):
    L = positions.shape[0]
    tl = _tile(L, 1024)
    inv = 1.0 / (ROPE_THETA ** (np.arange(0, HEAD_DIM, 2, dtype=np.float32) / HEAD_DIM))
    inv128 = jnp.asarray(np.tile(inv.astype(np.float32), 4).reshape(1, LANES))

    def body(p_ref, i_ref, c_ref, s_ref):
        ang = p_ref[...].astype(F32) * i_ref[...]
        c_ref[...] = jnp.cos(ang)
        s_ref[...] = jnp.sin(ang)

    return pl.pallas_call(
        body, name="rope_tables", grid=(L // tl,),
        in_specs=[pl.BlockSpec((tl, 1), lambda i: (i, 0)), pl.BlockSpec((1, LANES), lambda i: (0, 0))],
        out_specs=[pl.BlockSpec((tl, LANES), lambda i: (i, 0))] * 2,
        out_shape=[_sds((L, LANES), F32)] * 2,
    )(positions.reshape(L, 1), inv128)


_GM_TL = 256


def _gmlp_head(Z, W, bfull, lg, lb, maskv):
    G = _gelu(Z)
    mu = jnp.sum(jnp.where(maskv, G, 0.0), axis=-1, keepdims=True) * (1.0 / HEAD_DIM)
    xc = jnp.where(maskv, G - mu, 0.0)
    var = jnp.sum(xc * xc, axis=-1, keepdims=True) * (1.0 / HEAD_DIM)
    rstd = lax.rsqrt(var + EPS)
    xhat = xc * rstd
    vn = xhat * lg + lb
    sv = _dot(W, vn) + bfull
    return G, xhat, rstd, vn, sv


def _tril(W):
    return jnp.where(_row(W.shape) >= _lane(W.shape), W, 0.0)


def _triu(W):
    return jnp.where(_row(W.shape) <= _lane(W.shape), W, 0.0)


def _gmlp_fwd(z, ws, bfull, lgf, lbf, name):
    L = z.shape[0]
    tl = _tile(L, _GM_TL)
    nch = tl // CHUNK

    def body(z_ref, w_ref, b_ref, lg_ref, lb_ref, o_ref):
        maskv = _lane((CHUNK, LANES)) >= HEAD_DIM
        for c in range(nch):
            rows = slice(c * CHUNK, (c + 1) * CHUNK)
            for hp in range(A_HEADS // 2):
                acc = None
                for hh in range(2):
                    h = 2 * hp + hh
                    Z = z_ref[rows, h * LANES:(h + 1) * LANES]
                    G, _, _, _, sv = _gmlp_head(Z, _tril(w_ref[h]), b_ref[h], lg_ref[h:h + 1, :], lb_ref[h:h + 1, :], maskv)
                    prod = G * pltpu.roll(sv, HEAD_DIM, axis=1)
                    acc = prod if hh == 0 else acc + pltpu.roll(prod, HEAD_DIM, axis=1)
                o_ref[rows, hp * LANES:(hp + 1) * LANES] = acc

    return pl.pallas_call(
        body, name=name, grid=(L // tl,),
        in_specs=[pl.BlockSpec((tl, IN_A), lambda i: (i, 0)),
                  pl.BlockSpec((A_HEADS, CHUNK, CHUNK), lambda i: (0, 0, 0)),
                  pl.BlockSpec((A_HEADS, CHUNK, LANES), lambda i: (0, 0, 0)),
                  pl.BlockSpec((A_HEADS, LANES), lambda i: (0, 0)),
                  pl.BlockSpec((A_HEADS, LANES), lambda i: (0, 0))],
        out_specs=pl.BlockSpec((tl, 2 * LANES), lambda i: (i, 0)),
        out_shape=_sds((L, 2 * LANES), F32),
    )(z, ws, bfull, lgf, lbf)


def _gmlp_bwd(z, dya, ws, wsT, bfull, lgf, lbf, name):
    L = z.shape[0]
    tl = _tile(L, _GM_TL)
    nch = tl // CHUNK
    nsteps = L // tl

    def body(z_ref, d_ref, w_ref, wt_ref, b_ref, lg_ref, lb_ref, dz_ref, dw_ref, db_ref, dlg_ref, dlb_ref):
        step = pl.program_id(0)

        @pl.when(step == 0)
        def _():
            dw_ref[...] = jnp.zeros_like(dw_ref)
            db_ref[...] = jnp.zeros_like(db_ref)
            dlg_ref[...] = jnp.zeros_like(dlg_ref)
            dlb_ref[...] = jnp.zeros_like(dlb_ref)

        lane = _lane((CHUNK, LANES))
        maskv = lane >= HEAD_DIM
        for c in range(nch):
            rows = slice(c * CHUNK, (c + 1) * CHUNK)
            for h in range(A_HEADS):
                hp, hh = divmod(h, 2)
                Z = z_ref[rows, h * LANES:(h + 1) * LANES]
                lg = lg_ref[h:h + 1, :]
                G, xhat, rstd, vn, sv = _gmlp_head(Z, _tril(w_ref[h]), b_ref[h], lg, lb_ref[h:h + 1, :], maskv)
                dpair = d_ref[rows, hp * LANES:(hp + 1) * LANES]
                if hh == 1:
                    dpair = pltpu.roll(dpair, HEAD_DIM, axis=1)
                dout = jnp.where(maskv, 0.0, dpair)
                du = dout * pltpu.roll(sv, HEAD_DIM, axis=1)
                dsv = pltpu.roll(dout * G, HEAD_DIM, axis=1)
                dw_ref[h] += _tril(_dot(dsv, vn, _NT))
                db_ref[h] += dsv
                dvn = _dot(_triu(wt_ref[h]), dsv)
                dlg_ref[h] += dvn * xhat
                dlb_ref[h] += dvn
                dxh = dvn * lg
                m1 = jnp.sum(dxh, axis=-1, keepdims=True) * (1.0 / HEAD_DIM)
                m2 = jnp.sum(dxh * xhat, axis=-1, keepdims=True) * (1.0 / HEAD_DIM)
                dv = jnp.where(maskv, rstd * (dxh - m1 - xhat * m2), 0.0)
                dz_ref[rows, h * LANES:(h + 1) * LANES] = ((du + dv) * _gelu_grad(Z)).astype(dz_ref.dtype)

        @pl.when(step == nsteps - 1)
        def _():
            for h in range(A_HEADS):
                db_ref[h] = jnp.broadcast_to(jnp.sum(db_ref[h], axis=1, keepdims=True), (CHUNK, LANES))
                dlg_ref[h] = jnp.broadcast_to(jnp.sum(dlg_ref[h], axis=0, keepdims=True), (CHUNK, LANES))
                dlb_ref[h] = jnp.broadcast_to(jnp.sum(dlb_ref[h], axis=0, keepdims=True), (CHUNK, LANES))

    full3 = pl.BlockSpec((A_HEADS, CHUNK, LANES), lambda i: (0, 0, 0))
    return pl.pallas_call(
        body, name=name, grid=(nsteps,),
        in_specs=[pl.BlockSpec((tl, IN_A), lambda i: (i, 0)),
                  pl.BlockSpec((tl, 2 * LANES), lambda i: (i, 0)),
                  full3, full3, full3,
                  pl.BlockSpec((A_HEADS, LANES), lambda i: (0, 0)),
                  pl.BlockSpec((A_HEADS, LANES), lambda i: (0, 0))],
        out_specs=[pl.BlockSpec((tl, IN_A), lambda i: (i, 0)), full3, full3, full3, full3],
        out_shape=[_sds((L, IN_A), _ACT)] + [_sds((A_HEADS, CHUNK, LANES), F32)] * 4,
    )(z, dya, ws, wsT, bfull, lgf, lbf)


def _head_rstd(x, lo):
    sq = x * x
    s_lo = jnp.sum(jnp.where(lo, sq, 0.0), axis=-1, keepdims=True)
    s_hi = jnp.sum(jnp.where(lo, 0.0, sq), axis=-1, keepdims=True)
    return jnp.where(lo, lax.rsqrt(s_lo * (1.0 / HEAD_DIM) + EPS), lax.rsqrt(s_hi * (1.0 / HEAD_DIM) + EPS))


def _rot_half(x, first):
    return jnp.where(first, -pltpu.roll(x, LANES - HEAD_DIM // 2, axis=1), pltpu.roll(x, HEAD_DIM // 2, axis=1))


def _qk_prep(z, cos, sin, gq, gk, name):
    L = z.shape[0]
    tl = _tile(L, _TL)
    nq = IN_Q // LANES

    def body(q_ref, k_ref, c_ref, s_ref, gq_ref, gk_ref, qo_ref, ko_ref):
        lane = _lane((tl, LANES))
        lo = lane < HEAD_DIM
        first = (lane % HEAD_DIM) < (HEAD_DIM // 2)
        c, s = c_ref[...], s_ref[...]

        def prep(x, g):
            xn = (x * _head_rstd(x, lo)) * g
            return xn * c + _rot_half(xn, first) * s

        for j in range(nq):
            qo_ref[:, j * LANES:(j + 1) * LANES] = prep(q_ref[:, j * LANES:(j + 1) * LANES], gq_ref[...]).astype(qo_ref.dtype)
        ko_ref[...] = prep(k_ref[...], gk_ref[...]).astype(ko_ref.dtype)

    return pl.pallas_call(
        body, name=name, grid=(L // tl,),
        in_specs=[pl.BlockSpec((tl, IN_Q), lambda i: (i, 1)),
                  pl.BlockSpec((tl, IN_KV), lambda i: (i, 8)),
                  pl.BlockSpec((tl, LANES), lambda i: (i, 0)), pl.BlockSpec((tl, LANES), lambda i: (i, 0)),
                  pl.BlockSpec((1, LANES), lambda i: (0, 0)), pl.BlockSpec((1, LANES), lambda i: (0, 0))],
        out_specs=[pl.BlockSpec((tl, IN_Q), lambda i: (i, 0)), pl.BlockSpec((tl, IN_KV), lambda i: (i, 0))],
        out_shape=[_sds((L, IN_Q), _ACT), _sds((L, IN_KV), _ACT)],
    )(z, z, cos, sin, gq, gk)


def _qk_prep_bwd(z, dq, dkc, dkp, dvc, dvp, cos, sin, gq, gk, name):
    L = z.shape[0]
    tl = _ATT_QB * WINDOW
    nb = L // tl
    nq = IN_Q // LANES

    def body(q_ref, k_ref, dq_ref, dkc_ref, dkp_ref, dvc_ref, dvp_ref, c_ref, s_ref, gq_ref, gk_ref,
             dzq_ref, dzk_ref, dzv_ref, dgq_ref, dgk_ref):
        n = pl.program_id(0)

        @pl.when(n == 0)
        def _():
            dgq_ref[...] = jnp.zeros_like(dgq_ref)
            dgk_ref[...] = jnp.zeros_like(dgk_ref)

        lane = _lane((tl, LANES))
        lo = lane < HEAD_DIM
        first = (lane % HEAD_DIM) < (HEAD_DIM // 2)
        c, s = c_ref[...], s_ref[...]
        has_next = jnp.where(n < nb - 1, 1.0, 0.0)

        def bwd(x, g, dy):
            r = _head_rstd(x, lo)
            xhat = x * r
            dxn = dy * c - _rot_half(dy * s, first)
            gy = dxn * g
            t = gy * xhat
            m_lo = jnp.sum(jnp.where(lo, t, 0.0), axis=-1, keepdims=True)
            m_hi = jnp.sum(jnp.where(lo, 0.0, t), axis=-1, keepdims=True)
            m = jnp.where(lo, m_lo, m_hi) * (1.0 / HEAD_DIM)
            dx = r * (gy - xhat * m)
            dg = jnp.sum(dxn * xhat, axis=0, keepdims=True)
            return dx, dg

        dgq = jnp.zeros((1, LANES), F32)
        for j in range(nq):
            sl = slice(j * LANES, (j + 1) * LANES)
            dx, dg = bwd(q_ref[:, sl], gq_ref[...], dq_ref[:, sl].astype(F32))
            dzq_ref[:, sl] = dx.astype(dzq_ref.dtype)
            dgq = dgq + dg
        dgq_ref[...] += dgq + pltpu.roll(dgq, HEAD_DIM, axis=1)
        def with_next(cur_ref, nxt_ref):
            head = jnp.zeros((tl - WINDOW, IN_KV), F32)
            return cur_ref[...] + jnp.concatenate([head, has_next * nxt_ref[...]], axis=0)

        dx, dg = bwd(k_ref[...], gk_ref[...], with_next(dkc_ref, dkp_ref))
        dzk_ref[...] = dx.astype(dzk_ref.dtype)
        dgk_ref[...] += dg + pltpu.roll(dg, HEAD_DIM, axis=1)
        dzv_ref[...] = with_next(dvc_ref, dvp_ref).astype(dzv_ref.dtype)

    nxt = lambda i: (jnp.minimum(i + 1, nb - 1), 0)
    cur = lambda i: (i, 0)
    kv = pl.BlockSpec((tl, IN_KV), cur)
    kvn = pl.BlockSpec((WINDOW, IN_KV), nxt)
    one = pl.BlockSpec((1, LANES), lambda i: (0, 0))
    return pl.pallas_call(
        body, name=name, grid=(nb,),
        in_specs=[pl.BlockSpec((tl, IN_Q), lambda i: (i, 1)), pl.BlockSpec((tl, IN_KV), lambda i: (i, 8)),
                  pl.BlockSpec((tl, IN_Q), cur), kv, kvn, kv, kvn,
                  kv, kv, one, one],
        out_specs=[pl.BlockSpec((tl, IN_Q), cur), kv, kv, one, one],
        out_shape=[_sds((L, IN_Q), _ACT), _sds((L, IN_KV), _ACT),
                   _sds((L, IN_KV), _ACT), _sds((1, LANES), F32),
                   _sds((1, LANES), F32)],
    )(z, z, dq, dkc, dkp, dvc, dvp, cos, sin, gq, gk)


def _attn_mask(n):
    shp = (2 * WINDOW, B_GROUP * WINDOW)
    qi = _lane(shp) % WINDOW
    kj = _row(shp)
    off = 0 if n is None else jnp.where(n > 0, 0, 4 * WINDOW)
    return ((kj >= WINDOW) & (kj - WINDOW <= qi)) | ((kj < WINDOW) & (kj > qi + off))


def _kv_lanes(j):
    lane = _lane((WINDOW, LANES))
    return (lane >= j * HEAD_DIM) & (lane < (j + 1) * HEAD_DIM)


_ATT_QB = 4


def _stack_heads(ref, rows, j, kvl):
    parts = []
    for g in range(B_GROUP):
        h = j * B_GROUP + g
        slab = ref[rows, (h // 2) * LANES:(h // 2 + 1) * LANES].astype(F32)
        if (h % 2) != j:
            slab = pltpu.roll(slab, HEAD_DIM, axis=1)
        parts.append(jnp.where(kvl, slab, 0.0))
    return jnp.concatenate(parts, axis=0)


def _attn_probs(qs, k2, sink_row, mask):
    s = _dot(k2, qs, _NT) * (HEAD_DIM ** -0.5)
    s = jnp.where(mask, s, NEG)
    m = jnp.maximum(jnp.max(s, axis=0, keepdims=True), sink_row)
    p = jnp.exp(s - m)
    esink = jnp.exp(sink_row - m)
    inv = 1.0 / (jnp.sum(p, axis=0, keepdims=True) + esink)
    return p * inv, esink * inv


def _sink_row(sink_ref, j):
    lane = _lane((1, B_GROUP * WINDOW))
    row = jnp.full((1, B_GROUP * WINDOW), sink_ref[j * B_GROUP], F32)
    for g in range(1, B_GROUP):
        row = jnp.where(lane >= g * WINDOW, sink_ref[j * B_GROUP + g], row)
    return row


def _attn_fwd(q, k, z, sinks, name):
    L = q.shape[0]
    QB = _ATT_QB
    tq = QB * WINDOW
    prev = lambda n: (jnp.maximum(QB * n - 1, 0), 0)
    prev_v = lambda n: (jnp.maximum(QB * n - 1, 0), 9)

    def body(s_ref, q_ref, kp_ref, kc_ref, vp_ref, vc_ref, o_ref):
        n = pl.program_id(0)
        k3 = jnp.concatenate([kp_ref[...], kc_ref[...]], axis=0)
        v3 = jnp.concatenate([vp_ref[...], vc_ref[...]], axis=0)
        for b in range(QB):
            rows = slice(b * WINDOW, (b + 1) * WINDOW)
            mask = _attn_mask(n if b == 0 else None)
            k2 = k3[b * WINDOW:(b + 2) * WINDOW]
            v2 = v3[b * WINDOW:(b + 2) * WINDOW]
            slabs = [None] * (IN_Q // LANES)
            for j in range(B_KV_HEADS):
                kvl = _kv_lanes(j)
                qs = _stack_heads(q_ref, rows, j, kvl)
                pn, _ = _attn_probs(qs, k2, _sink_row(s_ref, j), mask)
                o = _dot(pn, v2, _TN)
                for g in range(B_GROUP):
                    h = j * B_GROUP + g
                    piece = jnp.where(kvl, o[g * WINDOW:(g + 1) * WINDOW], 0.0)
                    if (h % 2) != j:
                        piece = pltpu.roll(piece, HEAD_DIM, axis=1)
                    slabs[h // 2] = piece if slabs[h // 2] is None else slabs[h // 2] + piece
            for t, sl in enumerate(slabs):
                o_ref[rows, t * LANES:(t + 1) * LANES] = sl

    return pl.pallas_call(
        body, name=name, grid=(L // tq,),
        in_specs=[pl.BlockSpec(memory_space=pltpu.SMEM),
                  pl.BlockSpec((tq, IN_Q), lambda n: (n, 0)),
                  pl.BlockSpec((WINDOW, IN_KV), prev), pl.BlockSpec((tq, IN_KV), lambda n: (n, 0)),
                  pl.BlockSpec((WINDOW, IN_KV), prev_v), pl.BlockSpec((tq, IN_KV), lambda n: (n, 9))],
        out_specs=pl.BlockSpec((tq, IN_Q), lambda n: (n, 0)),
        out_shape=_sds((L, IN_Q), F32),
    )(sinks, q, k, k, z, z)


def _attn_bwd(q, k, z, sinks, dyb, name):
    L = q.shape[0]
    QB = _ATT_QB
    tq = QB * WINDOW
    nsteps = L // tq
    prev = lambda n: (jnp.maximum(QB * n - 1, 0), 0)
    prev_v = lambda n: (jnp.maximum(QB * n - 1, 0), 9)
    cur = lambda n: (n, 0)

    def body(s_ref, q_ref, kp_ref, kc_ref, vp_ref, vc_ref, d_ref, dq_ref, dkc_ref, dkp_ref, dvc_ref, dvp_ref, ds_ref):
        n = pl.program_id(0)

        @pl.when(n == 0)
        def _():
            ds_ref[...] = jnp.zeros_like(ds_ref)

        k3 = jnp.concatenate([kp_ref[...], kc_ref[...]], axis=0)
        v3 = jnp.concatenate([vp_ref[...], vc_ref[...]], axis=0)
        dkb = [None] * (QB + 1)
        dvb = [None] * (QB + 1)
        dsink = jnp.zeros((1, LANES), F32)
        lane1 = _lane((1, LANES))
        add = lambda acc, v: v if acc is None else acc + v
        for b in range(QB):
            rows = slice(b * WINDOW, (b + 1) * WINDOW)
            mask = _attn_mask(n if b == 0 else None)
            k2 = k3[b * WINDOW:(b + 2) * WINDOW]
            v2 = v3[b * WINDOW:(b + 2) * WINDOW]
            slabs = [None] * (IN_Q // LANES)
            for j in range(B_KV_HEADS):
                kvl = _kv_lanes(j)
                qs = _stack_heads(q_ref, rows, j, kvl)
                dos = _stack_heads(d_ref, rows, j, kvl)
                pn, psink = _attn_probs(qs, k2, _sink_row(s_ref, j), mask)
                dp = _dot(v2, dos, _NT)
                dd = jnp.sum(pn * dp, axis=0, keepdims=True)
                dss = (pn * (dp - dd)) * (HEAD_DIM ** -0.5)
                dqs = _dot(dss, k2, _TN)
                dk2 = _dot(dss, qs)
                dv2 = _dot(pn, dos)
                dkb[b], dkb[b + 1] = add(dkb[b], dk2[:WINDOW]), add(dkb[b + 1], dk2[WINDOW:])
                dvb[b], dvb[b + 1] = add(dvb[b], dv2[:WINDOW]), add(dvb[b + 1], dv2[WINDOW:])
                sd = psink * dd
                for g in range(B_GROUP):
                    h = j * B_GROUP + g
                    piece = jnp.where(kvl, dqs[g * WINDOW:(g + 1) * WINDOW], 0.0)
                    if (h % 2) != j:
                        piece = pltpu.roll(piece, HEAD_DIM, axis=1)
                    slabs[h // 2] = piece if slabs[h // 2] is None else slabs[h // 2] + piece
                    tot = jnp.sum(sd[:, g * WINDOW:(g + 1) * WINDOW], axis=1, keepdims=True)
                    dsink = dsink - jnp.where(lane1 == h, tot, 0.0)
            for t, sl in enumerate(slabs):
                dq_ref[rows, t * LANES:(t + 1) * LANES] = sl
        dkp_ref[...] = dkb[0]
        dvp_ref[...] = dvb[0]
        for b in range(QB):
            dkc_ref[b * WINDOW:(b + 1) * WINDOW, :] = dkb[b + 1]
            dvc_ref[b * WINDOW:(b + 1) * WINDOW, :] = dvb[b + 1]
        ds_ref[0:1, :] += dsink

    kvs = pl.BlockSpec((tq, IN_KV), cur)
    kvp = pl.BlockSpec((WINDOW, IN_KV), cur)
    kvo = _sds((L, IN_KV), F32)
    kvpo = _sds((nsteps * WINDOW, IN_KV), F32)
    return pl.pallas_call(
        body, name=name, grid=(nsteps,),
        in_specs=[pl.BlockSpec(memory_space=pltpu.SMEM),
                  pl.BlockSpec((tq, IN_Q), cur),
                  pl.BlockSpec((WINDOW, IN_KV), prev), kvs,
                  pl.BlockSpec((WINDOW, IN_KV), prev_v), pl.BlockSpec((tq, IN_KV), lambda n: (n, 9)),
                  pl.BlockSpec((tq, IN_Q), cur)],
        out_specs=[pl.BlockSpec((tq, IN_Q), cur), kvs, kvp, kvs, kvp, pl.BlockSpec((SUBLANES, LANES), lambda n: (0, 0))],
        out_shape=[_sds((L, IN_Q), F32), kvo, kvpo, kvo, kvpo, _sds((SUBLANES, LANES), F32)],
    )(sinks, q, k, k, z, z, dyb)


def _ssm_disc(are, aim, ldt, bre, bim):
    dt = jnp.exp(ldt)
    mag = jnp.exp(are * dt)
    lr, li = mag * jnp.cos(aim * dt), mag * jnp.sin(aim * dt)
    den = are * are + aim * aim
    xr, xi = lr - 1.0, li
    cr, ci = (xr * are + xi * aim) / den, (xi * are - xr * aim) / den
    return lr, li, cr * bre - ci * bim, cr * bim + ci * bre


def _ssm_prep(are, aim, ldt, bre, bim):
    shp3, shpb = are.shape, bre.shape

    def body(are_ref, aim_ref, ldt_ref, bre_ref, bim_ref, lr_ref, li_ref, br_ref, bi_ref):
        lr, li, br, bi = _ssm_disc(are_ref[...], aim_ref[...], ldt_ref[...], bre_ref[...], bim_ref[...])
        lr_ref[...] = lr
        li_ref[...] = li
        br_ref[...] = br
        bi_ref[...] = bi

    return pl.pallas_call(
        body, name="ssm_prep",
        out_shape=[_sds(shp3, F32)] * 2 + [_sds(shpb, F32)] * 2,
    )(are, aim, ldt, bre, bim)


def _ssm_prep_bwd(are, aim, ldt, bre, bim, dlr, dli, dbr, dbi):
    shp3, shpb = are.shape, bre.shape

    def body(are_ref, aim_ref, ldt_ref, bre_ref, bim_ref, dlr_ref, dli_ref, dbr_ref, dbi_ref,
             o_are, o_aim, o_ldt, o_bre, o_bim):
        _, vjp = jax.vjp(_ssm_disc, are_ref[...], aim_ref[...], ldt_ref[...], bre_ref[...], bim_ref[...])
        g = vjp((dlr_ref[...], dli_ref[...], dbr_ref[...], dbi_ref[...]))
        o_are[...] = g[0]
        o_aim[...] = g[1]
        o_ldt[...] = jnp.broadcast_to(jnp.sum(g[2], axis=-1, keepdims=True), shp3)
        o_bre[...] = g[3]
        o_bim[...] = g[4]

    return pl.pallas_call(
        body, name="ssm_prep_bwd",
        out_shape=[_sds(shp3, F32)] * 3 + [_sds(shpb, F32)] * 2,
    )(are, aim, ldt, bre, bim, dlr, dli, dbr, dbi)


_SCAN_TB = 512
_SCAN_W = 512


def _cmul(ar, ai, br, bi):
    return ar * br - ai * bi, ar * bi + ai * br


def _ssm_scan(x, lam_r, lam_i, name, reverse=False, states=None):
    L = x.shape[0]
    tb = _tile(L, _SCAN_TB)
    nrb = L // tb
    nt = tb // SUBLANES
    W = _SCAN_W
    with_da = states is not None

    def body(*refs):
        if with_da:
            xr_ref, xi_ref, sr_ref, si_ref, ar_ref, ai_ref, o_ref, dar_ref, dai_ref, cr_ref, ci_ref = refs
        else:
            xr_ref, xi_ref, ar_ref, ai_ref, o_ref, cr_ref, ci_ref = refs
        step = pl.program_id(0)

        @pl.when(step == 0)
        def _():
            cr_ref[...] = jnp.zeros_like(cr_ref)
            ci_ref[...] = jnp.zeros_like(ci_ref)
            if with_da:
                dar_ref[...] = jnp.zeros_like(dar_ref)
                dai_ref[...] = jnp.zeros_like(dai_ref)

        row = _row((SUBLANES, W))

        def shift(v, d, fill):
            if reverse:
                return jnp.where(row < SUBLANES - d, pltpu.roll(v, SUBLANES - d, axis=0), fill)
            return jnp.where(row >= d, pltpu.roll(v, d, axis=0), fill)

        edge = 0 if reverse else SUBLANES - 1
        for wb in range(N_STATE // W):
            cols = slice(wb * W, (wb + 1) * W)
            a1r = jnp.broadcast_to(ar_ref[:, cols], (SUBLANES, W))
            a1i = jnp.broadcast_to(ai_ref[:, cols], (SUBLANES, W))
            if reverse:
                a1i = -a1i
            a2r, a2i = _cmul(a1r, a1i, a1r, a1i)
            a4r, a4i = _cmul(a2r, a2i, a2r, a2i)
            pws = ((1, a1r, a1i), (2, a2r, a2i), (4, a4r, a4i))
            pr, pi = a1r, a1i
            for d, _, _ in pws:
                qr, qi = _cmul(pr, pi, shift(pr, d, 1.0), shift(pi, d, 0.0))
                pr, pi = qr, qi
            mws = []
            for d, er, ei in pws:
                ok = (row < SUBLANES - d) if reverse else (row >= d)
                mws.append(((SUBLANES - d) if reverse else d, jnp.where(ok, er, 0.0), jnp.where(ok, ei, 0.0)))

            def tile(i, carry):
                cr, ci, dr, di = carry
                t = (nt - 1 - i) if reverse else i
                r0 = pl.multiple_of(t * SUBLANES, SUBLANES)
                vr = xr_ref[pl.ds(r0, SUBLANES), cols]
                vi = xi_ref[pl.ds(r0, SUBLANES), cols]
                for sh, er, ei in mws:
                    tr, ti = _cmul(er, ei, pltpu.roll(vr, sh, axis=0), pltpu.roll(vi, sh, axis=0))
                    vr, vi = vr + tr, vi + ti
                tr, ti = _cmul(pr, pi, cr, ci)
                vr, vi = vr + tr, vi + ti
                o_ref[pl.ds(r0, SUBLANES), cols] = vr
                o_ref[pl.ds(r0, SUBLANES), slice(N_STATE + wb * W, N_STATE + (wb + 1) * W)] = vi
                if with_da:
                    gr = jnp.where(row < SUBLANES - 1, pltpu.roll(vr, SUBLANES - 1, axis=0), cr)
                    gi = jnp.where(row < SUBLANES - 1, pltpu.roll(vi, SUBLANES - 1, axis=0), ci)
                    sr = sr_ref[pl.ds(r0, SUBLANES), cols]
                    si = si_ref[pl.ds(r0, SUBLANES), cols]
                    dr = dr + sr * gr + si * gi
                    di = di + sr * gi - si * gr
                ncr = jnp.broadcast_to(vr[edge:edge + 1, :], (SUBLANES, W))
                nci = jnp.broadcast_to(vi[edge:edge + 1, :], (SUBLANES, W))
                return ncr, nci, dr, di

            zero = jnp.zeros((SUBLANES, W), F32)
            cr, ci, dr, di = lax.fori_loop(0, nt, tile, (cr_ref[:, cols], ci_ref[:, cols], zero, zero), unroll=2)
            cr_ref[:, cols] = cr
            ci_ref[:, cols] = ci
            if with_da:
                dar_ref[:, cols] += dr
                dai_ref[:, cols] += di

        if with_da:
            @pl.when(step == nrb - 1)
            def _():
                dar_ref[...] = jnp.broadcast_to(jnp.sum(dar_ref[...], axis=0, keepdims=True), dar_ref.shape)
                dai_ref[...] = jnp.broadcast_to(jnp.sum(dai_ref[...], axis=0, keepdims=True), dai_ref.shape)

    rb = (lambda i: (nrb - 1 - i, 0)) if reverse else (lambda i: (i, 0))
    rb_im = (lambda i: (nrb - 1 - i, 1)) if reverse else (lambda i: (i, 1))
    blk_r = pl.BlockSpec((tb, N_STATE), rb)
    blk_i = pl.BlockSpec((tb, N_STATE), rb_im)
    one = pl.BlockSpec((1, N_STATE), lambda i: (0, 0))
    acc = pl.BlockSpec((SUBLANES, N_STATE), lambda i: (0, 0))
    ins = [x, x] + ([states, states] if with_da else []) + [lam_r, lam_i]
    in_specs = [blk_r, blk_i] + ([blk_r, blk_i] if with_da else []) + [one, one]
    out_specs = [pl.BlockSpec((tb, 2 * N_STATE), rb)] + ([acc, acc] if with_da else [])
    out_shape = [_sds((L, 2 * N_STATE), F32)] + (
        [_sds((SUBLANES, N_STATE), F32)] * 2 if with_da else [])
    outs = pl.pallas_call(
        body, name=name, grid=(nrb,), in_specs=in_specs, out_specs=out_specs, out_shape=out_shape,
        scratch_shapes=[pltpu.VMEM((SUBLANES, N_STATE), F32)] * 2,
        compiler_params=_cparams((6 if with_da else 4) * _nbytes((tb, N_STATE), F32),
                                 dimension_semantics=("arbitrary",)),
    )(*ins)
    return outs if with_da else outs[0]


def _scan_block(x_ref, o_ref, s_ref, ar_ref, ai_ref, cr_ref, ci_ref, dar_ref, dai_ref, nt, reverse):
    W = _SCAN_W
    with_da = s_ref is not None
    row = _row((SUBLANES, W))

    def shift(v, d, fill):
        if reverse:
            return jnp.where(row < SUBLANES - d, pltpu.roll(v, SUBLANES - d, axis=0), fill)
        return jnp.where(row >= d, pltpu.roll(v, d, axis=0), fill)

    edge = 0 if reverse else SUBLANES - 1
    for wb in range(N_STATE // W):
        cols = slice(wb * W, (wb + 1) * W)
        icols = slice(N_STATE + wb * W, N_STATE + (wb + 1) * W)
        a1r = jnp.broadcast_to(ar_ref[:, cols], (SUBLANES, W))
        a1i = jnp.broadcast_to(ai_ref[:, cols], (SUBLANES, W))
        if reverse:
            a1i = -a1i
        a2r, a2i = _cmul(a1r, a1i, a1r, a1i)
        a4r, a4i = _cmul(a2r, a2i, a2r, a2i)
        pws = ((1, a1r, a1i), (2, a2r, a2i), (4, a4r, a4i))
        pr, pi = a1r, a1i
        for d, _, _ in pws:
            qr, qi = _cmul(pr, pi, shift(pr, d, 1.0), shift(pi, d, 0.0))
            pr, pi = qr, qi
        mws = []
        for d, er, ei in pws:
            ok = (row < SUBLANES - d) if reverse else (row >= d)
            mws.append(((SUBLANES - d) if reverse else d, jnp.where(ok, er, 0.0), jnp.where(ok, ei, 0.0)))

        def tile(i, carry):
            cr, ci, dr, di = carry
            t = (nt - 1 - i) if reverse else i
            r0 = pl.multiple_of(t * SUBLANES, SUBLANES)
            vr = x_ref[pl.ds(r0, SUBLANES), cols]
            vi = x_ref[pl.ds(r0, SUBLANES), icols]
            for sh, er, ei in mws:
                tr, ti = _cmul(er, ei, pltpu.roll(vr, sh, axis=0), pltpu.roll(vi, sh, axis=0))
                vr, vi = vr + tr, vi + ti
            tr, ti = _cmul(pr, pi, cr, ci)
            vr, vi = vr + tr, vi + ti
            o_ref[pl.ds(r0, SUBLANES), cols] = vr
            o_ref[pl.ds(r0, SUBLANES), icols] = vi
            if with_da:
                gr = jnp.where(row < SUBLANES - 1, pltpu.roll(vr, SUBLANES - 1, axis=0), cr)
                gi = jnp.where(row < SUBLANES - 1, pltpu.roll(vi, SUBLANES - 1, axis=0), ci)
                sr = s_ref[pl.ds(r0, SUBLANES), cols]
                si = s_ref[pl.ds(r0, SUBLANES), icols]
                dr = dr + sr * gr + si * gi
                di = di + sr * gi - si * gr
            ncr = jnp.broadcast_to(vr[edge:edge + 1, :], (SUBLANES, W))
            nci = jnp.broadcast_to(vi[edge:edge + 1, :], (SUBLANES, W))
            return ncr, nci, dr, di

        zero = jnp.zeros((SUBLANES, W), F32)
        cr, ci, dr, di = lax.fori_loop(0, nt, tile, (cr_ref[:, cols], ci_ref[:, cols], zero, zero), unroll=2)
        cr_ref[:, cols] = cr
        ci_ref[:, cols] = ci
        if with_da:
            dar_ref[:, cols] += dr
            dai_ref[:, cols] += di


def _ssm_fwd(z, bcat, ccat, dskip, lam_r, lam_i, name):
    L = z.shape[0]
    tb = _tile(L, _SCAN_TB)
    nrb = L // tb
    nt = tb // SUBLANES

    def body(u_ref, b_ref, c_ref, d_ref, ar_ref, ai_ref, s_ref, y_ref, yg_ref, xs_ref, cr_ref, ci_ref):
        @pl.when(pl.program_id(0) == 0)
        def _():
            cr_ref[...] = jnp.zeros_like(cr_ref)
            ci_ref[...] = jnp.zeros_like(ci_ref)

        u = u_ref[...]
        xs_ref[...] = _dot(u, b_ref[...])
        _scan_block(xs_ref, s_ref, None, ar_ref, ai_ref, cr_ref, ci_ref, None, None, nt, False)
        y = _dot(s_ref[...], c_ref[...]) + d_ref[...] * u
        y_ref[...] = y
        yg_ref[...] = _gelu(y).astype(yg_ref.dtype)

    full = lambda shp: pl.BlockSpec(shp, lambda i: (0, 0))
    rows = lambda w: pl.BlockSpec((tb, w), lambda i: (i, 0))
    return pl.pallas_call(
        body, name=name, grid=(nrb,),
        in_specs=[pl.BlockSpec((tb, C_WIDTH), lambda i: (i, 5)), full((C_WIDTH, 2 * N_STATE)), full((2 * N_STATE, C_WIDTH)),
                  full((1, C_WIDTH)), full((1, N_STATE)), full((1, N_STATE))],
        out_specs=[rows(2 * N_STATE), rows(C_WIDTH), rows(C_WIDTH)],
        out_shape=[_sds((L, 2 * N_STATE), F32), _sds((L, C_WIDTH), F32), _sds((L, C_WIDTH), _ACT)],
        scratch_shapes=[pltpu.VMEM((tb, 2 * N_STATE), F32)] + [pltpu.VMEM((SUBLANES, N_STATE), F32)] * 2,
        compiler_params=_cparams(5 * _nbytes((tb, 2 * N_STATE), F32), dimension_semantics=("arbitrary",)),
    )(_hbm(z), bcat, ccat, dskip, lam_r, lam_i)


def _ssm_bwd(dy, z, S, bcat, ccat, dskip, lam_r, lam_i, name):
    L = z.shape[0]
    tb = _tile(L, _SCAN_TB)
    nrb = L // tb
    nt = tb // SUBLANES

    def body(dy_ref, u_ref, s_ref, b_ref, c_ref, d_ref, ar_ref, ai_ref,
             du_ref, db_ref, dc_ref, dd_ref, dar_ref, dai_ref, xs_ref, gs_ref, cr_ref, ci_ref):
        step = pl.program_id(0)

        @pl.when(step == 0)
        def _():
            for r in (cr_ref, ci_ref, db_ref, dc_ref, dd_ref, dar_ref, dai_ref):
                r[...] = jnp.zeros_like(r)

        dyv, u = dy_ref[...], u_ref[...]
        xs_ref[...] = _dot(dyv, c_ref[...], _NT)
        _scan_block(xs_ref, gs_ref, s_ref, ar_ref, ai_ref, cr_ref, ci_ref, dar_ref, dai_ref, nt, True)
        g = gs_ref[...]
        du_ref[...] = (_dot(g, b_ref[...], _NT) + dyv * d_ref[...]).astype(du_ref.dtype)
        db_ref[...] += _dot(u, g, _TN)
        dc_ref[...] += _dot(s_ref[...], dyv, _TN)
        dd_ref[...] += jnp.sum(dyv * u, axis=0, keepdims=True)

        @pl.when(step == nrb - 1)
        def _():
            dar_ref[...] = jnp.broadcast_to(jnp.sum(dar_ref[...], axis=0, keepdims=True), dar_ref.shape)
            dai_ref[...] = jnp.broadcast_to(jnp.sum(dai_ref[...], axis=0, keepdims=True), dai_ref.shape)

    full = lambda shp: pl.BlockSpec(shp, lambda i: (0, 0))
    rows = lambda w, col=0: pl.BlockSpec((tb, w), lambda i: (nrb - 1 - i, col))
    acc = full((SUBLANES, N_STATE))
    return pl.pallas_call(
        body, name=name, grid=(nrb,),
        in_specs=[rows(C_WIDTH), rows(C_WIDTH, 5), rows(2 * N_STATE), full((C_WIDTH, 2 * N_STATE)),
                  full((2 * N_STATE, C_WIDTH)), full((1, C_WIDTH)), full((1, N_STATE)), full((1, N_STATE))],
        out_specs=[rows(C_WIDTH), full((C_WIDTH, 2 * N_STATE)), full((2 * N_STATE, C_WIDTH)), full((1, C_WIDTH)), acc, acc],
        out_shape=[_sds((L, C_WIDTH), _ACT), _sds((C_WIDTH, 2 * N_STATE), F32), _sds((2 * N_STATE, C_WIDTH), F32),
                   _sds((1, C_WIDTH), F32), _sds((SUBLANES, N_STATE), F32), _sds((SUBLANES, N_STATE), F32)],
        scratch_shapes=[pltpu.VMEM((tb, 2 * N_STATE), F32)] * 2 + [pltpu.VMEM((SUBLANES, N_STATE), F32)] * 2,
        compiler_params=_cparams(7 * _nbytes((tb, 2 * N_STATE), F32), dimension_semantics=("arbitrary",)),
    )(dy, _hbm(z), _hbm(S), bcat, ccat, dskip, lam_r, lam_i)


_GROUPS = ((0, 256), (256, 768), (768, 1024))


def _merge_fwd(ya, yb, g12, mixg, name):
    L = ya.shape[0]
    tl = _tile(L, _TL)

    def body(a_ref, b_ref, g_ref, m_ref, o_ref):
        g12v = g_ref[...]
        yc = g12v[:, :C_WIDTH] * _sigmoid(g12v[:, C_WIDTH:])
        for (lo, hi), y in zip(_GROUPS, (a_ref[...], b_ref[...], yc)):
            r = lax.rsqrt(jnp.mean(y * y, axis=-1, keepdims=True) + EPS)
            o_ref[:, lo:hi] = ((y * r) * m_ref[:, lo:hi]).astype(o_ref.dtype)

    row = lambda w: pl.BlockSpec((tl, w), lambda i: (i, 0))
    return pl.pallas_call(
        body, name=name, grid=(L // tl,),
        in_specs=[row(256), row(512), row(512), pl.BlockSpec((1, D_MODEL), lambda i: (0, 0))],
        out_specs=row(D_MODEL), out_shape=_sds((L, D_MODEL), _ACT),
    )(ya, yb, g12, mixg.reshape(1, D_MODEL))


def _merge_bwd(dy, ya, yb, g12, mixg, name):
    L = ya.shape[0]
    tl = _tile(L, _TL)

    def body(d_ref, a_ref, b_ref, g_ref, m_ref, da_ref, db_ref, dg_ref, dm_ref):
        @pl.when(pl.program_id(0) == 0)
        def _():
            dm_ref[...] = jnp.zeros_like(dm_ref)

        g12v = g_ref[...]
        g1, sg = g12v[:, :C_WIDTH], _sigmoid(g12v[:, C_WIDTH:])
        yc = g1 * sg
        outs = []
        for (lo, hi), y in zip(_GROUPS, (a_ref[...], b_ref[...], yc)):
            r = lax.rsqrt(jnp.mean(y * y, axis=-1, keepdims=True) + EPS)
            xhat = y * r
            d = d_ref[:, lo:hi]
            gy = d * m_ref[:, lo:hi]
            outs.append(r * (gy - xhat * jnp.mean(gy * xhat, axis=-1, keepdims=True)))
            dm_ref[:, lo:hi] += jnp.sum(d * xhat, axis=0, keepdims=True)
        da_ref[...] = outs[0]
        db_ref[...] = outs[1]
        dyc = outs[2]
        dg_ref[:, :C_WIDTH] = (dyc * sg).astype(dg_ref.dtype)
        dg_ref[:, C_WIDTH:] = (dyc * g1 * sg * (1.0 - sg)).astype(dg_ref.dtype)

    row = lambda w: pl.BlockSpec((tl, w), lambda i: (i, 0))
    one = pl.BlockSpec((1, D_MODEL), lambda i: (0, 0))
    return pl.pallas_call(
        body, name=name, grid=(L // tl,),
        in_specs=[row(D_MODEL), row(256), row(512), row(512), one],
        out_specs=[row(256), row(512), row(512), one],
        out_shape=[_sds((L, 256), F32), _sds((L, 512), F32),
                   _sds((L, 512), _ACT), _sds((1, D_MODEL), F32)],
    )(dy, ya, yb, g12, mixg.reshape(1, D_MODEL))


def _ple_bwd_elem(dh, gate, e, name):
    L, D = dh.shape
    tl = _tile(L, _TL)

    def body(d_ref, g_ref, e_ref, p_ref, o_ref):
        d, g = d_ref[...], g_ref[...]
        p_ref[...] = (d * e_ref[...] * g * (1.0 - g)).astype(p_ref.dtype)
        o_ref[...] = (d * g).astype(o_ref.dtype)

    row = pl.BlockSpec((tl, D), lambda i: (i, 0))
    return pl.pallas_call(
        body, name=name, grid=(L // tl,), in_specs=[row] * 3, out_specs=[row] * 2,
        out_shape=[_sds((L, D), _ACT)] * 2,
        compiler_params=_cparams(4 * _nbytes((tl, D), F32)),
    )(dh, gate, e)


def _dskip_bwd(dy, z, name):
    L = dy.shape[0]
    tl = _tile(L, _TL)

    def body(d_ref, u_ref, o_ref):
        @pl.when(pl.program_id(0) == 0)
        def _():
            o_ref[...] = jnp.zeros_like(o_ref)

        o_ref[...] += jnp.sum(d_ref[...] * u_ref[...], axis=0, keepdims=True)

    return pl.pallas_call(
        body, name=name, grid=(L // tl,),
        in_specs=[pl.BlockSpec((tl, C_WIDTH), lambda i: (i, 0)), pl.BlockSpec((tl, C_WIDTH), lambda i: (i, 5))],
        out_specs=pl.BlockSpec((1, C_WIDTH), lambda i: (0, 0)),
        out_shape=_sds((1, C_WIDTH), F32),
    )(dy, z)


def _loss_fwd_bwd(y, target):
    L, D = y.shape
    tl = _tile(L, _TL)

    def body(y_ref, t_ref, l_ref, d_ref):
        @pl.when(pl.program_id(0) == 0)
        def _():
            l_ref[...] = jnp.zeros_like(l_ref)

        e = y_ref[...] - t_ref[...]
        d_ref[...] = e * (1.0 / D)
        part = jnp.sum(jnp.sum(e * e, axis=-1, keepdims=True), axis=0, keepdims=True)
        l_ref[...] += jnp.broadcast_to(part, l_ref.shape)

    row = pl.BlockSpec((tl, D), lambda i: (i, 0))
    return pl.pallas_call(
        body, name="loss", grid=(L // tl,), in_specs=[row, row],
        out_specs=[pl.BlockSpec((SUBLANES, LANES), lambda i: (0, 0)), row],
        out_shape=[_sds((SUBLANES, LANES), F32), _sds((L, D), F32)],
    )(y, target)


def _adamw(w, g, m, v, name):
    R, C = w.shape
    tr = R if R <= 512 else _tile_rows(R, 512)

    def body(w_ref, g_ref, m_ref, v_ref, d_ref, nm_ref, nv_ref):
        gv = g_ref[...]
        nm = ADAM_B1 * m_ref[...] + (1.0 - ADAM_B1) * gv
        nv = ADAM_B2 * v_ref[...] + (1.0 - ADAM_B2) * (gv * gv)
        m_hat = nm / (1.0 - ADAM_B1 ** ADAM_STEP)
        v_hat = nv / (1.0 - ADAM_B2 ** ADAM_STEP)
        d_ref[...] = -ADAM_LR * (m_hat / (jnp.sqrt(v_hat) + ADAM_EPS) + ADAM_WD * w_ref[...])
        nm_ref[...] = nm
        nv_ref[...] = nv

    blk = pl.BlockSpec((tr, C), lambda i: (i, 0))
    return pl.pallas_call(
        body, name=name, grid=(R // tr,), in_specs=[blk] * 4, out_specs=[blk] * 3,
        out_shape=[_sds((R, C), F32)] * 3,
        compiler_params=_cparams(7 * _nbytes((tr, C), F32)),
    )(w, g, m, v)


def _tile_rows(R, pref):
    t = pref
    while R % t:
        t -= SUBLANES
    assert t > 0
    return t


def _add_n(xs, name):
    R, C = xs[0].shape
    tr = R if R <= 512 else _tile_rows(R, 512)
    n = len(xs)

    def body(*refs):
        acc = refs[0][...].astype(F32)
        for r in refs[1:n]:
            acc = acc + r[...].astype(F32)
        refs[n][...] = acc

    blk = pl.BlockSpec((tr, C), lambda i: (i, 0))
    return pl.pallas_call(
        body, name=name, grid=(R // tr,), in_specs=[blk] * n, out_specs=blk,
        out_shape=_sds((R, C), F32),
        compiler_params=_cparams((n + 1) * _nbytes((tr, C), F32)),
    )(*xs)


class _Exchange:
    def __init__(self, ins, aliased, fresh, n_sems, start, wait, done):
        self.ins, self.aliased, self.fresh, self.n_sems = ins, aliased, fresh, n_sems
        self.start, self.wait, self.done = start, wait, done


def _mm_host(lp, key, *args, **kw):
    plan = lp.get(key)
    if plan is None:
        return _mm(*args, **kw)
    if not isinstance(plan, _Exchange):
        plan = plan()
    res, outs = _mm(*args, comm=plan, **kw)
    plan.done(outs)
    return res


def _relu2(acc):
    r = jnp.maximum(acc, 0.0)
    return (r * r,)


def _rms_rows(x, g):
    return (x * lax.rsqrt(jnp.mean(x * x, axis=-1, keepdims=True) + EPS)) * g


def _resid_norm_epi(acc, res, g):
    h = res + acc
    return h, _rms_rows(h, g)


def _rms_bwd_epi(acc, h, dres, g):
    r = lax.rsqrt(jnp.mean(h * h, axis=-1, keepdims=True) + EPS)
    xhat = h * r
    gy = acc * g
    dx = r * (gy - xhat * jnp.mean(gy * xhat, axis=-1, keepdims=True))
    return dres + dx, jnp.sum(acc * xhat, axis=0, keepdims=True)


def _layer_fwd(h, xn, lp, cos, sin, g_next):
    L = h.shape[0]
    row = lambda n: lp[n].reshape(1, D_MODEL)
    z = _mm(xn, lp["w_in"], mode="nn", M=L, N=IN_COLS, K=D_MODEL, b_cb=True, out_dtypes=[F32], name="f_w_in")
    ya = _gmlp_fwd(z, lp["ws"], lp["bfull"], lp["lgf"], lp["lbf"], "f_gmlp")
    q, k = _qk_prep(z, cos, sin, lp["gq"], lp["gk"], "f_qk_prep")
    yb = _attn_fwd(q, k, z, lp["sinks"], "f_attn")
    S, y, yg = _ssm_fwd(z, lp["bcat"], lp["ccat"], lp["dskip"], lp["lam_r"], lp["lam_i"], "f_ssm")
    g12 = _mm(yg, lp["w12"], mode="nn", M=L, N=2 * C_WIDTH, K=C_WIDTH, out_dtypes=[F32], name="f_glu")
    ycat = _merge_fwd(ya, yb, g12, lp["mix_out_g"], "f_merge")
    h1, hn = _mm(ycat, lp["w_out"], mode="nn", M=L, N=D_MODEL, K=D_MODEL, extras=[(h, 0), (row("mlp_norm_g"), 0)],
                 epi=_resid_norm_epi, out_dtypes=[F32, _ACT], name="f_w_out")
    r = _mm_host(lp, "x_ff1", hn, lp["w_ff1"], mode="nn", M=L, N=D_FF, K=D_MODEL, b_cb=True, epi=_relu2,
                 out_dtypes=[_ACT], name="f_ff1")
    h2, hn3 = _mm_host(lp, "x_ff2", r, lp["w_ff2"], mode="nn", M=L, N=D_MODEL, K=D_FF,
                       extras=[(h1, 0), (row("ple_norm_g"), 0)],
                       epi=_resid_norm_epi, out_dtypes=[F32, _ACT], name="f_ff2")
    e = _mm(lp["p"], lp["w_ple_proj"], mode="nn", M=L, N=D_MODEL, K=PLE_DIM, b_cb=True, tk=PLE_DIM,
            out_dtypes=[F32], name="f_ple_proj")

    def gate_epi(acc, h2_, e_, *g):
        gate_ = _sigmoid(acc)
        h3_ = h2_ + gate_ * e_
        return (h3_, gate_) + ((_rms_rows(h3_, g[0]),) if g else ())

    outs = _mm_host(lp, "x_gate", hn3, lp["w_ple_gate"], mode="nn", M=L, N=D_MODEL, K=D_MODEL,
                    extras=[(h2, 0), (e, 0)] + ([(g_next.reshape(1, D_MODEL), 0)] if g_next is not None else []),
                    epi=gate_epi, out_dtypes=[F32, F32] + ([_ACT] if g_next is not None else []), name="f_ple_gate")
    h3, gate = outs[0], outs[1]
    xn_next = outs[2] if g_next is not None else None
    saved = dict(h=h, xn=xn, z=z, ya=ya, q=q, k=k, yb=yb, S=S, y=y, yg=yg, g12=g12, ycat=ycat, h1=h1, hn=hn,
                 r=r, h2=h2, hn3=hn3, e=e, gate=gate)
    return h3, xn_next, saved


def _layer_bwd(dh3, lp, sv, cos, sin):
    L = dh3.shape[0]
    z = sv["z"]
    dpre, de = _ple_bwd_elem(dh3, sv["gate"], sv["e"], "b_ple_elem")
    stk = {n: None for n in BIG}
    d_gate = _mm(sv["hn3"], dpre, mode="tn", M=D_MODEL, N=D_MODEL, K=L, out_dtypes=[F32], name="b_dw_gate",
                 o_stack=stk["w_ple_gate"])
    d_proj = _mm(lp["p"], de, mode="tn", M=PLE_DIM, N=D_MODEL, K=L, o_cb=True, tm=PLE_DIM,
                 out_dtypes=[F32], name="b_dw_proj", o_stack=stk["w_ple_proj"])
    row = lambda n: lp[n].reshape(1, D_MODEL)
    dh2, dg_ple = _mm_host(lp, "x_bwd0", dpre, lp["w_ple_gate"], mode="nt", M=L, N=D_MODEL, K=D_MODEL,
                           extras=[(sv["h2"], 0), (dh3, 0), (row("ple_norm_g"), 0)], epi=_rms_bwd_epi,
                           out_dtypes=[F32, F32], n_acc=1, name="b_dx_gate")
    da = _mm_host(lp, "x_bwd", dh2, lp["w_ff2"], mode="nt", M=L, N=D_FF, K=D_MODEL, extras=[(sv["r"], 0)],
                  epi=lambda acc, r_: (acc * (2.0 * jnp.sqrt(r_.astype(F32))),), out_dtypes=[_ACT], name="b_dx_ff2")
    d_ff2 = _mm_host(lp, "x_bwd2", sv["r"], dh2, mode="tn", M=D_FF, N=D_MODEL, K=L, out_dtypes=[F32], name="b_dw_ff2")
    d_ff1 = _mm(sv["hn"], da, mode="tn", M=D_MODEL, N=D_FF, K=L, o_cb=True, out_dtypes=[F32], name="b_dw_ff1",
                o_stack=stk["w_ff1"])
    dh1, dg_mlp = _mm(da, lp["w_ff1"], mode="nt", M=L, N=D_MODEL, K=D_FF, b_cb=True,
                      extras=[(sv["h1"], 0), (dh2, 0), (row("mlp_norm_g"), 0)], epi=_rms_bwd_epi,
                      out_dtypes=[F32, F32], n_acc=1, name="b_dx_ff1")
    d_out = _mm(sv["ycat"], dh1, mode="tn", M=D_MODEL, N=D_MODEL, K=L, out_dtypes=[F32], name="b_dw_out",
                o_stack=stk["w_out"])
    dycat = _mm(dh1, lp["w_out"], mode="nt", M=L, N=D_MODEL, K=D_MODEL, out_dtypes=[F32], name="b_dx_out")
    dya, dyb, dg12, dmix = _merge_bwd(dycat, sv["ya"], sv["yb"], sv["g12"], lp["mix_out_g"], "b_merge")
    d_w12 = _mm(sv["yg"], dg12, mode="tn", M=C_WIDTH, N=2 * C_WIDTH, K=L, tm=C_WIDTH, out_dtypes=[F32], name="b_dw_glu",
                o_stack=stk["w12"])
    dy = _mm(dg12, lp["w12"], mode="nt", M=L, N=C_WIDTH, K=2 * C_WIDTH, tk=2 * C_WIDTH, extras=[(sv["y"], 0)],
             epi=lambda acc, y_: (acc * _gelu_grad(y_),), out_dtypes=[F32], name="b_dx_glu")
    dzc, d_bcat, d_ccat, dd, dar, dai = _ssm_bwd(dy, z, sv["S"], lp["bcat"], lp["ccat"], lp["dskip"],
                                                 lp["lam_r"], lp["lam_i"], "b_ssm")
    dq, dkc, dkp, dvc, dvp, dsink = _attn_bwd(sv["q"], sv["k"], z, lp["sinks"], dyb, "b_attn")
    dzq, dzk, dzv, dgq, dgk = _qk_prep_bwd(z, dq, dkc, dkp, dvc, dvp, cos, sin, lp["gq"], lp["gk"], "b_qk_prep")
    dza, dws, dbs, dlg, dlb = _gmlp_bwd(z, dya, lp["ws"], lp["wsT"], lp["bfull"], lp["lgf"], lp["lbf"], "b_gmlp")
    dz = jnp.concatenate([dza, dzq, dzk, dzv, dzc], axis=1)
    d_in = _mm(sv["xn"], dz, mode="tn", M=D_MODEL, N=IN_COLS, K=L, o_cb=True, out_dtypes=[F32], name="b_dw_in",
               o_stack=stk["w_in"])
    dh, dg_attn = _mm(dz, lp["w_in"], mode="nt", M=L, N=D_MODEL, K=IN_COLS, b_cb=True,
                      extras=[(sv["h"], 0), (dh1, 0), (row("attn_norm_g"), 0)], epi=_rms_bwd_epi,
                      out_dtypes=[F32, F32], n_acc=1, name="b_dx_in")
    grads = dict(w_in=d_in, w12=d_w12, w_out=d_out, w_ff1=d_ff1, w_ff2=d_ff2, w_ple_gate=d_gate, w_ple_proj=d_proj,
                 attn_norm_g=dg_attn.reshape(D_MODEL), mlp_norm_g=dg_mlp.reshape(D_MODEL),
                 ple_norm_g=dg_ple.reshape(D_MODEL), mix_out_g=dmix.reshape(D_MODEL),
                 dws=dws, dbs=dbs, dlg=dlg, dlb=dlb, dgq=dgq, dgk=dgk, dsink=dsink,
                 dar=dar, dai=dai, d_bcat=d_bcat, d_ccat=d_ccat, dd=dd)
    return dh, grads


SMALL = ("attn_norm_g", "gmlp_ln_g", "gmlp_ln_b", "gmlp_ws", "gmlp_bs", "q_norm_g", "k_norm_g", "sinks",
         "ssm_a_re", "ssm_a_im", "ssm_log_dt", "ssm_b_re", "ssm_b_im", "ssm_c_re", "ssm_c_im", "ssm_d",
         "mix_out_g", "mlp_norm_g", "ple_norm_g")
BIG = ("w_in", "w12", "w_out", "w_ff1", "w_ff2", "w_ple_gate", "w_ple_proj")
COL_SHARDED = ("w_in", "w_ff1", "w_ple_proj")


def _block_diag(t):
    nl, g, a, b = t.shape
    eye = jnp.eye(g, dtype=t.dtype)
    return (t[:, :, :, None, :] * eye[None, :, None, :, None]).reshape(nl, g * a, g * b)


def _diag_blocks(t, a, b):
    nl = t.shape[0]
    t = t.reshape(nl, C_GROUPS, a, C_GROUPS, b)
    idx = jnp.arange(C_GROUPS)
    return jnp.moveaxis(t[:, idx, :, idx, :], 0, 1)


def _local_step(x, p, positions, target, sw, bw):
    nl = sw["attn_norm_g"].shape[0]
    G = nl * C_GROUPS
    zeros = lambda *s: jnp.zeros(s, F32)
    are = sw["ssm_a_re"].reshape(G, 1, C_STATE)
    aim = sw["ssm_a_im"].reshape(G, 1, C_STATE)
    ldt = jnp.broadcast_to(sw["ssm_log_dt"][..., None], (nl, C_GROUPS, C_STATE)).reshape(G, 1, C_STATE)
    bre = jnp.swapaxes(sw["ssm_b_re"], -1, -2).reshape(G, C_GROUP, C_STATE)
    bim = jnp.swapaxes(sw["ssm_b_im"], -1, -2).reshape(G, C_GROUP, C_STATE)
    lr, li, bbr, bbi = _ssm_prep(are, aim, ldt, bre, bim)
    unflat = lambda t: t.reshape(nl, C_GROUPS, C_GROUP, C_STATE)
    lp = dict(
        attn_norm_g=sw["attn_norm_g"], mlp_norm_g=sw["mlp_norm_g"], ple_norm_g=sw["ple_norm_g"],
        mix_out_g=sw["mix_out_g"], sinks=sw["sinks"],
        ws=sw["gmlp_ws"], wsT=jnp.swapaxes(sw["gmlp_ws"], -1, -2),
        bfull=jnp.concatenate([zeros(nl, A_HEADS, CHUNK, HEAD_DIM),
                               jnp.broadcast_to(sw["gmlp_bs"][..., None], (nl, A_HEADS, CHUNK, HEAD_DIM))], axis=-1),
        lgf=jnp.concatenate([zeros(nl, A_HEADS, HEAD_DIM), sw["gmlp_ln_g"]], axis=-1),
        lbf=jnp.concatenate([zeros(nl, A_HEADS, HEAD_DIM), sw["gmlp_ln_b"]], axis=-1),
        gq=jnp.tile(sw["q_norm_g"], (1, 2)).reshape(nl, 1, LANES),
        gk=jnp.tile(sw["k_norm_g"], (1, 2)).reshape(nl, 1, LANES),
        lam_r=lr.reshape(nl, 1, N_STATE), lam_i=li.reshape(nl, 1, N_STATE),
        bcat=jnp.concatenate([_block_diag(unflat(bbr)), _block_diag(unflat(bbi))], axis=-1),
        ccat=jnp.concatenate([_block_diag(jnp.swapaxes(sw["ssm_c_re"], -1, -2)),
                              -_block_diag(jnp.swapaxes(sw["ssm_c_im"], -1, -2))], axis=1),
        dskip=sw["ssm_d"].reshape(nl, 1, C_WIDTH))
    cos, sin = _rope_tables(positions)

    def layer_params(l, hooks):
        lpi = {n: v[l] for n, v in lp.items()}
        lpi.update(bw.layer(l))
        lpi["p"] = (p, l)
        lpi.update(hooks)
        return lpi

    h, saved = x, []
    xn = _rms_fwd(x, sw["attn_norm_g"][0], "f_norm_attn")
    for l in range(nl):
        g_next = sw["attn_norm_g"][l + 1] if l + 1 < nl else None
        h, xn, sv = _layer_fwd(h, xn, layer_params(l, bw.fwd_hooks(l)), cos, sin, g_next)
        saved.append(sv)
    sse, dh = _loss_fwd_bwd(h, target)

    per_layer = [None] * nl
    for l in reversed(range(nl)):
        dh, gl = _layer_bwd(dh, layer_params(l, bw.bwd_hooks(l)), saved[l], cos, sin)
        bw.grads(l, {n: gl.pop(n) for n in BIG})
        per_layer[l] = gl
    grad_x = dh
    g = {n: jnp.stack([per_layer[l][n] for l in range(nl)]) for n in per_layer[0]}

    d_bcat = g["d_bcat"]
    dbr = _diag_blocks(d_bcat[:, :, :N_STATE], C_GROUP, C_STATE).reshape(G, C_GROUP, C_STATE)
    dbi = _diag_blocks(d_bcat[:, :, N_STATE:], C_GROUP, C_STATE).reshape(G, C_GROUP, C_STATE)
    dlr = g["dar"][:, 0].reshape(G, 1, C_STATE)
    dli = g["dai"][:, 0].reshape(G, 1, C_STATE)
    g_are, g_aim, g_ldt, g_bre, g_bim = _ssm_prep_bwd(are, aim, ldt, bre, bim, dlr, dli, dbr, dbi)
    d_ccat = g["d_ccat"]
    sg = dict(
        attn_norm_g=g["attn_norm_g"], mlp_norm_g=g["mlp_norm_g"], ple_norm_g=g["ple_norm_g"], mix_out_g=g["mix_out_g"],
        gmlp_ln_g=g["dlg"][:, :, 0, HEAD_DIM:], gmlp_ln_b=g["dlb"][:, :, 0, HEAD_DIM:],
        gmlp_ws=g["dws"], gmlp_bs=g["dbs"][:, :, :, HEAD_DIM],
        q_norm_g=g["dgq"][:, 0, :HEAD_DIM], k_norm_g=g["dgk"][:, 0, :HEAD_DIM],
        sinks=g["dsink"][:, 0, :B_Q_HEADS],
        ssm_a_re=g_are.reshape(nl, C_GROUPS, C_STATE), ssm_a_im=g_aim.reshape(nl, C_GROUPS, C_STATE),
        ssm_log_dt=g_ldt[:, 0, 0].reshape(nl, C_GROUPS),
        ssm_b_re=jnp.swapaxes(g_bre.reshape(nl, C_GROUPS, C_GROUP, C_STATE), -1, -2),
        ssm_b_im=jnp.swapaxes(g_bim.reshape(nl, C_GROUPS, C_GROUP, C_STATE), -1, -2),
        ssm_c_re=jnp.swapaxes(_diag_blocks(d_ccat[:, :N_STATE], C_STATE, C_GROUP), -1, -2),
        ssm_c_im=-jnp.swapaxes(_diag_blocks(d_ccat[:, N_STATE:], C_STATE, C_GROUP), -1, -2),
        ssm_d=g["dd"].reshape(nl, C_GROUPS, C_GROUP),
    )
    return (sse[0, 0], grad_x, sg) + tuple(bw.finish(_pack(sg)))


_ANY = pl.BlockSpec(memory_space=pl.ANY)
N_LAYERS = 4


def _mesh_pos():
    x, y, c = lax.axis_index("x"), lax.axis_index("y"), lax.axis_index("c")
    chips = [(1 - x, y), (x, 1 - y), (1 - x, 1 - y)]
    return x, y, c, 2 * x + y, chips


def _cast_into_slot(ws, j, name):
    nl, R, _ = ws[0].shape
    widths = [w.shape[2] for w in ws]
    C = sum(widths)
    tr = R if R <= 512 else _tile_rows(R, 512)
    nw = len(ws)

    def body(s_ref, *refs):
        o_ref = refs[nw]
        off = 0
        for r, wd in zip(refs[:nw], widths):
            o_ref[:, off:off + wd] = r[...].astype(o_ref.dtype)
            off += wd

    return pl.pallas_call(
        body, name=name,
        grid_spec=pltpu.PrefetchScalarGridSpec(
            num_scalar_prefetch=1, grid=(nl, R // tr),
            in_specs=[pl.BlockSpec((None, tr, wd), lambda l, i, s: (l, i, 0)) for wd in widths],
            out_specs=pl.BlockSpec((None, None, tr, C), lambda l, i, s: (l, s[0], i, 0))),
        out_shape=_sds((nl, N_CHIPS, R, C), _MXU),
    )(jnp.reshape(j, (1,)).astype(jnp.int32), *ws)


def _gather_weights(bufs):
    nk = len(bufs)

    def body(*refs):
        ins, outs = refs[:nk], refs[nk:2 * nk]
        send_sems, recv_sems = refs[2 * nk:]
        x, y, c, j, chips = _mesh_pos()
        mine, other = pl.ds(2 * c, 2), pl.ds(2 * (1 - c), 2)

        def ici(t, q):
            cx, cy = chips[q]
            return pltpu.make_async_remote_copy(
                src_ref=ins[t].at[mine, j], dst_ref=outs[t].at[mine, j],
                send_sem=send_sems.at[6 * t + q], recv_sem=recv_sems.at[6 * t + q],
                device_id=(cx, cy, c), device_id_type=MESH)

        def landed(t, q):
            cx, cy = chips[q]
            blk = outs[t].at[mine, 2 * cx + cy]
            return pltpu.make_async_remote_copy(
                src_ref=blk, dst_ref=blk, send_sem=send_sems.at[6 * t + q], recv_sem=recv_sems.at[6 * t + q],
                device_id=(cx, cy, c), device_id_type=MESH)

        def fwd(t, q, rows):
            cx, cy = chips[q]
            blk = outs[t].at[rows, 2 * cx + cy]
            return pltpu.make_async_remote_copy(
                src_ref=blk, dst_ref=blk, send_sem=send_sems.at[6 * t + 3 + q], recv_sem=recv_sems.at[6 * t + 3 + q],
                device_id=(x, y, 1 - c), device_id_type=MESH)

        for t in range(nk):
            for q in range(3):
                ici(t, q).start()
        for t in range(nk):
            for q in range(3):
                landed(t, q).wait_recv()
                fwd(t, q, mine).start()
        for t in range(nk):
            for q in range(3):
                fwd(t, q, other).wait_recv()
        for t in range(nk):
            for q in range(3):
                ici(t, q).wait_send()
                fwd(t, q, mine).wait_send()

    return pl.pallas_call(
        body, name="gather_weights", in_specs=[_ANY] * nk, out_specs=[_ANY] * nk,
        out_shape=[_sds(b.shape, b.dtype) for b in bufs],
        input_output_aliases={t: t for t in range(nk)},
        scratch_shapes=[pltpu.SemaphoreType.DMA((6 * nk,)), pltpu.SemaphoreType.DMA((6 * nk,))],
    )(*bufs)


def _exchange_sibling_half(gl):
    nk = len(gl)

    def body(*refs):
        ins, outs = refs[:nk], refs[nk:2 * nk]
        send_sems, recv_sems = refs[2 * nk:]
        x, y, c, _, _ = _mesh_pos()
        cps = [pltpu.make_async_remote_copy(
            src_ref=ins[t].at[pl.ds(2 * (1 - c), 2)], dst_ref=outs[t],
            send_sem=send_sems.at[t], recv_sem=recv_sems.at[t],
            device_id=(x, y, 1 - c), device_id_type=MESH) for t in range(nk)]
        for cp in cps:
            cp.start()
        for cp in cps:
            cp.wait()

    return pl.pallas_call(
        body, name="reduce_sibling", in_specs=[_ANY] * nk, out_specs=[_ANY] * nk,
        out_shape=[_sds((2,) + g.shape[1:], g.dtype) for g in gl],
        scratch_shapes=[pltpu.SemaphoreType.DMA((nk,)), pltpu.SemaphoreType.DMA((nk,))],
    )(*gl)


def _exchange_chips(ps):
    nk = len(ps)

    def body(*refs):
        ins, outs = refs[:nk], refs[nk:2 * nk]
        send_sems, recv_sems = refs[2 * nk:]
        x, y, c, j, chips = _mesh_pos()

        def send(t, q):
            cx, cy = chips[q]
            return pltpu.make_async_remote_copy(
                src_ref=ins[t].at[:, 2 * cx + cy], dst_ref=outs[t].at[j],
                send_sem=send_sems.at[3 * t + q], recv_sem=recv_sems.at[3 * t + q],
                device_id=(cx, cy, c), device_id_type=MESH)

        def landed(t, q):
            cx, cy = chips[q]
            blk = outs[t].at[2 * cx + cy]
            return pltpu.make_async_remote_copy(
                src_ref=blk, dst_ref=blk, send_sem=send_sems.at[3 * t + q], recv_sem=recv_sems.at[3 * t + q],
                device_id=(cx, cy, c), device_id_type=MESH)

        for t in range(nk):
            for q in range(3):
                send(t, q).start()
        for t in range(nk):
            for q in range(3):
                landed(t, q).wait_recv()
        for t in range(nk):
            for q in range(3):
                send(t, q).wait_send()

    return pl.pallas_call(
        body, name="reduce_chips", in_specs=[_ANY] * nk, out_specs=[_ANY] * nk,
        out_shape=[_sds((N_CHIPS, 2) + p.shape[2:], p.dtype) for p in ps],
        scratch_shapes=[pltpu.SemaphoreType.DMA((3 * nk,)), pltpu.SemaphoreType.DMA((3 * nk,))],
    )(*ps)


def _share_sibling(fs):
    nk = len(fs)

    def body(*refs):
        ins, outs = refs[:nk], refs[nk:2 * nk]
        send_sems, recv_sems = refs[2 * nk:]
        x, y, c, _, _ = _mesh_pos()
        mine = pl.ds(2 * c, 2)
        cps = [pltpu.make_async_remote_copy(
            src_ref=ins[t].at[mine], dst_ref=outs[t].at[mine], send_sem=send_sems.at[t], recv_sem=recv_sems.at[t],
            device_id=(x, y, 1 - c), device_id_type=MESH) for t in range(nk)]
        for cp in cps:
            cp.start()
        for cp in cps:
            cp.wait_send()
        for t in range(nk):
            blk = outs[t].at[pl.ds(2 * (1 - c), 2)]
            pltpu.make_async_remote_copy(
                src_ref=blk, dst_ref=blk, send_sem=send_sems.at[t], recv_sem=recv_sems.at[t],
                device_id=(x, y, 1 - c), device_id_type=MESH).wait_recv()

    return pl.pallas_call(
        body, name="share_sibling", in_specs=[_ANY] * nk, out_specs=[_ANY] * nk,
        out_shape=[_sds(f.shape, f.dtype) for f in fs],
        input_output_aliases={t: t for t in range(nk)},
        scratch_shapes=[pltpu.SemaphoreType.DMA((nk,)), pltpu.SemaphoreType.DMA((nk,))],
    )(*fs)


def _add_own_half(gl, r1, c, name):
    _, ns, R, C = gl.shape
    rows = 2 * ns * R
    tr = _tile_rows(rows, 512)
    nblk = rows // tr

    def body(s_ref, a_ref, b_ref, o_ref):
        o_ref[...] = (a_ref[...] + b_ref[...]).astype(o_ref.dtype)

    out = pl.pallas_call(
        body, name=name,
        grid_spec=pltpu.PrefetchScalarGridSpec(
            num_scalar_prefetch=1, grid=(nblk,),
            in_specs=[pl.BlockSpec((tr, C), lambda i, s: (s[0] * nblk + i, 0)), pl.BlockSpec((tr, C), lambda i, s: (i, 0))],
            out_specs=pl.BlockSpec((tr, C), lambda i, s: (i, 0))),
        out_shape=_sds((rows, C), _WIRE),
        compiler_params=_cparams(3 * _nbytes((tr, C), F32)),
    )(jnp.reshape(c, (1,)).astype(jnp.int32), gl.reshape(2 * rows, C), r1.reshape(rows, C))
    return out.reshape(2, ns, R, C)


def _add_chips(p, r2, j, c, name):
    _, ns, R, C = p.shape
    tr = R if R <= 512 else _tile_rows(R, 512)

    def body(s_ref, own, a1, a2, a3, o_ref):
        f = lambda r: r[...].astype(F32)
        o_ref[...] = ((f(own) + f(a1)) + f(a2)) + f(a3)

    blk = (None, None, tr, C)
    return pl.pallas_call(
        body, name=name,
        grid_spec=pltpu.PrefetchScalarGridSpec(
            num_scalar_prefetch=1, grid=(2, R // tr),
            in_specs=[pl.BlockSpec(blk, lambda h, i, s: (h, s[0], i, 0))]
            + [pl.BlockSpec(blk, lambda h, i, s, k=k: ((s[0] + k) % N_CHIPS, h, i, 0)) for k in (1, 2, 3)],
            out_specs=pl.BlockSpec((None, tr, C), lambda h, i, s: (2 * s[1] + h, i, 0))),
        out_shape=_sds((N_LAYERS, R, C), F32),
        compiler_params=_cparams(6 * _nbytes((tr, C), F32)),
    )(jnp.stack([j, c]).astype(jnp.int32), p, r2, r2, r2)


def _allreduce_small(buf, plan=None):
    Rs = buf.shape[0]
    nx = 0 if plan is None else len(plan.ins)
    x_out_shape = [] if plan is None else [_sds(sh, dt) for sh, dt in plan.fresh]
    assert plan is None or not any(plan.aliased)
    nxo = len(x_out_shape)

    def body(*refs):
        b_ref, x_ins = refs[0], refs[1:1 + nx]
        o_ref, x_outs = refs[1 + nx], refs[2 + nx:2 + nx + nxo]
        t_ref, slots_ref, send_sems, recv_sems = refs[2 + nx + nxo:6 + nx + nxo]
        x_sems = refs[6 + nx + nxo:]
        if plan is not None:
            plan.start(x_ins, x_outs, *x_sems)
        x, y, c, j, chips = _mesh_pos()
        sib = pltpu.make_async_remote_copy(
            src_ref=b_ref, dst_ref=t_ref, send_sem=send_sems.at[0], recv_sem=recv_sems.at[0],
            device_id=(x, y, 1 - c), device_id_type=MESH)
        sib.start()
        sib.wait()
        slots_ref[j] = b_ref[...] + t_ref[...]

        def send(q):
            cx, cy = chips[q]
            return pltpu.make_async_remote_copy(
                src_ref=slots_ref.at[j], dst_ref=slots_ref.at[j], send_sem=send_sems.at[1 + q],
                recv_sem=recv_sems.at[1 + q], device_id=(cx, cy, c), device_id_type=MESH)

        def landed(q):
            cx, cy = chips[q]
            blk = slots_ref.at[2 * cx + cy]
            return pltpu.make_async_remote_copy(
                src_ref=blk, dst_ref=blk, send_sem=send_sems.at[1 + q], recv_sem=recv_sems.at[1 + q],
                device_id=(cx, cy, c), device_id_type=MESH)

        for q in range(3):
            send(q).start()
        for q in range(3):
            landed(q).wait_recv()
        for q in range(3):
            send(q).wait_send()
        o_ref[...] = ((slots_ref[0] + slots_ref[1]) + slots_ref[2]) + slots_ref[3]
        if plan is not None:
            plan.wait(x_ins, x_outs, *x_sems)

    vm = pl.BlockSpec(memory_space=pltpu.VMEM)
    outs = pl.pallas_call(
        body, name="allreduce_small", in_specs=[vm] + [_ANY] * nx, out_specs=[vm] + [_ANY] * nxo,
        out_shape=[_sds((Rs, LANES), F32)] + x_out_shape,
        scratch_shapes=[pltpu.VMEM((Rs, LANES), F32), pltpu.VMEM((N_CHIPS, Rs, LANES), F32),
                        pltpu.SemaphoreType.DMA((4,)), pltpu.SemaphoreType.DMA((4,))]
        + ([pltpu.SemaphoreType.DMA((plan.n_sems,))] * 2 if plan is not None else []),
        compiler_params=_cparams(4 * _nbytes((Rs, LANES), F32)),
    )(buf, *([] if plan is None else plan.ins))
    if plan is not None:
        plan.done(list(outs[1:]))
    return outs[0]


def _own_rows(c, R):
    return pl.ds(c * (R // 2), R // 2)


def _cast_layer_slot(ws, l, j, name):
    _, R, _ = ws[0].shape
    widths = [w.shape[2] for w in ws]
    C = sum(widths)
    tr = R if R <= 512 else _tile_rows(R, 512)
    nw = len(ws)

    def body(s_ref, *refs):
        o_ref = refs[nw]
        off = 0
        for r, wd in zip(refs[:nw], widths):
            o_ref[:, off:off + wd] = r[...].astype(o_ref.dtype)
            off += wd

    return pl.pallas_call(
        body, name=name,
        grid_spec=pltpu.PrefetchScalarGridSpec(
            num_scalar_prefetch=1, grid=(R // tr,),
            in_specs=[pl.BlockSpec((None, tr, wd), lambda i, s: (l, i, 0)) for wd in widths],
            out_specs=pl.BlockSpec((None, tr, C), lambda i, s: (s[0], i, 0))),
        out_shape=_sds((N_CHIPS, R, C), _MXU),
    )(jnp.reshape(j, (1,)).astype(jnp.int32), *ws)


def _gather_ici(bufs, done):
    nk = len(bufs)

    def copy(ins, outs, ss, rs, t, q, landed):
        x, y, c, j, chips = _mesh_pos()
        cx, cy = chips[q]
        rows = _own_rows(c, ins[t].shape[1])
        src = outs[t].at[2 * cx + cy, rows] if landed else ins[t].at[j, rows]
        dst = outs[t].at[2 * cx + cy, rows] if landed else outs[t].at[j, rows]
        return pltpu.make_async_remote_copy(src_ref=src, dst_ref=dst, send_sem=ss.at[3 * t + q], recv_sem=rs.at[3 * t + q],
                                            device_id=(cx, cy, c), device_id_type=MESH)

    def start(ins, outs, ss, rs):
        for t in range(nk):
            for q in range(3):
                copy(ins, outs, ss, rs, t, q, False).start()

    def wait(ins, outs, ss, rs):
        for t in range(nk):
            for q in range(3):
                copy(ins, outs, ss, rs, t, q, True).wait_recv()
                copy(ins, outs, ss, rs, t, q, False).wait_send()

    return _Exchange(bufs, [True] * nk, [], 3 * nk, start, wait, done)


def _gather_d2d(bufs, done):
    nk = len(bufs)

    def copy(ins, outs, ss, rs, t, q, mine):
        x, y, c, j, chips = _mesh_pos()
        cx, cy = chips[q]
        rows = _own_rows(c if mine else 1 - c, ins[t].shape[1])
        src = (ins if mine else outs)[t].at[2 * cx + cy, rows]
        return pltpu.make_async_remote_copy(src_ref=src, dst_ref=outs[t].at[2 * cx + cy, rows],
                                            send_sem=ss.at[3 * t + q], recv_sem=rs.at[3 * t + q],
                                            device_id=(x, y, 1 - c), device_id_type=MESH)

    def start(ins, outs, ss, rs):
        for t in range(nk):
            for q in range(3):
                copy(ins, outs, ss, rs, t, q, True).start()

    def wait(ins, outs, ss, rs):
        for t in range(nk):
            for q in range(3):
                copy(ins, outs, ss, rs, t, q, False).wait_recv()
                copy(ins, outs, ss, rs, t, q, True).wait_send()

    return _Exchange(bufs, [True] * nk, [], 3 * nk, start, wait, done)


def _reduce_d2d(gl, done):
    nk = len(gl)

    def copy(ins, outs, ss, rs, t):
        x, y, c, _, _ = _mesh_pos()
        return pltpu.make_async_remote_copy(
            src_ref=ins[t].at[:, _own_rows(1 - c, ins[t].shape[1])], dst_ref=outs[t],
            send_sem=ss.at[t], recv_sem=rs.at[t], device_id=(x, y, 1 - c), device_id_type=MESH)

    def start(ins, outs, ss, rs):
        for t in range(nk):
            copy(ins, outs, ss, rs, t).start()

    def wait(ins, outs, ss, rs):
        for t in range(nk):
            copy(ins, outs, ss, rs, t).wait()

    fresh = [((N_CHIPS, g.shape[1] // 2, g.shape[2]), g.dtype) for g in gl]
    return _Exchange(gl, [False] * nk, fresh, nk, start, wait, done)


def _reduce_ici(ps, done):
    nk = len(ps)

    def copy(ins, outs, ss, rs, t, q, landed):
        x, y, c, j, chips = _mesh_pos()
        cx, cy = chips[q]
        src = outs[t].at[2 * cx + cy] if landed else ins[t].at[2 * cx + cy]
        dst = outs[t].at[2 * cx + cy] if landed else outs[t].at[j]
        return pltpu.make_async_remote_copy(src_ref=src, dst_ref=dst, send_sem=ss.at[3 * t + q], recv_sem=rs.at[3 * t + q],
                                            device_id=(cx, cy, c), device_id_type=MESH)

    def start(ins, outs, ss, rs):
        for t in range(nk):
            for q in range(3):
                copy(ins, outs, ss, rs, t, q, False).start()

    def wait(ins, outs, ss, rs):
        for t in range(nk):
            for q in range(3):
                copy(ins, outs, ss, rs, t, q, True).wait_recv()
                copy(ins, outs, ss, rs, t, q, False).wait_send()

    return _Exchange(ps, [False] * nk, [(p_.shape, p_.dtype) for p_ in ps], 3 * nk, start, wait, done)


def _share_d2d(fs, done):
    nk = len(fs)

    def copy(ins, outs, ss, rs, t, mine):
        x, y, c, _, _ = _mesh_pos()
        rows = _own_rows(c if mine else 1 - c, ins[t].shape[1])
        src = (ins if mine else outs)[t].at[:, rows]
        return pltpu.make_async_remote_copy(src_ref=src, dst_ref=outs[t].at[:, rows], send_sem=ss.at[t], recv_sem=rs.at[t],
                                            device_id=(x, y, 1 - c), device_id_type=MESH)

    def start(ins, outs, ss, rs):
        for t in range(nk):
            copy(ins, outs, ss, rs, t, True).start()

    def wait(ins, outs, ss, rs):
        for t in range(nk):
            copy(ins, outs, ss, rs, t, False).wait_recv()
            copy(ins, outs, ss, rs, t, True).wait_send()

    return _Exchange(fs, [True] * nk, [], nk, start, wait, done)


def _run_exchange(plan, name):
    nin = len(plan.ins)
    out_shape = [_sds(x_.shape, x_.dtype) for x_, al in zip(plan.ins, plan.aliased) if al]
    aliases, k = {}, 0
    for t, al in enumerate(plan.aliased):
        if al:
            aliases[t] = k
            k += 1
    out_shape += [_sds(sh, dt) for sh, dt in plan.fresh]
    nout = len(out_shape)

    def body(*refs):
        ins, outs, sems = refs[:nin], refs[nin:nin + nout], refs[nin + nout:]
        plan.start(ins, outs, *sems)
        plan.wait(ins, outs, *sems)

    outs = pl.pallas_call(
        body, name=name, in_specs=[_ANY] * nin, out_specs=[_ANY] * nout, out_shape=out_shape,
        input_output_aliases=aliases,
        scratch_shapes=[pltpu.SemaphoreType.DMA((plan.n_sems,))] * 2,
    )(*plan.ins)
    plan.done(list(outs))


def _add_sibling_rows(g, r1, c, name):
    ns, R, C = g.shape
    hr = R // 2
    tr = hr if hr <= 512 else _tile_rows(hr, 512)
    nblk = hr // tr

    def body(s_ref, a_ref, b_ref, o_ref):
        o_ref[...] = (a_ref[...] + b_ref[...]).astype(o_ref.dtype)

    blk = (None, tr, C)
    return pl.pallas_call(
        body, name=name,
        grid_spec=pltpu.PrefetchScalarGridSpec(
            num_scalar_prefetch=1, grid=(ns, nblk),
            in_specs=[pl.BlockSpec(blk, lambda s_, i, s: (s_, s[0] * nblk + i, 0)), pl.BlockSpec(blk, lambda s_, i, s: (s_, i, 0))],
            out_specs=pl.BlockSpec(blk, lambda s_, i, s: (s_, i, 0))),
        out_shape=_sds((ns, hr, C), _WIRE),
        compiler_params=_cparams(3 * _nbytes((tr, C), F32)),
    )(jnp.reshape(c, (1,)).astype(jnp.int32), g, r1)


def _add_chip_rows(p_, r2, f, l, j, c, name):
    _, hr, C = p_.shape
    tr = hr if hr <= 512 else _tile_rows(hr, 512)
    nblk = hr // tr

    def body(s_ref, own, a1, a2, a3, f_ref, o_ref):
        v = lambda r: r[...].astype(F32)
        o_ref[...] = ((v(own) + v(a1)) + v(a2)) + v(a3)

    blk = (None, tr, C)
    return pl.pallas_call(
        body, name=name,
        grid_spec=pltpu.PrefetchScalarGridSpec(
            num_scalar_prefetch=1, grid=(nblk,),
            in_specs=[pl.BlockSpec(blk, lambda i, s: (s[0], i, 0))]
            + [pl.BlockSpec(blk, lambda i, s, k=k: ((s[0] + k) % N_CHIPS, i, 0)) for k in (1, 2, 3)]
            + [pl.BlockSpec(memory_space=pl.ANY)],
            out_specs=pl.BlockSpec(blk, lambda i, s: (l, s[1] * nblk + i, 0))),
        out_shape=_sds(f.shape, F32),
        input_output_aliases={5: 0},
        compiler_params=_cparams(6 * _nbytes((tr, C), F32)),
    )(jnp.stack([j, c]).astype(jnp.int32), p_, r2, r2, r2, f)


class _ShardedWeights:
    def __init__(self, a, j, c):
        self.j, self.c = j, c
        shards = dict(w_in=[a["w_in"]], w12=[a["glu_w1"], a["glu_w2"]], w_out=[a["w_out"]], w_ff1=[a["w_ff1"]],
                      w_ff2=[a["w_ff2"]], w_ple_gate=[a["w_ple_gate"]], w_ple_proj=[a["w_ple_proj"]])
        self.bufs = [[_cast_layer_slot(shards[n], l, j, "cast_%s_%d" % (n, l)) for n in BIG] for l in range(N_LAYERS)]
        _run_exchange(_gather_ici(self.bufs[0], lambda o: self._set_bufs(0, o)), "gather_ici_0")
        _run_exchange(_gather_d2d(self.bufs[0], lambda o: self._set_bufs(0, o)), "gather_d2d_0")
        self.raw = None
        self.pending = None
        self.final = [lax.empty((N_LAYERS,) + b.shape[1:], F32) for b in self.bufs[0]]

    def _set_bufs(self, l, outs):
        self.bufs[l] = outs

    def layer(self, l):
        return {n: (b if n in COL_SHARDED else b.reshape(N_CHIPS * b.shape[1], b.shape[2]))
                for n, b in zip(BIG, self.bufs[l])}

    _FIRST, _SECOND = (0, 3), (1, 2, 4, 5, 6)

    def _gather_part(self, nxt, idx):
        def done(outs):
            for i, o in zip(idx, outs):
                self.bufs[nxt][i] = o
        return _gather_ici([self.bufs[nxt][i] for i in idx], done)

    def fwd_hooks(self, l):
        if l + 1 == N_LAYERS:
            return {}
        nxt = l + 1
        return dict(x_ff1=lambda: self._gather_part(nxt, self._FIRST),
                    x_ff2=lambda: self._gather_part(nxt, self._SECOND),
                    x_gate=lambda: _gather_d2d(self.bufs[nxt], lambda o: self._set_bufs(nxt, o)))

    def _sibling_done(self, lyr, gl, got):
        ps = [_add_sibling_rows(g_, r1, self.c, "reduce_add_sibling_%s_%d" % (n, lyr)) for g_, r1, n in zip(gl, got, BIG)]
        self.pending = (lyr, ps)

    def _reduce_part(self, idx):
        lyr, ps = self.pending

        def done(r2):
            for i, r in zip(idx, r2):
                self.final[i] = _add_chip_rows(ps[i], r, self.final[i], lyr, self.j, self.c,
                                               "reduce_add_chips_%s_%d" % (BIG[i], lyr))
        return _reduce_ici([ps[i] for i in idx], done)

    def bwd_hooks(self, l):
        if self.raw is None:
            return {}
        lyr, gl = self.raw
        self.raw = None
        return dict(x_bwd0=lambda: _reduce_d2d(gl, lambda got: self._sibling_done(lyr, gl, got)),
                    x_bwd=lambda: self._reduce_part(self._FIRST),
                    x_bwd2=lambda: self._reduce_part(self._SECOND))

    def grads(self, l, g):
        gl = [g[n] if n in COL_SHARDED else g[n].reshape(N_CHIPS, g[n].shape[0] // N_CHIPS, g[n].shape[1]) for n in BIG]
        self.raw = (l, gl)

    def finish(self, small):
        lyr, gl = self.raw
        _run_exchange(_reduce_d2d(gl, lambda got: self._sibling_done(lyr, gl, got)), "reduce_d2d_%d" % lyr)
        small = _allreduce_small(small, self._reduce_part(self._FIRST + self._SECOND))
        out = []
        _run_exchange(_share_d2d(self.final, out.extend), "share_d2d")
        return dict(zip(BIG, out)), small


def _rows_of(shape):
    return -(-int(np.prod(shape)) // (SUBLANES * LANES)) * SUBLANES


def _pack(d):
    parts = []
    for n in SMALL:
        flat = d[n].reshape(-1)
        parts.append(jnp.pad(flat, (0, _rows_of(flat.shape) * LANES - flat.shape[0])).reshape(-1, LANES))
    return jnp.concatenate(parts, axis=0)


def _unpack(buf, like):
    out, r0 = {}, 0
    for n in SMALL:
        shape = like[n].shape
        size, nr = int(np.prod(shape)), _rows_of(shape)
        piece = lax.optimization_barrier(buf[r0:r0 + nr])
        out[n] = piece.reshape(-1)[:size].reshape(shape)
        r0 += nr
    return out


ARGS = ("x", "p", "positions", "attn_norm_g", "w_in", "gmlp_ln_g", "gmlp_ln_b", "gmlp_ws", "gmlp_bs", "q_norm_g",
        "k_norm_g", "sinks", "ssm_a_re", "ssm_a_im", "ssm_log_dt", "ssm_b_re", "ssm_b_im", "ssm_c_re", "ssm_c_im",
        "ssm_d", "glu_w1", "glu_w2", "mix_out_g", "w_out", "mlp_norm_g", "w_ff1", "w_ff2", "ple_norm_g", "w_ple_gate",
        "w_ple_proj")
WEIGHTS = ARGS[3:]


def kernel(x, p, positions, attn_norm_g, w_in, gmlp_ln_g, gmlp_ln_b, gmlp_ws, gmlp_bs, q_norm_g, k_norm_g, sinks, ssm_a_re, ssm_a_im, ssm_log_dt, ssm_b_re, ssm_b_im, ssm_c_re, ssm_c_im, ssm_d, glu_w1, glu_w2, mix_out_g, w_out, mlp_norm_g, w_ff1, w_ff2, ple_norm_g, w_ple_gate, w_ple_proj, loss_target, m_attn_norm_g, m_w_in, m_gmlp_ln_g, m_gmlp_ln_b, m_gmlp_ws, m_gmlp_bs, m_q_norm_g, m_k_norm_g, m_sinks, m_ssm_a_re, m_ssm_a_im, m_ssm_log_dt, m_ssm_b_re, m_ssm_b_im, m_ssm_c_re, m_ssm_c_im, m_ssm_d, m_glu_w1, m_glu_w2, m_mix_out_g, m_w_out, m_mlp_norm_g, m_w_ff1, m_w_ff2, m_ple_norm_g, m_w_ple_gate, m_w_ple_proj, v_attn_norm_g, v_w_in, v_gmlp_ln_g, v_gmlp_ln_b, v_gmlp_ws, v_gmlp_bs, v_q_norm_g, v_k_norm_g, v_sinks, v_ssm_a_re, v_ssm_a_im, v_ssm_log_dt, v_ssm_b_re, v_ssm_b_im, v_ssm_c_re, v_ssm_c_im, v_ssm_d, v_glu_w1, v_glu_w2, v_mix_out_g, v_w_out, v_mlp_norm_g, v_w_ff1, v_w_ff2, v_ple_norm_g, v_w_ple_gate, v_w_ple_proj):
    a = dict(locals())
    L = a["x"].shape[1]
    nl = N_LAYERS
    c = lax.axis_index("c")
    j = 2 * lax.axis_index("x") + lax.axis_index("y")

    sw = {n: a[n] for n in SMALL}
    sse, gx, _, big_grads, small_sum = _local_step(
        a["x"].reshape(L, D_MODEL), a["p"].reshape(nl, L, PLE_DIM), a["positions"].reshape(L),
        a["loss_target"].reshape(L, D_MODEL), sw, _ShardedWeights(a, j, c))
    loss = lax.psum(sse * (0.5 / D_MODEL), ("x", "y", "c"))
    g12 = big_grads.pop("w12")
    big_grads["glu_w1"], big_grads["glu_w2"] = g12[:, :, :C_WIDTH], g12[:, :, C_WIDTH:]

    small_grads = _unpack(small_sum, sw)

    grads, delta, new_m, new_v = {}, {}, {}, {}
    d_s, m_s, v_s = _adamw(_pack(sw), _pack(small_grads), _pack({n: a["m_" + n] for n in SMALL}),
                           _pack({n: a["v_" + n] for n in SMALL}), "adamw_small")
    grads.update(small_grads)
    delta.update(_unpack(d_s, sw))
    new_m.update(_unpack(m_s, sw))
    new_v.update(_unpack(v_s, sw))
    for n, g in big_grads.items():
        shp = a[n].shape
        two_d = lambda t: t.reshape(shp[0] * shp[1], shp[2])
        d, m, v = _adamw(two_d(a[n]), two_d(g), two_d(a["m_" + n]), two_d(a["v_" + n]), "adamw_" + n)
        grads[n], delta[n], new_m[n], new_v[n] = g, d.reshape(shp), m.reshape(shp), v.reshape(shp)

    return (loss, gx.reshape(1, L, D_MODEL), *[grads[n] for n in WEIGHTS], *[delta[n] for n in WEIGHTS],
            *[new_m[n] for n in WEIGHTS], *[new_v[n] for n in WEIGHTS])
```

```python
import functools
import math

import numpy as np
import jax
import jax.numpy as jnp
from jax import lax
from jax.experimental import pallas as pl
from jax.experimental.pallas import tpu as pltpu

F32 = jnp.float32
_MXU = jnp.bfloat16
_ACT = jnp.bfloat16
_WIRE = jnp.bfloat16

D_MODEL = 1024
HEAD_DIM = 64
A_HEADS = 4
CHUNK = 128
B_Q_HEADS = 8
B_KV_HEADS = 2
B_GROUP = 4
WINDOW = 128
ROPE_THETA = 10000.0
C_WIDTH = 256
C_GROUP = 16
C_GROUPS = 16
C_STATE = 64
N_STATE = C_GROUPS * C_STATE
IN_A, IN_Q, IN_KV, IN_C = 512, 512, 128, 256
IN_COLS = 1536
D_FF = 4096
PLE_DIM = 256
EPS = 1e-6
NEG = -1e30
ADAM_LR, ADAM_B1, ADAM_B2, ADAM_EPS, ADAM_WD, ADAM_STEP = 0.001, 0.9, 0.999, 1e-08, 0.01, 10

LANES = 128
SUBLANES = 8
VMEM_BYTES = 64 * 2 ** 20
N_CHIPS = 4
MESH = pl.DeviceIdType.MESH


_MM_VMEM_BUDGET = 50 * 2 ** 20
_EPI_ROWS = 256


def _vmem_limit(est_bytes):
    return int(min(max(2 * est_bytes + (8 << 20), 32 << 20), VMEM_BYTES - (6 << 20)))


def _cparams(est_bytes, **kw):
    return pltpu.CompilerParams(vmem_limit_bytes=_vmem_limit(est_bytes), **kw)


def _sds(shape, dtype):
    return pltpu.HBM(tuple(shape), dtype)


def _hbm(x):
    return pltpu.with_memory_space_constraint(x, pltpu.HBM) if x.size >= (1 << 20) else x


def _nbytes(shape, dtype):
    return int(np.prod(shape)) * jnp.dtype(dtype).itemsize


def _tile(dim, pref):
    t = min(dim, pref)
    while dim % t:
        t -= LANES
    assert t > 0, (dim, pref)
    return t


def _lane(shape):
    return lax.broadcasted_iota(jnp.int32, shape, len(shape) - 1)


def _row(shape):
    return lax.broadcasted_iota(jnp.int32, shape, len(shape) - 2)


def _gelu(x):
    c = math.sqrt(2.0 / math.pi)
    return 0.5 * x * (1.0 + jnp.tanh(c * (x + 0.044715 * (x * x * x))))


def _gelu_grad(x):
    c = math.sqrt(2.0 / math.pi)
    t = jnp.tanh(c * (x + 0.044715 * (x * x * x)))
    return 0.5 * (1.0 + t) + 0.5 * x * (1.0 - t * t) * (c * (1.0 + 3.0 * 0.044715 * (x * x)))


def _sigmoid(x):
    return 1.0 / (1.0 + jnp.exp(-x))


def _dot(a, b, dims=(((1,), (0,)), ((), ()))):
    return lax.dot_general(a.astype(_MXU), b.astype(_MXU), dims, preferred_element_type=F32)


_NT = (((1,), (1,)), ((), ()))
_TN = (((0,), (0,)), ((), ()))
_NN = (((1,), (0,)), ((), ()))


def _mm(a, b, *, mode, M, N, K, out_dtypes, name, epi=None, extras=(), b_cb=False, o_cb=False,
        a_off=0, b_off=0, tm=1024, tn=1024, tk=2048, a_lyr=None, b_lyr=None, o_stack=None, n_acc=0, comm=None):
    if isinstance(a, tuple):
        a, a_lyr = a
    if isinstance(b, tuple):
        b, b_lyr = b
    if b_cb or o_cb:
        nc = (b.shape[-1] if b_cb else N // N_CHIPS)
    tn_nom = nc if ((mode == "nn" and b_cb) or (mode == "tn" and o_cb)) else _tile(N, tn)
    tk_nom = nc if (mode == "nt" and b_cb) else _tile(K, tk)
    item = lambda d: jnp.dtype(d).itemsize
    per_row = tk_nom * item(a.dtype) + tn_nom * (sum(item(d) for d in out_dtypes)
                                                   + sum(item(e.dtype) for e, _ in extras if e.shape[0] > 1))
    fixed = tk_nom * tn_nom * item(b.dtype)
    tm = _tile(M, tm)
    while tm > 256 and M % (tm // 2) == 0 and 2 * (tm * per_row + fixed) + 8 * tm * tn_nom > _MM_VMEM_BUDGET:
        tm //= 2

    def spec(block, imap, lyr=None):
        if lyr is None:
            return pl.BlockSpec(block, imap)
        return pl.BlockSpec((None,) + block, lambda i, j, k: (lyr,) + imap(i, j, k))

    if mode == "nn":
        if b_cb:
            tn = nc
        tm, tn, tk = _tile(M, tm), _tile(N, tn), _tile(K, tk)
        a_spec = spec((tm, tk), lambda i, j, k: (i, k + a_off), a_lyr)
        if b_cb:
            b_spec = spec((None, tk, tn), lambda i, j, k: (j, k, 0), b_lyr)
        else:
            b_spec = spec((tk, tn), lambda i, j, k: (k, j + b_off), b_lyr)
        dims = _NN
        a_blk, b_blk = (tm, tk), (tk, tn)
    elif mode == "nt":
        if b_cb:
            tk = nc
        tm, tn, tk = _tile(M, tm), _tile(N, tn), _tile(K, tk)
        a_spec = spec((tm, tk), lambda i, j, k: (i, k + a_off), a_lyr)
        if b_cb:
            b_spec = spec((None, tn, tk), lambda i, j, k: (k, j, 0), b_lyr)
        else:
            b_spec = spec((tn, tk), lambda i, j, k: (j, k + b_off), b_lyr)
        dims = _NT
        a_blk, b_blk = (tm, tk), (tn, tk)
    else:
        if o_cb:
            tn = nc
        tm, tn, tk = _tile(M, tm), _tile(N, tn), _tile(K, tk)
        a_spec = spec((tk, tm), lambda i, j, k: (k, i + a_off), a_lyr)
        b_spec = spec((tk, tn), lambda i, j, k: (k, j + b_off), b_lyr)
        dims = _TN
        a_blk, b_blk = (tk, tm), (tk, tn)
    gi, gj, gk = M // tm, N // tn, K // tk
    o_lyr = None if o_stack is None else o_stack[1]
    if o_cb:
        o_spec = spec((None, tm, tn), lambda i, j, k: (j, i, 0), o_lyr)
        o_shape = (gj, M, tn)
    else:
        o_spec = spec((tm, tn), lambda i, j, k: (i, j), o_lyr)
        o_shape = (M, N)
    e_specs = []
    for e, off in extras:
        if e.shape[0] == 1:
            e_specs.append(pl.BlockSpec((1, tn), lambda i, j, k, off=off: (0, j + off)))
        else:
            e_specs.append(pl.BlockSpec((tm, tn), lambda i, j, k, off=off: (i, j + off)))
    extras = [e for e, _ in extras]
    ne, no = len(extras), len(out_dtypes)
    operands = [_hbm(t) for t in (a, b, *extras)]
    in_specs = [a_spec, b_spec] + e_specs
    out_shape = [_sds(o_shape, d) for d in out_dtypes]
    aliases = {}
    if o_stack is not None:
        assert no == 1 and o_stack[0].shape[1:] == o_shape and o_stack[0].dtype == out_dtypes[0]
        operands.append(_hbm(o_stack[0]))
        in_specs.append(pl.BlockSpec(memory_space=pl.ANY))
        out_shape = [_sds(o_stack[0].shape, o_stack[0].dtype)]
        aliases = {len(operands) - 1: 0}
    out_specs = [o_spec] * no
    if n_acc:
        assert gj == 1 and o_stack is None
        out_specs[no - n_acc:] = [pl.BlockSpec((1, tn), lambda i, j, k: (0, 0))] * n_acc
        out_shape[no - n_acc:] = [_sds((1, N), d) for d in out_dtypes[no - n_acc:]]
    nx_in = nx_out = 0
    if comm is not None:
        nx_in, ncin0 = len(comm.ins), len(operands)
        operands += list(comm.ins)
        in_specs += [pl.BlockSpec(memory_space=pl.ANY)] * nx_in
        for t, x_ in enumerate(comm.ins):
            if comm.aliased[t]:
                aliases[ncin0 + t] = len(out_shape)
                out_shape.append(_sds(x_.shape, x_.dtype))
        out_shape += [_sds(sh, dt) for sh, dt in comm.fresh]
        nx_out = len(out_shape) - no
        out_specs += [pl.BlockSpec(memory_space=pl.ANY)] * nx_out
    nin = len(operands)

    def body(*refs):
        a_ref, b_ref = refs[0], refs[1]
        e_refs = refs[2:2 + ne]
        o_refs = refs[nin:nin + no]
        first_rows = pl.program_id(0) == 0
        if comm is not None:
            x_ins = refs[nin - nx_in:nin]
            x_outs = refs[nin + no:nin + no + nx_out]
            sems = refs[nin + no + nx_out:nin + no + nx_out + 2]
            pid = [pl.program_id(d) for d in range(3)]

            @pl.when((pid[0] == 0) & (pid[1] == 0) & (pid[2] == 0))
            def _():
                comm.start(x_ins, x_outs, *sems)

        def fin(acc):
            for t in range(no - n_acc, no):
                @pl.when(first_rows)
                def _():
                    o_refs[t][...] = jnp.zeros_like(o_refs[t])

            rc = _EPI_ROWS if (epi is not None and tm % _EPI_ROWS == 0) else tm
            for c0 in range(0, tm, rc):
                rows = slice(c0, c0 + rc)
                ex = [e[...] if e.shape[0] == 1 else e[rows, :] for e in e_refs]
                vals = epi(acc[rows, :], *ex) if epi is not None else (acc[rows, :],)
                for t, (o, v) in enumerate(zip(o_refs, vals)):
                    if t < no - n_acc:
                        o[rows, :] = v.astype(o.dtype)
                    else:
                        o[...] += v.astype(o.dtype)

        prod = _dot(a_ref[...], b_ref[...], dims)
        if gk == 1:
            fin(prod)
        else:
            acc_ref = refs[-1]
            k = pl.program_id(2)

            @pl.when(k == 0)
            def _():
                acc_ref[...] = prod

            @pl.when(k > 0)
            def _():
                acc_ref[...] += prod

            @pl.when(k == gk - 1)
            def _():
                fin(acc_ref)

        if comm is not None:
            @pl.when((pid[0] == gi - 1) & (pid[1] == gj - 1) & (pid[2] == gk - 1))
            def _():
                comm.wait(x_ins, x_outs, *sems)

    est = (_nbytes(a_blk, a.dtype) + _nbytes(b_blk, b.dtype)
           + sum(_nbytes((tm, tn), d) for d in out_dtypes)
           + sum(_nbytes((tm, tn), e.dtype) for e in extras)) + 2 * _nbytes((tm, tn), F32)
    sem_scratch = [pltpu.SemaphoreType.DMA((comm.n_sems,))] * 2 if comm is not None else []
    row_sem = "arbitrary" if (n_acc or comm is not None) else "parallel"
    outs = pl.pallas_call(
        body, name=name, grid=(gi, gj, gk),
        in_specs=in_specs,
        out_specs=out_specs,
        out_shape=out_shape,
        scratch_shapes=sem_scratch + ([pltpu.VMEM((tm, tn), F32)] if gk > 1 else []),
        input_output_aliases=aliases,
        compiler_params=_cparams(est, dimension_semantics=(row_sem, "arbitrary" if comm is not None else "parallel",
                                                           "arbitrary")),
    )(*operands)
    if comm is not None:
        main = outs[:no]
        return (main if no > 1 else main[0]), list(outs[no:])
    return outs if no > 1 else outs[0]


_TL = 512


def _rms_fwd(h, g, name):
    L, D = h.shape
    tl = _tile(L, _TL)

    def body(h_ref, g_ref, o_ref):
        x = h_ref[...]
        r = lax.rsqrt(jnp.mean(x * x, axis=-1, keepdims=True) + EPS)
        o_ref[...] = ((x * r) * g_ref[...]).astype(o_ref.dtype)

    return pl.pallas_call(
        body, name=name, grid=(L // tl,),
        in_specs=[pl.BlockSpec((tl, D), lambda i: (i, 0)), pl.BlockSpec((1, D), lambda i: (0, 0))],
        out_specs=pl.BlockSpec((tl, D), lambda i: (i, 0)),
        out_shape=_sds((L, D), _ACT),
        compiler_params=_cparams(3 * _nbytes((tl, D), F32)),
    )(h, g.reshape(1, D))


def _rms_bwd(dxn, h, g, dres, name):
    L, D = h.shape
    tl = _tile(L, _TL)

    def body(d_ref, h_ref, g_ref, r_ref, o_ref, dg_ref):
        x = h_ref[...]
        r = lax.rsqrt(jnp.mean(x * x, axis=-1, keepdims=True) + EPS)
        xhat = x * r
        d = d_ref[...].astype(F32)
        gy = d * g_ref[...]
        dx = r * (gy - xhat * jnp.mean(gy * xhat, axis=-1, keepdims=True))
        o_ref[...] = r_ref[...] + dx

        @pl.when(pl.program_id(0) == 0)
        def _():
            dg_ref[...] = jnp.zeros_like(dg_ref)

        dg_ref[...] += jnp.sum(d * xhat, axis=0, keepdims=True)

    dh, dg = pl.pallas_call(
        body, name=name, grid=(L // tl,),
        in_specs=[pl.BlockSpec((tl, D), lambda i: (i, 0)), pl.BlockSpec((tl, D), lambda i: (i, 0)),
                  pl.BlockSpec((1, D), lambda i: (0, 0)), pl.BlockSpec((tl, D), lambda i: (i, 0))],
        out_specs=[pl.BlockSpec((tl, D), lambda i: (i, 0)), pl.BlockSpec((1, D), lambda i: (0, 0))],
        out_shape=[_sds((L, D), F32), _sds((1, D), F32)],
        compiler_params=_cparams(5 * _nbytes((tl, D), F32)),
    )(dxn, h, g.reshape(1, D), dres)
    return dh, dg.reshape(D)


def _rope_tables(positions):
    L = positions.shape[0]
    tl = _tile(L, 1024)
    inv = 1.0 / (ROPE_THETA ** (np.arange(0, HEAD_DIM, 2, dtype=np.float32) / HEAD_DIM))
    inv128 = jnp.asarray(np.tile(inv.astype(np.float32), 4).reshape(1, LANES))

    def body(p_ref, i_ref, c_ref, s_ref):
        ang = p_ref[...].astype(F32) * i_ref[...]
        c_ref[...] = jnp.cos(ang)
        s_ref[...] = jnp.sin(ang)

    return pl.pallas_call(
        body, name="rope_tables", grid=(L // tl,),
        in_specs=[pl.BlockSpec((tl, 1), lambda i: (i, 0)), pl.BlockSpec((1, LANES), lambda i: (0, 0))],
        out_specs=[pl.BlockSpec((tl, LANES), lambda i: (i, 0))] * 2,
        out_shape=[_sds((L, LANES), F32)] * 2,
    )(positions.reshape(L, 1), inv128)


_GM_TL = 256


def _gmlp_head(Z, W, bfull, lg, lb, maskv):
    G = _gelu(Z)
    mu = jnp.sum(jnp.where(maskv, G, 0.0), axis=-1, keepdims=True) * (1.0 / HEAD_DIM)
    xc = jnp.where(maskv, G - mu, 0.0)
    var = jnp.sum(xc * xc, axis=-1, keepdims=True) * (1.0 / HEAD_DIM)
    rstd = lax.rsqrt(var + EPS)
    xhat = xc * rstd
    vn = xhat * lg + lb
    sv = _dot(W, vn) + bfull
    return G, xhat, rstd, vn, sv


def _tril(W):
    return jnp.where(_row(W.shape) >= _lane(W.shape), W, 0.0)


def _triu(W):
    return jnp.where(_row(W.shape) <= _lane(W.shape), W, 0.0)


def _gmlp_fwd(z, ws, bfull, lgf, lbf, name):
    L = z.shape[0]
    tl = _tile(L, _GM_TL)
    nch = tl // CHUNK

    def body(z_ref, w_ref, b_ref, lg_ref, lb_ref, o_ref):
        maskv = _lane((CHUNK, LANES)) >= HEAD_DIM
        for c in range(nch):
            rows = slice(c * CHUNK, (c + 1) * CHUNK)
            for hp in range(A_HEADS // 2):
                acc = None
                for hh in range(2):
                    h = 2 * hp + hh
                    Z = z_ref[rows, h * LANES:(h + 1) * LANES]
                    G, _, _, _, sv = _gmlp_head(Z, _tril(w_ref[h]), b_ref[h], lg_ref[h:h + 1, :], lb_ref[h:h + 1, :], maskv)
                    prod = G * pltpu.roll(sv, HEAD_DIM, axis=1)
                    acc = prod if hh == 0 else acc + pltpu.roll(prod, HEAD_DIM, axis=1)
                o_ref[rows, hp * LANES:(hp + 1) * LANES] = acc

    return pl.pallas_call(
        body, name=name, grid=(L // tl,),
        in_specs=[pl.BlockSpec((tl, IN_A), lambda i: (i, 0)),
                  pl.BlockSpec((A_HEADS, CHUNK, CHUNK), lambda i: (0, 0, 0)),
                  pl.BlockSpec((A_HEADS, CHUNK, LANES), lambda i: (0, 0, 0)),
                  pl.BlockSpec((A_HEADS, LANES), lambda i: (0, 0)),
                  pl.BlockSpec((A_HEADS, LANES), lambda i: (0, 0))],
        out_specs=pl.BlockSpec((tl, 2 * LANES), lambda i: (i, 0)),
        out_shape=_sds((L, 2 * LANES), F32),
    )(z, ws, bfull, lgf, lbf)


def _gmlp_bwd(z, dya, ws, wsT, bfull, lgf, lbf, name):
    L = z.shape[0]
    tl = _tile(L, _GM_TL)
    nch = tl // CHUNK
    nsteps = L // tl

    def body(z_ref, d_ref, w_ref, wt_ref, b_ref, lg_ref, lb_ref, dz_ref, dw_ref, db_ref, dlg_ref, dlb_ref):
        step = pl.program_id(0)

        @pl.when(step == 0)
        def _():
            dw_ref[...] = jnp.zeros_like(dw_ref)
            db_ref[...] = jnp.zeros_like(db_ref)
            dlg_ref[...] = jnp.zeros_like(dlg_ref)
            dlb_ref[...] = jnp.zeros_like(dlb_ref)

        lane = _lane((CHUNK, LANES))
        maskv = lane >= HEAD_DIM
        for c in range(nch):
            rows = slice(c * CHUNK, (c + 1) * CHUNK)
            for h in range(A_HEADS):
                hp, hh = divmod(h, 2)
                Z = z_ref[rows, h * LANES:(h + 1) * LANES]
                lg = lg_ref[h:h + 1, :]
                G, xhat, rstd, vn, sv = _gmlp_head(Z, _tril(w_ref[h]), b_ref[h], lg, lb_ref[h:h + 1, :], maskv)
                dpair = d_ref[rows, hp * LANES:(hp + 1) * LANES]
                if hh == 1:
                    dpair = pltpu.roll(dpair, HEAD_DIM, axis=1)
                dout = jnp.where(maskv, 0.0, dpair)
                du = dout * pltpu.roll(sv, HEAD_DIM, axis=1)
                dsv = pltpu.roll(dout * G, HEAD_DIM, axis=1)
                dw_ref[h] += _tril(_dot(dsv, vn, _NT))
                db_ref[h] += dsv
                dvn = _dot(_triu(wt_ref[h]), dsv)
                dlg_ref[h] += dvn * xhat
                dlb_ref[h] += dvn
                dxh = dvn * lg
                m1 = jnp.sum(dxh, axis=-1, keepdims=True) * (1.0 / HEAD_DIM)
                m2 = jnp.sum(dxh * xhat, axis=-1, keepdims=True) * (1.0 / HEAD_DIM)
                dv = jnp.where(maskv, rstd * (dxh - m1 - xhat * m2), 0.0)
                dz_ref[rows, h * LANES:(h + 1) * LANES] = ((du + dv) * _gelu_grad(Z)).astype(dz_ref.dtype)

        @pl.when(step == nsteps - 1)
        def _():
            for h in range(A_HEADS):
                db_ref[h] = jnp.broadcast_to(jnp.sum(db_ref[h], axis=1, keepdims=True), (CHUNK, LANES))
                dlg_ref[h] = jnp.broadcast_to(jnp.sum(dlg_ref[h], axis=0, keepdims=True), (CHUNK, LANES))
                dlb_ref[h] = jnp.broadcast_to(jnp.sum(dlb_ref[h], axis=0, keepdims=True), (CHUNK, LANES))

    full3 = pl.BlockSpec((A_HEADS, CHUNK, LANES), lambda i: (0, 0, 0))
    return pl.pallas_call(
        body, name=name, grid=(nsteps,),
        in_specs=[pl.BlockSpec((tl, IN_A), lambda i: (i, 0)),
                  pl.BlockSpec((tl, 2 * LANES), lambda i: (i, 0)),
                  full3, full3, full3,
                  pl.BlockSpec((A_HEADS, LANES), lambda i: (0, 0)),
                  pl.BlockSpec((A_HEADS, LANES), lambda i: (0, 0))],
        out_specs=[pl.BlockSpec((tl, IN_A), lambda i: (i, 0)), full3, full3, full3, full3],
        out_shape=[_sds((L, IN_A), _ACT)] + [_sds((A_HEADS, CHUNK, LANES), F32)] * 4,
    )(z, dya, ws, wsT, bfull, lgf, lbf)


def _head_rstd(x, lo):
    sq = x * x
    s_lo = jnp.sum(jnp.where(lo, sq, 0.0), axis=-1, keepdims=True)
    s_hi = jnp.sum(jnp.where(lo, 0.0, sq), axis=-1, keepdims=True)
    return jnp.where(lo, lax.rsqrt(s_lo * (1.0 / HEAD_DIM) + EPS), lax.rsqrt(s_hi * (1.0 / HEAD_DIM) + EPS))


def _rot_half(x, first):
    return jnp.where(first, -pltpu.roll(x, LANES - HEAD_DIM // 2, axis=1), pltpu.roll(x, HEAD_DIM // 2, axis=1))


def _qk_prep(z, cos, sin, gq, gk, name):
    L = z.shape[0]
    tl = _tile(L, _TL)
    nq = IN_Q // LANES

    def body(q_ref, k_ref, c_ref, s_ref, gq_ref, gk_ref, qo_ref, ko_ref):
        lane = _lane((tl, LANES))
        lo = lane < HEAD_DIM
        first = (lane % HEAD_DIM) < (HEAD_DIM // 2)
        c, s = c_ref[...], s_ref[...]

        def prep(x, g):
            xn = (x * _head_rstd(x, lo)) * g
            return xn * c + _rot_half(xn, first) * s

        for j in range(nq):
            qo_ref[:, j * LANES:(j + 1) * LANES] = prep(q_ref[:, j * LANES:(j + 1) * LANES], gq_ref[...]).astype(qo_ref.dtype)
        ko_ref[...] = prep(k_ref[...], gk_ref[...]).astype(ko_ref.dtype)

    return pl.pallas_call(
        body, name=name, grid=(L // tl,),
        in_specs=[pl.BlockSpec((tl, IN_Q), lambda i: (i, 1)),
                  pl.BlockSpec((tl, IN_KV), lambda i: (i, 8)),
                  pl.BlockSpec((tl, LANES), lambda i: (i, 0)), pl.BlockSpec((tl, LANES), lambda i: (i, 0)),
                  pl.BlockSpec((1, LANES), lambda i: (0, 0)), pl.BlockSpec((1, LANES), lambda i: (0, 0))],
        out_specs=[pl.BlockSpec((tl, IN_Q), lambda i: (i, 0)), pl.BlockSpec((tl, IN_KV), lambda i: (i, 0))],
        out_shape=[_sds((L, IN_Q), _ACT), _sds((L, IN_KV), _ACT)],
    )(z, z, cos, sin, gq, gk)


def _qk_prep_bwd(z, dq, dkc, dkp, dvc, dvp, cos, sin, gq, gk, name):
    L = z.shape[0]
    tl = _ATT_QB * WINDOW
    nb = L // tl
    nq = IN_Q // LANES

    def body(q_ref, k_ref, dq_ref, dkc_ref, dkp_ref, dvc_ref, dvp_ref, c_ref, s_ref, gq_ref, gk_ref,
             dzq_ref, dzk_ref, dzv_ref, dgq_ref, dgk_ref):
        n = pl.program_id(0)

        @pl.when(n == 0)
        def _():
            dgq_ref[...] = jnp.zeros_like(dgq_ref)
            dgk_ref[...] = jnp.zeros_like(dgk_ref)

        lane = _lane((tl, LANES))
        lo = lane < HEAD_DIM
        first = (lane % HEAD_DIM) < (HEAD_DIM // 2)
        c, s = c_ref[...], s_ref[...]
        has_next = jnp.where(n < nb - 1, 1.0, 0.0)

        def bwd(x, g, dy):
            r = _head_rstd(x, lo)
            xhat = x * r
            dxn = dy * c - _rot_half(dy * s, first)
            gy = dxn * g
            t = gy * xhat
            m_lo = jnp.sum(jnp.where(lo, t, 0.0), axis=-1, keepdims=True)
            m_hi = jnp.sum(jnp.where(lo, 0.0, t), axis=-1, keepdims=True)
            m = jnp.where(lo, m_lo, m_hi) * (1.0 / HEAD_DIM)
            dx = r * (gy - xhat * m)
            dg = jnp.sum(dxn * xhat, axis=0, keepdims=True)
            return dx, dg

        dgq = jnp.zeros((1, LANES), F32)
        for j in range(nq):
            sl = slice(j * LANES, (j + 1) * LANES)
            dx, dg = bwd(q_ref[:, sl], gq_ref[...], dq_ref[:, sl].astype(F32))
            dzq_ref[:, sl] = dx.astype(dzq_ref.dtype)
            dgq = dgq + dg
        dgq_ref[...] += dgq + pltpu.roll(dgq, HEAD_DIM, axis=1)
        def with_next(cur_ref, nxt_ref):
            head = jnp.zeros((tl - WINDOW, IN_KV), F32)
            return cur_ref[...] + jnp.concatenate([head, has_next * nxt_ref[...]], axis=0)

        dx, dg = bwd(k_ref[...], gk_ref[...], with_next(dkc_ref, dkp_ref))
        dzk_ref[...] = dx.astype(dzk_ref.dtype)
        dgk_ref[...] += dg + pltpu.roll(dg, HEAD_DIM, axis=1)
        dzv_ref[...] = with_next(dvc_ref, dvp_ref).astype(dzv_ref.dtype)

    nxt = lambda i: (jnp.minimum(i + 1, nb - 1), 0)
    cur = lambda i: (i, 0)
    kv = pl.BlockSpec((tl, IN_KV), cur)
    kvn = pl.BlockSpec((WINDOW, IN_KV), nxt)
    one = pl.BlockSpec((1, LANES), lambda i: (0, 0))
    return pl.pallas_call(
        body, name=name, grid=(nb,),
        in_specs=[pl.BlockSpec((tl, IN_Q), lambda i: (i, 1)), pl.BlockSpec((tl, IN_KV), lambda i: (i, 8)),
                  pl.BlockSpec((tl, IN_Q), cur), kv, kvn, kv, kvn,
                  kv, kv, one, one],
        out_specs=[pl.BlockSpec((tl, IN_Q), cur), kv, kv, one, one],
        out_shape=[_sds((L, IN_Q), _ACT), _sds((L, IN_KV), _ACT),
                   _sds((L, IN_KV), _ACT), _sds((1, LANES), F32),
                   _sds((1, LANES), F32)],
    )(z, z, dq, dkc, dkp, dvc, dvp, cos, sin, gq, gk)


def _attn_mask(n):
    shp = (2 * WINDOW, B_GROUP * WINDOW)
    qi = _lane(shp) % WINDOW
    kj = _row(shp)
    off = 0 if n is None else jnp.where(n > 0, 0, 4 * WINDOW)
    return ((kj >= WINDOW) & (kj - WINDOW <= qi)) | ((kj < WINDOW) & (kj > qi + off))


def _kv_lanes(j):
    lane = _lane((WINDOW, LANES))
    return (lane >= j * HEAD_DIM) & (lane < (j + 1) * HEAD_DIM)


_ATT_QB = 4


def _stack_heads(ref, rows, j, kvl):
    parts = []
    for g in range(B_GROUP):
        h = j * B_GROUP + g
        slab = ref[rows, (h // 2) * LANES:(h // 2 + 1) * LANES].astype(F32)
        if (h % 2) != j:
            slab = pltpu.roll(slab, HEAD_DIM, axis=1)
        parts.append(jnp.where(kvl, slab, 0.0))
    return jnp.concatenate(parts, axis=0)


def _attn_probs(qs, k2, sink_row, mask):
    s = _dot(k2, qs, _NT) * (HEAD_DIM ** -0.5)
    s = jnp.where(mask, s, NEG)
    m = jnp.maximum(jnp.max(s, axis=0, keepdims=True), sink_row)
    p = jnp.exp(s - m)
    esink = jnp.exp(sink_row - m)
    inv = 1.0 / (jnp.sum(p, axis=0, keepdims=True) + esink)
    return p * inv, esink * inv


def _sink_row(sink_ref, j):
    lane = _lane((1, B_GROUP * WINDOW))
    row = jnp.full((1, B_GROUP * WINDOW), sink_ref[j * B_GROUP], F32)
    for g in range(1, B_GROUP):
        row = jnp.where(lane >= g * WINDOW, sink_ref[j * B_GROUP + g], row)
    return row


def _attn_fwd(q, k, z, sinks, name):
    L = q.shape[0]
    QB = _ATT_QB
    tq = QB * WINDOW
    prev = lambda n: (jnp.maximum(QB * n - 1, 0), 0)
    prev_v = lambda n: (jnp.maximum(QB * n - 1, 0), 9)

    def body(s_ref, q_ref, kp_ref, kc_ref, vp_ref, vc_ref, o_ref):
        n = pl.program_id(0)
        k3 = jnp.concatenate([kp_ref[...], kc_ref[...]], axis=0)
        v3 = jnp.concatenate([vp_ref[...], vc_ref[...]], axis=0)
        for b in range(QB):
            rows = slice(b * WINDOW, (b + 1) * WINDOW)
            mask = _attn_mask(n if b == 0 else None)
            k2 = k3[b * WINDOW:(b + 2) * WINDOW]
            v2 = v3[b * WINDOW:(b + 2) * WINDOW]
            slabs = [None] * (IN_Q // LANES)
            for j in range(B_KV_HEADS):
                kvl = _kv_lanes(j)
                qs = _stack_heads(q_ref, rows, j, kvl)
                pn, _ = _attn_probs(qs, k2, _sink_row(s_ref, j), mask)
                o = _dot(pn, v2, _TN)
                for g in range(B_GROUP):
                    h = j * B_GROUP + g
                    piece = jnp.where(kvl, o[g * WINDOW:(g + 1) * WINDOW], 0.0)
                    if (h % 2) != j:
                        piece = pltpu.roll(piece, HEAD_DIM, axis=1)
                    slabs[h // 2] = piece if slabs[h // 2] is None else slabs[h // 2] + piece
            for t, sl in enumerate(slabs):
                o_ref[rows, t * LANES:(t + 1) * LANES] = sl

    return pl.pallas_call(
        body, name=name, grid=(L // tq,),
        in_specs=[pl.BlockSpec(memory_space=pltpu.SMEM),
                  pl.BlockSpec((tq, IN_Q), lambda n: (n, 0)),
                  pl.BlockSpec((WINDOW, IN_KV), prev), pl.BlockSpec((tq, IN_KV), lambda n: (n, 0)),
                  pl.BlockSpec((WINDOW, IN_KV), prev_v), pl.BlockSpec((tq, IN_KV), lambda n: (n, 9))],
        out_specs=pl.BlockSpec((tq, IN_Q), lambda n: (n, 0)),
        out_shape=_sds((L, IN_Q), F32),
    )(sinks, q, k, k, z, z)


def _attn_bwd(q, k, z, sinks, dyb, name):
    L = q.shape[0]
    QB = _ATT_QB
    tq = QB * WINDOW
    nsteps = L // tq
    prev = lambda n: (jnp.maximum(QB * n - 1, 0), 0)
    prev_v = lambda n: (jnp.maximum(QB * n - 1, 0), 9)
    cur = lambda n: (n, 0)

    def body(s_ref, q_ref, kp_ref, kc_ref, vp_ref, vc_ref, d_ref, dq_ref, dkc_ref, dkp_ref, dvc_ref, dvp_ref, ds_ref):
        n = pl.program_id(0)

        @pl.when(n == 0)
        def _():
            ds_ref[...] = jnp.zeros_like(ds_ref)

        k3 = jnp.concatenate([kp_ref[...], kc_ref[...]], axis=0)
        v3 = jnp.concatenate([vp_ref[...], vc_ref[...]], axis=0)
        dkb = [None] * (QB + 1)
        dvb = [None] * (QB + 1)
        dsink = jnp.zeros((1, LANES), F32)
        lane1 = _lane((1, LANES))
        add = lambda acc, v: v if acc is None else acc + v
        for b in range(QB):
            rows = slice(b * WINDOW, (b + 1) * WINDOW)
            mask = _attn_mask(n if b == 0 else None)
            k2 = k3[b * WINDOW:(b + 2) * WINDOW]
            v2 = v3[b * WINDOW:(b + 2) * WINDOW]
            slabs = [None] * (IN_Q // LANES)
            for j in range(B_KV_HEADS):
                kvl = _kv_lanes(j)
                qs = _stack_heads(q_ref, rows, j, kvl)
                dos = _stack_heads(d_ref, rows, j, kvl)
                pn, psink = _attn_probs(qs, k2, _sink_row(s_ref, j), mask)
                dp = _dot(v2, dos, _NT)
                dd = jnp.sum(pn * dp, axis=0, keepdims=True)
                dss = (pn * (dp - dd)) * (HEAD_DIM ** -0.5)
                dqs = _dot(dss, k2, _TN)
                dk2 = _dot(dss, qs)
                dv2 = _dot(pn, dos)
                dkb[b], dkb[b + 1] = add(dkb[b], dk2[:WINDOW]), add(dkb[b + 1], dk2[WINDOW:])
                dvb[b], dvb[b + 1] = add(dvb[b], dv2[:WINDOW]), add(dvb[b + 1], dv2[WINDOW:])
                sd = psink * dd
                for g in range(B_GROUP):
                    h = j * B_GROUP + g
                    piece = jnp.where(kvl, dqs[g * WINDOW:(g + 1) * WINDOW], 0.0)
                    if (h % 2) != j:
                        piece = pltpu.roll(piece, HEAD_DIM, axis=1)
                    slabs[h // 2] = piece if slabs[h // 2] is None else slabs[h // 2] + piece
                    tot = jnp.sum(sd[:, g * WINDOW:(g + 1) * WINDOW], axis=1, keepdims=True)
                    dsink = dsink - jnp.where(lane1 == h, tot, 0.0)
            for t, sl in enumerate(slabs):
                dq_ref[rows, t * LANES:(t + 1) * LANES] = sl
        dkp_ref[...] = dkb[0]
        dvp_ref[...] = dvb[0]
        for b in range(QB):
            dkc_ref[b * WINDOW:(b + 1) * WINDOW, :] = dkb[b + 1]
            dvc_ref[b * WINDOW:(b + 1) * WINDOW, :] = dvb[b + 1]
        ds_ref[0:1, :] += dsink

    kvs = pl.BlockSpec((tq, IN_KV), cur)
    kvp = pl.BlockSpec((WINDOW, IN_KV), cur)
    kvo = _sds((L, IN_KV), F32)
    kvpo = _sds((nsteps * WINDOW, IN_KV), F32)
    return pl.pallas_call(
        body, name=name, grid=(nsteps,),
        in_specs=[pl.BlockSpec(memory_space=pltpu.SMEM),
                  pl.BlockSpec((tq, IN_Q), cur),
                  pl.BlockSpec((WINDOW, IN_KV), prev), kvs,
                  pl.BlockSpec((WINDOW, IN_KV), prev_v), pl.BlockSpec((tq, IN_KV), lambda n: (n, 9)),
                  pl.BlockSpec((tq, IN_Q), cur)],
        out_specs=[pl.BlockSpec((tq, IN_Q), cur), kvs, kvp, kvs, kvp, pl.BlockSpec((SUBLANES, LANES), lambda n: (0, 0))],
        out_shape=[_sds((L, IN_Q), F32), kvo, kvpo, kvo, kvpo, _sds((SUBLANES, LANES), F32)],
    )(sinks, q, k, k, z, z, dyb)


def _ssm_disc(are, aim, ldt, bre, bim):
    dt = jnp.exp(ldt)
    mag = jnp.exp(are * dt)
    lr, li = mag * jnp.cos(aim * dt), mag * jnp.sin(aim * dt)
    den = are * are + aim * aim
    xr, xi = lr - 1.0, li
    cr, ci = (xr * are + xi * aim) / den, (xi * are - xr * aim) / den
    return lr, li, cr * bre - ci * bim, cr * bim + ci * bre


def _ssm_prep(are, aim, ldt, bre, bim):
    shp3, shpb = are.shape, bre.shape

    def body(are_ref, aim_ref, ldt_ref, bre_ref, bim_ref, lr_ref, li_ref, br_ref, bi_ref):
        lr, li, br, bi = _ssm_disc(are_ref[...], aim_ref[...], ldt_ref[...], bre_ref[...], bim_ref[...])
        lr_ref[...] = lr
        li_ref[...] = li
        br_ref[...] = br
        bi_ref[...] = bi

    return pl.pallas_call(
        body, name="ssm_prep",
        out_shape=[_sds(shp3, F32)] * 2 + [_sds(shpb, F32)] * 2,
    )(are, aim, ldt, bre, bim)


def _ssm_prep_bwd(are, aim, ldt, bre, bim, dlr, dli, dbr, dbi):
    shp3, shpb = are.shape, bre.shape

    def body(are_ref, aim_ref, ldt_ref, bre_ref, bim_ref, dlr_ref, dli_ref, dbr_ref, dbi_ref,
             o_are, o_aim, o_ldt, o_bre, o_bim):
        _, vjp = jax.vjp(_ssm_disc, are_ref[...], aim_ref[...], ldt_ref[...], bre_ref[...], bim_ref[...])
        g = vjp((dlr_ref[...], dli_ref[...], dbr_ref[...], dbi_ref[...]))
        o_are[...] = g[0]
        o_aim[...] = g[1]
        o_ldt[...] = jnp.broadcast_to(jnp.sum(g[2], axis=-1, keepdims=True), shp3)
        o_bre[...] = g[3]
        o_bim[...] = g[4]

    return pl.pallas_call(
        body, name="ssm_prep_bwd",
        out_shape=[_sds(shp3, F32)] * 3 + [_sds(shpb, F32)] * 2,
    )(are, aim, ldt, bre, bim, dlr, dli, dbr, dbi)


_SCAN_TB = 512
_SCAN_W = 512


def _cmul(ar, ai, br, bi):
    return ar * br - ai * bi, ar * bi + ai * br


def _ssm_scan(x, lam_r, lam_i, name, reverse=False, states=None):
    L = x.shape[0]
    tb = _tile(L, _SCAN_TB)
    nrb = L // tb
    nt = tb // SUBLANES
    W = _SCAN_W
    with_da = states is not None

    def body(*refs):
        if with_da:
            xr_ref, xi_ref, sr_ref, si_ref, ar_ref, ai_ref, o_ref, dar_ref, dai_ref, cr_ref, ci_ref = refs
        else:
            xr_ref, xi_ref, ar_ref, ai_ref, o_ref, cr_ref, ci_ref = refs
        step = pl.program_id(0)

        @pl.when(step == 0)
        def _():
            cr_ref[...] = jnp.zeros_like(cr_ref)
            ci_ref[...] = jnp.zeros_like(ci_ref)
            if with_da:
                dar_ref[...] = jnp.zeros_like(dar_ref)
                dai_ref[...] = jnp.zeros_like(dai_ref)

        row = _row((SUBLANES, W))

        def shift(v, d, fill):
            if reverse:
                return jnp.where(row < SUBLANES - d, pltpu.roll(v, SUBLANES - d, axis=0), fill)
            return jnp.where(row >= d, pltpu.roll(v, d, axis=0), fill)

        edge = 0 if reverse else SUBLANES - 1
        for wb in range(N_STATE // W):
            cols = slice(wb * W, (wb + 1) * W)
            a1r = jnp.broadcast_to(ar_ref[:, cols], (SUBLANES, W))
            a1i = jnp.broadcast_to(ai_ref[:, cols], (SUBLANES, W))
            if reverse:
                a1i = -a1i
            a2r, a2i = _cmul(a1r, a1i, a1r, a1i)
            a4r, a4i = _cmul(a2r, a2i, a2r, a2i)
            pws = ((1, a1r, a1i), (2, a2r, a2i), (4, a4r, a4i))
            pr, pi = a1r, a1i
            for d, _, _ in pws:
                qr, qi = _cmul(pr, pi, shift(pr, d, 1.0), shift(pi, d, 0.0))
                pr, pi = qr, qi
            mws = []
            for d, er, ei in pws:
                ok = (row < SUBLANES - d) if reverse else (row >= d)
                mws.append(((SUBLANES - d) if reverse else d, jnp.where(ok, er, 0.0), jnp.where(ok, ei, 0.0)))

            def tile(i, carry):
                cr, ci, dr, di = carry
                t = (nt - 1 - i) if reverse else i
                r0 = pl.multiple_of(t * SUBLANES, SUBLANES)
                vr = xr_ref[pl.ds(r0, SUBLANES), cols]
                vi = xi_ref[pl.ds(r0, SUBLANES), cols]
                for sh, er, ei in mws:
                    tr, ti = _cmul(er, ei, pltpu.roll(vr, sh, axis=0), pltpu.roll(vi, sh, axis=0))
                    vr, vi = vr + tr, vi + ti
                tr, ti = _cmul(pr, pi, cr, ci)
                vr, vi = vr + tr, vi + ti
                o_ref[pl.ds(r0, SUBLANES), cols] = vr
                o_ref[pl.ds(r0, SUBLANES), slice(N_STATE + wb * W, N_STATE + (wb + 1) * W)] = vi
                if with_da:
                    gr = jnp.where(row < SUBLANES - 1, pltpu.roll(vr, SUBLANES - 1, axis=0), cr)
                    gi = jnp.where(row < SUBLANES - 1, pltpu.roll(vi, SUBLANES - 1, axis=0), ci)
                    sr = sr_ref[pl.ds(r0, SUBLANES), cols]
                    si = si_ref[pl.ds(r0, SUBLANES), cols]
                    dr = dr + sr * gr + si * gi
                    di = di + sr * gi - si * gr
                ncr = jnp.broadcast_to(vr[edge:edge + 1, :], (SUBLANES, W))
                nci = jnp.broadcast_to(vi[edge:edge + 1, :], (SUBLANES, W))
                return ncr, nci, dr, di

            zero = jnp.zeros((SUBLANES, W), F32)
            cr, ci, dr, di = lax.fori_loop(0, nt, tile, (cr_ref[:, cols], ci_ref[:, cols], zero, zero), unroll=2)
            cr_ref[:, cols] = cr
            ci_ref[:, cols] = ci
            if with_da:
                dar_ref[:, cols] += dr
                dai_ref[:, cols] += di

        if with_da:
            @pl.when(step == nrb - 1)
            def _():
                dar_ref[...] = jnp.broadcast_to(jnp.sum(dar_ref[...], axis=0, keepdims=True), dar_ref.shape)
                dai_ref[...] = jnp.broadcast_to(jnp.sum(dai_ref[...], axis=0, keepdims=True), dai_ref.shape)

    rb = (lambda i: (nrb - 1 - i, 0)) if reverse else (lambda i: (i, 0))
    rb_im = (lambda i: (nrb - 1 - i, 1)) if reverse else (lambda i: (i, 1))
    blk_r = pl.BlockSpec((tb, N_STATE), rb)
    blk_i = pl.BlockSpec((tb, N_STATE), rb_im)
    one = pl.BlockSpec((1, N_STATE), lambda i: (0, 0))
    acc = pl.BlockSpec((SUBLANES, N_STATE), lambda i: (0, 0))
    ins = [x, x] + ([states, states] if with_da else []) + [lam_r, lam_i]
    in_specs = [blk_r, blk_i] + ([blk_r, blk_i] if with_da else []) + [one, one]
    out_specs = [pl.BlockSpec((tb, 2 * N_STATE), rb)] + ([acc, acc] if with_da else [])
    out_shape = [_sds((L, 2 * N_STATE), F32)] + (
        [_sds((SUBLANES, N_STATE), F32)] * 2 if with_da else [])
    outs = pl.pallas_call(
        body, name=name, grid=(nrb,), in_specs=in_specs, out_specs=out_specs, out_shape=out_shape,
        scratch_shapes=[pltpu.VMEM((SUBLANES, N_STATE), F32)] * 2,
        compiler_params=_cparams((6 if with_da else 4) * _nbytes((tb, N_STATE), F32),
                                 dimension_semantics=("arbitrary",)),
    )(*ins)
    return outs if with_da else outs[0]


def _scan_block(x_ref, o_ref, s_ref, ar_ref, ai_ref, cr_ref, ci_ref, dar_ref, dai_ref, nt, reverse):
    W = _SCAN_W
    with_da = s_ref is not None
    row = _row((SUBLANES, W))

    def shift(v, d, fill):
        if reverse:
            return jnp.where(row < SUBLANES - d, pltpu.roll(v, SUBLANES - d, axis=0), fill)
        return jnp.where(row >= d, pltpu.roll(v, d, axis=0), fill)

    edge = 0 if reverse else SUBLANES - 1
    for wb in range(N_STATE // W):
        cols = slice(wb * W, (wb + 1) * W)
        icols = slice(N_STATE + wb * W, N_STATE + (wb + 1) * W)
        a1r = jnp.broadcast_to(ar_ref[:, cols], (SUBLANES, W))
        a1i = jnp.broadcast_to(ai_ref[:, cols], (SUBLANES, W))
        if reverse:
            a1i = -a1i
        a2r, a2i = _cmul(a1r, a1i, a1r, a1i)
        a4r, a4i = _cmul(a2r, a2i, a2r, a2i)
        pws = ((1, a1r, a1i), (2, a2r, a2i), (4, a4r, a4i))
        pr, pi = a1r, a1i
        for d, _, _ in pws:
            qr, qi = _cmul(pr, pi, shift(pr, d, 1.0), shift(pi, d, 0.0))
            pr, pi = qr, qi
        mws = []
        for d, er, ei in pws:
            ok = (row < SUBLANES - d) if reverse else (row >= d)
            mws.append(((SUBLANES - d) if reverse else d, jnp.where(ok, er, 0.0), jnp.where(ok, ei, 0.0)))

        def tile(i, carry):
            cr, ci, dr, di = carry
            t = (nt - 1 - i) if reverse else i
            r0 = pl.multiple_of(t * SUBLANES, SUBLANES)
            vr = x_ref[pl.ds(r0, SUBLANES), cols]
            vi = x_ref[pl.ds(r0, SUBLANES), icols]
            for sh, er, ei in mws:
                tr, ti = _cmul(er, ei, pltpu.roll(vr, sh, axis=0), pltpu.roll(vi, sh, axis=0))
                vr, vi = vr + tr, vi + ti
            tr, ti = _cmul(pr, pi, cr, ci)
            vr, vi = vr + tr, vi + ti
            o_ref[pl.ds(r0, SUBLANES), cols] = vr
            o_ref[pl.ds(r0, SUBLANES), icols] = vi
            if with_da:
                gr = jnp.where(row < SUBLANES - 1, pltpu.roll(vr, SUBLANES - 1, axis=0), cr)
                gi = jnp.where(row < SUBLANES - 1, pltpu.roll(vi, SUBLANES - 1, axis=0), ci)
                sr = s_ref[pl.ds(r0, SUBLANES), cols]
                si = s_ref[pl.ds(r0, SUBLANES), icols]
                dr = dr + sr * gr + si * gi
                di = di + sr * gi - si * gr
            ncr = jnp.broadcast_to(vr[edge:edge + 1, :], (SUBLANES, W))
            nci = jnp.broadcast_to(vi[edge:edge + 1, :], (SUBLANES, W))
            return ncr, nci, dr, di

        zero = jnp.zeros((SUBLANES, W), F32)
        cr, ci, dr, di = lax.fori_loop(0, nt, tile, (cr_ref[:, cols], ci_ref[:, cols], zero, zero), unroll=2)
        cr_ref[:, cols] = cr
        ci_ref[:, cols] = ci
        if with_da:
            dar_ref[:, cols] += dr
            dai_ref[:, cols] += di


def _carry(comm, operands, in_specs, out_shape, out_specs, aliases, scratch):
    if comm is None:
        return 0, 0
    n0, no0 = len(operands), len(out_shape)
    operands += list(comm.ins)
    in_specs += [pl.BlockSpec(memory_space=pl.ANY)] * len(comm.ins)
    for t, x_ in enumerate(comm.ins):
        if comm.aliased[t]:
            aliases[n0 + t] = len(out_shape)
            out_shape.append(_sds(x_.shape, x_.dtype))
    out_shape += [_sds(sh, dt) for sh, dt in comm.fresh]
    out_specs += [pl.BlockSpec(memory_space=pl.ANY)] * (len(out_shape) - no0)
    scratch += [pltpu.SemaphoreType.DMA((comm.n_sems,))] * 2
    return len(comm.ins), len(out_shape) - no0


def _ssm_fwd(z, bcat, ccat, dskip, lam_r, lam_i, name, comm=None):
    L = z.shape[0]
    tb = _tile(L, _SCAN_TB)
    nrb = L // tb
    nt = tb // SUBLANES
    full = lambda shp: pl.BlockSpec(shp, lambda i: (0, 0))
    rows = lambda w: pl.BlockSpec((tb, w), lambda i: (i, 0))
    operands = [_hbm(z), bcat, ccat, dskip, lam_r, lam_i]
    in_specs = [pl.BlockSpec((tb, C_WIDTH), lambda i: (i, 5)), full((C_WIDTH, 2 * N_STATE)), full((2 * N_STATE, C_WIDTH)),
                full((1, C_WIDTH)), full((1, N_STATE)), full((1, N_STATE))]
    out_specs = [rows(2 * N_STATE), rows(C_WIDTH), rows(C_WIDTH)]
    out_shape = [_sds((L, 2 * N_STATE), F32), _sds((L, C_WIDTH), F32), _sds((L, C_WIDTH), _ACT)]
    scratch = [pltpu.VMEM((tb, 2 * N_STATE), F32)] + [pltpu.VMEM((SUBLANES, N_STATE), F32)] * 2
    aliases = {}
    nxi, nxo = _carry(comm, operands, in_specs, out_shape, out_specs, aliases, scratch)

    def body(*refs):
        u_ref, b_ref, c_ref, d_ref, ar_ref, ai_ref = refs[:6]
        x_ins = refs[6:6 + nxi]
        s_ref, y_ref, yg_ref = refs[6 + nxi:9 + nxi]
        x_outs = refs[9 + nxi:9 + nxi + nxo]
        xs_ref, cr_ref, ci_ref = refs[9 + nxi + nxo:12 + nxi + nxo]
        sems = refs[12 + nxi + nxo:]
        step = pl.program_id(0)

        @pl.when(step == 0)
        def _():
            cr_ref[...] = jnp.zeros_like(cr_ref)
            ci_ref[...] = jnp.zeros_like(ci_ref)
            if comm is not None:
                comm.start(x_ins, x_outs, *sems)

        u = u_ref[...]
        xs_ref[...] = _dot(u, b_ref[...])
        _scan_block(xs_ref, s_ref, None, ar_ref, ai_ref, cr_ref, ci_ref, None, None, nt, False)
        y = _dot(s_ref[...], c_ref[...]) + d_ref[...] * u
        y_ref[...] = y
        yg_ref[...] = _gelu(y).astype(yg_ref.dtype)

        if comm is not None:
            @pl.when(step == nrb - 1)
            def _():
                comm.wait(x_ins, x_outs, *sems)

    outs = pl.pallas_call(
        body, name=name, grid=(nrb,), in_specs=in_specs, out_specs=out_specs, out_shape=out_shape,
        scratch_shapes=scratch, input_output_aliases=aliases,
        compiler_params=_cparams(5 * _nbytes((tb, 2 * N_STATE), F32), dimension_semantics=("arbitrary",)),
    )(*operands)
    if comm is not None:
        comm.done(list(outs[3:]))
    return outs[0], outs[1], outs[2]


def _ssm_bwd(dy, z, S, bcat, ccat, dskip, lam_r, lam_i, name):
    L = z.shape[0]
    tb = _tile(L, _SCAN_TB)
    nrb = L // tb
    nt = tb // SUBLANES

    def body(dy_ref, u_ref, s_ref, b_ref, c_ref, d_ref, ar_ref, ai_ref,
             du_ref, db_ref, dc_ref, dd_ref, dar_ref, dai_ref, xs_ref, gs_ref, cr_ref, ci_ref):
        step = pl.program_id(0)

        @pl.when(step == 0)
        def _():
            for r in (cr_ref, ci_ref, db_ref, dc_ref, dd_ref, dar_ref, dai_ref):
                r[...] = jnp.zeros_like(r)

        dyv, u = dy_ref[...], u_ref[...]
        xs_ref[...] = _dot(dyv, c_ref[...], _NT)
        _scan_block(xs_ref, gs_ref, s_ref, ar_ref, ai_ref, cr_ref, ci_ref, dar_ref, dai_ref, nt, True)
        g = gs_ref[...]
        du_ref[...] = (_dot(g, b_ref[...], _NT) + dyv * d_ref[...]).astype(du_ref.dtype)
        db_ref[...] += _dot(u, g, _TN)
        dc_ref[...] += _dot(s_ref[...], dyv, _TN)
        dd_ref[...] += jnp.sum(dyv * u, axis=0, keepdims=True)

        @pl.when(step == nrb - 1)
        def _():
            dar_ref[...] = jnp.broadcast_to(jnp.sum(dar_ref[...], axis=0, keepdims=True), dar_ref.shape)
            dai_ref[...] = jnp.broadcast_to(jnp.sum(dai_ref[...], axis=0, keepdims=True), dai_ref.shape)

    full = lambda shp: pl.BlockSpec(shp, lambda i: (0, 0))
    rows = lambda w, col=0: pl.BlockSpec((tb, w), lambda i: (nrb - 1 - i, col))
    acc = full((SUBLANES, N_STATE))
    return pl.pallas_call(
        body, name=name, grid=(nrb,),
        in_specs=[rows(C_WIDTH), rows(C_WIDTH, 5), rows(2 * N_STATE), full((C_WIDTH, 2 * N_STATE)),
                  full((2 * N_STATE, C_WIDTH)), full((1, C_WIDTH)), full((1, N_STATE)), full((1, N_STATE))],
        out_specs=[rows(C_WIDTH), full((C_WIDTH, 2 * N_STATE)), full((2 * N_STATE, C_WIDTH)), full((1, C_WIDTH)), acc, acc],
        out_shape=[_sds((L, C_WIDTH), _ACT), _sds((C_WIDTH, 2 * N_STATE), F32), _sds((2 * N_STATE, C_WIDTH), F32),
                   _sds((1, C_WIDTH), F32), _sds((SUBLANES, N_STATE), F32), _sds((SUBLANES, N_STATE), F32)],
        scratch_shapes=[pltpu.VMEM((tb, 2 * N_STATE), F32)] * 2 + [pltpu.VMEM((SUBLANES, N_STATE), F32)] * 2,
        compiler_params=_cparams(7 * _nbytes((tb, 2 * N_STATE), F32), dimension_semantics=("arbitrary",)),
    )(dy, _hbm(z), _hbm(S), bcat, ccat, dskip, lam_r, lam_i)


_GROUPS = ((0, 256), (256, 768), (768, 1024))


def _merge_fwd(ya, yb, g12, mixg, name):
    L = ya.shape[0]
    tl = _tile(L, _TL)

    def body(a_ref, b_ref, g_ref, m_ref, o_ref):
        g12v = g_ref[...]
        yc = g12v[:, :C_WIDTH] * _sigmoid(g12v[:, C_WIDTH:])
        for (lo, hi), y in zip(_GROUPS, (a_ref[...], b_ref[...], yc)):
            r = lax.rsqrt(jnp.mean(y * y, axis=-1, keepdims=True) + EPS)
            o_ref[:, lo:hi] = ((y * r) * m_ref[:, lo:hi]).astype(o_ref.dtype)

    row = lambda w: pl.BlockSpec((tl, w), lambda i: (i, 0))
    return pl.pallas_call(
        body, name=name, grid=(L // tl,),
        in_specs=[row(256), row(512), row(512), pl.BlockSpec((1, D_MODEL), lambda i: (0, 0))],
        out_specs=row(D_MODEL), out_shape=_sds((L, D_MODEL), _ACT),
    )(ya, yb, g12, mixg.reshape(1, D_MODEL))


def _merge_bwd(dy, ya, yb, g12, mixg, name):
    L = ya.shape[0]
    tl = _tile(L, _TL)

    def body(d_ref, a_ref, b_ref, g_ref, m_ref, da_ref, db_ref, dg_ref, dm_ref):
        @pl.when(pl.program_id(0) == 0)
        def _():
            dm_ref[...] = jnp.zeros_like(dm_ref)

        g12v = g_ref[...]
        g1, sg = g12v[:, :C_WIDTH], _sigmoid(g12v[:, C_WIDTH:])
        yc = g1 * sg
        outs = []
        for (lo, hi), y in zip(_GROUPS, (a_ref[...], b_ref[...], yc)):
            r = lax.rsqrt(jnp.mean(y * y, axis=-1, keepdims=True) + EPS)
            xhat = y * r
            d = d_ref[:, lo:hi]
            gy = d * m_ref[:, lo:hi]
            outs.append(r * (gy - xhat * jnp.mean(gy * xhat, axis=-1, keepdims=True)))
            dm_ref[:, lo:hi] += jnp.sum(d * xhat, axis=0, keepdims=True)
        da_ref[...] = outs[0]
        db_ref[...] = outs[1]
        dyc = outs[2]
        dg_ref[:, :C_WIDTH] = (dyc * sg).astype(dg_ref.dtype)
        dg_ref[:, C_WIDTH:] = (dyc * g1 * sg * (1.0 - sg)).astype(dg_ref.dtype)

    row = lambda w: pl.BlockSpec((tl, w), lambda i: (i, 0))
    one = pl.BlockSpec((1, D_MODEL), lambda i: (0, 0))
    return pl.pallas_call(
        body, name=name, grid=(L // tl,),
        in_specs=[row(D_MODEL), row(256), row(512), row(512), one],
        out_specs=[row(256), row(512), row(512), one],
        out_shape=[_sds((L, 256), F32), _sds((L, 512), F32),
                   _sds((L, 512), _ACT), _sds((1, D_MODEL), F32)],
    )(dy, ya, yb, g12, mixg.reshape(1, D_MODEL))


def _ple_bwd_elem(dh, gate, e, name):
    L, D = dh.shape
    tl = _tile(L, _TL)

    def body(d_ref, g_ref, e_ref, p_ref, o_ref):
        d, g = d_ref[...], g_ref[...]
        p_ref[...] = (d * e_ref[...] * g * (1.0 - g)).astype(p_ref.dtype)
        o_ref[...] = (d * g).astype(o_ref.dtype)

    row = pl.BlockSpec((tl, D), lambda i: (i, 0))
    return pl.pallas_call(
        body, name=name, grid=(L // tl,), in_specs=[row] * 3, out_specs=[row] * 2,
        out_shape=[_sds((L, D), _ACT)] * 2,
        compiler_params=_cparams(4 * _nbytes((tl, D), F32)),
    )(dh, gate, e)


def _dskip_bwd(dy, z, name):
    L = dy.shape[0]
    tl = _tile(L, _TL)

    def body(d_ref, u_ref, o_ref):
        @pl.when(pl.program_id(0) == 0)
        def _():
            o_ref[...] = jnp.zeros_like(o_ref)

        o_ref[...] += jnp.sum(d_ref[...] * u_ref[...], axis=0, keepdims=True)

    return pl.pallas_call(
        body, name=name, grid=(L // tl,),
        in_specs=[pl.BlockSpec((tl, C_WIDTH), lambda i: (i, 0)), pl.BlockSpec((tl, C_WIDTH), lambda i: (i, 5))],
        out_specs=pl.BlockSpec((1, C_WIDTH), lambda i: (0, 0)),
        out_shape=_sds((1, C_WIDTH), F32),
    )(dy, z)


def _loss_fwd_bwd(y, target):
    L, D = y.shape
    tl = _tile(L, _TL)

    def body(y_ref, t_ref, l_ref, d_ref):
        @pl.when(pl.program_id(0) == 0)
        def _():
            l_ref[...] = jnp.zeros_like(l_ref)

        e = y_ref[...] - t_ref[...]
        d_ref[...] = e * (1.0 / D)
        part = jnp.sum(jnp.sum(e * e, axis=-1, keepdims=True), axis=0, keepdims=True)
        l_ref[...] += jnp.broadcast_to(part, l_ref.shape)

    row = pl.BlockSpec((tl, D), lambda i: (i, 0))
    return pl.pallas_call(
        body, name="loss", grid=(L // tl,), in_specs=[row, row],
        out_specs=[pl.BlockSpec((SUBLANES, LANES), lambda i: (0, 0)), row],
        out_shape=[_sds((SUBLANES, LANES), F32), _sds((L, D), F32)],
    )(y, target)


def _adamw(w, g, m, v, name):
    R, C = w.shape
    tr = R if R <= 512 else _tile_rows(R, 512)

    def body(w_ref, g_ref, m_ref, v_ref, d_ref, nm_ref, nv_ref):
        gv = g_ref[...]
        nm = ADAM_B1 * m_ref[...] + (1.0 - ADAM_B1) * gv
        nv = ADAM_B2 * v_ref[...] + (1.0 - ADAM_B2) * (gv * gv)
        m_hat = nm / (1.0 - ADAM_B1 ** ADAM_STEP)
        v_hat = nv / (1.0 - ADAM_B2 ** ADAM_STEP)
        d_ref[...] = -ADAM_LR * (m_hat / (jnp.sqrt(v_hat) + ADAM_EPS) + ADAM_WD * w_ref[...])
        nm_ref[...] = nm
        nv_ref[...] = nv

    blk = pl.BlockSpec((tr, C), lambda i: (i, 0))
    return pl.pallas_call(
        body, name=name, grid=(R // tr,), in_specs=[blk] * 4, out_specs=[blk] * 3,
        out_shape=[_sds((R, C), F32)] * 3,
        compiler_params=_cparams(7 * _nbytes((tr, C), F32)),
    )(w, g, m, v)


def _tile_rows(R, pref):
    t = pref
    while R % t:
        t -= SUBLANES
    assert t > 0
    return t


def _add_n(xs, name):
    R, C = xs[0].shape
    tr = R if R <= 512 else _tile_rows(R, 512)
    n = len(xs)

    def body(*refs):
        acc = refs[0][...].astype(F32)
        for r in refs[1:n]:
            acc = acc + r[...].astype(F32)
        refs[n][...] = acc

    blk = pl.BlockSpec((tr, C), lambda i: (i, 0))
    return pl.pallas_call(
        body, name=name, grid=(R // tr,), in_specs=[blk] * n, out_specs=blk,
        out_shape=_sds((R, C), F32),
        compiler_params=_cparams((n + 1) * _nbytes((tr, C), F32)),
    )(*xs)


class _Exchange:
    def __init__(self, ins, aliased, fresh, n_sems, start, wait, done):
        self.ins, self.aliased, self.fresh, self.n_sems = ins, aliased, fresh, n_sems
        self.start, self.wait, self.done = start, wait, done


def _mm_host(lp, key, *args, **kw):
    plan = lp.get(key)
    if plan is None:
        return _mm(*args, **kw)
    if not isinstance(plan, _Exchange):
        plan = plan()
    res, outs = _mm(*args, comm=plan, **kw)
    plan.done(outs)
    return res


def _relu2(acc):
    r = jnp.maximum(acc, 0.0)
    return (r * r,)


def _rms_rows(x, g):
    return (x * lax.rsqrt(jnp.mean(x * x, axis=-1, keepdims=True) + EPS)) * g


def _resid_norm_epi(acc, res, g):
    h = res + acc
    return h, _rms_rows(h, g)


def _rms_bwd_epi(acc, h, dres, g):
    r = lax.rsqrt(jnp.mean(h * h, axis=-1, keepdims=True) + EPS)
    xhat = h * r
    gy = acc * g
    dh = dres + r * (gy - xhat * jnp.mean(gy * xhat, axis=-1, keepdims=True))
    return dh, dh, jnp.sum(acc * xhat, axis=0, keepdims=True)


def _layer_fwd(h, xn, lp, cos, sin, g_next):
    L = h.shape[0]
    row = lambda n: lp[n].reshape(1, D_MODEL)
    z = _mm(xn, lp["W"]("w_in"), mode="nn", M=L, N=IN_COLS, K=D_MODEL, b_cb=True, out_dtypes=[F32], name="f_w_in")
    ya = _gmlp_fwd(z, lp["ws"], lp["bfull"], lp["lgf"], lp["lbf"], "f_gmlp")
    q, k = _qk_prep(z, cos, sin, lp["gq"], lp["gk"], "f_qk_prep")
    yb = _attn_fwd(q, k, z, lp["sinks"], "f_attn")
    plan = lp.get("x_ssm")
    S, y, yg = _ssm_fwd(z, lp["bcat"], lp["ccat"], lp["dskip"], lp["lam_r"], lp["lam_i"], "f_ssm",
                        comm=None if plan is None else plan())
    g12 = _mm(yg, lp["W"]("w12"), mode="nn", M=L, N=2 * C_WIDTH, K=C_WIDTH, out_dtypes=[F32], name="f_glu")
    ycat = _merge_fwd(ya, yb, g12, lp["mix_out_g"], "f_merge")
    h1, hn = _mm_host(lp, "x_out", ycat, lp["W"]("w_out"), mode="nn", M=L, N=D_MODEL, K=D_MODEL, extras=[(h, 0), (row("mlp_norm_g"), 0)],
                 epi=_resid_norm_epi, out_dtypes=[F32, _ACT], name="f_w_out")
    r = _mm_host(lp, "x_ff1", hn, lp["W"]("w_ff1"), mode="nn", M=L, N=D_FF, K=D_MODEL, b_cb=True, epi=_relu2,
                 out_dtypes=[_ACT], name="f_ff1")
    h2, hn3 = _mm_host(lp, "x_ff2", r, lp["W"]("w_ff2"), mode="nn", M=L, N=D_MODEL, K=D_FF,
                       extras=[(h1, 0), (row("ple_norm_g"), 0)],
                       epi=_resid_norm_epi, out_dtypes=[F32, _ACT], name="f_ff2")
    e = _mm(lp["p"], lp["W"]("w_ple_proj"), mode="nn", M=L, N=D_MODEL, K=PLE_DIM, b_cb=True, tk=PLE_DIM,
            out_dtypes=[F32], name="f_ple_proj")

    def gate_epi(acc, h2_, e_, *g):
        gate_ = _sigmoid(acc)
        h3_ = h2_ + gate_ * e_
        return (h3_, gate_) + ((_rms_rows(h3_, g[0]),) if g else ())

    outs = _mm_host(lp, "x_gate", hn3, lp["W"]("w_ple_gate"), mode="nn", M=L, N=D_MODEL, K=D_MODEL,
                    extras=[(h2, 0), (e, 0)] + ([(g_next.reshape(1, D_MODEL), 0)] if g_next is not None else []),
                    epi=gate_epi, out_dtypes=[F32, F32] + ([_ACT] if g_next is not None else []), name="f_ple_gate")
    h3, gate = outs[0], outs[1]
    xn_next = outs[2] if g_next is not None else None
    saved = dict(h=h, xn=xn, z=z, ya=ya, q=q, k=k, yb=yb, S=S, y=y, yg=yg, g12=g12, ycat=ycat, h1=h1, hn=hn,
                 r=r, h2=h2, hn3=hn3, e=e, gate=gate)
    return h3, xn_next, saved


def _layer_bwd(dh3, lp, sv, cos, sin):
    L = dh3.shape[0]
    z = sv["z"]
    dpre, de = _ple_bwd_elem(dh3, sv["gate"], sv["e"], "b_ple_elem")
    stk = {n: None for n in BIG}
    d_gate = _mm(sv["hn3"], dpre, mode="tn", M=D_MODEL, N=D_MODEL, K=L, out_dtypes=[F32], name="b_dw_gate",
                 o_stack=stk["w_ple_gate"])
    d_proj = _mm(lp["p"], de, mode="tn", M=PLE_DIM, N=D_MODEL, K=L, o_cb=True, tm=PLE_DIM,
                 out_dtypes=[F32], name="b_dw_proj", o_stack=stk["w_ple_proj"])
    row = lambda n: lp[n].reshape(1, D_MODEL)
    dh2, dh2_op, dg_ple = _mm_host(lp, "x_bwd0", dpre, lp["W"]("w_ple_gate"), mode="nt", M=L, N=D_MODEL, K=D_MODEL,
                                   extras=[(sv["h2"], 0), (dh3, 0), (row("ple_norm_g"), 0)], epi=_rms_bwd_epi,
                                   out_dtypes=[F32, _ACT, F32], n_acc=1, name="b_dx_gate")
    da = _mm_host(lp, "x_bwd", dh2_op, lp["W"]("w_ff2"), mode="nt", M=L, N=D_FF, K=D_MODEL, extras=[(sv["r"], 0)],
                  epi=lambda acc, r_: (acc * (2.0 * jnp.sqrt(r_.astype(F32))),), out_dtypes=[_ACT], name="b_dx_ff2")
    d_ff2 = _mm_host(lp, "x_bwd2", sv["r"], dh2_op, mode="tn", M=D_FF, N=D_MODEL, K=L, out_dtypes=[F32], name="b_dw_ff2")
    d_ff1 = _mm(sv["hn"], da, mode="tn", M=D_MODEL, N=D_FF, K=L, o_cb=True, out_dtypes=[F32], name="b_dw_ff1",
                o_stack=stk["w_ff1"])
    dh1, dh1_op, dg_mlp = _mm(da, lp["W"]("w_ff1"), mode="nt", M=L, N=D_MODEL, K=D_FF, b_cb=True,
                              extras=[(sv["h1"], 0), (dh2, 0), (row("mlp_norm_g"), 0)], epi=_rms_bwd_epi,
                              out_dtypes=[F32, _ACT, F32], n_acc=1, name="b_dx_ff1")
    d_out = _mm(sv["ycat"], dh1_op, mode="tn", M=D_MODEL, N=D_MODEL, K=L, out_dtypes=[F32], name="b_dw_out",
                o_stack=stk["w_out"])
    dycat = _mm(dh1_op, lp["W"]("w_out"), mode="nt", M=L, N=D_MODEL, K=D_MODEL, out_dtypes=[F32], name="b_dx_out")
    dya, dyb, dg12, dmix = _merge_bwd(dycat, sv["ya"], sv["yb"], sv["g12"], lp["mix_out_g"], "b_merge")
    d_w12 = _mm(sv["yg"], dg12, mode="tn", M=C_WIDTH, N=2 * C_WIDTH, K=L, tm=C_WIDTH, out_dtypes=[F32], name="b_dw_glu",
                o_stack=stk["w12"])
    dy = _mm(dg12, lp["W"]("w12"), mode="nt", M=L, N=C_WIDTH, K=2 * C_WIDTH, tk=2 * C_WIDTH, extras=[(sv["y"], 0)],
             epi=lambda acc, y_: (acc * _gelu_grad(y_),), out_dtypes=[F32], name="b_dx_glu")
    dzc, d_bcat, d_ccat, dd, dar, dai = _ssm_bwd(dy, z, sv["S"], lp["bcat"], lp["ccat"], lp["dskip"],
                                                 lp["lam_r"], lp["lam_i"], "b_ssm")
    dq, dkc, dkp, dvc, dvp, dsink = _attn_bwd(sv["q"], sv["k"], z, lp["sinks"], dyb, "b_attn")
    dzq, dzk, dzv, dgq, dgk = _qk_prep_bwd(z, dq, dkc, dkp, dvc, dvp, cos, sin, lp["gq"], lp["gk"], "b_qk_prep")
    dza, dws, dbs, dlg, dlb = _gmlp_bwd(z, dya, lp["ws"], lp["wsT"], lp["bfull"], lp["lgf"], lp["lbf"], "b_gmlp")
    dz = jnp.concatenate([dza, dzq, dzk, dzv, dzc], axis=1)
    d_in = _mm(sv["xn"], dz, mode="tn", M=D_MODEL, N=IN_COLS, K=L, o_cb=True, out_dtypes=[F32], name="b_dw_in",
               o_stack=stk["w_in"])
    dh, dg_attn = _mm(dz, lp["W"]("w_in"), mode="nt", M=L, N=D_MODEL, K=IN_COLS, b_cb=True,
                         extras=[(sv["h"], 0), (dh1, 0), (row("attn_norm_g"), 0)],
                         epi=lambda *t: (lambda o: (o[0], o[2]))(_rms_bwd_epi(*t)),
                         out_dtypes=[F32, F32], n_acc=1, name="b_dx_in")
    grads = dict(w_in=d_in, w12=d_w12, w_out=d_out, w_ff1=d_ff1, w_ff2=d_ff2, w_ple_gate=d_gate, w_ple_proj=d_proj,
                 attn_norm_g=dg_attn.reshape(D_MODEL), mlp_norm_g=dg_mlp.reshape(D_MODEL),
                 ple_norm_g=dg_ple.reshape(D_MODEL), mix_out_g=dmix.reshape(D_MODEL),
                 dws=dws, dbs=dbs, dlg=dlg, dlb=dlb, dgq=dgq, dgk=dgk, dsink=dsink,
                 dar=dar, dai=dai, d_bcat=d_bcat, d_ccat=d_ccat, dd=dd)
    return dh, grads


SMALL = ("attn_norm_g", "gmlp_ln_g", "gmlp_ln_b", "gmlp_ws", "gmlp_bs", "q_norm_g", "k_norm_g", "sinks",
         "ssm_a_re", "ssm_a_im", "ssm_log_dt", "ssm_b_re", "ssm_b_im", "ssm_c_re", "ssm_c_im", "ssm_d",
         "mix_out_g", "mlp_norm_g", "ple_norm_g")
BIG = ("w_in", "w12", "w_out", "w_ff1", "w_ff2", "w_ple_gate", "w_ple_proj")
COL_SHARDED = ("w_in", "w_ff1", "w_ple_proj")


def _block_diag(t):
    nl, g, a, b = t.shape
    eye = jnp.eye(g, dtype=t.dtype)
    return (t[:, :, :, None, :] * eye[None, :, None, :, None]).reshape(nl, g * a, g * b)


def _diag_blocks(t, a, b):
    nl = t.shape[0]
    t = t.reshape(nl, C_GROUPS, a, C_GROUPS, b)
    idx = jnp.arange(C_GROUPS)
    return jnp.moveaxis(t[:, idx, :, idx, :], 0, 1)


def _local_step(x, p, positions, target, sw, bw):
    nl = sw["attn_norm_g"].shape[0]
    G = nl * C_GROUPS
    zeros = lambda *s: jnp.zeros(s, F32)
    are = sw["ssm_a_re"].reshape(G, 1, C_STATE)
    aim = sw["ssm_a_im"].reshape(G, 1, C_STATE)
    ldt = jnp.broadcast_to(sw["ssm_log_dt"][..., None], (nl, C_GROUPS, C_STATE)).reshape(G, 1, C_STATE)
    bre = jnp.swapaxes(sw["ssm_b_re"], -1, -2).reshape(G, C_GROUP, C_STATE)
    bim = jnp.swapaxes(sw["ssm_b_im"], -1, -2).reshape(G, C_GROUP, C_STATE)
    lr, li, bbr, bbi = _ssm_prep(are, aim, ldt, bre, bim)
    unflat = lambda t: t.reshape(nl, C_GROUPS, C_GROUP, C_STATE)
    lp = dict(
        attn_norm_g=sw["attn_norm_g"], mlp_norm_g=sw["mlp_norm_g"], ple_norm_g=sw["ple_norm_g"],
        mix_out_g=sw["mix_out_g"], sinks=sw["sinks"],
        ws=sw["gmlp_ws"], wsT=jnp.swapaxes(sw["gmlp_ws"], -1, -2),
        bfull=jnp.concatenate([zeros(nl, A_HEADS, CHUNK, HEAD_DIM),
                               jnp.broadcast_to(sw["gmlp_bs"][..., None], (nl, A_HEADS, CHUNK, HEAD_DIM))], axis=-1),
        lgf=jnp.concatenate([zeros(nl, A_HEADS, HEAD_DIM), sw["gmlp_ln_g"]], axis=-1),
        lbf=jnp.concatenate([zeros(nl, A_HEADS, HEAD_DIM), sw["gmlp_ln_b"]], axis=-1),
        gq=jnp.tile(sw["q_norm_g"], (1, 2)).reshape(nl, 1, LANES),
        gk=jnp.tile(sw["k_norm_g"], (1, 2)).reshape(nl, 1, LANES),
        lam_r=lr.reshape(nl, 1, N_STATE), lam_i=li.reshape(nl, 1, N_STATE),
        bcat=jnp.concatenate([_block_diag(unflat(bbr)), _block_diag(unflat(bbi))], axis=-1),
        ccat=jnp.concatenate([_block_diag(jnp.swapaxes(sw["ssm_c_re"], -1, -2)),
                              -_block_diag(jnp.swapaxes(sw["ssm_c_im"], -1, -2))], axis=1),
        dskip=sw["ssm_d"].reshape(nl, 1, C_WIDTH))
    cos, sin = _rope_tables(positions)

    def layer_params(l, hooks):
        lpi = {n: v[l] for n, v in lp.items()}
        lpi["W"] = lambda n: bw.layer(l)[n]
        lpi["p"] = (p, l)
        lpi.update(hooks)
        return lpi

    h, saved = x, []
    xn = _rms_fwd(x, sw["attn_norm_g"][0], "f_norm_attn")
    for l in range(nl):
        g_next = sw["attn_norm_g"][l + 1] if l + 1 < nl else None
        h, xn, sv = _layer_fwd(h, xn, layer_params(l, bw.fwd_hooks(l)), cos, sin, g_next)
        saved.append(sv)
    sse, dh = _loss_fwd_bwd(h, target)

    per_layer = [None] * nl
    for l in reversed(range(nl)):
        dh, gl = _layer_bwd(dh, layer_params(l, bw.bwd_hooks(l)), saved[l], cos, sin)
        bw.grads(l, {n: gl.pop(n) for n in BIG})
        per_layer[l] = gl
    grad_x = dh
    g = {n: jnp.stack([per_layer[l][n] for l in range(nl)]) for n in per_layer[0]}

    d_bcat = g["d_bcat"]
    dbr = _diag_blocks(d_bcat[:, :, :N_STATE], C_GROUP, C_STATE).reshape(G, C_GROUP, C_STATE)
    dbi = _diag_blocks(d_bcat[:, :, N_STATE:], C_GROUP, C_STATE).reshape(G, C_GROUP, C_STATE)
    dlr = g["dar"][:, 0].reshape(G, 1, C_STATE)
    dli = g["dai"][:, 0].reshape(G, 1, C_STATE)
    g_are, g_aim, g_ldt, g_bre, g_bim = _ssm_prep_bwd(are, aim, ldt, bre, bim, dlr, dli, dbr, dbi)
    d_ccat = g["d_ccat"]
    sg = dict(
        attn_norm_g=g["attn_norm_g"], mlp_norm_g=g["mlp_norm_g"], ple_norm_g=g["ple_norm_g"], mix_out_g=g["mix_out_g"],
        gmlp_ln_g=g["dlg"][:, :, 0, HEAD_DIM:], gmlp_ln_b=g["dlb"][:, :, 0, HEAD_DIM:],
        gmlp_ws=g["dws"], gmlp_bs=g["dbs"][:, :, :, HEAD_DIM],
        q_norm_g=g["dgq"][:, 0, :HEAD_DIM], k_norm_g=g["dgk"][:, 0, :HEAD_DIM],
        sinks=g["dsink"][:, 0, :B_Q_HEADS],
        ssm_a_re=g_are.reshape(nl, C_GROUPS, C_STATE), ssm_a_im=g_aim.reshape(nl, C_GROUPS, C_STATE),
        ssm_log_dt=g_ldt[:, 0, 0].reshape(nl, C_GROUPS),
        ssm_b_re=jnp.swapaxes(g_bre.reshape(nl, C_GROUPS, C_GROUP, C_STATE), -1, -2),
        ssm_b_im=jnp.swapaxes(g_bim.reshape(nl, C_GROUPS, C_GROUP, C_STATE), -1, -2),
        ssm_c_re=jnp.swapaxes(_diag_blocks(d_ccat[:, :N_STATE], C_STATE, C_GROUP), -1, -2),
        ssm_c_im=-jnp.swapaxes(_diag_blocks(d_ccat[:, N_STATE:], C_STATE, C_GROUP), -1, -2),
        ssm_d=g["dd"].reshape(nl, C_GROUPS, C_GROUP),
    )
    return (sse[0, 0], grad_x, sg) + tuple(bw.finish(_pack(sg)))


_ANY = pl.BlockSpec(memory_space=pl.ANY)
N_LAYERS = 4


def _mesh_pos():
    x, y, c = lax.axis_index("x"), lax.axis_index("y"), lax.axis_index("c")
    chips = [(1 - x, y), (x, 1 - y), (1 - x, 1 - y)]
    return x, y, c, 2 * x + y, chips


def _cast_into_slot(ws, j, name):
    nl, R, _ = ws[0].shape
    widths = [w.shape[2] for w in ws]
    C = sum(widths)
    tr = R if R <= 512 else _tile_rows(R, 512)
    nw = len(ws)

    def body(s_ref, *refs):
        o_ref = refs[nw]
        off = 0
        for r, wd in zip(refs[:nw], widths):
            o_ref[:, off:off + wd] = r[...].astype(o_ref.dtype)
            off += wd

    return pl.pallas_call(
        body, name=name,
        grid_spec=pltpu.PrefetchScalarGridSpec(
            num_scalar_prefetch=1, grid=(nl, R // tr),
            in_specs=[pl.BlockSpec((None, tr, wd), lambda l, i, s: (l, i, 0)) for wd in widths],
            out_specs=pl.BlockSpec((None, None, tr, C), lambda l, i, s: (l, s[0], i, 0))),
        out_shape=_sds((nl, N_CHIPS, R, C), _MXU),
    )(jnp.reshape(j, (1,)).astype(jnp.int32), *ws)


def _gather_weights(bufs):
    nk = len(bufs)

    def body(*refs):
        ins, outs = refs[:nk], refs[nk:2 * nk]
        send_sems, recv_sems = refs[2 * nk:]
        x, y, c, j, chips = _mesh_pos()
        mine, other = pl.ds(2 * c, 2), pl.ds(2 * (1 - c), 2)

        def ici(t, q):
            cx, cy = chips[q]
            return pltpu.make_async_remote_copy(
                src_ref=ins[t].at[mine, j], dst_ref=outs[t].at[mine, j],
                send_sem=send_sems.at[6 * t + q], recv_sem=recv_sems.at[6 * t + q],
                device_id=(cx, cy, c), device_id_type=MESH)

        def landed(t, q):
            cx, cy = chips[q]
            blk = outs[t].at[mine, 2 * cx + cy]
            return pltpu.make_async_remote_copy(
                src_ref=blk, dst_ref=blk, send_sem=send_sems.at[6 * t + q], recv_sem=recv_sems.at[6 * t + q],
                device_id=(cx, cy, c), device_id_type=MESH)

        def fwd(t, q, rows):
            cx, cy = chips[q]
            blk = outs[t].at[rows, 2 * cx + cy]
            return pltpu.make_async_remote_copy(
                src_ref=blk, dst_ref=blk, send_sem=send_sems.at[6 * t + 3 + q], recv_sem=recv_sems.at[6 * t + 3 + q],
                device_id=(x, y, 1 - c), device_id_type=MESH)

        for t in range(nk):
            for q in range(3):
                ici(t, q).start()
        for t in range(nk):
            for q in range(3):
                landed(t, q).wait_recv()
                fwd(t, q, mine).start()
        for t in range(nk):
            for q in range(3):
                fwd(t, q, other).wait_recv()
        for t in range(nk):
            for q in range(3):
                ici(t, q).wait_send()
                fwd(t, q, mine).wait_send()

    return pl.pallas_call(
        body, name="gather_weights", in_specs=[_ANY] * nk, out_specs=[_ANY] * nk,
        out_shape=[_sds(b.shape, b.dtype) for b in bufs],
        input_output_aliases={t: t for t in range(nk)},
        scratch_shapes=[pltpu.SemaphoreType.DMA((6 * nk,)), pltpu.SemaphoreType.DMA((6 * nk,))],
    )(*bufs)


def _exchange_sibling_half(gl):
    nk = len(gl)

    def body(*refs):
        ins, outs = refs[:nk], refs[nk:2 * nk]
        send_sems, recv_sems = refs[2 * nk:]
        x, y, c, _, _ = _mesh_pos()
        cps = [pltpu.make_async_remote_copy(
            src_ref=ins[t].at[pl.ds(2 * (1 - c), 2)], dst_ref=outs[t],
            send_sem=send_sems.at[t], recv_sem=recv_sems.at[t],
            device_id=(x, y, 1 - c), device_id_type=MESH) for t in range(nk)]
        for cp in cps:
            cp.start()
        for cp in cps:
            cp.wait()

    return pl.pallas_call(
        body, name="reduce_sibling", in_specs=[_ANY] * nk, out_specs=[_ANY] * nk,
        out_shape=[_sds((2,) + g.shape[1:], g.dtype) for g in gl],
        scratch_shapes=[pltpu.SemaphoreType.DMA((nk,)), pltpu.SemaphoreType.DMA((nk,))],
    )(*gl)


def _exchange_chips(ps):
    nk = len(ps)

    def body(*refs):
        ins, outs = refs[:nk], refs[nk:2 * nk]
        send_sems, recv_sems = refs[2 * nk:]
        x, y, c, j, chips = _mesh_pos()

        def send(t, q):
            cx, cy = chips[q]
            return pltpu.make_async_remote_copy(
                src_ref=ins[t].at[:, 2 * cx + cy], dst_ref=outs[t].at[j],
                send_sem=send_sems.at[3 * t + q], recv_sem=recv_sems.at[3 * t + q],
                device_id=(cx, cy, c), device_id_type=MESH)

        def landed(t, q):
            cx, cy = chips[q]
            blk = outs[t].at[2 * cx + cy]
            return pltpu.make_async_remote_copy(
                src_ref=blk, dst_ref=blk, send_sem=send_sems.at[3 * t + q], recv_sem=recv_sems.at[3 * t + q],
                device_id=(cx, cy, c), device_id_type=MESH)

        for t in range(nk):
            for q in range(3):
                send(t, q).start()
        for t in range(nk):
            for q in range(3):
                landed(t, q).wait_recv()
        for t in range(nk):
            for q in range(3):
                send(t, q).wait_send()

    return pl.pallas_call(
        body, name="reduce_chips", in_specs=[_ANY] * nk, out_specs=[_ANY] * nk,
        out_shape=[_sds((N_CHIPS, 2) + p.shape[2:], p.dtype) for p in ps],
        scratch_shapes=[pltpu.SemaphoreType.DMA((3 * nk,)), pltpu.SemaphoreType.DMA((3 * nk,))],
    )(*ps)


def _share_sibling(fs):
    nk = len(fs)

    def body(*refs):
        ins, outs = refs[:nk], refs[nk:2 * nk]
        send_sems, recv_sems = refs[2 * nk:]
        x, y, c, _, _ = _mesh_pos()
        mine = pl.ds(2 * c, 2)
        cps = [pltpu.make_async_remote_copy(
            src_ref=ins[t].at[mine], dst_ref=outs[t].at[mine], send_sem=send_sems.at[t], recv_sem=recv_sems.at[t],
            device_id=(x, y, 1 - c), device_id_type=MESH) for t in range(nk)]
        for cp in cps:
            cp.start()
        for cp in cps:
            cp.wait_send()
        for t in range(nk):
            blk = outs[t].at[pl.ds(2 * (1 - c), 2)]
            pltpu.make_async_remote_copy(
                src_ref=blk, dst_ref=blk, send_sem=send_sems.at[t], recv_sem=recv_sems.at[t],
                device_id=(x, y, 1 - c), device_id_type=MESH).wait_recv()

    return pl.pallas_call(
        body, name="share_sibling", in_specs=[_ANY] * nk, out_specs=[_ANY] * nk,
        out_shape=[_sds(f.shape, f.dtype) for f in fs],
        input_output_aliases={t: t for t in range(nk)},
        scratch_shapes=[pltpu.SemaphoreType.DMA((nk,)), pltpu.SemaphoreType.DMA((nk,))],
    )(*fs)


def _add_own_half(gl, r1, c, name):
    _, ns, R, C = gl.shape
    rows = 2 * ns * R
    tr = _tile_rows(rows, 512)
    nblk = rows // tr

    def body(s_ref, a_ref, b_ref, o_ref):
        o_ref[...] = (a_ref[...] + b_ref[...]).astype(o_ref.dtype)

    out = pl.pallas_call(
        body, name=name,
        grid_spec=pltpu.PrefetchScalarGridSpec(
            num_scalar_prefetch=1, grid=(nblk,),
            in_specs=[pl.BlockSpec((tr, C), lambda i, s: (s[0] * nblk + i, 0)), pl.BlockSpec((tr, C), lambda i, s: (i, 0))],
            out_specs=pl.BlockSpec((tr, C), lambda i, s: (i, 0))),
        out_shape=_sds((rows, C), _WIRE),
        compiler_params=_cparams(3 * _nbytes((tr, C), F32)),
    )(jnp.reshape(c, (1,)).astype(jnp.int32), gl.reshape(2 * rows, C), r1.reshape(rows, C))
    return out.reshape(2, ns, R, C)


def _add_chips(p, r2, j, c, name):
    _, ns, R, C = p.shape
    tr = R if R <= 512 else _tile_rows(R, 512)

    def body(s_ref, own, a1, a2, a3, o_ref):
        f = lambda r: r[...].astype(F32)
        o_ref[...] = ((f(own) + f(a1)) + f(a2)) + f(a3)

    blk = (None, None, tr, C)
    return pl.pallas_call(
        body, name=name,
        grid_spec=pltpu.PrefetchScalarGridSpec(
            num_scalar_prefetch=1, grid=(2, R // tr),
            in_specs=[pl.BlockSpec(blk, lambda h, i, s: (h, s[0], i, 0))]
            + [pl.BlockSpec(blk, lambda h, i, s, k=k: ((s[0] + k) % N_CHIPS, h, i, 0)) for k in (1, 2, 3)],
            out_specs=pl.BlockSpec((None, tr, C), lambda h, i, s: (2 * s[1] + h, i, 0))),
        out_shape=_sds((N_LAYERS, R, C), F32),
        compiler_params=_cparams(6 * _nbytes((tr, C), F32)),
    )(jnp.stack([j, c]).astype(jnp.int32), p, r2, r2, r2)


def _allreduce_small(buf, plan=None):
    Rs = buf.shape[0]
    nx = 0 if plan is None else len(plan.ins)
    x_out_shape = [] if plan is None else [_sds(sh, dt) for sh, dt in plan.fresh]
    assert plan is None or not any(plan.aliased)
    nxo = len(x_out_shape)

    def body(*refs):
        b_ref, x_ins = refs[0], refs[1:1 + nx]
        o_ref, x_outs = refs[1 + nx], refs[2 + nx:2 + nx + nxo]
        t_ref, slots_ref, send_sems, recv_sems = refs[2 + nx + nxo:6 + nx + nxo]
        x_sems = refs[6 + nx + nxo:]
        if plan is not None:
            plan.start(x_ins, x_outs, *x_sems)
        x, y, c, j, chips = _mesh_pos()
        sib = pltpu.make_async_remote_copy(
            src_ref=b_ref, dst_ref=t_ref, send_sem=send_sems.at[0], recv_sem=recv_sems.at[0],
            device_id=(x, y, 1 - c), device_id_type=MESH)
        sib.start()
        sib.wait()
        slots_ref[j] = b_ref[...] + t_ref[...]

        def send(q):
            cx, cy = chips[q]
            return pltpu.make_async_remote_copy(
                src_ref=slots_ref.at[j], dst_ref=slots_ref.at[j], send_sem=send_sems.at[1 + q],
                recv_sem=recv_sems.at[1 + q], device_id=(cx, cy, c), device_id_type=MESH)

        def landed(q):
            cx, cy = chips[q]
            blk = slots_ref.at[2 * cx + cy]
            return pltpu.make_async_remote_copy(
                src_ref=blk, dst_ref=blk, send_sem=send_sems.at[1 + q], recv_sem=recv_sems.at[1 + q],
                device_id=(cx, cy, c), device_id_type=MESH)

        for q in range(3):
            send(q).start()
        for q in range(3):
            landed(q).wait_recv()
        for q in range(3):
            send(q).wait_send()
        o_ref[...] = ((slots_ref[0] + slots_ref[1]) + slots_ref[2]) + slots_ref[3]
        if plan is not None:
            plan.wait(x_ins, x_outs, *x_sems)

    vm = pl.BlockSpec(memory_space=pltpu.VMEM)
    outs = pl.pallas_call(
        body, name="allreduce_small", in_specs=[vm] + [_ANY] * nx, out_specs=[vm] + [_ANY] * nxo,
        out_shape=[_sds((Rs, LANES), F32)] + x_out_shape,
        scratch_shapes=[pltpu.VMEM((Rs, LANES), F32), pltpu.VMEM((N_CHIPS, Rs, LANES), F32),
                        pltpu.SemaphoreType.DMA((4,)), pltpu.SemaphoreType.DMA((4,))]
        + ([pltpu.SemaphoreType.DMA((plan.n_sems,))] * 2 if plan is not None else []),
        compiler_params=_cparams(4 * _nbytes((Rs, LANES), F32)),
    )(buf, *([] if plan is None else plan.ins))
    if plan is not None:
        plan.done(list(outs[1:]))
    return outs[0]


def _own_rows(c, R):
    return pl.ds(c * (R // 2), R // 2)


def _cast_layer_slot(ws, l, j, name):
    _, R, _ = ws[0].shape
    widths = [w.shape[2] for w in ws]
    C = sum(widths)
    tr = R if R <= 512 else _tile_rows(R, 512)
    nw = len(ws)

    def body(s_ref, *refs):
        o_ref = refs[nw]
        off = 0
        for r, wd in zip(refs[:nw], widths):
            o_ref[:, off:off + wd] = r[...].astype(o_ref.dtype)
            off += wd

    return pl.pallas_call(
        body, name=name,
        grid_spec=pltpu.PrefetchScalarGridSpec(
            num_scalar_prefetch=1, grid=(R // tr,),
            in_specs=[pl.BlockSpec((None, tr, wd), lambda i, s: (l, i, 0)) for wd in widths],
            out_specs=pl.BlockSpec((None, tr, C), lambda i, s: (s[0], i, 0))),
        out_shape=_sds((N_CHIPS, R, C), _MXU),
    )(jnp.reshape(j, (1,)).astype(jnp.int32), *ws)


def _gather_ici(bufs, done):
    nk = len(bufs)

    def copy(ins, outs, ss, rs, t, q, landed):
        x, y, c, j, chips = _mesh_pos()
        cx, cy = chips[q]
        rows = _own_rows(c, ins[t].shape[1])
        src = outs[t].at[2 * cx + cy, rows] if landed else ins[t].at[j, rows]
        dst = outs[t].at[2 * cx + cy, rows] if landed else outs[t].at[j, rows]
        return pltpu.make_async_remote_copy(src_ref=src, dst_ref=dst, send_sem=ss.at[3 * t + q], recv_sem=rs.at[3 * t + q],
                                            device_id=(cx, cy, c), device_id_type=MESH)

    def start(ins, outs, ss, rs):
        for t in range(nk):
            for q in range(3):
                copy(ins, outs, ss, rs, t, q, False).start()

    def wait(ins, outs, ss, rs):
        for t in range(nk):
            for q in range(3):
                copy(ins, outs, ss, rs, t, q, True).wait_recv()
                copy(ins, outs, ss, rs, t, q, False).wait_send()

    return _Exchange(bufs, [True] * nk, [], 3 * nk, start, wait, done)


def _gather_d2d(bufs, done):
    nk = len(bufs)

    def copy(ins, outs, ss, rs, t, q, mine):
        x, y, c, j, chips = _mesh_pos()
        cx, cy = chips[q]
        rows = _own_rows(c if mine else 1 - c, ins[t].shape[1])
        src = (ins if mine else outs)[t].at[2 * cx + cy, rows]
        return pltpu.make_async_remote_copy(src_ref=src, dst_ref=outs[t].at[2 * cx + cy, rows],
                                            send_sem=ss.at[3 * t + q], recv_sem=rs.at[3 * t + q],
                                            device_id=(x, y, 1 - c), device_id_type=MESH)

    def start(ins, outs, ss, rs):
        for t in range(nk):
            for q in range(3):
                copy(ins, outs, ss, rs, t, q, True).start()

    def wait(ins, outs, ss, rs):
        for t in range(nk):
            for q in range(3):
                copy(ins, outs, ss, rs, t, q, False).wait_recv()
                copy(ins, outs, ss, rs, t, q, True).wait_send()

    return _Exchange(bufs, [True] * nk, [], 3 * nk, start, wait, done)


def _reduce_d2d(gl, done):
    nk = len(gl)

    def copy(ins, outs, ss, rs, t):
        x, y, c, _, _ = _mesh_pos()
        return pltpu.make_async_remote_copy(
            src_ref=ins[t].at[:, _own_rows(1 - c, ins[t].shape[1])], dst_ref=outs[t],
            send_sem=ss.at[t], recv_sem=rs.at[t], device_id=(x, y, 1 - c), device_id_type=MESH)

    def start(ins, outs, ss, rs):
        for t in range(nk):
            copy(ins, outs, ss, rs, t).start()

    def wait(ins, outs, ss, rs):
        for t in range(nk):
            copy(ins, outs, ss, rs, t).wait()

    fresh = [((N_CHIPS, g.shape[1] // 2, g.shape[2]), g.dtype) for g in gl]
    return _Exchange(gl, [False] * nk, fresh, nk, start, wait, done)


def _reduce_ici(ps, done):
    nk = len(ps)

    def copy(ins, outs, ss, rs, t, q, landed):
        x, y, c, j, chips = _mesh_pos()
        cx, cy = chips[q]
        src = outs[t].at[2 * cx + cy] if landed else ins[t].at[2 * cx + cy]
        dst = outs[t].at[2 * cx + cy] if landed else outs[t].at[j]
        return pltpu.make_async_remote_copy(src_ref=src, dst_ref=dst, send_sem=ss.at[3 * t + q], recv_sem=rs.at[3 * t + q],
                                            device_id=(cx, cy, c), device_id_type=MESH)

    def start(ins, outs, ss, rs):
        for t in range(nk):
            for q in range(3):
                copy(ins, outs, ss, rs, t, q, False).start()

    def wait(ins, outs, ss, rs):
        for t in range(nk):
            for q in range(3):
                copy(ins, outs, ss, rs, t, q, True).wait_recv()
                copy(ins, outs, ss, rs, t, q, False).wait_send()

    return _Exchange(ps, [False] * nk, [(p_.shape, p_.dtype) for p_ in ps], 3 * nk, start, wait, done)


def _share_d2d(fs, done):
    nk = len(fs)

    def copy(ins, outs, ss, rs, t, mine):
        x, y, c, _, _ = _mesh_pos()
        rows = _own_rows(c if mine else 1 - c, ins[t].shape[1])
        src = (ins if mine else outs)[t].at[:, rows]
        return pltpu.make_async_remote_copy(src_ref=src, dst_ref=outs[t].at[:, rows], send_sem=ss.at[t], recv_sem=rs.at[t],
                                            device_id=(x, y, 1 - c), device_id_type=MESH)

    def start(ins, outs, ss, rs):
        for t in range(nk):
            copy(ins, outs, ss, rs, t, True).start()

    def wait(ins, outs, ss, rs):
        for t in range(nk):
            copy(ins, outs, ss, rs, t, False).wait_recv()
            copy(ins, outs, ss, rs, t, True).wait_send()

    return _Exchange(fs, [True] * nk, [], nk, start, wait, done)


def _run_exchange(plan, name):
    nin = len(plan.ins)
    out_shape = [_sds(x_.shape, x_.dtype) for x_, al in zip(plan.ins, plan.aliased) if al]
    aliases, k = {}, 0
    for t, al in enumerate(plan.aliased):
        if al:
            aliases[t] = k
            k += 1
    out_shape += [_sds(sh, dt) for sh, dt in plan.fresh]
    nout = len(out_shape)

    def body(*refs):
        ins, outs, sems = refs[:nin], refs[nin:nin + nout], refs[nin + nout:]
        plan.start(ins, outs, *sems)
        plan.wait(ins, outs, *sems)

    outs = pl.pallas_call(
        body, name=name, in_specs=[_ANY] * nin, out_specs=[_ANY] * nout, out_shape=out_shape,
        input_output_aliases=aliases,
        scratch_shapes=[pltpu.SemaphoreType.DMA((plan.n_sems,))] * 2,
    )(*plan.ins)
    plan.done(list(outs))


def _add_sibling_rows(g, r1, c, name):
    ns, R, C = g.shape
    hr = R // 2
    tr = hr if hr <= 512 else _tile_rows(hr, 512)
    nblk = hr // tr

    def body(s_ref, a_ref, b_ref, o_ref):
        o_ref[...] = (a_ref[...] + b_ref[...]).astype(o_ref.dtype)

    blk = (None, tr, C)
    return pl.pallas_call(
        body, name=name,
        grid_spec=pltpu.PrefetchScalarGridSpec(
            num_scalar_prefetch=1, grid=(ns, nblk),
            in_specs=[pl.BlockSpec(blk, lambda s_, i, s: (s_, s[0] * nblk + i, 0)), pl.BlockSpec(blk, lambda s_, i, s: (s_, i, 0))],
            out_specs=pl.BlockSpec(blk, lambda s_, i, s: (s_, i, 0))),
        out_shape=_sds((ns, hr, C), _WIRE),
        compiler_params=_cparams(3 * _nbytes((tr, C), F32)),
    )(jnp.reshape(c, (1,)).astype(jnp.int32), g, r1)


def _add_chip_rows(p_, r2, f, l, j, c, name):
    _, hr, C = p_.shape
    tr = hr if hr <= 512 else _tile_rows(hr, 512)
    nblk = hr // tr

    def body(s_ref, own, a1, a2, a3, f_ref, o_ref):
        v = lambda r: r[...].astype(F32)
        o_ref[...] = ((v(own) + v(a1)) + v(a2)) + v(a3)

    blk = (None, tr, C)
    return pl.pallas_call(
        body, name=name,
        grid_spec=pltpu.PrefetchScalarGridSpec(
            num_scalar_prefetch=1, grid=(nblk,),
            in_specs=[pl.BlockSpec(blk, lambda i, s: (s[0], i, 0))]
            + [pl.BlockSpec(blk, lambda i, s, k=k: ((s[0] + k) % N_CHIPS, i, 0)) for k in (1, 2, 3)]
            + [pl.BlockSpec(memory_space=pl.ANY)],
            out_specs=pl.BlockSpec(blk, lambda i, s: (l, s[1] * nblk + i, 0))),
        out_shape=_sds(f.shape, F32),
        input_output_aliases={5: 0},
        compiler_params=_cparams(6 * _nbytes((tr, C), F32)),
    )(jnp.stack([j, c]).astype(jnp.int32), p_, r2, r2, r2, f)


class _ShardedWeights:
    def __init__(self, a, j, c):
        self.j, self.c = j, c
        shards = dict(w_in=[a["w_in"]], w12=[a["glu_w1"], a["glu_w2"]], w_out=[a["w_out"]], w_ff1=[a["w_ff1"]],
                      w_ff2=[a["w_ff2"]], w_ple_gate=[a["w_ple_gate"]], w_ple_proj=[a["w_ple_proj"]])
        self.bufs = [[_cast_layer_slot(shards[n], l, j, "cast_%s_%d" % (n, l)) for n in BIG] for l in range(N_LAYERS)]
        _run_exchange(self._gather_part(0, self._HEAD), "gather_ici_0")
        _run_exchange(self._gather_part(0, self._HEAD, _gather_d2d), "gather_d2d_0")
        self.raw = None
        self.pending = None
        self.final = [lax.empty((N_LAYERS,) + b.shape[1:], F32) for b in self.bufs[0]]

    def _set_bufs(self, l, outs):
        self.bufs[l] = outs

    def layer(self, l):
        return {n: (b if n in COL_SHARDED else b.reshape(N_CHIPS * b.shape[1], b.shape[2]))
                for n, b in zip(BIG, self.bufs[l])}

    _FIRST, _SECOND = (0, 3), (1, 2, 4, 5, 6)

    _HEAD, _REST = (0, 1, 2), (3, 4, 5, 6)

    def _gather_part(self, nxt, idx, exchange=_gather_ici):
        def done(outs):
            for i, o in zip(idx, outs):
                self.bufs[nxt][i] = o
        return exchange([self.bufs[nxt][i] for i in idx], done)

    def fwd_hooks(self, l):
        hooks = {}
        if l == 0:
            hooks.update(x_ssm=lambda: self._gather_part(0, self._REST),
                         x_out=lambda: self._gather_part(0, self._REST, _gather_d2d))
        if l + 1 < N_LAYERS:
            nxt = l + 1
            hooks.update(x_ff1=lambda: self._gather_part(nxt, self._FIRST),
                         x_ff2=lambda: self._gather_part(nxt, self._SECOND),
                         x_gate=lambda: _gather_d2d(self.bufs[nxt], lambda o: self._set_bufs(nxt, o)))
        return hooks

    def _sibling_done(self, lyr, gl, got):
        ps = [_add_sibling_rows(g_, r1, self.c, "reduce_add_sibling_%s_%d" % (n, lyr)) for g_, r1, n in zip(gl, got, BIG)]
        self.pending = (lyr, ps)

    def _reduce_part(self, idx):
        lyr, ps = self.pending

        def done(r2):
            for i, r in zip(idx, r2):
                self.final[i] = _add_chip_rows(ps[i], r, self.final[i], lyr, self.j, self.c,
                                               "reduce_add_chips_%s_%d" % (BIG[i], lyr))
        return _reduce_ici([ps[i] for i in idx], done)

    def bwd_hooks(self, l):
        if self.raw is None:
            return {}
        lyr, gl = self.raw
        self.raw = None
        return dict(x_bwd0=lambda: _reduce_d2d(gl, lambda got: self._sibling_done(lyr, gl, got)),
                    x_bwd=lambda: self._reduce_part(self._FIRST),
                    x_bwd2=lambda: self._reduce_part(self._SECOND))

    def grads(self, l, g):
        gl = [g[n] if n in COL_SHARDED else g[n].reshape(N_CHIPS, g[n].shape[0] // N_CHIPS, g[n].shape[1]) for n in BIG]
        self.raw = (l, gl)

    def finish(self, small):
        lyr, gl = self.raw
        _run_exchange(_reduce_d2d(gl, lambda got: self._sibling_done(lyr, gl, got)), "reduce_d2d_%d" % lyr)
        small = _allreduce_small(small, self._reduce_part(self._FIRST + self._SECOND))
        out = []
        _run_exchange(_share_d2d(self.final, out.extend), "share_d2d")
        return dict(zip(BIG, out)), small


def _rows_of(shape):
    return -(-int(np.prod(shape)) // (SUBLANES * LANES)) * SUBLANES


def _pack(d):
    parts = []
    for n in SMALL:
        flat = d[n].reshape(-1)
        parts.append(jnp.pad(flat, (0, _rows_of(flat.shape) * LANES - flat.shape[0])).reshape(-1, LANES))
    return jnp.concatenate(parts, axis=0)


def _unpack(buf, like):
    out, r0 = {}, 0
    for n in SMALL:
        shape = like[n].shape
        size, nr = int(np.prod(shape)), _rows_of(shape)
        piece = lax.optimization_barrier(buf[r0:r0 + nr])
        out[n] = piece.reshape(-1)[:size].reshape(shape)
        r0 += nr
    return out


ARGS = ("x", "p", "positions", "attn_norm_g", "w_in", "gmlp_ln_g", "gmlp_ln_b", "gmlp_ws", "gmlp_bs", "q_norm_g",
        "k_norm_g", "sinks", "ssm_a_re", "ssm_a_im", "ssm_log_dt", "ssm_b_re", "ssm_b_im", "ssm_c_re", "ssm_c_im",
        "ssm_d", "glu_w1", "glu_w2", "mix_out_g", "w_out", "mlp_norm_g", "w_ff1", "w_ff2", "ple_norm_g", "w_ple_gate",
        "w_ple_proj")
WEIGHTS = ARGS[3:]


def kernel(x, p, positions, attn_norm_g, w_in, gmlp_ln_g, gmlp_ln_b, gmlp_ws, gmlp_bs, q_norm_g, k_norm_g, sinks, ssm_a_re, ssm_a_im, ssm_log_dt, ssm_b_re, ssm_b_im, ssm_c_re, ssm_c_im, ssm_d, glu_w1, glu_w2, mix_out_g, w_out, mlp_norm_g, w_ff1, w_ff2, ple_norm_g, w_ple_gate, w_ple_proj, loss_target, m_attn_norm_g, m_w_in, m_gmlp_ln_g, m_gmlp_ln_b, m_gmlp_ws, m_gmlp_bs, m_q_norm_g, m_k_norm_g, m_sinks, m_ssm_a_re, m_ssm_a_im, m_ssm_log_dt, m_ssm_b_re, m_ssm_b_im, m_ssm_c_re, m_ssm_c_im, m_ssm_d, m_glu_w1, m_glu_w2, m_mix_out_g, m_w_out, m_mlp_norm_g, m_w_ff1, m_w_ff2, m_ple_norm_g, m_w_ple_gate, m_w_ple_proj, v_attn_norm_g, v_w_in, v_gmlp_ln_g, v_gmlp_ln_b, v_gmlp_ws, v_gmlp_bs, v_q_norm_g, v_k_norm_g, v_sinks, v_ssm_a_re, v_ssm_a_im, v_ssm_log_dt, v_ssm_b_re, v_ssm_b_im, v_ssm_c_re, v_ssm_c_im, v_ssm_d, v_glu_w1, v_glu_w2, v_mix_out_g, v_w_out, v_mlp_norm_g, v_w_ff1, v_w_ff2, v_ple_norm_g, v_w_ple_gate, v_w_ple_proj):
    a = dict(locals())
    L = a["x"].shape[1]
    nl = N_LAYERS
    c = lax.axis_index("c")
    j = 2 * lax.axis_index("x") + lax.axis_index("y")

    sw = {n: a[n] for n in SMALL}
    sse, gx, _, big_grads, small_sum = _local_step(
        a["x"].reshape(L, D_MODEL), a["p"].reshape(nl, L, PLE_DIM), a["positions"].reshape(L),
        a["loss_target"].reshape(L, D_MODEL), sw, _ShardedWeights(a, j, c))
    loss = lax.psum(sse * (0.5 / D_MODEL), ("x", "y", "c"))
    g12 = big_grads.pop("w12")
    big_grads["glu_w1"], big_grads["glu_w2"] = g12[:, :, :C_WIDTH], g12[:, :, C_WIDTH:]

    small_grads = _unpack(small_sum, sw)

    grads, delta, new_m, new_v = {}, {}, {}, {}
    d_s, m_s, v_s = _adamw(_pack(sw), _pack(small_grads), _pack({n: a["m_" + n] for n in SMALL}),
                           _pack({n: a["v_" + n] for n in SMALL}), "adamw_small")
    grads.update(small_grads)
    delta.update(_unpack(d_s, sw))
    new_m.update(_unpack(m_s, sw))
    new_v.update(_unpack(v_s, sw))
    for n, g in big_grads.items():
        shp = a[n].shape
        two_d = lambda t: t.reshape(shp[0] * shp[1], shp[2])
        d, m, v = _adamw(two_d(a[n]), two_d(g), two_d(a["m_" + n]), two_d(a["v_" + n]), "adamw_" + n)
        grads[n], delta[n], new_m[n], new_v[n] = g, d.reshape(shp), m.reshape(shp), v.reshape(shp)

    return (loss, gx.reshape(1, L, D_MODEL), *[grads[n] for n in WEIGHTS], *[delta[n] for n in WEIGHTS],
            *[new_m[n] for n in WEIGHTS], *[new_v[n] for n in WEIGHTS])
```

```python
import functools
import math

import numpy as np
import jax
import jax.numpy as jnp
from jax import lax
from jax.experimental import pallas as pl
from jax.experimental.pallas import tpu as pltpu

F32 = jnp.float32
_MXU = jnp.bfloat16
_ACT = jnp.bfloat16
_WIRE = jnp.bfloat16

D_MODEL = 1024
HEAD_DIM = 64
A_HEADS = 4
CHUNK = 128
B_Q_HEADS = 8
B_KV_HEADS = 2
B_GROUP = 4
WINDOW = 128
ROPE_THETA = 10000.0
C_WIDTH = 256
C_GROUP = 16
C_GROUPS = 16
C_STATE = 64
N_STATE = C_GROUPS * C_STATE
IN_A, IN_Q, IN_KV, IN_C = 512, 512, 128, 256
IN_COLS = 1536
D_FF = 4096
PLE_DIM = 256
EPS = 1e-6
NEG = -1e30
ADAM_LR, ADAM_B1, ADAM_B2, ADAM_EPS, ADAM_WD, ADAM_STEP = 0.001, 0.9, 0.999, 1e-08, 0.01, 10

LANES = 128
SUBLANES = 8
VMEM_BYTES = 64 * 2 ** 20
N_CHIPS = 4
MESH = pl.DeviceIdType.MESH


_MM_VMEM_BUDGET = 55 * 2 ** 20
_EPI_ROWS = 256


def _vmem_limit(est_bytes):
    return int(min(max(2 * est_bytes + (8 << 20), 32 << 20), VMEM_BYTES - (6 << 20)))


def _cparams(est_bytes, **kw):
    return pltpu.CompilerParams(vmem_limit_bytes=_vmem_limit(est_bytes), **kw)


def _sds(shape, dtype):
    return pltpu.HBM(tuple(shape), dtype)


def _hbm(x):
    return pltpu.with_memory_space_constraint(x, pltpu.HBM) if x.size >= (1 << 20) else x


def _nbytes(shape, dtype):
    return int(np.prod(shape)) * jnp.dtype(dtype).itemsize


def _tile(dim, pref):
    t = min(dim, pref)
    while dim % t:
        t -= LANES
    assert t > 0, (dim, pref)
    return t


def _lane(shape):
    return lax.broadcasted_iota(jnp.int32, shape, len(shape) - 1)


def _row(shape):
    return lax.broadcasted_iota(jnp.int32, shape, len(shape) - 2)


def _gelu(x):
    c = math.sqrt(2.0 / math.pi)
    return 0.5 * x * (1.0 + jnp.tanh(c * (x + 0.044715 * (x * x * x))))


def _gelu_grad(x):
    c = math.sqrt(2.0 / math.pi)
    t = jnp.tanh(c * (x + 0.044715 * (x * x * x)))
    return 0.5 * (1.0 + t) + 0.5 * x * (1.0 - t * t) * (c * (1.0 + 3.0 * 0.044715 * (x * x)))


def _sigmoid(x):
    return 1.0 / (1.0 + jnp.exp(-x))


def _dot(a, b, dims=(((1,), (0,)), ((), ()))):
    return lax.dot_general(a.astype(_MXU), b.astype(_MXU), dims, preferred_element_type=F32)


_NT = (((1,), (1,)), ((), ()))
_TN = (((0,), (0,)), ((), ()))
_NN = (((1,), (0,)), ((), ()))


def _mm(a, b, *, mode, M, N, K, out_dtypes, name, epi=None, extras=(), b_cb=False, o_cb=False,
        a_off=0, b_off=0, tm=1024, tn=1024, tk=2048, a_lyr=None, b_lyr=None, o_stack=None, n_acc=0, comm=None):
    if isinstance(a, tuple):
        a, a_lyr = a
    if isinstance(b, tuple):
        b, b_lyr = b
    if b_cb or o_cb:
        nc = (b.shape[-1] if b_cb else N // N_CHIPS)
    tn_nom = nc if ((mode == "nn" and b_cb) or (mode == "tn" and o_cb)) else _tile(N, tn)
    tk_nom = nc if (mode == "nt" and b_cb) else _tile(K, tk)
    item = lambda d: jnp.dtype(d).itemsize
    per_row = tk_nom * item(a.dtype) + tn_nom * (sum(item(d) for d in out_dtypes)
                                                   + sum(item(e.dtype) for e, _ in extras if e.shape[0] > 1))
    fixed = tk_nom * tn_nom * item(b.dtype)
    tm = _tile(M, tm)
    while tm > 256 and M % (tm // 2) == 0 and 2 * (tm * per_row + fixed) + 8 * tm * tn_nom > _MM_VMEM_BUDGET:
        tm //= 2

    def spec(block, imap, lyr=None):
        if lyr is None:
            return pl.BlockSpec(block, imap)
        return pl.BlockSpec((None,) + block, lambda i, j, k: (lyr,) + imap(i, j, k))

    if mode == "nn":
        if b_cb:
            tn = nc
        tm, tn, tk = _tile(M, tm), _tile(N, tn), _tile(K, tk)
        a_spec = spec((tm, tk), lambda i, j, k: (i, k + a_off), a_lyr)
        if b_cb:
            b_spec = spec((None, tk, tn), lambda i, j, k: (j, k, 0), b_lyr)
        else:
            b_spec = spec((tk, tn), lambda i, j, k: (k, j + b_off), b_lyr)
        dims = _NN
        a_blk, b_blk = (tm, tk), (tk, tn)
    elif mode == "nt":
        if b_cb:
            tk = nc
        tm, tn, tk = _tile(M, tm), _tile(N, tn), _tile(K, tk)
        a_spec = spec((tm, tk), lambda i, j, k: (i, k + a_off), a_lyr)
        if b_cb:
            b_spec = spec((None, tn, tk), lambda i, j, k: (k, j, 0), b_lyr)
        else:
            b_spec = spec((tn, tk), lambda i, j, k: (j, k + b_off), b_lyr)
        dims = _NT
        a_blk, b_blk = (tm, tk), (tn, tk)
    else:
        if o_cb:
            tn = nc
        tm, tn, tk = _tile(M, tm), _tile(N, tn), _tile(K, tk)
        a_spec = spec((tk, tm), lambda i, j, k: (k, i + a_off), a_lyr)
        b_spec = spec((tk, tn), lambda i, j, k: (k, j + b_off), b_lyr)
        dims = _TN
        a_blk, b_blk = (tk, tm), (tk, tn)
    gi, gj, gk = M // tm, N // tn, K // tk
    o_lyr = None if o_stack is None else o_stack[1]
    if o_cb:
        o_spec = spec((None, tm, tn), lambda i, j, k: (j, i, 0), o_lyr)
        o_shape = (gj, M, tn)
    else:
        o_spec = spec((tm, tn), lambda i, j, k: (i, j), o_lyr)
        o_shape = (M, N)
    e_specs = []
    for e, off in extras:
        if e.shape[0] == 1:
            e_specs.append(pl.BlockSpec((1, tn), lambda i, j, k, off=off: (0, j + off)))
        else:
            e_specs.append(pl.BlockSpec((tm, tn), lambda i, j, k, off=off: (i, j + off)))
    extras = [e for e, _ in extras]
    ne, no = len(extras), len(out_dtypes)
    operands = [_hbm(t) for t in (a, b, *extras)]
    in_specs = [a_spec, b_spec] + e_specs
    out_shape = [_sds(o_shape, d) for d in out_dtypes]
    aliases = {}
    if o_stack is not None:
        assert no == 1 and o_stack[0].shape[1:] == o_shape and o_stack[0].dtype == out_dtypes[0]
        operands.append(_hbm(o_stack[0]))
        in_specs.append(pl.BlockSpec(memory_space=pl.ANY))
        out_shape = [_sds(o_stack[0].shape, o_stack[0].dtype)]
        aliases = {len(operands) - 1: 0}
    out_specs = [o_spec] * no
    if n_acc:
        assert gj == 1 and o_stack is None
        out_specs[no - n_acc:] = [pl.BlockSpec((1, tn), lambda i, j, k: (0, 0))] * n_acc
        out_shape[no - n_acc:] = [_sds((1, N), d) for d in out_dtypes[no - n_acc:]]
    nx_in = nx_out = 0
    if comm is not None:
        nx_in, ncin0 = len(comm.ins), len(operands)
        operands += list(comm.ins)
        in_specs += [pl.BlockSpec(memory_space=pl.ANY)] * nx_in
        for t, x_ in enumerate(comm.ins):
            if comm.aliased[t]:
                aliases[ncin0 + t] = len(out_shape)
                out_shape.append(_sds(x_.shape, x_.dtype))
        out_shape += [_sds(sh, dt) for sh, dt in comm.fresh]
        nx_out = len(out_shape) - no
        out_specs += [pl.BlockSpec(memory_space=pl.ANY)] * nx_out
    nin = len(operands)

    def body(*refs):
        a_ref, b_ref = refs[0], refs[1]
        e_refs = refs[2:2 + ne]
        o_refs = refs[nin:nin + no]
        first_rows = pl.program_id(0) == 0
        if comm is not None:
            x_ins = refs[nin - nx_in:nin]
            x_outs = refs[nin + no:nin + no + nx_out]
            sems = refs[nin + no + nx_out:nin + no + nx_out + 2]
            pid = [pl.program_id(d) for d in range(3)]

            @pl.when((pid[0] == 0) & (pid[1] == 0) & (pid[2] == 0))
            def _():
                comm.start(x_ins, x_outs, *sems)

        def fin(acc):
            for t in range(no - n_acc, no):
                @pl.when(first_rows)
                def _():
                    o_refs[t][...] = jnp.zeros_like(o_refs[t])

            rc = _EPI_ROWS if (epi is not None and tm % _EPI_ROWS == 0) else tm
            for c0 in range(0, tm, rc):
                rows = slice(c0, c0 + rc)
                ex = [e[...] if e.shape[0] == 1 else e[rows, :] for e in e_refs]
                vals = epi(acc[rows, :], *ex) if epi is not None else (acc[rows, :],)
                for t, (o, v) in enumerate(zip(o_refs, vals)):
                    if t < no - n_acc:
                        o[rows, :] = v.astype(o.dtype)
                    else:
                        o[...] += v.astype(o.dtype)

        prod = _dot(a_ref[...], b_ref[...], dims)
        if gk == 1:
            fin(prod)
        else:
            acc_ref = refs[-1]
            k = pl.program_id(2)

            @pl.when(k == 0)
            def _():
                acc_ref[...] = prod

            @pl.when(k > 0)
            def _():
                acc_ref[...] += prod

            @pl.when(k == gk - 1)
            def _():
                fin(acc_ref)

        if comm is not None:
            @pl.when((pid[0] == gi - 1) & (pid[1] == gj - 1) & (pid[2] == gk - 1))
            def _():
                comm.wait(x_ins, x_outs, *sems)

    est = (_nbytes(a_blk, a.dtype) + _nbytes(b_blk, b.dtype)
           + sum(_nbytes((tm, tn), d) for d in out_dtypes)
           + sum(_nbytes((tm, tn), e.dtype) for e in extras)) + 2 * _nbytes((tm, tn), F32)
    sem_scratch = [pltpu.SemaphoreType.DMA((comm.n_sems,))] * 2 if comm is not None else []
    row_sem = "arbitrary" if (n_acc or comm is not None) else "parallel"
    outs = pl.pallas_call(
        body, name=name, grid=(gi, gj, gk),
        in_specs=in_specs,
        out_specs=out_specs,
        out_shape=out_shape,
        scratch_shapes=sem_scratch + ([pltpu.VMEM((tm, tn), F32)] if gk > 1 else []),
        input_output_aliases=aliases,
        compiler_params=_cparams(est, dimension_semantics=(row_sem, "arbitrary" if comm is not None else "parallel",
                                                           "arbitrary")),
    )(*operands)
    if comm is not None:
        main = outs[:no]
        return (main if no > 1 else main[0]), list(outs[no:])
    return outs if no > 1 else outs[0]


_TL = 512


def _rms_fwd(h, g, name):
    L, D = h.shape
    tl = _tile(L, _TL)

    def body(h_ref, g_ref, o_ref):
        x = h_ref[...]
        r = lax.rsqrt(jnp.mean(x * x, axis=-1, keepdims=True) + EPS)
        o_ref[...] = ((x * r) * g_ref[...]).astype(o_ref.dtype)

    return pl.pallas_call(
        body, name=name, grid=(L // tl,),
        in_specs=[pl.BlockSpec((tl, D), lambda i: (i, 0)), pl.BlockSpec((1, D), lambda i: (0, 0))],
        out_specs=pl.BlockSpec((tl, D), lambda i: (i, 0)),
        out_shape=_sds((L, D), _ACT),
        compiler_params=_cparams(3 * _nbytes((tl, D), F32)),
    )(h, g.reshape(1, D))


def _rms_bwd(dxn, h, g, dres, name):
    L, D = h.shape
    tl = _tile(L, _TL)

    def body(d_ref, h_ref, g_ref, r_ref, o_ref, dg_ref):
        x = h_ref[...]
        r = lax.rsqrt(jnp.mean(x * x, axis=-1, keepdims=True) + EPS)
        xhat = x * r
        d = d_ref[...].astype(F32)
        gy = d * g_ref[...]
        dx = r * (gy - xhat * jnp.mean(gy * xhat, axis=-1, keepdims=True))
        o_ref[...] = r_ref[...] + dx

        @pl.when(pl.program_id(0) == 0)
        def _():
            dg_ref[...] = jnp.zeros_like(dg_ref)

        dg_ref[...] += jnp.sum(d * xhat, axis=0, keepdims=True)

    dh, dg = pl.pallas_call(
        body, name=name, grid=(L // tl,),
        in_specs=[pl.BlockSpec((tl, D), lambda i: (i, 0)), pl.BlockSpec((tl, D), lambda i: (i, 0)),
                  pl.BlockSpec((1, D), lambda i: (0, 0)), pl.BlockSpec((tl, D), lambda i: (i, 0))],
        out_specs=[pl.BlockSpec((tl, D), lambda i: (i, 0)), pl.BlockSpec((1, D), lambda i: (0, 0))],
        out_shape=[_sds((L, D), F32), _sds((1, D), F32)],
        compiler_params=_cparams(5 * _nbytes((tl, D), F32)),
    )(dxn, h, g.reshape(1, D), dres)
    return dh, dg.reshape(D)


def _rope_tables(positions):
    L = positions.shape[0]
    tl = _tile(L, 1024)
    inv = 1.0 / (ROPE_THETA ** (np.arange(0, HEAD_DIM, 2, dtype=np.float32) / HEAD_DIM))
    inv128 = jnp.asarray(np.tile(inv.astype(np.float32), 4).reshape(1, LANES))

    def body(p_ref, i_ref, c_ref, s_ref):
        ang = p_ref[...].astype(F32) * i_ref[...]
        c_ref[...] = jnp.cos(ang)
        s_ref[...] = jnp.sin(ang)

    return pl.pallas_call(
        body, name="rope_tables", grid=(L // tl,),
        in_specs=[pl.BlockSpec((tl, 1), lambda i: (i, 0)), pl.BlockSpec((1, LANES), lambda i: (0, 0))],
        out_specs=[pl.BlockSpec((tl, LANES), lambda i: (i, 0))] * 2,
        out_shape=[_sds((L, LANES), F32)] * 2,
    )(positions.reshape(L, 1), inv128)


_GM_TL = 256


def _gmlp_head(Z, W, bfull, lg, lb, maskv):
    G = _gelu(Z)
    mu = jnp.sum(jnp.where(maskv, G, 0.0), axis=-1, keepdims=True) * (1.0 / HEAD_DIM)
    xc = jnp.where(maskv, G - mu, 0.0)
    var = jnp.sum(xc * xc, axis=-1, keepdims=True) * (1.0 / HEAD_DIM)
    rstd = lax.rsqrt(var + EPS)
    xhat = xc * rstd
    vn = xhat * lg + lb
    sv = _dot(W, vn) + bfull
    return G, xhat, rstd, vn, sv


def _tril(W):
    return jnp.where(_row(W.shape) >= _lane(W.shape), W, 0.0)


def _triu(W):
    return jnp.where(_row(W.shape) <= _lane(W.shape), W, 0.0)


def _gmlp_fwd(z, ws, bfull, lgf, lbf, name):
    L = z.shape[0]
    tl = _tile(L, _GM_TL)
    nch = tl // CHUNK

    def body(z_ref, w_ref, b_ref, lg_ref, lb_ref, o_ref):
        maskv = _lane((CHUNK, LANES)) >= HEAD_DIM
        for c in range(nch):
            rows = slice(c * CHUNK, (c + 1) * CHUNK)
            for hp in range(A_HEADS // 2):
                acc = None
                for hh in range(2):
                    h = 2 * hp + hh
                    Z = z_ref[rows, h * LANES:(h + 1) * LANES]
                    G, _, _, _, sv = _gmlp_head(Z, _tril(w_ref[h]), b_ref[h], lg_ref[h:h + 1, :], lb_ref[h:h + 1, :], maskv)
                    prod = G * pltpu.roll(sv, HEAD_DIM, axis=1)
                    acc = prod if hh == 0 else acc + pltpu.roll(prod, HEAD_DIM, axis=1)
                o_ref[rows, hp * LANES:(hp + 1) * LANES] = acc

    return pl.pallas_call(
        body, name=name, grid=(L // tl,),
        in_specs=[pl.BlockSpec((tl, IN_A), lambda i: (i, 0)),
                  pl.BlockSpec((A_HEADS, CHUNK, CHUNK), lambda i: (0, 0, 0)),
                  pl.BlockSpec((A_HEADS, CHUNK, LANES), lambda i: (0, 0, 0)),
                  pl.BlockSpec((A_HEADS, LANES), lambda i: (0, 0)),
                  pl.BlockSpec((A_HEADS, LANES), lambda i: (0, 0))],
        out_specs=pl.BlockSpec((tl, 2 * LANES), lambda i: (i, 0)),
        out_shape=_sds((L, 2 * LANES), F32),
    )(z, ws, bfull, lgf, lbf)


def _gmlp_bwd(z, dya, ws, wsT, bfull, lgf, lbf, name):
    L = z.shape[0]
    tl = _tile(L, _GM_TL)
    nch = tl // CHUNK
    nsteps = L // tl

    def body(z_ref, d_ref, w_ref, wt_ref, b_ref, lg_ref, lb_ref, dz_ref, dw_ref, db_ref, dlg_ref, dlb_ref):
        step = pl.program_id(0)

        @pl.when(step == 0)
        def _():
            dw_ref[...] = jnp.zeros_like(dw_ref)
            db_ref[...] = jnp.zeros_like(db_ref)
            dlg_ref[...] = jnp.zeros_like(dlg_ref)
            dlb_ref[...] = jnp.zeros_like(dlb_ref)

        lane = _lane((CHUNK, LANES))
        maskv = lane >= HEAD_DIM
        for c in range(nch):
            rows = slice(c * CHUNK, (c + 1) * CHUNK)
            for h in range(A_HEADS):
                hp, hh = divmod(h, 2)
                Z = z_ref[rows, h * LANES:(h + 1) * LANES]
                lg = lg_ref[h:h + 1, :]
                G, xhat, rstd, vn, sv = _gmlp_head(Z, _tril(w_ref[h]), b_ref[h], lg, lb_ref[h:h + 1, :], maskv)
                dpair = d_ref[rows, hp * LANES:(hp + 1) * LANES]
                if hh == 1:
                    dpair = pltpu.roll(dpair, HEAD_DIM, axis=1)
                dout = jnp.where(maskv, 0.0, dpair)
                du = dout * pltpu.roll(sv, HEAD_DIM, axis=1)
                dsv = pltpu.roll(dout * G, HEAD_DIM, axis=1)
                dw_ref[h] += _tril(_dot(dsv, vn, _NT))
                db_ref[h] += dsv
                dvn = _dot(_triu(wt_ref[h]), dsv)
                dlg_ref[h] += dvn * xhat
                dlb_ref[h] += dvn
                dxh = dvn * lg
                m1 = jnp.sum(dxh, axis=-1, keepdims=True) * (1.0 / HEAD_DIM)
                m2 = jnp.sum(dxh * xhat, axis=-1, keepdims=True) * (1.0 / HEAD_DIM)
                dv = jnp.where(maskv, rstd * (dxh - m1 - xhat * m2), 0.0)
                dz_ref[rows, h * LANES:(h + 1) * LANES] = ((du + dv) * _gelu_grad(Z)).astype(dz_ref.dtype)

        @pl.when(step == nsteps - 1)
        def _():
            for h in range(A_HEADS):
                db_ref[h] = jnp.broadcast_to(jnp.sum(db_ref[h], axis=1, keepdims=True), (CHUNK, LANES))
                dlg_ref[h] = jnp.broadcast_to(jnp.sum(dlg_ref[h], axis=0, keepdims=True), (CHUNK, LANES))
                dlb_ref[h] = jnp.broadcast_to(jnp.sum(dlb_ref[h], axis=0, keepdims=True), (CHUNK, LANES))

    full3 = pl.BlockSpec((A_HEADS, CHUNK, LANES), lambda i: (0, 0, 0))
    return pl.pallas_call(
        body, name=name, grid=(nsteps,),
        in_specs=[pl.BlockSpec((tl, IN_A), lambda i: (i, 0)),
                  pl.BlockSpec((tl, 2 * LANES), lambda i: (i, 0)),
                  full3, full3, full3,
                  pl.BlockSpec((A_HEADS, LANES), lambda i: (0, 0)),
                  pl.BlockSpec((A_HEADS, LANES), lambda i: (0, 0))],
        out_specs=[pl.BlockSpec((tl, IN_A), lambda i: (i, 0)), full3, full3, full3, full3],
        out_shape=[_sds((L, IN_A), _ACT)] + [_sds((A_HEADS, CHUNK, LANES), F32)] * 4,
    )(z, dya, ws, wsT, bfull, lgf, lbf)


def _head_rstd(x, lo):
    sq = x * x
    s_lo = jnp.sum(jnp.where(lo, sq, 0.0), axis=-1, keepdims=True)
    s_hi = jnp.sum(jnp.where(lo, 0.0, sq), axis=-1, keepdims=True)
    return jnp.where(lo, lax.rsqrt(s_lo * (1.0 / HEAD_DIM) + EPS), lax.rsqrt(s_hi * (1.0 / HEAD_DIM) + EPS))


def _rot_half(x, first):
    return jnp.where(first, -pltpu.roll(x, LANES - HEAD_DIM // 2, axis=1), pltpu.roll(x, HEAD_DIM // 2, axis=1))


def _qk_prep(z, cos, sin, gq, gk, name):
    L = z.shape[0]
    tl = _tile(L, _TL)
    nq = IN_Q // LANES

    def body(q_ref, k_ref, c_ref, s_ref, gq_ref, gk_ref, qo_ref, ko_ref):
        lane = _lane((tl, LANES))
        lo = lane < HEAD_DIM
        first = (lane % HEAD_DIM) < (HEAD_DIM // 2)
        c, s = c_ref[...], s_ref[...]

        def prep(x, g):
            xn = (x * _head_rstd(x, lo)) * g
            return xn * c + _rot_half(xn, first) * s

        for j in range(nq):
            qo_ref[:, j * LANES:(j + 1) * LANES] = prep(q_ref[:, j * LANES:(j + 1) * LANES], gq_ref[...]).astype(qo_ref.dtype)
        ko_ref[...] = prep(k_ref[...], gk_ref[...]).astype(ko_ref.dtype)

    return pl.pallas_call(
        body, name=name, grid=(L // tl,),
        in_specs=[pl.BlockSpec((tl, IN_Q), lambda i: (i, 1)),
                  pl.BlockSpec((tl, IN_KV), lambda i: (i, 8)),
                  pl.BlockSpec((tl, LANES), lambda i: (i, 0)), pl.BlockSpec((tl, LANES), lambda i: (i, 0)),
                  pl.BlockSpec((1, LANES), lambda i: (0, 0)), pl.BlockSpec((1, LANES), lambda i: (0, 0))],
        out_specs=[pl.BlockSpec((tl, IN_Q), lambda i: (i, 0)), pl.BlockSpec((tl, IN_KV), lambda i: (i, 0))],
        out_shape=[_sds((L, IN_Q), _ACT), _sds((L, IN_KV), _ACT)],
    )(z, z, cos, sin, gq, gk)


def _qk_prep_bwd(z, dq, dkc, dkp, dvc, dvp, cos, sin, gq, gk, name):
    L = z.shape[0]
    tl = _ATT_QB * WINDOW
    nb = L // tl
    nq = IN_Q // LANES

    def body(q_ref, k_ref, dq_ref, dkc_ref, dkp_ref, dvc_ref, dvp_ref, c_ref, s_ref, gq_ref, gk_ref,
             dzq_ref, dzk_ref, dzv_ref, dgq_ref, dgk_ref):
        n = pl.program_id(0)

        @pl.when(n == 0)
        def _():
            dgq_ref[...] = jnp.zeros_like(dgq_ref)
            dgk_ref[...] = jnp.zeros_like(dgk_ref)

        lane = _lane((tl, LANES))
        lo = lane < HEAD_DIM
        first = (lane % HEAD_DIM) < (HEAD_DIM // 2)
        c, s = c_ref[...], s_ref[...]
        has_next = jnp.where(n < nb - 1, 1.0, 0.0)

        def bwd(x, g, dy):
            r = _head_rstd(x, lo)
            xhat = x * r
            dxn = dy * c - _rot_half(dy * s, first)
            gy = dxn * g
            t = gy * xhat
            m_lo = jnp.sum(jnp.where(lo, t, 0.0), axis=-1, keepdims=True)
            m_hi = jnp.sum(jnp.where(lo, 0.0, t), axis=-1, keepdims=True)
            m = jnp.where(lo, m_lo, m_hi) * (1.0 / HEAD_DIM)
            dx = r * (gy - xhat * m)
            dg = jnp.sum(dxn * xhat, axis=0, keepdims=True)
            return dx, dg

        dgq = jnp.zeros((1, LANES), F32)
        for j in range(nq):
            sl = slice(j * LANES, (j + 1) * LANES)
            dx, dg = bwd(q_ref[:, sl], gq_ref[...], dq_ref[:, sl].astype(F32))
            dzq_ref[:, sl] = dx.astype(dzq_ref.dtype)
            dgq = dgq + dg
        dgq_ref[...] += dgq + pltpu.roll(dgq, HEAD_DIM, axis=1)
        def with_next(cur_ref, nxt_ref):
            head = jnp.zeros((tl - WINDOW, IN_KV), F32)
            return cur_ref[...] + jnp.concatenate([head, has_next * nxt_ref[...]], axis=0)

        dx, dg = bwd(k_ref[...], gk_ref[...], with_next(dkc_ref, dkp_ref))
        dzk_ref[...] = dx.astype(dzk_ref.dtype)
        dgk_ref[...] += dg + pltpu.roll(dg, HEAD_DIM, axis=1)
        dzv_ref[...] = with_next(dvc_ref, dvp_ref).astype(dzv_ref.dtype)

    nxt = lambda i: (jnp.minimum(i + 1, nb - 1), 0)
    cur = lambda i: (i, 0)
    kv = pl.BlockSpec((tl, IN_KV), cur)
    kvn = pl.BlockSpec((WINDOW, IN_KV), nxt)
    one = pl.BlockSpec((1, LANES), lambda i: (0, 0))
    return pl.pallas_call(
        body, name=name, grid=(nb,),
        in_specs=[pl.BlockSpec((tl, IN_Q), lambda i: (i, 1)), pl.BlockSpec((tl, IN_KV), lambda i: (i, 8)),
                  pl.BlockSpec((tl, IN_Q), cur), kv, kvn, kv, kvn,
                  kv, kv, one, one],
        out_specs=[pl.BlockSpec((tl, IN_Q), cur), kv, kv, one, one],
        out_shape=[_sds((L, IN_Q), _ACT), _sds((L, IN_KV), _ACT),
                   _sds((L, IN_KV), _ACT), _sds((1, LANES), F32),
                   _sds((1, LANES), F32)],
    )(z, z, dq, dkc, dkp, dvc, dvp, cos, sin, gq, gk)


def _attn_mask(n):
    shp = (2 * WINDOW, B_GROUP * WINDOW)
    qi = _lane(shp) % WINDOW
    kj = _row(shp)
    off = 0 if n is None else jnp.where(n > 0, 0, 4 * WINDOW)
    return ((kj >= WINDOW) & (kj - WINDOW <= qi)) | ((kj < WINDOW) & (kj > qi + off))


def _kv_lanes(j):
    lane = _lane((WINDOW, LANES))
    return (lane >= j * HEAD_DIM) & (lane < (j + 1) * HEAD_DIM)


_ATT_QB = 4


def _stack_heads(ref, rows, j, kvl):
    parts = []
    for g in range(B_GROUP):
        h = j * B_GROUP + g
        slab = ref[rows, (h // 2) * LANES:(h // 2 + 1) * LANES].astype(F32)
        if (h % 2) != j:
            slab = pltpu.roll(slab, HEAD_DIM, axis=1)
        parts.append(jnp.where(kvl, slab, 0.0))
    return jnp.concatenate(parts, axis=0)


def _attn_probs(qs, k2, sink_row, mask):
    s = _dot(k2, qs, _NT) * (HEAD_DIM ** -0.5)
    s = jnp.where(mask, s, NEG)
    m = jnp.maximum(jnp.max(s, axis=0, keepdims=True), sink_row)
    p = jnp.exp(s - m)
    esink = jnp.exp(sink_row - m)
    inv = 1.0 / (jnp.sum(p, axis=0, keepdims=True) + esink)
    return p * inv, esink * inv


def _sink_row(sink_ref, j):
    lane = _lane((1, B_GROUP * WINDOW))
    row = jnp.full((1, B_GROUP * WINDOW), sink_ref[j * B_GROUP], F32)
    for g in range(1, B_GROUP):
        row = jnp.where(lane >= g * WINDOW, sink_ref[j * B_GROUP + g], row)
    return row


def _attn_fwd(q, k, z, sinks, name):
    L = q.shape[0]
    QB = _ATT_QB
    tq = QB * WINDOW
    prev = lambda n: (jnp.maximum(QB * n - 1, 0), 0)
    prev_v = lambda n: (jnp.maximum(QB * n - 1, 0), 9)

    def body(s_ref, q_ref, kp_ref, kc_ref, vp_ref, vc_ref, o_ref):
        n = pl.program_id(0)
        k3 = jnp.concatenate([kp_ref[...], kc_ref[...]], axis=0)
        v3 = jnp.concatenate([vp_ref[...], vc_ref[...]], axis=0)
        for b in range(QB):
            rows = slice(b * WINDOW, (b + 1) * WINDOW)
            mask = _attn_mask(n if b == 0 else None)
            k2 = k3[b * WINDOW:(b + 2) * WINDOW]
            v2 = v3[b * WINDOW:(b + 2) * WINDOW]
            slabs = [None] * (IN_Q // LANES)
            for j in range(B_KV_HEADS):
                kvl = _kv_lanes(j)
                qs = _stack_heads(q_ref, rows, j, kvl)
                pn, _ = _attn_probs(qs, k2, _sink_row(s_ref, j), mask)
                o = _dot(pn, v2, _TN)
                for g in range(B_GROUP):
                    h = j * B_GROUP + g
                    piece = jnp.where(kvl, o[g * WINDOW:(g + 1) * WINDOW], 0.0)
                    if (h % 2) != j:
                        piece = pltpu.roll(piece, HEAD_DIM, axis=1)
                    slabs[h // 2] = piece if slabs[h // 2] is None else slabs[h // 2] + piece
            for t, sl in enumerate(slabs):
                o_ref[rows, t * LANES:(t + 1) * LANES] = sl

    return pl.pallas_call(
        body, name=name, grid=(L // tq,),
        in_specs=[pl.BlockSpec(memory_space=pltpu.SMEM),
                  pl.BlockSpec((tq, IN_Q), lambda n: (n, 0)),
                  pl.BlockSpec((WINDOW, IN_KV), prev), pl.BlockSpec((tq, IN_KV), lambda n: (n, 0)),
                  pl.BlockSpec((WINDOW, IN_KV), prev_v), pl.BlockSpec((tq, IN_KV), lambda n: (n, 9))],
        out_specs=pl.BlockSpec((tq, IN_Q), lambda n: (n, 0)),
        out_shape=_sds((L, IN_Q), F32),
    )(sinks, q, k, k, z, z)


def _attn_bwd(q, k, z, sinks, dyb, name):
    L = q.shape[0]
    QB = _ATT_QB
    tq = QB * WINDOW
    nsteps = L // tq
    prev = lambda n: (jnp.maximum(QB * n - 1, 0), 0)
    prev_v = lambda n: (jnp.maximum(QB * n - 1, 0), 9)
    cur = lambda n: (n, 0)

    def body(s_ref, q_ref, kp_ref, kc_ref, vp_ref, vc_ref, d_ref, dq_ref, dkc_ref, dkp_ref, dvc_ref, dvp_ref, ds_ref):
        n = pl.program_id(0)

        @pl.when(n == 0)
        def _():
            ds_ref[...] = jnp.zeros_like(ds_ref)

        k3 = jnp.concatenate([kp_ref[...], kc_ref[...]], axis=0)
        v3 = jnp.concatenate([vp_ref[...], vc_ref[...]], axis=0)
        dkb = [None] * (QB + 1)
        dvb = [None] * (QB + 1)
        dsink = jnp.zeros((1, LANES), F32)
        lane1 = _lane((1, LANES))
        add = lambda acc, v: v if acc is None else acc + v
        for b in range(QB):
            rows = slice(b * WINDOW, (b + 1) * WINDOW)
            mask = _attn_mask(n if b == 0 else None)
            k2 = k3[b * WINDOW:(b + 2) * WINDOW]
            v2 = v3[b * WINDOW:(b + 2) * WINDOW]
            slabs = [None] * (IN_Q // LANES)
            for j in range(B_KV_HEADS):
                kvl = _kv_lanes(j)
                qs = _stack_heads(q_ref, rows, j, kvl)
                dos = _stack_heads(d_ref, rows, j, kvl)
                pn, psink = _attn_probs(qs, k2, _sink_row(s_ref, j), mask)
                dp = _dot(v2, dos, _NT)
                dd = jnp.sum(pn * dp, axis=0, keepdims=True)
                dss = (pn * (dp - dd)) * (HEAD_DIM ** -0.5)
                dqs = _dot(dss, k2, _TN)
                dk2 = _dot(dss, qs)
                dv2 = _dot(pn, dos)
                dkb[b], dkb[b + 1] = add(dkb[b], dk2[:WINDOW]), add(dkb[b + 1], dk2[WINDOW:])
                dvb[b], dvb[b + 1] = add(dvb[b], dv2[:WINDOW]), add(dvb[b + 1], dv2[WINDOW:])
                sd = psink * dd
                for g in range(B_GROUP):
                    h = j * B_GROUP + g
                    piece = jnp.where(kvl, dqs[g * WINDOW:(g + 1) * WINDOW], 0.0)
                    if (h % 2) != j:
                        piece = pltpu.roll(piece, HEAD_DIM, axis=1)
                    slabs[h // 2] = piece if slabs[h // 2] is None else slabs[h // 2] + piece
                    tot = jnp.sum(sd[:, g * WINDOW:(g + 1) * WINDOW], axis=1, keepdims=True)
                    dsink = dsink - jnp.where(lane1 == h, tot, 0.0)
            for t, sl in enumerate(slabs):
                dq_ref[rows, t * LANES:(t + 1) * LANES] = sl
        dkp_ref[...] = dkb[0]
        dvp_ref[...] = dvb[0]
        for b in range(QB):
            dkc_ref[b * WINDOW:(b + 1) * WINDOW, :] = dkb[b + 1]
            dvc_ref[b * WINDOW:(b + 1) * WINDOW, :] = dvb[b + 1]
        ds_ref[0:1, :] += dsink

    kvs = pl.BlockSpec((tq, IN_KV), cur)
    kvp = pl.BlockSpec((WINDOW, IN_KV), cur)
    kvo = _sds((L, IN_KV), F32)
    kvpo = _sds((nsteps * WINDOW, IN_KV), F32)
    return pl.pallas_call(
        body, name=name, grid=(nsteps,),
        in_specs=[pl.BlockSpec(memory_space=pltpu.SMEM),
                  pl.BlockSpec((tq, IN_Q), cur),
                  pl.BlockSpec((WINDOW, IN_KV), prev), kvs,
                  pl.BlockSpec((WINDOW, IN_KV), prev_v), pl.BlockSpec((tq, IN_KV), lambda n: (n, 9)),
                  pl.BlockSpec((tq, IN_Q), cur)],
        out_specs=[pl.BlockSpec((tq, IN_Q), cur), kvs, kvp, kvs, kvp, pl.BlockSpec((SUBLANES, LANES), lambda n: (0, 0))],
        out_shape=[_sds((L, IN_Q), F32), kvo, kvpo, kvo, kvpo, _sds((SUBLANES, LANES), F32)],
    )(sinks, q, k, k, z, z, dyb)


def _ssm_disc(are, aim, ldt, bre, bim):
    dt = jnp.exp(ldt)
    mag = jnp.exp(are * dt)
    lr, li = mag * jnp.cos(aim * dt), mag * jnp.sin(aim * dt)
    den = are * are + aim * aim
    xr, xi = lr - 1.0, li
    cr, ci = (xr * are + xi * aim) / den, (xi * are - xr * aim) / den
    return lr, li, cr * bre - ci * bim, cr * bim + ci * bre


def _ssm_prep(are, aim, ldt, bre, bim):
    shp3, shpb = are.shape, bre.shape

    def body(are_ref, aim_ref, ldt_ref, bre_ref, bim_ref, lr_ref, li_ref, br_ref, bi_ref):
        lr, li, br, bi = _ssm_disc(are_ref[...], aim_ref[...], ldt_ref[...], bre_ref[...], bim_ref[...])
        lr_ref[...] = lr
        li_ref[...] = li
        br_ref[...] = br
        bi_ref[...] = bi

    return pl.pallas_call(
        body, name="ssm_prep",
        out_shape=[_sds(shp3, F32)] * 2 + [_sds(shpb, F32)] * 2,
    )(are, aim, ldt, bre, bim)


def _ssm_prep_bwd(are, aim, ldt, bre, bim, dlr, dli, dbr, dbi):
    shp3, shpb = are.shape, bre.shape

    def body(are_ref, aim_ref, ldt_ref, bre_ref, bim_ref, dlr_ref, dli_ref, dbr_ref, dbi_ref,
             o_are, o_aim, o_ldt, o_bre, o_bim):
        _, vjp = jax.vjp(_ssm_disc, are_ref[...], aim_ref[...], ldt_ref[...], bre_ref[...], bim_ref[...])
        g = vjp((dlr_ref[...], dli_ref[...], dbr_ref[...], dbi_ref[...]))
        o_are[...] = g[0]
        o_aim[...] = g[1]
        o_ldt[...] = jnp.broadcast_to(jnp.sum(g[2], axis=-1, keepdims=True), shp3)
        o_bre[...] = g[3]
        o_bim[...] = g[4]

    return pl.pallas_call(
        body, name="ssm_prep_bwd",
        out_shape=[_sds(shp3, F32)] * 3 + [_sds(shpb, F32)] * 2,
    )(are, aim, ldt, bre, bim, dlr, dli, dbr, dbi)


_SCAN_TB = 512
_SCAN_W = 512


def _cmul(ar, ai, br, bi):
    return ar * br - ai * bi, ar * bi + ai * br


def _ssm_scan(x, lam_r, lam_i, name, reverse=False, states=None):
    L = x.shape[0]
    tb = _tile(L, _SCAN_TB)
    nrb = L // tb
    nt = tb // SUBLANES
    W = _SCAN_W
    with_da = states is not None

    def body(*refs):
        if with_da:
            xr_ref, xi_ref, sr_ref, si_ref, ar_ref, ai_ref, o_ref, dar_ref, dai_ref, cr_ref, ci_ref = refs
        else:
            xr_ref, xi_ref, ar_ref, ai_ref, o_ref, cr_ref, ci_ref = refs
        step = pl.program_id(0)

        @pl.when(step == 0)
        def _():
            cr_ref[...] = jnp.zeros_like(cr_ref)
            ci_ref[...] = jnp.zeros_like(ci_ref)
            if with_da:
                dar_ref[...] = jnp.zeros_like(dar_ref)
                dai_ref[...] = jnp.zeros_like(dai_ref)

        row = _row((SUBLANES, W))

        def shift(v, d, fill):
            if reverse:
                return jnp.where(row < SUBLANES - d, pltpu.roll(v, SUBLANES - d, axis=0), fill)
            return jnp.where(row >= d, pltpu.roll(v, d, axis=0), fill)

        edge = 0 if reverse else SUBLANES - 1
        for wb in range(N_STATE // W):
            cols = slice(wb * W, (wb + 1) * W)
            a1r = jnp.broadcast_to(ar_ref[:, cols], (SUBLANES, W))
            a1i = jnp.broadcast_to(ai_ref[:, cols], (SUBLANES, W))
            if reverse:
                a1i = -a1i
            a2r, a2i = _cmul(a1r, a1i, a1r, a1i)
            a4r, a4i = _cmul(a2r, a2i, a2r, a2i)
            pws = ((1, a1r, a1i), (2, a2r, a2i), (4, a4r, a4i))
            pr, pi = a1r, a1i
            for d, _, _ in pws:
                qr, qi = _cmul(pr, pi, shift(pr, d, 1.0), shift(pi, d, 0.0))
                pr, pi = qr, qi
            mws = []
            for d, er, ei in pws:
                ok = (row < SUBLANES - d) if reverse else (row >= d)
                mws.append(((SUBLANES - d) if reverse else d, jnp.where(ok, er, 0.0), jnp.where(ok, ei, 0.0)))

            def tile(i, carry):
                cr, ci, dr, di = carry
                t = (nt - 1 - i) if reverse else i
                r0 = pl.multiple_of(t * SUBLANES, SUBLANES)
                vr = xr_ref[pl.ds(r0, SUBLANES), cols]
                vi = xi_ref[pl.ds(r0, SUBLANES), cols]
                for sh, er, ei in mws:
                    tr, ti = _cmul(er, ei, pltpu.roll(vr, sh, axis=0), pltpu.roll(vi, sh, axis=0))
                    vr, vi = vr + tr, vi + ti
                tr, ti = _cmul(pr, pi, cr, ci)
                vr, vi = vr + tr, vi + ti
                o_ref[pl.ds(r0, SUBLANES), cols] = vr
                o_ref[pl.ds(r0, SUBLANES), slice(N_STATE + wb * W, N_STATE + (wb + 1) * W)] = vi
                if with_da:
                    gr = jnp.where(row < SUBLANES - 1, pltpu.roll(vr, SUBLANES - 1, axis=0), cr)
                    gi = jnp.where(row < SUBLANES - 1, pltpu.roll(vi, SUBLANES - 1, axis=0), ci)
                    sr = sr_ref[pl.ds(r0, SUBLANES), cols]
                    si = si_ref[pl.ds(r0, SUBLANES), cols]
                    dr = dr + sr * gr + si * gi
                    di = di + sr * gi - si * gr
                ncr = jnp.broadcast_to(vr[edge:edge + 1, :], (SUBLANES, W))
                nci = jnp.broadcast_to(vi[edge:edge + 1, :], (SUBLANES, W))
                return ncr, nci, dr, di

            zero = jnp.zeros((SUBLANES, W), F32)
            cr, ci, dr, di = lax.fori_loop(0, nt, tile, (cr_ref[:, cols], ci_ref[:, cols], zero, zero), unroll=2)
            cr_ref[:, cols] = cr
            ci_ref[:, cols] = ci
            if with_da:
                dar_ref[:, cols] += dr
                dai_ref[:, cols] += di

        if with_da:
            @pl.when(step == nrb - 1)
            def _():
                dar_ref[...] = jnp.broadcast_to(jnp.sum(dar_ref[...], axis=0, keepdims=True), dar_ref.shape)
                dai_ref[...] = jnp.broadcast_to(jnp.sum(dai_ref[...], axis=0, keepdims=True), dai_ref.shape)

    rb = (lambda i: (nrb - 1 - i, 0)) if reverse else (lambda i: (i, 0))
    rb_im = (lambda i: (nrb - 1 - i, 1)) if reverse else (lambda i: (i, 1))
    blk_r = pl.BlockSpec((tb, N_STATE), rb)
    blk_i = pl.BlockSpec((tb, N_STATE), rb_im)
    one = pl.BlockSpec((1, N_STATE), lambda i: (0, 0))
    acc = pl.BlockSpec((SUBLANES, N_STATE), lambda i: (0, 0))
    ins = [x, x] + ([states, states] if with_da else []) + [lam_r, lam_i]
    in_specs = [blk_r, blk_i] + ([blk_r, blk_i] if with_da else []) + [one, one]
    out_specs = [pl.BlockSpec((tb, 2 * N_STATE), rb)] + ([acc, acc] if with_da else [])
    out_shape = [_sds((L, 2 * N_STATE), F32)] + (
        [_sds((SUBLANES, N_STATE), F32)] * 2 if with_da else [])
    outs = pl.pallas_call(
        body, name=name, grid=(nrb,), in_specs=in_specs, out_specs=out_specs, out_shape=out_shape,
        scratch_shapes=[pltpu.VMEM((SUBLANES, N_STATE), F32)] * 2,
        compiler_params=_cparams((6 if with_da else 4) * _nbytes((tb, N_STATE), F32),
                                 dimension_semantics=("arbitrary",)),
    )(*ins)
    return outs if with_da else outs[0]


def _scan_block(x_ref, o_ref, s_ref, ar_ref, ai_ref, cr_ref, ci_ref, dar_ref, dai_ref, nt, reverse):
    W = _SCAN_W
    with_da = s_ref is not None
    row = _row((SUBLANES, W))

    def shift(v, d, fill):
        if reverse:
            return jnp.where(row < SUBLANES - d, pltpu.roll(v, SUBLANES - d, axis=0), fill)
        return jnp.where(row >= d, pltpu.roll(v, d, axis=0), fill)

    edge = 0 if reverse else SUBLANES - 1
    for wb in range(N_STATE // W):
        cols = slice(wb * W, (wb + 1) * W)
        icols = slice(N_STATE + wb * W, N_STATE + (wb + 1) * W)
        a1r = jnp.broadcast_to(ar_ref[:, cols], (SUBLANES, W))
        a1i = jnp.broadcast_to(ai_ref[:, cols], (SUBLANES, W))
        if reverse:
            a1i = -a1i
        a2r, a2i = _cmul(a1r, a1i, a1r, a1i)
        a4r, a4i = _cmul(a2r, a2i, a2r, a2i)
        pws = ((1, a1r, a1i), (2, a2r, a2i), (4, a4r, a4i))
        pr, pi = a1r, a1i
        for d, _, _ in pws:
            qr, qi = _cmul(pr, pi, shift(pr, d, 1.0), shift(pi, d, 0.0))
            pr, pi = qr, qi
        mws = []
        for d, er, ei in pws:
            ok = (row < SUBLANES - d) if reverse else (row >= d)
            mws.append(((SUBLANES - d) if reverse else d, jnp.where(ok, er, 0.0), jnp.where(ok, ei, 0.0)))

        def tile(i, carry):
            cr, ci, dr, di = carry
            t = (nt - 1 - i) if reverse else i
            r0 = pl.multiple_of(t * SUBLANES, SUBLANES)
            vr = x_ref[pl.ds(r0, SUBLANES), cols]
            vi = x_ref[pl.ds(r0, SUBLANES), icols]
            for sh, er, ei in mws:
                tr, ti = _cmul(er, ei, pltpu.roll(vr, sh, axis=0), pltpu.roll(vi, sh, axis=0))
                vr, vi = vr + tr, vi + ti
            tr, ti = _cmul(pr, pi, cr, ci)
            vr, vi = vr + tr, vi + ti
            o_ref[pl.ds(r0, SUBLANES), cols] = vr
            o_ref[pl.ds(r0, SUBLANES), icols] = vi
            if with_da:
                gr = jnp.where(row < SUBLANES - 1, pltpu.roll(vr, SUBLANES - 1, axis=0), cr)
                gi = jnp.where(row < SUBLANES - 1, pltpu.roll(vi, SUBLANES - 1, axis=0), ci)
                sr = s_ref[pl.ds(r0, SUBLANES), cols]
                si = s_ref[pl.ds(r0, SUBLANES), icols]
                dr = dr + sr * gr + si * gi
                di = di + sr * gi - si * gr
            ncr = jnp.broadcast_to(vr[edge:edge + 1, :], (SUBLANES, W))
            nci = jnp.broadcast_to(vi[edge:edge + 1, :], (SUBLANES, W))
            return ncr, nci, dr, di

        zero = jnp.zeros((SUBLANES, W), F32)
        cr, ci, dr, di = lax.fori_loop(0, nt, tile, (cr_ref[:, cols], ci_ref[:, cols], zero, zero), unroll=2)
        cr_ref[:, cols] = cr
        ci_ref[:, cols] = ci
        if with_da:
            dar_ref[:, cols] += dr
            dai_ref[:, cols] += di


def _carry(comm, operands, in_specs, out_shape, out_specs, aliases, scratch):
    if comm is None:
        return 0, 0
    n0, no0 = len(operands), len(out_shape)
    operands += list(comm.ins)
    in_specs += [pl.BlockSpec(memory_space=pl.ANY)] * len(comm.ins)
    for t, x_ in enumerate(comm.ins):
        if comm.aliased[t]:
            aliases[n0 + t] = len(out_shape)
            out_shape.append(_sds(x_.shape, x_.dtype))
    out_shape += [_sds(sh, dt) for sh, dt in comm.fresh]
    out_specs += [pl.BlockSpec(memory_space=pl.ANY)] * (len(out_shape) - no0)
    scratch += [pltpu.SemaphoreType.DMA((comm.n_sems,))] * 2
    return len(comm.ins), len(out_shape) - no0


def _ssm_fwd(z, bcat, ccat, dskip, lam_r, lam_i, name, comm=None):
    L = z.shape[0]
    tb = _tile(L, _SCAN_TB)
    nrb = L // tb
    nt = tb // SUBLANES
    full = lambda shp: pl.BlockSpec(shp, lambda i: (0, 0))
    rows = lambda w: pl.BlockSpec((tb, w), lambda i: (i, 0))
    operands = [_hbm(z), bcat, ccat, dskip, lam_r, lam_i]
    in_specs = [pl.BlockSpec((tb, C_WIDTH), lambda i: (i, 5)), full((C_WIDTH, 2 * N_STATE)), full((2 * N_STATE, C_WIDTH)),
                full((1, C_WIDTH)), full((1, N_STATE)), full((1, N_STATE))]
    out_specs = [rows(2 * N_STATE), rows(C_WIDTH), rows(C_WIDTH)]
    out_shape = [_sds((L, 2 * N_STATE), F32), _sds((L, C_WIDTH), F32), _sds((L, C_WIDTH), _ACT)]
    scratch = [pltpu.VMEM((tb, 2 * N_STATE), F32)] + [pltpu.VMEM((SUBLANES, N_STATE), F32)] * 2
    aliases = {}
    nxi, nxo = _carry(comm, operands, in_specs, out_shape, out_specs, aliases, scratch)

    def body(*refs):
        u_ref, b_ref, c_ref, d_ref, ar_ref, ai_ref = refs[:6]
        x_ins = refs[6:6 + nxi]
        s_ref, y_ref, yg_ref = refs[6 + nxi:9 + nxi]
        x_outs = refs[9 + nxi:9 + nxi + nxo]
        xs_ref, cr_ref, ci_ref = refs[9 + nxi + nxo:12 + nxi + nxo]
        sems = refs[12 + nxi + nxo:]
        step = pl.program_id(0)

        @pl.when(step == 0)
        def _():
            cr_ref[...] = jnp.zeros_like(cr_ref)
            ci_ref[...] = jnp.zeros_like(ci_ref)
            if comm is not None:
                comm.start(x_ins, x_outs, *sems)

        u = u_ref[...]
        xs_ref[...] = _dot(u, b_ref[...])
        _scan_block(xs_ref, s_ref, None, ar_ref, ai_ref, cr_ref, ci_ref, None, None, nt, False)
        y = _dot(s_ref[...], c_ref[...]) + d_ref[...] * u
        y_ref[...] = y
        yg_ref[...] = _gelu(y).astype(yg_ref.dtype)

        if comm is not None:
            @pl.when(step == nrb - 1)
            def _():
                comm.wait(x_ins, x_outs, *sems)

    outs = pl.pallas_call(
        body, name=name, grid=(nrb,), in_specs=in_specs, out_specs=out_specs, out_shape=out_shape,
        scratch_shapes=scratch, input_output_aliases=aliases,
        compiler_params=_cparams(5 * _nbytes((tb, 2 * N_STATE), F32), dimension_semantics=("arbitrary",)),
    )(*operands)
    if comm is not None:
        comm.done(list(outs[3:]))
    return outs[0], outs[1], outs[2]


def _ssm_bwd(dy, z, S, bcat, ccat, dskip, lam_r, lam_i, name, comm=None):
    L = z.shape[0]
    tb = _tile(L, _SCAN_TB)
    nrb = L // tb
    nt = tb // SUBLANES

    full = lambda shp: pl.BlockSpec(shp, lambda i: (0, 0))
    rows = lambda w, col=0: pl.BlockSpec((tb, w), lambda i: (nrb - 1 - i, col))
    acc = full((SUBLANES, N_STATE))
    operands = [dy, _hbm(z), _hbm(S), bcat, ccat, dskip, lam_r, lam_i]
    in_specs = [rows(C_WIDTH), rows(C_WIDTH, 5), rows(2 * N_STATE), full((C_WIDTH, 2 * N_STATE)),
                full((2 * N_STATE, C_WIDTH)), full((1, C_WIDTH)), full((1, N_STATE)), full((1, N_STATE))]
    out_specs = [rows(C_WIDTH), full((C_WIDTH, 2 * N_STATE)), full((2 * N_STATE, C_WIDTH)), full((1, C_WIDTH)), acc, acc]
    out_shape = [_sds((L, C_WIDTH), _ACT), _sds((C_WIDTH, 2 * N_STATE), F32), _sds((2 * N_STATE, C_WIDTH), F32),
                 _sds((1, C_WIDTH), F32), _sds((SUBLANES, N_STATE), F32), _sds((SUBLANES, N_STATE), F32)]
    scratch = [pltpu.VMEM((tb, 2 * N_STATE), F32)] * 2 + [pltpu.VMEM((SUBLANES, N_STATE), F32)] * 2
    aliases = {}
    nxi, nxo = _carry(comm, operands, in_specs, out_shape, out_specs, aliases, scratch)

    def body(*refs):
        dy_ref, u_ref, s_ref, b_ref, c_ref, d_ref, ar_ref, ai_ref = refs[:8]
        x_ins = refs[8:8 + nxi]
        du_ref, db_ref, dc_ref, dd_ref, dar_ref, dai_ref = refs[8 + nxi:14 + nxi]
        x_outs = refs[14 + nxi:14 + nxi + nxo]
        xs_ref, gs_ref, cr_ref, ci_ref = refs[14 + nxi + nxo:18 + nxi + nxo]
        sems = refs[18 + nxi + nxo:]
        step = pl.program_id(0)

        @pl.when(step == 0)
        def _():
            for r in (cr_ref, ci_ref, db_ref, dc_ref, dd_ref, dar_ref, dai_ref):
                r[...] = jnp.zeros_like(r)
            if comm is not None:
                comm.start(x_ins, x_outs, *sems)

        dyv, u = dy_ref[...], u_ref[...]
        xs_ref[...] = _dot(dyv, c_ref[...], _NT)
        _scan_block(xs_ref, gs_ref, s_ref, ar_ref, ai_ref, cr_ref, ci_ref, dar_ref, dai_ref, nt, True)
        g = gs_ref[...]
        du_ref[...] = (_dot(g, b_ref[...], _NT) + dyv * d_ref[...]).astype(du_ref.dtype)
        db_ref[...] += _dot(u, g, _TN)
        dc_ref[...] += _dot(s_ref[...], dyv, _TN)
        dd_ref[...] += jnp.sum(dyv * u, axis=0, keepdims=True)

        @pl.when(step == nrb - 1)
        def _():
            dar_ref[...] = jnp.broadcast_to(jnp.sum(dar_ref[...], axis=0, keepdims=True), dar_ref.shape)
            dai_ref[...] = jnp.broadcast_to(jnp.sum(dai_ref[...], axis=0, keepdims=True), dai_ref.shape)
            if comm is not None:
                comm.wait(x_ins, x_outs, *sems)

    outs = pl.pallas_call(
        body, name=name, grid=(nrb,), in_specs=in_specs, out_specs=out_specs, out_shape=out_shape,
        scratch_shapes=scratch, input_output_aliases=aliases,
        compiler_params=_cparams(7 * _nbytes((tb, 2 * N_STATE), F32), dimension_semantics=("arbitrary",)),
    )(*operands)
    if comm is not None:
        comm.done(list(outs[6:]))
    return tuple(outs[:6])


_GROUPS = ((0, 256), (256, 768), (768, 1024))


def _merge_fwd(ya, yb, g12, mixg, name):
    L = ya.shape[0]
    tl = _tile(L, _TL)

    def body(a_ref, b_ref, g_ref, m_ref, o_ref):
        g12v = g_ref[...]
        yc = g12v[:, :C_WIDTH] * _sigmoid(g12v[:, C_WIDTH:])
        for (lo, hi), y in zip(_GROUPS, (a_ref[...], b_ref[...], yc)):
            r = lax.rsqrt(jnp.mean(y * y, axis=-1, keepdims=True) + EPS)
            o_ref[:, lo:hi] = ((y * r) * m_ref[:, lo:hi]).astype(o_ref.dtype)

    row = lambda w: pl.BlockSpec((tl, w), lambda i: (i, 0))
    return pl.pallas_call(
        body, name=name, grid=(L // tl,),
        in_specs=[row(256), row(512), row(512), pl.BlockSpec((1, D_MODEL), lambda i: (0, 0))],
        out_specs=row(D_MODEL), out_shape=_sds((L, D_MODEL), _ACT),
    )(ya, yb, g12, mixg.reshape(1, D_MODEL))


def _merge_bwd(dy, ya, yb, g12, mixg, name):
    L = ya.shape[0]
    tl = _tile(L, _TL)

    def body(d_ref, a_ref, b_ref, g_ref, m_ref, da_ref, db_ref, dg_ref, dm_ref):
        @pl.when(pl.program_id(0) == 0)
        def _():
            dm_ref[...] = jnp.zeros_like(dm_ref)

        g12v = g_ref[...]
        g1, sg = g12v[:, :C_WIDTH], _sigmoid(g12v[:, C_WIDTH:])
        yc = g1 * sg
        outs = []
        for (lo, hi), y in zip(_GROUPS, (a_ref[...], b_ref[...], yc)):
            r = lax.rsqrt(jnp.mean(y * y, axis=-1, keepdims=True) + EPS)
            xhat = y * r
            d = d_ref[:, lo:hi]
            gy = d * m_ref[:, lo:hi]
            outs.append(r * (gy - xhat * jnp.mean(gy * xhat, axis=-1, keepdims=True)))
            dm_ref[:, lo:hi] += jnp.sum(d * xhat, axis=0, keepdims=True)
        da_ref[...] = outs[0]
        db_ref[...] = outs[1]
        dyc = outs[2]
        dg_ref[:, :C_WIDTH] = (dyc * sg).astype(dg_ref.dtype)
        dg_ref[:, C_WIDTH:] = (dyc * g1 * sg * (1.0 - sg)).astype(dg_ref.dtype)

    row = lambda w: pl.BlockSpec((tl, w), lambda i: (i, 0))
    one = pl.BlockSpec((1, D_MODEL), lambda i: (0, 0))
    return pl.pallas_call(
        body, name=name, grid=(L // tl,),
        in_specs=[row(D_MODEL), row(256), row(512), row(512), one],
        out_specs=[row(256), row(512), row(512), one],
        out_shape=[_sds((L, 256), F32), _sds((L, 512), F32),
                   _sds((L, 512), _ACT), _sds((1, D_MODEL), F32)],
    )(dy, ya, yb, g12, mixg.reshape(1, D_MODEL))


def _ple_bwd_elem(dh, gate, e, name):
    L, D = dh.shape
    tl = _tile(L, _TL)

    def body(d_ref, g_ref, e_ref, p_ref, o_ref):
        d, g = d_ref[...], g_ref[...]
        p_ref[...] = (d * e_ref[...] * g * (1.0 - g)).astype(p_ref.dtype)
        o_ref[...] = (d * g).astype(o_ref.dtype)

    row = pl.BlockSpec((tl, D), lambda i: (i, 0))
    return pl.pallas_call(
        body, name=name, grid=(L // tl,), in_specs=[row] * 3, out_specs=[row] * 2,
        out_shape=[_sds((L, D), _ACT)] * 2,
        compiler_params=_cparams(4 * _nbytes((tl, D), F32)),
    )(dh, gate, e)


def _dskip_bwd(dy, z, name):
    L = dy.shape[0]
    tl = _tile(L, _TL)

    def body(d_ref, u_ref, o_ref):
        @pl.when(pl.program_id(0) == 0)
        def _():
            o_ref[...] = jnp.zeros_like(o_ref)

        o_ref[...] += jnp.sum(d_ref[...] * u_ref[...], axis=0, keepdims=True)

    return pl.pallas_call(
        body, name=name, grid=(L // tl,),
        in_specs=[pl.BlockSpec((tl, C_WIDTH), lambda i: (i, 0)), pl.BlockSpec((tl, C_WIDTH), lambda i: (i, 5))],
        out_specs=pl.BlockSpec((1, C_WIDTH), lambda i: (0, 0)),
        out_shape=_sds((1, C_WIDTH), F32),
    )(dy, z)


def _loss_fwd_bwd(y, target):
    L, D = y.shape
    tl = _tile(L, _TL)

    def body(y_ref, t_ref, l_ref, d_ref):
        @pl.when(pl.program_id(0) == 0)
        def _():
            l_ref[...] = jnp.zeros_like(l_ref)

        e = y_ref[...] - t_ref[...]
        d_ref[...] = e * (1.0 / D)
        part = jnp.sum(jnp.sum(e * e, axis=-1, keepdims=True), axis=0, keepdims=True)
        l_ref[...] += jnp.broadcast_to(part, l_ref.shape)

    row = pl.BlockSpec((tl, D), lambda i: (i, 0))
    return pl.pallas_call(
        body, name="loss", grid=(L // tl,), in_specs=[row, row],
        out_specs=[pl.BlockSpec((SUBLANES, LANES), lambda i: (0, 0)), row],
        out_shape=[_sds((SUBLANES, LANES), F32), _sds((L, D), F32)],
    )(y, target)


def _adamw(w, g, m, v, name):
    R, C = w.shape
    tr = R if R <= 512 else _tile_rows(R, 512)

    def body(w_ref, g_ref, m_ref, v_ref, d_ref, nm_ref, nv_ref):
        gv = g_ref[...]
        nm = ADAM_B1 * m_ref[...] + (1.0 - ADAM_B1) * gv
        nv = ADAM_B2 * v_ref[...] + (1.0 - ADAM_B2) * (gv * gv)
        m_hat = nm / (1.0 - ADAM_B1 ** ADAM_STEP)
        v_hat = nv / (1.0 - ADAM_B2 ** ADAM_STEP)
        d_ref[...] = -ADAM_LR * (m_hat / (jnp.sqrt(v_hat) + ADAM_EPS) + ADAM_WD * w_ref[...])
        nm_ref[...] = nm
        nv_ref[...] = nv

    blk = pl.BlockSpec((tr, C), lambda i: (i, 0))
    return pl.pallas_call(
        body, name=name, grid=(R // tr,), in_specs=[blk] * 4, out_specs=[blk] * 3,
        out_shape=[_sds((R, C), F32)] * 3,
        compiler_params=_cparams(7 * _nbytes((tr, C), F32)),
    )(w, g, m, v)


def _tile_rows(R, pref):
    t = pref
    while R % t:
        t -= SUBLANES
    assert t > 0
    return t


def _add_n(xs, name):
    R, C = xs[0].shape
    tr = R if R <= 512 else _tile_rows(R, 512)
    n = len(xs)

    def body(*refs):
        acc = refs[0][...].astype(F32)
        for r in refs[1:n]:
            acc = acc + r[...].astype(F32)
        refs[n][...] = acc

    blk = pl.BlockSpec((tr, C), lambda i: (i, 0))
    return pl.pallas_call(
        body, name=name, grid=(R // tr,), in_specs=[blk] * n, out_specs=blk,
        out_shape=_sds((R, C), F32),
        compiler_params=_cparams((n + 1) * _nbytes((tr, C), F32)),
    )(*xs)


class _Exchange:
    def __init__(self, ins, aliased, fresh, n_sems, start, wait, done):
        self.ins, self.aliased, self.fresh, self.n_sems = ins, aliased, fresh, n_sems
        self.start, self.wait, self.done = start, wait, done


def _mm_host(lp, key, *args, **kw):
    plan = lp.get(key)
    if plan is None:
        return _mm(*args, **kw)
    if not isinstance(plan, _Exchange):
        plan = plan()
    res, outs = _mm(*args, comm=plan, **kw)
    plan.done(outs)
    return res


def _relu2(acc):
    r = jnp.maximum(acc, 0.0)
    return (r * r,)


def _rms_rows(x, g):
    return (x * lax.rsqrt(jnp.mean(x * x, axis=-1, keepdims=True) + EPS)) * g


def _resid_norm_epi(acc, res, g):
    h = res + acc
    return h, _rms_rows(h, g)


def _rms_bwd_epi(acc, h, dres, g):
    r = lax.rsqrt(jnp.mean(h * h, axis=-1, keepdims=True) + EPS)
    xhat = h * r
    gy = acc * g
    dh = dres + r * (gy - xhat * jnp.mean(gy * xhat, axis=-1, keepdims=True))
    return dh, dh, jnp.sum(acc * xhat, axis=0, keepdims=True)


def _layer_fwd(h, xn, lp, cos, sin, g_next):
    L = h.shape[0]
    row = lambda n: lp[n].reshape(1, D_MODEL)
    z = _mm(xn, lp["W"]("w_in"), mode="nn", M=L, N=IN_COLS, K=D_MODEL, b_cb=True, out_dtypes=[F32], name="f_w_in")
    ya = _gmlp_fwd(z, lp["ws"], lp["bfull"], lp["lgf"], lp["lbf"], "f_gmlp")
    q, k = _qk_prep(z, cos, sin, lp["gq"], lp["gk"], "f_qk_prep")
    yb = _attn_fwd(q, k, z, lp["sinks"], "f_attn")
    plan = lp.get("x_ssm")
    S, y, yg = _ssm_fwd(z, lp["bcat"], lp["ccat"], lp["dskip"], lp["lam_r"], lp["lam_i"], "f_ssm",
                        comm=None if plan is None else plan())
    g12 = _mm(yg, lp["W"]("w12"), mode="nn", M=L, N=2 * C_WIDTH, K=C_WIDTH, out_dtypes=[F32], name="f_glu")
    ycat = _merge_fwd(ya, yb, g12, lp["mix_out_g"], "f_merge")
    h1, hn = _mm_host(lp, "x_out", ycat, lp["W"]("w_out"), mode="nn", M=L, N=D_MODEL, K=D_MODEL, extras=[(h, 0), (row("mlp_norm_g"), 0)],
                 epi=_resid_norm_epi, out_dtypes=[F32, _ACT], name="f_w_out")
    r = _mm_host(lp, "x_ff1", hn, lp["W"]("w_ff1"), mode="nn", M=L, N=D_FF, K=D_MODEL, b_cb=True, epi=_relu2,
                 out_dtypes=[_ACT], name="f_ff1")
    h2, hn3 = _mm_host(lp, "x_ff2", r, lp["W"]("w_ff2"), mode="nn", M=L, N=D_MODEL, K=D_FF,
                       extras=[(h1, 0), (row("ple_norm_g"), 0)],
                       epi=_resid_norm_epi, out_dtypes=[F32, _ACT], name="f_ff2")
    e = _mm(lp["p"], lp["W"]("w_ple_proj"), mode="nn", M=L, N=D_MODEL, K=PLE_DIM, b_cb=True, tk=PLE_DIM,
            out_dtypes=[F32], name="f_ple_proj")

    def gate_epi(acc, h2_, e_, *g):
        gate_ = _sigmoid(acc)
        h3_ = h2_ + gate_ * e_
        return (h3_, gate_) + ((_rms_rows(h3_, g[0]),) if g else ())

    outs = _mm_host(lp, "x_gate", hn3, lp["W"]("w_ple_gate"), mode="nn", M=L, N=D_MODEL, K=D_MODEL,
                    extras=[(h2, 0), (e, 0)] + ([(g_next.reshape(1, D_MODEL), 0)] if g_next is not None else []),
                    epi=gate_epi, out_dtypes=[F32, F32] + ([_ACT] if g_next is not None else []), name="f_ple_gate")
    h3, gate = outs[0], outs[1]
    xn_next = outs[2] if g_next is not None else None
    saved = dict(h=h, xn=xn, z=z, ya=ya, q=q, k=k, yb=yb, S=S, y=y, yg=yg, g12=g12, ycat=ycat, h1=h1, hn=hn,
                 r=r, h2=h2, hn3=hn3, e=e, gate=gate)
    return h3, xn_next, saved


def _layer_bwd(dh3, lp, sv, cos, sin):
    L = dh3.shape[0]
    z = sv["z"]
    dpre, de = _ple_bwd_elem(dh3, sv["gate"], sv["e"], "b_ple_elem")
    stk = {n: None for n in BIG}
    d_gate = _mm(sv["hn3"], dpre, mode="tn", M=D_MODEL, N=D_MODEL, K=L, out_dtypes=[F32], name="b_dw_gate",
                 o_stack=stk["w_ple_gate"])
    d_proj = _mm(lp["p"], de, mode="tn", M=PLE_DIM, N=D_MODEL, K=L, o_cb=True, tm=PLE_DIM,
                 out_dtypes=[F32], name="b_dw_proj", o_stack=stk["w_ple_proj"])
    row = lambda n: lp[n].reshape(1, D_MODEL)
    dh2, dh2_op, dg_ple = _mm_host(lp, "x_bwd0", dpre, lp["W"]("w_ple_gate"), mode="nt", M=L, N=D_MODEL, K=D_MODEL,
                                   extras=[(sv["h2"], 0), (dh3, 0), (row("ple_norm_g"), 0)], epi=_rms_bwd_epi,
                                   out_dtypes=[F32, _ACT, F32], n_acc=1, name="b_dx_gate")
    da = _mm_host(lp, "x_bwd", dh2_op, lp["W"]("w_ff2"), mode="nt", M=L, N=D_FF, K=D_MODEL, extras=[(sv["r"], 0)],
                  epi=lambda acc, r_: (acc * (2.0 * jnp.sqrt(r_.astype(F32))),), out_dtypes=[_ACT], name="b_dx_ff2")
    d_ff2 = _mm_host(lp, "x_bwd2", sv["r"], dh2_op, mode="tn", M=D_FF, N=D_MODEL, K=L, out_dtypes=[F32], name="b_dw_ff2")
    d_ff1 = _mm(sv["hn"], da, mode="tn", M=D_MODEL, N=D_FF, K=L, o_cb=True, out_dtypes=[F32], name="b_dw_ff1",
                o_stack=stk["w_ff1"])
    dh1, dh1_op, dg_mlp = _mm(da, lp["W"]("w_ff1"), mode="nt", M=L, N=D_MODEL, K=D_FF, b_cb=True,
                              extras=[(sv["h1"], 0), (dh2, 0), (row("mlp_norm_g"), 0)], epi=_rms_bwd_epi,
                              out_dtypes=[F32, _ACT, F32], n_acc=1, name="b_dx_ff1")
    d_out = _mm(sv["ycat"], dh1_op, mode="tn", M=D_MODEL, N=D_MODEL, K=L, out_dtypes=[F32], name="b_dw_out",
                o_stack=stk["w_out"])
    if "early" in lp:
        lp["early"](dict(w_out=d_out, w_ff1=d_ff1, w_ff2=d_ff2, w_ple_gate=d_gate, w_ple_proj=d_proj))
    dycat = _mm_host(lp, "x_e0", dh1_op, lp["W"]("w_out"), mode="nt", M=L, N=D_MODEL, K=D_MODEL, out_dtypes=[F32],
                     name="b_dx_out")
    dya, dyb, dg12, dmix = _merge_bwd(dycat, sv["ya"], sv["yb"], sv["g12"], lp["mix_out_g"], "b_merge")
    d_w12 = _mm(sv["yg"], dg12, mode="tn", M=C_WIDTH, N=2 * C_WIDTH, K=L, tm=C_WIDTH, out_dtypes=[F32], name="b_dw_glu",
                o_stack=stk["w12"])
    dy = _mm(dg12, lp["W"]("w12"), mode="nt", M=L, N=C_WIDTH, K=2 * C_WIDTH, tk=2 * C_WIDTH, extras=[(sv["y"], 0)],
             epi=lambda acc, y_: (acc * _gelu_grad(y_),), out_dtypes=[F32], name="b_dx_glu")
    plan = lp.get("x_e1")
    dzc, d_bcat, d_ccat, dd, dar, dai = _ssm_bwd(dy, z, sv["S"], lp["bcat"], lp["ccat"], lp["dskip"],
                                                 lp["lam_r"], lp["lam_i"], "b_ssm",
                                                 comm=None if plan is None else plan())
    dq, dkc, dkp, dvc, dvp, dsink = _attn_bwd(sv["q"], sv["k"], z, lp["sinks"], dyb, "b_attn")
    dzq, dzk, dzv, dgq, dgk = _qk_prep_bwd(z, dq, dkc, dkp, dvc, dvp, cos, sin, lp["gq"], lp["gk"], "b_qk_prep")
    dza, dws, dbs, dlg, dlb = _gmlp_bwd(z, dya, lp["ws"], lp["wsT"], lp["bfull"], lp["lgf"], lp["lbf"], "b_gmlp")
    dz = jnp.concatenate([dza, dzq, dzk, dzv, dzc], axis=1)
    d_in = _mm(sv["xn"], dz, mode="tn", M=D_MODEL, N=IN_COLS, K=L, o_cb=True, out_dtypes=[F32], name="b_dw_in",
               o_stack=stk["w_in"])
    dh, dg_attn = _mm(dz, lp["W"]("w_in"), mode="nt", M=L, N=D_MODEL, K=IN_COLS, b_cb=True,
                         extras=[(sv["h"], 0), (dh1, 0), (row("attn_norm_g"), 0)],
                         epi=lambda *t: (lambda o: (o[0], o[2]))(_rms_bwd_epi(*t)),
                         out_dtypes=[F32, F32], n_acc=1, name="b_dx_in")
    grads = dict(w_in=d_in, w12=d_w12, w_out=d_out, w_ff1=d_ff1, w_ff2=d_ff2, w_ple_gate=d_gate, w_ple_proj=d_proj,
                 attn_norm_g=dg_attn.reshape(D_MODEL), mlp_norm_g=dg_mlp.reshape(D_MODEL),
                 ple_norm_g=dg_ple.reshape(D_MODEL), mix_out_g=dmix.reshape(D_MODEL),
                 dws=dws, dbs=dbs, dlg=dlg, dlb=dlb, dgq=dgq, dgk=dgk, dsink=dsink,
                 dar=dar, dai=dai, d_bcat=d_bcat, d_ccat=d_ccat, dd=dd)
    return dh, grads


SMALL = ("attn_norm_g", "gmlp_ln_g", "gmlp_ln_b", "gmlp_ws", "gmlp_bs", "q_norm_g", "k_norm_g", "sinks",
         "ssm_a_re", "ssm_a_im", "ssm_log_dt", "ssm_b_re", "ssm_b_im", "ssm_c_re", "ssm_c_im", "ssm_d",
         "mix_out_g", "mlp_norm_g", "ple_norm_g")
BIG = ("w_in", "w12", "w_out", "w_ff1", "w_ff2", "w_ple_gate", "w_ple_proj")
COL_SHARDED = ("w_in", "w_ff1", "w_ple_proj")


def _block_diag(t):
    nl, g, a, b = t.shape
    eye = jnp.eye(g, dtype=t.dtype)
    return (t[:, :, :, None, :] * eye[None, :, None, :, None]).reshape(nl, g * a, g * b)


def _diag_blocks(t, a, b):
    nl = t.shape[0]
    t = t.reshape(nl, C_GROUPS, a, C_GROUPS, b)
    idx = jnp.arange(C_GROUPS)
    return jnp.moveaxis(t[:, idx, :, idx, :], 0, 1)


def _local_step(x, p, positions, target, sw, bw):
    nl = sw["attn_norm_g"].shape[0]
    G = nl * C_GROUPS
    zeros = lambda *s: jnp.zeros(s, F32)
    are = sw["ssm_a_re"].reshape(G, 1, C_STATE)
    aim = sw["ssm_a_im"].reshape(G, 1, C_STATE)
    ldt = jnp.broadcast_to(sw["ssm_log_dt"][..., None], (nl, C_GROUPS, C_STATE)).reshape(G, 1, C_STATE)
    bre = jnp.swapaxes(sw["ssm_b_re"], -1, -2).reshape(G, C_GROUP, C_STATE)
    bim = jnp.swapaxes(sw["ssm_b_im"], -1, -2).reshape(G, C_GROUP, C_STATE)
    lr, li, bbr, bbi = _ssm_prep(are, aim, ldt, bre, bim)
    unflat = lambda t: t.reshape(nl, C_GROUPS, C_GROUP, C_STATE)
    lp = dict(
        attn_norm_g=sw["attn_norm_g"], mlp_norm_g=sw["mlp_norm_g"], ple_norm_g=sw["ple_norm_g"],
        mix_out_g=sw["mix_out_g"], sinks=sw["sinks"],
        ws=sw["gmlp_ws"], wsT=jnp.swapaxes(sw["gmlp_ws"], -1, -2),
        bfull=jnp.concatenate([zeros(nl, A_HEADS, CHUNK, HEAD_DIM),
                               jnp.broadcast_to(sw["gmlp_bs"][..., None], (nl, A_HEADS, CHUNK, HEAD_DIM))], axis=-1),
        lgf=jnp.concatenate([zeros(nl, A_HEADS, HEAD_DIM), sw["gmlp_ln_g"]], axis=-1),
        lbf=jnp.concatenate([zeros(nl, A_HEADS, HEAD_DIM), sw["gmlp_ln_b"]], axis=-1),
        gq=jnp.tile(sw["q_norm_g"], (1, 2)).reshape(nl, 1, LANES),
        gk=jnp.tile(sw["k_norm_g"], (1, 2)).reshape(nl, 1, LANES),
        lam_r=lr.reshape(nl, 1, N_STATE), lam_i=li.reshape(nl, 1, N_STATE),
        bcat=jnp.concatenate([_block_diag(unflat(bbr)), _block_diag(unflat(bbi))], axis=-1),
        ccat=jnp.concatenate([_block_diag(jnp.swapaxes(sw["ssm_c_re"], -1, -2)),
                              -_block_diag(jnp.swapaxes(sw["ssm_c_im"], -1, -2))], axis=1),
        dskip=sw["ssm_d"].reshape(nl, 1, C_WIDTH))
    cos, sin = _rope_tables(positions)

    def layer_params(l, hooks):
        lpi = {n: v[l] for n, v in lp.items()}
        lpi["W"] = lambda n: bw.layer(l)[n]
        lpi["p"] = (p, l)
        lpi.update(hooks)
        return lpi

    h, saved = x, []
    xn = _rms_fwd(x, sw["attn_norm_g"][0], "f_norm_attn")
    for l in range(nl):
        g_next = sw["attn_norm_g"][l + 1] if l + 1 < nl else None
        h, xn, sv = _layer_fwd(h, xn, layer_params(l, bw.fwd_hooks(l)), cos, sin, g_next)
        saved.append(sv)
    sse, dh = _loss_fwd_bwd(h, target)

    per_layer = [None] * nl
    for l in reversed(range(nl)):
        dh, gl = _layer_bwd(dh, layer_params(l, bw.bwd_hooks(l)), saved[l], cos, sin)
        bw.grads(l, {n: gl.pop(n) for n in BIG})
        per_layer[l] = gl
    grad_x = dh
    g = {n: jnp.stack([per_layer[l][n] for l in range(nl)]) for n in per_layer[0]}

    d_bcat = g["d_bcat"]
    dbr = _diag_blocks(d_bcat[:, :, :N_STATE], C_GROUP, C_STATE).reshape(G, C_GROUP, C_STATE)
    dbi = _diag_blocks(d_bcat[:, :, N_STATE:], C_GROUP, C_STATE).reshape(G, C_GROUP, C_STATE)
    dlr = g["dar"][:, 0].reshape(G, 1, C_STATE)
    dli = g["dai"][:, 0].reshape(G, 1, C_STATE)
    g_are, g_aim, g_ldt, g_bre, g_bim = _ssm_prep_bwd(are, aim, ldt, bre, bim, dlr, dli, dbr, dbi)
    d_ccat = g["d_ccat"]
    sg = dict(
        attn_norm_g=g["attn_norm_g"], mlp_norm_g=g["mlp_norm_g"], ple_norm_g=g["ple_norm_g"], mix_out_g=g["mix_out_g"],
        gmlp_ln_g=g["dlg"][:, :, 0, HEAD_DIM:], gmlp_ln_b=g["dlb"][:, :, 0, HEAD_DIM:],
        gmlp_ws=g["dws"], gmlp_bs=g["dbs"][:, :, :, HEAD_DIM],
        q_norm_g=g["dgq"][:, 0, :HEAD_DIM], k_norm_g=g["dgk"][:, 0, :HEAD_DIM],
        sinks=g["dsink"][:, 0, :B_Q_HEADS],
        ssm_a_re=g_are.reshape(nl, C_GROUPS, C_STATE), ssm_a_im=g_aim.reshape(nl, C_GROUPS, C_STATE),
        ssm_log_dt=g_ldt[:, 0, 0].reshape(nl, C_GROUPS),
        ssm_b_re=jnp.swapaxes(g_bre.reshape(nl, C_GROUPS, C_GROUP, C_STATE), -1, -2),
        ssm_b_im=jnp.swapaxes(g_bim.reshape(nl, C_GROUPS, C_GROUP, C_STATE), -1, -2),
        ssm_c_re=jnp.swapaxes(_diag_blocks(d_ccat[:, :N_STATE], C_STATE, C_GROUP), -1, -2),
        ssm_c_im=-jnp.swapaxes(_diag_blocks(d_ccat[:, N_STATE:], C_STATE, C_GROUP), -1, -2),
        ssm_d=g["dd"].reshape(nl, C_GROUPS, C_GROUP),
    )
    return (sse[0, 0], grad_x, sg) + tuple(bw.finish(_pack(sg)))


_ANY = pl.BlockSpec(memory_space=pl.ANY)
N_LAYERS = 4


def _mesh_pos():
    x, y, c = lax.axis_index("x"), lax.axis_index("y"), lax.axis_index("c")
    chips = [(1 - x, y), (x, 1 - y), (1 - x, 1 - y)]
    return x, y, c, 2 * x + y, chips


def _cast_into_slot(ws, j, name):
    nl, R, _ = ws[0].shape
    widths = [w.shape[2] for w in ws]
    C = sum(widths)
    tr = R if R <= 512 else _tile_rows(R, 512)
    nw = len(ws)

    def body(s_ref, *refs):
        o_ref = refs[nw]
        off = 0
        for r, wd in zip(refs[:nw], widths):
            o_ref[:, off:off + wd] = r[...].astype(o_ref.dtype)
            off += wd

    return pl.pallas_call(
        body, name=name,
        grid_spec=pltpu.PrefetchScalarGridSpec(
            num_scalar_prefetch=1, grid=(nl, R // tr),
            in_specs=[pl.BlockSpec((None, tr, wd), lambda l, i, s: (l, i, 0)) for wd in widths],
            out_specs=pl.BlockSpec((None, None, tr, C), lambda l, i, s: (l, s[0], i, 0))),
        out_shape=_sds((nl, N_CHIPS, R, C), _MXU),
    )(jnp.reshape(j, (1,)).astype(jnp.int32), *ws)


def _gather_weights(bufs):
    nk = len(bufs)

    def body(*refs):
        ins, outs = refs[:nk], refs[nk:2 * nk]
        send_sems, recv_sems = refs[2 * nk:]
        x, y, c, j, chips = _mesh_pos()
        mine, other = pl.ds(2 * c, 2), pl.ds(2 * (1 - c), 2)

        def ici(t, q):
            cx, cy = chips[q]
            return pltpu.make_async_remote_copy(
                src_ref=ins[t].at[mine, j], dst_ref=outs[t].at[mine, j],
                send_sem=send_sems.at[6 * t + q], recv_sem=recv_sems.at[6 * t + q],
                device_id=(cx, cy, c), device_id_type=MESH)

        def landed(t, q):
            cx, cy = chips[q]
            blk = outs[t].at[mine, 2 * cx + cy]
            return pltpu.make_async_remote_copy(
                src_ref=blk, dst_ref=blk, send_sem=send_sems.at[6 * t + q], recv_sem=recv_sems.at[6 * t + q],
                device_id=(cx, cy, c), device_id_type=MESH)

        def fwd(t, q, rows):
            cx, cy = chips[q]
            blk = outs[t].at[rows, 2 * cx + cy]
            return pltpu.make_async_remote_copy(
                src_ref=blk, dst_ref=blk, send_sem=send_sems.at[6 * t + 3 + q], recv_sem=recv_sems.at[6 * t + 3 + q],
                device_id=(x, y, 1 - c), device_id_type=MESH)

        for t in range(nk):
            for q in range(3):
                ici(t, q).start()
        for t in range(nk):
            for q in range(3):
                landed(t, q).wait_recv()
                fwd(t, q, mine).start()
        for t in range(nk):
            for q in range(3):
                fwd(t, q, other).wait_recv()
        for t in range(nk):
            for q in range(3):
                ici(t, q).wait_send()
                fwd(t, q, mine).wait_send()

    return pl.pallas_call(
        body, name="gather_weights", in_specs=[_ANY] * nk, out_specs=[_ANY] * nk,
        out_shape=[_sds(b.shape, b.dtype) for b in bufs],
        input_output_aliases={t: t for t in range(nk)},
        scratch_shapes=[pltpu.SemaphoreType.DMA((6 * nk,)), pltpu.SemaphoreType.DMA((6 * nk,))],
    )(*bufs)


def _exchange_sibling_half(gl):
    nk = len(gl)

    def body(*refs):
        ins, outs = refs[:nk], refs[nk:2 * nk]
        send_sems, recv_sems = refs[2 * nk:]
        x, y, c, _, _ = _mesh_pos()
        cps = [pltpu.make_async_remote_copy(
            src_ref=ins[t].at[pl.ds(2 * (1 - c), 2)], dst_ref=outs[t],
            send_sem=send_sems.at[t], recv_sem=recv_sems.at[t],
            device_id=(x, y, 1 - c), device_id_type=MESH) for t in range(nk)]
        for cp in cps:
            cp.start()
        for cp in cps:
            cp.wait()

    return pl.pallas_call(
        body, name="reduce_sibling", in_specs=[_ANY] * nk, out_specs=[_ANY] * nk,
        out_shape=[_sds((2,) + g.shape[1:], g.dtype) for g in gl],
        scratch_shapes=[pltpu.SemaphoreType.DMA((nk,)), pltpu.SemaphoreType.DMA((nk,))],
    )(*gl)


def _exchange_chips(ps):
    nk = len(ps)

    def body(*refs):
        ins, outs = refs[:nk], refs[nk:2 * nk]
        send_sems, recv_sems = refs[2 * nk:]
        x, y, c, j, chips = _mesh_pos()

        def send(t, q):
            cx, cy = chips[q]
            return pltpu.make_async_remote_copy(
                src_ref=ins[t].at[:, 2 * cx + cy], dst_ref=outs[t].at[j],
                send_sem=send_sems.at[3 * t + q], recv_sem=recv_sems.at[3 * t + q],
                device_id=(cx, cy, c), device_id_type=MESH)

        def landed(t, q):
            cx, cy = chips[q]
            blk = outs[t].at[2 * cx + cy]
            return pltpu.make_async_remote_copy(
                src_ref=blk, dst_ref=blk, send_sem=send_sems.at[3 * t + q], recv_sem=recv_sems.at[3 * t + q],
                device_id=(cx, cy, c), device_id_type=MESH)

        for t in range(nk):
            for q in range(3):
                send(t, q).start()
        for t in range(nk):
            for q in range(3):
                landed(t, q).wait_recv()
        for t in range(nk):
            for q in range(3):
                send(t, q).wait_send()

    return pl.pallas_call(
        body, name="reduce_chips", in_specs=[_ANY] * nk, out_specs=[_ANY] * nk,
        out_shape=[_sds((N_CHIPS, 2) + p.shape[2:], p.dtype) for p in ps],
        scratch_shapes=[pltpu.SemaphoreType.DMA((3 * nk,)), pltpu.SemaphoreType.DMA((3 * nk,))],
    )(*ps)


def _share_sibling(fs):
    nk = len(fs)

    def body(*refs):
        ins, outs = refs[:nk], refs[nk:2 * nk]
        send_sems, recv_sems = refs[2 * nk:]
        x, y, c, _, _ = _mesh_pos()
        mine = pl.ds(2 * c, 2)
        cps = [pltpu.make_async_remote_copy(
            src_ref=ins[t].at[mine], dst_ref=outs[t].at[mine], send_sem=send_sems.at[t], recv_sem=recv_sems.at[t],
            device_id=(x, y, 1 - c), device_id_type=MESH) for t in range(nk)]
        for cp in cps:
            cp.start()
        for cp in cps:
            cp.wait_send()
        for t in range(nk):
            blk = outs[t].at[pl.ds(2 * (1 - c), 2)]
            pltpu.make_async_remote_copy(
                src_ref=blk, dst_ref=blk, send_sem=send_sems.at[t], recv_sem=recv_sems.at[t],
                device_id=(x, y, 1 - c), device_id_type=MESH).wait_recv()

    return pl.pallas_call(
        body, name="share_sibling", in_specs=[_ANY] * nk, out_specs=[_ANY] * nk,
        out_shape=[_sds(f.shape, f.dtype) for f in fs],
        input_output_aliases={t: t for t in range(nk)},
        scratch_shapes=[pltpu.SemaphoreType.DMA((nk,)), pltpu.SemaphoreType.DMA((nk,))],
    )(*fs)


def _add_own_half(gl, r1, c, name):
    _, ns, R, C = gl.shape
    rows = 2 * ns * R
    tr = _tile_rows(rows, 512)
    nblk = rows // tr

    def body(s_ref, a_ref, b_ref, o_ref):
        o_ref[...] = (a_ref[...] + b_ref[...]).astype(o_ref.dtype)

    out = pl.pallas_call(
        body, name=name,
        grid_spec=pltpu.PrefetchScalarGridSpec(
            num_scalar_prefetch=1, grid=(nblk,),
            in_specs=[pl.BlockSpec((tr, C), lambda i, s: (s[0] * nblk + i, 0)), pl.BlockSpec((tr, C), lambda i, s: (i, 0))],
            out_specs=pl.BlockSpec((tr, C), lambda i, s: (i, 0))),
        out_shape=_sds((rows, C), _WIRE),
        compiler_params=_cparams(3 * _nbytes((tr, C), F32)),
    )(jnp.reshape(c, (1,)).astype(jnp.int32), gl.reshape(2 * rows, C), r1.reshape(rows, C))
    return out.reshape(2, ns, R, C)


def _add_chips(p, r2, j, c, name):
    _, ns, R, C = p.shape
    tr = R if R <= 512 else _tile_rows(R, 512)

    def body(s_ref, own, a1, a2, a3, o_ref):
        f = lambda r: r[...].astype(F32)
        o_ref[...] = ((f(own) + f(a1)) + f(a2)) + f(a3)

    blk = (None, None, tr, C)
    return pl.pallas_call(
        body, name=name,
        grid_spec=pltpu.PrefetchScalarGridSpec(
            num_scalar_prefetch=1, grid=(2, R // tr),
            in_specs=[pl.BlockSpec(blk, lambda h, i, s: (h, s[0], i, 0))]
            + [pl.BlockSpec(blk, lambda h, i, s, k=k: ((s[0] + k) % N_CHIPS, h, i, 0)) for k in (1, 2, 3)],
            out_specs=pl.BlockSpec((None, tr, C), lambda h, i, s: (2 * s[1] + h, i, 0))),
        out_shape=_sds((N_LAYERS, R, C), F32),
        compiler_params=_cparams(6 * _nbytes((tr, C), F32)),
    )(jnp.stack([j, c]).astype(jnp.int32), p, r2, r2, r2)


def _allreduce_small(buf, plan=None):
    Rs = buf.shape[0]
    nx = 0 if plan is None else len(plan.ins)
    x_out_shape = [] if plan is None else [_sds(sh, dt) for sh, dt in plan.fresh]
    assert plan is None or not any(plan.aliased)
    nxo = len(x_out_shape)

    def body(*refs):
        b_ref, x_ins = refs[0], refs[1:1 + nx]
        o_ref, x_outs = refs[1 + nx], refs[2 + nx:2 + nx + nxo]
        t_ref, slots_ref, send_sems, recv_sems = refs[2 + nx + nxo:6 + nx + nxo]
        x_sems = refs[6 + nx + nxo:]
        if plan is not None:
            plan.start(x_ins, x_outs, *x_sems)
        x, y, c, j, chips = _mesh_pos()
        sib = pltpu.make_async_remote_copy(
            src_ref=b_ref, dst_ref=t_ref, send_sem=send_sems.at[0], recv_sem=recv_sems.at[0],
            device_id=(x, y, 1 - c), device_id_type=MESH)
        sib.start()
        sib.wait()
        slots_ref[j] = b_ref[...] + t_ref[...]

        def send(q):
            cx, cy = chips[q]
            return pltpu.make_async_remote_copy(
                src_ref=slots_ref.at[j], dst_ref=slots_ref.at[j], send_sem=send_sems.at[1 + q],
                recv_sem=recv_sems.at[1 + q], device_id=(cx, cy, c), device_id_type=MESH)

        def landed(q):
            cx, cy = chips[q]
            blk = slots_ref.at[2 * cx + cy]
            return pltpu.make_async_remote_copy(
                src_ref=blk, dst_ref=blk, send_sem=send_sems.at[1 + q], recv_sem=recv_sems.at[1 + q],
                device_id=(cx, cy, c), device_id_type=MESH)

        for q in range(3):
            send(q).start()
        for q in range(3):
            landed(q).wait_recv()
        for q in range(3):
            send(q).wait_send()
        o_ref[...] = ((slots_ref[0] + slots_ref[1]) + slots_ref[2]) + slots_ref[3]
        if plan is not None:
            plan.wait(x_ins, x_outs, *x_sems)

    vm = pl.BlockSpec(memory_space=pltpu.VMEM)
    outs = pl.pallas_call(
        body, name="allreduce_small", in_specs=[vm] + [_ANY] * nx, out_specs=[vm] + [_ANY] * nxo,
        out_shape=[_sds((Rs, LANES), F32)] + x_out_shape,
        scratch_shapes=[pltpu.VMEM((Rs, LANES), F32), pltpu.VMEM((N_CHIPS, Rs, LANES), F32),
                        pltpu.SemaphoreType.DMA((4,)), pltpu.SemaphoreType.DMA((4,))]
        + ([pltpu.SemaphoreType.DMA((plan.n_sems,))] * 2 if plan is not None else []),
        compiler_params=_cparams(4 * _nbytes((Rs, LANES), F32)),
    )(buf, *([] if plan is None else plan.ins))
    if plan is not None:
        plan.done(list(outs[1:]))
    return outs[0]


def _own_rows(c, R):
    return pl.ds(c * (R // 2), R // 2)


def _cast_layer_slot(ws, l, j, name):
    _, R, _ = ws[0].shape
    widths = [w.shape[2] for w in ws]
    C = sum(widths)
    tr = R if R <= 512 else _tile_rows(R, 512)
    nw = len(ws)

    def body(s_ref, *refs):
        o_ref = refs[nw]
        off = 0
        for r, wd in zip(refs[:nw], widths):
            o_ref[:, off:off + wd] = r[...].astype(o_ref.dtype)
            off += wd

    return pl.pallas_call(
        body, name=name,
        grid_spec=pltpu.PrefetchScalarGridSpec(
            num_scalar_prefetch=1, grid=(R // tr,),
            in_specs=[pl.BlockSpec((None, tr, wd), lambda i, s: (l, i, 0)) for wd in widths],
            out_specs=pl.BlockSpec((None, tr, C), lambda i, s: (s[0], i, 0))),
        out_shape=_sds((N_CHIPS, R, C), _MXU),
    )(jnp.reshape(j, (1,)).astype(jnp.int32), *ws)


def _gather_ici(bufs, done):
    nk = len(bufs)

    def copy(ins, outs, ss, rs, t, q, landed):
        x, y, c, j, chips = _mesh_pos()
        cx, cy = chips[q]
        rows = _own_rows(c, ins[t].shape[1])
        src = outs[t].at[2 * cx + cy, rows] if landed else ins[t].at[j, rows]
        dst = outs[t].at[2 * cx + cy, rows] if landed else outs[t].at[j, rows]
        return pltpu.make_async_remote_copy(src_ref=src, dst_ref=dst, send_sem=ss.at[3 * t + q], recv_sem=rs.at[3 * t + q],
                                            device_id=(cx, cy, c), device_id_type=MESH)

    def start(ins, outs, ss, rs):
        for t in range(nk):
            for q in range(3):
                copy(ins, outs, ss, rs, t, q, False).start()

    def wait(ins, outs, ss, rs):
        for t in range(nk):
            for q in range(3):
                copy(ins, outs, ss, rs, t, q, True).wait_recv()
                copy(ins, outs, ss, rs, t, q, False).wait_send()

    return _Exchange(bufs, [True] * nk, [], 3 * nk, start, wait, done)


def _gather_d2d(bufs, done):
    nk = len(bufs)

    def copy(ins, outs, ss, rs, t, q, mine):
        x, y, c, j, chips = _mesh_pos()
        cx, cy = chips[q]
        rows = _own_rows(c if mine else 1 - c, ins[t].shape[1])
        src = (ins if mine else outs)[t].at[2 * cx + cy, rows]
        return pltpu.make_async_remote_copy(src_ref=src, dst_ref=outs[t].at[2 * cx + cy, rows],
                                            send_sem=ss.at[3 * t + q], recv_sem=rs.at[3 * t + q],
                                            device_id=(x, y, 1 - c), device_id_type=MESH)

    def start(ins, outs, ss, rs):
        for t in range(nk):
            for q in range(3):
                copy(ins, outs, ss, rs, t, q, True).start()

    def wait(ins, outs, ss, rs):
        for t in range(nk):
            for q in range(3):
                copy(ins, outs, ss, rs, t, q, False).wait_recv()
                copy(ins, outs, ss, rs, t, q, True).wait_send()

    return _Exchange(bufs, [True] * nk, [], 3 * nk, start, wait, done)


def _reduce_d2d(gl, done):
    nk = len(gl)

    def copy(ins, outs, ss, rs, t):
        x, y, c, _, _ = _mesh_pos()
        return pltpu.make_async_remote_copy(
            src_ref=ins[t].at[:, _own_rows(1 - c, ins[t].shape[1])], dst_ref=outs[t],
            send_sem=ss.at[t], recv_sem=rs.at[t], device_id=(x, y, 1 - c), device_id_type=MESH)

    def start(ins, outs, ss, rs):
        for t in range(nk):
            copy(ins, outs, ss, rs, t).start()

    def wait(ins, outs, ss, rs):
        for t in range(nk):
            copy(ins, outs, ss, rs, t).wait()

    fresh = [((N_CHIPS, g.shape[1] // 2, g.shape[2]), g.dtype) for g in gl]
    return _Exchange(gl, [False] * nk, fresh, nk, start, wait, done)


def _reduce_ici(ps, done):
    nk = len(ps)

    def copy(ins, outs, ss, rs, t, q, landed):
        x, y, c, j, chips = _mesh_pos()
        cx, cy = chips[q]
        src = outs[t].at[2 * cx + cy] if landed else ins[t].at[2 * cx + cy]
        dst = outs[t].at[2 * cx + cy] if landed else outs[t].at[j]
        return pltpu.make_async_remote_copy(src_ref=src, dst_ref=dst, send_sem=ss.at[3 * t + q], recv_sem=rs.at[3 * t + q],
                                            device_id=(cx, cy, c), device_id_type=MESH)

    def start(ins, outs, ss, rs):
        for t in range(nk):
            for q in range(3):
                copy(ins, outs, ss, rs, t, q, False).start()

    def wait(ins, outs, ss, rs):
        for t in range(nk):
            for q in range(3):
                copy(ins, outs, ss, rs, t, q, True).wait_recv()
                copy(ins, outs, ss, rs, t, q, False).wait_send()

    return _Exchange(ps, [False] * nk, [(p_.shape, p_.dtype) for p_ in ps], 3 * nk, start, wait, done)


def _share_d2d(fs, done):
    nk = len(fs)

    def copy(ins, outs, ss, rs, t, mine):
        x, y, c, _, _ = _mesh_pos()
        rows = _own_rows(c if mine else 1 - c, ins[t].shape[1])
        src = (ins if mine else outs)[t].at[:, rows]
        return pltpu.make_async_remote_copy(src_ref=src, dst_ref=outs[t].at[:, rows], send_sem=ss.at[t], recv_sem=rs.at[t],
                                            device_id=(x, y, 1 - c), device_id_type=MESH)

    def start(ins, outs, ss, rs):
        for t in range(nk):
            copy(ins, outs, ss, rs, t, True).start()

    def wait(ins, outs, ss, rs):
        for t in range(nk):
            copy(ins, outs, ss, rs, t, False).wait_recv()
            copy(ins, outs, ss, rs, t, True).wait_send()

    return _Exchange(fs, [True] * nk, [], nk, start, wait, done)


def _run_exchange(plan, name):
    nin = len(plan.ins)
    out_shape = [_sds(x_.shape, x_.dtype) for x_, al in zip(plan.ins, plan.aliased) if al]
    aliases, k = {}, 0
    for t, al in enumerate(plan.aliased):
        if al:
            aliases[t] = k
            k += 1
    out_shape += [_sds(sh, dt) for sh, dt in plan.fresh]
    nout = len(out_shape)

    def body(*refs):
        ins, outs, sems = refs[:nin], refs[nin:nin + nout], refs[nin + nout:]
        plan.start(ins, outs, *sems)
        plan.wait(ins, outs, *sems)

    outs = pl.pallas_call(
        body, name=name, in_specs=[_ANY] * nin, out_specs=[_ANY] * nout, out_shape=out_shape,
        input_output_aliases=aliases,
        scratch_shapes=[pltpu.SemaphoreType.DMA((plan.n_sems,))] * 2,
    )(*plan.ins)
    plan.done(list(outs))


def _add_sibling_rows(g, r1, c, name):
    ns, R, C = g.shape
    hr = R // 2
    tr = hr if hr <= 512 else _tile_rows(hr, 512)
    nblk = hr // tr

    def body(s_ref, a_ref, b_ref, o_ref):
        o_ref[...] = (a_ref[...] + b_ref[...]).astype(o_ref.dtype)

    blk = (None, tr, C)
    return pl.pallas_call(
        body, name=name,
        grid_spec=pltpu.PrefetchScalarGridSpec(
            num_scalar_prefetch=1, grid=(ns, nblk),
            in_specs=[pl.BlockSpec(blk, lambda s_, i, s: (s_, s[0] * nblk + i, 0)), pl.BlockSpec(blk, lambda s_, i, s: (s_, i, 0))],
            out_specs=pl.BlockSpec(blk, lambda s_, i, s: (s_, i, 0))),
        out_shape=_sds((ns, hr, C), _WIRE),
        compiler_params=_cparams(3 * _nbytes((tr, C), F32)),
    )(jnp.reshape(c, (1,)).astype(jnp.int32), g, r1)


def _add_chip_rows(p_, r2, f, l, j, c, name):
    _, hr, C = p_.shape
    tr = hr if hr <= 512 else _tile_rows(hr, 512)
    nblk = hr // tr

    def body(s_ref, own, a1, a2, a3, f_ref, o_ref):
        v = lambda r: r[...].astype(F32)
        o_ref[...] = ((v(own) + v(a1)) + v(a2)) + v(a3)

    blk = (None, tr, C)
    return pl.pallas_call(
        body, name=name,
        grid_spec=pltpu.PrefetchScalarGridSpec(
            num_scalar_prefetch=1, grid=(nblk,),
            in_specs=[pl.BlockSpec(blk, lambda i, s: (s[0], i, 0))]
            + [pl.BlockSpec(blk, lambda i, s, k=k: ((s[0] + k) % N_CHIPS, i, 0)) for k in (1, 2, 3)]
            + [pl.BlockSpec(memory_space=pl.ANY)],
            out_specs=pl.BlockSpec(blk, lambda i, s: (l, s[1] * nblk + i, 0))),
        out_shape=_sds(f.shape, F32),
        input_output_aliases={5: 0},
        compiler_params=_cparams(6 * _nbytes((tr, C), F32)),
    )(jnp.stack([j, c]).astype(jnp.int32), p_, r2, r2, r2, f)


class _ShardedWeights:
    def __init__(self, a, j, c):
        self.j, self.c = j, c
        shards = dict(w_in=[a["w_in"]], w12=[a["glu_w1"], a["glu_w2"]], w_out=[a["w_out"]], w_ff1=[a["w_ff1"]],
                      w_ff2=[a["w_ff2"]], w_ple_gate=[a["w_ple_gate"]], w_ple_proj=[a["w_ple_proj"]])
        self.bufs = [[_cast_layer_slot(shards[n], l, j, "cast_%s_%d" % (n, l)) for n in BIG] for l in range(N_LAYERS)]
        _run_exchange(self._gather_part(0, self._HEAD), "gather_ici_0")
        _run_exchange(self._gather_part(0, self._HEAD, _gather_d2d), "gather_d2d_0")
        self.raw_ = {}
        self.pend_ = {}
        self.final = [lax.empty((N_LAYERS,) + b.shape[1:], F32) for b in self.bufs[0]]

    def _set_bufs(self, l, outs):
        self.bufs[l] = outs

    def layer(self, l):
        return {n: (b if n in COL_SHARDED else b.reshape(N_CHIPS * b.shape[1], b.shape[2]))
                for n, b in zip(BIG, self.bufs[l])}

    _FIRST, _SECOND = (0, 3), (1, 2, 4, 5, 6)

    _HEAD, _REST = (0, 1, 2), (3, 4, 5, 6)

    def _gather_part(self, nxt, idx, exchange=_gather_ici):
        def done(outs):
            for i, o in zip(idx, outs):
                self.bufs[nxt][i] = o
        return exchange([self.bufs[nxt][i] for i in idx], done)

    def fwd_hooks(self, l):
        hooks = {}
        if l == 0:
            hooks.update(x_ssm=lambda: self._gather_part(0, self._REST),
                         x_out=lambda: self._gather_part(0, self._REST, _gather_d2d))
        if l + 1 < N_LAYERS:
            nxt = l + 1
            hooks.update(x_ff1=lambda: self._gather_part(nxt, self._FIRST),
                         x_ff2=lambda: self._gather_part(nxt, self._SECOND),
                         x_gate=lambda: _gather_d2d(self.bufs[nxt], lambda o: self._set_bufs(nxt, o)))
        return hooks

    _EARLY, _LATE = (2, 3, 4, 5, 6), (0, 1)

    @staticmethod
    def _shards(idx, g):
        out = []
        for i in idx:
            n = BIG[i]
            out.append(g[n] if n in COL_SHARDED else g[n].reshape(N_CHIPS, g[n].shape[0] // N_CHIPS, g[n].shape[1]))
        return out

    def _sibling_plan(self, slot):
        lyr, idx, gl = self.raw_[slot]

        def done(got):
            ps = [_add_sibling_rows(g_, r1, self.c, "reduce_add_sibling_%s_%d" % (BIG[i], lyr))
                  for g_, r1, i in zip(gl, got, idx)]
            self.pend_[slot] = (lyr, idx, ps)
        return _reduce_d2d(gl, done)

    def _chips_plan(self, slot):
        lyr, idx, ps = self.pend_[slot]

        def done(r2):
            for i, p_, r in zip(idx, ps, r2):
                self.final[i] = _add_chip_rows(p_, r, self.final[i], lyr, self.j, self.c,
                                               "reduce_add_chips_%s_%d" % (BIG[i], lyr))
        return _reduce_ici(ps, done)

    def _early(self, l, g):
        self.raw_["early"] = (l, self._EARLY, self._shards(self._EARLY, g))

    def bwd_hooks(self, l):
        hooks = dict(early=lambda g: self._early(l, g),
                     x_e0=lambda: self._sibling_plan("early"), x_e1=lambda: self._chips_plan("early"))
        if "late" in self.raw_:
            hooks.update(x_bwd0=lambda: self._sibling_plan("late"), x_bwd=lambda: self._chips_plan("late"))
        return hooks

    def grads(self, l, g):
        self.raw_["late"] = (l, self._LATE, self._shards(self._LATE, g))

    def finish(self, small):
        _run_exchange(self._sibling_plan("late"), "reduce_d2d_last")
        small = _allreduce_small(small, self._chips_plan("late"))
        out = []
        _run_exchange(_share_d2d(self.final, out.extend), "share_d2d")
        return dict(zip(BIG, out)), small


def _rows_of(shape):
    return -(-int(np.prod(shape)) // (SUBLANES * LANES)) * SUBLANES


def _pack(d):
    parts = []
    for n in SMALL:
        flat = d[n].reshape(-1)
        parts.append(jnp.pad(flat, (0, _rows_of(flat.shape) * LANES - flat.shape[0])).reshape(-1, LANES))
    return jnp.concatenate(parts, axis=0)


def _unpack(buf, like):
    out, r0 = {}, 0
    for n in SMALL:
        shape = like[n].shape
        size, nr = int(np.prod(shape)), _rows_of(shape)
        piece = lax.optimization_barrier(buf[r0:r0 + nr])
        out[n] = piece.reshape(-1)[:size].reshape(shape)
        r0 += nr
    return out


ARGS = ("x", "p", "positions", "attn_norm_g", "w_in", "gmlp_ln_g", "gmlp_ln_b", "gmlp_ws", "gmlp_bs", "q_norm_g",
        "k_norm_g", "sinks", "ssm_a_re", "ssm_a_im", "ssm_log_dt", "ssm_b_re", "ssm_b_im", "ssm_c_re", "ssm_c_im",
        "ssm_d", "glu_w1", "glu_w2", "mix_out_g", "w_out", "mlp_norm_g", "w_ff1", "w_ff2", "ple_norm_g", "w_ple_gate",
        "w_ple_proj")
WEIGHTS = ARGS[3:]


def kernel(x, p, positions, attn_norm_g, w_in, gmlp_ln_g, gmlp_ln_b, gmlp_ws, gmlp_bs, q_norm_g, k_norm_g, sinks, ssm_a_re, ssm_a_im, ssm_log_dt, ssm_b_re, ssm_b_im, ssm_c_re, ssm_c_im, ssm_d, glu_w1, glu_w2, mix_out_g, w_out, mlp_norm_g, w_ff1, w_ff2, ple_norm_g, w_ple_gate, w_ple_proj, loss_target, m_attn_norm_g, m_w_in, m_gmlp_ln_g, m_gmlp_ln_b, m_gmlp_ws, m_gmlp_bs, m_q_norm_g, m_k_norm_g, m_sinks, m_ssm_a_re, m_ssm_a_im, m_ssm_log_dt, m_ssm_b_re, m_ssm_b_im, m_ssm_c_re, m_ssm_c_im, m_ssm_d, m_glu_w1, m_glu_w2, m_mix_out_g, m_w_out, m_mlp_norm_g, m_w_ff1, m_w_ff2, m_ple_norm_g, m_w_ple_gate, m_w_ple_proj, v_attn_norm_g, v_w_in, v_gmlp_ln_g, v_gmlp_ln_b, v_gmlp_ws, v_gmlp_bs, v_q_norm_g, v_k_norm_g, v_sinks, v_ssm_a_re, v_ssm_a_im, v_ssm_log_dt, v_ssm_b_re, v_ssm_b_im, v_ssm_c_re, v_ssm_c_im, v_ssm_d, v_glu_w1, v_glu_w2, v_mix_out_g, v_w_out, v_mlp_norm_g, v_w_ff1, v_w_ff2, v_ple_norm_g, v_w_ple_gate, v_w_ple_proj):
    a = dict(locals())
    L = a["x"].shape[1]
    nl = N_LAYERS
    c = lax.axis_index("c")
    j = 2 * lax.axis_index("x") + lax.axis_index("y")

    sw = {n: a[n] for n in SMALL}
    sse, gx, _, big_grads, small_sum = _local_step(
        a["x"].reshape(L, D_MODEL), a["p"].reshape(nl, L, PLE_DIM), a["positions"].reshape(L),
        a["loss_target"].reshape(L, D_MODEL), sw, _ShardedWeights(a, j, c))
    loss = lax.psum(sse * (0.5 / D_MODEL), ("x", "y", "c"))
    g12 = big_grads.pop("w12")
    big_grads["glu_w1"], big_grads["glu_w2"] = g12[:, :, :C_WIDTH], g12[:, :, C_WIDTH:]

    small_grads = _unpack(small_sum, sw)

    grads, delta, new_m, new_v = {}, {}, {}, {}
    d_s, m_s, v_s = _adamw(_pack(sw), _pack(small_grads), _pack({n: a["m_" + n] for n in SMALL}),
                           _pack({n: a["v_" + n] for n in SMALL}), "adamw_small")
    grads.update(small_grads)
    delta.update(_unpack(d_s, sw))
    new_m.update(_unpack(m_s, sw))
    new_v.update(_unpack(v_s, sw))
    for n, g in big_grads.items():
        shp = a[n].shape
        two_d = lambda t: t.reshape(shp[0] * shp[1], shp[2])
        d, m, v = _adamw(two_d(a[n]), two_d(g), two_d(a["m_" + n]), two_d(a["v_" + n]), "adamw_" + n)
        grads[n], delta[n], new_m[n], new_v[n] = g, d.reshape(shp), m.reshape(shp), v.reshape(shp)

    return (loss, gx.reshape(1, L, D_MODEL), *[grads[n] for n in WEIGHTS], *[delta[n] for n in WEIGHTS],
            *[new_m[n] for n in WEIGHTS], *[new_v[n] for n in WEIGHTS])
```

```python
import functools
import math

import numpy as np
import jax
import jax.numpy as jnp
from jax import lax
from jax.experimental import pallas as pl
from jax.experimental.pallas import tpu as pltpu

F32 = jnp.float32
_MXU = jnp.bfloat16
_ACT = jnp.bfloat16
_WIRE = jnp.bfloat16

D_MODEL = 1024
HEAD_DIM = 64
A_HEADS = 4
CHUNK = 128
B_Q_HEADS = 8
B_KV_HEADS = 2
B_GROUP = 4
WINDOW = 128
ROPE_THETA = 10000.0
C_WIDTH = 256
C_GROUP = 16
C_GROUPS = 16
C_STATE = 64
N_STATE = C_GROUPS * C_STATE
IN_A, IN_Q, IN_KV, IN_C = 512, 512, 128, 256
IN_COLS = 1536
D_FF = 4096
PLE_DIM = 256
EPS = 1e-6
NEG = -1e30
ADAM_LR, ADAM_B1, ADAM_B2, ADAM_EPS, ADAM_WD, ADAM_STEP = 0.001, 0.9, 0.999, 1e-08, 0.01, 10

LANES = 128
SUBLANES = 8
VMEM_BYTES = 64 * 2 ** 20
N_CHIPS = 4
MESH = pl.DeviceIdType.MESH


_MM_VMEM_BUDGET = 55 * 2 ** 20
_EPI_ROWS = 256


def _vmem_limit(est_bytes):
    return int(min(max(2 * est_bytes + (8 << 20), 32 << 20), VMEM_BYTES - (6 << 20)))


def _cparams(est_bytes, **kw):
    return pltpu.CompilerParams(vmem_limit_bytes=_vmem_limit(est_bytes), **kw)


def _sds(shape, dtype):
    return pltpu.HBM(tuple(shape), dtype)


def _hbm(x):
    return pltpu.with_memory_space_constraint(x, pltpu.HBM) if x.size >= (1 << 20) else x


def _nbytes(shape, dtype):
    return int(np.prod(shape)) * jnp.dtype(dtype).itemsize


def _tile(dim, pref):
    t = min(dim, pref)
    while dim % t:
        t -= LANES
    assert t > 0, (dim, pref)
    return t


def _lane(shape):
    return lax.broadcasted_iota(jnp.int32, shape, len(shape) - 1)


def _row(shape):
    return lax.broadcasted_iota(jnp.int32, shape, len(shape) - 2)


def _gelu(x):
    c = math.sqrt(2.0 / math.pi)
    return 0.5 * x * (1.0 + jnp.tanh(c * (x + 0.044715 * (x * x * x))))


def _gelu_grad(x):
    c = math.sqrt(2.0 / math.pi)
    t = jnp.tanh(c * (x + 0.044715 * (x * x * x)))
    return 0.5 * (1.0 + t) + 0.5 * x * (1.0 - t * t) * (c * (1.0 + 3.0 * 0.044715 * (x * x)))


def _sigmoid(x):
    return 1.0 / (1.0 + jnp.exp(-x))


def _dot(a, b, dims=(((1,), (0,)), ((), ()))):
    return lax.dot_general(a.astype(_MXU), b.astype(_MXU), dims, preferred_element_type=F32)


_NT = (((1,), (1,)), ((), ()))
_TN = (((0,), (0,)), ((), ()))
_NN = (((1,), (0,)), ((), ()))


def _mm(a, b, *, mode, M, N, K, out_dtypes, name, epi=None, extras=(), b_cb=False, o_cb=False,
        a_off=0, b_off=0, tm=1024, tn=1024, tk=2048, a_lyr=None, b_lyr=None, o_stack=None, n_acc=0, comm=None):
    if isinstance(a, tuple):
        a, a_lyr = a
    if isinstance(b, tuple):
        b, b_lyr = b
    if b_cb or o_cb:
        nc = (b.shape[-1] if b_cb else N // N_CHIPS)
    tn_nom = nc if ((mode == "nn" and b_cb) or (mode == "tn" and o_cb)) else _tile(N, tn)
    tk_nom = nc if (mode == "nt" and b_cb) else _tile(K, tk)
    item = lambda d: jnp.dtype(d).itemsize
    per_row = tk_nom * item(a.dtype) + tn_nom * (sum(item(d) for d in out_dtypes)
                                                   + sum(item(e.dtype) for e, _ in extras if e.shape[0] > 1))
    fixed = tk_nom * tn_nom * item(b.dtype)
    tm = _tile(M, tm)
    while tm > 256 and M % (tm // 2) == 0 and 2 * (tm * per_row + fixed) + 8 * tm * tn_nom > _MM_VMEM_BUDGET:
        tm //= 2

    def spec(block, imap, lyr=None):
        if lyr is None:
            return pl.BlockSpec(block, imap)
        return pl.BlockSpec((None,) + block, lambda i, j, k: (lyr,) + imap(i, j, k))

    if mode == "nn":
        if b_cb:
            tn = nc
        tm, tn, tk = _tile(M, tm), _tile(N, tn), _tile(K, tk)
        a_spec = spec((tm, tk), lambda i, j, k: (i, k + a_off), a_lyr)
        if b_cb:
            b_spec = spec((None, tk, tn), lambda i, j, k: (j, k, 0), b_lyr)
        else:
            b_spec = spec((tk, tn), lambda i, j, k: (k, j + b_off), b_lyr)
        dims = _NN
        a_blk, b_blk = (tm, tk), (tk, tn)
    elif mode == "nt":
        if b_cb:
            tk = nc
        tm, tn, tk = _tile(M, tm), _tile(N, tn), _tile(K, tk)
        a_spec = spec((tm, tk), lambda i, j, k: (i, k + a_off), a_lyr)
        if b_cb:
            b_spec = spec((None, tn, tk), lambda i, j, k: (k, j, 0), b_lyr)
        else:
            b_spec = spec((tn, tk), lambda i, j, k: (j, k + b_off), b_lyr)
        dims = _NT
        a_blk, b_blk = (tm, tk), (tn, tk)
    else:
        if o_cb:
            tn = nc
        tm, tn, tk = _tile(M, tm), _tile(N, tn), _tile(K, tk)
        a_spec = spec((tk, tm), lambda i, j, k: (k, i + a_off), a_lyr)
        b_spec = spec((tk, tn), lambda i, j, k: (k, j + b_off), b_lyr)
        dims = _TN
        a_blk, b_blk = (tk, tm), (tk, tn)
    gi, gj, gk = M // tm, N // tn, K // tk
    o_lyr = None if o_stack is None else o_stack[1]
    if o_cb:
        o_spec = spec((None, tm, tn), lambda i, j, k: (j, i, 0), o_lyr)
        o_shape = (gj, M, tn)
    else:
        o_spec = spec((tm, tn), lambda i, j, k: (i, j), o_lyr)
        o_shape = (M, N)
    e_specs = []
    for e, off in extras:
        if e.shape[0] == 1:
            e_specs.append(pl.BlockSpec((1, tn), lambda i, j, k, off=off: (0, j + off)))
        else:
            e_specs.append(pl.BlockSpec((tm, tn), lambda i, j, k, off=off: (i, j + off)))
    extras = [e for e, _ in extras]
    ne, no = len(extras), len(out_dtypes)
    operands = [_hbm(t) for t in (a, b, *extras)]
    in_specs = [a_spec, b_spec] + e_specs
    out_shape = [_sds(o_shape, d) for d in out_dtypes]
    aliases = {}
    if o_stack is not None:
        assert no == 1 and o_stack[0].shape[1:] == o_shape and o_stack[0].dtype == out_dtypes[0]
        operands.append(_hbm(o_stack[0]))
        in_specs.append(pl.BlockSpec(memory_space=pl.ANY))
        out_shape = [_sds(o_stack[0].shape, o_stack[0].dtype)]
        aliases = {len(operands) - 1: 0}
    out_specs = [o_spec] * no
    if n_acc:
        assert gj == 1 and o_stack is None
        out_specs[no - n_acc:] = [pl.BlockSpec((1, tn), lambda i, j, k: (0, 0))] * n_acc
        out_shape[no - n_acc:] = [_sds((1, N), d) for d in out_dtypes[no - n_acc:]]
    nx_in = nx_out = 0
    if comm is not None:
        nx_in, ncin0 = len(comm.ins), len(operands)
        operands += list(comm.ins)
        in_specs += [pl.BlockSpec(memory_space=pl.ANY)] * nx_in
        for t, x_ in enumerate(comm.ins):
            if comm.aliased[t]:
                aliases[ncin0 + t] = len(out_shape)
                out_shape.append(_sds(x_.shape, x_.dtype))
        out_shape += [_sds(sh, dt) for sh, dt in comm.fresh]
        nx_out = len(out_shape) - no
        out_specs += [pl.BlockSpec(memory_space=pl.ANY)] * nx_out
    nin = len(operands)

    def body(*refs):
        a_ref, b_ref = refs[0], refs[1]
        e_refs = refs[2:2 + ne]
        o_refs = refs[nin:nin + no]
        first_rows = pl.program_id(0) == 0
        if comm is not None:
            x_ins = refs[nin - nx_in:nin]
            x_outs = refs[nin + no:nin + no + nx_out]
            sems = refs[nin + no + nx_out:nin + no + nx_out + 2]
            pid = [pl.program_id(d) for d in range(3)]

            @pl.when((pid[0] == 0) & (pid[1] == 0) & (pid[2] == 0))
            def _():
                comm.start(x_ins, x_outs, *sems)

        def fin(acc):
            for t in range(no - n_acc, no):
                @pl.when(first_rows)
                def _():
                    o_refs[t][...] = jnp.zeros_like(o_refs[t])

            rc = _EPI_ROWS if (epi is not None and tm % _EPI_ROWS == 0) else tm
            for c0 in range(0, tm, rc):
                rows = slice(c0, c0 + rc)
                ex = [e[...] if e.shape[0] == 1 else e[rows, :] for e in e_refs]
                vals = epi(acc[rows, :], *ex) if epi is not None else (acc[rows, :],)
                for t, (o, v) in enumerate(zip(o_refs, vals)):
                    if t < no - n_acc:
                        o[rows, :] = v.astype(o.dtype)
                    else:
                        o[...] += v.astype(o.dtype)

        prod = _dot(a_ref[...], b_ref[...], dims)
        if gk == 1:
            fin(prod)
        else:
            acc_ref = refs[-1]
            k = pl.program_id(2)

            @pl.when(k == 0)
            def _():
                acc_ref[...] = prod

            @pl.when(k > 0)
            def _():
                acc_ref[...] += prod

            @pl.when(k == gk - 1)
            def _():
                fin(acc_ref)

        if comm is not None:
            @pl.when((pid[0] == gi - 1) & (pid[1] == gj - 1) & (pid[2] == gk - 1))
            def _():
                comm.wait(x_ins, x_outs, *sems)

    est = (_nbytes(a_blk, a.dtype) + _nbytes(b_blk, b.dtype)
           + sum(_nbytes((tm, tn), d) for d in out_dtypes)
           + sum(_nbytes((tm, tn), e.dtype) for e in extras)) + 2 * _nbytes((tm, tn), F32)
    sem_scratch = [pltpu.SemaphoreType.DMA((comm.n_sems,))] * 2 if comm is not None else []
    row_sem = "arbitrary" if (n_acc or comm is not None) else "parallel"
    outs = pl.pallas_call(
        body, name=name, grid=(gi, gj, gk),
        in_specs=in_specs,
        out_specs=out_specs,
        out_shape=out_shape,
        scratch_shapes=sem_scratch + ([pltpu.VMEM((tm, tn), F32)] if gk > 1 else []),
        input_output_aliases=aliases,
        compiler_params=_cparams(est, dimension_semantics=(row_sem, "arbitrary" if comm is not None else "parallel",
                                                           "arbitrary")),
    )(*operands)
    if comm is not None:
        main = outs[:no]
        return (main if no > 1 else main[0]), list(outs[no:])
    return outs if no > 1 else outs[0]


_TL = 512


def _rms_fwd(h, g, name):
    L, D = h.shape
    tl = _tile(L, _TL)

    def body(h_ref, g_ref, o_ref):
        x = h_ref[...]
        r = lax.rsqrt(jnp.mean(x * x, axis=-1, keepdims=True) + EPS)
        o_ref[...] = ((x * r) * g_ref[...]).astype(o_ref.dtype)

    return pl.pallas_call(
        body, name=name, grid=(L // tl,),
        in_specs=[pl.BlockSpec((tl, D), lambda i: (i, 0)), pl.BlockSpec((1, D), lambda i: (0, 0))],
        out_specs=pl.BlockSpec((tl, D), lambda i: (i, 0)),
        out_shape=_sds((L, D), _ACT),
        compiler_params=_cparams(3 * _nbytes((tl, D), F32)),
    )(h, g.reshape(1, D))


def _rms_bwd(dxn, h, g, dres, name):
    L, D = h.shape
    tl = _tile(L, _TL)

    def body(d_ref, h_ref, g_ref, r_ref, o_ref, dg_ref):
        x = h_ref[...]
        r = lax.rsqrt(jnp.mean(x * x, axis=-1, keepdims=True) + EPS)
        xhat = x * r
        d = d_ref[...].astype(F32)
        gy = d * g_ref[...]
        dx = r * (gy - xhat * jnp.mean(gy * xhat, axis=-1, keepdims=True))
        o_ref[...] = r_ref[...] + dx

        @pl.when(pl.program_id(0) == 0)
        def _():
            dg_ref[...] = jnp.zeros_like(dg_ref)

        dg_ref[...] += jnp.sum(d * xhat, axis=0, keepdims=True)

    dh, dg = pl.pallas_call(
        body, name=name, grid=(L // tl,),
        in_specs=[pl.BlockSpec((tl, D), lambda i: (i, 0)), pl.BlockSpec((tl, D), lambda i: (i, 0)),
                  pl.BlockSpec((1, D), lambda i: (0, 0)), pl.BlockSpec((tl, D), lambda i: (i, 0))],
        out_specs=[pl.BlockSpec((tl, D), lambda i: (i, 0)), pl.BlockSpec((1, D), lambda i: (0, 0))],
        out_shape=[_sds((L, D), F32), _sds((1, D), F32)],
        compiler_params=_cparams(5 * _nbytes((tl, D), F32)),
    )(dxn, h, g.reshape(1, D), dres)
    return dh, dg.reshape(D)


def _rope_tables(positions):
    L = positions.shape[0]
    tl = _tile(L, 1024)
    inv = 1.0 / (ROPE_THETA ** (np.arange(0, HEAD_DIM, 2, dtype=np.float32) / HEAD_DIM))
    inv128 = jnp.asarray(np.tile(inv.astype(np.float32), 4).reshape(1, LANES))

    def body(p_ref, i_ref, c_ref, s_ref):
        ang = p_ref[...].astype(F32) * i_ref[...]
        c_ref[...] = jnp.cos(ang)
        s_ref[...] = jnp.sin(ang)

    return pl.pallas_call(
        body, name="rope_tables", grid=(L // tl,),
        in_specs=[pl.BlockSpec((tl, 1), lambda i: (i, 0)), pl.BlockSpec((1, LANES), lambda i: (0, 0))],
        out_specs=[pl.BlockSpec((tl, LANES), lambda i: (i, 0))] * 2,
        out_shape=[_sds((L, LANES), F32)] * 2,
    )(positions.reshape(L, 1), inv128)


_GM_TL = 256


def _gmlp_head(Z, W, bfull, lg, lb, maskv):
    G = _gelu(Z)
    mu = jnp.sum(jnp.where(maskv, G, 0.0), axis=-1, keepdims=True) * (1.0 / HEAD_DIM)
    xc = jnp.where(maskv, G - mu, 0.0)
    var = jnp.sum(xc * xc, axis=-1, keepdims=True) * (1.0 / HEAD_DIM)
    rstd = lax.rsqrt(var + EPS)
    xhat = xc * rstd
    vn = xhat * lg + lb
    sv = _dot(W, vn) + bfull
    return G, xhat, rstd, vn, sv


def _tril(W):
    return jnp.where(_row(W.shape) >= _lane(W.shape), W, 0.0)


def _triu(W):
    return jnp.where(_row(W.shape) <= _lane(W.shape), W, 0.0)


def _gmlp_fwd(z, ws, bfull, lgf, lbf, name):
    L = z.shape[0]
    tl = _tile(L, _GM_TL)
    nch = tl // CHUNK

    def body(z_ref, w_ref, b_ref, lg_ref, lb_ref, o_ref):
        maskv = _lane((CHUNK, LANES)) >= HEAD_DIM
        for c in range(nch):
            rows = slice(c * CHUNK, (c + 1) * CHUNK)
            for hp in range(A_HEADS // 2):
                acc = None
                for hh in range(2):
                    h = 2 * hp + hh
                    Z = z_ref[rows, h * LANES:(h + 1) * LANES]
                    G, _, _, _, sv = _gmlp_head(Z, _tril(w_ref[h]), b_ref[h], lg_ref[h:h + 1, :], lb_ref[h:h + 1, :], maskv)
                    prod = G * pltpu.roll(sv, HEAD_DIM, axis=1)
                    acc = prod if hh == 0 else acc + pltpu.roll(prod, HEAD_DIM, axis=1)
                o_ref[rows, hp * LANES:(hp + 1) * LANES] = acc

    return pl.pallas_call(
        body, name=name, grid=(L // tl,),
        in_specs=[pl.BlockSpec((tl, IN_A), lambda i: (i, 0)),
                  pl.BlockSpec((A_HEADS, CHUNK, CHUNK), lambda i: (0, 0, 0)),
                  pl.BlockSpec((A_HEADS, CHUNK, LANES), lambda i: (0, 0, 0)),
                  pl.BlockSpec((A_HEADS, LANES), lambda i: (0, 0)),
                  pl.BlockSpec((A_HEADS, LANES), lambda i: (0, 0))],
        out_specs=pl.BlockSpec((tl, 2 * LANES), lambda i: (i, 0)),
        out_shape=_sds((L, 2 * LANES), F32),
    )(z, ws, bfull, lgf, lbf)


def _gmlp_bwd(z, dya, ws, wsT, bfull, lgf, lbf, name):
    L = z.shape[0]
    tl = _tile(L, _GM_TL)
    nch = tl // CHUNK
    nsteps = L // tl

    def body(z_ref, d_ref, w_ref, wt_ref, b_ref, lg_ref, lb_ref, dz_ref, dw_ref, db_ref, dlg_ref, dlb_ref):
        step = pl.program_id(0)

        @pl.when(step == 0)
        def _():
            dw_ref[...] = jnp.zeros_like(dw_ref)
            db_ref[...] = jnp.zeros_like(db_ref)
            dlg_ref[...] = jnp.zeros_like(dlg_ref)
            dlb_ref[...] = jnp.zeros_like(dlb_ref)

        lane = _lane((CHUNK, LANES))
        maskv = lane >= HEAD_DIM
        for c in range(nch):
            rows = slice(c * CHUNK, (c + 1) * CHUNK)
            for h in range(A_HEADS):
                hp, hh = divmod(h, 2)
                Z = z_ref[rows, h * LANES:(h + 1) * LANES]
                lg = lg_ref[h:h + 1, :]
                G, xhat, rstd, vn, sv = _gmlp_head(Z, _tril(w_ref[h]), b_ref[h], lg, lb_ref[h:h + 1, :], maskv)
                dpair = d_ref[rows, hp * LANES:(hp + 1) * LANES]
                if hh == 1:
                    dpair = pltpu.roll(dpair, HEAD_DIM, axis=1)
                dout = jnp.where(maskv, 0.0, dpair)
                du = dout * pltpu.roll(sv, HEAD_DIM, axis=1)
                dsv = pltpu.roll(dout * G, HEAD_DIM, axis=1)
                dw_ref[h] += _tril(_dot(dsv, vn, _NT))
                db_ref[h] += dsv
                dvn = _dot(_triu(wt_ref[h]), dsv)
                dlg_ref[h] += dvn * xhat
                dlb_ref[h] += dvn
                dxh = dvn * lg
                m1 = jnp.sum(dxh, axis=-1, keepdims=True) * (1.0 / HEAD_DIM)
                m2 = jnp.sum(dxh * xhat, axis=-1, keepdims=True) * (1.0 / HEAD_DIM)
                dv = jnp.where(maskv, rstd * (dxh - m1 - xhat * m2), 0.0)
                dz_ref[rows, h * LANES:(h + 1) * LANES] = ((du + dv) * _gelu_grad(Z)).astype(dz_ref.dtype)

        @pl.when(step == nsteps - 1)
        def _():
            for h in range(A_HEADS):
                db_ref[h] = jnp.broadcast_to(jnp.sum(db_ref[h], axis=1, keepdims=True), (CHUNK, LANES))
                dlg_ref[h] = jnp.broadcast_to(jnp.sum(dlg_ref[h], axis=0, keepdims=True), (CHUNK, LANES))
                dlb_ref[h] = jnp.broadcast_to(jnp.sum(dlb_ref[h], axis=0, keepdims=True), (CHUNK, LANES))

    full3 = pl.BlockSpec((A_HEADS, CHUNK, LANES), lambda i: (0, 0, 0))
    return pl.pallas_call(
        body, name=name, grid=(nsteps,),
        in_specs=[pl.BlockSpec((tl, IN_A), lambda i: (i, 0)),
                  pl.BlockSpec((tl, 2 * LANES), lambda i: (i, 0)),
                  full3, full3, full3,
                  pl.BlockSpec((A_HEADS, LANES), lambda i: (0, 0)),
                  pl.BlockSpec((A_HEADS, LANES), lambda i: (0, 0))],
        out_specs=[pl.BlockSpec((tl, IN_A), lambda i: (i, 0)), full3, full3, full3, full3],
        out_shape=[_sds((L, IN_A), _ACT)] + [_sds((A_HEADS, CHUNK, LANES), F32)] * 4,
    )(z, dya, ws, wsT, bfull, lgf, lbf)


def _head_rstd(x, lo):
    sq = x * x
    s_lo = jnp.sum(jnp.where(lo, sq, 0.0), axis=-1, keepdims=True)
    s_hi = jnp.sum(jnp.where(lo, 0.0, sq), axis=-1, keepdims=True)
    return jnp.where(lo, lax.rsqrt(s_lo * (1.0 / HEAD_DIM) + EPS), lax.rsqrt(s_hi * (1.0 / HEAD_DIM) + EPS))


def _rot_half(x, first):
    return jnp.where(first, -pltpu.roll(x, LANES - HEAD_DIM // 2, axis=1), pltpu.roll(x, HEAD_DIM // 2, axis=1))


def _qk_prep(z, cos, sin, gq, gk, name):
    L = z.shape[0]
    tl = _tile(L, _TL)
    nq = IN_Q // LANES

    def body(q_ref, k_ref, c_ref, s_ref, gq_ref, gk_ref, qo_ref, ko_ref):
        lane = _lane((tl, LANES))
        lo = lane < HEAD_DIM
        first = (lane % HEAD_DIM) < (HEAD_DIM // 2)
        c, s = c_ref[...], s_ref[...]

        def prep(x, g):
            xn = (x * _head_rstd(x, lo)) * g
            return xn * c + _rot_half(xn, first) * s

        for j in range(nq):
            qo_ref[:, j * LANES:(j + 1) * LANES] = prep(q_ref[:, j * LANES:(j + 1) * LANES], gq_ref[...]).astype(qo_ref.dtype)
        ko_ref[...] = prep(k_ref[...], gk_ref[...]).astype(ko_ref.dtype)

    return pl.pallas_call(
        body, name=name, grid=(L // tl,),
        in_specs=[pl.BlockSpec((tl, IN_Q), lambda i: (i, 1)),
                  pl.BlockSpec((tl, IN_KV), lambda i: (i, 8)),
                  pl.BlockSpec((tl, LANES), lambda i: (i, 0)), pl.BlockSpec((tl, LANES), lambda i: (i, 0)),
                  pl.BlockSpec((1, LANES), lambda i: (0, 0)), pl.BlockSpec((1, LANES), lambda i: (0, 0))],
        out_specs=[pl.BlockSpec((tl, IN_Q), lambda i: (i, 0)), pl.BlockSpec((tl, IN_KV), lambda i: (i, 0))],
        out_shape=[_sds((L, IN_Q), _ACT), _sds((L, IN_KV), _ACT)],
    )(z, z, cos, sin, gq, gk)


def _qk_prep_bwd(z, dq, dkc, dkp, dvc, dvp, cos, sin, gq, gk, name):
    L = z.shape[0]
    tl = _ATT_QB * WINDOW
    nb = L // tl
    nq = IN_Q // LANES

    def body(q_ref, k_ref, dq_ref, dkc_ref, dkp_ref, dvc_ref, dvp_ref, c_ref, s_ref, gq_ref, gk_ref,
             dzq_ref, dzk_ref, dzv_ref, dgq_ref, dgk_ref):
        n = pl.program_id(0)

        @pl.when(n == 0)
        def _():
            dgq_ref[...] = jnp.zeros_like(dgq_ref)
            dgk_ref[...] = jnp.zeros_like(dgk_ref)

        lane = _lane((tl, LANES))
        lo = lane < HEAD_DIM
        first = (lane % HEAD_DIM) < (HEAD_DIM // 2)
        c, s = c_ref[...], s_ref[...]
        has_next = jnp.where(n < nb - 1, 1.0, 0.0)

        def bwd(x, g, dy):
            r = _head_rstd(x, lo)
            xhat = x * r
            dxn = dy * c - _rot_half(dy * s, first)
            gy = dxn * g
            t = gy * xhat
            m_lo = jnp.sum(jnp.where(lo, t, 0.0), axis=-1, keepdims=True)
            m_hi = jnp.sum(jnp.where(lo, 0.0, t), axis=-1, keepdims=True)
            m = jnp.where(lo, m_lo, m_hi) * (1.0 / HEAD_DIM)
            dx = r * (gy - xhat * m)
            dg = jnp.sum(dxn * xhat, axis=0, keepdims=True)
            return dx, dg

        dgq = jnp.zeros((1, LANES), F32)
        for j in range(nq):
            sl = slice(j * LANES, (j + 1) * LANES)
            dx, dg = bwd(q_ref[:, sl], gq_ref[...], dq_ref[:, sl].astype(F32))
            dzq_ref[:, sl] = dx.astype(dzq_ref.dtype)
            dgq = dgq + dg
        dgq_ref[...] += dgq + pltpu.roll(dgq, HEAD_DIM, axis=1)
        def with_next(cur_ref, nxt_ref):
            head = jnp.zeros((tl - WINDOW, IN_KV), F32)
            return cur_ref[...] + jnp.concatenate([head, has_next * nxt_ref[...]], axis=0)

        dx, dg = bwd(k_ref[...], gk_ref[...], with_next(dkc_ref, dkp_ref))
        dzk_ref[...] = dx.astype(dzk_ref.dtype)
        dgk_ref[...] += dg + pltpu.roll(dg, HEAD_DIM, axis=1)
        dzv_ref[...] = with_next(dvc_ref, dvp_ref).astype(dzv_ref.dtype)

    nxt = lambda i: (jnp.minimum(i + 1, nb - 1), 0)
    cur = lambda i: (i, 0)
    kv = pl.BlockSpec((tl, IN_KV), cur)
    kvn = pl.BlockSpec((WINDOW, IN_KV), nxt)
    one = pl.BlockSpec((1, LANES), lambda i: (0, 0))
    return pl.pallas_call(
        body, name=name, grid=(nb,),
        in_specs=[pl.BlockSpec((tl, IN_Q), lambda i: (i, 1)), pl.BlockSpec((tl, IN_KV), lambda i: (i, 8)),
                  pl.BlockSpec((tl, IN_Q), cur), kv, kvn, kv, kvn,
                  kv, kv, one, one],
        out_specs=[pl.BlockSpec((tl, IN_Q), cur), kv, kv, one, one],
        out_shape=[_sds((L, IN_Q), _ACT), _sds((L, IN_KV), _ACT),
                   _sds((L, IN_KV), _ACT), _sds((1, LANES), F32),
                   _sds((1, LANES), F32)],
    )(z, z, dq, dkc, dkp, dvc, dvp, cos, sin, gq, gk)


def _attn_mask(n):
    shp = (2 * WINDOW, B_GROUP * WINDOW)
    qi = _lane(shp) % WINDOW
    kj = _row(shp)
    off = 0 if n is None else jnp.where(n > 0, 0, 4 * WINDOW)
    return ((kj >= WINDOW) & (kj - WINDOW <= qi)) | ((kj < WINDOW) & (kj > qi + off))


def _kv_lanes(j):
    lane = _lane((WINDOW, LANES))
    return (lane >= j * HEAD_DIM) & (lane < (j + 1) * HEAD_DIM)


_ATT_QB = 4


def _stack_heads(ref, rows, j, kvl):
    parts = []
    for g in range(B_GROUP):
        h = j * B_GROUP + g
        slab = ref[rows, (h // 2) * LANES:(h // 2 + 1) * LANES].astype(F32)
        if (h % 2) != j:
            slab = pltpu.roll(slab, HEAD_DIM, axis=1)
        parts.append(jnp.where(kvl, slab, 0.0))
    return jnp.concatenate(parts, axis=0)


def _attn_probs(qs, k2, sink_row, mask):
    s = _dot(k2, qs, _NT) * (HEAD_DIM ** -0.5)
    s = jnp.where(mask, s, NEG)
    m = jnp.maximum(jnp.max(s, axis=0, keepdims=True), sink_row)
    p = jnp.exp(s - m)
    esink = jnp.exp(sink_row - m)
    inv = 1.0 / (jnp.sum(p, axis=0, keepdims=True) + esink)
    return p * inv, esink * inv


def _sink_row(sink_ref, j):
    lane = _lane((1, B_GROUP * WINDOW))
    row = jnp.full((1, B_GROUP * WINDOW), sink_ref[j * B_GROUP], F32)
    for g in range(1, B_GROUP):
        row = jnp.where(lane >= g * WINDOW, sink_ref[j * B_GROUP + g], row)
    return row


def _attn_fwd(q, k, z, sinks, name):
    L = q.shape[0]
    QB = _ATT_QB
    tq = QB * WINDOW
    prev = lambda n: (jnp.maximum(QB * n - 1, 0), 0)
    prev_v = lambda n: (jnp.maximum(QB * n - 1, 0), 9)

    def body(s_ref, q_ref, kp_ref, kc_ref, vp_ref, vc_ref, o_ref):
        n = pl.program_id(0)
        k3 = jnp.concatenate([kp_ref[...], kc_ref[...]], axis=0)
        v3 = jnp.concatenate([vp_ref[...], vc_ref[...]], axis=0)
        for b in range(QB):
            rows = slice(b * WINDOW, (b + 1) * WINDOW)
            mask = _attn_mask(n if b == 0 else None)
            k2 = k3[b * WINDOW:(b + 2) * WINDOW]
            v2 = v3[b * WINDOW:(b + 2) * WINDOW]
            slabs = [None] * (IN_Q // LANES)
            for j in range(B_KV_HEADS):
                kvl = _kv_lanes(j)
                qs = _stack_heads(q_ref, rows, j, kvl)
                pn, _ = _attn_probs(qs, k2, _sink_row(s_ref, j), mask)
                o = _dot(pn, v2, _TN)
                for g in range(B_GROUP):
                    h = j * B_GROUP + g
                    piece = jnp.where(kvl, o[g * WINDOW:(g + 1) * WINDOW], 0.0)
                    if (h % 2) != j:
                        piece = pltpu.roll(piece, HEAD_DIM, axis=1)
                    slabs[h // 2] = piece if slabs[h // 2] is None else slabs[h // 2] + piece
            for t, sl in enumerate(slabs):
                o_ref[rows, t * LANES:(t + 1) * LANES] = sl

    return pl.pallas_call(
        body, name=name, grid=(L // tq,),
        in_specs=[pl.BlockSpec(memory_space=pltpu.SMEM),
                  pl.BlockSpec((tq, IN_Q), lambda n: (n, 0)),
                  pl.BlockSpec((WINDOW, IN_KV), prev), pl.BlockSpec((tq, IN_KV), lambda n: (n, 0)),
                  pl.BlockSpec((WINDOW, IN_KV), prev_v), pl.BlockSpec((tq, IN_KV), lambda n: (n, 9))],
        out_specs=pl.BlockSpec((tq, IN_Q), lambda n: (n, 0)),
        out_shape=_sds((L, IN_Q), F32),
    )(sinks, q, k, k, z, z)


def _attn_bwd(q, k, z, sinks, dyb, name):
    L = q.shape[0]
    QB = _ATT_QB
    tq = QB * WINDOW
    nsteps = L // tq
    prev = lambda n: (jnp.maximum(QB * n - 1, 0), 0)
    prev_v = lambda n: (jnp.maximum(QB * n - 1, 0), 9)
    cur = lambda n: (n, 0)

    def body(s_ref, q_ref, kp_ref, kc_ref, vp_ref, vc_ref, d_ref, dq_ref, dkc_ref, dkp_ref, dvc_ref, dvp_ref, ds_ref):
        n = pl.program_id(0)

        @pl.when(n == 0)
        def _():
            ds_ref[...] = jnp.zeros_like(ds_ref)

        k3 = jnp.concatenate([kp_ref[...], kc_ref[...]], axis=0)
        v3 = jnp.concatenate([vp_ref[...], vc_ref[...]], axis=0)
        dkb = [None] * (QB + 1)
        dvb = [None] * (QB + 1)
        dsink = jnp.zeros((1, LANES), F32)
        lane1 = _lane((1, LANES))
        add = lambda acc, v: v if acc is None else acc + v
        for b in range(QB):
            rows = slice(b * WINDOW, (b + 1) * WINDOW)
            mask = _attn_mask(n if b == 0 else None)
            k2 = k3[b * WINDOW:(b + 2) * WINDOW]
            v2 = v3[b * WINDOW:(b + 2) * WINDOW]
            slabs = [None] * (IN_Q // LANES)
            for j in range(B_KV_HEADS):
                kvl = _kv_lanes(j)
                qs = _stack_heads(q_ref, rows, j, kvl)
                dos = _stack_heads(d_ref, rows, j, kvl)
                pn, psink = _attn_probs(qs, k2, _sink_row(s_ref, j), mask)
                dp = _dot(v2, dos, _NT)
                dd = jnp.sum(pn * dp, axis=0, keepdims=True)
                dss = (pn * (dp - dd)) * (HEAD_DIM ** -0.5)
                dqs = _dot(dss, k2, _TN)
                dk2 = _dot(dss, qs)
                dv2 = _dot(pn, dos)
                dkb[b], dkb[b + 1] = add(dkb[b], dk2[:WINDOW]), add(dkb[b + 1], dk2[WINDOW:])
                dvb[b], dvb[b + 1] = add(dvb[b], dv2[:WINDOW]), add(dvb[b + 1], dv2[WINDOW:])
                sd = psink * dd
                for g in range(B_GROUP):
                    h = j * B_GROUP + g
                    piece = jnp.where(kvl, dqs[g * WINDOW:(g + 1) * WINDOW], 0.0)
                    if (h % 2) != j:
                        piece = pltpu.roll(piece, HEAD_DIM, axis=1)
                    slabs[h // 2] = piece if slabs[h // 2] is None else slabs[h // 2] + piece
                    tot = jnp.sum(sd[:, g * WINDOW:(g + 1) * WINDOW], axis=1, keepdims=True)
                    dsink = dsink - jnp.where(lane1 == h, tot, 0.0)
            for t, sl in enumerate(slabs):
                dq_ref[rows, t * LANES:(t + 1) * LANES] = sl
        dkp_ref[...] = dkb[0]
        dvp_ref[...] = dvb[0]
        for b in range(QB):
            dkc_ref[b * WINDOW:(b + 1) * WINDOW, :] = dkb[b + 1]
            dvc_ref[b * WINDOW:(b + 1) * WINDOW, :] = dvb[b + 1]
        ds_ref[0:1, :] += dsink

    kvs = pl.BlockSpec((tq, IN_KV), cur)
    kvp = pl.BlockSpec((WINDOW, IN_KV), cur)
    kvo = _sds((L, IN_KV), F32)
    kvpo = _sds((nsteps * WINDOW, IN_KV), F32)
    return pl.pallas_call(
        body, name=name, grid=(nsteps,),
        in_specs=[pl.BlockSpec(memory_space=pltpu.SMEM),
                  pl.BlockSpec((tq, IN_Q), cur),
                  pl.BlockSpec((WINDOW, IN_KV), prev), kvs,
                  pl.BlockSpec((WINDOW, IN_KV), prev_v), pl.BlockSpec((tq, IN_KV), lambda n: (n, 9)),
                  pl.BlockSpec((tq, IN_Q), cur)],
        out_specs=[pl.BlockSpec((tq, IN_Q), cur), kvs, kvp, kvs, kvp, pl.BlockSpec((SUBLANES, LANES), lambda n: (0, 0))],
        out_shape=[_sds((L, IN_Q), F32), kvo, kvpo, kvo, kvpo, _sds((SUBLANES, LANES), F32)],
    )(sinks, q, k, k, z, z, dyb)


def _ssm_disc(are, aim, ldt, bre, bim):
    dt = jnp.exp(ldt)
    mag = jnp.exp(are * dt)
    lr, li = mag * jnp.cos(aim * dt), mag * jnp.sin(aim * dt)
    den = are * are + aim * aim
    xr, xi = lr - 1.0, li
    cr, ci = (xr * are + xi * aim) / den, (xi * are - xr * aim) / den
    return lr, li, cr * bre - ci * bim, cr * bim + ci * bre


def _ssm_prep(are, aim, ldt, bre, bim):
    shp3, shpb = are.shape, bre.shape

    def body(are_ref, aim_ref, ldt_ref, bre_ref, bim_ref, lr_ref, li_ref, br_ref, bi_ref):
        lr, li, br, bi = _ssm_disc(are_ref[...], aim_ref[...], ldt_ref[...], bre_ref[...], bim_ref[...])
        lr_ref[...] = lr
        li_ref[...] = li
        br_ref[...] = br
        bi_ref[...] = bi

    return pl.pallas_call(
        body, name="ssm_prep",
        out_shape=[_sds(shp3, F32)] * 2 + [_sds(shpb, F32)] * 2,
    )(are, aim, ldt, bre, bim)


def _ssm_prep_bwd(are, aim, ldt, bre, bim, dlr, dli, dbr, dbi):
    shp3, shpb = are.shape, bre.shape

    def body(are_ref, aim_ref, ldt_ref, bre_ref, bim_ref, dlr_ref, dli_ref, dbr_ref, dbi_ref,
             o_are, o_aim, o_ldt, o_bre, o_bim):
        _, vjp = jax.vjp(_ssm_disc, are_ref[...], aim_ref[...], ldt_ref[...], bre_ref[...], bim_ref[...])
        g = vjp((dlr_ref[...], dli_ref[...], dbr_ref[...], dbi_ref[...]))
        o_are[...] = g[0]
        o_aim[...] = g[1]
        o_ldt[...] = jnp.broadcast_to(jnp.sum(g[2], axis=-1, keepdims=True), shp3)
        o_bre[...] = g[3]
        o_bim[...] = g[4]

    return pl.pallas_call(
        body, name="ssm_prep_bwd",
        out_shape=[_sds(shp3, F32)] * 3 + [_sds(shpb, F32)] * 2,
    )(are, aim, ldt, bre, bim, dlr, dli, dbr, dbi)


_SCAN_TB = 512
_SCAN_W = 512


def _cmul(ar, ai, br, bi):
    return ar * br - ai * bi, ar * bi + ai * br


def _ssm_scan(x, lam_r, lam_i, name, reverse=False, states=None):
    L = x.shape[0]
    tb = _tile(L, _SCAN_TB)
    nrb = L // tb
    nt = tb // SUBLANES
    W = _SCAN_W
    with_da = states is not None

    def body(*refs):
        if with_da:
            xr_ref, xi_ref, sr_ref, si_ref, ar_ref, ai_ref, o_ref, dar_ref, dai_ref, cr_ref, ci_ref = refs
        else:
            xr_ref, xi_ref, ar_ref, ai_ref, o_ref, cr_ref, ci_ref = refs
        step = pl.program_id(0)

        @pl.when(step == 0)
        def _():
            cr_ref[...] = jnp.zeros_like(cr_ref)
            ci_ref[...] = jnp.zeros_like(ci_ref)
            if with_da:
                dar_ref[...] = jnp.zeros_like(dar_ref)
                dai_ref[...] = jnp.zeros_like(dai_ref)

        row = _row((SUBLANES, W))

        def shift(v, d, fill):
            if reverse:
                return jnp.where(row < SUBLANES - d, pltpu.roll(v, SUBLANES - d, axis=0), fill)
            return jnp.where(row >= d, pltpu.roll(v, d, axis=0), fill)

        edge = 0 if reverse else SUBLANES - 1
        for wb in range(N_STATE // W):
            cols = slice(wb * W, (wb + 1) * W)
            a1r = jnp.broadcast_to(ar_ref[:, cols], (SUBLANES, W))
            a1i = jnp.broadcast_to(ai_ref[:, cols], (SUBLANES, W))
            if reverse:
                a1i = -a1i
            a2r, a2i = _cmul(a1r, a1i, a1r, a1i)
            a4r, a4i = _cmul(a2r, a2i, a2r, a2i)
            pws = ((1, a1r, a1i), (2, a2r, a2i), (4, a4r, a4i))
            pr, pi = a1r, a1i
            for d, _, _ in pws:
                qr, qi = _cmul(pr, pi, shift(pr, d, 1.0), shift(pi, d, 0.0))
                pr, pi = qr, qi
            mws = []
            for d, er, ei in pws:
                ok = (row < SUBLANES - d) if reverse else (row >= d)
                mws.append(((SUBLANES - d) if reverse else d, jnp.where(ok, er, 0.0), jnp.where(ok, ei, 0.0)))

            def tile(i, carry):
                cr, ci, dr, di = carry
                t = (nt - 1 - i) if reverse else i
                r0 = pl.multiple_of(t * SUBLANES, SUBLANES)
                vr = xr_ref[pl.ds(r0, SUBLANES), cols]
                vi = xi_ref[pl.ds(r0, SUBLANES), cols]
                for sh, er, ei in mws:
                    tr, ti = _cmul(er, ei, pltpu.roll(vr, sh, axis=0), pltpu.roll(vi, sh, axis=0))
                    vr, vi = vr + tr, vi + ti
                tr, ti = _cmul(pr, pi, cr, ci)
                vr, vi = vr + tr, vi + ti
                o_ref[pl.ds(r0, SUBLANES), cols] = vr
                o_ref[pl.ds(r0, SUBLANES), slice(N_STATE + wb * W, N_STATE + (wb + 1) * W)] = vi
                if with_da:
                    gr = jnp.where(row < SUBLANES - 1, pltpu.roll(vr, SUBLANES - 1, axis=0), cr)
                    gi = jnp.where(row < SUBLANES - 1, pltpu.roll(vi, SUBLANES - 1, axis=0), ci)
                    sr = sr_ref[pl.ds(r0, SUBLANES), cols]
                    si = si_ref[pl.ds(r0, SUBLANES), cols]
                    dr = dr + sr * gr + si * gi
                    di = di + sr * gi - si * gr
                ncr = jnp.broadcast_to(vr[edge:edge + 1, :], (SUBLANES, W))
                nci = jnp.broadcast_to(vi[edge:edge + 1, :], (SUBLANES, W))
                return ncr, nci, dr, di

            zero = jnp.zeros((SUBLANES, W), F32)
            cr, ci, dr, di = lax.fori_loop(0, nt, tile, (cr_ref[:, cols], ci_ref[:, cols], zero, zero), unroll=2)
            cr_ref[:, cols] = cr
            ci_ref[:, cols] = ci
            if with_da:
                dar_ref[:, cols] += dr
                dai_ref[:, cols] += di

        if with_da:
            @pl.when(step == nrb - 1)
            def _():
                dar_ref[...] = jnp.broadcast_to(jnp.sum(dar_ref[...], axis=0, keepdims=True), dar_ref.shape)
                dai_ref[...] = jnp.broadcast_to(jnp.sum(dai_ref[...], axis=0, keepdims=True), dai_ref.shape)

    rb = (lambda i: (nrb - 1 - i, 0)) if reverse else (lambda i: (i, 0))
    rb_im = (lambda i: (nrb - 1 - i, 1)) if reverse else (lambda i: (i, 1))
    blk_r = pl.BlockSpec((tb, N_STATE), rb)
    blk_i = pl.BlockSpec((tb, N_STATE), rb_im)
    one = pl.BlockSpec((1, N_STATE), lambda i: (0, 0))
    acc = pl.BlockSpec((SUBLANES, N_STATE), lambda i: (0, 0))
    ins = [x, x] + ([states, states] if with_da else []) + [lam_r, lam_i]
    in_specs = [blk_r, blk_i] + ([blk_r, blk_i] if with_da else []) + [one, one]
    out_specs = [pl.BlockSpec((tb, 2 * N_STATE), rb)] + ([acc, acc] if with_da else [])
    out_shape = [_sds((L, 2 * N_STATE), F32)] + (
        [_sds((SUBLANES, N_STATE), F32)] * 2 if with_da else [])
    outs = pl.pallas_call(
        body, name=name, grid=(nrb,), in_specs=in_specs, out_specs=out_specs, out_shape=out_shape,
        scratch_shapes=[pltpu.VMEM((SUBLANES, N_STATE), F32)] * 2,
        compiler_params=_cparams((6 if with_da else 4) * _nbytes((tb, N_STATE), F32),
                                 dimension_semantics=("arbitrary",)),
    )(*ins)
    return outs if with_da else outs[0]


def _scan_block(x_ref, o_ref, s_ref, ar_ref, ai_ref, cr_ref, ci_ref, dar_ref, dai_ref, nt, reverse):
    W = _SCAN_W
    with_da = s_ref is not None
    row = _row((SUBLANES, W))

    def shift(v, d, fill):
        if reverse:
            return jnp.where(row < SUBLANES - d, pltpu.roll(v, SUBLANES - d, axis=0), fill)
        return jnp.where(row >= d, pltpu.roll(v, d, axis=0), fill)

    edge = 0 if reverse else SUBLANES - 1
    for wb in range(N_STATE // W):
        cols = slice(wb * W, (wb + 1) * W)
        icols = slice(N_STATE + wb * W, N_STATE + (wb + 1) * W)
        a1r = jnp.broadcast_to(ar_ref[:, cols], (SUBLANES, W))
        a1i = jnp.broadcast_to(ai_ref[:, cols], (SUBLANES, W))
        if reverse:
            a1i = -a1i
        a2r, a2i = _cmul(a1r, a1i, a1r, a1i)
        a4r, a4i = _cmul(a2r, a2i, a2r, a2i)
        pws = ((1, a1r, a1i), (2, a2r, a2i), (4, a4r, a4i))
        pr, pi = a1r, a1i
        for d, _, _ in pws:
            qr, qi = _cmul(pr, pi, shift(pr, d, 1.0), shift(pi, d, 0.0))
            pr, pi = qr, qi
        mws = []
        for d, er, ei in pws:
            ok = (row < SUBLANES - d) if reverse else (row >= d)
            mws.append(((SUBLANES - d) if reverse else d, jnp.where(ok, er, 0.0), jnp.where(ok, ei, 0.0)))

        def tile(i, carry):
            cr, ci, dr, di = carry
            t = (nt - 1 - i) if reverse else i
            r0 = pl.multiple_of(t * SUBLANES, SUBLANES)
            vr = x_ref[pl.ds(r0, SUBLANES), cols]
            vi = x_ref[pl.ds(r0, SUBLANES), icols]
            for sh, er, ei in mws:
                tr, ti = _cmul(er, ei, pltpu.roll(vr, sh, axis=0), pltpu.roll(vi, sh, axis=0))
                vr, vi = vr + tr, vi + ti
            tr, ti = _cmul(pr, pi, cr, ci)
            vr, vi = vr + tr, vi + ti
            o_ref[pl.ds(r0, SUBLANES), cols] = vr
            o_ref[pl.ds(r0, SUBLANES), icols] = vi
            if with_da:
                gr = jnp.where(row < SUBLANES - 1, pltpu.roll(vr, SUBLANES - 1, axis=0), cr)
                gi = jnp.where(row < SUBLANES - 1, pltpu.roll(vi, SUBLANES - 1, axis=0), ci)
                sr = s_ref[pl.ds(r0, SUBLANES), cols]
                si = s_ref[pl.ds(r0, SUBLANES), icols]
                dr = dr + sr * gr + si * gi
                di = di + sr * gi - si * gr
            ncr = jnp.broadcast_to(vr[edge:edge + 1, :], (SUBLANES, W))
            nci = jnp.broadcast_to(vi[edge:edge + 1, :], (SUBLANES, W))
            return ncr, nci, dr, di

        zero = jnp.zeros((SUBLANES, W), F32)
        cr, ci, dr, di = lax.fori_loop(0, nt, tile, (cr_ref[:, cols], ci_ref[:, cols], zero, zero), unroll=2)
        cr_ref[:, cols] = cr
        ci_ref[:, cols] = ci
        if with_da:
            dar_ref[:, cols] += dr
            dai_ref[:, cols] += di


def _carry(comm, operands, in_specs, out_shape, out_specs, aliases, scratch):
    if comm is None:
        return 0, 0
    n0, no0 = len(operands), len(out_shape)
    operands += list(comm.ins)
    in_specs += [pl.BlockSpec(memory_space=pl.ANY)] * len(comm.ins)
    for t, x_ in enumerate(comm.ins):
        if comm.aliased[t]:
            aliases[n0 + t] = len(out_shape)
            out_shape.append(_sds(x_.shape, x_.dtype))
    out_shape += [_sds(sh, dt) for sh, dt in comm.fresh]
    out_specs += [pl.BlockSpec(memory_space=pl.ANY)] * (len(out_shape) - no0)
    scratch += [pltpu.SemaphoreType.DMA((comm.n_sems,))] * 2
    return len(comm.ins), len(out_shape) - no0


def _ssm_fwd(z, bcat, ccat, dskip, lam_r, lam_i, name, comm=None):
    L = z.shape[0]
    tb = _tile(L, _SCAN_TB)
    nrb = L // tb
    nt = tb // SUBLANES
    full = lambda shp: pl.BlockSpec(shp, lambda i: (0, 0))
    rows = lambda w: pl.BlockSpec((tb, w), lambda i: (i, 0))
    operands = [_hbm(z), bcat, ccat, dskip, lam_r, lam_i]
    in_specs = [pl.BlockSpec((tb, C_WIDTH), lambda i: (i, 5)), full((C_WIDTH, 2 * N_STATE)), full((2 * N_STATE, C_WIDTH)),
                full((1, C_WIDTH)), full((1, N_STATE)), full((1, N_STATE))]
    out_specs = [rows(2 * N_STATE), rows(C_WIDTH), rows(C_WIDTH)]
    out_shape = [_sds((L, 2 * N_STATE), F32), _sds((L, C_WIDTH), F32), _sds((L, C_WIDTH), _ACT)]
    scratch = [pltpu.VMEM((tb, 2 * N_STATE), F32)] + [pltpu.VMEM((SUBLANES, N_STATE), F32)] * 2
    aliases = {}
    nxi, nxo = _carry(comm, operands, in_specs, out_shape, out_specs, aliases, scratch)

    def body(*refs):
        u_ref, b_ref, c_ref, d_ref, ar_ref, ai_ref = refs[:6]
        x_ins = refs[6:6 + nxi]
        s_ref, y_ref, yg_ref = refs[6 + nxi:9 + nxi]
        x_outs = refs[9 + nxi:9 + nxi + nxo]
        xs_ref, cr_ref, ci_ref = refs[9 + nxi + nxo:12 + nxi + nxo]
        sems = refs[12 + nxi + nxo:]
        step = pl.program_id(0)

        @pl.when(step == 0)
        def _():
            cr_ref[...] = jnp.zeros_like(cr_ref)
            ci_ref[...] = jnp.zeros_like(ci_ref)
            if comm is not None:
                comm.start(x_ins, x_outs, *sems)

        u = u_ref[...]
        xs_ref[...] = _dot(u, b_ref[...])
        _scan_block(xs_ref, s_ref, None, ar_ref, ai_ref, cr_ref, ci_ref, None, None, nt, False)
        y = _dot(s_ref[...], c_ref[...]) + d_ref[...] * u
        y_ref[...] = y
        yg_ref[...] = _gelu(y).astype(yg_ref.dtype)

        if comm is not None:
            @pl.when(step == nrb - 1)
            def _():
                comm.wait(x_ins, x_outs, *sems)

    outs = pl.pallas_call(
        body, name=name, grid=(nrb,), in_specs=in_specs, out_specs=out_specs, out_shape=out_shape,
        scratch_shapes=scratch, input_output_aliases=aliases,
        compiler_params=_cparams(5 * _nbytes((tb, 2 * N_STATE), F32), dimension_semantics=("arbitrary",)),
    )(*operands)
    if comm is not None:
        comm.done(list(outs[3:]))
    return outs[0], outs[1], outs[2]


def _ssm_bwd(dy, z, S, bcat, ccat, dskip, lam_r, lam_i, name, comm=None):
    L = z.shape[0]
    tb = _tile(L, _SCAN_TB)
    nrb = L // tb
    nt = tb // SUBLANES

    full = lambda shp: pl.BlockSpec(shp, lambda i: (0, 0))
    rows = lambda w, col=0: pl.BlockSpec((tb, w), lambda i: (nrb - 1 - i, col))
    acc = full((SUBLANES, N_STATE))
    operands = [dy, _hbm(z), _hbm(S), bcat, ccat, dskip, lam_r, lam_i]
    in_specs = [rows(C_WIDTH), rows(C_WIDTH, 5), rows(2 * N_STATE), full((C_WIDTH, 2 * N_STATE)),
                full((2 * N_STATE, C_WIDTH)), full((1, C_WIDTH)), full((1, N_STATE)), full((1, N_STATE))]
    out_specs = [rows(C_WIDTH), full((C_WIDTH, 2 * N_STATE)), full((2 * N_STATE, C_WIDTH)), full((1, C_WIDTH)), acc, acc]
    out_shape = [_sds((L, C_WIDTH), _ACT), _sds((C_WIDTH, 2 * N_STATE), F32), _sds((2 * N_STATE, C_WIDTH), F32),
                 _sds((1, C_WIDTH), F32), _sds((SUBLANES, N_STATE), F32), _sds((SUBLANES, N_STATE), F32)]
    scratch = [pltpu.VMEM((tb, 2 * N_STATE), F32)] * 2 + [pltpu.VMEM((SUBLANES, N_STATE), F32)] * 2
    aliases = {}
    nxi, nxo = _carry(comm, operands, in_specs, out_shape, out_specs, aliases, scratch)

    def body(*refs):
        dy_ref, u_ref, s_ref, b_ref, c_ref, d_ref, ar_ref, ai_ref = refs[:8]
        x_ins = refs[8:8 + nxi]
        du_ref, db_ref, dc_ref, dd_ref, dar_ref, dai_ref = refs[8 + nxi:14 + nxi]
        x_outs = refs[14 + nxi:14 + nxi + nxo]
        xs_ref, gs_ref, cr_ref, ci_ref = refs[14 + nxi + nxo:18 + nxi + nxo]
        sems = refs[18 + nxi + nxo:]
        step = pl.program_id(0)

        @pl.when(step == 0)
        def _():
            for r in (cr_ref, ci_ref, db_ref, dc_ref, dd_ref, dar_ref, dai_ref):
                r[...] = jnp.zeros_like(r)
            if comm is not None:
                comm.start(x_ins, x_outs, *sems)

        dyv, u = dy_ref[...], u_ref[...]
        xs_ref[...] = _dot(dyv, c_ref[...], _NT)
        _scan_block(xs_ref, gs_ref, s_ref, ar_ref, ai_ref, cr_ref, ci_ref, dar_ref, dai_ref, nt, True)
        g = gs_ref[...]
        du_ref[...] = (_dot(g, b_ref[...], _NT) + dyv * d_ref[...]).astype(du_ref.dtype)
        db_ref[...] += _dot(u, g, _TN)
        dc_ref[...] += _dot(s_ref[...], dyv, _TN)
        dd_ref[...] += jnp.sum(dyv * u, axis=0, keepdims=True)

        @pl.when(step == nrb - 1)
        def _():
            dar_ref[...] = jnp.broadcast_to(jnp.sum(dar_ref[...], axis=0, keepdims=True), dar_ref.shape)
            dai_ref[...] = jnp.broadcast_to(jnp.sum(dai_ref[...], axis=0, keepdims=True), dai_ref.shape)
            if comm is not None:
                comm.wait(x_ins, x_outs, *sems)

    outs = pl.pallas_call(
        body, name=name, grid=(nrb,), in_specs=in_specs, out_specs=out_specs, out_shape=out_shape,
        scratch_shapes=scratch, input_output_aliases=aliases,
        compiler_params=_cparams(7 * _nbytes((tb, 2 * N_STATE), F32), dimension_semantics=("arbitrary",)),
    )(*operands)
    if comm is not None:
        comm.done(list(outs[6:]))
    return tuple(outs[:6])


_GROUPS = ((0, 256), (256, 768), (768, 1024))


def _merge_fwd(ya, yb, g12, mixg, name):
    L = ya.shape[0]
    tl = _tile(L, _TL)

    def body(a_ref, b_ref, g_ref, m_ref, o_ref):
        g12v = g_ref[...]
        yc = g12v[:, :C_WIDTH] * _sigmoid(g12v[:, C_WIDTH:])
        for (lo, hi), y in zip(_GROUPS, (a_ref[...], b_ref[...], yc)):
            r = lax.rsqrt(jnp.mean(y * y, axis=-1, keepdims=True) + EPS)
            o_ref[:, lo:hi] = ((y * r) * m_ref[:, lo:hi]).astype(o_ref.dtype)

    row = lambda w: pl.BlockSpec((tl, w), lambda i: (i, 0))
    return pl.pallas_call(
        body, name=name, grid=(L // tl,),
        in_specs=[row(256), row(512), row(512), pl.BlockSpec((1, D_MODEL), lambda i: (0, 0))],
        out_specs=row(D_MODEL), out_shape=_sds((L, D_MODEL), _ACT),
    )(ya, yb, g12, mixg.reshape(1, D_MODEL))


def _merge_bwd(dy, ya, yb, g12, mixg, name):
    L = ya.shape[0]
    tl = _tile(L, _TL)

    def body(d_ref, a_ref, b_ref, g_ref, m_ref, da_ref, db_ref, dg_ref, dm_ref):
        @pl.when(pl.program_id(0) == 0)
        def _():
            dm_ref[...] = jnp.zeros_like(dm_ref)

        g12v = g_ref[...]
        g1, sg = g12v[:, :C_WIDTH], _sigmoid(g12v[:, C_WIDTH:])
        yc = g1 * sg
        outs = []
        for (lo, hi), y in zip(_GROUPS, (a_ref[...], b_ref[...], yc)):
            r = lax.rsqrt(jnp.mean(y * y, axis=-1, keepdims=True) + EPS)
            xhat = y * r
            d = d_ref[:, lo:hi]
            gy = d * m_ref[:, lo:hi]
            outs.append(r * (gy - xhat * jnp.mean(gy * xhat, axis=-1, keepdims=True)))
            dm_ref[:, lo:hi] += jnp.sum(d * xhat, axis=0, keepdims=True)
        da_ref[...] = outs[0]
        db_ref[...] = outs[1]
        dyc = outs[2]
        dg_ref[:, :C_WIDTH] = (dyc * sg).astype(dg_ref.dtype)
        dg_ref[:, C_WIDTH:] = (dyc * g1 * sg * (1.0 - sg)).astype(dg_ref.dtype)

    row = lambda w: pl.BlockSpec((tl, w), lambda i: (i, 0))
    one = pl.BlockSpec((1, D_MODEL), lambda i: (0, 0))
    return pl.pallas_call(
        body, name=name, grid=(L // tl,),
        in_specs=[row(D_MODEL), row(256), row(512), row(512), one],
        out_specs=[row(256), row(512), row(512), one],
        out_shape=[_sds((L, 256), F32), _sds((L, 512), F32),
                   _sds((L, 512), _ACT), _sds((1, D_MODEL), F32)],
    )(dy, ya, yb, g12, mixg.reshape(1, D_MODEL))


def _ple_bwd_elem(dh, gate, e, name):
    L, D = dh.shape
    tl = _tile(L, _TL)

    def body(d_ref, g_ref, e_ref, p_ref, o_ref):
        d, g = d_ref[...], g_ref[...]
        p_ref[...] = (d * e_ref[...] * g * (1.0 - g)).astype(p_ref.dtype)
        o_ref[...] = (d * g).astype(o_ref.dtype)

    row = pl.BlockSpec((tl, D), lambda i: (i, 0))
    return pl.pallas_call(
        body, name=name, grid=(L // tl,), in_specs=[row] * 3, out_specs=[row] * 2,
        out_shape=[_sds((L, D), _ACT)] * 2,
        compiler_params=_cparams(4 * _nbytes((tl, D), F32)),
    )(dh, gate, e)


def _dskip_bwd(dy, z, name):
    L = dy.shape[0]
    tl = _tile(L, _TL)

    def body(d_ref, u_ref, o_ref):
        @pl.when(pl.program_id(0) == 0)
        def _():
            o_ref[...] = jnp.zeros_like(o_ref)

        o_ref[...] += jnp.sum(d_ref[...] * u_ref[...], axis=0, keepdims=True)

    return pl.pallas_call(
        body, name=name, grid=(L // tl,),
        in_specs=[pl.BlockSpec((tl, C_WIDTH), lambda i: (i, 0)), pl.BlockSpec((tl, C_WIDTH), lambda i: (i, 5))],
        out_specs=pl.BlockSpec((1, C_WIDTH), lambda i: (0, 0)),
        out_shape=_sds((1, C_WIDTH), F32),
    )(dy, z)


def _loss_fwd_bwd(y, target):
    L, D = y.shape
    tl = _tile(L, _TL)

    def body(y_ref, t_ref, l_ref, d_ref):
        @pl.when(pl.program_id(0) == 0)
        def _():
            l_ref[...] = jnp.zeros_like(l_ref)

        e = y_ref[...] - t_ref[...]
        d_ref[...] = e * (1.0 / D)
        part = jnp.sum(jnp.sum(e * e, axis=-1, keepdims=True), axis=0, keepdims=True)
        l_ref[...] += jnp.broadcast_to(part, l_ref.shape)

    row = pl.BlockSpec((tl, D), lambda i: (i, 0))
    return pl.pallas_call(
        body, name="loss", grid=(L // tl,), in_specs=[row, row],
        out_specs=[pl.BlockSpec((SUBLANES, LANES), lambda i: (0, 0)), row],
        out_shape=[_sds((SUBLANES, LANES), F32), _sds((L, D), F32)],
    )(y, target)


def _adamw(w, g, m, v, name, emit_g=False):
    R, C = w.shape
    tr = R if R <= 512 else _tile_rows(R, 512)

    def body(w_ref, g_ref, m_ref, v_ref, d_ref, nm_ref, nv_ref, *g_out):
        gv = g_ref[...]
        if emit_g:
            g_out[0][...] = gv
        nm = ADAM_B1 * m_ref[...] + (1.0 - ADAM_B1) * gv
        nv = ADAM_B2 * v_ref[...] + (1.0 - ADAM_B2) * (gv * gv)
        m_hat = nm / (1.0 - ADAM_B1 ** ADAM_STEP)
        v_hat = nv / (1.0 - ADAM_B2 ** ADAM_STEP)
        d_ref[...] = -ADAM_LR * (m_hat / (jnp.sqrt(v_hat) + ADAM_EPS) + ADAM_WD * w_ref[...])
        nm_ref[...] = nm
        nv_ref[...] = nv

    blk = pl.BlockSpec((tr, C), lambda i: (i, 0))
    return pl.pallas_call(
        body, name=name, grid=(R // tr,), in_specs=[blk] * 4, out_specs=[blk] * (3 + emit_g),
        out_shape=[_sds((R, C), F32)] * (3 + emit_g),
        compiler_params=_cparams(8 * _nbytes((tr, C), F32)),
    )(w, g, m, v)


def _tile_rows(R, pref):
    t = pref
    while R % t:
        t -= SUBLANES
    assert t > 0
    return t


def _add_n(xs, name):
    R, C = xs[0].shape
    tr = R if R <= 512 else _tile_rows(R, 512)
    n = len(xs)

    def body(*refs):
        acc = refs[0][...].astype(F32)
        for r in refs[1:n]:
            acc = acc + r[...].astype(F32)
        refs[n][...] = acc

    blk = pl.BlockSpec((tr, C), lambda i: (i, 0))
    return pl.pallas_call(
        body, name=name, grid=(R // tr,), in_specs=[blk] * n, out_specs=blk,
        out_shape=_sds((R, C), F32),
        compiler_params=_cparams((n + 1) * _nbytes((tr, C), F32)),
    )(*xs)


class _Exchange:
    def __init__(self, ins, aliased, fresh, n_sems, start, wait, done):
        self.ins, self.aliased, self.fresh, self.n_sems = ins, aliased, fresh, n_sems
        self.start, self.wait, self.done = start, wait, done


def _mm_host(lp, key, *args, **kw):
    plan = lp.get(key)
    if plan is None:
        return _mm(*args, **kw)
    if not isinstance(plan, _Exchange):
        plan = plan()
    res, outs = _mm(*args, comm=plan, **kw)
    plan.done(outs)
    return res


def _relu2(acc):
    r = jnp.maximum(acc, 0.0)
    return (r * r,)


def _rms_rows(x, g):
    return (x * lax.rsqrt(jnp.mean(x * x, axis=-1, keepdims=True) + EPS)) * g


def _resid_norm_epi(acc, res, g):
    h = res + acc
    return h, _rms_rows(h, g)


def _rms_bwd_epi(acc, h, dres, g):
    r = lax.rsqrt(jnp.mean(h * h, axis=-1, keepdims=True) + EPS)
    xhat = h * r
    gy = acc * g
    dh = dres + r * (gy - xhat * jnp.mean(gy * xhat, axis=-1, keepdims=True))
    return dh, dh, jnp.sum(acc * xhat, axis=0, keepdims=True)


def _layer_fwd(h, xn, lp, cos, sin, g_next):
    L = h.shape[0]
    row = lambda n: lp[n].reshape(1, D_MODEL)
    z = _mm(xn, lp["W"]("w_in"), mode="nn", M=L, N=IN_COLS, K=D_MODEL, b_cb=True, out_dtypes=[F32], name="f_w_in")
    ya = _gmlp_fwd(z, lp["ws"], lp["bfull"], lp["lgf"], lp["lbf"], "f_gmlp")
    q, k = _qk_prep(z, cos, sin, lp["gq"], lp["gk"], "f_qk_prep")
    yb = _attn_fwd(q, k, z, lp["sinks"], "f_attn")
    plan = lp.get("x_ssm")
    S, y, yg = _ssm_fwd(z, lp["bcat"], lp["ccat"], lp["dskip"], lp["lam_r"], lp["lam_i"], "f_ssm",
                        comm=None if plan is None else plan())
    g12 = _mm(yg, lp["W"]("w12"), mode="nn", M=L, N=2 * C_WIDTH, K=C_WIDTH, out_dtypes=[F32], name="f_glu")
    ycat = _merge_fwd(ya, yb, g12, lp["mix_out_g"], "f_merge")
    h1, hn = _mm_host(lp, "x_out", ycat, lp["W"]("w_out"), mode="nn", M=L, N=D_MODEL, K=D_MODEL, extras=[(h, 0), (row("mlp_norm_g"), 0)],
                 epi=_resid_norm_epi, out_dtypes=[F32, _ACT], name="f_w_out")
    r = _mm_host(lp, "x_ff1", hn, lp["W"]("w_ff1"), mode="nn", M=L, N=D_FF, K=D_MODEL, b_cb=True, epi=_relu2,
                 out_dtypes=[_ACT], name="f_ff1")
    h2, hn3 = _mm_host(lp, "x_ff2", r, lp["W"]("w_ff2"), mode="nn", M=L, N=D_MODEL, K=D_FF,
                       extras=[(h1, 0), (row("ple_norm_g"), 0)],
                       epi=_resid_norm_epi, out_dtypes=[F32, _ACT], name="f_ff2")
    e = _mm(lp["p"], lp["W"]("w_ple_proj"), mode="nn", M=L, N=D_MODEL, K=PLE_DIM, b_cb=True, tk=PLE_DIM,
            out_dtypes=[F32], name="f_ple_proj")

    def gate_epi(acc, h2_, e_, *g):
        gate_ = _sigmoid(acc)
        h3_ = h2_ + gate_ * e_
        return (h3_, gate_) + ((_rms_rows(h3_, g[0]),) if g else ())

    outs = _mm_host(lp, "x_gate", hn3, lp["W"]("w_ple_gate"), mode="nn", M=L, N=D_MODEL, K=D_MODEL,
                    extras=[(h2, 0), (e, 0)] + ([(g_next.reshape(1, D_MODEL), 0)] if g_next is not None else []),
                    epi=gate_epi, out_dtypes=[F32, F32] + ([_ACT] if g_next is not None else []), name="f_ple_gate")
    h3, gate = outs[0], outs[1]
    xn_next = outs[2] if g_next is not None else None
    saved = dict(h=h, xn=xn, z=z, ya=ya, q=q, k=k, yb=yb, S=S, y=y, yg=yg, g12=g12, ycat=ycat, h1=h1, hn=hn,
                 r=r, h2=h2, hn3=hn3, e=e, gate=gate)
    return h3, xn_next, saved


def _layer_bwd(dh3, lp, sv, cos, sin):
    L = dh3.shape[0]
    z = sv["z"]
    dpre, de = _ple_bwd_elem(dh3, sv["gate"], sv["e"], "b_ple_elem")
    stk = {n: None for n in BIG}
    d_gate = _mm(sv["hn3"], dpre, mode="tn", M=D_MODEL, N=D_MODEL, K=L, out_dtypes=[F32], name="b_dw_gate",
                 o_stack=stk["w_ple_gate"])
    d_proj = _mm(lp["p"], de, mode="tn", M=PLE_DIM, N=D_MODEL, K=L, o_cb=True, tm=PLE_DIM,
                 out_dtypes=[F32], name="b_dw_proj", o_stack=stk["w_ple_proj"])
    row = lambda n: lp[n].reshape(1, D_MODEL)
    dh2, dh2_op, dg_ple = _mm_host(lp, "x_bwd0", dpre, lp["W"]("w_ple_gate"), mode="nt", M=L, N=D_MODEL, K=D_MODEL,
                                   extras=[(sv["h2"], 0), (dh3, 0), (row("ple_norm_g"), 0)], epi=_rms_bwd_epi,
                                   out_dtypes=[F32, _ACT, F32], n_acc=1, name="b_dx_gate")
    da = _mm_host(lp, "x_bwd", dh2_op, lp["W"]("w_ff2"), mode="nt", M=L, N=D_FF, K=D_MODEL, extras=[(sv["r"], 0)],
                  epi=lambda acc, r_: (acc * (2.0 * jnp.sqrt(r_.astype(F32))),), out_dtypes=[_ACT], name="b_dx_ff2")
    d_ff2 = _mm_host(lp, "x_bwd2", sv["r"], dh2_op, mode="tn", M=D_FF, N=D_MODEL, K=L, out_dtypes=[F32], name="b_dw_ff2")
    d_ff1 = _mm(sv["hn"], da, mode="tn", M=D_MODEL, N=D_FF, K=L, o_cb=True, out_dtypes=[F32], name="b_dw_ff1",
                o_stack=stk["w_ff1"])
    dh1, dh1_op, dg_mlp = _mm(da, lp["W"]("w_ff1"), mode="nt", M=L, N=D_MODEL, K=D_FF, b_cb=True,
                              extras=[(sv["h1"], 0), (dh2, 0), (row("mlp_norm_g"), 0)], epi=_rms_bwd_epi,
                              out_dtypes=[F32, _ACT, F32], n_acc=1, name="b_dx_ff1")
    d_out = _mm(sv["ycat"], dh1_op, mode="tn", M=D_MODEL, N=D_MODEL, K=L, out_dtypes=[F32], name="b_dw_out",
                o_stack=stk["w_out"])
    if "early" in lp:
        lp["early"](dict(w_out=d_out, w_ff1=d_ff1, w_ff2=d_ff2, w_ple_gate=d_gate, w_ple_proj=d_proj))
    dycat = _mm_host(lp, "x_e0", dh1_op, lp["W"]("w_out"), mode="nt", M=L, N=D_MODEL, K=D_MODEL, out_dtypes=[F32],
                     name="b_dx_out")
    dya, dyb, dg12, dmix = _merge_bwd(dycat, sv["ya"], sv["yb"], sv["g12"], lp["mix_out_g"], "b_merge")
    d_w12 = _mm(sv["yg"], dg12, mode="tn", M=C_WIDTH, N=2 * C_WIDTH, K=L, tm=C_WIDTH, out_dtypes=[F32], name="b_dw_glu",
                o_stack=stk["w12"])
    dy = _mm(dg12, lp["W"]("w12"), mode="nt", M=L, N=C_WIDTH, K=2 * C_WIDTH, tk=2 * C_WIDTH, extras=[(sv["y"], 0)],
             epi=lambda acc, y_: (acc * _gelu_grad(y_),), out_dtypes=[F32], name="b_dx_glu")
    plan = lp.get("x_e1")
    dzc, d_bcat, d_ccat, dd, dar, dai = _ssm_bwd(dy, z, sv["S"], lp["bcat"], lp["ccat"], lp["dskip"],
                                                 lp["lam_r"], lp["lam_i"], "b_ssm",
                                                 comm=None if plan is None else plan())
    dq, dkc, dkp, dvc, dvp, dsink = _attn_bwd(sv["q"], sv["k"], z, lp["sinks"], dyb, "b_attn")
    dzq, dzk, dzv, dgq, dgk = _qk_prep_bwd(z, dq, dkc, dkp, dvc, dvp, cos, sin, lp["gq"], lp["gk"], "b_qk_prep")
    dza, dws, dbs, dlg, dlb = _gmlp_bwd(z, dya, lp["ws"], lp["wsT"], lp["bfull"], lp["lgf"], lp["lbf"], "b_gmlp")
    dz = jnp.concatenate([dza, dzq, dzk, dzv, dzc], axis=1)
    d_in = _mm(sv["xn"], dz, mode="tn", M=D_MODEL, N=IN_COLS, K=L, o_cb=True, out_dtypes=[F32], name="b_dw_in",
               o_stack=stk["w_in"])
    dh, dg_attn = _mm(dz, lp["W"]("w_in"), mode="nt", M=L, N=D_MODEL, K=IN_COLS, b_cb=True,
                         extras=[(sv["h"], 0), (dh1, 0), (row("attn_norm_g"), 0)],
                         epi=lambda *t: (lambda o: (o[0], o[2]))(_rms_bwd_epi(*t)),
                         out_dtypes=[F32, F32], n_acc=1, name="b_dx_in")
    grads = dict(w_in=d_in, w12=d_w12, w_out=d_out, w_ff1=d_ff1, w_ff2=d_ff2, w_ple_gate=d_gate, w_ple_proj=d_proj,
                 attn_norm_g=dg_attn.reshape(D_MODEL), mlp_norm_g=dg_mlp.reshape(D_MODEL),
                 ple_norm_g=dg_ple.reshape(D_MODEL), mix_out_g=dmix.reshape(D_MODEL),
                 dws=dws, dbs=dbs, dlg=dlg, dlb=dlb, dgq=dgq, dgk=dgk, dsink=dsink,
                 dar=dar, dai=dai, d_bcat=d_bcat, d_ccat=d_ccat, dd=dd)
    return dh, grads


SMALL = ("attn_norm_g", "gmlp_ln_g", "gmlp_ln_b", "gmlp_ws", "gmlp_bs", "q_norm_g", "k_norm_g", "sinks",
         "ssm_a_re", "ssm_a_im", "ssm_log_dt", "ssm_b_re", "ssm_b_im", "ssm_c_re", "ssm_c_im", "ssm_d",
         "mix_out_g", "mlp_norm_g", "ple_norm_g")
BIG = ("w_in", "w12", "w_out", "w_ff1", "w_ff2", "w_ple_gate", "w_ple_proj")
COL_SHARDED = ("w_in", "w_ff1", "w_ple_proj")


def _block_diag(t):
    nl, g, a, b = t.shape
    eye = jnp.eye(g, dtype=t.dtype)
    return (t[:, :, :, None, :] * eye[None, :, None, :, None]).reshape(nl, g * a, g * b)


def _diag_blocks(t, a, b):
    nl = t.shape[0]
    t = t.reshape(nl, C_GROUPS, a, C_GROUPS, b)
    idx = jnp.arange(C_GROUPS)
    return jnp.moveaxis(t[:, idx, :, idx, :], 0, 1)


def _local_step(x, p, positions, target, sw, bw):
    nl = sw["attn_norm_g"].shape[0]
    G = nl * C_GROUPS
    zeros = lambda *s: jnp.zeros(s, F32)
    are = sw["ssm_a_re"].reshape(G, 1, C_STATE)
    aim = sw["ssm_a_im"].reshape(G, 1, C_STATE)
    ldt = jnp.broadcast_to(sw["ssm_log_dt"][..., None], (nl, C_GROUPS, C_STATE)).reshape(G, 1, C_STATE)
    bre = jnp.swapaxes(sw["ssm_b_re"], -1, -2).reshape(G, C_GROUP, C_STATE)
    bim = jnp.swapaxes(sw["ssm_b_im"], -1, -2).reshape(G, C_GROUP, C_STATE)
    lr, li, bbr, bbi = _ssm_prep(are, aim, ldt, bre, bim)
    unflat = lambda t: t.reshape(nl, C_GROUPS, C_GROUP, C_STATE)
    lp = dict(
        attn_norm_g=sw["attn_norm_g"], mlp_norm_g=sw["mlp_norm_g"], ple_norm_g=sw["ple_norm_g"],
        mix_out_g=sw["mix_out_g"], sinks=sw["sinks"],
        ws=sw["gmlp_ws"], wsT=jnp.swapaxes(sw["gmlp_ws"], -1, -2),
        bfull=jnp.concatenate([zeros(nl, A_HEADS, CHUNK, HEAD_DIM),
                               jnp.broadcast_to(sw["gmlp_bs"][..., None], (nl, A_HEADS, CHUNK, HEAD_DIM))], axis=-1),
        lgf=jnp.concatenate([zeros(nl, A_HEADS, HEAD_DIM), sw["gmlp_ln_g"]], axis=-1),
        lbf=jnp.concatenate([zeros(nl, A_HEADS, HEAD_DIM), sw["gmlp_ln_b"]], axis=-1),
        gq=jnp.tile(sw["q_norm_g"], (1, 2)).reshape(nl, 1, LANES),
        gk=jnp.tile(sw["k_norm_g"], (1, 2)).reshape(nl, 1, LANES),
        lam_r=lr.reshape(nl, 1, N_STATE), lam_i=li.reshape(nl, 1, N_STATE),
        bcat=jnp.concatenate([_block_diag(unflat(bbr)), _block_diag(unflat(bbi))], axis=-1),
        ccat=jnp.concatenate([_block_diag(jnp.swapaxes(sw["ssm_c_re"], -1, -2)),
                              -_block_diag(jnp.swapaxes(sw["ssm_c_im"], -1, -2))], axis=1),
        dskip=sw["ssm_d"].reshape(nl, 1, C_WIDTH))
    cos, sin = _rope_tables(positions)

    def layer_params(l, hooks):
        lpi = {n: v[l] for n, v in lp.items()}
        lpi["W"] = lambda n: bw.layer(l)[n]
        lpi["p"] = (p, l)
        lpi.update(hooks)
        return lpi

    h, saved = x, []
    xn = _rms_fwd(x, sw["attn_norm_g"][0], "f_norm_attn")
    for l in range(nl):
        g_next = sw["attn_norm_g"][l + 1] if l + 1 < nl else None
        h, xn, sv = _layer_fwd(h, xn, layer_params(l, bw.fwd_hooks(l)), cos, sin, g_next)
        saved.append(sv)
    sse, dh = _loss_fwd_bwd(h, target)

    per_layer = [None] * nl
    for l in reversed(range(nl)):
        dh, gl = _layer_bwd(dh, layer_params(l, bw.bwd_hooks(l)), saved[l], cos, sin)
        bw.grads(l, {n: gl.pop(n) for n in BIG})
        per_layer[l] = gl
    grad_x = dh
    g = {n: jnp.stack([per_layer[l][n] for l in range(nl)]) for n in per_layer[0]}

    d_bcat = g["d_bcat"]
    dbr = _diag_blocks(d_bcat[:, :, :N_STATE], C_GROUP, C_STATE).reshape(G, C_GROUP, C_STATE)
    dbi = _diag_blocks(d_bcat[:, :, N_STATE:], C_GROUP, C_STATE).reshape(G, C_GROUP, C_STATE)
    dlr = g["dar"][:, 0].reshape(G, 1, C_STATE)
    dli = g["dai"][:, 0].reshape(G, 1, C_STATE)
    g_are, g_aim, g_ldt, g_bre, g_bim = _ssm_prep_bwd(are, aim, ldt, bre, bim, dlr, dli, dbr, dbi)
    d_ccat = g["d_ccat"]
    sg = dict(
        attn_norm_g=g["attn_norm_g"], mlp_norm_g=g["mlp_norm_g"], ple_norm_g=g["ple_norm_g"], mix_out_g=g["mix_out_g"],
        gmlp_ln_g=g["dlg"][:, :, 0, HEAD_DIM:], gmlp_ln_b=g["dlb"][:, :, 0, HEAD_DIM:],
        gmlp_ws=g["dws"], gmlp_bs=g["dbs"][:, :, :, HEAD_DIM],
        q_norm_g=g["dgq"][:, 0, :HEAD_DIM], k_norm_g=g["dgk"][:, 0, :HEAD_DIM],
        sinks=g["dsink"][:, 0, :B_Q_HEADS],
        ssm_a_re=g_are.reshape(nl, C_GROUPS, C_STATE), ssm_a_im=g_aim.reshape(nl, C_GROUPS, C_STATE),
        ssm_log_dt=g_ldt[:, 0, 0].reshape(nl, C_GROUPS),
        ssm_b_re=jnp.swapaxes(g_bre.reshape(nl, C_GROUPS, C_GROUP, C_STATE), -1, -2),
        ssm_b_im=jnp.swapaxes(g_bim.reshape(nl, C_GROUPS, C_GROUP, C_STATE), -1, -2),
        ssm_c_re=jnp.swapaxes(_diag_blocks(d_ccat[:, :N_STATE], C_STATE, C_GROUP), -1, -2),
        ssm_c_im=-jnp.swapaxes(_diag_blocks(d_ccat[:, N_STATE:], C_STATE, C_GROUP), -1, -2),
        ssm_d=g["dd"].reshape(nl, C_GROUPS, C_GROUP),
    )
    return (sse[0, 0], grad_x, sg) + tuple(bw.finish(_pack(sg)))


_ANY = pl.BlockSpec(memory_space=pl.ANY)
N_LAYERS = 4


def _mesh_pos():
    x, y, c = lax.axis_index("x"), lax.axis_index("y"), lax.axis_index("c")
    chips = [(1 - x, y), (x, 1 - y), (1 - x, 1 - y)]
    return x, y, c, 2 * x + y, chips


def _cast_into_slot(ws, j, name):
    nl, R, _ = ws[0].shape
    widths = [w.shape[2] for w in ws]
    C = sum(widths)
    tr = R if R <= 512 else _tile_rows(R, 512)
    nw = len(ws)

    def body(s_ref, *refs):
        o_ref = refs[nw]
        off = 0
        for r, wd in zip(refs[:nw], widths):
            o_ref[:, off:off + wd] = r[...].astype(o_ref.dtype)
            off += wd

    return pl.pallas_call(
        body, name=name,
        grid_spec=pltpu.PrefetchScalarGridSpec(
            num_scalar_prefetch=1, grid=(nl, R // tr),
            in_specs=[pl.BlockSpec((None, tr, wd), lambda l, i, s: (l, i, 0)) for wd in widths],
            out_specs=pl.BlockSpec((None, None, tr, C), lambda l, i, s: (l, s[0], i, 0))),
        out_shape=_sds((nl, N_CHIPS, R, C), _MXU),
    )(jnp.reshape(j, (1,)).astype(jnp.int32), *ws)


def _gather_weights(bufs):
    nk = len(bufs)

    def body(*refs):
        ins, outs = refs[:nk], refs[nk:2 * nk]
        send_sems, recv_sems = refs[2 * nk:]
        x, y, c, j, chips = _mesh_pos()
        mine, other = pl.ds(2 * c, 2), pl.ds(2 * (1 - c), 2)

        def ici(t, q):
            cx, cy = chips[q]
            return pltpu.make_async_remote_copy(
                src_ref=ins[t].at[mine, j], dst_ref=outs[t].at[mine, j],
                send_sem=send_sems.at[6 * t + q], recv_sem=recv_sems.at[6 * t + q],
                device_id=(cx, cy, c), device_id_type=MESH)

        def landed(t, q):
            cx, cy = chips[q]
            blk = outs[t].at[mine, 2 * cx + cy]
            return pltpu.make_async_remote_copy(
                src_ref=blk, dst_ref=blk, send_sem=send_sems.at[6 * t + q], recv_sem=recv_sems.at[6 * t + q],
                device_id=(cx, cy, c), device_id_type=MESH)

        def fwd(t, q, rows):
            cx, cy = chips[q]
            blk = outs[t].at[rows, 2 * cx + cy]
            return pltpu.make_async_remote_copy(
                src_ref=blk, dst_ref=blk, send_sem=send_sems.at[6 * t + 3 + q], recv_sem=recv_sems.at[6 * t + 3 + q],
                device_id=(x, y, 1 - c), device_id_type=MESH)

        for t in range(nk):
            for q in range(3):
                ici(t, q).start()
        for t in range(nk):
            for q in range(3):
                landed(t, q).wait_recv()
                fwd(t, q, mine).start()
        for t in range(nk):
            for q in range(3):
                fwd(t, q, other).wait_recv()
        for t in range(nk):
            for q in range(3):
                ici(t, q).wait_send()
                fwd(t, q, mine).wait_send()

    return pl.pallas_call(
        body, name="gather_weights", in_specs=[_ANY] * nk, out_specs=[_ANY] * nk,
        out_shape=[_sds(b.shape, b.dtype) for b in bufs],
        input_output_aliases={t: t for t in range(nk)},
        scratch_shapes=[pltpu.SemaphoreType.DMA((6 * nk,)), pltpu.SemaphoreType.DMA((6 * nk,))],
    )(*bufs)


def _exchange_sibling_half(gl):
    nk = len(gl)

    def body(*refs):
        ins, outs = refs[:nk], refs[nk:2 * nk]
        send_sems, recv_sems = refs[2 * nk:]
        x, y, c, _, _ = _mesh_pos()
        cps = [pltpu.make_async_remote_copy(
            src_ref=ins[t].at[pl.ds(2 * (1 - c), 2)], dst_ref=outs[t],
            send_sem=send_sems.at[t], recv_sem=recv_sems.at[t],
            device_id=(x, y, 1 - c), device_id_type=MESH) for t in range(nk)]
        for cp in cps:
            cp.start()
        for cp in cps:
            cp.wait()

    return pl.pallas_call(
        body, name="reduce_sibling", in_specs=[_ANY] * nk, out_specs=[_ANY] * nk,
        out_shape=[_sds((2,) + g.shape[1:], g.dtype) for g in gl],
        scratch_shapes=[pltpu.SemaphoreType.DMA((nk,)), pltpu.SemaphoreType.DMA((nk,))],
    )(*gl)


def _exchange_chips(ps):
    nk = len(ps)

    def body(*refs):
        ins, outs = refs[:nk], refs[nk:2 * nk]
        send_sems, recv_sems = refs[2 * nk:]
        x, y, c, j, chips = _mesh_pos()

        def send(t, q):
            cx, cy = chips[q]
            return pltpu.make_async_remote_copy(
                src_ref=ins[t].at[:, 2 * cx + cy], dst_ref=outs[t].at[j],
                send_sem=send_sems.at[3 * t + q], recv_sem=recv_sems.at[3 * t + q],
                device_id=(cx, cy, c), device_id_type=MESH)

        def landed(t, q):
            cx, cy = chips[q]
            blk = outs[t].at[2 * cx + cy]
            return pltpu.make_async_remote_copy(
                src_ref=blk, dst_ref=blk, send_sem=send_sems.at[3 * t + q], recv_sem=recv_sems.at[3 * t + q],
                device_id=(cx, cy, c), device_id_type=MESH)

        for t in range(nk):
            for q in range(3):
                send(t, q).start()
        for t in range(nk):
            for q in range(3):
                landed(t, q).wait_recv()
        for t in range(nk):
            for q in range(3):
                send(t, q).wait_send()

    return pl.pallas_call(
        body, name="reduce_chips", in_specs=[_ANY] * nk, out_specs=[_ANY] * nk,
        out_shape=[_sds((N_CHIPS, 2) + p.shape[2:], p.dtype) for p in ps],
        scratch_shapes=[pltpu.SemaphoreType.DMA((3 * nk,)), pltpu.SemaphoreType.DMA((3 * nk,))],
    )(*ps)


def _share_sibling(fs):
    nk = len(fs)

    def body(*refs):
        ins, outs = refs[:nk], refs[nk:2 * nk]
        send_sems, recv_sems = refs[2 * nk:]
        x, y, c, _, _ = _mesh_pos()
        mine = pl.ds(2 * c, 2)
        cps = [pltpu.make_async_remote_copy(
            src_ref=ins[t].at[mine], dst_ref=outs[t].at[mine], send_sem=send_sems.at[t], recv_sem=recv_sems.at[t],
            device_id=(x, y, 1 - c), device_id_type=MESH) for t in range(nk)]
        for cp in cps:
            cp.start()
        for cp in cps:
            cp.wait_send()
        for t in range(nk):
            blk = outs[t].at[pl.ds(2 * (1 - c), 2)]
            pltpu.make_async_remote_copy(
                src_ref=blk, dst_ref=blk, send_sem=send_sems.at[t], recv_sem=recv_sems.at[t],
                device_id=(x, y, 1 - c), device_id_type=MESH).wait_recv()

    return pl.pallas_call(
        body, name="share_sibling", in_specs=[_ANY] * nk, out_specs=[_ANY] * nk,
        out_shape=[_sds(f.shape, f.dtype) for f in fs],
        input_output_aliases={t: t for t in range(nk)},
        scratch_shapes=[pltpu.SemaphoreType.DMA((nk,)), pltpu.SemaphoreType.DMA((nk,))],
    )(*fs)


def _add_own_half(gl, r1, c, name):
    _, ns, R, C = gl.shape
    rows = 2 * ns * R
    tr = _tile_rows(rows, 512)
    nblk = rows // tr

    def body(s_ref, a_ref, b_ref, o_ref):
        o_ref[...] = (a_ref[...] + b_ref[...]).astype(o_ref.dtype)

    out = pl.pallas_call(
        body, name=name,
        grid_spec=pltpu.PrefetchScalarGridSpec(
            num_scalar_prefetch=1, grid=(nblk,),
            in_specs=[pl.BlockSpec((tr, C), lambda i, s: (s[0] * nblk + i, 0)), pl.BlockSpec((tr, C), lambda i, s: (i, 0))],
            out_specs=pl.BlockSpec((tr, C), lambda i, s: (i, 0))),
        out_shape=_sds((rows, C), _WIRE),
        compiler_params=_cparams(3 * _nbytes((tr, C), F32)),
    )(jnp.reshape(c, (1,)).astype(jnp.int32), gl.reshape(2 * rows, C), r1.reshape(rows, C))
    return out.reshape(2, ns, R, C)


def _add_chips(p, r2, j, c, name):
    _, ns, R, C = p.shape
    tr = R if R <= 512 else _tile_rows(R, 512)

    def body(s_ref, own, a1, a2, a3, o_ref):
        f = lambda r: r[...].astype(F32)
        o_ref[...] = ((f(own) + f(a1)) + f(a2)) + f(a3)

    blk = (None, None, tr, C)
    return pl.pallas_call(
        body, name=name,
        grid_spec=pltpu.PrefetchScalarGridSpec(
            num_scalar_prefetch=1, grid=(2, R // tr),
            in_specs=[pl.BlockSpec(blk, lambda h, i, s: (h, s[0], i, 0))]
            + [pl.BlockSpec(blk, lambda h, i, s, k=k: ((s[0] + k) % N_CHIPS, h, i, 0)) for k in (1, 2, 3)],
            out_specs=pl.BlockSpec((None, tr, C), lambda h, i, s: (2 * s[1] + h, i, 0))),
        out_shape=_sds((N_LAYERS, R, C), F32),
        compiler_params=_cparams(6 * _nbytes((tr, C), F32)),
    )(jnp.stack([j, c]).astype(jnp.int32), p, r2, r2, r2)


def _allreduce_small(buf, plan=None):
    Rs = buf.shape[0]
    nx = 0 if plan is None else len(plan.ins)
    x_out_shape = [] if plan is None else [_sds(sh, dt) for sh, dt in plan.fresh]
    assert plan is None or not any(plan.aliased)
    nxo = len(x_out_shape)

    def body(*refs):
        b_ref, x_ins = refs[0], refs[1:1 + nx]
        o_ref, x_outs = refs[1 + nx], refs[2 + nx:2 + nx + nxo]
        t_ref, slots_ref, send_sems, recv_sems = refs[2 + nx + nxo:6 + nx + nxo]
        x_sems = refs[6 + nx + nxo:]
        if plan is not None:
            plan.start(x_ins, x_outs, *x_sems)
        x, y, c, j, chips = _mesh_pos()
        sib = pltpu.make_async_remote_copy(
            src_ref=b_ref, dst_ref=t_ref, send_sem=send_sems.at[0], recv_sem=recv_sems.at[0],
            device_id=(x, y, 1 - c), device_id_type=MESH)
        sib.start()
        sib.wait()
        slots_ref[j] = b_ref[...] + t_ref[...]

        def send(q):
            cx, cy = chips[q]
            return pltpu.make_async_remote_copy(
                src_ref=slots_ref.at[j], dst_ref=slots_ref.at[j], send_sem=send_sems.at[1 + q],
                recv_sem=recv_sems.at[1 + q], device_id=(cx, cy, c), device_id_type=MESH)

        def landed(q):
            cx, cy = chips[q]
            blk = slots_ref.at[2 * cx + cy]
            return pltpu.make_async_remote_copy(
                src_ref=blk, dst_ref=blk, send_sem=send_sems.at[1 + q], recv_sem=recv_sems.at[1 + q],
                device_id=(cx, cy, c), device_id_type=MESH)

        for q in range(3):
            send(q).start()
        for q in range(3):
            landed(q).wait_recv()
        for q in range(3):
            send(q).wait_send()
        o_ref[...] = ((slots_ref[0] + slots_ref[1]) + slots_ref[2]) + slots_ref[3]
        if plan is not None:
            plan.wait(x_ins, x_outs, *x_sems)

    vm = pl.BlockSpec(memory_space=pltpu.VMEM)
    outs = pl.pallas_call(
        body, name="allreduce_small", in_specs=[vm] + [_ANY] * nx, out_specs=[vm] + [_ANY] * nxo,
        out_shape=[_sds((Rs, LANES), F32)] + x_out_shape,
        scratch_shapes=[pltpu.VMEM((Rs, LANES), F32), pltpu.VMEM((N_CHIPS, Rs, LANES), F32),
                        pltpu.SemaphoreType.DMA((4,)), pltpu.SemaphoreType.DMA((4,))]
        + ([pltpu.SemaphoreType.DMA((plan.n_sems,))] * 2 if plan is not None else []),
        compiler_params=_cparams(4 * _nbytes((Rs, LANES), F32)),
    )(buf, *([] if plan is None else plan.ins))
    if plan is not None:
        plan.done(list(outs[1:]))
    return outs[0]


def _own_rows(c, R):
    return pl.ds(c * (R // 2), R // 2)


def _cast_layer_slot(ws, l, j, name):
    _, R, _ = ws[0].shape
    widths = [w.shape[2] for w in ws]
    C = sum(widths)
    tr = R if R <= 512 else _tile_rows(R, 512)
    nw = len(ws)

    def body(s_ref, *refs):
        o_ref = refs[nw]
        off = 0
        for r, wd in zip(refs[:nw], widths):
            o_ref[:, off:off + wd] = r[...].astype(o_ref.dtype)
            off += wd

    return pl.pallas_call(
        body, name=name,
        grid_spec=pltpu.PrefetchScalarGridSpec(
            num_scalar_prefetch=1, grid=(R // tr,),
            in_specs=[pl.BlockSpec((None, tr, wd), lambda i, s: (l, i, 0)) for wd in widths],
            out_specs=pl.BlockSpec((None, tr, C), lambda i, s: (s[0], i, 0))),
        out_shape=_sds((N_CHIPS, R, C), _MXU),
    )(jnp.reshape(j, (1,)).astype(jnp.int32), *ws)


def _gather_ici(bufs, done):
    nk = len(bufs)

    def copy(ins, outs, ss, rs, t, q, landed):
        x, y, c, j, chips = _mesh_pos()
        cx, cy = chips[q]
        rows = _own_rows(c, ins[t].shape[1])
        src = outs[t].at[2 * cx + cy, rows] if landed else ins[t].at[j, rows]
        dst = outs[t].at[2 * cx + cy, rows] if landed else outs[t].at[j, rows]
        return pltpu.make_async_remote_copy(src_ref=src, dst_ref=dst, send_sem=ss.at[3 * t + q], recv_sem=rs.at[3 * t + q],
                                            device_id=(cx, cy, c), device_id_type=MESH)

    def start(ins, outs, ss, rs):
        for t in range(nk):
            for q in range(3):
                copy(ins, outs, ss, rs, t, q, False).start()

    def wait(ins, outs, ss, rs):
        for t in range(nk):
            for q in range(3):
                copy(ins, outs, ss, rs, t, q, True).wait_recv()
                copy(ins, outs, ss, rs, t, q, False).wait_send()

    return _Exchange(bufs, [True] * nk, [], 3 * nk, start, wait, done)


def _gather_d2d(bufs, done):
    nk = len(bufs)

    def copy(ins, outs, ss, rs, t, q, mine):
        x, y, c, j, chips = _mesh_pos()
        cx, cy = chips[q]
        rows = _own_rows(c if mine else 1 - c, ins[t].shape[1])
        src = (ins if mine else outs)[t].at[2 * cx + cy, rows]
        return pltpu.make_async_remote_copy(src_ref=src, dst_ref=outs[t].at[2 * cx + cy, rows],
                                            send_sem=ss.at[3 * t + q], recv_sem=rs.at[3 * t + q],
                                            device_id=(x, y, 1 - c), device_id_type=MESH)

    def start(ins, outs, ss, rs):
        for t in range(nk):
            for q in range(3):
                copy(ins, outs, ss, rs, t, q, True).start()

    def wait(ins, outs, ss, rs):
        for t in range(nk):
            for q in range(3):
                copy(ins, outs, ss, rs, t, q, False).wait_recv()
                copy(ins, outs, ss, rs, t, q, True).wait_send()

    return _Exchange(bufs, [True] * nk, [], 3 * nk, start, wait, done)


def _reduce_d2d(gl, done):
    nk = len(gl)

    def copy(ins, outs, ss, rs, t):
        x, y, c, _, _ = _mesh_pos()
        return pltpu.make_async_remote_copy(
            src_ref=ins[t].at[:, _own_rows(1 - c, ins[t].shape[1])], dst_ref=outs[t],
            send_sem=ss.at[t], recv_sem=rs.at[t], device_id=(x, y, 1 - c), device_id_type=MESH)

    def start(ins, outs, ss, rs):
        for t in range(nk):
            copy(ins, outs, ss, rs, t).start()

    def wait(ins, outs, ss, rs):
        for t in range(nk):
            copy(ins, outs, ss, rs, t).wait()

    fresh = [((N_CHIPS, g.shape[1] // 2, g.shape[2]), g.dtype) for g in gl]
    return _Exchange(gl, [False] * nk, fresh, nk, start, wait, done)


def _reduce_ici(ps, done):
    nk = len(ps)

    def copy(ins, outs, ss, rs, t, q, landed):
        x, y, c, j, chips = _mesh_pos()
        cx, cy = chips[q]
        src = outs[t].at[2 * cx + cy] if landed else ins[t].at[2 * cx + cy]
        dst = outs[t].at[2 * cx + cy] if landed else outs[t].at[j]
        return pltpu.make_async_remote_copy(src_ref=src, dst_ref=dst, send_sem=ss.at[3 * t + q], recv_sem=rs.at[3 * t + q],
                                            device_id=(cx, cy, c), device_id_type=MESH)

    def start(ins, outs, ss, rs):
        for t in range(nk):
            for q in range(3):
                copy(ins, outs, ss, rs, t, q, False).start()

    def wait(ins, outs, ss, rs):
        for t in range(nk):
            for q in range(3):
                copy(ins, outs, ss, rs, t, q, True).wait_recv()
                copy(ins, outs, ss, rs, t, q, False).wait_send()

    return _Exchange(ps, [False] * nk, [(p_.shape, p_.dtype) for p_ in ps], 3 * nk, start, wait, done)


def _share_d2d(fs, done):
    nk = len(fs)

    def copy(ins, outs, ss, rs, t, mine):
        x, y, c, _, _ = _mesh_pos()
        rows = _own_rows(c if mine else 1 - c, ins[t].shape[1])
        src = (ins if mine else outs)[t].at[:, rows]
        return pltpu.make_async_remote_copy(src_ref=src, dst_ref=outs[t].at[:, rows], send_sem=ss.at[t], recv_sem=rs.at[t],
                                            device_id=(x, y, 1 - c), device_id_type=MESH)

    def start(ins, outs, ss, rs):
        for t in range(nk):
            copy(ins, outs, ss, rs, t, True).start()

    def wait(ins, outs, ss, rs):
        for t in range(nk):
            copy(ins, outs, ss, rs, t, False).wait_recv()
            copy(ins, outs, ss, rs, t, True).wait_send()

    return _Exchange(fs, [True] * nk, [], nk, start, wait, done)


def _run_exchange(plan, name):
    nin = len(plan.ins)
    out_shape = [_sds(x_.shape, x_.dtype) for x_, al in zip(plan.ins, plan.aliased) if al]
    aliases, k = {}, 0
    for t, al in enumerate(plan.aliased):
        if al:
            aliases[t] = k
            k += 1
    out_shape += [_sds(sh, dt) for sh, dt in plan.fresh]
    nout = len(out_shape)

    def body(*refs):
        ins, outs, sems = refs[:nin], refs[nin:nin + nout], refs[nin + nout:]
        plan.start(ins, outs, *sems)
        plan.wait(ins, outs, *sems)

    outs = pl.pallas_call(
        body, name=name, in_specs=[_ANY] * nin, out_specs=[_ANY] * nout, out_shape=out_shape,
        input_output_aliases=aliases,
        scratch_shapes=[pltpu.SemaphoreType.DMA((plan.n_sems,))] * 2,
    )(*plan.ins)
    plan.done(list(outs))


def _add_sibling_rows(g, r1, c, name):
    ns, R, C = g.shape
    hr = R // 2
    tr = hr if hr <= 512 else _tile_rows(hr, 512)
    nblk = hr // tr

    def body(s_ref, a_ref, b_ref, o_ref):
        o_ref[...] = (a_ref[...] + b_ref[...]).astype(o_ref.dtype)

    blk = (None, tr, C)
    return pl.pallas_call(
        body, name=name,
        grid_spec=pltpu.PrefetchScalarGridSpec(
            num_scalar_prefetch=1, grid=(ns, nblk),
            in_specs=[pl.BlockSpec(blk, lambda s_, i, s: (s_, s[0] * nblk + i, 0)), pl.BlockSpec(blk, lambda s_, i, s: (s_, i, 0))],
            out_specs=pl.BlockSpec(blk, lambda s_, i, s: (s_, i, 0))),
        out_shape=_sds((ns, hr, C), _WIRE),
        compiler_params=_cparams(3 * _nbytes((tr, C), F32)),
    )(jnp.reshape(c, (1,)).astype(jnp.int32), g, r1)


def _add_chip_rows(p_, r2, f, l, j, c, name):
    _, hr, C = p_.shape
    tr = hr if hr <= 512 else _tile_rows(hr, 512)
    nblk = hr // tr

    def body(s_ref, own, a1, a2, a3, f_ref, o_ref):
        v = lambda r: r[...].astype(F32)
        o_ref[...] = ((v(own) + v(a1)) + v(a2)) + v(a3)

    blk = (None, tr, C)
    return pl.pallas_call(
        body, name=name,
        grid_spec=pltpu.PrefetchScalarGridSpec(
            num_scalar_prefetch=1, grid=(nblk,),
            in_specs=[pl.BlockSpec(blk, lambda i, s: (s[0], i, 0))]
            + [pl.BlockSpec(blk, lambda i, s, k=k: ((s[0] + k) % N_CHIPS, i, 0)) for k in (1, 2, 3)]
            + [pl.BlockSpec(memory_space=pl.ANY)],
            out_specs=pl.BlockSpec(blk, lambda i, s: (l, s[1] * nblk + i, 0))),
        out_shape=_sds(f.shape, F32),
        input_output_aliases={5: 0},
        compiler_params=_cparams(6 * _nbytes((tr, C), F32)),
    )(jnp.stack([j, c]).astype(jnp.int32), p_, r2, r2, r2, f)


class _ShardedWeights:
    def __init__(self, a, j, c):
        self.j, self.c = j, c
        shards = dict(w_in=[a["w_in"]], w12=[a["glu_w1"], a["glu_w2"]], w_out=[a["w_out"]], w_ff1=[a["w_ff1"]],
                      w_ff2=[a["w_ff2"]], w_ple_gate=[a["w_ple_gate"]], w_ple_proj=[a["w_ple_proj"]])
        self.bufs = [[_cast_layer_slot(shards[n], l, j, "cast_%s_%d" % (n, l)) for n in BIG] for l in range(N_LAYERS)]
        _run_exchange(self._gather_part(0, self._HEAD), "gather_ici_0")
        _run_exchange(self._gather_part(0, self._HEAD, _gather_d2d), "gather_d2d_0")
        self.raw_ = {}
        self.pend_ = {}
        self.final = [lax.empty((N_LAYERS,) + b.shape[1:], F32) for b in self.bufs[0]]

    def _set_bufs(self, l, outs):
        self.bufs[l] = outs

    def layer(self, l):
        return {n: (b if n in COL_SHARDED else b.reshape(N_CHIPS * b.shape[1], b.shape[2]))
                for n, b in zip(BIG, self.bufs[l])}

    _FIRST, _SECOND = (0, 3), (1, 2, 4, 5, 6)

    _HEAD, _REST = (0, 1, 2), (3, 4, 5, 6)

    def _gather_part(self, nxt, idx, exchange=_gather_ici):
        def done(outs):
            for i, o in zip(idx, outs):
                self.bufs[nxt][i] = o
        return exchange([self.bufs[nxt][i] for i in idx], done)

    def fwd_hooks(self, l):
        hooks = {}
        if l == 0:
            hooks.update(x_ssm=lambda: self._gather_part(0, self._REST),
                         x_out=lambda: self._gather_part(0, self._REST, _gather_d2d))
        if l + 1 < N_LAYERS:
            nxt = l + 1
            hooks.update(x_ff1=lambda: self._gather_part(nxt, self._FIRST),
                         x_ff2=lambda: self._gather_part(nxt, self._SECOND),
                         x_gate=lambda: _gather_d2d(self.bufs[nxt], lambda o: self._set_bufs(nxt, o)))
        return hooks

    _EARLY, _LATE = (2, 3, 4, 5, 6), (0, 1)

    @staticmethod
    def _shards(idx, g):
        out = []
        for i in idx:
            n = BIG[i]
            out.append(g[n] if n in COL_SHARDED else g[n].reshape(N_CHIPS, g[n].shape[0] // N_CHIPS, g[n].shape[1]))
        return out

    def _sibling_plan(self, slot):
        lyr, idx, gl = self.raw_[slot]

        def done(got):
            ps = [_add_sibling_rows(g_, r1, self.c, "reduce_add_sibling_%s_%d" % (BIG[i], lyr))
                  for g_, r1, i in zip(gl, got, idx)]
            self.pend_[slot] = (lyr, idx, ps)
        return _reduce_d2d(gl, done)

    def _chips_plan(self, slot):
        lyr, idx, ps = self.pend_[slot]

        def done(r2):
            for i, p_, r in zip(idx, ps, r2):
                self.final[i] = _add_chip_rows(p_, r, self.final[i], lyr, self.j, self.c,
                                               "reduce_add_chips_%s_%d" % (BIG[i], lyr))
        return _reduce_ici(ps, done)

    def _early(self, l, g):
        self.raw_["early"] = (l, self._EARLY, self._shards(self._EARLY, g))

    def bwd_hooks(self, l):
        hooks = dict(early=lambda g: self._early(l, g),
                     x_e0=lambda: self._sibling_plan("early"), x_e1=lambda: self._chips_plan("early"))
        if "late" in self.raw_:
            hooks.update(x_bwd0=lambda: self._sibling_plan("late"), x_bwd=lambda: self._chips_plan("late"))
        return hooks

    def grads(self, l, g):
        self.raw_["late"] = (l, self._LATE, self._shards(self._LATE, g))

    def finish(self, small):
        _run_exchange(self._sibling_plan("late"), "reduce_d2d_last")
        small = _allreduce_small(small, self._chips_plan("late"))
        out = []
        _run_exchange(_share_d2d(self.final, out.extend), "share_d2d")
        return dict(zip(BIG, out)), small


def _rows_of(shape):
    return -(-int(np.prod(shape)) // (SUBLANES * LANES)) * SUBLANES


def _pack(d):
    parts = []
    for n in SMALL:
        flat = d[n].reshape(-1)
        parts.append(jnp.pad(flat, (0, _rows_of(flat.shape) * LANES - flat.shape[0])).reshape(-1, LANES))
    return jnp.concatenate(parts, axis=0)


def _unpack(buf, like):
    out, r0 = {}, 0
    for n in SMALL:
        shape = like[n].shape
        size, nr = int(np.prod(shape)), _rows_of(shape)
        piece = lax.optimization_barrier(buf[r0:r0 + nr])
        out[n] = piece.reshape(-1)[:size].reshape(shape)
        r0 += nr
    return out


ARGS = ("x", "p", "positions", "attn_norm_g", "w_in", "gmlp_ln_g", "gmlp_ln_b", "gmlp_ws", "gmlp_bs", "q_norm_g",
        "k_norm_g", "sinks", "ssm_a_re", "ssm_a_im", "ssm_log_dt", "ssm_b_re", "ssm_b_im", "ssm_c_re", "ssm_c_im",
        "ssm_d", "glu_w1", "glu_w2", "mix_out_g", "w_out", "mlp_norm_g", "w_ff1", "w_ff2", "ple_norm_g", "w_ple_gate",
        "w_ple_proj")
WEIGHTS = ARGS[3:]


def kernel(x, p, positions, attn_norm_g, w_in, gmlp_ln_g, gmlp_ln_b, gmlp_ws, gmlp_bs, q_norm_g, k_norm_g, sinks, ssm_a_re, ssm_a_im, ssm_log_dt, ssm_b_re, ssm_b_im, ssm_c_re, ssm_c_im, ssm_d, glu_w1, glu_w2, mix_out_g, w_out, mlp_norm_g, w_ff1, w_ff2, ple_norm_g, w_ple_gate, w_ple_proj, loss_target, m_attn_norm_g, m_w_in, m_gmlp_ln_g, m_gmlp_ln_b, m_gmlp_ws, m_gmlp_bs, m_q_norm_g, m_k_norm_g, m_sinks, m_ssm_a_re, m_ssm_a_im, m_ssm_log_dt, m_ssm_b_re, m_ssm_b_im, m_ssm_c_re, m_ssm_c_im, m_ssm_d, m_glu_w1, m_glu_w2, m_mix_out_g, m_w_out, m_mlp_norm_g, m_w_ff1, m_w_ff2, m_ple_norm_g, m_w_ple_gate, m_w_ple_proj, v_attn_norm_g, v_w_in, v_gmlp_ln_g, v_gmlp_ln_b, v_gmlp_ws, v_gmlp_bs, v_q_norm_g, v_k_norm_g, v_sinks, v_ssm_a_re, v_ssm_a_im, v_ssm_log_dt, v_ssm_b_re, v_ssm_b_im, v_ssm_c_re, v_ssm_c_im, v_ssm_d, v_glu_w1, v_glu_w2, v_mix_out_g, v_w_out, v_mlp_norm_g, v_w_ff1, v_w_ff2, v_ple_norm_g, v_w_ple_gate, v_w_ple_proj):
    a = dict(locals())
    L = a["x"].shape[1]
    nl = N_LAYERS
    c = lax.axis_index("c")
    j = 2 * lax.axis_index("x") + lax.axis_index("y")

    sw = {n: a[n] for n in SMALL}
    sse, gx, _, big_grads, small_sum = _local_step(
        a["x"].reshape(L, D_MODEL), a["p"].reshape(nl, L, PLE_DIM), a["positions"].reshape(L),
        a["loss_target"].reshape(L, D_MODEL), sw, _ShardedWeights(a, j, c))
    loss = lax.psum(sse * (0.5 / D_MODEL), ("x", "y", "c"))
    g12 = big_grads.pop("w12")
    big_grads["glu_w1"], big_grads["glu_w2"] = g12[:, :, :C_WIDTH], g12[:, :, C_WIDTH:]

    small_grads = _unpack(small_sum, sw)

    grads, delta, new_m, new_v = {}, {}, {}, {}
    d_s, m_s, v_s = _adamw(_pack(sw), _pack(small_grads), _pack({n: a["m_" + n] for n in SMALL}),
                           _pack({n: a["v_" + n] for n in SMALL}), "adamw_small")
    grads.update(small_grads)
    delta.update(_unpack(d_s, sw))
    new_m.update(_unpack(m_s, sw))
    new_v.update(_unpack(v_s, sw))
    for n, g in big_grads.items():
        shp = a[n].shape
        two_d = lambda t: t.reshape(shp[0] * shp[1], shp[2])
        direct = n not in ("glu_w1", "glu_w2")
        res = _adamw(two_d(a[n]), two_d(g), two_d(a["m_" + n]), two_d(a["v_" + n]), "adamw_" + n, emit_g=direct)
        d, m, v = res[:3]
        grads[n] = res[3].reshape(shp) if direct else g
        delta[n], new_m[n], new_v[n] = d.reshape(shp), m.reshape(shp), v.reshape(shp)

    return (loss, gx.reshape(1, L, D_MODEL), *[grads[n] for n in WEIGHTS], *[delta[n] for n in WEIGHTS],
            *[new_m[n] for n in WEIGHTS], *[new_v[n] for n in WEIGHTS])
```

```python
import functools
import math

import numpy as np
import jax
import jax.numpy as jnp
from jax import lax
from jax.experimental import pallas as pl
from jax.experimental.pallas import tpu as pltpu

F32 = jnp.float32
_MXU = jnp.bfloat16
_ACT = jnp.bfloat16
_WIRE = jnp.bfloat16

D_MODEL = 1024
HEAD_DIM = 64
A_HEADS = 4
CHUNK = 128
B_Q_HEADS = 8
B_KV_HEADS = 2
B_GROUP = 4
WINDOW = 128
ROPE_THETA = 10000.0
C_WIDTH = 256
C_GROUP = 16
C_GROUPS = 16
C_STATE = 64
N_STATE = C_GROUPS * C_STATE
IN_A, IN_Q, IN_KV, IN_C = 512, 512, 128, 256
IN_COLS = 1536
D_FF = 4096
PLE_DIM = 256
EPS = 1e-6
NEG = -1e30
ADAM_LR, ADAM_B1, ADAM_B2, ADAM_EPS, ADAM_WD, ADAM_STEP = 0.001, 0.9, 0.999, 1e-08, 0.01, 10

LANES = 128
SUBLANES = 8
VMEM_BYTES = 64 * 2 ** 20
N_CHIPS = 4
MESH = pl.DeviceIdType.MESH


_MM_VMEM_BUDGET = 55 * 2 ** 20
_EPI_ROWS = 256


def _vmem_limit(est_bytes):
    return int(min(max(2 * est_bytes + (8 << 20), 32 << 20), VMEM_BYTES - (6 << 20)))


def _cparams(est_bytes, **kw):
    return pltpu.CompilerParams(vmem_limit_bytes=_vmem_limit(est_bytes), **kw)


def _sds(shape, dtype):
    return pltpu.HBM(tuple(shape), dtype)


def _hbm(x):
    return pltpu.with_memory_space_constraint(x, pltpu.HBM) if x.size >= (1 << 20) else x


def _nbytes(shape, dtype):
    return int(np.prod(shape)) * jnp.dtype(dtype).itemsize


def _tile(dim, pref):
    t = min(dim, pref)
    while dim % t:
        t -= LANES
    assert t > 0, (dim, pref)
    return t


def _lane(shape):
    return lax.broadcasted_iota(jnp.int32, shape, len(shape) - 1)


def _row(shape):
    return lax.broadcasted_iota(jnp.int32, shape, len(shape) - 2)


def _gelu(x):
    c = math.sqrt(2.0 / math.pi)
    return 0.5 * x * (1.0 + jnp.tanh(c * (x + 0.044715 * (x * x * x))))


def _gelu_grad(x):
    c = math.sqrt(2.0 / math.pi)
    t = jnp.tanh(c * (x + 0.044715 * (x * x * x)))
    return 0.5 * (1.0 + t) + 0.5 * x * (1.0 - t * t) * (c * (1.0 + 3.0 * 0.044715 * (x * x)))


def _gelu_and_grad(x):
    c = math.sqrt(2.0 / math.pi)
    x2 = x * x
    t = jnp.tanh(c * (x + 0.044715 * (x2 * x)))
    half = 0.5 * (1.0 + t)
    return x * half, half + 0.5 * x * (1.0 - t * t) * (c * (1.0 + 3.0 * 0.044715 * x2))


def _sigmoid(x):
    return 1.0 / (1.0 + jnp.exp(-x))


def _dot(a, b, dims=(((1,), (0,)), ((), ()))):
    return lax.dot_general(a.astype(_MXU), b.astype(_MXU), dims, preferred_element_type=F32)


_NT = (((1,), (1,)), ((), ()))
_TN = (((0,), (0,)), ((), ()))
_NN = (((1,), (0,)), ((), ()))


def _mm(a, b, *, mode, M, N, K, out_dtypes, name, epi=None, extras=(), b_cb=False, o_cb=False,
        a_off=0, b_off=0, tm=1024, tn=1024, tk=2048, a_lyr=None, b_lyr=None, o_stack=None, n_acc=0, comm=None):
    if isinstance(a, tuple):
        a, a_lyr = a
    if isinstance(b, tuple):
        b, b_lyr = b
    if b_cb or o_cb:
        nc = (b.shape[-1] if b_cb else N // N_CHIPS)
    tn_nom = nc if ((mode == "nn" and b_cb) or (mode == "tn" and o_cb)) else _tile(N, tn)
    tk_nom = nc if (mode == "nt" and b_cb) else _tile(K, tk)
    item = lambda d: jnp.dtype(d).itemsize
    per_row = tk_nom * item(a.dtype) + tn_nom * (sum(item(d) for d in out_dtypes)
                                                   + sum(item(e.dtype) for e, _ in extras if e.shape[0] > 1))
    fixed = tk_nom * tn_nom * item(b.dtype)
    tm = _tile(M, tm)
    while tm > 256 and M % (tm // 2) == 0 and 2 * (tm * per_row + fixed) + 8 * tm * tn_nom > _MM_VMEM_BUDGET:
        tm //= 2

    def spec(block, imap, lyr=None):
        if lyr is None:
            return pl.BlockSpec(block, imap)
        return pl.BlockSpec((None,) + block, lambda i, j, k: (lyr,) + imap(i, j, k))

    if mode == "nn":
        if b_cb:
            tn = nc
        tm, tn, tk = _tile(M, tm), _tile(N, tn), _tile(K, tk)
        a_spec = spec((tm, tk), lambda i, j, k: (i, k + a_off), a_lyr)
        if b_cb:
            b_spec = spec((None, tk, tn), lambda i, j, k: (j, k, 0), b_lyr)
        else:
            b_spec = spec((tk, tn), lambda i, j, k: (k, j + b_off), b_lyr)
        dims = _NN
        a_blk, b_blk = (tm, tk), (tk, tn)
    elif mode == "nt":
        if b_cb:
            tk = nc
        tm, tn, tk = _tile(M, tm), _tile(N, tn), _tile(K, tk)
        a_spec = spec((tm, tk), lambda i, j, k: (i, k + a_off), a_lyr)
        if b_cb:
            b_spec = spec((None, tn, tk), lambda i, j, k: (k, j, 0), b_lyr)
        else:
            b_spec = spec((tn, tk), lambda i, j, k: (j, k + b_off), b_lyr)
        dims = _NT
        a_blk, b_blk = (tm, tk), (tn, tk)
    else:
        if o_cb:
            tn = nc
        tm, tn, tk = _tile(M, tm), _tile(N, tn), _tile(K, tk)
        a_spec = spec((tk, tm), lambda i, j, k: (k, i + a_off), a_lyr)
        b_spec = spec((tk, tn), lambda i, j, k: (k, j + b_off), b_lyr)
        dims = _TN
        a_blk, b_blk = (tk, tm), (tk, tn)
    gi, gj, gk = M // tm, N // tn, K // tk
    o_lyr = None if o_stack is None else o_stack[1]
    if o_cb:
        o_spec = spec((None, tm, tn), lambda i, j, k: (j, i, 0), o_lyr)
        o_shape = (gj, M, tn)
    else:
        o_spec = spec((tm, tn), lambda i, j, k: (i, j), o_lyr)
        o_shape = (M, N)
    e_specs = []
    for e, off in extras:
        if e.shape[0] == 1:
            e_specs.append(pl.BlockSpec((1, tn), lambda i, j, k, off=off: (0, j + off)))
        else:
            e_specs.append(pl.BlockSpec((tm, tn), lambda i, j, k, off=off: (i, j + off)))
    extras = [e for e, _ in extras]
    ne, no = len(extras), len(out_dtypes)
    operands = [_hbm(t) for t in (a, b, *extras)]
    in_specs = [a_spec, b_spec] + e_specs
    out_shape = [_sds(o_shape, d) for d in out_dtypes]
    aliases = {}
    if o_stack is not None:
        assert no == 1 and o_stack[0].shape[1:] == o_shape and o_stack[0].dtype == out_dtypes[0]
        operands.append(_hbm(o_stack[0]))
        in_specs.append(pl.BlockSpec(memory_space=pl.ANY))
        out_shape = [_sds(o_stack[0].shape, o_stack[0].dtype)]
        aliases = {len(operands) - 1: 0}
    out_specs = [o_spec] * no
    if n_acc:
        assert gj == 1 and o_stack is None
        out_specs[no - n_acc:] = [pl.BlockSpec((1, tn), lambda i, j, k: (0, 0))] * n_acc
        out_shape[no - n_acc:] = [_sds((1, N), d) for d in out_dtypes[no - n_acc:]]
    nx_in = nx_out = 0
    if comm is not None:
        nx_in, ncin0 = len(comm.ins), len(operands)
        operands += list(comm.ins)
        in_specs += [pl.BlockSpec(memory_space=pl.ANY)] * nx_in
        for t, x_ in enumerate(comm.ins):
            if comm.aliased[t]:
                aliases[ncin0 + t] = len(out_shape)
                out_shape.append(_sds(x_.shape, x_.dtype))
        out_shape += [_sds(sh, dt) for sh, dt in comm.fresh]
        nx_out = len(out_shape) - no
        out_specs += [pl.BlockSpec(memory_space=pl.ANY)] * nx_out
    nin = len(operands)

    def body(*refs):
        a_ref, b_ref = refs[0], refs[1]
        e_refs = refs[2:2 + ne]
        o_refs = refs[nin:nin + no]
        first_rows = pl.program_id(0) == 0
        if comm is not None:
            x_ins = refs[nin - nx_in:nin]
            x_outs = refs[nin + no:nin + no + nx_out]
            sems = refs[nin + no + nx_out:nin + no + nx_out + 2]
            pid = [pl.program_id(d) for d in range(3)]

            @pl.when((pid[0] == 0) & (pid[1] == 0) & (pid[2] == 0))
            def _():
                comm.start(x_ins, x_outs, *sems)

        def fin(acc):
            for t in range(no - n_acc, no):
                @pl.when(first_rows)
                def _():
                    o_refs[t][...] = jnp.zeros_like(o_refs[t])

            rc = _EPI_ROWS if (epi is not None and tm % _EPI_ROWS == 0) else tm
            for c0 in range(0, tm, rc):
                rows = slice(c0, c0 + rc)
                ex = [e[...] if e.shape[0] == 1 else e[rows, :] for e in e_refs]
                vals = epi(acc[rows, :], *ex) if epi is not None else (acc[rows, :],)
                for t, (o, v) in enumerate(zip(o_refs, vals)):
                    if t < no - n_acc:
                        o[rows, :] = v.astype(o.dtype)
                    else:
                        o[...] += v.astype(o.dtype)

        prod = _dot(a_ref[...], b_ref[...], dims)
        if gk == 1:
            fin(prod)
        else:
            acc_ref = refs[-1]
            k = pl.program_id(2)

            @pl.when(k == 0)
            def _():
                acc_ref[...] = prod

            @pl.when(k > 0)
            def _():
                acc_ref[...] += prod

            @pl.when(k == gk - 1)
            def _():
                fin(acc_ref)

        if comm is not None:
            @pl.when((pid[0] == gi - 1) & (pid[1] == gj - 1) & (pid[2] == gk - 1))
            def _():
                comm.wait(x_ins, x_outs, *sems)

    est = (_nbytes(a_blk, a.dtype) + _nbytes(b_blk, b.dtype)
           + sum(_nbytes((tm, tn), d) for d in out_dtypes)
           + sum(_nbytes((tm, tn), e.dtype) for e in extras)) + 2 * _nbytes((tm, tn), F32)
    sem_scratch = [pltpu.SemaphoreType.DMA((comm.n_sems,))] * 2 if comm is not None else []
    row_sem = "arbitrary" if (n_acc or comm is not None) else "parallel"
    outs = pl.pallas_call(
        body, name=name, grid=(gi, gj, gk),
        in_specs=in_specs,
        out_specs=out_specs,
        out_shape=out_shape,
        scratch_shapes=sem_scratch + ([pltpu.VMEM((tm, tn), F32)] if gk > 1 else []),
        input_output_aliases=aliases,
        compiler_params=_cparams(est, dimension_semantics=(row_sem, "arbitrary" if comm is not None else "parallel",
                                                           "arbitrary")),
    )(*operands)
    if comm is not None:
        main = outs[:no]
        return (main if no > 1 else main[0]), list(outs[no:])
    return outs if no > 1 else outs[0]


_TL = 512


def _rms_fwd(h, g, name):
    L, D = h.shape
    tl = _tile(L, _TL)

    def body(h_ref, g_ref, o_ref):
        x = h_ref[...]
        r = lax.rsqrt(jnp.mean(x * x, axis=-1, keepdims=True) + EPS)
        o_ref[...] = ((x * r) * g_ref[...]).astype(o_ref.dtype)

    return pl.pallas_call(
        body, name=name, grid=(L // tl,),
        in_specs=[pl.BlockSpec((tl, D), lambda i: (i, 0)), pl.BlockSpec((1, D), lambda i: (0, 0))],
        out_specs=pl.BlockSpec((tl, D), lambda i: (i, 0)),
        out_shape=_sds((L, D), _ACT),
        compiler_params=_cparams(3 * _nbytes((tl, D), F32)),
    )(h, g.reshape(1, D))


def _rms_bwd(dxn, h, g, dres, name):
    L, D = h.shape
    tl = _tile(L, _TL)

    def body(d_ref, h_ref, g_ref, r_ref, o_ref, dg_ref):
        x = h_ref[...]
        r = lax.rsqrt(jnp.mean(x * x, axis=-1, keepdims=True) + EPS)
        xhat = x * r
        d = d_ref[...].astype(F32)
        gy = d * g_ref[...]
        dx = r * (gy - xhat * jnp.mean(gy * xhat, axis=-1, keepdims=True))
        o_ref[...] = r_ref[...] + dx

        @pl.when(pl.program_id(0) == 0)
        def _():
            dg_ref[...] = jnp.zeros_like(dg_ref)

        dg_ref[...] += jnp.sum(d * xhat, axis=0, keepdims=True)

    dh, dg = pl.pallas_call(
        body, name=name, grid=(L // tl,),
        in_specs=[pl.BlockSpec((tl, D), lambda i: (i, 0)), pl.BlockSpec((tl, D), lambda i: (i, 0)),
                  pl.BlockSpec((1, D), lambda i: (0, 0)), pl.BlockSpec((tl, D), lambda i: (i, 0))],
        out_specs=[pl.BlockSpec((tl, D), lambda i: (i, 0)), pl.BlockSpec((1, D), lambda i: (0, 0))],
        out_shape=[_sds((L, D), F32), _sds((1, D), F32)],
        compiler_params=_cparams(5 * _nbytes((tl, D), F32)),
    )(dxn, h, g.reshape(1, D), dres)
    return dh, dg.reshape(D)


def _rope_tables(positions):
    L = positions.shape[0]
    tl = _tile(L, 1024)
    inv = 1.0 / (ROPE_THETA ** (np.arange(0, HEAD_DIM, 2, dtype=np.float32) / HEAD_DIM))
    inv128 = jnp.asarray(np.tile(inv.astype(np.float32), 4).reshape(1, LANES))

    def body(p_ref, i_ref, c_ref, s_ref):
        ang = p_ref[...].astype(F32) * i_ref[...]
        c_ref[...] = jnp.cos(ang)
        s_ref[...] = jnp.sin(ang)

    return pl.pallas_call(
        body, name="rope_tables", grid=(L // tl,),
        in_specs=[pl.BlockSpec((tl, 1), lambda i: (i, 0)), pl.BlockSpec((1, LANES), lambda i: (0, 0))],
        out_specs=[pl.BlockSpec((tl, LANES), lambda i: (i, 0))] * 2,
        out_shape=[_sds((L, LANES), F32)] * 2,
    )(positions.reshape(L, 1), inv128)


_GM_TL = 256


def _gmlp_head(Z, W, bfull, lg, lb, maskv, G=None):
    G = _gelu(Z) if G is None else G
    mu = jnp.sum(jnp.where(maskv, G, 0.0), axis=-1, keepdims=True) * (1.0 / HEAD_DIM)
    xc = jnp.where(maskv, G - mu, 0.0)
    var = jnp.sum(xc * xc, axis=-1, keepdims=True) * (1.0 / HEAD_DIM)
    rstd = lax.rsqrt(var + EPS)
    xhat = xc * rstd
    vn = xhat * lg + lb
    sv = _dot(W, vn) + bfull
    return G, xhat, rstd, vn, sv


def _tril(W):
    return jnp.where(_row(W.shape) >= _lane(W.shape), W, 0.0)


def _triu(W):
    return jnp.where(_row(W.shape) <= _lane(W.shape), W, 0.0)


def _gmlp_fwd(z, ws, bfull, lgf, lbf, name):
    L = z.shape[0]
    tl = _tile(L, _GM_TL)
    nch = tl // CHUNK

    def body(z_ref, w_ref, b_ref, lg_ref, lb_ref, o_ref):
        maskv = _lane((CHUNK, LANES)) >= HEAD_DIM
        for c in range(nch):
            rows = slice(c * CHUNK, (c + 1) * CHUNK)
            for hp in range(A_HEADS // 2):
                acc = None
                for hh in range(2):
                    h = 2 * hp + hh
                    Z = z_ref[rows, h * LANES:(h + 1) * LANES]
                    G, _, _, _, sv = _gmlp_head(Z, _tril(w_ref[h]), b_ref[h], lg_ref[h:h + 1, :], lb_ref[h:h + 1, :], maskv)
                    prod = G * pltpu.roll(sv, HEAD_DIM, axis=1)
                    acc = prod if hh == 0 else acc + pltpu.roll(prod, HEAD_DIM, axis=1)
                o_ref[rows, hp * LANES:(hp + 1) * LANES] = acc

    return pl.pallas_call(
        body, name=name, grid=(L // tl,),
        in_specs=[pl.BlockSpec((tl, IN_A), lambda i: (i, 0)),
                  pl.BlockSpec((A_HEADS, CHUNK, CHUNK), lambda i: (0, 0, 0)),
                  pl.BlockSpec((A_HEADS, CHUNK, LANES), lambda i: (0, 0, 0)),
                  pl.BlockSpec((A_HEADS, LANES), lambda i: (0, 0)),
                  pl.BlockSpec((A_HEADS, LANES), lambda i: (0, 0))],
        out_specs=pl.BlockSpec((tl, 2 * LANES), lambda i: (i, 0)),
        out_shape=_sds((L, 2 * LANES), F32),
    )(z, ws, bfull, lgf, lbf)


def _gmlp_bwd(z, dya, ws, wsT, bfull, lgf, lbf, name):
    L = z.shape[0]
    tl = _tile(L, _GM_TL)
    nch = tl // CHUNK
    nsteps = L // tl

    def body(z_ref, d_ref, w_ref, wt_ref, b_ref, lg_ref, lb_ref, dz_ref, dw_ref, db_ref, dlg_ref, dlb_ref):
        step = pl.program_id(0)

        @pl.when(step == 0)
        def _():
            dw_ref[...] = jnp.zeros_like(dw_ref)
            db_ref[...] = jnp.zeros_like(db_ref)
            dlg_ref[...] = jnp.zeros_like(dlg_ref)
            dlb_ref[...] = jnp.zeros_like(dlb_ref)

        lane = _lane((CHUNK, LANES))
        maskv = lane >= HEAD_DIM
        for c in range(nch):
            rows = slice(c * CHUNK, (c + 1) * CHUNK)
            for h in range(A_HEADS):
                hp, hh = divmod(h, 2)
                Z = z_ref[rows, h * LANES:(h + 1) * LANES]
                lg = lg_ref[h:h + 1, :]
                G, dG = _gelu_and_grad(Z)
                G, xhat, rstd, vn, sv = _gmlp_head(Z, _tril(w_ref[h]), b_ref[h], lg, lb_ref[h:h + 1, :], maskv, G=G)
                dpair = d_ref[rows, hp * LANES:(hp + 1) * LANES]
                if hh == 1:
                    dpair = pltpu.roll(dpair, HEAD_DIM, axis=1)
                dout = jnp.where(maskv, 0.0, dpair)
                du = dout * pltpu.roll(sv, HEAD_DIM, axis=1)
                dsv = pltpu.roll(dout * G, HEAD_DIM, axis=1)
                dw_ref[h] += _tril(_dot(dsv, vn, _NT))
                db_ref[h] += dsv
                dvn = _dot(_triu(wt_ref[h]), dsv)
                dlg_ref[h] += dvn * xhat
                dlb_ref[h] += dvn
                dxh = dvn * lg
                m1 = jnp.sum(dxh, axis=-1, keepdims=True) * (1.0 / HEAD_DIM)
                m2 = jnp.sum(dxh * xhat, axis=-1, keepdims=True) * (1.0 / HEAD_DIM)
                dv = jnp.where(maskv, rstd * (dxh - m1 - xhat * m2), 0.0)
                dz_ref[rows, h * LANES:(h + 1) * LANES] = ((du + dv) * dG).astype(dz_ref.dtype)

        @pl.when(step == nsteps - 1)
        def _():
            for h in range(A_HEADS):
                db_ref[h] = jnp.broadcast_to(jnp.sum(db_ref[h], axis=1, keepdims=True), (CHUNK, LANES))
                dlg_ref[h] = jnp.broadcast_to(jnp.sum(dlg_ref[h], axis=0, keepdims=True), (CHUNK, LANES))
                dlb_ref[h] = jnp.broadcast_to(jnp.sum(dlb_ref[h], axis=0, keepdims=True), (CHUNK, LANES))

    full3 = pl.BlockSpec((A_HEADS, CHUNK, LANES), lambda i: (0, 0, 0))
    return pl.pallas_call(
        body, name=name, grid=(nsteps,),
        in_specs=[pl.BlockSpec((tl, IN_A), lambda i: (i, 0)),
                  pl.BlockSpec((tl, 2 * LANES), lambda i: (i, 0)),
                  full3, full3, full3,
                  pl.BlockSpec((A_HEADS, LANES), lambda i: (0, 0)),
                  pl.BlockSpec((A_HEADS, LANES), lambda i: (0, 0))],
        out_specs=[pl.BlockSpec((tl, IN_A), lambda i: (i, 0)), full3, full3, full3, full3],
        out_shape=[_sds((L, IN_A), _ACT)] + [_sds((A_HEADS, CHUNK, LANES), F32)] * 4,
    )(z, dya, ws, wsT, bfull, lgf, lbf)


def _head_rstd(x, lo):
    sq = x * x
    s_lo = jnp.sum(jnp.where(lo, sq, 0.0), axis=-1, keepdims=True)
    s_hi = jnp.sum(jnp.where(lo, 0.0, sq), axis=-1, keepdims=True)
    return jnp.where(lo, lax.rsqrt(s_lo * (1.0 / HEAD_DIM) + EPS), lax.rsqrt(s_hi * (1.0 / HEAD_DIM) + EPS))


def _rot_half(x, first):
    return jnp.where(first, -pltpu.roll(x, LANES - HEAD_DIM // 2, axis=1), pltpu.roll(x, HEAD_DIM // 2, axis=1))


def _qk_prep(z, cos, sin, gq, gk, name):
    L = z.shape[0]
    tl = _tile(L, _TL)
    nq = IN_Q // LANES

    def body(q_ref, k_ref, c_ref, s_ref, gq_ref, gk_ref, qo_ref, ko_ref):
        lane = _lane((tl, LANES))
        lo = lane < HEAD_DIM
        first = (lane % HEAD_DIM) < (HEAD_DIM // 2)
        c, s = c_ref[...], s_ref[...]

        def prep(x, g):
            xn = (x * _head_rstd(x, lo)) * g
            return xn * c + _rot_half(xn, first) * s

        for j in range(nq):
            qo_ref[:, j * LANES:(j + 1) * LANES] = prep(q_ref[:, j * LANES:(j + 1) * LANES], gq_ref[...]).astype(qo_ref.dtype)
        ko_ref[...] = prep(k_ref[...], gk_ref[...]).astype(ko_ref.dtype)

    return pl.pallas_call(
        body, name=name, grid=(L // tl,),
        in_specs=[pl.BlockSpec((tl, IN_Q), lambda i: (i, 1)),
                  pl.BlockSpec((tl, IN_KV), lambda i: (i, 8)),
                  pl.BlockSpec((tl, LANES), lambda i: (i, 0)), pl.BlockSpec((tl, LANES), lambda i: (i, 0)),
                  pl.BlockSpec((1, LANES), lambda i: (0, 0)), pl.BlockSpec((1, LANES), lambda i: (0, 0))],
        out_specs=[pl.BlockSpec((tl, IN_Q), lambda i: (i, 0)), pl.BlockSpec((tl, IN_KV), lambda i: (i, 0))],
        out_shape=[_sds((L, IN_Q), _ACT), _sds((L, IN_KV), _ACT)],
    )(z, z, cos, sin, gq, gk)


def _qk_prep_bwd(z, dq, dkc, dkp, dvc, dvp, cos, sin, gq, gk, name):
    L = z.shape[0]
    tl = _ATT_QB * WINDOW
    nb = L // tl
    nq = IN_Q // LANES

    def body(q_ref, k_ref, dq_ref, dkc_ref, dkp_ref, dvc_ref, dvp_ref, c_ref, s_ref, gq_ref, gk_ref,
             dzq_ref, dzk_ref, dzv_ref, dgq_ref, dgk_ref):
        n = pl.program_id(0)

        @pl.when(n == 0)
        def _():
            dgq_ref[...] = jnp.zeros_like(dgq_ref)
            dgk_ref[...] = jnp.zeros_like(dgk_ref)

        lane = _lane((tl, LANES))
        lo = lane < HEAD_DIM
        first = (lane % HEAD_DIM) < (HEAD_DIM // 2)
        c, s = c_ref[...], s_ref[...]
        has_next = jnp.where(n < nb - 1, 1.0, 0.0)

        def bwd(x, g, dy):
            r = _head_rstd(x, lo)
            xhat = x * r
            dxn = dy * c - _rot_half(dy * s, first)
            gy = dxn * g
            t = gy * xhat
            m_lo = jnp.sum(jnp.where(lo, t, 0.0), axis=-1, keepdims=True)
            m_hi = jnp.sum(jnp.where(lo, 0.0, t), axis=-1, keepdims=True)
            m = jnp.where(lo, m_lo, m_hi) * (1.0 / HEAD_DIM)
            dx = r * (gy - xhat * m)
            dg = jnp.sum(dxn * xhat, axis=0, keepdims=True)
            return dx, dg

        dgq = jnp.zeros((1, LANES), F32)
        for j in range(nq):
            sl = slice(j * LANES, (j + 1) * LANES)
            dx, dg = bwd(q_ref[:, sl], gq_ref[...], dq_ref[:, sl].astype(F32))
            dzq_ref[:, sl] = dx.astype(dzq_ref.dtype)
            dgq = dgq + dg
        dgq_ref[...] += dgq + pltpu.roll(dgq, HEAD_DIM, axis=1)
        def with_next(cur_ref, nxt_ref):
            head = jnp.zeros((tl - WINDOW, IN_KV), F32)
            return cur_ref[...] + jnp.concatenate([head, has_next * nxt_ref[...]], axis=0)

        dx, dg = bwd(k_ref[...], gk_ref[...], with_next(dkc_ref, dkp_ref))
        dzk_ref[...] = dx.astype(dzk_ref.dtype)
        dgk_ref[...] += dg + pltpu.roll(dg, HEAD_DIM, axis=1)
        dzv_ref[...] = with_next(dvc_ref, dvp_ref).astype(dzv_ref.dtype)

    nxt = lambda i: (jnp.minimum(i + 1, nb - 1), 0)
    cur = lambda i: (i, 0)
    kv = pl.BlockSpec((tl, IN_KV), cur)
    kvn = pl.BlockSpec((WINDOW, IN_KV), nxt)
    one = pl.BlockSpec((1, LANES), lambda i: (0, 0))
    return pl.pallas_call(
        body, name=name, grid=(nb,),
        in_specs=[pl.BlockSpec((tl, IN_Q), lambda i: (i, 1)), pl.BlockSpec((tl, IN_KV), lambda i: (i, 8)),
                  pl.BlockSpec((tl, IN_Q), cur), kv, kvn, kv, kvn,
                  kv, kv, one, one],
        out_specs=[pl.BlockSpec((tl, IN_Q), cur), kv, kv, one, one],
        out_shape=[_sds((L, IN_Q), _ACT), _sds((L, IN_KV), _ACT),
                   _sds((L, IN_KV), _ACT), _sds((1, LANES), F32),
                   _sds((1, LANES), F32)],
    )(z, z, dq, dkc, dkp, dvc, dvp, cos, sin, gq, gk)


def _attn_mask(n):
    shp = (2 * WINDOW, B_GROUP * WINDOW)
    qi = _lane(shp) % WINDOW
    kj = _row(shp)
    off = 0 if n is None else jnp.where(n > 0, 0, 4 * WINDOW)
    return ((kj >= WINDOW) & (kj - WINDOW <= qi)) | ((kj < WINDOW) & (kj > qi + off))


def _kv_lanes(j):
    lane = _lane((WINDOW, LANES))
    return (lane >= j * HEAD_DIM) & (lane < (j + 1) * HEAD_DIM)


_ATT_QB = 4


def _stack_heads(ref, rows, j, kvl):
    parts = []
    for g in range(B_GROUP):
        h = j * B_GROUP + g
        slab = ref[rows, (h // 2) * LANES:(h // 2 + 1) * LANES].astype(F32)
        if (h % 2) != j:
            slab = pltpu.roll(slab, HEAD_DIM, axis=1)
        parts.append(jnp.where(kvl, slab, 0.0))
    return jnp.concatenate(parts, axis=0)


def _attn_probs(qs, k2, sink_row, mask):
    s = _dot(k2, qs, _NT) * (HEAD_DIM ** -0.5)
    s = jnp.where(mask, s, NEG)
    m = jnp.maximum(jnp.max(s, axis=0, keepdims=True), sink_row)
    p = jnp.exp(s - m)
    esink = jnp.exp(sink_row - m)
    inv = 1.0 / (jnp.sum(p, axis=0, keepdims=True) + esink)
    return p * inv, esink * inv


def _sink_row(sink_ref, j):
    lane = _lane((1, B_GROUP * WINDOW))
    row = jnp.full((1, B_GROUP * WINDOW), sink_ref[j * B_GROUP], F32)
    for g in range(1, B_GROUP):
        row = jnp.where(lane >= g * WINDOW, sink_ref[j * B_GROUP + g], row)
    return row


def _attn_fwd(q, k, z, sinks, name):
    L = q.shape[0]
    QB = _ATT_QB
    tq = QB * WINDOW
    prev = lambda n: (jnp.maximum(QB * n - 1, 0), 0)
    prev_v = lambda n: (jnp.maximum(QB * n - 1, 0), 9)

    def body(s_ref, q_ref, kp_ref, kc_ref, vp_ref, vc_ref, o_ref):
        n = pl.program_id(0)
        k3 = jnp.concatenate([kp_ref[...], kc_ref[...]], axis=0)
        v3 = jnp.concatenate([vp_ref[...], vc_ref[...]], axis=0)
        for b in range(QB):
            rows = slice(b * WINDOW, (b + 1) * WINDOW)
            mask = _attn_mask(n if b == 0 else None)
            k2 = k3[b * WINDOW:(b + 2) * WINDOW]
            v2 = v3[b * WINDOW:(b + 2) * WINDOW]
            slabs = [None] * (IN_Q // LANES)
            for j in range(B_KV_HEADS):
                kvl = _kv_lanes(j)
                qs = _stack_heads(q_ref, rows, j, kvl)
                pn, _ = _attn_probs(qs, k2, _sink_row(s_ref, j), mask)
                o = _dot(pn, v2, _TN)
                for g in range(B_GROUP):
                    h = j * B_GROUP + g
                    piece = jnp.where(kvl, o[g * WINDOW:(g + 1) * WINDOW], 0.0)
                    if (h % 2) != j:
                        piece = pltpu.roll(piece, HEAD_DIM, axis=1)
                    slabs[h // 2] = piece if slabs[h // 2] is None else slabs[h // 2] + piece
            for t, sl in enumerate(slabs):
                o_ref[rows, t * LANES:(t + 1) * LANES] = sl

    return pl.pallas_call(
        body, name=name, grid=(L // tq,),
        in_specs=[pl.BlockSpec(memory_space=pltpu.SMEM),
                  pl.BlockSpec((tq, IN_Q), lambda n: (n, 0)),
                  pl.BlockSpec((WINDOW, IN_KV), prev), pl.BlockSpec((tq, IN_KV), lambda n: (n, 0)),
                  pl.BlockSpec((WINDOW, IN_KV), prev_v), pl.BlockSpec((tq, IN_KV), lambda n: (n, 9))],
        out_specs=pl.BlockSpec((tq, IN_Q), lambda n: (n, 0)),
        out_shape=_sds((L, IN_Q), F32),
    )(sinks, q, k, k, z, z)


def _attn_bwd(q, k, z, sinks, dyb, name):
    L = q.shape[0]
    QB = _ATT_QB
    tq = QB * WINDOW
    nsteps = L // tq
    prev = lambda n: (jnp.maximum(QB * n - 1, 0), 0)
    prev_v = lambda n: (jnp.maximum(QB * n - 1, 0), 9)
    cur = lambda n: (n, 0)

    def body(s_ref, q_ref, kp_ref, kc_ref, vp_ref, vc_ref, d_ref, dq_ref, dkc_ref, dkp_ref, dvc_ref, dvp_ref, ds_ref):
        n = pl.program_id(0)

        @pl.when(n == 0)
        def _():
            ds_ref[...] = jnp.zeros_like(ds_ref)

        k3 = jnp.concatenate([kp_ref[...], kc_ref[...]], axis=0)
        v3 = jnp.concatenate([vp_ref[...], vc_ref[...]], axis=0)
        dkb = [None] * (QB + 1)
        dvb = [None] * (QB + 1)
        dsink = jnp.zeros((1, LANES), F32)
        lane1 = _lane((1, LANES))
        add = lambda acc, v: v if acc is None else acc + v
        for b in range(QB):
            rows = slice(b * WINDOW, (b + 1) * WINDOW)
            mask = _attn_mask(n if b == 0 else None)
            k2 = k3[b * WINDOW:(b + 2) * WINDOW]
            v2 = v3[b * WINDOW:(b + 2) * WINDOW]
            slabs = [None] * (IN_Q // LANES)
            for j in range(B_KV_HEADS):
                kvl = _kv_lanes(j)
                qs = _stack_heads(q_ref, rows, j, kvl)
                dos = _stack_heads(d_ref, rows, j, kvl)
                pn, psink = _attn_probs(qs, k2, _sink_row(s_ref, j), mask)
                dp = _dot(v2, dos, _NT)
                dd = jnp.sum(pn * dp, axis=0, keepdims=True)
                dss = (pn * (dp - dd)) * (HEAD_DIM ** -0.5)
                dqs = _dot(dss, k2, _TN)
                dk2 = _dot(dss, qs)
                dv2 = _dot(pn, dos)
                dkb[b], dkb[b + 1] = add(dkb[b], dk2[:WINDOW]), add(dkb[b + 1], dk2[WINDOW:])
                dvb[b], dvb[b + 1] = add(dvb[b], dv2[:WINDOW]), add(dvb[b + 1], dv2[WINDOW:])
                sd = psink * dd
                for g in range(B_GROUP):
                    h = j * B_GROUP + g
                    piece = jnp.where(kvl, dqs[g * WINDOW:(g + 1) * WINDOW], 0.0)
                    if (h % 2) != j:
                        piece = pltpu.roll(piece, HEAD_DIM, axis=1)
                    slabs[h // 2] = piece if slabs[h // 2] is None else slabs[h // 2] + piece
                    tot = jnp.sum(sd[:, g * WINDOW:(g + 1) * WINDOW], axis=1, keepdims=True)
                    dsink = dsink - jnp.where(lane1 == h, tot, 0.0)
            for t, sl in enumerate(slabs):
                dq_ref[rows, t * LANES:(t + 1) * LANES] = sl
        dkp_ref[...] = dkb[0]
        dvp_ref[...] = dvb[0]
        for b in range(QB):
            dkc_ref[b * WINDOW:(b + 1) * WINDOW, :] = dkb[b + 1]
            dvc_ref[b * WINDOW:(b + 1) * WINDOW, :] = dvb[b + 1]
        ds_ref[0:1, :] += dsink

    kvs = pl.BlockSpec((tq, IN_KV), cur)
    kvp = pl.BlockSpec((WINDOW, IN_KV), cur)
    kvo = _sds((L, IN_KV), F32)
    kvpo = _sds((nsteps * WINDOW, IN_KV), F32)
    return pl.pallas_call(
        body, name=name, grid=(nsteps,),
        in_specs=[pl.BlockSpec(memory_space=pltpu.SMEM),
                  pl.BlockSpec((tq, IN_Q), cur),
                  pl.BlockSpec((WINDOW, IN_KV), prev), kvs,
                  pl.BlockSpec((WINDOW, IN_KV), prev_v), pl.BlockSpec((tq, IN_KV), lambda n: (n, 9)),
                  pl.BlockSpec((tq, IN_Q), cur)],
        out_specs=[pl.BlockSpec((tq, IN_Q), cur), kvs, kvp, kvs, kvp, pl.BlockSpec((SUBLANES, LANES), lambda n: (0, 0))],
        out_shape=[_sds((L, IN_Q), F32), kvo, kvpo, kvo, kvpo, _sds((SUBLANES, LANES), F32)],
    )(sinks, q, k, k, z, z, dyb)


def _ssm_disc(are, aim, ldt, bre, bim):
    dt = jnp.exp(ldt)
    mag = jnp.exp(are * dt)
    lr, li = mag * jnp.cos(aim * dt), mag * jnp.sin(aim * dt)
    den = are * are + aim * aim
    xr, xi = lr - 1.0, li
    cr, ci = (xr * are + xi * aim) / den, (xi * are - xr * aim) / den
    return lr, li, cr * bre - ci * bim, cr * bim + ci * bre


def _ssm_prep(are, aim, ldt, bre, bim):
    shp3, shpb = are.shape, bre.shape

    def body(are_ref, aim_ref, ldt_ref, bre_ref, bim_ref, lr_ref, li_ref, br_ref, bi_ref):
        lr, li, br, bi = _ssm_disc(are_ref[...], aim_ref[...], ldt_ref[...], bre_ref[...], bim_ref[...])
        lr_ref[...] = lr
        li_ref[...] = li
        br_ref[...] = br
        bi_ref[...] = bi

    return pl.pallas_call(
        body, name="ssm_prep",
        out_shape=[_sds(shp3, F32)] * 2 + [_sds(shpb, F32)] * 2,
    )(are, aim, ldt, bre, bim)


def _ssm_prep_bwd(are, aim, ldt, bre, bim, dlr, dli, dbr, dbi):
    shp3, shpb = are.shape, bre.shape

    def body(are_ref, aim_ref, ldt_ref, bre_ref, bim_ref, dlr_ref, dli_ref, dbr_ref, dbi_ref,
             o_are, o_aim, o_ldt, o_bre, o_bim):
        _, vjp = jax.vjp(_ssm_disc, are_ref[...], aim_ref[...], ldt_ref[...], bre_ref[...], bim_ref[...])
        g = vjp((dlr_ref[...], dli_ref[...], dbr_ref[...], dbi_ref[...]))
        o_are[...] = g[0]
        o_aim[...] = g[1]
        o_ldt[...] = jnp.broadcast_to(jnp.sum(g[2], axis=-1, keepdims=True), shp3)
        o_bre[...] = g[3]
        o_bim[...] = g[4]

    return pl.pallas_call(
        body, name="ssm_prep_bwd",
        out_shape=[_sds(shp3, F32)] * 3 + [_sds(shpb, F32)] * 2,
    )(are, aim, ldt, bre, bim, dlr, dli, dbr, dbi)


_SCAN_TB = 512
_SCAN_W = 512


def _cmul(ar, ai, br, bi):
    return ar * br - ai * bi, ar * bi + ai * br


def _ssm_scan(x, lam_r, lam_i, name, reverse=False, states=None):
    L = x.shape[0]
    tb = _tile(L, _SCAN_TB)
    nrb = L // tb
    nt = tb // SUBLANES
    W = _SCAN_W
    with_da = states is not None

    def body(*refs):
        if with_da:
            xr_ref, xi_ref, sr_ref, si_ref, ar_ref, ai_ref, o_ref, dar_ref, dai_ref, cr_ref, ci_ref = refs
        else:
            xr_ref, xi_ref, ar_ref, ai_ref, o_ref, cr_ref, ci_ref = refs
        step = pl.program_id(0)

        @pl.when(step == 0)
        def _():
            cr_ref[...] = jnp.zeros_like(cr_ref)
            ci_ref[...] = jnp.zeros_like(ci_ref)
            if with_da:
                dar_ref[...] = jnp.zeros_like(dar_ref)
                dai_ref[...] = jnp.zeros_like(dai_ref)

        row = _row((SUBLANES, W))

        def shift(v, d, fill):
            if reverse:
                return jnp.where(row < SUBLANES - d, pltpu.roll(v, SUBLANES - d, axis=0), fill)
            return jnp.where(row >= d, pltpu.roll(v, d, axis=0), fill)

        edge = 0 if reverse else SUBLANES - 1
        for wb in range(N_STATE // W):
            cols = slice(wb * W, (wb + 1) * W)
            a1r = jnp.broadcast_to(ar_ref[:, cols], (SUBLANES, W))
            a1i = jnp.broadcast_to(ai_ref[:, cols], (SUBLANES, W))
            if reverse:
                a1i = -a1i
            a2r, a2i = _cmul(a1r, a1i, a1r, a1i)
            a4r, a4i = _cmul(a2r, a2i, a2r, a2i)
            pws = ((1, a1r, a1i), (2, a2r, a2i), (4, a4r, a4i))
            pr, pi = a1r, a1i
            for d, _, _ in pws:
                qr, qi = _cmul(pr, pi, shift(pr, d, 1.0), shift(pi, d, 0.0))
                pr, pi = qr, qi
            mws = []
            for d, er, ei in pws:
                ok = (row < SUBLANES - d) if reverse else (row >= d)
                mws.append(((SUBLANES - d) if reverse else d, jnp.where(ok, er, 0.0), jnp.where(ok, ei, 0.0)))

            def tile(i, carry):
                cr, ci, dr, di = carry
                t = (nt - 1 - i) if reverse else i
                r0 = pl.multiple_of(t * SUBLANES, SUBLANES)
                vr = xr_ref[pl.ds(r0, SUBLANES), cols]
                vi = xi_ref[pl.ds(r0, SUBLANES), cols]
                for sh, er, ei in mws:
                    tr, ti = _cmul(er, ei, pltpu.roll(vr, sh, axis=0), pltpu.roll(vi, sh, axis=0))
                    vr, vi = vr + tr, vi + ti
                tr, ti = _cmul(pr, pi, cr, ci)
                vr, vi = vr + tr, vi + ti
                o_ref[pl.ds(r0, SUBLANES), cols] = vr
                o_ref[pl.ds(r0, SUBLANES), slice(N_STATE + wb * W, N_STATE + (wb + 1) * W)] = vi
                if with_da:
                    gr = jnp.where(row < SUBLANES - 1, pltpu.roll(vr, SUBLANES - 1, axis=0), cr)
                    gi = jnp.where(row < SUBLANES - 1, pltpu.roll(vi, SUBLANES - 1, axis=0), ci)
                    sr = sr_ref[pl.ds(r0, SUBLANES), cols]
                    si = si_ref[pl.ds(r0, SUBLANES), cols]
                    dr = dr + sr * gr + si * gi
                    di = di + sr * gi - si * gr
                ncr = jnp.broadcast_to(vr[edge:edge + 1, :], (SUBLANES, W))
                nci = jnp.broadcast_to(vi[edge:edge + 1, :], (SUBLANES, W))
                return ncr, nci, dr, di

            zero = jnp.zeros((SUBLANES, W), F32)
            cr, ci, dr, di = lax.fori_loop(0, nt, tile, (cr_ref[:, cols], ci_ref[:, cols], zero, zero), unroll=2)
            cr_ref[:, cols] = cr
            ci_ref[:, cols] = ci
            if with_da:
                dar_ref[:, cols] += dr
                dai_ref[:, cols] += di

        if with_da:
            @pl.when(step == nrb - 1)
            def _():
                dar_ref[...] = jnp.broadcast_to(jnp.sum(dar_ref[...], axis=0, keepdims=True), dar_ref.shape)
                dai_ref[...] = jnp.broadcast_to(jnp.sum(dai_ref[...], axis=0, keepdims=True), dai_ref.shape)

    rb = (lambda i: (nrb - 1 - i, 0)) if reverse else (lambda i: (i, 0))
    rb_im = (lambda i: (nrb - 1 - i, 1)) if reverse else (lambda i: (i, 1))
    blk_r = pl.BlockSpec((tb, N_STATE), rb)
    blk_i = pl.BlockSpec((tb, N_STATE), rb_im)
    one = pl.BlockSpec((1, N_STATE), lambda i: (0, 0))
    acc = pl.BlockSpec((SUBLANES, N_STATE), lambda i: (0, 0))
    ins = [x, x] + ([states, states] if with_da else []) + [lam_r, lam_i]
    in_specs = [blk_r, blk_i] + ([blk_r, blk_i] if with_da else []) + [one, one]
    out_specs = [pl.BlockSpec((tb, 2 * N_STATE), rb)] + ([acc, acc] if with_da else [])
    out_shape = [_sds((L, 2 * N_STATE), F32)] + (
        [_sds((SUBLANES, N_STATE), F32)] * 2 if with_da else [])
    outs = pl.pallas_call(
        body, name=name, grid=(nrb,), in_specs=in_specs, out_specs=out_specs, out_shape=out_shape,
        scratch_shapes=[pltpu.VMEM((SUBLANES, N_STATE), F32)] * 2,
        compiler_params=_cparams((6 if with_da else 4) * _nbytes((tb, N_STATE), F32),
                                 dimension_semantics=("arbitrary",)),
    )(*ins)
    return outs if with_da else outs[0]


def _scan_block(x_ref, o_ref, s_ref, ar_ref, ai_ref, cr_ref, ci_ref, dar_ref, dai_ref, nt, reverse):
    W = _SCAN_W
    with_da = s_ref is not None
    row = _row((SUBLANES, W))

    def shift(v, d, fill):
        if reverse:
            return jnp.where(row < SUBLANES - d, pltpu.roll(v, SUBLANES - d, axis=0), fill)
        return jnp.where(row >= d, pltpu.roll(v, d, axis=0), fill)

    edge = 0 if reverse else SUBLANES - 1
    for wb in range(N_STATE // W):
        cols = slice(wb * W, (wb + 1) * W)
        icols = slice(N_STATE + wb * W, N_STATE + (wb + 1) * W)
        a1r = jnp.broadcast_to(ar_ref[:, cols], (SUBLANES, W))
        a1i = jnp.broadcast_to(ai_ref[:, cols], (SUBLANES, W))
        if reverse:
            a1i = -a1i
        a2r, a2i = _cmul(a1r, a1i, a1r, a1i)
        a4r, a4i = _cmul(a2r, a2i, a2r, a2i)
        pws = ((1, a1r, a1i), (2, a2r, a2i), (4, a4r, a4i))
        pr, pi = a1r, a1i
        for d, _, _ in pws:
            qr, qi = _cmul(pr, pi, shift(pr, d, 1.0), shift(pi, d, 0.0))
            pr, pi = qr, qi
        mws = []
        for d, er, ei in pws:
            ok = (row < SUBLANES - d) if reverse else (row >= d)
            mws.append(((SUBLANES - d) if reverse else d, jnp.where(ok, er, 0.0), jnp.where(ok, ei, 0.0)))

        def tile(i, carry):
            cr, ci, dr, di = carry
            t = (nt - 1 - i) if reverse else i
            r0 = pl.multiple_of(t * SUBLANES, SUBLANES)
            vr = x_ref[pl.ds(r0, SUBLANES), cols]
            vi = x_ref[pl.ds(r0, SUBLANES), icols]
            for sh, er, ei in mws:
                tr, ti = _cmul(er, ei, pltpu.roll(vr, sh, axis=0), pltpu.roll(vi, sh, axis=0))
                vr, vi = vr + tr, vi + ti
            tr, ti = _cmul(pr, pi, cr, ci)
            vr, vi = vr + tr, vi + ti
            o_ref[pl.ds(r0, SUBLANES), cols] = vr
            o_ref[pl.ds(r0, SUBLANES), icols] = vi
            if with_da:
                gr = jnp.where(row < SUBLANES - 1, pltpu.roll(vr, SUBLANES - 1, axis=0), cr)
                gi = jnp.where(row < SUBLANES - 1, pltpu.roll(vi, SUBLANES - 1, axis=0), ci)
                sr = s_ref[pl.ds(r0, SUBLANES), cols]
                si = s_ref[pl.ds(r0, SUBLANES), icols]
                dr = dr + sr * gr + si * gi
                di = di + sr * gi - si * gr
            ncr = jnp.broadcast_to(vr[edge:edge + 1, :], (SUBLANES, W))
            nci = jnp.broadcast_to(vi[edge:edge + 1, :], (SUBLANES, W))
            return ncr, nci, dr, di

        zero = jnp.zeros((SUBLANES, W), F32)
        cr, ci, dr, di = lax.fori_loop(0, nt, tile, (cr_ref[:, cols], ci_ref[:, cols], zero, zero), unroll=2)
        cr_ref[:, cols] = cr
        ci_ref[:, cols] = ci
        if with_da:
            dar_ref[:, cols] += dr
            dai_ref[:, cols] += di


def _carry(comm, operands, in_specs, out_shape, out_specs, aliases, scratch):
    if comm is None:
        return 0, 0
    n0, no0 = len(operands), len(out_shape)
    operands += list(comm.ins)
    in_specs += [pl.BlockSpec(memory_space=pl.ANY)] * len(comm.ins)
    for t, x_ in enumerate(comm.ins):
        if comm.aliased[t]:
            aliases[n0 + t] = len(out_shape)
            out_shape.append(_sds(x_.shape, x_.dtype))
    out_shape += [_sds(sh, dt) for sh, dt in comm.fresh]
    out_specs += [pl.BlockSpec(memory_space=pl.ANY)] * (len(out_shape) - no0)
    scratch += [pltpu.SemaphoreType.DMA((comm.n_sems,))] * 2
    return len(comm.ins), len(out_shape) - no0


def _ssm_fwd(z, bcat, ccat, dskip, lam_r, lam_i, name, comm=None):
    L = z.shape[0]
    tb = _tile(L, _SCAN_TB)
    nrb = L // tb
    nt = tb // SUBLANES
    full = lambda shp: pl.BlockSpec(shp, lambda i: (0, 0))
    rows = lambda w: pl.BlockSpec((tb, w), lambda i: (i, 0))
    operands = [_hbm(z), bcat, ccat, dskip, lam_r, lam_i]
    in_specs = [pl.BlockSpec((tb, C_WIDTH), lambda i: (i, 5)), full((C_WIDTH, 2 * N_STATE)), full((2 * N_STATE, C_WIDTH)),
                full((1, C_WIDTH)), full((1, N_STATE)), full((1, N_STATE))]
    out_specs = [rows(2 * N_STATE), rows(C_WIDTH), rows(C_WIDTH)]
    out_shape = [_sds((L, 2 * N_STATE), F32), _sds((L, C_WIDTH), F32), _sds((L, C_WIDTH), _ACT)]
    scratch = [pltpu.VMEM((tb, 2 * N_STATE), F32)] + [pltpu.VMEM((SUBLANES, N_STATE), F32)] * 2
    aliases = {}
    nxi, nxo = _carry(comm, operands, in_specs, out_shape, out_specs, aliases, scratch)

    def body(*refs):
        u_ref, b_ref, c_ref, d_ref, ar_ref, ai_ref = refs[:6]
        x_ins = refs[6:6 + nxi]
        s_ref, y_ref, yg_ref = refs[6 + nxi:9 + nxi]
        x_outs = refs[9 + nxi:9 + nxi + nxo]
        xs_ref, cr_ref, ci_ref = refs[9 + nxi + nxo:12 + nxi + nxo]
        sems = refs[12 + nxi + nxo:]
        step = pl.program_id(0)

        @pl.when(step == 0)
        def _():
            cr_ref[...] = jnp.zeros_like(cr_ref)
            ci_ref[...] = jnp.zeros_like(ci_ref)
            if comm is not None:
                comm.start(x_ins, x_outs, *sems)

        u = u_ref[...]
        xs_ref[...] = _dot(u, b_ref[...])
        _scan_block(xs_ref, s_ref, None, ar_ref, ai_ref, cr_ref, ci_ref, None, None, nt, False)
        y = _dot(s_ref[...], c_ref[...]) + d_ref[...] * u
        y_ref[...] = y
        yg_ref[...] = _gelu(y).astype(yg_ref.dtype)

        if comm is not None:
            @pl.when(step == nrb - 1)
            def _():
                comm.wait(x_ins, x_outs, *sems)

    outs = pl.pallas_call(
        body, name=name, grid=(nrb,), in_specs=in_specs, out_specs=out_specs, out_shape=out_shape,
        scratch_shapes=scratch, input_output_aliases=aliases,
        compiler_params=_cparams(5 * _nbytes((tb, 2 * N_STATE), F32), dimension_semantics=("arbitrary",)),
    )(*operands)
    if comm is not None:
        comm.done(list(outs[3:]))
    return outs[0], outs[1], outs[2]


def _ssm_bwd(dy, z, S, bcat, ccat, dskip, lam_r, lam_i, name, comm=None):
    L = z.shape[0]
    tb = _tile(L, _SCAN_TB)
    nrb = L // tb
    nt = tb // SUBLANES

    full = lambda shp: pl.BlockSpec(shp, lambda i: (0, 0))
    rows = lambda w, col=0: pl.BlockSpec((tb, w), lambda i: (nrb - 1 - i, col))
    acc = full((SUBLANES, N_STATE))
    operands = [dy, _hbm(z), _hbm(S), bcat, ccat, dskip, lam_r, lam_i]
    in_specs = [rows(C_WIDTH), rows(C_WIDTH, 5), rows(2 * N_STATE), full((C_WIDTH, 2 * N_STATE)),
                full((2 * N_STATE, C_WIDTH)), full((1, C_WIDTH)), full((1, N_STATE)), full((1, N_STATE))]
    out_specs = [rows(C_WIDTH), full((C_WIDTH, 2 * N_STATE)), full((2 * N_STATE, C_WIDTH)), full((1, C_WIDTH)), acc, acc]
    out_shape = [_sds((L, C_WIDTH), _ACT), _sds((C_WIDTH, 2 * N_STATE), F32), _sds((2 * N_STATE, C_WIDTH), F32),
                 _sds((1, C_WIDTH), F32), _sds((SUBLANES, N_STATE), F32), _sds((SUBLANES, N_STATE), F32)]
    scratch = [pltpu.VMEM((tb, 2 * N_STATE), F32)] * 2 + [pltpu.VMEM((SUBLANES, N_STATE), F32)] * 2
    aliases = {}
    nxi, nxo = _carry(comm, operands, in_specs, out_shape, out_specs, aliases, scratch)

    def body(*refs):
        dy_ref, u_ref, s_ref, b_ref, c_ref, d_ref, ar_ref, ai_ref = refs[:8]
        x_ins = refs[8:8 + nxi]
        du_ref, db_ref, dc_ref, dd_ref, dar_ref, dai_ref = refs[8 + nxi:14 + nxi]
        x_outs = refs[14 + nxi:14 + nxi + nxo]
        xs_ref, gs_ref, cr_ref, ci_ref = refs[14 + nxi + nxo:18 + nxi + nxo]
        sems = refs[18 + nxi + nxo:]
        step = pl.program_id(0)

        @pl.when(step == 0)
        def _():
            for r in (cr_ref, ci_ref, db_ref, dc_ref, dd_ref, dar_ref, dai_ref):
                r[...] = jnp.zeros_like(r)
            if comm is not None:
                comm.start(x_ins, x_outs, *sems)

        dyv, u = dy_ref[...], u_ref[...]
        xs_ref[...] = _dot(dyv, c_ref[...], _NT)
        _scan_block(xs_ref, gs_ref, s_ref, ar_ref, ai_ref, cr_ref, ci_ref, dar_ref, dai_ref, nt, True)
        g = gs_ref[...]
        du_ref[...] = (_dot(g, b_ref[...], _NT) + dyv * d_ref[...]).astype(du_ref.dtype)
        db_ref[...] += _dot(u, g, _TN)
        dc_ref[...] += _dot(s_ref[...], dyv, _TN)
        dd_ref[...] += jnp.sum(dyv * u, axis=0, keepdims=True)

        @pl.when(step == nrb - 1)
        def _():
            dar_ref[...] = jnp.broadcast_to(jnp.sum(dar_ref[...], axis=0, keepdims=True), dar_ref.shape)
            dai_ref[...] = jnp.broadcast_to(jnp.sum(dai_ref[...], axis=0, keepdims=True), dai_ref.shape)
            if comm is not None:
                comm.wait(x_ins, x_outs, *sems)

    outs = pl.pallas_call(
        body, name=name, grid=(nrb,), in_specs=in_specs, out_specs=out_specs, out_shape=out_shape,
        scratch_shapes=scratch, input_output_aliases=aliases,
        compiler_params=_cparams(7 * _nbytes((tb, 2 * N_STATE), F32), dimension_semantics=("arbitrary",)),
    )(*operands)
    if comm is not None:
        comm.done(list(outs[6:]))
    return tuple(outs[:6])


_GROUPS = ((0, 256), (256, 768), (768, 1024))


def _merge_fwd(ya, yb, g12, mixg, name):
    L = ya.shape[0]
    tl = _tile(L, _TL)

    def body(a_ref, b_ref, g_ref, m_ref, o_ref):
        g12v = g_ref[...]
        yc = g12v[:, :C_WIDTH] * _sigmoid(g12v[:, C_WIDTH:])
        for (lo, hi), y in zip(_GROUPS, (a_ref[...], b_ref[...], yc)):
            r = lax.rsqrt(jnp.mean(y * y, axis=-1, keepdims=True) + EPS)
            o_ref[:, lo:hi] = ((y * r) * m_ref[:, lo:hi]).astype(o_ref.dtype)

    row = lambda w: pl.BlockSpec((tl, w), lambda i: (i, 0))
    return pl.pallas_call(
        body, name=name, grid=(L // tl,),
        in_specs=[row(256), row(512), row(512), pl.BlockSpec((1, D_MODEL), lambda i: (0, 0))],
        out_specs=row(D_MODEL), out_shape=_sds((L, D_MODEL), _ACT),
    )(ya, yb, g12, mixg.reshape(1, D_MODEL))


def _merge_bwd(dy, ya, yb, g12, mixg, name):
    L = ya.shape[0]
    tl = _tile(L, _TL)

    def body(d_ref, a_ref, b_ref, g_ref, m_ref, da_ref, db_ref, dg_ref, dm_ref):
        @pl.when(pl.program_id(0) == 0)
        def _():
            dm_ref[...] = jnp.zeros_like(dm_ref)

        g12v = g_ref[...]
        g1, sg = g12v[:, :C_WIDTH], _sigmoid(g12v[:, C_WIDTH:])
        yc = g1 * sg
        outs = []
        for (lo, hi), y in zip(_GROUPS, (a_ref[...], b_ref[...], yc)):
            r = lax.rsqrt(jnp.mean(y * y, axis=-1, keepdims=True) + EPS)
            xhat = y * r
            d = d_ref[:, lo:hi]
            gy = d * m_ref[:, lo:hi]
            outs.append(r * (gy - xhat * jnp.mean(gy * xhat, axis=-1, keepdims=True)))
            dm_ref[:, lo:hi] += jnp.sum(d * xhat, axis=0, keepdims=True)
        da_ref[...] = outs[0]
        db_ref[...] = outs[1]
        dyc = outs[2]
        dg_ref[:, :C_WIDTH] = (dyc * sg).astype(dg_ref.dtype)
        dg_ref[:, C_WIDTH:] = (dyc * g1 * sg * (1.0 - sg)).astype(dg_ref.dtype)

    row = lambda w: pl.BlockSpec((tl, w), lambda i: (i, 0))
    one = pl.BlockSpec((1, D_MODEL), lambda i: (0, 0))
    return pl.pallas_call(
        body, name=name, grid=(L // tl,),
        in_specs=[row(D_MODEL), row(256), row(512), row(512), one],
        out_specs=[row(256), row(512), row(512), one],
        out_shape=[_sds((L, 256), F32), _sds((L, 512), F32),
                   _sds((L, 512), _ACT), _sds((1, D_MODEL), F32)],
    )(dy, ya, yb, g12, mixg.reshape(1, D_MODEL))


def _ple_bwd_elem(dh, gate, e, name):
    L, D = dh.shape
    tl = _tile(L, _TL)

    def body(d_ref, g_ref, e_ref, p_ref, o_ref):
        d, g = d_ref[...], g_ref[...]
        p_ref[...] = (d * e_ref[...] * g * (1.0 - g)).astype(p_ref.dtype)
        o_ref[...] = (d * g).astype(o_ref.dtype)

    row = pl.BlockSpec((tl, D), lambda i: (i, 0))
    return pl.pallas_call(
        body, name=name, grid=(L // tl,), in_specs=[row] * 3, out_specs=[row] * 2,
        out_shape=[_sds((L, D), _ACT)] * 2,
        compiler_params=_cparams(4 * _nbytes((tl, D), F32)),
    )(dh, gate, e)


def _dskip_bwd(dy, z, name):
    L = dy.shape[0]
    tl = _tile(L, _TL)

    def body(d_ref, u_ref, o_ref):
        @pl.when(pl.program_id(0) == 0)
        def _():
            o_ref[...] = jnp.zeros_like(o_ref)

        o_ref[...] += jnp.sum(d_ref[...] * u_ref[...], axis=0, keepdims=True)

    return pl.pallas_call(
        body, name=name, grid=(L // tl,),
        in_specs=[pl.BlockSpec((tl, C_WIDTH), lambda i: (i, 0)), pl.BlockSpec((tl, C_WIDTH), lambda i: (i, 5))],
        out_specs=pl.BlockSpec((1, C_WIDTH), lambda i: (0, 0)),
        out_shape=_sds((1, C_WIDTH), F32),
    )(dy, z)


def _loss_fwd_bwd(y, target):
    L, D = y.shape
    tl = _tile(L, _TL)

    def body(y_ref, t_ref, l_ref, d_ref):
        @pl.when(pl.program_id(0) == 0)
        def _():
            l_ref[...] = jnp.zeros_like(l_ref)

        e = y_ref[...] - t_ref[...]
        d_ref[...] = e * (1.0 / D)
        part = jnp.sum(jnp.sum(e * e, axis=-1, keepdims=True), axis=0, keepdims=True)
        l_ref[...] += jnp.broadcast_to(part, l_ref.shape)

    row = pl.BlockSpec((tl, D), lambda i: (i, 0))
    return pl.pallas_call(
        body, name="loss", grid=(L // tl,), in_specs=[row, row],
        out_specs=[pl.BlockSpec((SUBLANES, LANES), lambda i: (0, 0)), row],
        out_shape=[_sds((SUBLANES, LANES), F32), _sds((L, D), F32)],
    )(y, target)


def _adamw(w, g, m, v, name, emit_g=False):
    R, C = w.shape
    tr = R if R <= 512 else _tile_rows(R, 512)

    def body(w_ref, g_ref, m_ref, v_ref, d_ref, nm_ref, nv_ref, *g_out):
        gv = g_ref[...]
        if emit_g:
            g_out[0][...] = gv
        nm = ADAM_B1 * m_ref[...] + (1.0 - ADAM_B1) * gv
        nv = ADAM_B2 * v_ref[...] + (1.0 - ADAM_B2) * (gv * gv)
        m_hat = nm / (1.0 - ADAM_B1 ** ADAM_STEP)
        v_hat = nv / (1.0 - ADAM_B2 ** ADAM_STEP)
        d_ref[...] = -ADAM_LR * (m_hat / (jnp.sqrt(v_hat) + ADAM_EPS) + ADAM_WD * w_ref[...])
        nm_ref[...] = nm
        nv_ref[...] = nv

    blk = pl.BlockSpec((tr, C), lambda i: (i, 0))
    return pl.pallas_call(
        body, name=name, grid=(R // tr,), in_specs=[blk] * 4, out_specs=[blk] * (3 + emit_g),
        out_shape=[_sds((R, C), F32)] * (3 + emit_g),
        compiler_params=_cparams(8 * _nbytes((tr, C), F32)),
    )(w, g, m, v)


def _tile_rows(R, pref):
    t = pref
    while R % t:
        t -= SUBLANES
    assert t > 0
    return t


def _add_n(xs, name):
    R, C = xs[0].shape
    tr = R if R <= 512 else _tile_rows(R, 512)
    n = len(xs)

    def body(*refs):
        acc = refs[0][...].astype(F32)
        for r in refs[1:n]:
            acc = acc + r[...].astype(F32)
        refs[n][...] = acc

    blk = pl.BlockSpec((tr, C), lambda i: (i, 0))
    return pl.pallas_call(
        body, name=name, grid=(R // tr,), in_specs=[blk] * n, out_specs=blk,
        out_shape=_sds((R, C), F32),
        compiler_params=_cparams((n + 1) * _nbytes((tr, C), F32)),
    )(*xs)


class _Exchange:
    def __init__(self, ins, aliased, fresh, n_sems, start, wait, done):
        self.ins, self.aliased, self.fresh, self.n_sems = ins, aliased, fresh, n_sems
        self.start, self.wait, self.done = start, wait, done


def _mm_host(lp, key, *args, **kw):
    plan = lp.get(key)
    if plan is None:
        return _mm(*args, **kw)
    if not isinstance(plan, _Exchange):
        plan = plan()
    res, outs = _mm(*args, comm=plan, **kw)
    plan.done(outs)
    return res


def _relu2(acc):
    r = jnp.maximum(acc, 0.0)
    return (r * r,)


def _rms_rows(x, g):
    return (x * lax.rsqrt(jnp.mean(x * x, axis=-1, keepdims=True) + EPS)) * g


def _resid_norm_epi(acc, res, g):
    h = res + acc
    return h, _rms_rows(h, g)


def _rms_bwd_epi(acc, h, dres, g):
    r = lax.rsqrt(jnp.mean(h * h, axis=-1, keepdims=True) + EPS)
    xhat = h * r
    gy = acc * g
    dh = dres + r * (gy - xhat * jnp.mean(gy * xhat, axis=-1, keepdims=True))
    return dh, dh, jnp.sum(acc * xhat, axis=0, keepdims=True)


def _layer_fwd(h, xn, lp, cos, sin, g_next):
    L = h.shape[0]
    row = lambda n: lp[n].reshape(1, D_MODEL)
    z = _mm(xn, lp["W"]("w_in"), mode="nn", M=L, N=IN_COLS, K=D_MODEL, b_cb=True, out_dtypes=[F32], name="f_w_in")
    ya = _gmlp_fwd(z, lp["ws"], lp["bfull"], lp["lgf"], lp["lbf"], "f_gmlp")
    q, k = _qk_prep(z, cos, sin, lp["gq"], lp["gk"], "f_qk_prep")
    yb = _attn_fwd(q, k, z, lp["sinks"], "f_attn")
    plan = lp.get("x_ssm")
    S, y, yg = _ssm_fwd(z, lp["bcat"], lp["ccat"], lp["dskip"], lp["lam_r"], lp["lam_i"], "f_ssm",
                        comm=None if plan is None else plan())
    g12 = _mm(yg, lp["W"]("w12"), mode="nn", M=L, N=2 * C_WIDTH, K=C_WIDTH, out_dtypes=[F32], name="f_glu")
    ycat = _merge_fwd(ya, yb, g12, lp["mix_out_g"], "f_merge")
    h1, hn = _mm_host(lp, "x_out", ycat, lp["W"]("w_out"), mode="nn", M=L, N=D_MODEL, K=D_MODEL, extras=[(h, 0), (row("mlp_norm_g"), 0)],
                 epi=_resid_norm_epi, out_dtypes=[F32, _ACT], name="f_w_out")
    r = _mm_host(lp, "x_ff1", hn, lp["W"]("w_ff1"), mode="nn", M=L, N=D_FF, K=D_MODEL, b_cb=True, epi=_relu2,
                 out_dtypes=[_ACT], name="f_ff1")
    h2, hn3 = _mm_host(lp, "x_ff2", r, lp["W"]("w_ff2"), mode="nn", M=L, N=D_MODEL, K=D_FF,
                       extras=[(h1, 0), (row("ple_norm_g"), 0)],
                       epi=_resid_norm_epi, out_dtypes=[F32, _ACT], name="f_ff2")
    e = _mm(lp["p"], lp["W"]("w_ple_proj"), mode="nn", M=L, N=D_MODEL, K=PLE_DIM, b_cb=True, tk=PLE_DIM,
            out_dtypes=[F32], name="f_ple_proj")

    def gate_epi(acc, h2_, e_, *g):
        gate_ = _sigmoid(acc)
        h3_ = h2_ + gate_ * e_
        return (h3_, gate_) + ((_rms_rows(h3_, g[0]),) if g else ())

    outs = _mm_host(lp, "x_gate", hn3, lp["W"]("w_ple_gate"), mode="nn", M=L, N=D_MODEL, K=D_MODEL,
                    extras=[(h2, 0), (e, 0)] + ([(g_next.reshape(1, D_MODEL), 0)] if g_next is not None else []),
                    epi=gate_epi, out_dtypes=[F32, F32] + ([_ACT] if g_next is not None else []), name="f_ple_gate")
    h3, gate = outs[0], outs[1]
    xn_next = outs[2] if g_next is not None else None
    saved = dict(h=h, xn=xn, z=z, ya=ya, q=q, k=k, yb=yb, S=S, y=y, yg=yg, g12=g12, ycat=ycat, h1=h1, hn=hn,
                 r=r, h2=h2, hn3=hn3, e=e, gate=gate)
    return h3, xn_next, saved


def _layer_bwd(dh3, lp, sv, cos, sin):
    L = dh3.shape[0]
    z = sv["z"]
    dpre, de = _ple_bwd_elem(dh3, sv["gate"], sv["e"], "b_ple_elem")
    stk = {n: None for n in BIG}
    d_gate = _mm(sv["hn3"], dpre, mode="tn", M=D_MODEL, N=D_MODEL, K=L, out_dtypes=[F32], name="b_dw_gate",
                 o_stack=stk["w_ple_gate"])
    d_proj = _mm(lp["p"], de, mode="tn", M=PLE_DIM, N=D_MODEL, K=L, o_cb=True, tm=PLE_DIM,
                 out_dtypes=[F32], name="b_dw_proj", o_stack=stk["w_ple_proj"])
    row = lambda n: lp[n].reshape(1, D_MODEL)
    dh2, dh2_op, dg_ple = _mm_host(lp, "x_bwd0", dpre, lp["W"]("w_ple_gate"), mode="nt", M=L, N=D_MODEL, K=D_MODEL,
                                   extras=[(sv["h2"], 0), (dh3, 0), (row("ple_norm_g"), 0)], epi=_rms_bwd_epi,
                                   out_dtypes=[F32, _ACT, F32], n_acc=1, name="b_dx_gate")
    da = _mm_host(lp, "x_bwd", dh2_op, lp["W"]("w_ff2"), mode="nt", M=L, N=D_FF, K=D_MODEL, extras=[(sv["r"], 0)],
                  epi=lambda acc, r_: (acc * (2.0 * jnp.sqrt(r_.astype(F32))),), out_dtypes=[_ACT], name="b_dx_ff2")
    d_ff2 = _mm_host(lp, "x_bwd2", sv["r"], dh2_op, mode="tn", M=D_FF, N=D_MODEL, K=L, out_dtypes=[F32], name="b_dw_ff2")
    d_ff1 = _mm(sv["hn"], da, mode="tn", M=D_MODEL, N=D_FF, K=L, o_cb=True, out_dtypes=[F32], name="b_dw_ff1",
                o_stack=stk["w_ff1"])
    dh1, dh1_op, dg_mlp = _mm(da, lp["W"]("w_ff1"), mode="nt", M=L, N=D_MODEL, K=D_FF, b_cb=True,
                              extras=[(sv["h1"], 0), (dh2, 0), (row("mlp_norm_g"), 0)], epi=_rms_bwd_epi,
                              out_dtypes=[F32, _ACT, F32], n_acc=1, name="b_dx_ff1")
    d_out = _mm(sv["ycat"], dh1_op, mode="tn", M=D_MODEL, N=D_MODEL, K=L, out_dtypes=[F32], name="b_dw_out",
                o_stack=stk["w_out"])
    if "early" in lp:
        lp["early"](dict(w_out=d_out, w_ff1=d_ff1, w_ff2=d_ff2, w_ple_gate=d_gate, w_ple_proj=d_proj))
    dycat = _mm_host(lp, "x_e0", dh1_op, lp["W"]("w_out"), mode="nt", M=L, N=D_MODEL, K=D_MODEL, out_dtypes=[F32],
                     name="b_dx_out")
    dya, dyb, dg12, dmix = _merge_bwd(dycat, sv["ya"], sv["yb"], sv["g12"], lp["mix_out_g"], "b_merge")
    d_w12 = _mm(sv["yg"], dg12, mode="tn", M=C_WIDTH, N=2 * C_WIDTH, K=L, tm=C_WIDTH, out_dtypes=[F32], name="b_dw_glu",
                o_stack=stk["w12"])
    dy = _mm(dg12, lp["W"]("w12"), mode="nt", M=L, N=C_WIDTH, K=2 * C_WIDTH, tk=2 * C_WIDTH, extras=[(sv["y"], 0)],
             epi=lambda acc, y_: (acc * _gelu_grad(y_),), out_dtypes=[F32], name="b_dx_glu")
    plan = lp.get("x_e1")
    dzc, d_bcat, d_ccat, dd, dar, dai = _ssm_bwd(dy, z, sv["S"], lp["bcat"], lp["ccat"], lp["dskip"],
                                                 lp["lam_r"], lp["lam_i"], "b_ssm",
                                                 comm=None if plan is None else plan())
    dq, dkc, dkp, dvc, dvp, dsink = _attn_bwd(sv["q"], sv["k"], z, lp["sinks"], dyb, "b_attn")
    dzq, dzk, dzv, dgq, dgk = _qk_prep_bwd(z, dq, dkc, dkp, dvc, dvp, cos, sin, lp["gq"], lp["gk"], "b_qk_prep")
    dza, dws, dbs, dlg, dlb = _gmlp_bwd(z, dya, lp["ws"], lp["wsT"], lp["bfull"], lp["lgf"], lp["lbf"], "b_gmlp")
    dz = jnp.concatenate([dza, dzq, dzk, dzv, dzc], axis=1)
    d_in = _mm(sv["xn"], dz, mode="tn", M=D_MODEL, N=IN_COLS, K=L, o_cb=True, out_dtypes=[F32], name="b_dw_in",
               o_stack=stk["w_in"])
    dh, dg_attn = _mm(dz, lp["W"]("w_in"), mode="nt", M=L, N=D_MODEL, K=IN_COLS, b_cb=True,
                         extras=[(sv["h"], 0), (dh1, 0), (row("attn_norm_g"), 0)],
                         epi=lambda *t: (lambda o: (o[0], o[2]))(_rms_bwd_epi(*t)),
                         out_dtypes=[F32, F32], n_acc=1, name="b_dx_in")
    grads = dict(w_in=d_in, w12=d_w12, w_out=d_out, w_ff1=d_ff1, w_ff2=d_ff2, w_ple_gate=d_gate, w_ple_proj=d_proj,
                 attn_norm_g=dg_attn.reshape(D_MODEL), mlp_norm_g=dg_mlp.reshape(D_MODEL),
                 ple_norm_g=dg_ple.reshape(D_MODEL), mix_out_g=dmix.reshape(D_MODEL),
                 dws=dws, dbs=dbs, dlg=dlg, dlb=dlb, dgq=dgq, dgk=dgk, dsink=dsink,
                 dar=dar, dai=dai, d_bcat=d_bcat, d_ccat=d_ccat, dd=dd)
    return dh, grads


SMALL = ("attn_norm_g", "gmlp_ln_g", "gmlp_ln_b", "gmlp_ws", "gmlp_bs", "q_norm_g", "k_norm_g", "sinks",
         "ssm_a_re", "ssm_a_im", "ssm_log_dt", "ssm_b_re", "ssm_b_im", "ssm_c_re", "ssm_c_im", "ssm_d",
         "mix_out_g", "mlp_norm_g", "ple_norm_g")
BIG = ("w_in", "w12", "w_out", "w_ff1", "w_ff2", "w_ple_gate", "w_ple_proj")
COL_SHARDED = ("w_in", "w_ff1", "w_ple_proj")


def _block_diag(t):
    nl, g, a, b = t.shape
    eye = jnp.eye(g, dtype=t.dtype)
    return (t[:, :, :, None, :] * eye[None, :, None, :, None]).reshape(nl, g * a, g * b)


def _diag_blocks(t, a, b):
    nl = t.shape[0]
    t = t.reshape(nl, C_GROUPS, a, C_GROUPS, b)
    idx = jnp.arange(C_GROUPS)
    return jnp.moveaxis(t[:, idx, :, idx, :], 0, 1)


def _local_step(x, p, positions, target, sw, bw):
    nl = sw["attn_norm_g"].shape[0]
    G = nl * C_GROUPS
    zeros = lambda *s: jnp.zeros(s, F32)
    are = sw["ssm_a_re"].reshape(G, 1, C_STATE)
    aim = sw["ssm_a_im"].reshape(G, 1, C_STATE)
    ldt = jnp.broadcast_to(sw["ssm_log_dt"][..., None], (nl, C_GROUPS, C_STATE)).reshape(G, 1, C_STATE)
    bre = jnp.swapaxes(sw["ssm_b_re"], -1, -2).reshape(G, C_GROUP, C_STATE)
    bim = jnp.swapaxes(sw["ssm_b_im"], -1, -2).reshape(G, C_GROUP, C_STATE)
    lr, li, bbr, bbi = _ssm_prep(are, aim, ldt, bre, bim)
    unflat = lambda t: t.reshape(nl, C_GROUPS, C_GROUP, C_STATE)
    lp = dict(
        attn_norm_g=sw["attn_norm_g"], mlp_norm_g=sw["mlp_norm_g"], ple_norm_g=sw["ple_norm_g"],
        mix_out_g=sw["mix_out_g"], sinks=sw["sinks"],
        ws=sw["gmlp_ws"], wsT=jnp.swapaxes(sw["gmlp_ws"], -1, -2),
        bfull=jnp.concatenate([zeros(nl, A_HEADS, CHUNK, HEAD_DIM),
                               jnp.broadcast_to(sw["gmlp_bs"][..., None], (nl, A_HEADS, CHUNK, HEAD_DIM))], axis=-1),
        lgf=jnp.concatenate([zeros(nl, A_HEADS, HEAD_DIM), sw["gmlp_ln_g"]], axis=-1),
        lbf=jnp.concatenate([zeros(nl, A_HEADS, HEAD_DIM), sw["gmlp_ln_b"]], axis=-1),
        gq=jnp.tile(sw["q_norm_g"], (1, 2)).reshape(nl, 1, LANES),
        gk=jnp.tile(sw["k_norm_g"], (1, 2)).reshape(nl, 1, LANES),
        lam_r=lr.reshape(nl, 1, N_STATE), lam_i=li.reshape(nl, 1, N_STATE),
        bcat=jnp.concatenate([_block_diag(unflat(bbr)), _block_diag(unflat(bbi))], axis=-1),
        ccat=jnp.concatenate([_block_diag(jnp.swapaxes(sw["ssm_c_re"], -1, -2)),
                              -_block_diag(jnp.swapaxes(sw["ssm_c_im"], -1, -2))], axis=1),
        dskip=sw["ssm_d"].reshape(nl, 1, C_WIDTH))
    cos, sin = _rope_tables(positions)

    def layer_params(l, hooks):
        lpi = {n: v[l] for n, v in lp.items()}
        lpi["W"] = lambda n: bw.layer(l)[n]
        lpi["p"] = (p, l)
        lpi.update(hooks)
        return lpi

    h, saved = x, []
    xn = _rms_fwd(x, sw["attn_norm_g"][0], "f_norm_attn")
    for l in range(nl):
        g_next = sw["attn_norm_g"][l + 1] if l + 1 < nl else None
        h, xn, sv = _layer_fwd(h, xn, layer_params(l, bw.fwd_hooks(l)), cos, sin, g_next)
        saved.append(sv)
    sse, dh = _loss_fwd_bwd(h, target)

    per_layer = [None] * nl
    for l in reversed(range(nl)):
        dh, gl = _layer_bwd(dh, layer_params(l, bw.bwd_hooks(l)), saved[l], cos, sin)
        bw.grads(l, {n: gl.pop(n) for n in BIG})
        per_layer[l] = gl
    grad_x = dh
    g = {n: jnp.stack([per_layer[l][n] for l in range(nl)]) for n in per_layer[0]}

    d_bcat = g["d_bcat"]
    dbr = _diag_blocks(d_bcat[:, :, :N_STATE], C_GROUP, C_STATE).reshape(G, C_GROUP, C_STATE)
    dbi = _diag_blocks(d_bcat[:, :, N_STATE:], C_GROUP, C_STATE).reshape(G, C_GROUP, C_STATE)
    dlr = g["dar"][:, 0].reshape(G, 1, C_STATE)
    dli = g["dai"][:, 0].reshape(G, 1, C_STATE)
    g_are, g_aim, g_ldt, g_bre, g_bim = _ssm_prep_bwd(are, aim, ldt, bre, bim, dlr, dli, dbr, dbi)
    d_ccat = g["d_ccat"]
    sg = dict(
        attn_norm_g=g["attn_norm_g"], mlp_norm_g=g["mlp_norm_g"], ple_norm_g=g["ple_norm_g"], mix_out_g=g["mix_out_g"],
        gmlp_ln_g=g["dlg"][:, :, 0, HEAD_DIM:], gmlp_ln_b=g["dlb"][:, :, 0, HEAD_DIM:],
        gmlp_ws=g["dws"], gmlp_bs=g["dbs"][:, :, :, HEAD_DIM],
        q_norm_g=g["dgq"][:, 0, :HEAD_DIM], k_norm_g=g["dgk"][:, 0, :HEAD_DIM],
        sinks=g["dsink"][:, 0, :B_Q_HEADS],
        ssm_a_re=g_are.reshape(nl, C_GROUPS, C_STATE), ssm_a_im=g_aim.reshape(nl, C_GROUPS, C_STATE),
        ssm_log_dt=g_ldt[:, 0, 0].reshape(nl, C_GROUPS),
        ssm_b_re=jnp.swapaxes(g_bre.reshape(nl, C_GROUPS, C_GROUP, C_STATE), -1, -2),
        ssm_b_im=jnp.swapaxes(g_bim.reshape(nl, C_GROUPS, C_GROUP, C_STATE), -1, -2),
        ssm_c_re=jnp.swapaxes(_diag_blocks(d_ccat[:, :N_STATE], C_STATE, C_GROUP), -1, -2),
        ssm_c_im=-jnp.swapaxes(_diag_blocks(d_ccat[:, N_STATE:], C_STATE, C_GROUP), -1, -2),
        ssm_d=g["dd"].reshape(nl, C_GROUPS, C_GROUP),
    )
    return (sse[0, 0], grad_x, sg) + tuple(bw.finish(_pack(sg)))


_ANY = pl.BlockSpec(memory_space=pl.ANY)
N_LAYERS = 4


def _mesh_pos():
    x, y, c = lax.axis_index("x"), lax.axis_index("y"), lax.axis_index("c")
    chips = [(1 - x, y), (x, 1 - y), (1 - x, 1 - y)]
    return x, y, c, 2 * x + y, chips


def _cast_into_slot(ws, j, name):
    nl, R, _ = ws[0].shape
    widths = [w.shape[2] for w in ws]
    C = sum(widths)
    tr = R if R <= 512 else _tile_rows(R, 512)
    nw = len(ws)

    def body(s_ref, *refs):
        o_ref = refs[nw]
        off = 0
        for r, wd in zip(refs[:nw], widths):
            o_ref[:, off:off + wd] = r[...].astype(o_ref.dtype)
            off += wd

    return pl.pallas_call(
        body, name=name,
        grid_spec=pltpu.PrefetchScalarGridSpec(
            num_scalar_prefetch=1, grid=(nl, R // tr),
            in_specs=[pl.BlockSpec((None, tr, wd), lambda l, i, s: (l, i, 0)) for wd in widths],
            out_specs=pl.BlockSpec((None, None, tr, C), lambda l, i, s: (l, s[0], i, 0))),
        out_shape=_sds((nl, N_CHIPS, R, C), _MXU),
    )(jnp.reshape(j, (1,)).astype(jnp.int32), *ws)


def _gather_weights(bufs):
    nk = len(bufs)

    def body(*refs):
        ins, outs = refs[:nk], refs[nk:2 * nk]
        send_sems, recv_sems = refs[2 * nk:]
        x, y, c, j, chips = _mesh_pos()
        mine, other = pl.ds(2 * c, 2), pl.ds(2 * (1 - c), 2)

        def ici(t, q):
            cx, cy = chips[q]
            return pltpu.make_async_remote_copy(
                src_ref=ins[t].at[mine, j], dst_ref=outs[t].at[mine, j],
                send_sem=send_sems.at[6 * t + q], recv_sem=recv_sems.at[6 * t + q],
                device_id=(cx, cy, c), device_id_type=MESH)

        def landed(t, q):
            cx, cy = chips[q]
            blk = outs[t].at[mine, 2 * cx + cy]
            return pltpu.make_async_remote_copy(
                src_ref=blk, dst_ref=blk, send_sem=send_sems.at[6 * t + q], recv_sem=recv_sems.at[6 * t + q],
                device_id=(cx, cy, c), device_id_type=MESH)

        def fwd(t, q, rows):
            cx, cy = chips[q]
            blk = outs[t].at[rows, 2 * cx + cy]
            return pltpu.make_async_remote_copy(
                src_ref=blk, dst_ref=blk, send_sem=send_sems.at[6 * t + 3 + q], recv_sem=recv_sems.at[6 * t + 3 + q],
                device_id=(x, y, 1 - c), device_id_type=MESH)

        for t in range(nk):
            for q in range(3):
                ici(t, q).start()
        for t in range(nk):
            for q in range(3):
                landed(t, q).wait_recv()
                fwd(t, q, mine).start()
        for t in range(nk):
            for q in range(3):
                fwd(t, q, other).wait_recv()
        for t in range(nk):
            for q in range(3):
                ici(t, q).wait_send()
                fwd(t, q, mine).wait_send()

    return pl.pallas_call(
        body, name="gather_weights", in_specs=[_ANY] * nk, out_specs=[_ANY] * nk,
        out_shape=[_sds(b.shape, b.dtype) for b in bufs],
        input_output_aliases={t: t for t in range(nk)},
        scratch_shapes=[pltpu.SemaphoreType.DMA((6 * nk,)), pltpu.SemaphoreType.DMA((6 * nk,))],
    )(*bufs)


def _exchange_sibling_half(gl):
    nk = len(gl)

    def body(*refs):
        ins, outs = refs[:nk], refs[nk:2 * nk]
        send_sems, recv_sems = refs[2 * nk:]
        x, y, c, _, _ = _mesh_pos()
        cps = [pltpu.make_async_remote_copy(
            src_ref=ins[t].at[pl.ds(2 * (1 - c), 2)], dst_ref=outs[t],
            send_sem=send_sems.at[t], recv_sem=recv_sems.at[t],
            device_id=(x, y, 1 - c), device_id_type=MESH) for t in range(nk)]
        for cp in cps:
            cp.start()
        for cp in cps:
            cp.wait()

    return pl.pallas_call(
        body, name="reduce_sibling", in_specs=[_ANY] * nk, out_specs=[_ANY] * nk,
        out_shape=[_sds((2,) + g.shape[1:], g.dtype) for g in gl],
        scratch_shapes=[pltpu.SemaphoreType.DMA((nk,)), pltpu.SemaphoreType.DMA((nk,))],
    )(*gl)


def _exchange_chips(ps):
    nk = len(ps)

    def body(*refs):
        ins, outs = refs[:nk], refs[nk:2 * nk]
        send_sems, recv_sems = refs[2 * nk:]
        x, y, c, j, chips = _mesh_pos()

        def send(t, q):
            cx, cy = chips[q]
            return pltpu.make_async_remote_copy(
                src_ref=ins[t].at[:, 2 * cx + cy], dst_ref=outs[t].at[j],
                send_sem=send_sems.at[3 * t + q], recv_sem=recv_sems.at[3 * t + q],
                device_id=(cx, cy, c), device_id_type=MESH)

        def landed(t, q):
            cx, cy = chips[q]
            blk = outs[t].at[2 * cx + cy]
            return pltpu.make_async_remote_copy(
                src_ref=blk, dst_ref=blk, send_sem=send_sems.at[3 * t + q], recv_sem=recv_sems.at[3 * t + q],
                device_id=(cx, cy, c), device_id_type=MESH)

        for t in range(nk):
            for q in range(3):
                send(t, q).start()
        for t in range(nk):
            for q in range(3):
                landed(t, q).wait_recv()
        for t in range(nk):
            for q in range(3):
                send(t, q).wait_send()

    return pl.pallas_call(
        body, name="reduce_chips", in_specs=[_ANY] * nk, out_specs=[_ANY] * nk,
        out_shape=[_sds((N_CHIPS, 2) + p.shape[2:], p.dtype) for p in ps],
        scratch_shapes=[pltpu.SemaphoreType.DMA((3 * nk,)), pltpu.SemaphoreType.DMA((3 * nk,))],
    )(*ps)


def _share_sibling(fs):
    nk = len(fs)

    def body(*refs):
        ins, outs = refs[:nk], refs[nk:2 * nk]
        send_sems, recv_sems = refs[2 * nk:]
        x, y, c, _, _ = _mesh_pos()
        mine = pl.ds(2 * c, 2)
        cps = [pltpu.make_async_remote_copy(
            src_ref=ins[t].at[mine], dst_ref=outs[t].at[mine], send_sem=send_sems.at[t], recv_sem=recv_sems.at[t],
            device_id=(x, y, 1 - c), device_id_type=MESH) for t in range(nk)]
        for cp in cps:
            cp.start()
        for cp in cps:
            cp.wait_send()
        for t in range(nk):
            blk = outs[t].at[pl.ds(2 * (1 - c), 2)]
            pltpu.make_async_remote_copy(
                src_ref=blk, dst_ref=blk, send_sem=send_sems.at[t], recv_sem=recv_sems.at[t],
                device_id=(x, y, 1 - c), device_id_type=MESH).wait_recv()

    return pl.pallas_call(
        body, name="share_sibling", in_specs=[_ANY] * nk, out_specs=[_ANY] * nk,
        out_shape=[_sds(f.shape, f.dtype) for f in fs],
        input_output_aliases={t: t for t in range(nk)},
        scratch_shapes=[pltpu.SemaphoreType.DMA((nk,)), pltpu.SemaphoreType.DMA((nk,))],
    )(*fs)


def _add_own_half(gl, r1, c, name):
    _, ns, R, C = gl.shape
    rows = 2 * ns * R
    tr = _tile_rows(rows, 512)
    nblk = rows // tr

    def body(s_ref, a_ref, b_ref, o_ref):
        o_ref[...] = (a_ref[...] + b_ref[...]).astype(o_ref.dtype)

    out = pl.pallas_call(
        body, name=name,
        grid_spec=pltpu.PrefetchScalarGridSpec(
            num_scalar_prefetch=1, grid=(nblk,),
            in_specs=[pl.BlockSpec((tr, C), lambda i, s: (s[0] * nblk + i, 0)), pl.BlockSpec((tr, C), lambda i, s: (i, 0))],
            out_specs=pl.BlockSpec((tr, C), lambda i, s: (i, 0))),
        out_shape=_sds((rows, C), _WIRE),
        compiler_params=_cparams(3 * _nbytes((tr, C), F32)),
    )(jnp.reshape(c, (1,)).astype(jnp.int32), gl.reshape(2 * rows, C), r1.reshape(rows, C))
    return out.reshape(2, ns, R, C)


def _add_chips(p, r2, j, c, name):
    _, ns, R, C = p.shape
    tr = R if R <= 512 else _tile_rows(R, 512)

    def body(s_ref, own, a1, a2, a3, o_ref):
        f = lambda r: r[...].astype(F32)
        o_ref[...] = ((f(own) + f(a1)) + f(a2)) + f(a3)

    blk = (None, None, tr, C)
    return pl.pallas_call(
        body, name=name,
        grid_spec=pltpu.PrefetchScalarGridSpec(
            num_scalar_prefetch=1, grid=(2, R // tr),
            in_specs=[pl.BlockSpec(blk, lambda h, i, s: (h, s[0], i, 0))]
            + [pl.BlockSpec(blk, lambda h, i, s, k=k: ((s[0] + k) % N_CHIPS, h, i, 0)) for k in (1, 2, 3)],
            out_specs=pl.BlockSpec((None, tr, C), lambda h, i, s: (2 * s[1] + h, i, 0))),
        out_shape=_sds((N_LAYERS, R, C), F32),
        compiler_params=_cparams(6 * _nbytes((tr, C), F32)),
    )(jnp.stack([j, c]).astype(jnp.int32), p, r2, r2, r2)


def _allreduce_small(buf, plan=None):
    Rs = buf.shape[0]
    nx = 0 if plan is None else len(plan.ins)
    x_out_shape = [] if plan is None else [_sds(sh, dt) for sh, dt in plan.fresh]
    assert plan is None or not any(plan.aliased)
    nxo = len(x_out_shape)

    def body(*refs):
        b_ref, x_ins = refs[0], refs[1:1 + nx]
        o_ref, x_outs = refs[1 + nx], refs[2 + nx:2 + nx + nxo]
        t_ref, slots_ref, send_sems, recv_sems = refs[2 + nx + nxo:6 + nx + nxo]
        x_sems = refs[6 + nx + nxo:]
        if plan is not None:
            plan.start(x_ins, x_outs, *x_sems)
        x, y, c, j, chips = _mesh_pos()
        sib = pltpu.make_async_remote_copy(
            src_ref=b_ref, dst_ref=t_ref, send_sem=send_sems.at[0], recv_sem=recv_sems.at[0],
            device_id=(x, y, 1 - c), device_id_type=MESH)
        sib.start()
        sib.wait()
        slots_ref[j] = b_ref[...] + t_ref[...]

        def send(q):
            cx, cy = chips[q]
            return pltpu.make_async_remote_copy(
                src_ref=slots_ref.at[j], dst_ref=slots_ref.at[j], send_sem=send_sems.at[1 + q],
                recv_sem=recv_sems.at[1 + q], device_id=(cx, cy, c), device_id_type=MESH)

        def landed(q):
            cx, cy = chips[q]
            blk = slots_ref.at[2 * cx + cy]
            return pltpu.make_async_remote_copy(
                src_ref=blk, dst_ref=blk, send_sem=send_sems.at[1 + q], recv_sem=recv_sems.at[1 + q],
                device_id=(cx, cy, c), device_id_type=MESH)

        for q in range(3):
            send(q).start()
        for q in range(3):
            landed(q).wait_recv()
        for q in range(3):
            send(q).wait_send()
        o_ref[...] = ((slots_ref[0] + slots_ref[1]) + slots_ref[2]) + slots_ref[3]
        if plan is not None:
            plan.wait(x_ins, x_outs, *x_sems)

    vm = pl.BlockSpec(memory_space=pltpu.VMEM)
    outs = pl.pallas_call(
        body, name="allreduce_small", in_specs=[vm] + [_ANY] * nx, out_specs=[vm] + [_ANY] * nxo,
        out_shape=[_sds((Rs, LANES), F32)] + x_out_shape,
        scratch_shapes=[pltpu.VMEM((Rs, LANES), F32), pltpu.VMEM((N_CHIPS, Rs, LANES), F32),
                        pltpu.SemaphoreType.DMA((4,)), pltpu.SemaphoreType.DMA((4,))]
        + ([pltpu.SemaphoreType.DMA((plan.n_sems,))] * 2 if plan is not None else []),
        compiler_params=_cparams(4 * _nbytes((Rs, LANES), F32)),
    )(buf, *([] if plan is None else plan.ins))
    if plan is not None:
        plan.done(list(outs[1:]))
    return outs[0]


def _own_rows(c, R):
    return pl.ds(c * (R // 2), R // 2)


def _cast_layer_slot(ws, l, j, name):
    _, R, _ = ws[0].shape
    widths = [w.shape[2] for w in ws]
    C = sum(widths)
    tr = R if R <= 512 else _tile_rows(R, 512)
    nw = len(ws)

    def body(s_ref, *refs):
        o_ref = refs[nw]
        off = 0
        for r, wd in zip(refs[:nw], widths):
            o_ref[:, off:off + wd] = r[...].astype(o_ref.dtype)
            off += wd

    return pl.pallas_call(
        body, name=name,
        grid_spec=pltpu.PrefetchScalarGridSpec(
            num_scalar_prefetch=1, grid=(R // tr,),
            in_specs=[pl.BlockSpec((None, tr, wd), lambda i, s: (l, i, 0)) for wd in widths],
            out_specs=pl.BlockSpec((None, tr, C), lambda i, s: (s[0], i, 0))),
        out_shape=_sds((N_CHIPS, R, C), _MXU),
    )(jnp.reshape(j, (1,)).astype(jnp.int32), *ws)


def _gather_ici(bufs, done):
    nk = len(bufs)

    def copy(ins, outs, ss, rs, t, q, landed):
        x, y, c, j, chips = _mesh_pos()
        cx, cy = chips[q]
        rows = _own_rows(c, ins[t].shape[1])
        src = outs[t].at[2 * cx + cy, rows] if landed else ins[t].at[j, rows]
        dst = outs[t].at[2 * cx + cy, rows] if landed else outs[t].at[j, rows]
        return pltpu.make_async_remote_copy(src_ref=src, dst_ref=dst, send_sem=ss.at[3 * t + q], recv_sem=rs.at[3 * t + q],
                                            device_id=(cx, cy, c), device_id_type=MESH)

    def start(ins, outs, ss, rs):
        for t in range(nk):
            for q in range(3):
                copy(ins, outs, ss, rs, t, q, False).start()

    def wait(ins, outs, ss, rs):
        for t in range(nk):
            for q in range(3):
                copy(ins, outs, ss, rs, t, q, True).wait_recv()
                copy(ins, outs, ss, rs, t, q, False).wait_send()

    return _Exchange(bufs, [True] * nk, [], 3 * nk, start, wait, done)


def _gather_d2d(bufs, done):
    nk = len(bufs)

    def copy(ins, outs, ss, rs, t, q, mine):
        x, y, c, j, chips = _mesh_pos()
        cx, cy = chips[q]
        rows = _own_rows(c if mine else 1 - c, ins[t].shape[1])
        src = (ins if mine else outs)[t].at[2 * cx + cy, rows]
        return pltpu.make_async_remote_copy(src_ref=src, dst_ref=outs[t].at[2 * cx + cy, rows],
                                            send_sem=ss.at[3 * t + q], recv_sem=rs.at[3 * t + q],
                                            device_id=(x, y, 1 - c), device_id_type=MESH)

    def start(ins, outs, ss, rs):
        for t in range(nk):
            for q in range(3):
                copy(ins, outs, ss, rs, t, q, True).start()

    def wait(ins, outs, ss, rs):
        for t in range(nk):
            for q in range(3):
                copy(ins, outs, ss, rs, t, q, False).wait_recv()
                copy(ins, outs, ss, rs, t, q, True).wait_send()

    return _Exchange(bufs, [True] * nk, [], 3 * nk, start, wait, done)


def _reduce_d2d(gl, done):
    nk = len(gl)

    def copy(ins, outs, ss, rs, t):
        x, y, c, _, _ = _mesh_pos()
        return pltpu.make_async_remote_copy(
            src_ref=ins[t].at[:, _own_rows(1 - c, ins[t].shape[1])], dst_ref=outs[t],
            send_sem=ss.at[t], recv_sem=rs.at[t], device_id=(x, y, 1 - c), device_id_type=MESH)

    def start(ins, outs, ss, rs):
        for t in range(nk):
            copy(ins, outs, ss, rs, t).start()

    def wait(ins, outs, ss, rs):
        for t in range(nk):
            copy(ins, outs, ss, rs, t).wait()

    fresh = [((N_CHIPS, g.shape[1] // 2, g.shape[2]), g.dtype) for g in gl]
    return _Exchange(gl, [False] * nk, fresh, nk, start, wait, done)


def _reduce_ici(ps, done):
    nk = len(ps)

    def copy(ins, outs, ss, rs, t, q, landed):
        x, y, c, j, chips = _mesh_pos()
        cx, cy = chips[q]
        src = outs[t].at[2 * cx + cy] if landed else ins[t].at[2 * cx + cy]
        dst = outs[t].at[2 * cx + cy] if landed else outs[t].at[j]
        return pltpu.make_async_remote_copy(src_ref=src, dst_ref=dst, send_sem=ss.at[3 * t + q], recv_sem=rs.at[3 * t + q],
                                            device_id=(cx, cy, c), device_id_type=MESH)

    def start(ins, outs, ss, rs):
        for t in range(nk):
            for q in range(3):
                copy(ins, outs, ss, rs, t, q, False).start()

    def wait(ins, outs, ss, rs):
        for t in range(nk):
            for q in range(3):
                copy(ins, outs, ss, rs, t, q, True).wait_recv()
                copy(ins, outs, ss, rs, t, q, False).wait_send()

    return _Exchange(ps, [False] * nk, [(p_.shape, p_.dtype) for p_ in ps], 3 * nk, start, wait, done)


def _share_d2d(fs, done):
    nk = len(fs)

    def copy(ins, outs, ss, rs, t, mine):
        x, y, c, _, _ = _mesh_pos()
        rows = _own_rows(c if mine else 1 - c, ins[t].shape[1])
        src = (ins if mine else outs)[t].at[:, rows]
        return pltpu.make_async_remote_copy(src_ref=src, dst_ref=outs[t].at[:, rows], send_sem=ss.at[t], recv_sem=rs.at[t],
                                            device_id=(x, y, 1 - c), device_id_type=MESH)

    def start(ins, outs, ss, rs):
        for t in range(nk):
            copy(ins, outs, ss, rs, t, True).start()

    def wait(ins, outs, ss, rs):
        for t in range(nk):
            copy(ins, outs, ss, rs, t, False).wait_recv()
            copy(ins, outs, ss, rs, t, True).wait_send()

    return _Exchange(fs, [True] * nk, [], nk, start, wait, done)


def _run_exchange(plan, name):
    nin = len(plan.ins)
    out_shape = [_sds(x_.shape, x_.dtype) for x_, al in zip(plan.ins, plan.aliased) if al]
    aliases, k = {}, 0
    for t, al in enumerate(plan.aliased):
        if al:
            aliases[t] = k
            k += 1
    out_shape += [_sds(sh, dt) for sh, dt in plan.fresh]
    nout = len(out_shape)

    def body(*refs):
        ins, outs, sems = refs[:nin], refs[nin:nin + nout], refs[nin + nout:]
        plan.start(ins, outs, *sems)
        plan.wait(ins, outs, *sems)

    outs = pl.pallas_call(
        body, name=name, in_specs=[_ANY] * nin, out_specs=[_ANY] * nout, out_shape=out_shape,
        input_output_aliases=aliases,
        scratch_shapes=[pltpu.SemaphoreType.DMA((plan.n_sems,))] * 2,
    )(*plan.ins)
    plan.done(list(outs))


def _add_sibling_rows(g, r1, c, name):
    ns, R, C = g.shape
    hr = R // 2
    tr = hr if hr <= 512 else _tile_rows(hr, 512)
    nblk = hr // tr

    def body(s_ref, a_ref, b_ref, o_ref):
        o_ref[...] = (a_ref[...] + b_ref[...]).astype(o_ref.dtype)

    blk = (None, tr, C)
    return pl.pallas_call(
        body, name=name,
        grid_spec=pltpu.PrefetchScalarGridSpec(
            num_scalar_prefetch=1, grid=(ns, nblk),
            in_specs=[pl.BlockSpec(blk, lambda s_, i, s: (s_, s[0] * nblk + i, 0)), pl.BlockSpec(blk, lambda s_, i, s: (s_, i, 0))],
            out_specs=pl.BlockSpec(blk, lambda s_, i, s: (s_, i, 0))),
        out_shape=_sds((ns, hr, C), _WIRE),
        compiler_params=_cparams(3 * _nbytes((tr, C), F32)),
    )(jnp.reshape(c, (1,)).astype(jnp.int32), g, r1)


def _add_chip_rows(p_, r2, f, l, j, c, name):
    _, hr, C = p_.shape
    tr = hr if hr <= 512 else _tile_rows(hr, 512)
    nblk = hr // tr

    def body(s_ref, own, a1, a2, a3, f_ref, o_ref):
        v = lambda r: r[...].astype(F32)
        o_ref[...] = ((v(own) + v(a1)) + v(a2)) + v(a3)

    blk = (None, tr, C)
    return pl.pallas_call(
        body, name=name,
        grid_spec=pltpu.PrefetchScalarGridSpec(
            num_scalar_prefetch=1, grid=(nblk,),
            in_specs=[pl.BlockSpec(blk, lambda i, s: (s[0], i, 0))]
            + [pl.BlockSpec(blk, lambda i, s, k=k: ((s[0] + k) % N_CHIPS, i, 0)) for k in (1, 2, 3)]
            + [pl.BlockSpec(memory_space=pl.ANY)],
            out_specs=pl.BlockSpec(blk, lambda i, s: (l, s[1] * nblk + i, 0))),
        out_shape=_sds(f.shape, F32),
        input_output_aliases={5: 0},
        compiler_params=_cparams(6 * _nbytes((tr, C), F32)),
    )(jnp.stack([j, c]).astype(jnp.int32), p_, r2, r2, r2, f)


class _ShardedWeights:
    def __init__(self, a, j, c):
        self.j, self.c = j, c
        shards = dict(w_in=[a["w_in"]], w12=[a["glu_w1"], a["glu_w2"]], w_out=[a["w_out"]], w_ff1=[a["w_ff1"]],
                      w_ff2=[a["w_ff2"]], w_ple_gate=[a["w_ple_gate"]], w_ple_proj=[a["w_ple_proj"]])
        self.bufs = [[_cast_layer_slot(shards[n], l, j, "cast_%s_%d" % (n, l)) for n in BIG] for l in range(N_LAYERS)]
        _run_exchange(self._gather_part(0, self._HEAD), "gather_ici_0")
        _run_exchange(self._gather_part(0, self._HEAD, _gather_d2d), "gather_d2d_0")
        self.raw_ = {}
        self.pend_ = {}
        self.final = [lax.empty((N_LAYERS,) + b.shape[1:], F32) for b in self.bufs[0]]

    def _set_bufs(self, l, outs):
        self.bufs[l] = outs

    def layer(self, l):
        return {n: (b if n in COL_SHARDED else b.reshape(N_CHIPS * b.shape[1], b.shape[2]))
                for n, b in zip(BIG, self.bufs[l])}

    _FIRST, _SECOND = (0, 3), (1, 2, 4, 5, 6)

    _HEAD, _REST = (0, 1, 2), (3, 4, 5, 6)

    def _gather_part(self, nxt, idx, exchange=_gather_ici):
        def done(outs):
            for i, o in zip(idx, outs):
                self.bufs[nxt][i] = o
        return exchange([self.bufs[nxt][i] for i in idx], done)

    def fwd_hooks(self, l):
        hooks = {}
        if l == 0:
            hooks.update(x_ssm=lambda: self._gather_part(0, self._REST),
                         x_out=lambda: self._gather_part(0, self._REST, _gather_d2d))
        if l + 1 < N_LAYERS:
            nxt = l + 1
            hooks.update(x_ff1=lambda: self._gather_part(nxt, self._FIRST),
                         x_ff2=lambda: self._gather_part(nxt, self._SECOND),
                         x_gate=lambda: _gather_d2d(self.bufs[nxt], lambda o: self._set_bufs(nxt, o)))
        return hooks

    _EARLY, _LATE = (2, 3, 4, 5, 6), (0, 1)

    @staticmethod
    def _shards(idx, g):
        out = []
        for i in idx:
            n = BIG[i]
            out.append(g[n] if n in COL_SHARDED else g[n].reshape(N_CHIPS, g[n].shape[0] // N_CHIPS, g[n].shape[1]))
        return out

    def _sibling_plan(self, slot):
        lyr, idx, gl = self.raw_[slot]

        def done(got):
            ps = [_add_sibling_rows(g_, r1, self.c, "reduce_add_sibling_%s_%d" % (BIG[i], lyr))
                  for g_, r1, i in zip(gl, got, idx)]
            self.pend_[slot] = (lyr, idx, ps)
        return _reduce_d2d(gl, done)

    def _chips_plan(self, slot):
        lyr, idx, ps = self.pend_[slot]

        def done(r2):
            for i, p_, r in zip(idx, ps, r2):
                self.final[i] = _add_chip_rows(p_, r, self.final[i], lyr, self.j, self.c,
                                               "reduce_add_chips_%s_%d" % (BIG[i], lyr))
        return _reduce_ici(ps, done)

    def _early(self, l, g):
        self.raw_["early"] = (l, self._EARLY, self._shards(self._EARLY, g))

    def bwd_hooks(self, l):
        hooks = dict(early=lambda g: self._early(l, g),
                     x_e0=lambda: self._sibling_plan("early"), x_e1=lambda: self._chips_plan("early"))
        if "late" in self.raw_:
            hooks.update(x_bwd0=lambda: self._sibling_plan("late"), x_bwd=lambda: self._chips_plan("late"))
        return hooks

    def grads(self, l, g):
        self.raw_["late"] = (l, self._LATE, self._shards(self._LATE, g))

    def finish(self, small):
        _run_exchange(self._sibling_plan("late"), "reduce_d2d_last")
        small = _allreduce_small(small, self._chips_plan("late"))
        out = []
        _run_exchange(_share_d2d(self.final, out.extend), "share_d2d")
        return dict(zip(BIG, out)), small


def _rows_of(shape):
    return -(-int(np.prod(shape)) // (SUBLANES * LANES)) * SUBLANES


def _pack(d):
    parts = []
    for n in SMALL:
        flat = d[n].reshape(-1)
        parts.append(jnp.pad(flat, (0, _rows_of(flat.shape) * LANES - flat.shape[0])).reshape(-1, LANES))
    return jnp.concatenate(parts, axis=0)


def _unpack(buf, like):
    out, r0 = {}, 0
    for n in SMALL:
        shape = like[n].shape
        size, nr = int(np.prod(shape)), _rows_of(shape)
        piece = lax.optimization_barrier(buf[r0:r0 + nr])
        out[n] = piece.reshape(-1)[:size].reshape(shape)
        r0 += nr
    return out


ARGS = ("x", "p", "positions", "attn_norm_g", "w_in", "gmlp_ln_g", "gmlp_ln_b", "gmlp_ws", "gmlp_bs", "q_norm_g",
        "k_norm_g", "sinks", "ssm_a_re", "ssm_a_im", "ssm_log_dt", "ssm_b_re", "ssm_b_im", "ssm_c_re", "ssm_c_im",
        "ssm_d", "glu_w1", "glu_w2", "mix_out_g", "w_out", "mlp_norm_g", "w_ff1", "w_ff2", "ple_norm_g", "w_ple_gate",
        "w_ple_proj")
WEIGHTS = ARGS[3:]


def kernel(x, p, positions, attn_norm_g, w_in, gmlp_ln_g, gmlp_ln_b, gmlp_ws, gmlp_bs, q_norm_g, k_norm_g, sinks, ssm_a_re, ssm_a_im, ssm_log_dt, ssm_b_re, ssm_b_im, ssm_c_re, ssm_c_im, ssm_d, glu_w1, glu_w2, mix_out_g, w_out, mlp_norm_g, w_ff1, w_ff2, ple_norm_g, w_ple_gate, w_ple_proj, loss_target, m_attn_norm_g, m_w_in, m_gmlp_ln_g, m_gmlp_ln_b, m_gmlp_ws, m_gmlp_bs, m_q_norm_g, m_k_norm_g, m_sinks, m_ssm_a_re, m_ssm_a_im, m_ssm_log_dt, m_ssm_b_re, m_ssm_b_im, m_ssm_c_re, m_ssm_c_im, m_ssm_d, m_glu_w1, m_glu_w2, m_mix_out_g, m_w_out, m_mlp_norm_g, m_w_ff1, m_w_ff2, m_ple_norm_g, m_w_ple_gate, m_w_ple_proj, v_attn_norm_g, v_w_in, v_gmlp_ln_g, v_gmlp_ln_b, v_gmlp_ws, v_gmlp_bs, v_q_norm_g, v_k_norm_g, v_sinks, v_ssm_a_re, v_ssm_a_im, v_ssm_log_dt, v_ssm_b_re, v_ssm_b_im, v_ssm_c_re, v_ssm_c_im, v_ssm_d, v_glu_w1, v_glu_w2, v_mix_out_g, v_w_out, v_mlp_norm_g, v_w_ff1, v_w_ff2, v_ple_norm_g, v_w_ple_gate, v_w_ple_proj):
    a = dict(locals())
    L = a["x"].shape[1]
    nl = N_LAYERS
    c = lax.axis_index("c")
    j = 2 * lax.axis_index("x") + lax.axis_index("y")

    sw = {n: a[n] for n in SMALL}
    sse, gx, _, big_grads, small_sum = _local_step(
        a["x"].reshape(L, D_MODEL), a["p"].reshape(nl, L, PLE_DIM), a["positions"].reshape(L),
        a["loss_target"].reshape(L, D_MODEL), sw, _ShardedWeights(a, j, c))
    loss = lax.psum(sse * (0.5 / D_MODEL), ("x", "y", "c"))
    g12 = big_grads.pop("w12")
    big_grads["glu_w1"], big_grads["glu_w2"] = g12[:, :, :C_WIDTH], g12[:, :, C_WIDTH:]

    small_grads = _unpack(small_sum, sw)

    grads, delta, new_m, new_v = {}, {}, {}, {}
    d_s, m_s, v_s = _adamw(_pack(sw), _pack(small_grads), _pack({n: a["m_" + n] for n in SMALL}),
                           _pack({n: a["v_" + n] for n in SMALL}), "adamw_small")
    grads.update(small_grads)
    delta.update(_unpack(d_s, sw))
    new_m.update(_unpack(m_s, sw))
    new_v.update(_unpack(v_s, sw))
    for n, g in big_grads.items():
        shp = a[n].shape
        two_d = lambda t: t.reshape(shp[0] * shp[1], shp[2])
        direct = n not in ("glu_w1", "glu_w2")
        res = _adamw(two_d(a[n]), two_d(g), two_d(a["m_" + n]), two_d(a["v_" + n]), "adamw_" + n, emit_g=direct)
        d, m, v = res[:3]
        grads[n] = res[3].reshape(shp) if direct else g
        delta[n], new_m[n], new_v[n] = d.reshape(shp), m.reshape(shp), v.reshape(shp)

    return (loss, gx.reshape(1, L, D_MODEL), *[grads[n] for n in WEIGHTS], *[delta[n] for n in WEIGHTS],
            *[new_m[n] for n in WEIGHTS], *[new_v[n] for n in WEIGHTS])
```
